```python
import jax, jax.numpy as jnp
from jax import lax
import numpy as np

D_MODEL = 1024
BATCH = 16
SEQ = 256
DEPTH = 4
DEC_BATCH = 2
DEC_SEQ = 1024
PAST_LEN = 256

GRID_W = 64
N_MIXERS = 3
N_FOURIER_LAYERS = (DEPTH + 2) // 3
N_DELTA_LAYERS = (DEPTH + 1) // 3
N_MLSTM_LAYERS = DEPTH // 3
N_DIR = 2
EPS = 1e-6

FNET_GROUPS = 4
FNET_GROUP_DIM = D_MODEL // FNET_GROUPS

DN_HEADS = 8
DN_DK = 128
DN_DV = 128
DN_CONV = 5
DN_CHUNK = 64
DN_QKV = DN_HEADS * (2 * DN_DK + DN_DV)
DN_PROJ = DN_QKV + DN_HEADS * DN_DV + 2 * N_DIR * DN_HEADS

ML_HEADS = 8
ML_DQK = 64
ML_DV = 128
ML_CHUNK = 64
ML_HQ = ML_HEADS * ML_DQK
ML_HV = ML_HEADS * ML_DV
ML_PROJ = 2 * ML_HQ + 2 * ML_HV + 2 * N_DIR * ML_HEADS

D_FF = 2816

kernel_name = 'hybrid_fnet_gdn_mlstm_diffusion_step'


def rmsnorm(x, g):
    xf = x.astype(jnp.float32)
    y = xf * lax.rsqrt(jnp.mean(xf * xf, axis=-1, keepdims=True) + EPS)
    return (y * g.astype(jnp.float32)).astype(x.dtype)


def l2norm(x):
    return x * lax.rsqrt(jnp.sum(x * x, axis=-1, keepdims=True) + EPS)


def adaln(cond, w, b):
    m = (jax.nn.silu(cond) @ w + b)[:, None, :]
    return jnp.split(m, 6, axis=-1)


def dwconv_seq(x, w):
    K, T = w.shape[0], x.shape[1]
    p = K // 2
    xp = jnp.pad(x, ((0, 0), (p, p), (0, 0)))
    return sum(xp[:, j:j + T, :] * w[j] for j in range(K))


def dwconv_grid(x, w):
    R, W = x.shape[1], x.shape[2]
    xp = jnp.pad(x, ((0, 0), (1, 1), (1, 1), (0, 0)))
    return sum(xp[:, i:i + R, j:j + W, :] * w[i, j] for i in range(3) for j in range(3))


def to_chunks(a, chunk):
    B, T, H = a.shape[:3]
    a = a.reshape((B, T // chunk, chunk, H) + a.shape[3:])
    return jnp.moveaxis(a, (1, 3), (0, 2))


def from_chunks(o):
    o = jnp.moveaxis(o, (0, 2), (1, 3))
    return o.reshape((o.shape[0], o.shape[1] * o.shape[2]) + o.shape[3:])


def conv_ffn(h, w_up, conv_w, conv_b, w_down, rows):
    B, T, _ = h.shape
    a, g = jnp.split(h @ w_up, 2, axis=-1)
    g = dwconv_grid(g.reshape(B, rows, T // rows, D_FF), conv_w).reshape(B, T, D_FF) + conv_b
    return (jax.nn.silu(g) * a) @ w_down


def fourier_mix(h, w, b):
    B, T, _ = h.shape
    hg = h.astype(jnp.float32).reshape(B, T, FNET_GROUPS, FNET_GROUP_DIM)
    f = jnp.real(jnp.fft.fft2(hg, axes=(1, 3), norm='ortho'))
    return f.reshape(B, T, D_MODEL).astype(h.dtype) @ w + b


def gdn_chunked(q, k, v, g, beta, s0):
    C = DN_CHUNK
    qc, kc, vc = to_chunks(q, C), to_chunks(k, C), to_chunks(v, C)
    gc = jnp.cumsum(to_chunks(g, C), axis=-1)
    bc = to_chunks(beta, C)
    tril = jnp.tril(jnp.ones((C, C), bool))
    strict = jnp.tril(jnp.ones((C, C), bool), -1)
    decay = jnp.exp(jnp.where(tril, gc[..., :, None] - gc[..., None, :], -jnp.inf))
    kb = kc * bc[..., None]
    a_mat = jnp.where(strict, jnp.einsum('nbhcd,nbhsd->nbhcs', kb, kc) * decay, 0.0) + jnp.eye(C, dtype=jnp.float32)
    u = lax.linalg.triangular_solve(a_mat, vc * bc[..., None], left_side=True, lower=True, unit_diagonal=True)
    w = lax.linalg.triangular_solve(a_mat, kb * jnp.exp(gc)[..., None], left_side=True, lower=True, unit_diagonal=True)
    qk = jnp.einsum('nbhcd,nbhsd->nbhcs', qc, kc) * decay
    q_dec = qc * jnp.exp(gc)[..., None]
    g_last = gc[..., -1]
    k_dec = kc * jnp.exp(g_last[..., None] - gc)[..., None]

    def step(s, xs):
        u_n, w_n, qk_n, qd_n, kd_n, gl_n = xs
        v_new = u_n - jnp.einsum('bhcd,bhde->bhce', w_n, s)
        o_n = jnp.einsum('bhcd,bhde->bhce', qd_n, s) + jnp.einsum('bhcs,bhse->bhce', qk_n, v_new)
        s = s * jnp.exp(gl_n)[..., None, None] + jnp.einsum('bhcd,bhce->bhde', kd_n, v_new)
        return s, o_n

    s_fin, o = lax.scan(step, s0, (u, w, qk, q_dec, k_dec, g_last))
    return from_chunks(o), s_fin


def gated_delta_mix(h, w_in, conv_w, a_log, dt_bias, norm_g, w_out, s0):
    B, T, _ = h.shape
    f32 = jnp.float32
    qkv, z, gates = jnp.split(h @ w_in, [DN_QKV, DN_QKV + DN_HEADS * DN_DV], axis=-1)
    qkv = jax.nn.silu(dwconv_seq(qkv, conv_w)).astype(f32)
    q, k, v = jnp.split(qkv, [DN_HEADS * DN_DK, 2 * DN_HEADS * DN_DK], axis=-1)
    q = l2norm(q.reshape(B, T, DN_HEADS, DN_DK)) * (DN_DK ** -0.5)
    k = l2norm(k.reshape(B, T, DN_HEADS, DN_DK))
    v = v.reshape(B, T, DN_HEADS, DN_DV)
    gates = gates.astype(f32).reshape(B, T, N_DIR, 2, DN_HEADS)
    g = -jnp.exp(a_log.astype(f32)) * jax.nn.softplus(gates[..., 0, :] + dt_bias)
    beta = jax.nn.sigmoid(gates[..., 1, :])
    s0 = s0.astype(f32)
    o_f, s_f = gdn_chunked(q, k, v, g[:, :, 0], beta[:, :, 0], s0[:, 0])
    o_b, s_b = gdn_chunked(q[:, ::-1], k[:, ::-1], v[:, ::-1], g[:, ::-1, 1], beta[:, ::-1, 1], s0[:, 1])
    o = o_f + o_b[:, ::-1]
    o = rmsnorm(o, norm_g) * jax.nn.silu(z.astype(f32).reshape(B, T, DN_HEADS, DN_DV))
    y = o.reshape(B, T, DN_HEADS * DN_DV).astype(h.dtype) @ w_out
    return y, jnp.stack([s_f, s_b], axis=1)


def mlstm_chunked(q, k, v, li, lf, c0, n0, m0):
    C = ML_CHUNK
    qc, kc, vc = to_chunks(q, C), to_chunks(k, C), to_chunks(v, C)
    bcum = jnp.cumsum(to_chunks(lf, C), axis=-1)
    lic = to_chunks(li, C)
    tril = jnp.tril(jnp.ones((C, C), bool))
    d_log = jnp.where(tril, bcum[..., :, None] - bcum[..., None, :] + lic[..., None, :], -jnp.inf)
    d_max = jnp.max(d_log, axis=-1)
    qk = jnp.einsum('nbhcd,nbhsd->nbhcs', qc, kc)
    b_last = bcum[..., -1]
    w_log = b_last[..., None] - bcum + lic
    w_max = jnp.max(w_log, axis=-1)

    def step(carry, xs):
        c, n, m = carry
        qk_n, dl_n, dm_n, b_n, q_n, k_n, v_n, bl_n, wl_n, wm_n = xs
        m_t = jnp.maximum(b_n + m[..., None], dm_n)
        inter = jnp.exp(b_n + m[..., None] - m_t)
        s = qk_n * jnp.exp(dl_n - m_t[..., None])
        num = inter[..., None] * jnp.einsum('bhcd,bhde->bhce', q_n, c) + jnp.einsum('bhcs,bhse->bhce', s, v_n)
        den = inter * jnp.einsum('bhcd,bhd->bhc', q_n, n) + jnp.sum(s, axis=-1)
        h = num / jnp.maximum(jnp.abs(den), jnp.exp(-m_t))[..., None]
        m_new = jnp.maximum(bl_n + m, wm_n)
        dec = jnp.exp(bl_n + m - m_new)
        kw = k_n * jnp.exp(wl_n - m_new[..., None])[..., None]
        c = dec[..., None, None] * c + jnp.einsum('bhcd,bhce->bhde', kw, v_n)
        n = dec[..., None] * n + jnp.sum(kw, axis=-2)
        return (c, n, m_new), h

    (c_f, n_f, m_f), h = lax.scan(step, (c0, n0, m0), (qk, d_log, d_max, bcum, qc, kc, vc, b_last, w_log, w_max))
    return from_chunks(h), c_f, n_f, m_f


def mlstm_mix(h, w_in, b_i, b_f, norm_g, w_out, c0, n0, m0):
    B, T, _ = h.shape
    f32 = jnp.float32
    q, k, v, o, gates = jnp.split((h @ w_in).astype(f32), [ML_HQ, 2 * ML_HQ, 2 * ML_HQ + ML_HV, 2 * ML_HQ + 2 * ML_HV], axis=-1)
    q = q.reshape(B, T, ML_HEADS, ML_DQK) * (ML_DQK ** -0.5)
    k = k.reshape(B, T, ML_HEADS, ML_DQK)
    v = v.reshape(B, T, ML_HEADS, ML_DV)
    gates = gates.reshape(B, T, N_DIR, 2, ML_HEADS)
    li = gates[..., 0, :] + b_i
    lf = jax.nn.log_sigmoid(gates[..., 1, :] + b_f)
    c0, n0, m0 = c0.astype(f32), n0.astype(f32), m0.astype(f32)
    h_f, cf, nf, mf = mlstm_chunked(q, k, v, li[:, :, 0], lf[:, :, 0], c0[:, 0], n0[:, 0], m0[:, 0])
    h_b, cb, nb, mb = mlstm_chunked(q[:, ::-1], k[:, ::-1], v[:, ::-1], li[:, ::-1, 1], lf[:, ::-1, 1], c0[:, 1], n0[:, 1], m0[:, 1])
    hs = h_f + h_b[:, ::-1]
    hs = rmsnorm(hs, norm_g) * jax.nn.sigmoid(o.reshape(B, T, ML_HEADS, ML_DV))
    y = hs.reshape(B, T, ML_HV).astype(h.dtype) @ w_out
    return y, jnp.stack([cf, cb], axis=1), jnp.stack([nf, nb], axis=1), jnp.stack([mf, mb], axis=1)


def trunk(x, cond, rows, st_d, st_c, st_n, st_m, params):
    (w_ada, b_ada, norm_mix, norm_ffn, norm_final, ffn_w_up, ffn_conv_w, ffn_conv_b, ffn_w_down,
     fnet_w, fnet_b, dn_w_in, dn_conv_w, dn_a_log, dn_dt_bias, dn_norm, dn_w_out,
     ml_w_in, ml_b_i, ml_b_f, ml_norm, ml_w_out) = params
    out_d, out_c, out_n, out_m = [], [], [], []
    for layer in range(DEPTH):
        sh1, sc1, g1, sh2, sc2, g2 = adaln(cond, w_ada[layer], b_ada[layer])
        h = rmsnorm(x, norm_mix[layer]) * (1 + sc1) + sh1
        kind, j = layer % N_MIXERS, layer // N_MIXERS
        if kind == 0:
            y = fourier_mix(h, fnet_w[j], fnet_b[j])
        elif kind == 1:
            y, sd = gated_delta_mix(h, dn_w_in[j], dn_conv_w[j], dn_a_log[j], dn_dt_bias[j], dn_norm[j], dn_w_out[j], st_d[:, j])
            out_d.append(sd)
        else:
            y, mc, mn, mm = mlstm_mix(h, ml_w_in[j], ml_b_i[j], ml_b_f[j], ml_norm[j], ml_w_out[j], st_c[:, j], st_n[:, j], st_m[:, j])
            out_c.append(mc)
            out_n.append(mn)
            out_m.append(mm)
        x = x + g1 * y
        h = rmsnorm(x, norm_ffn[layer]) * (1 + sc2) + sh2
        x = x + g2 * conv_ffn(h, ffn_w_up[layer], ffn_conv_w[layer], ffn_conv_b[layer], ffn_w_down[layer], rows)
    return (rmsnorm(x, norm_final), jnp.stack(out_d, axis=1), jnp.stack(out_c, axis=1),
            jnp.stack(out_n, axis=1), jnp.stack(out_m, axis=1))


def setup_inputs(seed: int = 0) -> dict:
    key = jax.random.key(seed)
    ks = jax.random.split(key, 32)
    f32 = jnp.float32

    def nrm(k, shape, s):
        return jax.random.normal(k, shape, f32) * s

    D = D_MODEL
    dt = jnp.exp(jax.random.uniform(ks[22], (N_DELTA_LAYERS, N_DIR, DN_HEADS), f32, np.log(1e-3), np.log(1e-1)))
    return {
        'x_prompt': nrm(ks[0], (BATCH, SEQ, D), 1.0),
        'x_sample': nrm(ks[1], (DEC_BATCH, DEC_SEQ, D), 1.0),
        'state_delta': nrm(ks[2], (DEC_BATCH, N_DELTA_LAYERS, N_DIR, DN_HEADS, DN_DK, DN_DV), 0.1),
        'state_mlstm_c': nrm(ks[3], (DEC_BATCH, N_MLSTM_LAYERS, N_DIR, ML_HEADS, ML_DQK, ML_DV), 0.5),
        'state_mlstm_n': nrm(ks[4], (DEC_BATCH, N_MLSTM_LAYERS, N_DIR, ML_HEADS, ML_DQK), 0.5),
        'state_mlstm_m': nrm(ks[5], (DEC_BATCH, N_MLSTM_LAYERS, N_DIR, ML_HEADS), 1.0),
        'c': nrm(ks[6], (DEC_BATCH, D), 1.0),
        'c_ctx': nrm(ks[7], (D,), 1.0),
        'w_ada': nrm(ks[8], (DEPTH, D, 6 * D), D ** -0.5),
        'b_ada': nrm(ks[9], (DEPTH, 6 * D), 0.02),
        'norm_mix': 1.0 + nrm(ks[10], (DEPTH, D), 0.02),
        'norm_ffn': 1.0 + nrm(ks[11], (DEPTH, D), 0.02),
        'norm_final': 1.0 + nrm(ks[12], (D,), 0.02),
        'ffn_w_up': nrm(ks[13], (DEPTH, D, 2 * D_FF), D ** -0.5),
        'ffn_conv_w': nrm(ks[14], (DEPTH, 3, 3, D_FF), 1.0 / 3.0),
        'ffn_conv_b': nrm(ks[15], (DEPTH, D_FF), 0.02),
        'ffn_w_down': nrm(ks[16], (DEPTH, D_FF, D), D_FF ** -0.5),
        'fnet_w': nrm(ks[17], (N_FOURIER_LAYERS, D, D), D ** -0.5),
        'fnet_b': nrm(ks[18], (N_FOURIER_LAYERS, D), 0.02),
        'dn_w_in': nrm(ks[19], (N_DELTA_LAYERS, D, DN_PROJ), D ** -0.5),
        'dn_conv_w': nrm(ks[20], (N_DELTA_LAYERS, DN_CONV, DN_QKV), DN_CONV ** -0.5),
        'dn_a_log': jnp.log(jax.random.uniform(ks[21], (N_DELTA_LAYERS, N_DIR, DN_HEADS), f32, 1.0, 16.0)),
        'dn_dt_bias': jnp.log(jnp.expm1(dt)),
        'dn_norm': 1.0 + nrm(ks[23], (N_DELTA_LAYERS, DN_DV), 0.02),
        'dn_w_out': nrm(ks[24], (N_DELTA_LAYERS, DN_HEADS * DN_DV, D), (DN_HEADS * DN_DV) ** -0.5),
        'ml_w_in': nrm(ks[25], (N_MLSTM_LAYERS, D, ML_PROJ), D ** -0.5),
        'ml_b_i': nrm(ks[26], (N_MLSTM_LAYERS, N_DIR, ML_HEADS), 0.1),
        'ml_b_f': jax.random.uniform(ks[27], (N_MLSTM_LAYERS, N_DIR, ML_HEADS), f32, 3.0, 6.0),
        'ml_norm': 1.0 + nrm(ks[28], (N_MLSTM_LAYERS, ML_DV), 0.02),
        'ml_w_out': nrm(ks[29], (N_MLSTM_LAYERS, ML_HV, D), ML_HV ** -0.5),
    }


def reference(x_prompt, x_sample, state_delta, state_mlstm_c, state_mlstm_n, state_mlstm_m, c,
              c_ctx, w_ada, b_ada, norm_mix, norm_ffn, norm_final,
              ffn_w_up, ffn_conv_w, ffn_conv_b, ffn_w_down,
              fnet_w, fnet_b,
              dn_w_in, dn_conv_w, dn_a_log, dn_dt_bias, dn_norm, dn_w_out,
              ml_w_in, ml_b_i, ml_b_f, ml_norm, ml_w_out):
    params = (w_ada, b_ada, norm_mix, norm_ffn, norm_final, ffn_w_up, ffn_conv_w, ffn_conv_b, ffn_w_down,
              fnet_w, fnet_b, dn_w_in, dn_conv_w, dn_a_log, dn_dt_bias, dn_norm, dn_w_out,
              ml_w_in, ml_b_i, ml_b_f, ml_norm, ml_w_out)
    b = x_prompt.shape[0]
    f32 = jnp.float32
    zd = jnp.zeros((b, N_DELTA_LAYERS, N_DIR, DN_HEADS, DN_DK, DN_DV), f32)
    zc = jnp.zeros((b, N_MLSTM_LAYERS, N_DIR, ML_HEADS, ML_DQK, ML_DV), f32)
    zn = jnp.zeros((b, N_MLSTM_LAYERS, N_DIR, ML_HEADS, ML_DQK), f32)
    zm = jnp.zeros((b, N_MLSTM_LAYERS, N_DIR, ML_HEADS), f32)
    y_prompt, new_d, new_c, new_n, new_m = trunk(x_prompt, c_ctx[None, :], 1, zd, zc, zn, zm, params)
    rows = x_sample.shape[1] // GRID_W
    y_sample, _, _, _, _ = trunk(x_sample, c, rows, state_delta, state_mlstm_c, state_mlstm_n, state_mlstm_m, params)
    return (y_prompt, y_sample, new_d, new_c, new_n, new_m)
```

```python
import functools

import numpy as np
import jax
import jax.numpy as jnp
from jax import lax
from jax.experimental import pallas as pl
from jax.experimental.pallas import tpu as pltpu

F32 = jnp.float32
BF16 = jnp.bfloat16
HIGHEST = lax.Precision.HIGHEST

D_MODEL = 1024
EPS = 1e-6
N_DIR = 2
CHUNK = 64
FNET_GROUP_DIM = 256
DN_HEADS, DN_DK, DN_DV, DN_CONV = 8, 128, 128, 5
ML_HEADS, ML_DQK, ML_DV = 8, 64, 128
D_FF = 2816
GRID_W = 64

FFN_TOKEN_TILE = 1024
FFN_FF_TILE = 256
PROJ_TOKEN_TILE = 1024
PROJ_COL_TILE = 512
OUT_TOKEN_TILE = 512
ADA_COL_TILE = 1536
VMEM_LIMIT = 56 * 1024 * 1024


def _params(*sem):
    return pltpu.CompilerParams(dimension_semantics=sem, vmem_limit_bytes=VMEM_LIMIT)


def _dot(a, b):
    return jnp.dot(a.astype(BF16), b.astype(BF16), preferred_element_type=F32)


def _dot_nt(a, b):
    return lax.dot_general(a.astype(BF16), b.astype(BF16), (((1,), (1,)), ((), ())),
                           preferred_element_type=F32)


def _dot_tn(a, b):
    return lax.dot_general(a.astype(BF16), b.astype(BF16), (((0,), (0,)), ((), ())),
                           preferred_element_type=F32)


def _dot_f32(a, b):
    return jnp.dot(a, b, precision=HIGHEST, preferred_element_type=F32)


def _rms(x, g):
    return x * lax.rsqrt(jnp.mean(x * x, axis=-1, keepdims=True) + EPS) * g


def _norm_mod(x, g, sc, sh):
    return _rms(x, g) * (1.0 + sc) + sh


def _softplus(x):
    return jnp.maximum(x, 0.0) + jnp.log1p(jnp.exp(-jnp.abs(x)))


def _mod_spec(chunk, cond_of_tile):
    return pl.BlockSpec((1, 1, D_MODEL), lambda i, *_: (cond_of_tile(i), 0, chunk))


def _adaln_kernel(c_ref, w_ref, b_ref, o_ref):
    s = jax.nn.silu(c_ref[...])
    o_ref[0] = _dot(s, w_ref[0]) + b_ref[0]


def _adaln_all(cond8, w_ada, b_ada):
    depth = w_ada.shape[0]
    n_out = w_ada.shape[2]
    tn = ADA_COL_TILE
    return pl.pallas_call(
        _adaln_kernel,
        grid=(depth, n_out // tn),
        in_specs=[pl.BlockSpec((8, D_MODEL), lambda l, j: (0, 0)),
                  pl.BlockSpec((1, D_MODEL, tn), lambda l, j: (l, 0, j)),
                  pl.BlockSpec((1, 1, tn), lambda l, j: (l, 0, j))],
        out_specs=pl.BlockSpec((1, 8, tn), lambda l, j: (l, 0, j)),
        out_shape=jax.ShapeDtypeStruct((depth, 8, n_out), F32),
        compiler_params=_params("parallel", "parallel"),
        name="adaln",
    )(cond8, w_ada, b_ada.reshape(depth, 1, n_out))


def _norm_proj_kernel(x_ref, g_ref, sc_ref, sh_ref, w_ref, wgt_ref, o_ref, gt_ref, h_scr):
    @pl.when(pl.program_id(1) == 0)
    def _():
        h = _norm_mod(x_ref[...], g_ref[...], sc_ref[0], sh_ref[0]).astype(BF16)
        h_scr[...] = h
        gt_ref[...] = _dot_nt(wgt_ref[...], h)

    o_ref[...] = _dot(h_scr[...], w_ref[...])


def _norm_proj(x, norm_g, mod, cond_of_tile, w_in, n_main):
    n_tok = x.shape[0]
    tm, tn = PROJ_TOKEN_TILE, PROJ_COL_TILE
    n_gate = w_in.shape[1] - n_main
    wgt = w_in[:, n_main:].T
    return pl.pallas_call(
        _norm_proj_kernel,
        grid=(n_tok // tm, n_main // tn),
        in_specs=[pl.BlockSpec((tm, D_MODEL), lambda i, j: (i, 0)),
                  pl.BlockSpec((1, D_MODEL), lambda i, j: (0, 0)),
                  _mod_spec(1, cond_of_tile), _mod_spec(0, cond_of_tile),
                  pl.BlockSpec((D_MODEL, tn), lambda i, j: (0, j)),
                  pl.BlockSpec((n_gate, D_MODEL), lambda i, j: (0, 0))],
        out_specs=[pl.BlockSpec((tm, tn), lambda i, j: (i, j)),
                   pl.BlockSpec((n_gate, tm), lambda i, j: (0, i))],
        out_shape=[jax.ShapeDtypeStruct((n_tok, n_main), F32),
                   jax.ShapeDtypeStruct((n_gate, n_tok), F32)],
        scratch_shapes=[pltpu.VMEM((tm, D_MODEL), BF16)],
        compiler_params=_params("parallel", "arbitrary"),
        name="norm_proj",
    )(x, norm_g.reshape(1, D_MODEL), mod, mod, w_in, wgt)


def _gated_out_kernel(x_ref, o_ref, z_ref, ng_ref, g1_ref, w_ref, y_ref, wb_scr, *, n_heads, gate_fn):
    @pl.when(pl.program_id(0) == 0)
    def _():
        wb_scr[...] = w_ref[...].astype(BF16)

    dv = o_ref.shape[1] // n_heads
    parts = []
    for h in range(n_heads):
        sl = slice(h * dv, (h + 1) * dv)
        parts.append((_rms(o_ref[:, sl], ng_ref[...]) * gate_fn(z_ref[:, sl])).astype(BF16))
    hs = jnp.concatenate(parts, axis=1)
    y = jnp.dot(hs, wb_scr[...], preferred_element_type=F32)
    y_ref[...] = x_ref[...] + g1_ref[0] * y


def _gated_out(x, o, main, z_block, norm_g, mod, cond_of_tile, w_out, n_heads, gate_fn):
    n_tok = x.shape[0]
    tm = OUT_TOKEN_TILE
    width = o.shape[1]
    return pl.pallas_call(
        functools.partial(_gated_out_kernel, n_heads=n_heads, gate_fn=gate_fn),
        grid=(n_tok // tm,),
        in_specs=[pl.BlockSpec((tm, D_MODEL), lambda i: (i, 0)),
                  pl.BlockSpec((tm, width), lambda i: (i, 0)),
                  pl.BlockSpec((tm, width), lambda i: (i, z_block)),
                  pl.BlockSpec((1, width // n_heads), lambda i: (0, 0)),
                  _mod_spec(2, cond_of_tile),
                  pl.BlockSpec((width, D_MODEL), lambda i: (0, 0))],
        out_specs=pl.BlockSpec((tm, D_MODEL), lambda i: (i, 0)),
        out_shape=jax.ShapeDtypeStruct((n_tok, D_MODEL), F32),
        scratch_shapes=[pltpu.VMEM((width, D_MODEL), BF16)],
        compiler_params=_params("arbitrary"),
        name="gated_out",
    )(x, o, main, norm_g.reshape(1, -1), mod, w_out)


def _dft_mats(n):
    k = np.arange(n, dtype=np.int64)
    ang = 2.0 * np.pi * ((k[:, None] * k[None, :]) % n).astype(np.float64) / n
    s = 1.0 / np.sqrt(n)
    return np.cos(ang) * s, np.sin(ang) * s


def _fnet_kernel(x_ref, ng_ref, sc_ref, sh_ref, g1_ref, ct_ref, st_ref, cs_ref, w_ref, b_ref, y_ref, wb_scr):
    @pl.when(pl.program_id(0) == 0)
    def _():
        wb_scr[...] = w_ref[...].astype(BF16)

    x = x_ref[...]
    h = _norm_mod(x, ng_ref[...], sc_ref[0], sh_ref[0]).astype(BF16)
    p = jnp.dot(ct_ref[...], h, preferred_element_type=F32)
    q = jnp.dot(st_ref[...], h, preferred_element_type=F32)
    gd = FNET_GROUP_DIM
    parts = []
    for g in range(D_MODEL // gd):
        sl = slice(g * gd, (g + 1) * gd)
        pq = jnp.concatenate([p[:, sl], q[:, sl]], axis=1).astype(BF16)
        parts.append(jnp.dot(pq, cs_ref[...], preferred_element_type=F32).astype(BF16))
    f = jnp.concatenate(parts, axis=1)
    y = jnp.dot(f, wb_scr[...], preferred_element_type=F32) + b_ref[...]
    y_ref[...] = x + g1_ref[0] * y


def _fnet_mix(x, seq, norm_g, mod, cond_of_seq, w, b):
    n_tok = x.shape[0]
    ct, st = _dft_mats(seq)
    cc, sc = _dft_mats(FNET_GROUP_DIM)
    cs = np.concatenate([cc, -sc], axis=0)
    const = lambda i: (0, 0)
    ct, st, cs = (jnp.asarray(m, F32).astype(BF16) for m in (ct, st, cs))
    return pl.pallas_call(
        _fnet_kernel,
        grid=(n_tok // seq,),
        in_specs=[pl.BlockSpec((seq, D_MODEL), lambda i: (i, 0)),
                  pl.BlockSpec((1, D_MODEL), const),
                  _mod_spec(1, cond_of_seq), _mod_spec(0, cond_of_seq), _mod_spec(2, cond_of_seq),
                  pl.BlockSpec((seq, seq), const),
                  pl.BlockSpec((seq, seq), const),
                  pl.BlockSpec((2 * FNET_GROUP_DIM, FNET_GROUP_DIM), const),
                  pl.BlockSpec((D_MODEL, D_MODEL), const),
                  pl.BlockSpec((1, D_MODEL), const)],
        out_specs=pl.BlockSpec((seq, D_MODEL), lambda i: (i, 0)),
        out_shape=jax.ShapeDtypeStruct((n_tok, D_MODEL), F32),
        scratch_shapes=[pltpu.VMEM((D_MODEL, D_MODEL), BF16)],
        compiler_params=_params("arbitrary"),
        name="fnet",
    )(x, norm_g.reshape(1, D_MODEL), mod, mod, mod,
      ct, st, cs, w, b.reshape(1, D_MODEL))


def _shift_rows(x, delta):
    n = x.shape[0]
    return pltpu.roll(x, (-delta) % n, 0)


def _dwconv_tokens(g, cw_ref, seq, rows):
    tm = g.shape[0]
    width = seq // rows
    assert seq & (seq - 1) == 0 and width & (width - 1) == 0
    t = lax.broadcasted_iota(jnp.int32, (tm, 1), 0)
    pos = t & (seq - 1)
    col = pos & (width - 1)
    row = pos >> (width.bit_length() - 1)
    g_cols = (_shift_rows(g, -1) * (col >= 1).astype(F32), g,
              _shift_rows(g, 1) * (col <= width - 2).astype(F32))
    out = None
    for di in (-1, 0, 1):
        if rows == 1 and di != 0:
            continue
        r = sum(g_cols[dj] * cw_ref[pl.ds(3 * (di + 1) + dj, 1), :] for dj in range(3))
        if di != 0:
            ok = (row + di >= 0) & (row + di <= rows - 1)
            r = _shift_rows(r, di * width) * ok.astype(F32)
        out = r if out is None else out + r
    return out


def _ffn_kernel(x_ref, ng_ref, sc_ref, sh_ref, g2_ref, wa_ref, wg_ref, cw_ref, cb_ref, wd_ref, nf_ref,
                y_ref, h_scr, acc_scr, *, seq, rows, final_norm):
    j = pl.program_id(1)

    @pl.when(j == 0)
    def _():
        h_scr[...] = _norm_mod(x_ref[...], ng_ref[...], sc_ref[0], sh_ref[0]).astype(BF16)
        acc_scr[...] = jnp.zeros_like(acc_scr)

    h = h_scr[...]
    a = jnp.dot(h, wa_ref[...].astype(BF16), preferred_element_type=F32)
    g = jnp.dot(h, wg_ref[...].astype(BF16), preferred_element_type=F32)
    g = _dwconv_tokens(g, cw_ref, seq, rows) + cb_ref[...]
    act = (jax.nn.silu(g) * a).astype(BF16)
    acc_scr[...] += jnp.dot(act, wd_ref[...].astype(BF16), preferred_element_type=F32)

    @pl.when(j == pl.num_programs(1) - 1)
    def _():
        y = x_ref[...] + g2_ref[0] * acc_scr[...]
        if final_norm:
            y = _rms(y, nf_ref[...])
        y_ref[...] = y


def _conv_ffn(x, seq, rows, norm_g, mod, cond_of_tile, w_up, conv_w, conv_b, w_down, norm_final, final_norm):
    n_tok = x.shape[0]
    tm, tf = FFN_TOKEN_TILE, FFN_FF_TILE
    n_ff_tiles = D_FF // tf
    return pl.pallas_call(
        functools.partial(_ffn_kernel, seq=seq, rows=rows, final_norm=final_norm),
        grid=(n_tok // tm, n_ff_tiles),
        in_specs=[pl.BlockSpec((tm, D_MODEL), lambda i, j: (i, 0)),
                  pl.BlockSpec((1, D_MODEL), lambda i, j: (0, 0)),
                  _mod_spec(4, cond_of_tile), _mod_spec(3, cond_of_tile), _mod_spec(5, cond_of_tile),
                  pl.BlockSpec((D_MODEL, tf), lambda i, j: (0, j)),
                  pl.BlockSpec((D_MODEL, tf), lambda i, j: (0, n_ff_tiles + j)),
                  pl.BlockSpec((9, tf), lambda i, j: (0, j)),
                  pl.BlockSpec((1, tf), lambda i, j: (0, j)),
                  pl.BlockSpec((tf, D_MODEL), lambda i, j: (j, 0)),
                  pl.BlockSpec((1, D_MODEL), lambda i, j: (0, 0))],
        out_specs=pl.BlockSpec((tm, D_MODEL), lambda i, j: (i, 0)),
        out_shape=jax.ShapeDtypeStruct((n_tok, D_MODEL), F32),
        scratch_shapes=[pltpu.VMEM((tm, D_MODEL), BF16), pltpu.VMEM((tm, D_MODEL), F32)],
        compiler_params=_params("parallel", "arbitrary"),
        name="conv_ffn",
    )(x, norm_g.reshape(1, D_MODEL), mod, mod, mod, w_up, w_up, conv_w.reshape(9, D_FF),
      conv_b.reshape(1, D_FF), w_down, norm_final.reshape(1, D_MODEL))


def _tri_masks():
    r = lax.broadcasted_iota(jnp.int32, (CHUNK, CHUNK), 0)
    c = lax.broadcasted_iota(jnp.int32, (CHUNK, CHUNK), 1)
    return ((r >= c, r > c), (r <= c, r < c))


def _gate_layouts(gt, batch, n_heads):
    n_chunks = gt.shape[1] // (batch * CHUNK)
    g = gt.reshape(N_DIR, 2, n_heads, batch, n_chunks, CHUNK)
    rows = jnp.transpose(g, (3, 2, 0, 1, 4, 5))
    cols = jnp.swapaxes(rows, -1, -2)
    pad = max(8 - n_chunks, 0)
    rows = jnp.pad(rows, ((0, 0),) * 4 + ((0, pad), (0, 0)))
    return rows, cols


def _cumsum_forms(row, col, d, masks):
    incl = masks[d][0].astype(F32)
    incl_t = masks[1 - d][0].astype(F32)
    return _dot_f32(row, incl_t), _dot_f32(incl, col)


def _bcast_cols(dst_ref, d, col, n_chunks):
    for n in range(n_chunks):
        dst_ref[d, n] = jnp.broadcast_to(col[:, n:n + 1], (CHUNK, 128))


def _conv_silu(x, w_ref):
    n = x.shape[0]
    k = w_ref.shape[0]
    t = lax.broadcasted_iota(jnp.int32, (n, 1), 0)
    acc = None
    for j in range(k):
        delta = j - k // 2
        if delta == 0:
            term = x
        else:
            ok = (t + delta >= 0) & (t + delta <= n - 1)
            term = _shift_rows(x, delta) * ok.astype(F32)
        term = term * w_ref[pl.ds(j, 1), :]
        acc = term if acc is None else acc + term
    return jax.nn.silu(acc)


def _l2norm(x):
    return x * lax.rsqrt(jnp.sum(x * x, axis=-1, keepdims=True) + EPS)


def _unit_tri_inverse(a):
    n = a.shape[0]
    r = lax.broadcasted_iota(jnp.int32, (n, n), 0)
    c = lax.broadcasted_iota(jnp.int32, (n, n), 1)
    x = -a
    inv = jnp.where(r == c, 1.0, 0.0) + x
    p = x
    steps = int(np.log2(n)) - 1
    for _ in range(steps):
        p = _dot_f32(p, p)
        inv = inv + _dot_f32(inv, p)
    return inv


def _gdn_kernel(*refs, seq, zero_init, emit_state):
    alog_ref, dtb_ref, q_ref, k_ref, v_ref, cwq_ref, cwk_ref, cwv_ref, grow_ref, gcol_ref = refs[:10]
    pos = 10
    s0_ref = None
    if not zero_init:
        s0_ref = refs[pos]
        pos += 1
    o_ref = refs[pos]
    pos += 1
    sfin_ref = None
    if emit_state:
        sfin_ref = refs[pos]
        pos += 1
    q_scr, k_scr, v_scr, gcr_scr, gcb_scr, bb_scr = refs[pos:]

    n_chunks = seq // CHUNK
    head = pl.program_id(1)
    masks = _tri_masks()

    q_scr[...] = _l2norm(_conv_silu(q_ref[...], cwq_ref)) * (DN_DK ** -0.5)
    k_scr[...] = _l2norm(_conv_silu(k_ref[...], cwk_ref))
    v_scr[...] = _conv_silu(v_ref[...], cwv_ref)

    for d in range(N_DIR):
        neg_a = -jnp.exp(jnp.full((1, 1), alog_ref[d, head], F32))
        dtb = dtb_ref[d, head]
        g_row = neg_a * _softplus(grow_ref[0, 0, d, 0] + dtb)
        g_col = neg_a * _softplus(gcol_ref[0, 0, d, 0] + dtb)
        beta_col = jax.nn.sigmoid(gcol_ref[0, 0, d, 1])
        gc_row, gc_col = _cumsum_forms(g_row, g_col, d, masks)
        gcr_scr[d] = gc_row
        _bcast_cols(gcb_scr, d, gc_col, n_chunks)
        _bcast_cols(bb_scr, d, beta_col, n_chunks)

    o_ref[...] = jnp.zeros_like(o_ref)

    def chunk_step(n, carry):
        new = []
        for d in range(N_DIR):
            idx = n if d == 0 else n_chunks - 1 - n
            rows = pl.ds(pl.multiple_of(idx * CHUNK, CHUNK), CHUNK)
            qc, kc, vc = q_scr[rows, :], k_scr[rows, :], v_scr[rows, :]
            gcr = gcr_scr[d, pl.ds(idx, 1), :]
            gcb = gcb_scr[d, idx]
            bb = bb_scr[d, idx]
            incl, strict = masks[d]
            decay = jnp.exp(jnp.where(incl, gcb[:, :CHUNK] - gcr, -jnp.inf))
            kq = _dot_nt(jnp.concatenate([kc, qc], axis=0), kc)
            a = jnp.where(strict, kq[:CHUNK] * bb[:, :CHUNK] * decay, 0.0)
            t_inv = _unit_tri_inverse(a)
            e_gc = jnp.exp(gcb)
            uw = _dot_f32(t_inv, jnp.concatenate([vc * bb, kc * bb * e_gc], axis=1))
            u, w = uw[:, :DN_DV], uw[:, DN_DV:]
            g_last = gcr[:, CHUNK - 1:CHUNK] if d == 0 else gcr[:, 0:1]
            s = carry[d]
            wq = _dot(jnp.concatenate([w, qc * e_gc], axis=0), s)
            v_new = u - wq[:CHUNK]
            o_n = wq[CHUNK:] + _dot(kq[CHUNK:] * decay, v_new)
            k_dec = kc * jnp.exp(g_last - gcb)
            new.append(s * jnp.exp(g_last) + _dot_tn(k_dec, v_new))
            o_ref[rows, :] += o_n
        return tuple(new)

    if zero_init:
        init = (jnp.zeros((DN_DK, DN_DV), F32),) * N_DIR
    else:
        init = tuple(s0_ref[0, d, 0] for d in range(N_DIR))
    fin = lax.fori_loop(0, n_chunks, chunk_step, init)
    if emit_state:
        for d in range(N_DIR):
            sfin_ref[0, d, 0] = fin[d]


def _gdn_scan(main, gt, batch, seq, conv_w, a_log, dt_bias, s0, emit_state):
    n_tok = batch * seq
    n_chunks = seq // CHUNK
    nh = DN_HEADS
    grow, gcol = _gate_layouts(gt, batch, nh)
    zero_init = s0 is None
    smem = pl.BlockSpec(memory_space=pltpu.SMEM)
    qkv_spec = lambda off: pl.BlockSpec((seq, DN_DK), lambda b, h: (b, off + h))
    cw_spec = lambda off: pl.BlockSpec((DN_CONV, DN_DK), lambda b, h: (0, off + h))
    state_spec = pl.BlockSpec((1, N_DIR, 1, DN_DK, DN_DV), lambda b, h: (b, 0, h, 0, 0))
    in_specs = [smem, smem, qkv_spec(0), qkv_spec(nh), qkv_spec(2 * nh), cw_spec(0), cw_spec(nh), cw_spec(2 * nh),
                pl.BlockSpec((1, 1) + grow.shape[2:], lambda b, h: (b, h, 0, 0, 0, 0)),
                pl.BlockSpec((1, 1) + gcol.shape[2:], lambda b, h: (b, h, 0, 0, 0, 0))]
    args = [a_log, dt_bias, main, main, main, conv_w, conv_w, conv_w, grow, gcol]
    if not zero_init:
        in_specs.append(state_spec)
        args.append(s0)
    out_specs = [pl.BlockSpec((seq, DN_DV), lambda b, h: (b, h))]
    out_shape = [jax.ShapeDtypeStruct((n_tok, nh * DN_DV), F32)]
    if emit_state:
        out_specs.append(state_spec)
        out_shape.append(jax.ShapeDtypeStruct((batch, N_DIR, nh, DN_DK, DN_DV), F32))
    outs = pl.pallas_call(
        functools.partial(_gdn_kernel, seq=seq, zero_init=zero_init, emit_state=emit_state),
        grid=(batch, nh),
        in_specs=in_specs, out_specs=out_specs, out_shape=out_shape,
        scratch_shapes=[pltpu.VMEM((seq, DN_DK), F32), pltpu.VMEM((seq, DN_DK), F32), pltpu.VMEM((seq, DN_DV), F32),
                        pltpu.VMEM((N_DIR, grow.shape[4], CHUNK), F32),
                        pltpu.VMEM((N_DIR, n_chunks, CHUNK, 128), F32),
                        pltpu.VMEM((N_DIR, n_chunks, CHUNK, 128), F32)],
        compiler_params=_params("parallel", "parallel"),
        name="gdn_scan",
    )(*args)
    return outs[0], (outs[1] if emit_state else None)


ML_AUG = 2 * ML_DV


def _mlstm_kernel(*refs, seq, zero_init, emit_state):
    bi_ref, bf_ref, q_ref, k_ref, v_ref, grow_ref, gcol_ref = refs[:7]
    pos = 7
    c0_ref = m0_ref = None
    if not zero_init:
        c0_ref, m0_ref = refs[pos:pos + 2]
        pos += 2
    o_ref = refs[pos]
    pos += 1
    cfin_ref = mfin_ref = None
    if emit_state:
        cfin_ref, mfin_ref = refs[pos:pos + 2]
        pos += 2
    bcr_scr, lir_scr, bcb_scr, lib_scr = refs[pos:]

    n_chunks = seq // CHUNK
    pair = pl.program_id(1)
    masks = _tri_masks()
    ones_col = jnp.where(lax.broadcasted_iota(jnp.int32, (CHUNK, ML_DV), 1) == 0, 1.0, 0.0)

    for hh in range(2):
        for d in range(N_DIR):
            b_i = bi_ref[d, 2 * pair + hh]
            b_f = bf_ref[d, 2 * pair + hh]
            li_row = grow_ref[0, hh, d, 0] + b_i
            li_col = gcol_ref[0, hh, d, 0] + b_i
            lf_row = -_softplus(-(grow_ref[0, hh, d, 1] + b_f))
            lf_col = -_softplus(-(gcol_ref[0, hh, d, 1] + b_f))
            bc_row, bc_col = _cumsum_forms(lf_row, lf_col, d, masks)
            bcr_scr[hh, d] = bc_row
            lir_scr[hh, d] = li_row
            _bcast_cols(bcb_scr.at[hh], d, bc_col, n_chunks)
            _bcast_cols(lib_scr.at[hh], d, li_col, n_chunks)

    o_ref[...] = jnp.zeros_like(o_ref)

    def chunk_step(n, carry):
        new = []
        for hh in range(2):
            for d in range(N_DIR):
                idx = n if d == 0 else n_chunks - 1 - n
                rows = pl.ds(pl.multiple_of(idx * CHUNK, CHUNK), CHUNK)
                qc = q_ref[rows, hh * ML_DQK:(hh + 1) * ML_DQK] * (ML_DQK ** -0.5)
                kc = k_ref[rows, hh * ML_DQK:(hh + 1) * ML_DQK]
                v_aug = jnp.concatenate([v_ref[rows, hh * ML_DV:(hh + 1) * ML_DV], ones_col], axis=1)
                b_row = bcr_scr[hh, d, pl.ds(idx, 1), :]
                li_row = lir_scr[hh, d, pl.ds(idx, 1), :]
                b_colb = bcb_scr[hh, d, idx]
                li_colb = lib_scr[hh, d, idx]
                b_col = b_colb[:, 0:1]
                incl = masks[d][0]
                c_aug, m = carry[2 * hh + d]
                b_last = b_row[:, CHUNK - 1:CHUNK] if d == 0 else b_row[:, 0:1]
                d_log = jnp.where(incl, b_colb[:, :CHUNK] - b_row + li_row, -jnp.inf)
                d_max = jnp.max(d_log, axis=-1, keepdims=True)
                m_t = jnp.maximum(b_col + m, d_max)
                inter = jnp.exp(b_col + m - m_t)
                s = _dot_nt(qc, kc) * jnp.exp(d_log - m_t)
                num = inter * _dot(qc, c_aug) + _dot(s, v_aug)
                den = num[:, ML_DV:ML_DV + 1]
                h = num[:, :ML_DV] / jnp.maximum(jnp.abs(den), jnp.exp(-m_t))
                w_row = b_last - b_row + li_row
                m_new = jnp.maximum(b_last + m, jnp.max(w_row, axis=-1, keepdims=True))
                dec = jnp.exp(b_last + m - m_new)
                kw = kc * jnp.exp(b_last - b_colb[:, :ML_DQK] + li_colb[:, :ML_DQK] - m_new)
                new.append((dec * c_aug + _dot_tn(kw, v_aug), m_new))
                o_ref[rows, hh * ML_DV:(hh + 1) * ML_DV] += h
        return tuple(new)

    init = []
    for hh in range(2):
        for d in range(N_DIR):
            if zero_init:
                init.append((jnp.zeros((ML_DQK, ML_AUG), F32), jnp.zeros((1, 1), F32)))
            else:
                init.append((c0_ref[0, d, hh], m0_ref[0, d, hh][:, 0:1]))
    fin = lax.fori_loop(0, n_chunks, chunk_step, tuple(init))
    if emit_state:
        for hh in range(2):
            for d in range(N_DIR):
                c_aug, m = fin[2 * hh + d]
                cfin_ref[0, d, hh] = c_aug
                mfin_ref[0, d, hh] = jnp.broadcast_to(m, (1, 128))


def _mlstm_scan(main, gt, batch, seq, b_i, b_f, state0, emit_state):
    n_tok = batch * seq
    n_chunks = seq // CHUNK
    nh = ML_HEADS
    n_pairs = nh // 2
    grow, gcol = _gate_layouts(gt, batch, nh)
    zero_init = state0 is None
    smem = pl.BlockSpec(memory_space=pltpu.SMEM)
    q_off = 0
    k_off = (nh * ML_DQK) // (2 * ML_DQK)
    v_off = (2 * nh * ML_DQK) // (2 * ML_DV)
    c_spec = pl.BlockSpec((1, N_DIR, 2, ML_DQK, ML_AUG), lambda b, p: (b, 0, p, 0, 0))
    m_spec = pl.BlockSpec((1, N_DIR, 2, 1, 128), lambda b, p: (b, 0, p, 0, 0))
    in_specs = [smem, smem,
                pl.BlockSpec((seq, 2 * ML_DQK), lambda b, p: (b, q_off + p)),
                pl.BlockSpec((seq, 2 * ML_DQK), lambda b, p: (b, k_off + p)),
                pl.BlockSpec((seq, 2 * ML_DV), lambda b, p: (b, v_off + p)),
                pl.BlockSpec((1, 2) + grow.shape[2:], lambda b, p: (b, p, 0, 0, 0, 0)),
                pl.BlockSpec((1, 2) + gcol.shape[2:], lambda b, p: (b, p, 0, 0, 0, 0))]
    args = [b_i, b_f, main, main, main, grow, gcol]
    if not zero_init:
        in_specs += [c_spec, m_spec]
        args += list(state0)
    out_specs = [pl.BlockSpec((seq, 2 * ML_DV), lambda b, p: (b, p))]
    out_shape = [jax.ShapeDtypeStruct((n_tok, nh * ML_DV), F32)]
    if emit_state:
        out_specs += [c_spec, m_spec]
        out_shape += [jax.ShapeDtypeStruct((batch, N_DIR, nh, ML_DQK, ML_AUG), F32),
                      jax.ShapeDtypeStruct((batch, N_DIR, nh, 1, 128), F32)]
    outs = pl.pallas_call(
        functools.partial(_mlstm_kernel, seq=seq, zero_init=zero_init, emit_state=emit_state),
        grid=(batch, n_pairs),
        in_specs=in_specs, out_specs=out_specs, out_shape=out_shape,
        scratch_shapes=[pltpu.VMEM((2, N_DIR, grow.shape[4], CHUNK), F32),
                        pltpu.VMEM((2, N_DIR, grow.shape[4], CHUNK), F32),
                        pltpu.VMEM((2, N_DIR, n_chunks, CHUNK, 128), F32),
                        pltpu.VMEM((2, N_DIR, n_chunks, CHUNK, 128), F32)],
        compiler_params=_params("parallel", "parallel"),
        name="mlstm_scan",
    )(*args)
    return outs[0], ((outs[1], outs[2]) if emit_state else None)


def _trunk(x, batch, seq, rows, mods, tokens_per_cond, st_d, st_ml, emit_state, p):
    n_tok = batch * seq
    depth = p["w_ada"].shape[0]
    tile_cond = lambda tile: (lambda i: (i * tile) // tokens_per_cond)
    new_d = new_ml = None
    for layer in range(depth):
        mod = mods[layer]
        kind, j = layer % 3, layer // 3
        if kind == 0:
            x = _fnet_mix(x, seq, p["norm_mix"][layer], mod, tile_cond(seq), p["fnet_w"][j], p["fnet_b"][j])
        elif kind == 1:
            n_main = DN_HEADS * (2 * DN_DK + 2 * DN_DV)
            main, gt = _norm_proj(x, p["norm_mix"][layer], mod, tile_cond(PROJ_TOKEN_TILE), p["dn_w_in"][j], n_main)
            s0 = None if st_d is None else st_d[:, j]
            o, sfin = _gdn_scan(main, gt, batch, seq, p["dn_conv_w"][j], p["dn_a_log"][j], p["dn_dt_bias"][j],
                                s0, emit_state)
            if emit_state:
                new_d = sfin
            x = _gated_out(x, o, main, 3, p["dn_norm"][j], mod, tile_cond(OUT_TOKEN_TILE), p["dn_w_out"][j],
                           DN_HEADS, jax.nn.silu)
        else:
            n_main = 2 * ML_HEADS * ML_DQK + 2 * ML_HEADS * ML_DV
            main, gt = _norm_proj(x, p["norm_mix"][layer], mod, tile_cond(PROJ_TOKEN_TILE), p["ml_w_in"][j], n_main)
            state0 = None
            if st_ml is not None:
                c0, n0, m0 = (s[:, j] for s in st_ml)
                pad = jnp.zeros(c0.shape[:-1] + (ML_AUG - ML_DV - 1,), F32)
                c_aug0 = jnp.concatenate([c0, n0[..., None], pad], axis=-1)
                m0b = jnp.broadcast_to(m0[..., None, None], m0.shape + (1, 128))
                state0 = (c_aug0, m0b)
            o, fin = _mlstm_scan(main, gt, batch, seq, p["ml_b_i"][j], p["ml_b_f"][j], state0, emit_state)
            if emit_state:
                new_ml = (fin[0][..., :ML_DV], fin[0][..., ML_DV], fin[1][..., 0, 0])
            x = _gated_out(x, o, main, 2, p["ml_norm"][j], mod, tile_cond(OUT_TOKEN_TILE), p["ml_w_out"][j],
                           ML_HEADS, jax.nn.sigmoid)
        x = _conv_ffn(x, seq, rows, p["norm_ffn"][layer], mod, tile_cond(FFN_TOKEN_TILE), p["ffn_w_up"][layer],
                      p["ffn_conv_w"][layer], p["ffn_conv_b"][layer], p["ffn_w_down"][layer], p["norm_final"],
                      final_norm=(layer == depth - 1))
    return x, new_d, new_ml


def kernel(x_prompt, x_sample, state_delta, state_mlstm_c, state_mlstm_n, state_mlstm_m, c, c_ctx, w_ada, b_ada, norm_mix, norm_ffn, norm_final, ffn_w_up, ffn_conv_w, ffn_conv_b, ffn_w_down, fnet_w, fnet_b, dn_w_in, dn_conv_w, dn_a_log, dn_dt_bias, dn_norm, dn_w_out, ml_w_in, ml_b_i, ml_b_f, ml_norm, ml_w_out):
    p = dict(w_ada=w_ada, norm_mix=norm_mix, norm_ffn=norm_ffn, norm_final=norm_final, ffn_w_up=ffn_w_up,
             ffn_conv_w=ffn_conv_w, ffn_conv_b=ffn_conv_b, ffn_w_down=ffn_w_down, fnet_w=fnet_w, fnet_b=fnet_b,
             dn_w_in=dn_w_in, dn_conv_w=dn_conv_w, dn_a_log=dn_a_log, dn_dt_bias=dn_dt_bias, dn_norm=dn_norm,
             dn_w_out=dn_w_out, ml_w_in=ml_w_in, ml_b_i=ml_b_i, ml_b_f=ml_b_f, ml_norm=ml_norm, ml_w_out=ml_w_out)
    b_ctx, t_ctx, _ = x_prompt.shape
    b_smp, t_smp, _ = x_sample.shape
    depth = w_ada.shape[0]

    cond8 = jnp.concatenate([c_ctx[None, :], c, jnp.zeros((8 - 1 - b_smp, D_MODEL), F32)], axis=0)
    mods = _adaln_all(cond8, w_ada, b_ada)
    mods_ctx = [mods[l, 0:1].reshape(1, 1, -1) for l in range(depth)]
    mods_smp = [mods[l, 1:1 + b_smp].reshape(b_smp, 1, -1) for l in range(depth)]

    y_ctx, new_d, new_ml = _trunk(x_prompt.reshape(b_ctx * t_ctx, D_MODEL), b_ctx, t_ctx, 1, mods_ctx,
                                  b_ctx * t_ctx, None, None, True, p)
    y_smp, _, _ = _trunk(x_sample.reshape(b_smp * t_smp, D_MODEL), b_smp, t_smp, t_smp // GRID_W, mods_smp,
                         t_smp, state_delta, (state_mlstm_c, state_mlstm_n, state_mlstm_m), False, p)

    new_c, new_n, new_m = new_ml
    return (y_ctx.reshape(b_ctx, t_ctx, D_MODEL), y_smp.reshape(b_smp, t_smp, D_MODEL),
            new_d[:, None], new_c[:, None], new_n[:, None], new_m[:, None])
```

```python
import collections
import functools

import numpy as np
import jax
import jax.numpy as jnp
from jax import lax
from jax.experimental import pallas as pl
from jax.experimental.pallas import tpu as pltpu

F32 = jnp.float32
BF16 = jnp.bfloat16
HIGHEST = lax.Precision.HIGHEST

D_MODEL = 1024
EPS = 1e-6
N_DIR = 2
CHUNK = 64
LOG_CHUNK = 6
GROUP = 256
CHUNKS_PER_GROUP = GROUP // CHUNK
FNET_GROUP_DIM = 256
DN_HEADS, DN_DK, DN_DV, DN_CONV = 8, 128, 128, 5
ML_HEADS, ML_DQK, ML_DV = 8, 64, 128
D_FF = 2816
GRID_W = 64

FFN_TOKEN_TILE = 1024
FFN_FF_TILE = 256
PROJ_TOKEN_TILE = 1024
PROJ_COL_TILE = 512
OUT_TOKEN_TILE = 512
ADA_COL_TILE = 1536
VMEM_LIMIT = 56 * 1024 * 1024


def _params(*sem):
    return pltpu.CompilerParams(dimension_semantics=sem, vmem_limit_bytes=VMEM_LIMIT)


def _dot(a, b):
    return jnp.dot(a.astype(BF16), b.astype(BF16), preferred_element_type=F32)


def _dot_nt(a, b):
    return lax.dot_general(a.astype(BF16), b.astype(BF16), (((1,), (1,)), ((), ())),
                           preferred_element_type=F32)


def _dot_tn(a, b):
    return lax.dot_general(a.astype(BF16), b.astype(BF16), (((0,), (0,)), ((), ())),
                           preferred_element_type=F32)


def _dot_f32(a, b):
    return jnp.dot(a, b, precision=HIGHEST, preferred_element_type=F32)


def _rms(x, g):
    return x * lax.rsqrt(jnp.mean(x * x, axis=-1, keepdims=True) + EPS) * g


def _norm_mod(x, g, sc, sh):
    return _rms(x, g) * (1.0 + sc) + sh


def _softplus(x):
    return jnp.maximum(x, 0.0) + jnp.log1p(jnp.exp(-jnp.abs(x)))


def _mod_spec(chunk, cond_of_tile):
    return pl.BlockSpec((1, 1, D_MODEL), lambda i, *_: (cond_of_tile(i), 0, chunk))


def _adaln_kernel(c_ref, w_ref, b_ref, o_ref):
    s = jax.nn.silu(c_ref[...])
    o_ref[0] = _dot(s, w_ref[0]) + b_ref[0]


def _adaln_all(cond8, w_ada, b_ada):
    depth = w_ada.shape[0]
    n_out = w_ada.shape[2]
    tn = ADA_COL_TILE
    return pl.pallas_call(
        _adaln_kernel,
        grid=(depth, n_out // tn),
        in_specs=[pl.BlockSpec((8, D_MODEL), lambda l, j: (0, 0)),
                  pl.BlockSpec((1, D_MODEL, tn), lambda l, j: (l, 0, j)),
                  pl.BlockSpec((1, 1, tn), lambda l, j: (l, 0, j))],
        out_specs=pl.BlockSpec((1, 8, tn), lambda l, j: (l, 0, j)),
        out_shape=jax.ShapeDtypeStruct((depth, 8, n_out), F32),
        compiler_params=_params("parallel", "parallel"),
        name="adaln",
    )(cond8, w_ada, b_ada.reshape(depth, 1, n_out))


def _norm_proj_kernel(x_ref, g_ref, sc_ref, sh_ref, w_ref, wgt_ref, o_ref, gt_ref, h_scr):
    @pl.when(pl.program_id(1) == 0)
    def _():
        h = _norm_mod(x_ref[...], g_ref[...], sc_ref[0], sh_ref[0]).astype(BF16)
        h_scr[...] = h
        gt_ref[...] = _dot_nt(wgt_ref[...], h)

    o_ref[...] = _dot(h_scr[...], w_ref[...])


def _norm_proj(x, norm_g, mod, cond_of_tile, w_in, n_main):
    n_tok = x.shape[0]
    tm, tn = PROJ_TOKEN_TILE, PROJ_COL_TILE
    n_gate = w_in.shape[1] - n_main
    wgt = w_in[:, n_main:].T
    return pl.pallas_call(
        _norm_proj_kernel,
        grid=(n_tok // tm, n_main // tn),
        in_specs=[pl.BlockSpec((tm, D_MODEL), lambda i, j: (i, 0)),
                  pl.BlockSpec((1, D_MODEL), lambda i, j: (0, 0)),
                  _mod_spec(1, cond_of_tile), _mod_spec(0, cond_of_tile),
                  pl.BlockSpec((D_MODEL, tn), lambda i, j: (0, j)),
                  pl.BlockSpec((n_gate, D_MODEL), lambda i, j: (0, 0))],
        out_specs=[pl.BlockSpec((tm, tn), lambda i, j: (i, j)),
                   pl.BlockSpec((n_gate, tm), lambda i, j: (0, i))],
        out_shape=[jax.ShapeDtypeStruct((n_tok, n_main), F32),
                   jax.ShapeDtypeStruct((n_gate, n_tok), F32)],
        scratch_shapes=[pltpu.VMEM((tm, D_MODEL), BF16)],
        compiler_params=_params("parallel", "arbitrary"),
        name="norm_proj",
    )(x, norm_g.reshape(1, D_MODEL), mod, mod, w_in, wgt)


def _gated_out_kernel(x_ref, o_ref, z_ref, ng_ref, g1_ref, w_ref, y_ref, wb_scr, *, n_heads, gate_fn):
    @pl.when(pl.program_id(0) == 0)
    def _():
        wb_scr[...] = w_ref[...].astype(BF16)

    dv = o_ref.shape[1] // n_heads
    parts = []
    for h in range(n_heads):
        sl = slice(h * dv, (h + 1) * dv)
        parts.append((_rms(o_ref[:, sl], ng_ref[...]) * gate_fn(z_ref[:, sl])).astype(BF16))
    hs = jnp.concatenate(parts, axis=1)
    y = jnp.dot(hs, wb_scr[...], preferred_element_type=F32)
    y_ref[...] = x_ref[...] + g1_ref[0] * y


def _gated_out(x, o, main, z_block, norm_g, mod, cond_of_tile, w_out, n_heads, gate_fn):
    n_tok = x.shape[0]
    tm = OUT_TOKEN_TILE
    width = o.shape[1]
    return pl.pallas_call(
        functools.partial(_gated_out_kernel, n_heads=n_heads, gate_fn=gate_fn),
        grid=(n_tok // tm,),
        in_specs=[pl.BlockSpec((tm, D_MODEL), lambda i: (i, 0)),
                  pl.BlockSpec((tm, width), lambda i: (i, 0)),
                  pl.BlockSpec((tm, width), lambda i: (i, z_block)),
                  pl.BlockSpec((1, width // n_heads), lambda i: (0, 0)),
                  _mod_spec(2, cond_of_tile),
                  pl.BlockSpec((width, D_MODEL), lambda i: (0, 0))],
        out_specs=pl.BlockSpec((tm, D_MODEL), lambda i: (i, 0)),
        out_shape=jax.ShapeDtypeStruct((n_tok, D_MODEL), F32),
        scratch_shapes=[pltpu.VMEM((width, D_MODEL), BF16)],
        compiler_params=_params("arbitrary"),
        name="gated_out",
    )(x, o, main, norm_g.reshape(1, -1), mod, w_out)


def _dft_mats(n):
    k = np.arange(n, dtype=np.int64)
    ang = 2.0 * np.pi * ((k[:, None] * k[None, :]) % n).astype(np.float64) / n
    s = 1.0 / np.sqrt(n)
    return np.cos(ang) * s, np.sin(ang) * s


def _fnet_kernel(x_ref, ng_ref, sc_ref, sh_ref, g1_ref, ct_ref, st_ref, cs_ref, w_ref, b_ref, y_ref, wb_scr):
    @pl.when(pl.program_id(0) == 0)
    def _():
        wb_scr[...] = w_ref[...].astype(BF16)

    x = x_ref[...]
    h = _norm_mod(x, ng_ref[...], sc_ref[0], sh_ref[0]).astype(BF16)
    p = jnp.dot(ct_ref[...], h, preferred_element_type=F32)
    q = jnp.dot(st_ref[...], h, preferred_element_type=F32)
    gd = FNET_GROUP_DIM
    parts = []
    for g in range(D_MODEL // gd):
        sl = slice(g * gd, (g + 1) * gd)
        pq = jnp.concatenate([p[:, sl], q[:, sl]], axis=1).astype(BF16)
        parts.append(jnp.dot(pq, cs_ref[...], preferred_element_type=F32).astype(BF16))
    f = jnp.concatenate(parts, axis=1)
    y = jnp.dot(f, wb_scr[...], preferred_element_type=F32) + b_ref[...]
    y_ref[...] = x + g1_ref[0] * y


def _fnet_mix(x, seq, norm_g, mod, cond_of_seq, w, b):
    n_tok = x.shape[0]
    ct, st = _dft_mats(seq)
    cc, sc = _dft_mats(FNET_GROUP_DIM)
    cs = np.concatenate([cc, -sc], axis=0)
    const = lambda i: (0, 0)
    ct, st, cs = (jnp.asarray(m, F32).astype(BF16) for m in (ct, st, cs))
    return pl.pallas_call(
        _fnet_kernel,
        grid=(n_tok // seq,),
        in_specs=[pl.BlockSpec((seq, D_MODEL), lambda i: (i, 0)),
                  pl.BlockSpec((1, D_MODEL), const),
                  _mod_spec(1, cond_of_seq), _mod_spec(0, cond_of_seq), _mod_spec(2, cond_of_seq),
                  pl.BlockSpec((seq, seq), const),
                  pl.BlockSpec((seq, seq), const),
                  pl.BlockSpec((2 * FNET_GROUP_DIM, FNET_GROUP_DIM), const),
                  pl.BlockSpec((D_MODEL, D_MODEL), const),
                  pl.BlockSpec((1, D_MODEL), const)],
        out_specs=pl.BlockSpec((seq, D_MODEL), lambda i: (i, 0)),
        out_shape=jax.ShapeDtypeStruct((n_tok, D_MODEL), F32),
        scratch_shapes=[pltpu.VMEM((D_MODEL, D_MODEL), BF16)],
        compiler_params=_params("arbitrary"),
        name="fnet",
    )(x, norm_g.reshape(1, D_MODEL), mod, mod, mod,
      ct, st, cs, w, b.reshape(1, D_MODEL))


def _shift_rows(x, delta):
    n = x.shape[0]
    return pltpu.roll(x, (-delta) % n, 0)


def _dwconv_tokens(g, cw_ref, seq, rows):
    tm = g.shape[0]
    width = seq // rows
    assert seq & (seq - 1) == 0 and width & (width - 1) == 0
    t = lax.broadcasted_iota(jnp.int32, (tm, 1), 0)
    pos = t & (seq - 1)
    col = pos & (width - 1)
    row = pos >> (width.bit_length() - 1)
    g_cols = (_shift_rows(g, -1) * (col >= 1).astype(F32), g,
              _shift_rows(g, 1) * (col <= width - 2).astype(F32))
    out = None
    for di in (-1, 0, 1):
        if rows == 1 and di != 0:
            continue
        r = sum(g_cols[dj] * cw_ref[pl.ds(3 * (di + 1) + dj, 1), :] for dj in range(3))
        if di != 0:
            ok = (row + di >= 0) & (row + di <= rows - 1)
            r = _shift_rows(r, di * width) * ok.astype(F32)
        out = r if out is None else out + r
    return out


def _ffn_kernel(x_ref, ng_ref, sc_ref, sh_ref, g2_ref, wa_ref, wg_ref, cw_ref, cb_ref, wd_ref, nf_ref,
                y_ref, h_scr, acc_scr, *, seq, rows, final_norm):
    j = pl.program_id(1)

    @pl.when(j == 0)
    def _():
        h_scr[...] = _norm_mod(x_ref[...], ng_ref[...], sc_ref[0], sh_ref[0]).astype(BF16)
        acc_scr[...] = jnp.zeros_like(acc_scr)

    h = h_scr[...]
    a = jnp.dot(h, wa_ref[...].astype(BF16), preferred_element_type=F32)
    g = jnp.dot(h, wg_ref[...].astype(BF16), preferred_element_type=F32)
    g = _dwconv_tokens(g, cw_ref, seq, rows) + cb_ref[...]
    act = (jax.nn.silu(g) * a).astype(BF16)
    acc_scr[...] += jnp.dot(act, wd_ref[...].astype(BF16), preferred_element_type=F32)

    @pl.when(j == pl.num_programs(1) - 1)
    def _():
        y = x_ref[...] + g2_ref[0] * acc_scr[...]
        if final_norm:
            y = _rms(y, nf_ref[...])
        y_ref[...] = y


def _conv_ffn(x, seq, rows, norm_g, mod, cond_of_tile, w_up, conv_w, conv_b, w_down, norm_final, final_norm):
    n_tok = x.shape[0]
    tm, tf = FFN_TOKEN_TILE, FFN_FF_TILE
    n_ff_tiles = D_FF // tf
    return pl.pallas_call(
        functools.partial(_ffn_kernel, seq=seq, rows=rows, final_norm=final_norm),
        grid=(n_tok // tm, n_ff_tiles),
        in_specs=[pl.BlockSpec((tm, D_MODEL), lambda i, j: (i, 0)),
                  pl.BlockSpec((1, D_MODEL), lambda i, j: (0, 0)),
                  _mod_spec(4, cond_of_tile), _mod_spec(3, cond_of_tile), _mod_spec(5, cond_of_tile),
                  pl.BlockSpec((D_MODEL, tf), lambda i, j: (0, j)),
                  pl.BlockSpec((D_MODEL, tf), lambda i, j: (0, n_ff_tiles + j)),
                  pl.BlockSpec((9, tf), lambda i, j: (0, j)),
                  pl.BlockSpec((1, tf), lambda i, j: (0, j)),
                  pl.BlockSpec((tf, D_MODEL), lambda i, j: (j, 0)),
                  pl.BlockSpec((1, D_MODEL), lambda i, j: (0, 0))],
        out_specs=pl.BlockSpec((tm, D_MODEL), lambda i, j: (i, 0)),
        out_shape=jax.ShapeDtypeStruct((n_tok, D_MODEL), F32),
        scratch_shapes=[pltpu.VMEM((tm, D_MODEL), BF16), pltpu.VMEM((tm, D_MODEL), F32)],
        compiler_params=_params("parallel", "arbitrary"),
        name="conv_ffn",
    )(x, norm_g.reshape(1, D_MODEL), mod, mod, mod, w_up, w_up, conv_w.reshape(9, D_FF),
      conv_b.reshape(1, D_FF), w_down, norm_final.reshape(1, D_MODEL))


GroupMasks = collections.namedtuple("GroupMasks", "r c same incl strict")


def _chunk_masks():
    r = lax.broadcasted_iota(jnp.int32, (CHUNK, CHUNK), 0)
    c = lax.broadcasted_iota(jnp.int32, (CHUNK, CHUNK), 1)
    return (r >= c, r <= c)


def _group_masks():
    r = lax.broadcasted_iota(jnp.int32, (GROUP, GROUP), 0)
    c = lax.broadcasted_iota(jnp.int32, (GROUP, GROUP), 1)
    same = (r >> LOG_CHUNK) == (c >> LOG_CHUNK)
    return GroupMasks(r, c, same, (same & (r >= c), same & (r <= c)), (same & (r > c), same & (r < c)))


def _gate_layouts(gt, batch, n_heads):
    seq = gt.shape[1] // batch
    n_groups = seq // GROUP
    assert seq % GROUP == 0 and n_groups <= 8
    g = jnp.transpose(gt.reshape(N_DIR, 2, n_heads, batch, seq), (3, 2, 0, 1, 4))
    rows = g.reshape(g.shape[:4] + (n_groups, GROUP))
    rows = jnp.pad(rows, ((0, 0),) * 4 + ((0, 8 - n_groups), (0, 0)))
    cols = jnp.swapaxes(g.reshape(g.shape[:4] + (seq // CHUNK, CHUNK)), -1, -2)
    return rows, cols


def _cumsum_rows(row, d, gm):
    return _dot_f32(row, gm.incl[1 - d].astype(F32))


def _cumsum_cols(col, d, cm):
    return _dot_f32(cm[d].astype(F32), col)


def _store_cols(dst, col, n_chunks):
    for n in range(n_chunks):
        dst[n * CHUNK:(n + 1) * CHUNK, :] = jnp.broadcast_to(col[:, n:n + 1], (CHUNK, 128))


def _wide(x):
    return jnp.concatenate([x, x], axis=1)


def _rows(i, size):
    if isinstance(i, int):
        return pl.ds(i * size, size)
    return pl.ds(pl.multiple_of(i * size, size), size)


def _loop(n, body, init):
    if n == 1:
        return body(0, init)
    return lax.fori_loop(0, n, body, init)


def _conv_silu(x, w_ref):
    n = x.shape[0]
    k = w_ref.shape[0]
    t = lax.broadcasted_iota(jnp.int32, (n, 1), 0)
    acc = None
    for j in range(k):
        delta = j - k // 2
        if delta == 0:
            term = x
        else:
            ok = (t + delta >= 0) & (t + delta <= n - 1)
            term = _shift_rows(x, delta) * ok.astype(F32)
        term = term * w_ref[pl.ds(j, 1), :]
        acc = term if acc is None else acc + term
    return jax.nn.silu(acc)


def _l2norm(x):
    return x * lax.rsqrt(jnp.sum(x * x, axis=-1, keepdims=True) + EPS)


def _blockdiag_tri_inverse(a, gm):
    x = -jnp.where((gm.r >> 3) == (gm.c >> 3), a, 0.0)
    inv = jnp.where(gm.r == gm.c, 1.0, 0.0) + x
    p = x
    for _ in range(2):
        p = _dot_f32(p, p)
        inv = inv + _dot_f32(inv, p)
    for s in (3, 4, 5):
        off = jnp.where(((gm.r >> s) ^ (gm.c >> s)) == 1, a, 0.0)
        inv = inv - _dot_f32(inv, _dot_f32(off, inv))
    return inv


def _gdn_kernel(*refs, seq, zero_init, emit_state):
    alog_ref, dtb_ref, q_ref, k_ref, v_ref, cwq_ref, cwk_ref, cwv_ref, grow_ref, gcol_ref = refs[:10]
    pos = 10
    s0_ref = None
    if not zero_init:
        s0_ref = refs[pos]
        pos += 1
    o_ref = refs[pos]
    pos += 1
    sfin_ref = None
    if emit_state:
        sfin_ref = refs[pos]
        pos += 1
    (q_scr, k_scr, v_scr, gcr_scr, gcb_scr, bb_scr, gl_scr,
     u_scr, w_scr, qd_scr, kd_scr, qkd_scr) = refs[pos:]

    n_chunks = seq // CHUNK
    head = pl.program_id(1)
    cm = _chunk_masks()
    gm = _group_masks()

    q_scr[...] = _l2norm(_conv_silu(q_ref[...], cwq_ref)) * (DN_DK ** -0.5)
    k_scr[...] = _l2norm(_conv_silu(k_ref[...], cwk_ref))
    v_scr[...] = _conv_silu(v_ref[...], cwv_ref)

    for d in range(N_DIR):
        neg_a = -jnp.exp(jnp.full((1, 1), alog_ref[d, head], F32))
        dtb = dtb_ref[d, head]
        gcr_scr[d] = _cumsum_rows(neg_a * _softplus(grow_ref[0, 0, d, 0] + dtb), d, gm)
        gc_col = _cumsum_cols(neg_a * _softplus(gcol_ref[0, 0, d, 0] + dtb), d, cm)
        _store_cols(gcb_scr.at[d], gc_col, n_chunks)
        _store_cols(bb_scr.at[d], jax.nn.sigmoid(gcol_ref[0, 0, d, 1]), n_chunks)
        last = CHUNK - 1 if d == 0 else 0
        _store_cols(gl_scr.at[d], jnp.broadcast_to(gc_col[last:last + 1, :], gc_col.shape), n_chunks)

    def group_step(gi, carry):
        rows = _rows(gi, GROUP)
        qg, kg, vg = q_scr[rows, :], k_scr[rows, :], v_scr[rows, :]
        kq = _dot_nt(jnp.concatenate([kg, qg], axis=0), kg)
        for d in range(N_DIR):
            gcb = gcb_scr[d, rows, :]
            bb = bb_scr[d, rows, :]
            gcr = gcr_scr[d, pl.ds(gi, 1), :]
            decay = jnp.exp(jnp.where(gm.incl[d], _wide(gcb) - gcr, -jnp.inf))
            a = jnp.where(gm.strict[d], kq[:GROUP] * _wide(bb) * decay, 0.0)
            t_inv = _blockdiag_tri_inverse(a, gm)
            e_gc = jnp.exp(gcb)
            uw = _dot_f32(t_inv, jnp.concatenate([vg * bb, kg * bb * e_gc], axis=1))
            u_scr[d, rows, :] = uw[:, :DN_DV]
            w_scr[d, rows, :] = uw[:, DN_DV:]
            qd_scr[d, rows, :] = qg * e_gc
            kd_scr[d, rows, :] = kg * jnp.exp(gl_scr[d, rows, :] - gcb)
            qkd = kq[GROUP:] * decay
            qkd_scr[d, rows, :] = sum(qkd[:, n * CHUNK:(n + 1) * CHUNK] for n in range(CHUNKS_PER_GROUP))
        return carry

    _loop(seq // GROUP, group_step, 0)

    o_ref[...] = jnp.zeros_like(o_ref)

    def chunk_step(n, carry):
        new = []
        for d in range(N_DIR):
            idx = n if d == 0 else n_chunks - 1 - n
            rows = _rows(idx, CHUNK)
            s = carry[d]
            wq = _dot(jnp.concatenate([w_scr[d, rows, :], qd_scr[d, rows, :]], axis=0), s)
            v_new = u_scr[d, rows, :] - wq[:CHUNK]
            o_n = wq[CHUNK:] + _dot(qkd_scr[d, rows, :], v_new)
            s_decay = jnp.exp(gl_scr[d, pl.ds(idx * CHUNK, 1), :])
            new.append(s * s_decay + _dot_tn(kd_scr[d, rows, :], v_new))
            o_ref[rows, :] += o_n
        return tuple(new)

    if zero_init:
        init = (jnp.zeros((DN_DK, DN_DV), F32),) * N_DIR
    else:
        init = tuple(s0_ref[0, d, 0] for d in range(N_DIR))
    fin = lax.fori_loop(0, n_chunks, chunk_step, init)
    if emit_state:
        for d in range(N_DIR):
            sfin_ref[0, d, 0] = fin[d]


def _gdn_scan(main, gt, batch, seq, conv_w, a_log, dt_bias, s0, emit_state):
    n_tok = batch * seq
    nh = DN_HEADS
    grow, gcol = _gate_layouts(gt, batch, nh)
    zero_init = s0 is None
    smem = pl.BlockSpec(memory_space=pltpu.SMEM)
    qkv_spec = lambda off: pl.BlockSpec((seq, DN_DK), lambda b, h: (b, off + h))
    cw_spec = lambda off: pl.BlockSpec((DN_CONV, DN_DK), lambda b, h: (0, off + h))
    state_spec = pl.BlockSpec((1, N_DIR, 1, DN_DK, DN_DV), lambda b, h: (b, 0, h, 0, 0))
    in_specs = [smem, smem, qkv_spec(0), qkv_spec(nh), qkv_spec(2 * nh), cw_spec(0), cw_spec(nh), cw_spec(2 * nh),
                pl.BlockSpec((1, 1) + grow.shape[2:], lambda b, h: (b, h, 0, 0, 0, 0)),
                pl.BlockSpec((1, 1) + gcol.shape[2:], lambda b, h: (b, h, 0, 0, 0, 0))]
    args = [a_log, dt_bias, main, main, main, conv_w, conv_w, conv_w, grow, gcol]
    if not zero_init:
        in_specs.append(state_spec)
        args.append(s0)
    out_specs = [pl.BlockSpec((seq, DN_DV), lambda b, h: (b, h))]
    out_shape = [jax.ShapeDtypeStruct((n_tok, nh * DN_DV), F32)]
    if emit_state:
        out_specs.append(state_spec)
        out_shape.append(jax.ShapeDtypeStruct((batch, N_DIR, nh, DN_DK, DN_DV), F32))
    per_dir = lambda width: pltpu.VMEM((N_DIR, seq, width), F32)
    outs = pl.pallas_call(
        functools.partial(_gdn_kernel, seq=seq, zero_init=zero_init, emit_state=emit_state),
        grid=(batch, nh),
        in_specs=in_specs, out_specs=out_specs, out_shape=out_shape,
        scratch_shapes=[pltpu.VMEM((seq, DN_DK), F32), pltpu.VMEM((seq, DN_DK), F32), pltpu.VMEM((seq, DN_DV), F32),
                        pltpu.VMEM((N_DIR, 8, GROUP), F32),
                        per_dir(128), per_dir(128), per_dir(128),
                        per_dir(DN_DV), per_dir(DN_DK), per_dir(DN_DK), per_dir(DN_DK), per_dir(CHUNK)],
        compiler_params=_params("parallel", "parallel"),
        name="gdn_scan",
    )(*args)
    return outs[0], (outs[1] if emit_state else None)


ML_AUG = 2 * ML_DV


def _mlstm_kernel(*refs, seq, zero_init, emit_state):
    bi_ref, bf_ref, q_ref, k_ref, v_ref, grow_ref, gcol_ref = refs[:7]
    pos = 7
    c0_ref = m0_ref = None
    if not zero_init:
        c0_ref, m0_ref = refs[pos:pos + 2]
        pos += 2
    o_ref = refs[pos]
    pos += 1
    cfin_ref = mfin_ref = None
    if emit_state:
        cfin_ref, mfin_ref = refs[pos:pos + 2]
        pos += 2
    bcr_scr, lir_scr, bcb_scr, mi_scr, kwf_scr, dec_scr, dc_scr, cin_scr = refs[pos:]

    n_chunks = seq // CHUNK
    n_groups = seq // GROUP
    pair = pl.program_id(1)
    cm = _chunk_masks()
    gm = _group_masks()
    ones_col = jnp.where(lax.broadcasted_iota(jnp.int32, (GROUP, ML_DV), 1) == 0, 1.0, 0.0)
    head_dirs = [(hh, d) for hh in range(2) for d in range(N_DIR)]

    m_fin = {}
    for hh, d in head_dirs:
        b_i = bi_ref[d, 2 * pair + hh]
        b_f = bf_ref[d, 2 * pair + hh]
        lir_scr[hh, d] = grow_ref[0, hh, d, 0] + b_i
        bcr_scr[hh, d] = _cumsum_rows(-_softplus(-(grow_ref[0, hh, d, 1] + b_f)), d, gm)
        li_col = gcol_ref[0, hh, d, 0] + b_i
        bc_col = _cumsum_cols(-_softplus(-(gcol_ref[0, hh, d, 1] + b_f)), d, cm)
        last = CHUNK - 1 if d == 0 else 0
        b_last = bc_col[last:last + 1, :]
        w_col = b_last - bc_col + li_col
        w_max = jnp.max(w_col, axis=0, keepdims=True)
        m = jnp.zeros((1, 1), F32) if zero_init else m0_ref[0, d, hh][:, 0:1]
        for step in range(n_chunks):
            n = step if d == 0 else n_chunks - 1 - step
            sl = slice(n * CHUNK, (n + 1) * CHUNK)
            m_new = jnp.maximum(b_last[:, n:n + 1] + m, w_max[:, n:n + 1])
            bcb_scr[hh, d, sl, :] = jnp.broadcast_to(bc_col[:, n:n + 1], (CHUNK, 128))
            mi_scr[hh, d, sl, :] = jnp.broadcast_to(m, (CHUNK, 128))
            kwf_scr[hh, d, sl, :] = jnp.broadcast_to(jnp.exp(w_col[:, n:n + 1] - m_new), (CHUNK, 128))
            dec_scr[hh, d, n:n + 1, :] = jnp.broadcast_to(jnp.exp(b_last[:, n:n + 1] + m - m_new), (1, 128))
            m = m_new
        m_fin[hh, d] = m

    def load_kv(rows, hh):
        kg = k_ref[rows, hh * ML_DQK:(hh + 1) * ML_DQK]
        v_aug = jnp.concatenate([v_ref[rows, hh * ML_DV:(hh + 1) * ML_DV], ones_col], axis=1)
        return kg, v_aug

    def block_diag(x):
        return jnp.where(gm.same, jnp.concatenate([x] * CHUNKS_PER_GROUP, axis=1), 0.0)

    def delta_step(gi, carry):
        rows = _rows(gi, GROUP)
        for hh in range(2):
            kg, v_aug = load_kv(rows, hh)
            for d in range(N_DIR):
                kw = kg * kwf_scr[hh, d, rows, :][:, :ML_DQK]
                dc_scr[hh, d, rows, :] = _dot_tn(block_diag(kw), v_aug)
        return carry

    _loop(n_groups, delta_step, 0)

    def prefix_step(n, carry):
        new = []
        for hh, d in head_dirs:
            idx = n if d == 0 else n_chunks - 1 - n
            rows = _rows(idx, CHUNK)
            c_aug = carry[2 * hh + d]
            cin_scr[hh, d, rows, :] = c_aug
            new.append(c_aug * _wide(dec_scr[hh, d, pl.ds(idx, 1), :]) + dc_scr[hh, d, rows, :])
        return tuple(new)

    if zero_init:
        init = (jnp.zeros((ML_DQK, ML_AUG), F32),) * 4
    else:
        init = tuple(c0_ref[0, d, hh] for hh, d in head_dirs)
    fin = lax.fori_loop(0, n_chunks, prefix_step, init)

    def out_step(gi, carry):
        rows = _rows(gi, GROUP)
        for hh in range(2):
            kg, v_aug = load_kv(rows, hh)
            qg = q_ref[rows, hh * ML_DQK:(hh + 1) * ML_DQK] * (ML_DQK ** -0.5)
            qk = _dot_nt(qg, kg)
            q_bd = block_diag(qg)
            h_sum = None
            for d in range(N_DIR):
                b_colb = bcb_scr[hh, d, rows, :]
                b_m = b_colb[:, 0:1] + mi_scr[hh, d, rows, :][:, 0:1]
                d_log = jnp.where(gm.incl[d], _wide(b_colb) - bcr_scr[hh, d, pl.ds(gi, 1), :]
                                  + lir_scr[hh, d, pl.ds(gi, 1), :], -jnp.inf)
                m_t = jnp.maximum(b_m, jnp.max(d_log, axis=-1, keepdims=True))
                s = qk * jnp.exp(d_log - m_t)
                num = jnp.exp(b_m - m_t) * _dot(q_bd, cin_scr[hh, d, rows, :]) + _dot(s, v_aug)
                den = num[:, ML_DV:ML_DV + 1]
                h = num[:, :ML_DV] / jnp.maximum(jnp.abs(den), jnp.exp(-m_t))
                h_sum = h if h_sum is None else h_sum + h
            o_ref[rows, hh * ML_DV:(hh + 1) * ML_DV] = h_sum
        return carry

    _loop(n_groups, out_step, 0)

    if emit_state:
        for hh, d in head_dirs:
            cfin_ref[0, d, hh] = fin[2 * hh + d]
            mfin_ref[0, d, hh] = jnp.broadcast_to(m_fin[hh, d], (1, 128))


def _mlstm_scan(main, gt, batch, seq, b_i, b_f, state0, emit_state):
    n_tok = batch * seq
    n_chunks = seq // CHUNK
    nh = ML_HEADS
    n_pairs = nh // 2
    grow, gcol = _gate_layouts(gt, batch, nh)
    zero_init = state0 is None
    smem = pl.BlockSpec(memory_space=pltpu.SMEM)
    q_off = 0
    k_off = (nh * ML_DQK) // (2 * ML_DQK)
    v_off = (2 * nh * ML_DQK) // (2 * ML_DV)
    c_spec = pl.BlockSpec((1, N_DIR, 2, ML_DQK, ML_AUG), lambda b, p: (b, 0, p, 0, 0))
    m_spec = pl.BlockSpec((1, N_DIR, 2, 1, 128), lambda b, p: (b, 0, p, 0, 0))
    in_specs = [smem, smem,
                pl.BlockSpec((seq, 2 * ML_DQK), lambda b, p: (b, q_off + p)),
                pl.BlockSpec((seq, 2 * ML_DQK), lambda b, p: (b, k_off + p)),
                pl.BlockSpec((seq, 2 * ML_DV), lambda b, p: (b, v_off + p)),
                pl.BlockSpec((1, 2) + grow.shape[2:], lambda b, p: (b, p, 0, 0, 0, 0)),
                pl.BlockSpec((1, 2) + gcol.shape[2:], lambda b, p: (b, p, 0, 0, 0, 0))]
    args = [b_i, b_f, main, main, main, grow, gcol]
    if not zero_init:
        in_specs += [c_spec, m_spec]
        args += list(state0)
    out_specs = [pl.BlockSpec((seq, 2 * ML_DV), lambda b, p: (b, p))]
    out_shape = [jax.ShapeDtypeStruct((n_tok, nh * ML_DV), F32)]
    if emit_state:
        out_specs += [c_spec, m_spec]
        out_shape += [jax.ShapeDtypeStruct((batch, N_DIR, nh, ML_DQK, ML_AUG), F32),
                      jax.ShapeDtypeStruct((batch, N_DIR, nh, 1, 128), F32)]
    per_hd = lambda rows, width: pltpu.VMEM((2, N_DIR, rows, width), F32)
    outs = pl.pallas_call(
        functools.partial(_mlstm_kernel, seq=seq, zero_init=zero_init, emit_state=emit_state),
        grid=(batch, n_pairs),
        in_specs=in_specs, out_specs=out_specs, out_shape=out_shape,
        scratch_shapes=[per_hd(8, GROUP), per_hd(8, GROUP),
                        per_hd(seq, 128), per_hd(seq, 128), per_hd(seq, 128),
                        per_hd(max(n_chunks, 8), 128),
                        per_hd(seq, ML_AUG), per_hd(seq, ML_AUG)],
        compiler_params=_params("parallel", "parallel"),
        name="mlstm_scan",
    )(*args)
    return outs[0], ((outs[1], outs[2]) if emit_state else None)


def _trunk(x, batch, seq, rows, mods, tokens_per_cond, st_d, st_ml, emit_state, p):
    depth = p["w_ada"].shape[0]
    tile_cond = lambda tile: (lambda i: (i * tile) // tokens_per_cond)
    new_d = new_ml = None
    for layer in range(depth):
        mod = mods[layer]
        kind, j = layer % 3, layer // 3
        if kind == 0:
            x = _fnet_mix(x, seq, p["norm_mix"][layer], mod, tile_cond(seq), p["fnet_w"][j], p["fnet_b"][j])
        elif kind == 1:
            n_main = DN_HEADS * (2 * DN_DK + 2 * DN_DV)
            main, gt = _norm_proj(x, p["norm_mix"][layer], mod, tile_cond(PROJ_TOKEN_TILE), p["dn_w_in"][j], n_main)
            s0 = None if st_d is None else st_d[:, j]
            o, sfin = _gdn_scan(main, gt, batch, seq, p["dn_conv_w"][j], p["dn_a_log"][j], p["dn_dt_bias"][j],
                                s0, emit_state)
            if emit_state:
                new_d = sfin
            x = _gated_out(x, o, main, 3, p["dn_norm"][j], mod, tile_cond(OUT_TOKEN_TILE), p["dn_w_out"][j],
                           DN_HEADS, jax.nn.silu)
        else:
            n_main = 2 * ML_HEADS * ML_DQK + 2 * ML_HEADS * ML_DV
            main, gt = _norm_proj(x, p["norm_mix"][layer], mod, tile_cond(PROJ_TOKEN_TILE), p["ml_w_in"][j], n_main)
            state0 = None
            if st_ml is not None:
                c0, n0, m0 = (s[:, j] for s in st_ml)
                pad = jnp.zeros(c0.shape[:-1] + (ML_AUG - ML_DV - 1,), F32)
                c_aug0 = jnp.concatenate([c0, n0[..., None], pad], axis=-1)
                m0b = jnp.broadcast_to(m0[..., None, None], m0.shape + (1, 128))
                state0 = (c_aug0, m0b)
            o, fin = _mlstm_scan(main, gt, batch, seq, p["ml_b_i"][j], p["ml_b_f"][j], state0, emit_state)
            if emit_state:
                new_ml = (fin[0][..., :ML_DV], fin[0][..., ML_DV], fin[1][..., 0, 0])
            x = _gated_out(x, o, main, 2, p["ml_norm"][j], mod, tile_cond(OUT_TOKEN_TILE), p["ml_w_out"][j],
                           ML_HEADS, jax.nn.sigmoid)
        x = _conv_ffn(x, seq, rows, p["norm_ffn"][layer], mod, tile_cond(FFN_TOKEN_TILE), p["ffn_w_up"][layer],
                      p["ffn_conv_w"][layer], p["ffn_conv_b"][layer], p["ffn_w_down"][layer], p["norm_final"],
                      final_norm=(layer == depth - 1))
    return x, new_d, new_ml


def kernel(x_prompt, x_sample, state_delta, state_mlstm_c, state_mlstm_n, state_mlstm_m, c, c_ctx, w_ada, b_ada, norm_mix, norm_ffn, norm_final, ffn_w_up, ffn_conv_w, ffn_conv_b, ffn_w_down, fnet_w, fnet_b, dn_w_in, dn_conv_w, dn_a_log, dn_dt_bias, dn_norm, dn_w_out, ml_w_in, ml_b_i, ml_b_f, ml_norm, ml_w_out):
    p = dict(w_ada=w_ada, norm_mix=norm_mix, norm_ffn=norm_ffn, norm_final=norm_final, ffn_w_up=ffn_w_up,
             ffn_conv_w=ffn_conv_w, ffn_conv_b=ffn_conv_b, ffn_w_down=ffn_w_down, fnet_w=fnet_w, fnet_b=fnet_b,
             dn_w_in=dn_w_in, dn_conv_w=dn_conv_w, dn_a_log=dn_a_log, dn_dt_bias=dn_dt_bias, dn_norm=dn_norm,
             dn_w_out=dn_w_out, ml_w_in=ml_w_in, ml_b_i=ml_b_i, ml_b_f=ml_b_f, ml_norm=ml_norm, ml_w_out=ml_w_out)
    b_ctx, t_ctx, _ = x_prompt.shape
    b_smp, t_smp, _ = x_sample.shape
    depth = w_ada.shape[0]

    cond8 = jnp.concatenate([c_ctx[None, :], c, jnp.zeros((8 - 1 - b_smp, D_MODEL), F32)], axis=0)
    mods = _adaln_all(cond8, w_ada, b_ada)
    mods_ctx = [mods[l, 0:1].reshape(1, 1, -1) for l in range(depth)]
    mods_smp = [mods[l, 1:1 + b_smp].reshape(b_smp, 1, -1) for l in range(depth)]

    y_ctx, new_d, new_ml = _trunk(x_prompt.reshape(b_ctx * t_ctx, D_MODEL), b_ctx, t_ctx, 1, mods_ctx,
                                  b_ctx * t_ctx, None, None, True, p)
    y_smp, _, _ = _trunk(x_sample.reshape(b_smp * t_smp, D_MODEL), b_smp, t_smp, t_smp // GRID_W, mods_smp,
                         t_smp, state_delta, (state_mlstm_c, state_mlstm_n, state_mlstm_m), False, p)

    new_c, new_n, new_m = new_ml
    return (y_ctx.reshape(b_ctx, t_ctx, D_MODEL), y_smp.reshape(b_smp, t_smp, D_MODEL),
            new_d[:, None], new_c[:, None], new_n[:, None], new_m[:, None])
```

```python
import collections
import functools

import numpy as np
import jax
import jax.numpy as jnp
from jax import lax
from jax.experimental import pallas as pl
from jax.experimental.pallas import tpu as pltpu

F32 = jnp.float32
BF16 = jnp.bfloat16
HIGHEST = lax.Precision.HIGHEST

D_MODEL = 1024
EPS = 1e-6
N_DIR = 2
CHUNK = 64
LOG_CHUNK = 6
GROUP = 256
CHUNKS_PER_GROUP = GROUP // CHUNK
FNET_GROUP_DIM = 256
DN_HEADS, DN_DK, DN_DV, DN_CONV = 8, 128, 128, 5
ML_HEADS, ML_DQK, ML_DV = 8, 64, 128
D_FF = 2816
GRID_W = 64

FFN_TOKEN_TILE = 1024
FFN_FF_TILE = 256
PROJ_TOKEN_TILE = 1024
PROJ_COL_TILE = 512
OUT_TOKEN_TILE = 512
ADA_COL_TILE = 1536
SCAN_STEP_POSITIONS = 1024
VMEM_LIMIT = 56 * 1024 * 1024


def _params(*sem):
    return pltpu.CompilerParams(dimension_semantics=sem, vmem_limit_bytes=VMEM_LIMIT)


def _dot(a, b):
    return jnp.dot(a.astype(BF16), b.astype(BF16), preferred_element_type=F32)


def _dot_nt(a, b):
    return lax.dot_general(a.astype(BF16), b.astype(BF16), (((1,), (1,)), ((), ())),
                           preferred_element_type=F32)


def _dot_tn(a, b):
    return lax.dot_general(a.astype(BF16), b.astype(BF16), (((0,), (0,)), ((), ())),
                           preferred_element_type=F32)


def _dot_f32(a, b):
    return jnp.dot(a, b, precision=HIGHEST, preferred_element_type=F32)


def _rms(x, g):
    return x * lax.rsqrt(jnp.mean(x * x, axis=-1, keepdims=True) + EPS) * g


def _norm_mod(x, g, sc, sh):
    return _rms(x, g) * (1.0 + sc) + sh


def _softplus(x):
    return jnp.maximum(x, 0.0) + jnp.log1p(jnp.exp(-jnp.abs(x)))


def _mod_spec(chunk, cond_of_tile):
    return pl.BlockSpec((1, 1, D_MODEL), lambda i, *_: (cond_of_tile(i), 0, chunk))


def _adaln_kernel(c_ref, w_ref, b_ref, o_ref):
    s = jax.nn.silu(c_ref[...])
    o_ref[0] = _dot(s, w_ref[0]) + b_ref[0]


def _adaln_all(cond8, w_ada, b_ada):
    depth = w_ada.shape[0]
    n_out = w_ada.shape[2]
    tn = ADA_COL_TILE
    return pl.pallas_call(
        _adaln_kernel,
        grid=(depth, n_out // tn),
        in_specs=[pl.BlockSpec((8, D_MODEL), lambda l, j: (0, 0)),
                  pl.BlockSpec((1, D_MODEL, tn), lambda l, j: (l, 0, j)),
                  pl.BlockSpec((1, 1, tn), lambda l, j: (l, 0, j))],
        out_specs=pl.BlockSpec((1, 8, tn), lambda l, j: (l, 0, j)),
        out_shape=jax.ShapeDtypeStruct((depth, 8, n_out), F32),
        compiler_params=_params("parallel", "parallel"),
        name="adaln",
    )(cond8, w_ada, b_ada.reshape(depth, 1, n_out))


def _norm_proj_kernel(x_ref, g_ref, sc_ref, sh_ref, w_ref, wgt_ref, o_ref, gt_ref, h_scr):
    @pl.when(pl.program_id(1) == 0)
    def _():
        h = _norm_mod(x_ref[...], g_ref[...], sc_ref[0], sh_ref[0]).astype(BF16)
        h_scr[...] = h
        gt_ref[...] = _dot_nt(wgt_ref[...], h)

    o_ref[...] = _dot(h_scr[...], w_ref[...])


def _norm_proj(x, norm_g, mod, cond_of_tile, w_in, n_main):
    n_tok = x.shape[0]
    tm, tn = PROJ_TOKEN_TILE, PROJ_COL_TILE
    n_gate = w_in.shape[1] - n_main
    wgt = w_in[:, n_main:].T
    return pl.pallas_call(
        _norm_proj_kernel,
        grid=(n_tok // tm, n_main // tn),
        in_specs=[pl.BlockSpec((tm, D_MODEL), lambda i, j: (i, 0)),
                  pl.BlockSpec((1, D_MODEL), lambda i, j: (0, 0)),
                  _mod_spec(1, cond_of_tile), _mod_spec(0, cond_of_tile),
                  pl.BlockSpec((D_MODEL, tn), lambda i, j: (0, j)),
                  pl.BlockSpec((n_gate, D_MODEL), lambda i, j: (0, 0))],
        out_specs=[pl.BlockSpec((tm, tn), lambda i, j: (i, j)),
                   pl.BlockSpec((n_gate, tm), lambda i, j: (0, i))],
        out_shape=[jax.ShapeDtypeStruct((n_tok, n_main), F32),
                   jax.ShapeDtypeStruct((n_gate, n_tok), F32)],
        scratch_shapes=[pltpu.VMEM((tm, D_MODEL), BF16)],
        compiler_params=_params("parallel", "arbitrary"),
        name="norm_proj",
    )(x, norm_g.reshape(1, D_MODEL), mod, mod, w_in, wgt)


def _gated_out_kernel(x_ref, o_ref, z_ref, ng_ref, g1_ref, w_ref, y_ref, wb_scr, *, n_heads, gate_fn):
    @pl.when(pl.program_id(0) == 0)
    def _():
        wb_scr[...] = w_ref[...].astype(BF16)

    dv = o_ref.shape[1] // n_heads
    parts = []
    for h in range(n_heads):
        sl = slice(h * dv, (h + 1) * dv)
        parts.append((_rms(o_ref[:, sl], ng_ref[...]) * gate_fn(z_ref[:, sl])).astype(BF16))
    hs = jnp.concatenate(parts, axis=1)
    y = jnp.dot(hs, wb_scr[...], preferred_element_type=F32)
    y_ref[...] = x_ref[...] + g1_ref[0] * y


def _gated_out(x, o, main, z_block, norm_g, mod, cond_of_tile, w_out, n_heads, gate_fn):
    n_tok = x.shape[0]
    tm = OUT_TOKEN_TILE
    width = o.shape[1]
    return pl.pallas_call(
        functools.partial(_gated_out_kernel, n_heads=n_heads, gate_fn=gate_fn),
        grid=(n_tok // tm,),
        in_specs=[pl.BlockSpec((tm, D_MODEL), lambda i: (i, 0)),
                  pl.BlockSpec((tm, width), lambda i: (i, 0)),
                  pl.BlockSpec((tm, width), lambda i: (i, z_block)),
                  pl.BlockSpec((1, width // n_heads), lambda i: (0, 0)),
                  _mod_spec(2, cond_of_tile),
                  pl.BlockSpec((width, D_MODEL), lambda i: (0, 0))],
        out_specs=pl.BlockSpec((tm, D_MODEL), lambda i: (i, 0)),
        out_shape=jax.ShapeDtypeStruct((n_tok, D_MODEL), F32),
        scratch_shapes=[pltpu.VMEM((width, D_MODEL), BF16)],
        compiler_params=_params("arbitrary"),
        name="gated_out",
    )(x, o, main, norm_g.reshape(1, -1), mod, w_out)


def _dft_mats(n):
    k = np.arange(n, dtype=np.int64)
    ang = 2.0 * np.pi * ((k[:, None] * k[None, :]) % n).astype(np.float64) / n
    s = 1.0 / np.sqrt(n)
    return np.cos(ang) * s, np.sin(ang) * s


def _fnet_kernel(x_ref, ng_ref, sc_ref, sh_ref, g1_ref, ct_ref, st_ref, cs_ref, w_ref, b_ref, y_ref, wb_scr):
    @pl.when(pl.program_id(0) == 0)
    def _():
        wb_scr[...] = w_ref[...].astype(BF16)

    x = x_ref[...]
    h = _norm_mod(x, ng_ref[...], sc_ref[0], sh_ref[0]).astype(BF16)
    p = jnp.dot(ct_ref[...], h, preferred_element_type=F32)
    q = jnp.dot(st_ref[...], h, preferred_element_type=F32)
    gd = FNET_GROUP_DIM
    parts = []
    for g in range(D_MODEL // gd):
        sl = slice(g * gd, (g + 1) * gd)
        pq = jnp.concatenate([p[:, sl], q[:, sl]], axis=1).astype(BF16)
        parts.append(jnp.dot(pq, cs_ref[...], preferred_element_type=F32).astype(BF16))
    f = jnp.concatenate(parts, axis=1)
    y = jnp.dot(f, wb_scr[...], preferred_element_type=F32) + b_ref[...]
    y_ref[...] = x + g1_ref[0] * y


def _fnet_mix(x, seq, norm_g, mod, cond_of_seq, w, b):
    n_tok = x.shape[0]
    ct, st = _dft_mats(seq)
    cc, sc = _dft_mats(FNET_GROUP_DIM)
    cs = np.concatenate([cc, -sc], axis=0)
    const = lambda i: (0, 0)
    ct, st, cs = (jnp.asarray(m, F32).astype(BF16) for m in (ct, st, cs))
    return pl.pallas_call(
        _fnet_kernel,
        grid=(n_tok // seq,),
        in_specs=[pl.BlockSpec((seq, D_MODEL), lambda i: (i, 0)),
                  pl.BlockSpec((1, D_MODEL), const),
                  _mod_spec(1, cond_of_seq), _mod_spec(0, cond_of_seq), _mod_spec(2, cond_of_seq),
                  pl.BlockSpec((seq, seq), const),
                  pl.BlockSpec((seq, seq), const),
                  pl.BlockSpec((2 * FNET_GROUP_DIM, FNET_GROUP_DIM), const),
                  pl.BlockSpec((D_MODEL, D_MODEL), const),
                  pl.BlockSpec((1, D_MODEL), const)],
        out_specs=pl.BlockSpec((seq, D_MODEL), lambda i: (i, 0)),
        out_shape=jax.ShapeDtypeStruct((n_tok, D_MODEL), F32),
        scratch_shapes=[pltpu.VMEM((D_MODEL, D_MODEL), BF16)],
        compiler_params=_params("arbitrary"),
        name="fnet",
    )(x, norm_g.reshape(1, D_MODEL), mod, mod, mod,
      ct, st, cs, w, b.reshape(1, D_MODEL))


def _shift_rows(x, delta):
    n = x.shape[0]
    return pltpu.roll(x, (-delta) % n, 0)


def _dwconv_tokens(g, cw_ref, seq, rows):
    tm = g.shape[0]
    width = seq // rows
    assert seq & (seq - 1) == 0 and width & (width - 1) == 0
    t = lax.broadcasted_iota(jnp.int32, (tm, 1), 0)
    pos = t & (seq - 1)
    col = pos & (width - 1)
    row = pos >> (width.bit_length() - 1)
    g_cols = (_shift_rows(g, -1) * (col >= 1).astype(F32), g,
              _shift_rows(g, 1) * (col <= width - 2).astype(F32))
    out = None
    for di in (-1, 0, 1):
        if rows == 1 and di != 0:
            continue
        r = sum(g_cols[dj] * cw_ref[pl.ds(3 * (di + 1) + dj, 1), :] for dj in range(3))
        if di != 0:
            ok = (row + di >= 0) & (row + di <= rows - 1)
            r = _shift_rows(r, di * width) * ok.astype(F32)
        out = r if out is None else out + r
    return out


def _ffn_kernel(x_ref, ng_ref, sc_ref, sh_ref, g2_ref, wa_ref, wg_ref, cw_ref, cb_ref, wd_ref, nf_ref,
                y_ref, h_scr, acc_scr, *, seq, rows, final_norm):
    j = pl.program_id(1)

    @pl.when(j == 0)
    def _():
        h_scr[...] = _norm_mod(x_ref[...], ng_ref[...], sc_ref[0], sh_ref[0]).astype(BF16)
        acc_scr[...] = jnp.zeros_like(acc_scr)

    h = h_scr[...]
    a = jnp.dot(h, wa_ref[...].astype(BF16), preferred_element_type=F32)
    g = jnp.dot(h, wg_ref[...].astype(BF16), preferred_element_type=F32)
    g = _dwconv_tokens(g, cw_ref, seq, rows) + cb_ref[...]
    act = (jax.nn.silu(g) * a).astype(BF16)
    acc_scr[...] += jnp.dot(act, wd_ref[...].astype(BF16), preferred_element_type=F32)

    @pl.when(j == pl.num_programs(1) - 1)
    def _():
        y = x_ref[...] + g2_ref[0] * acc_scr[...]
        if final_norm:
            y = _rms(y, nf_ref[...])
        y_ref[...] = y


def _conv_ffn(x, seq, rows, norm_g, mod, cond_of_tile, w_up, conv_w, conv_b, w_down, norm_final, final_norm):
    n_tok = x.shape[0]
    tm, tf = FFN_TOKEN_TILE, FFN_FF_TILE
    n_ff_tiles = D_FF // tf
    return pl.pallas_call(
        functools.partial(_ffn_kernel, seq=seq, rows=rows, final_norm=final_norm),
        grid=(n_tok // tm, n_ff_tiles),
        in_specs=[pl.BlockSpec((tm, D_MODEL), lambda i, j: (i, 0)),
                  pl.BlockSpec((1, D_MODEL), lambda i, j: (0, 0)),
                  _mod_spec(4, cond_of_tile), _mod_spec(3, cond_of_tile), _mod_spec(5, cond_of_tile),
                  pl.BlockSpec((D_MODEL, tf), lambda i, j: (0, j)),
                  pl.BlockSpec((D_MODEL, tf), lambda i, j: (0, n_ff_tiles + j)),
                  pl.BlockSpec((9, tf), lambda i, j: (0, j)),
                  pl.BlockSpec((1, tf), lambda i, j: (0, j)),
                  pl.BlockSpec((tf, D_MODEL), lambda i, j: (j, 0)),
                  pl.BlockSpec((1, D_MODEL), lambda i, j: (0, 0))],
        out_specs=pl.BlockSpec((tm, D_MODEL), lambda i, j: (i, 0)),
        out_shape=jax.ShapeDtypeStruct((n_tok, D_MODEL), F32),
        scratch_shapes=[pltpu.VMEM((tm, D_MODEL), BF16), pltpu.VMEM((tm, D_MODEL), F32)],
        compiler_params=_params("parallel", "arbitrary"),
        name="conv_ffn",
    )(x, norm_g.reshape(1, D_MODEL), mod, mod, mod, w_up, w_up, conv_w.reshape(9, D_FF),
      conv_b.reshape(1, D_FF), w_down, norm_final.reshape(1, D_MODEL))


GroupMasks = collections.namedtuple("GroupMasks", "r c same incl strict")


def _chunk_masks():
    r = lax.broadcasted_iota(jnp.int32, (CHUNK, CHUNK), 0)
    c = lax.broadcasted_iota(jnp.int32, (CHUNK, CHUNK), 1)
    return (r >= c, r <= c)


def _group_masks():
    r = lax.broadcasted_iota(jnp.int32, (GROUP, GROUP), 0)
    c = lax.broadcasted_iota(jnp.int32, (GROUP, GROUP), 1)
    same = (r >> LOG_CHUNK) == (c >> LOG_CHUNK)
    return GroupMasks(r, c, same, (same & (r >= c), same & (r <= c)), (same & (r > c), same & (r < c)))


def _gate_layouts(gt, batch, n_heads):
    seq = gt.shape[1] // batch
    n_groups = seq // GROUP
    assert seq % GROUP == 0 and n_groups <= 8
    g = jnp.transpose(gt.reshape(N_DIR, 2, n_heads, batch, seq), (3, 2, 0, 1, 4))
    rows = g.reshape(g.shape[:4] + (n_groups, GROUP))
    rows = jnp.pad(rows, ((0, 0),) * 4 + ((0, 8 - n_groups), (0, 0)))
    cols = jnp.swapaxes(g.reshape(g.shape[:4] + (seq // CHUNK, CHUNK)), -1, -2)
    return rows, cols


def _cumsum_rows(row, d, gm):
    return _dot_f32(row, gm.incl[1 - d].astype(F32))


def _cumsum_cols(col, d, cm):
    return _dot_f32(cm[d].astype(F32), col)


def _store_cols(dst, col, n_chunks):
    for n in range(n_chunks):
        dst[n * CHUNK:(n + 1) * CHUNK, :] = jnp.broadcast_to(col[:, n:n + 1], (CHUNK, 128))


def _wide(x):
    return jnp.concatenate([x, x], axis=1)


def _rows(i, size):
    if isinstance(i, int):
        return pl.ds(i * size, size)
    return pl.ds(pl.multiple_of(i * size, size), size)


def _loop(n, body, init):
    if n == 1:
        return body(0, init)
    return lax.fori_loop(0, n, body, init)


def _conv_silu(x, w_ref):
    n = x.shape[0]
    k = w_ref.shape[0]
    t = lax.broadcasted_iota(jnp.int32, (n, 1), 0)
    acc = None
    for j in range(k):
        delta = j - k // 2
        if delta == 0:
            term = x
        else:
            ok = (t + delta >= 0) & (t + delta <= n - 1)
            term = _shift_rows(x, delta) * ok.astype(F32)
        term = term * w_ref[pl.ds(j, 1), :]
        acc = term if acc is None else acc + term
    return jax.nn.silu(acc)


def _l2norm(x):
    return x * lax.rsqrt(jnp.sum(x * x, axis=-1, keepdims=True) + EPS)


def _blockdiag_tri_inverse(mats, gm):
    base = (gm.r >> 3) == (gm.c >> 3)
    eye = jnp.where(gm.r == gm.c, 1.0, 0.0)
    ps = [-jnp.where(base, a, 0.0) for a in mats]
    invs = [eye + p for p in ps]
    for _ in range(2):
        ps = [_dot(p, p) for p in ps]
        invs = [inv + _dot(inv, p) for inv, p in zip(invs, ps)]
    for s in (3, 4, 5):
        join = ((gm.r >> s) ^ (gm.c >> s)) == 1
        tmp = [_dot(jnp.where(join, a, 0.0), inv) for a, inv in zip(mats, invs)]
        invs = [inv - _dot(inv, t) for inv, t in zip(invs, tmp)]
    return invs


def _gdn_kernel(*refs, seq, hp, zero_init, emit_state):
    alog_ref, dtb_ref, q_ref, k_ref, v_ref, cwq_ref, cwk_ref, cwv_ref, grow_ref, gcol_ref = refs[:10]
    pos = 10
    s0_ref = None
    if not zero_init:
        s0_ref = refs[pos]
        pos += 1
    o_ref = refs[pos]
    pos += 1
    sfin_ref = None
    if emit_state:
        sfin_ref = refs[pos]
        pos += 1
    (q_scr, k_scr, v_scr, gcr_scr, gcb_scr, bb_scr, gl_scr,
     u_scr, w_scr, qd_scr, kd_scr, qkd_scr) = refs[pos:]

    n_chunks = seq // CHUNK
    cm = _chunk_masks()
    gm = _group_masks()
    head_dirs = [(hh, d) for hh in range(hp) for d in range(N_DIR)]
    lanes = lambda hh: slice(hh * DN_DK, (hh + 1) * DN_DK)

    q_scr[...] = _conv_silu(q_ref[...], cwq_ref)
    k_scr[...] = _conv_silu(k_ref[...], cwk_ref)
    v_scr[...] = _conv_silu(v_ref[...], cwv_ref)
    for hh in range(hp):
        q_scr[:, lanes(hh)] = _l2norm(q_scr[:, lanes(hh)]) * (DN_DK ** -0.5)
        k_scr[:, lanes(hh)] = _l2norm(k_scr[:, lanes(hh)])

    for hh, d in head_dirs:
        head = pl.program_id(1) * hp + hh
        neg_a = -jnp.exp(jnp.full((1, 1), alog_ref[d, head], F32))
        dtb = dtb_ref[d, head]
        gcr_scr[hh, d] = _cumsum_rows(neg_a * _softplus(grow_ref[0, hh, d, 0] + dtb), d, gm)
        gc_col = _cumsum_cols(neg_a * _softplus(gcol_ref[0, hh, d, 0] + dtb), d, cm)
        _store_cols(gcb_scr.at[hh, d], gc_col, n_chunks)
        _store_cols(bb_scr.at[hh, d], jax.nn.sigmoid(gcol_ref[0, hh, d, 1]), n_chunks)
        last = CHUNK - 1 if d == 0 else 0
        _store_cols(gl_scr.at[hh, d], jnp.broadcast_to(gc_col[last:last + 1, :], gc_col.shape), n_chunks)

    def group_step(gi, carry):
        rows = _rows(gi, GROUP)
        kqs = []
        for hh in range(hp):
            kg, qg = k_scr[rows, lanes(hh)], q_scr[rows, lanes(hh)]
            kqs.append(_dot_nt(jnp.concatenate([kg, qg], axis=0), kg))
        mats = []
        for hh, d in head_dirs:
            gcb = gcb_scr[hh, d, rows, :]
            gcr = gcr_scr[hh, d, pl.ds(gi, 1), :]
            decay = jnp.exp(jnp.where(gm.incl[d], _wide(gcb) - gcr, -jnp.inf))
            mats.append(jnp.where(gm.strict[d], kqs[hh][:GROUP] * _wide(bb_scr[hh, d, rows, :]) * decay, 0.0))
            qkd = kqs[hh][GROUP:] * decay
            qkd_scr[hh, d, rows, :] = sum(qkd[:, n * CHUNK:(n + 1) * CHUNK] for n in range(CHUNKS_PER_GROUP))
        t_invs = _blockdiag_tri_inverse(mats, gm)
        for (hh, d), t_inv in zip(head_dirs, t_invs):
            qg, kg, vg = q_scr[rows, lanes(hh)], k_scr[rows, lanes(hh)], v_scr[rows, lanes(hh)]
            gcb = gcb_scr[hh, d, rows, :]
            bb = bb_scr[hh, d, rows, :]
            e_gc = jnp.exp(gcb)
            uw = _dot(t_inv, jnp.concatenate([vg * bb, kg * bb * e_gc], axis=1))
            u_scr[hh, d, rows, :] = uw[:, :DN_DV]
            w_scr[hh, d, rows, :] = uw[:, DN_DV:]
            qd_scr[hh, d, rows, :] = qg * e_gc
            kd_scr[hh, d, rows, :] = kg * jnp.exp(gl_scr[hh, d, rows, :] - gcb)
        return carry

    _loop(seq // GROUP, group_step, 0)

    o_ref[...] = jnp.zeros_like(o_ref)

    def chunk_step(n, carry):
        idxs = [n if d == 0 else n_chunks - 1 - n for _, d in head_dirs]
        rows = [_rows(idx, CHUNK) for idx in idxs]
        wqs = [_dot(jnp.concatenate([w_scr[hh, d, r, :], qd_scr[hh, d, r, :]], axis=0), s)
               for (hh, d), r, s in zip(head_dirs, rows, carry)]
        v_news = [u_scr[hh, d, r, :] - wq[:CHUNK] for (hh, d), r, wq in zip(head_dirs, rows, wqs)]
        o_ns = [wq[CHUNK:] + _dot(qkd_scr[hh, d, r, :], v_new)
                for (hh, d), r, wq, v_new in zip(head_dirs, rows, wqs, v_news)]
        new = []
        for (hh, d), idx, r, s, v_new in zip(head_dirs, idxs, rows, carry, v_news):
            s_decay = jnp.exp(gl_scr[hh, d, pl.ds(idx * CHUNK, 1), :])
            new.append(s * s_decay + _dot_tn(kd_scr[hh, d, r, :], v_new))
        for (hh, d), r, o_n in zip(head_dirs, rows, o_ns):
            o_ref[r, lanes(hh)] += o_n
        return tuple(new)

    if zero_init:
        init = (jnp.zeros((DN_DK, DN_DV), F32),) * len(head_dirs)
    else:
        init = tuple(s0_ref[0, d, hh] for hh, d in head_dirs)
    fin = lax.fori_loop(0, n_chunks, chunk_step, init)
    if emit_state:
        for i, (hh, d) in enumerate(head_dirs):
            sfin_ref[0, d, hh] = fin[i]


def _gdn_scan(main, gt, batch, seq, conv_w, a_log, dt_bias, s0, emit_state):
    n_tok = batch * seq
    nh = DN_HEADS
    hp = min(nh, max(2, SCAN_STEP_POSITIONS // seq))
    n_hb = nh // hp
    grow, gcol = _gate_layouts(gt, batch, nh)
    zero_init = s0 is None
    smem = pl.BlockSpec(memory_space=pltpu.SMEM)
    qkv_spec = lambda off: pl.BlockSpec((seq, hp * DN_DK), lambda b, h: (b, off + h))
    cw_spec = lambda off: pl.BlockSpec((DN_CONV, hp * DN_DK), lambda b, h: (0, off + h))
    state_spec = pl.BlockSpec((1, N_DIR, hp, DN_DK, DN_DV), lambda b, h: (b, 0, h, 0, 0))
    in_specs = [smem, smem, qkv_spec(0), qkv_spec(n_hb), qkv_spec(2 * n_hb),
                cw_spec(0), cw_spec(n_hb), cw_spec(2 * n_hb),
                pl.BlockSpec((1, hp) + grow.shape[2:], lambda b, h: (b, h, 0, 0, 0, 0)),
                pl.BlockSpec((1, hp) + gcol.shape[2:], lambda b, h: (b, h, 0, 0, 0, 0))]
    args = [a_log, dt_bias, main, main, main, conv_w, conv_w, conv_w, grow, gcol]
    if not zero_init:
        in_specs.append(state_spec)
        args.append(s0)
    out_specs = [pl.BlockSpec((seq, hp * DN_DV), lambda b, h: (b, h))]
    out_shape = [jax.ShapeDtypeStruct((n_tok, nh * DN_DV), F32)]
    if emit_state:
        out_specs.append(state_spec)
        out_shape.append(jax.ShapeDtypeStruct((batch, N_DIR, nh, DN_DK, DN_DV), F32))
    per_dir = lambda width: pltpu.VMEM((hp, N_DIR, seq, width), F32)
    qkv_scr = pltpu.VMEM((seq, hp * DN_DK), F32)
    outs = pl.pallas_call(
        functools.partial(_gdn_kernel, seq=seq, hp=hp, zero_init=zero_init, emit_state=emit_state),
        grid=(batch, n_hb),
        in_specs=in_specs, out_specs=out_specs, out_shape=out_shape,
        scratch_shapes=[qkv_scr, qkv_scr, qkv_scr,
                        pltpu.VMEM((hp, N_DIR, 8, GROUP), F32),
                        per_dir(128), per_dir(128), per_dir(128),
                        per_dir(DN_DV), per_dir(DN_DK), per_dir(DN_DK), per_dir(DN_DK), per_dir(CHUNK)],
        compiler_params=_params("parallel", "parallel"),
        name="gdn_scan",
    )(*args)
    return outs[0], (outs[1] if emit_state else None)


ML_AUG = 2 * ML_DV


def _mlstm_kernel(*refs, seq, hp, zero_init, emit_state):
    bi_ref, bf_ref, q_ref, k_ref, v_ref, grow_ref, gcol_ref = refs[:7]
    pos = 7
    c0_ref = m0_ref = None
    if not zero_init:
        c0_ref, m0_ref = refs[pos:pos + 2]
        pos += 2
    o_ref = refs[pos]
    pos += 1
    cfin_ref = mfin_ref = None
    if emit_state:
        cfin_ref, mfin_ref = refs[pos:pos + 2]
        pos += 2
    bcr_scr, lir_scr, bcb_scr, mi_scr, kwf_scr, dec_scr, dc_scr, cin_scr = refs[pos:]

    n_chunks = seq // CHUNK
    n_groups = seq // GROUP
    head0 = pl.program_id(1) * hp
    cm = _chunk_masks()
    gm = _group_masks()
    ones_col = jnp.where(lax.broadcasted_iota(jnp.int32, (GROUP, ML_DV), 1) == 0, 1.0, 0.0)
    head_dirs = [(hh, d) for hh in range(hp) for d in range(N_DIR)]

    m_fin = {}
    for hh, d in head_dirs:
        b_i = bi_ref[d, head0 + hh]
        b_f = bf_ref[d, head0 + hh]
        lir_scr[hh, d] = grow_ref[0, hh, d, 0] + b_i
        bcr_scr[hh, d] = _cumsum_rows(-_softplus(-(grow_ref[0, hh, d, 1] + b_f)), d, gm)
        li_col = gcol_ref[0, hh, d, 0] + b_i
        bc_col = _cumsum_cols(-_softplus(-(gcol_ref[0, hh, d, 1] + b_f)), d, cm)
        last = CHUNK - 1 if d == 0 else 0
        b_last = bc_col[last:last + 1, :]
        w_col = b_last - bc_col + li_col
        w_max = jnp.max(w_col, axis=0, keepdims=True)
        m = jnp.zeros((1, 1), F32) if zero_init else m0_ref[0, d, hh][:, 0:1]
        for step in range(n_chunks):
            n = step if d == 0 else n_chunks - 1 - step
            sl = slice(n * CHUNK, (n + 1) * CHUNK)
            m_new = jnp.maximum(b_last[:, n:n + 1] + m, w_max[:, n:n + 1])
            bcb_scr[hh, d, sl, :] = jnp.broadcast_to(bc_col[:, n:n + 1], (CHUNK, 128))
            mi_scr[hh, d, sl, :] = jnp.broadcast_to(m, (CHUNK, 128))
            kwf_scr[hh, d, sl, :] = jnp.broadcast_to(jnp.exp(w_col[:, n:n + 1] - m_new), (CHUNK, 128))
            dec_scr[hh, d, n:n + 1, :] = jnp.broadcast_to(jnp.exp(b_last[:, n:n + 1] + m - m_new), (1, 128))
            m = m_new
        m_fin[hh, d] = m

    def load_kv(rows, hh):
        kg = k_ref[rows, hh * ML_DQK:(hh + 1) * ML_DQK]
        v_aug = jnp.concatenate([v_ref[rows, hh * ML_DV:(hh + 1) * ML_DV], ones_col], axis=1)
        return kg, v_aug

    def block_diag(x):
        return jnp.where(gm.same, jnp.concatenate([x] * CHUNKS_PER_GROUP, axis=1), 0.0)

    def delta_step(gi, carry):
        rows = _rows(gi, GROUP)
        kvs = [load_kv(rows, hh) for hh in range(hp)]
        kws = [block_diag(kvs[hh][0] * kwf_scr[hh, d, rows, :][:, :ML_DQK]) for hh, d in head_dirs]
        for (hh, d), kw in zip(head_dirs, kws):
            dc_scr[hh, d, rows, :] = _dot_tn(kw, kvs[hh][1])
        return carry

    _loop(n_groups, delta_step, 0)

    def prefix_step(n, carry):
        new = []
        for (hh, d), c_aug in zip(head_dirs, carry):
            idx = n if d == 0 else n_chunks - 1 - n
            rows = _rows(idx, CHUNK)
            cin_scr[hh, d, rows, :] = c_aug
            new.append(c_aug * _wide(dec_scr[hh, d, pl.ds(idx, 1), :]) + dc_scr[hh, d, rows, :])
        return tuple(new)

    if zero_init:
        init = (jnp.zeros((ML_DQK, ML_AUG), F32),) * len(head_dirs)
    else:
        init = tuple(c0_ref[0, d, hh] for hh, d in head_dirs)
    fin = lax.fori_loop(0, n_chunks, prefix_step, init)

    def out_step(gi, carry):
        rows = _rows(gi, GROUP)
        kvs = [load_kv(rows, hh) for hh in range(hp)]
        qgs = [q_ref[rows, hh * ML_DQK:(hh + 1) * ML_DQK] * (ML_DQK ** -0.5) for hh in range(hp)]
        qks = [_dot_nt(qg, kv[0]) for qg, kv in zip(qgs, kvs)]
        q_bds = [block_diag(qg) for qg in qgs]
        inters = [_dot(q_bds[hh], cin_scr[hh, d, rows, :]) for hh, d in head_dirs]
        m_ts, b_ms, ss = [], [], []
        for hh, d in head_dirs:
            b_colb = bcb_scr[hh, d, rows, :]
            b_m = b_colb[:, 0:1] + mi_scr[hh, d, rows, :][:, 0:1]
            d_log = jnp.where(gm.incl[d], _wide(b_colb) - bcr_scr[hh, d, pl.ds(gi, 1), :]
                              + lir_scr[hh, d, pl.ds(gi, 1), :], -jnp.inf)
            m_t = jnp.maximum(b_m, jnp.max(d_log, axis=-1, keepdims=True))
            m_ts.append(m_t)
            b_ms.append(b_m)
            ss.append(qks[hh] * jnp.exp(d_log - m_t))
        intras = [_dot(s, kvs[hh][1]) for (hh, d), s in zip(head_dirs, ss)]
        hs = []
        for inter, intra, m_t, b_m in zip(inters, intras, m_ts, b_ms):
            num = jnp.exp(b_m - m_t) * inter + intra
            den = num[:, ML_DV:ML_DV + 1]
            hs.append(num[:, :ML_DV] / jnp.maximum(jnp.abs(den), jnp.exp(-m_t)))
        for hh in range(hp):
            o_ref[rows, hh * ML_DV:(hh + 1) * ML_DV] = hs[N_DIR * hh] + hs[N_DIR * hh + 1]
        return carry

    _loop(n_groups, out_step, 0)

    if emit_state:
        for i, (hh, d) in enumerate(head_dirs):
            cfin_ref[0, d, hh] = fin[i]
            mfin_ref[0, d, hh] = jnp.broadcast_to(m_fin[hh, d], (1, 128))


def _mlstm_scan(main, gt, batch, seq, b_i, b_f, state0, emit_state):
    n_tok = batch * seq
    n_chunks = seq // CHUNK
    nh = ML_HEADS
    hp = min(nh, max(2, SCAN_STEP_POSITIONS // seq))
    n_hb = nh // hp
    grow, gcol = _gate_layouts(gt, batch, nh)
    zero_init = state0 is None
    smem = pl.BlockSpec(memory_space=pltpu.SMEM)
    k_off = (nh * ML_DQK) // (hp * ML_DQK)
    v_off = (2 * nh * ML_DQK) // (hp * ML_DV)
    c_spec = pl.BlockSpec((1, N_DIR, hp, ML_DQK, ML_AUG), lambda b, p: (b, 0, p, 0, 0))
    m_spec = pl.BlockSpec((1, N_DIR, hp, 1, 128), lambda b, p: (b, 0, p, 0, 0))
    in_specs = [smem, smem,
                pl.BlockSpec((seq, hp * ML_DQK), lambda b, p: (b, p)),
                pl.BlockSpec((seq, hp * ML_DQK), lambda b, p: (b, k_off + p)),
                pl.BlockSpec((seq, hp * ML_DV), lambda b, p: (b, v_off + p)),
                pl.BlockSpec((1, hp) + grow.shape[2:], lambda b, p: (b, p, 0, 0, 0, 0)),
                pl.BlockSpec((1, hp) + gcol.shape[2:], lambda b, p: (b, p, 0, 0, 0, 0))]
    args = [b_i, b_f, main, main, main, grow, gcol]
    if not zero_init:
        in_specs += [c_spec, m_spec]
        args += list(state0)
    out_specs = [pl.BlockSpec((seq, hp * ML_DV), lambda b, p: (b, p))]
    out_shape = [jax.ShapeDtypeStruct((n_tok, nh * ML_DV), F32)]
    if emit_state:
        out_specs += [c_spec, m_spec]
        out_shape += [jax.ShapeDtypeStruct((batch, N_DIR, nh, ML_DQK, ML_AUG), F32),
                      jax.ShapeDtypeStruct((batch, N_DIR, nh, 1, 128), F32)]
    per_hd = lambda rows, width: pltpu.VMEM((hp, N_DIR, rows, width), F32)
    outs = pl.pallas_call(
        functools.partial(_mlstm_kernel, seq=seq, hp=hp, zero_init=zero_init, emit_state=emit_state),
        grid=(batch, n_hb),
        in_specs=in_specs, out_specs=out_specs, out_shape=out_shape,
        scratch_shapes=[per_hd(8, GROUP), per_hd(8, GROUP),
                        per_hd(seq, 128), per_hd(seq, 128), per_hd(seq, 128),
                        per_hd(max(n_chunks, 8), 128),
                        per_hd(seq, ML_AUG), per_hd(seq, ML_AUG)],
        compiler_params=_params("parallel", "parallel"),
        name="mlstm_scan",
    )(*args)
    return outs[0], ((outs[1], outs[2]) if emit_state else None)


def _trunk(x, batch, seq, rows, mods, tokens_per_cond, st_d, st_ml, emit_state, p):
    depth = p["w_ada"].shape[0]
    tile_cond = lambda tile: (lambda i: (i * tile) // tokens_per_cond)
    new_d = new_ml = None
    for layer in range(depth):
        mod = mods[layer]
        kind, j = layer % 3, layer // 3
        if kind == 0:
            x = _fnet_mix(x, seq, p["norm_mix"][layer], mod, tile_cond(seq), p["fnet_w"][j], p["fnet_b"][j])
        elif kind == 1:
            n_main = DN_HEADS * (2 * DN_DK + 2 * DN_DV)
            main, gt = _norm_proj(x, p["norm_mix"][layer], mod, tile_cond(PROJ_TOKEN_TILE), p["dn_w_in"][j], n_main)
            s0 = None if st_d is None else st_d[:, j]
            o, sfin = _gdn_scan(main, gt, batch, seq, p["dn_conv_w"][j], p["dn_a_log"][j], p["dn_dt_bias"][j],
                                s0, emit_state)
            if emit_state:
                new_d = sfin
            x = _gated_out(x, o, main, 3, p["dn_norm"][j], mod, tile_cond(OUT_TOKEN_TILE), p["dn_w_out"][j],
                           DN_HEADS, jax.nn.silu)
        else:
            n_main = 2 * ML_HEADS * ML_DQK + 2 * ML_HEADS * ML_DV
            main, gt = _norm_proj(x, p["norm_mix"][layer], mod, tile_cond(PROJ_TOKEN_TILE), p["ml_w_in"][j], n_main)
            state0 = None
            if st_ml is not None:
                c0, n0, m0 = (s[:, j] for s in st_ml)
                pad = jnp.zeros(c0.shape[:-1] + (ML_AUG - ML_DV - 1,), F32)
                c_aug0 = jnp.concatenate([c0, n0[..., None], pad], axis=-1)
                m0b = jnp.broadcast_to(m0[..., None, None], m0.shape + (1, 128))
                state0 = (c_aug0, m0b)
            o, fin = _mlstm_scan(main, gt, batch, seq, p["ml_b_i"][j], p["ml_b_f"][j], state0, emit_state)
            if emit_state:
                new_ml = (fin[0][..., :ML_DV], fin[0][..., ML_DV], fin[1][..., 0, 0])
            x = _gated_out(x, o, main, 2, p["ml_norm"][j], mod, tile_cond(OUT_TOKEN_TILE), p["ml_w_out"][j],
                           ML_HEADS, jax.nn.sigmoid)
        x = _conv_ffn(x, seq, rows, p["norm_ffn"][layer], mod, tile_cond(FFN_TOKEN_TILE), p["ffn_w_up"][layer],
                      p["ffn_conv_w"][layer], p["ffn_conv_b"][layer], p["ffn_w_down"][layer], p["norm_final"],
                      final_norm=(layer == depth - 1))
    return x, new_d, new_ml


def kernel(x_prompt, x_sample, state_delta, state_mlstm_c, state_mlstm_n, state_mlstm_m, c, c_ctx, w_ada, b_ada, norm_mix, norm_ffn, norm_final, ffn_w_up, ffn_conv_w, ffn_conv_b, ffn_w_down, fnet_w, fnet_b, dn_w_in, dn_conv_w, dn_a_log, dn_dt_bias, dn_norm, dn_w_out, ml_w_in, ml_b_i, ml_b_f, ml_norm, ml_w_out):
    p = dict(w_ada=w_ada, norm_mix=norm_mix, norm_ffn=norm_ffn, norm_final=norm_final, ffn_w_up=ffn_w_up,
             ffn_conv_w=ffn_conv_w, ffn_conv_b=ffn_conv_b, ffn_w_down=ffn_w_down, fnet_w=fnet_w, fnet_b=fnet_b,
             dn_w_in=dn_w_in, dn_conv_w=dn_conv_w, dn_a_log=dn_a_log, dn_dt_bias=dn_dt_bias, dn_norm=dn_norm,
             dn_w_out=dn_w_out, ml_w_in=ml_w_in, ml_b_i=ml_b_i, ml_b_f=ml_b_f, ml_norm=ml_norm, ml_w_out=ml_w_out)
    b_ctx, t_ctx, _ = x_prompt.shape
    b_smp, t_smp, _ = x_sample.shape
    depth = w_ada.shape[0]

    cond8 = jnp.concatenate([c_ctx[None, :], c, jnp.zeros((8 - 1 - b_smp, D_MODEL), F32)], axis=0)
    mods = _adaln_all(cond8, w_ada, b_ada)
    mods_ctx = [mods[l, 0:1].reshape(1, 1, -1) for l in range(depth)]
    mods_smp = [mods[l, 1:1 + b_smp].reshape(b_smp, 1, -1) for l in range(depth)]

    y_ctx, new_d, new_ml = _trunk(x_prompt.reshape(b_ctx * t_ctx, D_MODEL), b_ctx, t_ctx, 1, mods_ctx,
                                  b_ctx * t_ctx, None, None, True, p)
    y_smp, _, _ = _trunk(x_sample.reshape(b_smp * t_smp, D_MODEL), b_smp, t_smp, t_smp // GRID_W, mods_smp,
                         t_smp, state_delta, (state_mlstm_c, state_mlstm_n, state_mlstm_m), False, p)

    new_c, new_n, new_m = new_ml
    return (y_ctx.reshape(b_ctx, t_ctx, D_MODEL), y_smp.reshape(b_smp, t_smp, D_MODEL),
            new_d[:, None], new_c[:, None], new_n[:, None], new_m[:, None])
```

```python
import collections
import functools

import numpy as np
import jax
import jax.numpy as jnp
from jax import lax
from jax.experimental import pallas as pl
from jax.experimental.pallas import tpu as pltpu

F32 = jnp.float32
BF16 = jnp.bfloat16
HIGHEST = lax.Precision.HIGHEST

D_MODEL = 1024
EPS = 1e-6
N_DIR = 2
CHUNK = 64
LOG_CHUNK = 6
GROUP = 256
CHUNKS_PER_GROUP = GROUP // CHUNK
FNET_GROUP_DIM = 256
DN_HEADS, DN_DK, DN_DV, DN_CONV = 8, 128, 128, 5
ML_HEADS, ML_DQK, ML_DV = 8, 64, 128
D_FF = 2816
GRID_W = 64

FFN_TOKEN_TILE = 1024
FFN_FF_TILE = 256
PROJ_TOKEN_TILE = 2048
PROJ_COL_TILE = 512
OUT_TOKEN_TILE = 512
ADA_COL_TILE = 1536
SCAN_STEP_POSITIONS = 1024
VMEM_LIMIT = 56 * 1024 * 1024


def _params(*sem):
    return pltpu.CompilerParams(dimension_semantics=sem, vmem_limit_bytes=VMEM_LIMIT)


def _dot(a, b):
    return jnp.dot(a.astype(BF16), b.astype(BF16), preferred_element_type=F32)


def _dot_nt(a, b):
    return lax.dot_general(a.astype(BF16), b.astype(BF16), (((1,), (1,)), ((), ())),
                           preferred_element_type=F32)


def _dot_tn(a, b):
    return lax.dot_general(a.astype(BF16), b.astype(BF16), (((0,), (0,)), ((), ())),
                           preferred_element_type=F32)


def _dot_f32(a, b):
    return jnp.dot(a, b, precision=HIGHEST, preferred_element_type=F32)


def _rms(x, g):
    return x * lax.rsqrt(jnp.mean(x * x, axis=-1, keepdims=True) + EPS) * g


def _norm_mod(x, g, sc, sh):
    return _rms(x, g) * (1.0 + sc) + sh


def _softplus(x):
    return jnp.maximum(x, 0.0) + jnp.log1p(jnp.exp(-jnp.abs(x)))


def _mod_spec(chunk, cond_of_tile):
    return pl.BlockSpec((1, 1, D_MODEL), lambda i, *_: (cond_of_tile(i), 0, chunk))


def _adaln_kernel(c_ref, w_ref, b_ref, o_ref):
    s = jax.nn.silu(c_ref[...])
    o_ref[0] = _dot(s, w_ref[0]) + b_ref[0]


def _adaln_all(cond8, w_ada, b_ada):
    depth = w_ada.shape[0]
    n_out = w_ada.shape[2]
    tn = ADA_COL_TILE
    return pl.pallas_call(
        _adaln_kernel,
        grid=(depth, n_out // tn),
        in_specs=[pl.BlockSpec((8, D_MODEL), lambda l, j: (0, 0)),
                  pl.BlockSpec((1, D_MODEL, tn), lambda l, j: (l, 0, j)),
                  pl.BlockSpec((1, 1, tn), lambda l, j: (l, 0, j))],
        out_specs=pl.BlockSpec((1, 8, tn), lambda l, j: (l, 0, j)),
        out_shape=jax.ShapeDtypeStruct((depth, 8, n_out), F32),
        compiler_params=_params("parallel", "parallel"),
        name="adaln",
    )(cond8, w_ada, b_ada.reshape(depth, 1, n_out))


def _norm_proj_kernel(x_ref, g_ref, sc_ref, sh_ref, w_ref, wgt_ref, o_ref, gt_ref, h_scr):
    @pl.when(pl.program_id(1) == 0)
    def _():
        h = _norm_mod(x_ref[...], g_ref[...], sc_ref[0], sh_ref[0]).astype(BF16)
        h_scr[...] = h
        gt_ref[...] = _dot_nt(wgt_ref[...], h)

    o_ref[...] = _dot(h_scr[...], w_ref[...]).astype(o_ref.dtype)


def _norm_proj(x, norm_g, mod, tokens_per_cond, w_in, n_main):
    n_tok = x.shape[0]
    tm, tn = min(PROJ_TOKEN_TILE, tokens_per_cond), PROJ_COL_TILE
    cond_of_tile = lambda i: (i * tm) // tokens_per_cond
    n_gate = w_in.shape[1] - n_main
    wgt = w_in[:, n_main:].T
    return pl.pallas_call(
        _norm_proj_kernel,
        grid=(n_tok // tm, n_main // tn),
        in_specs=[pl.BlockSpec((tm, D_MODEL), lambda i, j: (i, 0)),
                  pl.BlockSpec((1, D_MODEL), lambda i, j: (0, 0)),
                  _mod_spec(1, cond_of_tile), _mod_spec(0, cond_of_tile),
                  pl.BlockSpec((D_MODEL, tn), lambda i, j: (0, j)),
                  pl.BlockSpec((n_gate, D_MODEL), lambda i, j: (0, 0))],
        out_specs=[pl.BlockSpec((tm, tn), lambda i, j: (i, j)),
                   pl.BlockSpec((n_gate, tm), lambda i, j: (0, i))],
        out_shape=[jax.ShapeDtypeStruct((n_tok, n_main), BF16),
                   jax.ShapeDtypeStruct((n_gate, n_tok), F32)],
        scratch_shapes=[pltpu.VMEM((tm, D_MODEL), BF16)],
        compiler_params=_params("parallel", "arbitrary"),
        name="norm_proj",
    )(x, norm_g.reshape(1, D_MODEL), mod, mod, w_in, wgt)


def _gated_out_kernel(x_ref, o_ref, z_ref, ng_ref, g1_ref, w_ref, y_ref, wb_scr, *, n_heads, gate_fn):
    @pl.when(pl.program_id(0) == 0)
    def _():
        wb_scr[...] = w_ref[...].astype(BF16)

    dv = o_ref.shape[1] // n_heads
    parts = []
    for h in range(n_heads):
        sl = slice(h * dv, (h + 1) * dv)
        parts.append((_rms(o_ref[:, sl], ng_ref[...]) * gate_fn(z_ref[:, sl].astype(F32))).astype(BF16))
    hs = jnp.concatenate(parts, axis=1)
    y = jnp.dot(hs, wb_scr[...], preferred_element_type=F32)
    y_ref[...] = x_ref[...] + g1_ref[0] * y


def _gated_out(x, o, main, z_block, norm_g, mod, cond_of_tile, w_out, n_heads, gate_fn):
    n_tok = x.shape[0]
    tm = OUT_TOKEN_TILE
    width = o.shape[1]
    return pl.pallas_call(
        functools.partial(_gated_out_kernel, n_heads=n_heads, gate_fn=gate_fn),
        grid=(n_tok // tm,),
        in_specs=[pl.BlockSpec((tm, D_MODEL), lambda i: (i, 0)),
                  pl.BlockSpec((tm, width), lambda i: (i, 0)),
                  pl.BlockSpec((tm, width), lambda i: (i, z_block)),
                  pl.BlockSpec((1, width // n_heads), lambda i: (0, 0)),
                  _mod_spec(2, cond_of_tile),
                  pl.BlockSpec((width, D_MODEL), lambda i: (0, 0))],
        out_specs=pl.BlockSpec((tm, D_MODEL), lambda i: (i, 0)),
        out_shape=jax.ShapeDtypeStruct((n_tok, D_MODEL), F32),
        scratch_shapes=[pltpu.VMEM((width, D_MODEL), BF16)],
        compiler_params=_params("arbitrary"),
        name="gated_out",
    )(x, o, main, norm_g.reshape(1, -1), mod, w_out)


def _dft_mats(n):
    k = np.arange(n, dtype=np.int64)
    ang = 2.0 * np.pi * ((k[:, None] * k[None, :]) % n).astype(np.float64) / n
    s = 1.0 / np.sqrt(n)
    return np.cos(ang) * s, np.sin(ang) * s


def _fnet_kernel(x_ref, ng_ref, sc_ref, sh_ref, g1_ref, ct_ref, st_ref, cs_ref, w_ref, b_ref, y_ref, wb_scr):
    @pl.when(pl.program_id(0) == 0)
    def _():
        wb_scr[...] = w_ref[...].astype(BF16)

    x = x_ref[...]
    h = _norm_mod(x, ng_ref[...], sc_ref[0], sh_ref[0]).astype(BF16)
    p = jnp.dot(ct_ref[...], h, preferred_element_type=F32)
    q = jnp.dot(st_ref[...], h, preferred_element_type=F32)
    gd = FNET_GROUP_DIM
    parts = []
    for g in range(D_MODEL // gd):
        sl = slice(g * gd, (g + 1) * gd)
        pq = jnp.concatenate([p[:, sl], q[:, sl]], axis=1).astype(BF16)
        parts.append(jnp.dot(pq, cs_ref[...], preferred_element_type=F32).astype(BF16))
    f = jnp.concatenate(parts, axis=1)
    y = jnp.dot(f, wb_scr[...], preferred_element_type=F32) + b_ref[...]
    y_ref[...] = x + g1_ref[0] * y


def _fnet_mix(x, seq, layer, norm_g, mod, cond_of_seq, w, b):
    n_tok = x.shape[0]
    ct, st = _dft_mats(seq)
    cc, sc = _dft_mats(FNET_GROUP_DIM)
    cs = np.concatenate([cc, -sc], axis=0)
    const = lambda i: (0, 0)
    ct, st, cs = (jnp.asarray(m, F32).astype(BF16) for m in (ct, st, cs))
    return pl.pallas_call(
        _fnet_kernel,
        grid=(n_tok // seq,),
        in_specs=[pl.BlockSpec((seq, D_MODEL), lambda i: (i, 0)),
                  pl.BlockSpec((1, D_MODEL), const),
                  _mod_spec(1, cond_of_seq), _mod_spec(0, cond_of_seq), _mod_spec(2, cond_of_seq),
                  pl.BlockSpec((seq, seq), const),
                  pl.BlockSpec((seq, seq), const),
                  pl.BlockSpec((2 * FNET_GROUP_DIM, FNET_GROUP_DIM), const),
                  pl.BlockSpec((None, D_MODEL, D_MODEL), lambda i: (layer, 0, 0)),
                  pl.BlockSpec((1, D_MODEL), const)],
        out_specs=pl.BlockSpec((seq, D_MODEL), lambda i: (i, 0)),
        out_shape=jax.ShapeDtypeStruct((n_tok, D_MODEL), F32),
        scratch_shapes=[pltpu.VMEM((D_MODEL, D_MODEL), BF16)],
        compiler_params=_params("arbitrary"),
        name="fnet",
    )(x, norm_g.reshape(1, D_MODEL), mod, mod, mod,
      ct, st, cs, w, b.reshape(1, D_MODEL))


def _shift_rows(x, delta):
    n = x.shape[0]
    return pltpu.roll(x, (-delta) % n, 0)


def _dwconv_tokens(g, cw_ref, seq, rows):
    tm = g.shape[0]
    width = seq // rows
    assert seq & (seq - 1) == 0 and width & (width - 1) == 0
    t = lax.broadcasted_iota(jnp.int32, (tm, 1), 0)
    pos = t & (seq - 1)
    col = pos & (width - 1)
    row = pos >> (width.bit_length() - 1)
    g_cols = (_shift_rows(g, -1) * (col >= 1).astype(F32), g,
              _shift_rows(g, 1) * (col <= width - 2).astype(F32))
    out = None
    for di in (-1, 0, 1):
        if rows == 1 and di != 0:
            continue
        r = sum(g_cols[dj] * cw_ref[pl.ds(3 * (di + 1) + dj, 1), :] for dj in range(3))
        if di != 0:
            ok = (row + di >= 0) & (row + di <= rows - 1)
            r = _shift_rows(r, di * width) * ok.astype(F32)
        out = r if out is None else out + r
    return out


def _ffn_kernel(x_ref, ng_ref, sc_ref, sh_ref, g2_ref, wa_ref, wg_ref, cw_ref, cb_ref, wd_ref, nf_ref,
                y_ref, h_scr, acc_scr, *, seq, rows, final_norm):
    j = pl.program_id(1)

    @pl.when(j == 0)
    def _():
        h_scr[...] = _norm_mod(x_ref[...], ng_ref[...], sc_ref[0], sh_ref[0]).astype(BF16)
        acc_scr[...] = jnp.zeros_like(acc_scr)

    h = h_scr[...]
    a = jnp.dot(h, wa_ref[...].astype(BF16), preferred_element_type=F32)
    g = jnp.dot(h, wg_ref[...].astype(BF16), preferred_element_type=F32)
    g = _dwconv_tokens(g, cw_ref, seq, rows) + cb_ref[...]
    act = (jax.nn.silu(g) * a).astype(BF16)
    acc_scr[...] += jnp.dot(act, wd_ref[...].astype(BF16), preferred_element_type=F32)

    @pl.when(j == pl.num_programs(1) - 1)
    def _():
        y = x_ref[...] + g2_ref[0] * acc_scr[...]
        if final_norm:
            y = _rms(y, nf_ref[...])
        y_ref[...] = y


def _conv_ffn(x, seq, rows, layer, norm_g, mod, cond_of_tile, w_up, conv_w, conv_b, w_down, norm_final, final_norm):
    n_tok = x.shape[0]
    tm, tf = FFN_TOKEN_TILE, FFN_FF_TILE
    n_ff_tiles = D_FF // tf
    depth = w_up.shape[0]
    return pl.pallas_call(
        functools.partial(_ffn_kernel, seq=seq, rows=rows, final_norm=final_norm),
        grid=(n_tok // tm, n_ff_tiles),
        in_specs=[pl.BlockSpec((tm, D_MODEL), lambda i, j: (i, 0)),
                  pl.BlockSpec((1, D_MODEL), lambda i, j: (0, 0)),
                  _mod_spec(4, cond_of_tile), _mod_spec(3, cond_of_tile), _mod_spec(5, cond_of_tile),
                  pl.BlockSpec((None, D_MODEL, tf), lambda i, j: (layer, 0, j)),
                  pl.BlockSpec((None, D_MODEL, tf), lambda i, j: (layer, 0, n_ff_tiles + j)),
                  pl.BlockSpec((None, 9, tf), lambda i, j: (layer, 0, j)),
                  pl.BlockSpec((None, 1, tf), lambda i, j: (layer, 0, j)),
                  pl.BlockSpec((None, tf, D_MODEL), lambda i, j: (layer, j, 0)),
                  pl.BlockSpec((1, D_MODEL), lambda i, j: (0, 0))],
        out_specs=pl.BlockSpec((tm, D_MODEL), lambda i, j: (i, 0)),
        out_shape=jax.ShapeDtypeStruct((n_tok, D_MODEL), F32),
        scratch_shapes=[pltpu.VMEM((tm, D_MODEL), BF16), pltpu.VMEM((tm, D_MODEL), F32)],
        compiler_params=_params("parallel", "arbitrary"),
        name="conv_ffn",
    )(x, norm_g.reshape(1, D_MODEL), mod, mod, mod, w_up, w_up, conv_w.reshape(depth, 9, D_FF),
      conv_b.reshape(depth, 1, D_FF), w_down, norm_final.reshape(1, D_MODEL))


GroupMasks = collections.namedtuple("GroupMasks", "r c same incl strict")


def _chunk_masks():
    r = lax.broadcasted_iota(jnp.int32, (CHUNK, CHUNK), 0)
    c = lax.broadcasted_iota(jnp.int32, (CHUNK, CHUNK), 1)
    return (r >= c, r <= c)


def _group_masks():
    r = lax.broadcasted_iota(jnp.int32, (GROUP, GROUP), 0)
    c = lax.broadcasted_iota(jnp.int32, (GROUP, GROUP), 1)
    same = (r >> LOG_CHUNK) == (c >> LOG_CHUNK)
    return GroupMasks(r, c, same, (same & (r >= c), same & (r <= c)), (same & (r > c), same & (r < c)))


def _gate_layouts(gt, batch, n_heads):
    seq = gt.shape[1] // batch
    n_groups = seq // GROUP
    assert seq % GROUP == 0 and n_groups <= 8
    g = jnp.transpose(gt.reshape(N_DIR, 2, n_heads, batch, seq), (3, 2, 0, 1, 4))
    rows = g.reshape(g.shape[:4] + (n_groups, GROUP))
    rows = jnp.pad(rows, ((0, 0),) * 4 + ((0, 8 - n_groups), (0, 0)))
    cols = jnp.swapaxes(g.reshape(g.shape[:4] + (seq // CHUNK, CHUNK)), -1, -2)
    return rows, cols


def _cumsum_rows(row, d, gm):
    return _dot_f32(row, gm.incl[1 - d].astype(F32))


def _cumsum_cols(col, d, cm):
    return _dot_f32(cm[d].astype(F32), col)


def _store_cols(dst, col, n_chunks):
    for n in range(n_chunks):
        dst[n * CHUNK:(n + 1) * CHUNK, :] = jnp.broadcast_to(col[:, n:n + 1], (CHUNK, 128))


def _wide(x):
    return jnp.concatenate([x, x], axis=1)


def _rows(i, size):
    if isinstance(i, int):
        return pl.ds(i * size, size)
    return pl.ds(pl.multiple_of(i * size, size), size)


def _loop(n, body, init):
    if n == 1:
        return body(0, init)
    return lax.fori_loop(0, n, body, init)


def _conv_silu(x, w_ref):
    n = x.shape[0]
    k = w_ref.shape[0]
    t = lax.broadcasted_iota(jnp.int32, (n, 1), 0)
    acc = None
    for j in range(k):
        delta = j - k // 2
        if delta == 0:
            term = x
        else:
            ok = (t + delta >= 0) & (t + delta <= n - 1)
            term = _shift_rows(x, delta) * ok.astype(F32)
        term = term * w_ref[pl.ds(j, 1), :]
        acc = term if acc is None else acc + term
    return jax.nn.silu(acc)


def _l2norm(x):
    return x * lax.rsqrt(jnp.sum(x * x, axis=-1, keepdims=True) + EPS)


def _blockdiag_tri_inverse(mats, gm):
    base = (gm.r >> 3) == (gm.c >> 3)
    eye = jnp.where(gm.r == gm.c, 1.0, 0.0)
    ps = [-jnp.where(base, a, 0.0) for a in mats]
    invs = [eye + p for p in ps]
    for _ in range(2):
        ps = [_dot(p, p) for p in ps]
        invs = [inv + _dot(inv, p) for inv, p in zip(invs, ps)]
    for s in (3, 4, 5):
        join = ((gm.r >> s) ^ (gm.c >> s)) == 1
        tmp = [_dot(jnp.where(join, a, 0.0), inv) for a, inv in zip(mats, invs)]
        invs = [inv - _dot(inv, t) for inv, t in zip(invs, tmp)]
    return invs


def _gdn_kernel(*refs, seq, hp, zero_init, emit_state):
    alog_ref, dtb_ref, q_ref, k_ref, v_ref, cwq_ref, cwk_ref, cwv_ref, grow_ref, gcol_ref = refs[:10]
    pos = 10
    s0_ref = None
    if not zero_init:
        s0_ref = refs[pos]
        pos += 1
    o_ref = refs[pos]
    pos += 1
    sfin_ref = None
    if emit_state:
        sfin_ref = refs[pos]
        pos += 1
    (q_scr, k_scr, v_scr, gcr_scr, gcb_scr, bb_scr, gl_scr,
     u_scr, w_scr, qd_scr, kd_scr, qkd_scr) = refs[pos:]

    n_chunks = seq // CHUNK
    cm = _chunk_masks()
    gm = _group_masks()
    head_dirs = [(hh, d) for hh in range(hp) for d in range(N_DIR)]
    lanes = lambda hh: slice(hh * DN_DK, (hh + 1) * DN_DK)

    q_scr[...] = _conv_silu(q_ref[...].astype(F32), cwq_ref)
    k_scr[...] = _conv_silu(k_ref[...].astype(F32), cwk_ref)
    v_scr[...] = _conv_silu(v_ref[...].astype(F32), cwv_ref)
    for hh in range(hp):
        q_scr[:, lanes(hh)] = _l2norm(q_scr[:, lanes(hh)]) * (DN_DK ** -0.5)
        k_scr[:, lanes(hh)] = _l2norm(k_scr[:, lanes(hh)])

    for hh, d in head_dirs:
        head = pl.program_id(1) * hp + hh
        neg_a = -jnp.exp(jnp.full((1, 1), alog_ref[d, head], F32))
        dtb = dtb_ref[d, head]
        gcr_scr[hh, d] = _cumsum_rows(neg_a * _softplus(grow_ref[0, hh, d, 0] + dtb), d, gm)
        gc_col = _cumsum_cols(neg_a * _softplus(gcol_ref[0, hh, d, 0] + dtb), d, cm)
        _store_cols(gcb_scr.at[hh, d], gc_col, n_chunks)
        _store_cols(bb_scr.at[hh, d], jax.nn.sigmoid(gcol_ref[0, hh, d, 1]), n_chunks)
        last = CHUNK - 1 if d == 0 else 0
        _store_cols(gl_scr.at[hh, d], jnp.broadcast_to(gc_col[last:last + 1, :], gc_col.shape), n_chunks)

    def group_step(gi, carry):
        rows = _rows(gi, GROUP)
        kqs = []
        for hh in range(hp):
            kg, qg = k_scr[rows, lanes(hh)], q_scr[rows, lanes(hh)]
            kqs.append(_dot_nt(jnp.concatenate([kg, qg], axis=0), kg))
        mats = []
        for hh, d in head_dirs:
            gcb = gcb_scr[hh, d, rows, :]
            gcr = gcr_scr[hh, d, pl.ds(gi, 1), :]
            decay = jnp.exp(jnp.where(gm.incl[d], _wide(gcb) - gcr, -jnp.inf))
            mats.append(jnp.where(gm.strict[d], kqs[hh][:GROUP] * _wide(bb_scr[hh, d, rows, :]) * decay, 0.0))
            qkd = kqs[hh][GROUP:] * decay
            qkd_scr[hh, d, rows, :] = sum(qkd[:, n * CHUNK:(n + 1) * CHUNK] for n in range(CHUNKS_PER_GROUP))
        t_invs = _blockdiag_tri_inverse(mats, gm)
        for (hh, d), t_inv in zip(head_dirs, t_invs):
            qg, kg, vg = q_scr[rows, lanes(hh)], k_scr[rows, lanes(hh)], v_scr[rows, lanes(hh)]
            gcb = gcb_scr[hh, d, rows, :]
            bb = bb_scr[hh, d, rows, :]
            e_gc = jnp.exp(gcb)
            uw = _dot(t_inv, jnp.concatenate([vg * bb, kg * bb * e_gc], axis=1))
            u_scr[hh, d, rows, :] = uw[:, :DN_DV]
            w_scr[hh, d, rows, :] = uw[:, DN_DV:]
            qd_scr[hh, d, rows, :] = qg * e_gc
            kd_scr[hh, d, rows, :] = kg * jnp.exp(gl_scr[hh, d, rows, :] - gcb)
        return carry

    _loop(seq // GROUP, group_step, 0)

    o_ref[...] = jnp.zeros_like(o_ref)

    def chunk_step(n, carry):
        idxs = [n if d == 0 else n_chunks - 1 - n for _, d in head_dirs]
        rows = [_rows(idx, CHUNK) for idx in idxs]
        wqs = [_dot(jnp.concatenate([w_scr[hh, d, r, :], qd_scr[hh, d, r, :]], axis=0), s)
               for (hh, d), r, s in zip(head_dirs, rows, carry)]
        v_news = [u_scr[hh, d, r, :] - wq[:CHUNK] for (hh, d), r, wq in zip(head_dirs, rows, wqs)]
        o_ns = [wq[CHUNK:] + _dot(qkd_scr[hh, d, r, :], v_new)
                for (hh, d), r, wq, v_new in zip(head_dirs, rows, wqs, v_news)]
        new = []
        for (hh, d), idx, r, s, v_new in zip(head_dirs, idxs, rows, carry, v_news):
            s_decay = jnp.exp(gl_scr[hh, d, pl.ds(idx * CHUNK, 1), :])
            new.append(s * s_decay + _dot_tn(kd_scr[hh, d, r, :], v_new))
        for (hh, d), r, o_n in zip(head_dirs, rows, o_ns):
            o_ref[r, lanes(hh)] += o_n
        return tuple(new)

    if zero_init:
        init = (jnp.zeros((DN_DK, DN_DV), F32),) * len(head_dirs)
    else:
        init = tuple(s0_ref[0, d, hh] for hh, d in head_dirs)
    fin = lax.fori_loop(0, n_chunks, chunk_step, init)
    if emit_state:
        for i, (hh, d) in enumerate(head_dirs):
            sfin_ref[0, d, hh] = fin[i]


def _gdn_scan(main, gt, batch, seq, conv_w, a_log, dt_bias, s0, emit_state):
    n_tok = batch * seq
    nh = DN_HEADS
    hp = min(nh, max(2, SCAN_STEP_POSITIONS // seq))
    n_hb = nh // hp
    grow, gcol = _gate_layouts(gt, batch, nh)
    zero_init = s0 is None
    smem = pl.BlockSpec(memory_space=pltpu.SMEM)
    qkv_spec = lambda off: pl.BlockSpec((seq, hp * DN_DK), lambda b, h: (b, off + h))
    cw_spec = lambda off: pl.BlockSpec((DN_CONV, hp * DN_DK), lambda b, h: (0, off + h))
    state_spec = pl.BlockSpec((1, N_DIR, hp, DN_DK, DN_DV), lambda b, h: (b, 0, h, 0, 0))
    in_specs = [smem, smem, qkv_spec(0), qkv_spec(n_hb), qkv_spec(2 * n_hb),
                cw_spec(0), cw_spec(n_hb), cw_spec(2 * n_hb),
                pl.BlockSpec((1, hp) + grow.shape[2:], lambda b, h: (b, h, 0, 0, 0, 0)),
                pl.BlockSpec((1, hp) + gcol.shape[2:], lambda b, h: (b, h, 0, 0, 0, 0))]
    args = [a_log, dt_bias, main, main, main, conv_w, conv_w, conv_w, grow, gcol]
    if not zero_init:
        in_specs.append(state_spec)
        args.append(s0)
    out_specs = [pl.BlockSpec((seq, hp * DN_DV), lambda b, h: (b, h))]
    out_shape = [jax.ShapeDtypeStruct((n_tok, nh * DN_DV), F32)]
    if emit_state:
        out_specs.append(state_spec)
        out_shape.append(jax.ShapeDtypeStruct((batch, N_DIR, nh, DN_DK, DN_DV), F32))
    per_dir = lambda width: pltpu.VMEM((hp, N_DIR, seq, width), F32)
    qkv_scr = pltpu.VMEM((seq, hp * DN_DK), F32)
    outs = pl.pallas_call(
        functools.partial(_gdn_kernel, seq=seq, hp=hp, zero_init=zero_init, emit_state=emit_state),
        grid=(batch, n_hb),
        in_specs=in_specs, out_specs=out_specs, out_shape=out_shape,
        scratch_shapes=[qkv_scr, qkv_scr, qkv_scr,
                        pltpu.VMEM((hp, N_DIR, 8, GROUP), F32),
                        per_dir(128), per_dir(128), per_dir(128),
                        per_dir(DN_DV), per_dir(DN_DK), per_dir(DN_DK), per_dir(DN_DK), per_dir(CHUNK)],
        compiler_params=_params("parallel", "parallel"),
        name="gdn_scan",
    )(*args)
    return outs[0], (outs[1] if emit_state else None)


ML_AUG = 2 * ML_DV


def _mlstm_kernel(*refs, seq, hp, zero_init, emit_state):
    bi_ref, bf_ref, q_ref, k_ref, v_ref, grow_ref, gcol_ref = refs[:7]
    pos = 7
    c0_ref = m0_ref = None
    if not zero_init:
        c0_ref, m0_ref = refs[pos:pos + 2]
        pos += 2
    o_ref = refs[pos]
    pos += 1
    cfin_ref = nfin_ref = mfin_ref = None
    if emit_state:
        cfin_ref, nfin_ref, mfin_ref = refs[pos:pos + 3]
        pos += 3
    bcr_scr, lir_scr, bcb_scr, mi_scr, kwf_scr, dec_scr, dc_scr, cin_scr = refs[pos:]

    n_chunks = seq // CHUNK
    n_groups = seq // GROUP
    head0 = pl.program_id(1) * hp
    cm = _chunk_masks()
    gm = _group_masks()
    ones_col = jnp.where(lax.broadcasted_iota(jnp.int32, (GROUP, ML_DV), 1) == 0, 1.0, 0.0)
    head_dirs = [(hh, d) for hh in range(hp) for d in range(N_DIR)]

    m_fin = {}
    for hh, d in head_dirs:
        b_i = bi_ref[d, head0 + hh]
        b_f = bf_ref[d, head0 + hh]
        lir_scr[hh, d] = grow_ref[0, hh, d, 0] + b_i
        bcr_scr[hh, d] = _cumsum_rows(-_softplus(-(grow_ref[0, hh, d, 1] + b_f)), d, gm)
        li_col = gcol_ref[0, hh, d, 0] + b_i
        bc_col = _cumsum_cols(-_softplus(-(gcol_ref[0, hh, d, 1] + b_f)), d, cm)
        last = CHUNK - 1 if d == 0 else 0
        b_last = bc_col[last:last + 1, :]
        w_col = b_last - bc_col + li_col
        w_max = jnp.max(w_col, axis=0, keepdims=True)
        m = jnp.zeros((1, 1), F32) if zero_init else m0_ref[0, d, hh][:, 0:1]
        for step in range(n_chunks):
            n = step if d == 0 else n_chunks - 1 - step
            sl = slice(n * CHUNK, (n + 1) * CHUNK)
            m_new = jnp.maximum(b_last[:, n:n + 1] + m, w_max[:, n:n + 1])
            bcb_scr[hh, d, sl, :] = jnp.broadcast_to(bc_col[:, n:n + 1], (CHUNK, 128))
            mi_scr[hh, d, sl, :] = jnp.broadcast_to(m, (CHUNK, 128))
            kwf_scr[hh, d, sl, :] = jnp.broadcast_to(jnp.exp(w_col[:, n:n + 1] - m_new), (CHUNK, 128))
            dec_scr[hh, d, n:n + 1, :] = jnp.broadcast_to(jnp.exp(b_last[:, n:n + 1] + m - m_new), (1, 128))
            m = m_new
        m_fin[hh, d] = m

    def load_kv(rows, hh):
        kg = k_ref[rows, hh * ML_DQK:(hh + 1) * ML_DQK].astype(F32)
        v_aug = jnp.concatenate([v_ref[rows, hh * ML_DV:(hh + 1) * ML_DV].astype(F32), ones_col], axis=1)
        return kg, v_aug

    def block_diag(x):
        return jnp.where(gm.same, jnp.concatenate([x] * CHUNKS_PER_GROUP, axis=1), 0.0)

    def delta_step(gi, carry):
        rows = _rows(gi, GROUP)
        kvs = [load_kv(rows, hh) for hh in range(hp)]
        kws = [block_diag(kvs[hh][0] * kwf_scr[hh, d, rows, :][:, :ML_DQK]) for hh, d in head_dirs]
        for (hh, d), kw in zip(head_dirs, kws):
            dc_scr[hh, d, rows, :] = _dot_tn(kw, kvs[hh][1])
        return carry

    _loop(n_groups, delta_step, 0)

    def prefix_step(n, carry):
        new = []
        for (hh, d), c_aug in zip(head_dirs, carry):
            idx = n if d == 0 else n_chunks - 1 - n
            rows = _rows(idx, CHUNK)
            cin_scr[hh, d, rows, :] = c_aug
            new.append(c_aug * _wide(dec_scr[hh, d, pl.ds(idx, 1), :]) + dc_scr[hh, d, rows, :])
        return tuple(new)

    if zero_init:
        init = (jnp.zeros((ML_DQK, ML_AUG), F32),) * len(head_dirs)
    else:
        init = tuple(c0_ref[0, d, hh] for hh, d in head_dirs)
    fin = lax.fori_loop(0, n_chunks, prefix_step, init)

    def out_step(gi, carry):
        rows = _rows(gi, GROUP)
        kvs = [load_kv(rows, hh) for hh in range(hp)]
        qgs = [q_ref[rows, hh * ML_DQK:(hh + 1) * ML_DQK].astype(F32) * (ML_DQK ** -0.5) for hh in range(hp)]
        qks = [_dot_nt(qg, kv[0]) for qg, kv in zip(qgs, kvs)]
        q_bds = [block_diag(qg) for qg in qgs]
        inters = [_dot(q_bds[hh], cin_scr[hh, d, rows, :]) for hh, d in head_dirs]
        m_ts, b_ms, ss = [], [], []
        for hh, d in head_dirs:
            b_colb = bcb_scr[hh, d, rows, :]
            b_m = b_colb[:, 0:1] + mi_scr[hh, d, rows, :][:, 0:1]
            d_log = jnp.where(gm.incl[d], _wide(b_colb) - bcr_scr[hh, d, pl.ds(gi, 1), :]
                              + lir_scr[hh, d, pl.ds(gi, 1), :], -jnp.inf)
            m_t = jnp.maximum(b_m, jnp.max(d_log, axis=-1, keepdims=True))
            m_ts.append(m_t)
            b_ms.append(b_m)
            ss.append(qks[hh] * jnp.exp(d_log - m_t))
        intras = [_dot(s, kvs[hh][1]) for (hh, d), s in zip(head_dirs, ss)]
        hs = []
        for inter, intra, m_t, b_m in zip(inters, intras, m_ts, b_ms):
            num = jnp.exp(b_m - m_t) * inter + intra
            den = num[:, ML_DV:ML_DV + 1]
            hs.append(num[:, :ML_DV] / jnp.maximum(jnp.abs(den), jnp.exp(-m_t)))
        for hh in range(hp):
            o_ref[rows, hh * ML_DV:(hh + 1) * ML_DV] = hs[N_DIR * hh] + hs[N_DIR * hh + 1]
        return carry

    _loop(n_groups, out_step, 0)

    if emit_state:
        for i, (hh, d) in enumerate(head_dirs):
            cfin_ref[0, d, hh] = fin[i][:, :ML_DV]
            nfin_ref[0, d, hh] = fin[i][:, ML_DV:]
            mfin_ref[0, d, hh] = jnp.broadcast_to(m_fin[hh, d], (1, 128))


def _mlstm_scan(main, gt, batch, seq, b_i, b_f, state0, emit_state):
    n_tok = batch * seq
    n_chunks = seq // CHUNK
    nh = ML_HEADS
    hp = min(nh, max(2, SCAN_STEP_POSITIONS // seq))
    n_hb = nh // hp
    grow, gcol = _gate_layouts(gt, batch, nh)
    zero_init = state0 is None
    smem = pl.BlockSpec(memory_space=pltpu.SMEM)
    k_off = (nh * ML_DQK) // (hp * ML_DQK)
    v_off = (2 * nh * ML_DQK) // (hp * ML_DV)
    c_spec = pl.BlockSpec((1, N_DIR, hp, ML_DQK, ML_AUG), lambda b, p: (b, 0, p, 0, 0))
    m_spec = pl.BlockSpec((1, N_DIR, hp, 1, 128), lambda b, p: (b, 0, p, 0, 0))
    in_specs = [smem, smem,
                pl.BlockSpec((seq, hp * ML_DQK), lambda b, p: (b, p)),
                pl.BlockSpec((seq, hp * ML_DQK), lambda b, p: (b, k_off + p)),
                pl.BlockSpec((seq, hp * ML_DV), lambda b, p: (b, v_off + p)),
                pl.BlockSpec((1, hp) + grow.shape[2:], lambda b, p: (b, p, 0, 0, 0, 0)),
                pl.BlockSpec((1, hp) + gcol.shape[2:], lambda b, p: (b, p, 0, 0, 0, 0))]
    args = [b_i, b_f, main, main, main, grow, gcol]
    if not zero_init:
        in_specs += [c_spec, m_spec]
        args += list(state0)
    out_specs = [pl.BlockSpec((seq, hp * ML_DV), lambda b, p: (b, p))]
    out_shape = [jax.ShapeDtypeStruct((n_tok, nh * ML_DV), F32)]
    if emit_state:
        half_spec = pl.BlockSpec((1, N_DIR, hp, ML_DQK, ML_DV), lambda b, p: (b, 0, p, 0, 0))
        out_specs += [half_spec, half_spec, m_spec]
        out_shape += [jax.ShapeDtypeStruct((batch, N_DIR, nh, ML_DQK, ML_DV), F32),
                      jax.ShapeDtypeStruct((batch, N_DIR, nh, ML_DQK, ML_AUG - ML_DV), F32),
                      jax.ShapeDtypeStruct((batch, N_DIR, nh, 1, 128), F32)]
    per_hd = lambda rows, width: pltpu.VMEM((hp, N_DIR, rows, width), F32)
    outs = pl.pallas_call(
        functools.partial(_mlstm_kernel, seq=seq, hp=hp, zero_init=zero_init, emit_state=emit_state),
        grid=(batch, n_hb),
        in_specs=in_specs, out_specs=out_specs, out_shape=out_shape,
        scratch_shapes=[per_hd(8, GROUP), per_hd(8, GROUP),
                        per_hd(seq, 128), per_hd(seq, 128), per_hd(seq, 128),
                        per_hd(max(n_chunks, 8), 128),
                        per_hd(seq, ML_AUG), per_hd(seq, ML_AUG)],
        compiler_params=_params("parallel", "parallel"),
        name="mlstm_scan",
    )(*args)
    return outs[0], (tuple(outs[1:4]) if emit_state else None)


def _trunk(x, batch, seq, rows, mods, tokens_per_cond, st_d, st_ml, emit_state, p):
    depth = p["w_ada"].shape[0]
    tile_cond = lambda tile: (lambda i: (i * tile) // tokens_per_cond)
    new_d = new_ml = None
    for layer in range(depth):
        mod = mods[layer]
        kind, j = layer % 3, layer // 3
        if kind == 0:
            x = _fnet_mix(x, seq, j, p["norm_mix"][layer], mod, tile_cond(seq), p["fnet_w"], p["fnet_b"][j])
        elif kind == 1:
            n_main = DN_HEADS * (2 * DN_DK + 2 * DN_DV)
            main, gt = _norm_proj(x, p["norm_mix"][layer], mod, tokens_per_cond, p["dn_w_in"][j], n_main)
            s0 = None if st_d is None else st_d[:, j]
            o, sfin = _gdn_scan(main, gt, batch, seq, p["dn_conv_w"][j], p["dn_a_log"][j], p["dn_dt_bias"][j],
                                s0, emit_state)
            if emit_state:
                new_d = sfin
            x = _gated_out(x, o, main, 3, p["dn_norm"][j], mod, tile_cond(OUT_TOKEN_TILE), p["dn_w_out"][j],
                           DN_HEADS, jax.nn.silu)
        else:
            n_main = 2 * ML_HEADS * ML_DQK + 2 * ML_HEADS * ML_DV
            main, gt = _norm_proj(x, p["norm_mix"][layer], mod, tokens_per_cond, p["ml_w_in"][j], n_main)
            state0 = None
            if st_ml is not None:
                c0, n0, m0 = (s[:, j] for s in st_ml)
                pad = jnp.zeros(c0.shape[:-1] + (ML_AUG - ML_DV - 1,), F32)
                c_aug0 = jnp.concatenate([c0, n0[..., None], pad], axis=-1)
                m0b = jnp.broadcast_to(m0[..., None, None], m0.shape + (1, 128))
                state0 = (c_aug0, m0b)
            o, fin = _mlstm_scan(main, gt, batch, seq, p["ml_b_i"][j], p["ml_b_f"][j], state0, emit_state)
            if emit_state:
                new_ml = (fin[0], fin[1][..., 0], fin[2][..., 0, 0])
            x = _gated_out(x, o, main, 2, p["ml_norm"][j], mod, tile_cond(OUT_TOKEN_TILE), p["ml_w_out"][j],
                           ML_HEADS, jax.nn.sigmoid)
        x = _conv_ffn(x, seq, rows, layer, p["norm_ffn"][layer], mod, tile_cond(FFN_TOKEN_TILE), p["ffn_w_up"],
                      p["ffn_conv_w"], p["ffn_conv_b"], p["ffn_w_down"], p["norm_final"],
                      final_norm=(layer == depth - 1))
    return x, new_d, new_ml


def kernel(x_prompt, x_sample, state_delta, state_mlstm_c, state_mlstm_n, state_mlstm_m, c, c_ctx, w_ada, b_ada, norm_mix, norm_ffn, norm_final, ffn_w_up, ffn_conv_w, ffn_conv_b, ffn_w_down, fnet_w, fnet_b, dn_w_in, dn_conv_w, dn_a_log, dn_dt_bias, dn_norm, dn_w_out, ml_w_in, ml_b_i, ml_b_f, ml_norm, ml_w_out):
    p = dict(w_ada=w_ada, norm_mix=norm_mix, norm_ffn=norm_ffn, norm_final=norm_final, ffn_w_up=ffn_w_up,
             ffn_conv_w=ffn_conv_w, ffn_conv_b=ffn_conv_b, ffn_w_down=ffn_w_down, fnet_w=fnet_w, fnet_b=fnet_b,
             dn_w_in=dn_w_in, dn_conv_w=dn_conv_w, dn_a_log=dn_a_log, dn_dt_bias=dn_dt_bias, dn_norm=dn_norm,
             dn_w_out=dn_w_out, ml_w_in=ml_w_in, ml_b_i=ml_b_i, ml_b_f=ml_b_f, ml_norm=ml_norm, ml_w_out=ml_w_out)
    b_ctx, t_ctx, _ = x_prompt.shape
    b_smp, t_smp, _ = x_sample.shape
    depth = w_ada.shape[0]

    cond8 = jnp.concatenate([c_ctx[None, :], c, jnp.zeros((8 - 1 - b_smp, D_MODEL), F32)], axis=0)
    mods = _adaln_all(cond8, w_ada, b_ada)
    mods_ctx = [mods[l, 0:1].reshape(1, 1, -1) for l in range(depth)]
    mods_smp = [mods[l, 1:1 + b_smp].reshape(b_smp, 1, -1) for l in range(depth)]

    y_ctx, new_d, new_ml = _trunk(x_prompt.reshape(b_ctx * t_ctx, D_MODEL), b_ctx, t_ctx, 1, mods_ctx,
                                  b_ctx * t_ctx, None, None, True, p)
    y_smp, _, _ = _trunk(x_sample.reshape(b_smp * t_smp, D_MODEL), b_smp, t_smp, t_smp // GRID_W, mods_smp,
                         t_smp, state_delta, (state_mlstm_c, state_mlstm_n, state_mlstm_m), False, p)

    new_c, new_n, new_m = new_ml
    return (y_ctx.reshape(b_ctx, t_ctx, D_MODEL), y_smp.reshape(b_smp, t_smp, D_MODEL),
            new_d[:, None], new_c[:, None], new_n[:, None], new_m[:, None])
```

```python
import collections
import functools

import numpy as np
import jax
import jax.numpy as jnp
from jax import lax
from jax.experimental import pallas as pl
from jax.experimental.pallas import tpu as pltpu

F32 = jnp.float32
BF16 = jnp.bfloat16
HIGHEST = lax.Precision.HIGHEST

D_MODEL = 1024
EPS = 1e-6
N_DIR = 2
CHUNK = 64
LOG_CHUNK = 6
GROUP = 256
CHUNKS_PER_GROUP = GROUP // CHUNK
FNET_GROUP_DIM = 256
DN_HEADS, DN_DK, DN_DV, DN_CONV = 8, 128, 128, 5
ML_HEADS, ML_DQK, ML_DV = 8, 64, 128
D_FF = 2816
GRID_W = 64

FFN_TOKEN_TILE = 1024
FFN_FF_TILE = 256
FFN_SUB_TILES = 4
PROJ_TOKEN_TILE = 2048
PROJ_COL_TILE = 512
OUT_TOKEN_TILE = 512
ADA_COL_TILE = 1536
SCAN_STEP_POSITIONS = 1024
VMEM_LIMIT = 56 * 1024 * 1024


def _params(*sem):
    return pltpu.CompilerParams(dimension_semantics=sem, vmem_limit_bytes=VMEM_LIMIT)


def _dot(a, b):
    return jnp.dot(a.astype(BF16), b.astype(BF16), preferred_element_type=F32)


def _dot_nt(a, b):
    return lax.dot_general(a.astype(BF16), b.astype(BF16), (((1,), (1,)), ((), ())),
                           preferred_element_type=F32)


def _dot_tn(a, b):
    return lax.dot_general(a.astype(BF16), b.astype(BF16), (((0,), (0,)), ((), ())),
                           preferred_element_type=F32)


def _dot_f32(a, b):
    return jnp.dot(a, b, precision=HIGHEST, preferred_element_type=F32)


def _rms(x, g):
    return x * lax.rsqrt(jnp.mean(x * x, axis=-1, keepdims=True) + EPS) * g


def _norm_mod(x, g, sc, sh):
    return _rms(x, g) * (1.0 + sc) + sh


def _softplus(x):
    return jnp.maximum(x, 0.0) + jnp.log1p(jnp.exp(-jnp.abs(x)))


def _mod_spec(chunk, cond_of_tile):
    return pl.BlockSpec((1, 1, D_MODEL), lambda i, *_: (cond_of_tile(i), 0, chunk))


def _adaln_kernel(c_ref, w_ref, b_ref, o_ref):
    s = jax.nn.silu(c_ref[...])
    o_ref[0] = _dot(s, w_ref[0]) + b_ref[0]


def _adaln_all(cond8, w_ada, b_ada):
    depth = w_ada.shape[0]
    n_out = w_ada.shape[2]
    tn = ADA_COL_TILE
    return pl.pallas_call(
        _adaln_kernel,
        grid=(depth, n_out // tn),
        in_specs=[pl.BlockSpec((8, D_MODEL), lambda l, j: (0, 0)),
                  pl.BlockSpec((1, D_MODEL, tn), lambda l, j: (l, 0, j)),
                  pl.BlockSpec((1, 1, tn), lambda l, j: (l, 0, j))],
        out_specs=pl.BlockSpec((1, 8, tn), lambda l, j: (l, 0, j)),
        out_shape=jax.ShapeDtypeStruct((depth, 8, n_out), F32),
        compiler_params=_params("parallel", "parallel"),
        name="adaln",
    )(cond8, w_ada, b_ada.reshape(depth, 1, n_out))


def _norm_proj_kernel(x_ref, g_ref, sc_ref, sh_ref, w_ref, wgt_ref, o_ref, gt_ref, h_scr):
    @pl.when(pl.program_id(1) == 0)
    def _():
        h = _norm_mod(x_ref[...], g_ref[...], sc_ref[0], sh_ref[0]).astype(BF16)
        h_scr[...] = h
        gt_ref[...] = _dot_nt(wgt_ref[...], h)

    o_ref[...] = _dot(h_scr[...], w_ref[...]).astype(o_ref.dtype)


def _norm_proj(x, norm_g, mod, tokens_per_cond, w_in, n_main):
    n_tok = x.shape[0]
    tm, tn = min(PROJ_TOKEN_TILE, tokens_per_cond), PROJ_COL_TILE
    cond_of_tile = lambda i: (i * tm) // tokens_per_cond
    n_gate = w_in.shape[1] - n_main
    wgt = w_in[:, n_main:].T
    return pl.pallas_call(
        _norm_proj_kernel,
        grid=(n_tok // tm, n_main // tn),
        in_specs=[pl.BlockSpec((tm, D_MODEL), lambda i, j: (i, 0)),
                  pl.BlockSpec((1, D_MODEL), lambda i, j: (0, 0)),
                  _mod_spec(1, cond_of_tile), _mod_spec(0, cond_of_tile),
                  pl.BlockSpec((D_MODEL, tn), lambda i, j: (0, j)),
                  pl.BlockSpec((n_gate, D_MODEL), lambda i, j: (0, 0))],
        out_specs=[pl.BlockSpec((tm, tn), lambda i, j: (i, j)),
                   pl.BlockSpec((n_gate, tm), lambda i, j: (0, i))],
        out_shape=[jax.ShapeDtypeStruct((n_tok, n_main), BF16),
                   jax.ShapeDtypeStruct((n_gate, n_tok), F32)],
        scratch_shapes=[pltpu.VMEM((tm, D_MODEL), BF16)],
        compiler_params=_params("parallel", "arbitrary"),
        name="norm_proj",
    )(x, norm_g.reshape(1, D_MODEL), mod, mod, w_in, wgt)


def _gated_out_kernel(x_ref, o_ref, z_ref, ng_ref, g1_ref, w_ref, y_ref, wb_scr, *, n_heads, gate_fn):
    @pl.when(pl.program_id(0) == 0)
    def _():
        wb_scr[...] = w_ref[...].astype(BF16)

    dv = o_ref.shape[1] // n_heads
    parts = []
    for h in range(n_heads):
        sl = slice(h * dv, (h + 1) * dv)
        parts.append((_rms(o_ref[:, sl], ng_ref[...]) * gate_fn(z_ref[:, sl].astype(F32))).astype(BF16))
    hs = jnp.concatenate(parts, axis=1)
    y = jnp.dot(hs, wb_scr[...], preferred_element_type=F32)
    y_ref[...] = x_ref[...] + g1_ref[0] * y


def _gated_out(x, o, main, z_block, norm_g, mod, cond_of_tile, w_out, n_heads, gate_fn):
    n_tok = x.shape[0]
    tm = OUT_TOKEN_TILE
    width = o.shape[1]
    return pl.pallas_call(
        functools.partial(_gated_out_kernel, n_heads=n_heads, gate_fn=gate_fn),
        grid=(n_tok // tm,),
        in_specs=[pl.BlockSpec((tm, D_MODEL), lambda i: (i, 0)),
                  pl.BlockSpec((tm, width), lambda i: (i, 0)),
                  pl.BlockSpec((tm, width), lambda i: (i, z_block)),
                  pl.BlockSpec((1, width // n_heads), lambda i: (0, 0)),
                  _mod_spec(2, cond_of_tile),
                  pl.BlockSpec((width, D_MODEL), lambda i: (0, 0))],
        out_specs=pl.BlockSpec((tm, D_MODEL), lambda i: (i, 0)),
        out_shape=jax.ShapeDtypeStruct((n_tok, D_MODEL), F32),
        scratch_shapes=[pltpu.VMEM((width, D_MODEL), BF16)],
        compiler_params=_params("arbitrary"),
        name="gated_out",
    )(x, o, main, norm_g.reshape(1, -1), mod, w_out)


def _dft_mats(n):
    k = np.arange(n, dtype=np.int64)
    ang = 2.0 * np.pi * ((k[:, None] * k[None, :]) % n).astype(np.float64) / n
    s = 1.0 / np.sqrt(n)
    return np.cos(ang) * s, np.sin(ang) * s


def _fnet_kernel(x_ref, ng_ref, sc_ref, sh_ref, g1_ref, ct_ref, st_ref, cs_ref, w_ref, b_ref, y_ref, wb_scr):
    @pl.when(pl.program_id(0) == 0)
    def _():
        wb_scr[...] = w_ref[...].astype(BF16)

    x = x_ref[...]
    h = _norm_mod(x, ng_ref[...], sc_ref[0], sh_ref[0]).astype(BF16)
    p = jnp.dot(ct_ref[...], h, preferred_element_type=F32)
    q = jnp.dot(st_ref[...], h, preferred_element_type=F32)
    gd = FNET_GROUP_DIM
    parts = []
    for g in range(D_MODEL // gd):
        sl = slice(g * gd, (g + 1) * gd)
        pq = jnp.concatenate([p[:, sl], q[:, sl]], axis=1).astype(BF16)
        parts.append(jnp.dot(pq, cs_ref[...], preferred_element_type=F32).astype(BF16))
    f = jnp.concatenate(parts, axis=1)
    y = jnp.dot(f, wb_scr[...], preferred_element_type=F32) + b_ref[...]
    y_ref[...] = x + g1_ref[0] * y


def _fnet_mix(x, seq, layer, norm_g, mod, cond_of_seq, w, b):
    n_tok = x.shape[0]
    ct, st = _dft_mats(seq)
    cc, sc = _dft_mats(FNET_GROUP_DIM)
    cs = np.concatenate([cc, -sc], axis=0)
    const = lambda i: (0, 0)
    ct, st, cs = (jnp.asarray(m, F32).astype(BF16) for m in (ct, st, cs))
    return pl.pallas_call(
        _fnet_kernel,
        grid=(n_tok // seq,),
        in_specs=[pl.BlockSpec((seq, D_MODEL), lambda i: (i, 0)),
                  pl.BlockSpec((1, D_MODEL), const),
                  _mod_spec(1, cond_of_seq), _mod_spec(0, cond_of_seq), _mod_spec(2, cond_of_seq),
                  pl.BlockSpec((seq, seq), const),
                  pl.BlockSpec((seq, seq), const),
                  pl.BlockSpec((2 * FNET_GROUP_DIM, FNET_GROUP_DIM), const),
                  pl.BlockSpec((None, D_MODEL, D_MODEL), lambda i: (layer, 0, 0)),
                  pl.BlockSpec((1, D_MODEL), const)],
        out_specs=pl.BlockSpec((seq, D_MODEL), lambda i: (i, 0)),
        out_shape=jax.ShapeDtypeStruct((n_tok, D_MODEL), F32),
        scratch_shapes=[pltpu.VMEM((D_MODEL, D_MODEL), BF16)],
        compiler_params=_params("arbitrary"),
        name="fnet",
    )(x, norm_g.reshape(1, D_MODEL), mod, mod, mod,
      ct, st, cs, w, b.reshape(1, D_MODEL))


def _shift_rows(x, delta):
    n = x.shape[0]
    return pltpu.roll(x, (-delta) % n, 0)


def _dwconv_tokens(g, cw_ref, seq, rows):
    tm = g.shape[0]
    width = seq // rows
    assert seq & (seq - 1) == 0 and width & (width - 1) == 0
    t = lax.broadcasted_iota(jnp.int32, (tm, 1), 0)
    pos = t & (seq - 1)
    col = pos & (width - 1)
    row = pos >> (width.bit_length() - 1)
    g_cols = (_shift_rows(g, -1) * (col >= 1).astype(F32), g,
              _shift_rows(g, 1) * (col <= width - 2).astype(F32))
    out = None
    for di in (-1, 0, 1):
        if rows == 1 and di != 0:
            continue
        r = sum(g_cols[dj] * cw_ref[pl.ds(3 * (di + 1) + dj, 1), :] for dj in range(3))
        if di != 0:
            ok = (row + di >= 0) & (row + di <= rows - 1)
            r = _shift_rows(r, di * width) * ok.astype(F32)
        out = r if out is None else out + r
    return out


def _ffn_kernel(x_ref, ng_ref, sc_ref, sh_ref, g2_ref, wa_ref, wg_ref, cw_ref, cb_ref, wd_ref, nf_ref,
                y_ref, h_scr, acc_scr, *, seq, rows, final_norm):
    j = pl.program_id(1)

    @pl.when(j == 0)
    def _():
        h_scr[...] = _norm_mod(x_ref[...], ng_ref[...], sc_ref[0], sh_ref[0]).astype(BF16)
        acc_scr[...] = jnp.zeros_like(acc_scr)

    tm = h_scr.shape[0]
    sub = tm // FFN_SUB_TILES
    parts = [pl.ds(r, sub) for r in range(0, tm, sub)]
    wa, wg, wd = (w[...].astype(BF16) for w in (wa_ref, wg_ref, wd_ref))
    gs = [jnp.dot(h_scr[p, :], wg, preferred_element_type=F32) for p in parts]
    as_ = [jnp.dot(h_scr[p, :], wa, preferred_element_type=F32) for p in parts]
    per_seq = max(seq // sub, 1)
    convs = []
    for i in range(0, len(parts), per_seq):
        g = gs[i] if per_seq == 1 else jnp.concatenate(gs[i:i + per_seq], axis=0)
        c = _dwconv_tokens(g, cw_ref, seq, rows) + cb_ref[...]
        convs += [c[k * sub:(k + 1) * sub] for k in range(per_seq)]
    acts = [(jax.nn.silu(c) * a).astype(BF16) for c, a in zip(convs, as_)]
    for p, act in zip(parts, acts):
        acc_scr[p, :] += jnp.dot(act, wd, preferred_element_type=F32)

    @pl.when(j == pl.num_programs(1) - 1)
    def _():
        y = x_ref[...] + g2_ref[0] * acc_scr[...]
        if final_norm:
            y = _rms(y, nf_ref[...])
        y_ref[...] = y


def _conv_ffn(x, seq, rows, layer, norm_g, mod, cond_of_tile, w_up, conv_w, conv_b, w_down, norm_final, final_norm):
    n_tok = x.shape[0]
    tm, tf = FFN_TOKEN_TILE, FFN_FF_TILE
    n_ff_tiles = D_FF // tf
    depth = w_up.shape[0]
    return pl.pallas_call(
        functools.partial(_ffn_kernel, seq=seq, rows=rows, final_norm=final_norm),
        grid=(n_tok // tm, n_ff_tiles),
        in_specs=[pl.BlockSpec((tm, D_MODEL), lambda i, j: (i, 0)),
                  pl.BlockSpec((1, D_MODEL), lambda i, j: (0, 0)),
                  _mod_spec(4, cond_of_tile), _mod_spec(3, cond_of_tile), _mod_spec(5, cond_of_tile),
                  pl.BlockSpec((None, D_MODEL, tf), lambda i, j: (layer, 0, j)),
                  pl.BlockSpec((None, D_MODEL, tf), lambda i, j: (layer, 0, n_ff_tiles + j)),
                  pl.BlockSpec((None, 9, tf), lambda i, j: (layer, 0, j)),
                  pl.BlockSpec((None, 1, tf), lambda i, j: (layer, 0, j)),
                  pl.BlockSpec((None, tf, D_MODEL), lambda i, j: (layer, j, 0)),
                  pl.BlockSpec((1, D_MODEL), lambda i, j: (0, 0))],
        out_specs=pl.BlockSpec((tm, D_MODEL), lambda i, j: (i, 0)),
        out_shape=jax.ShapeDtypeStruct((n_tok, D_MODEL), F32),
        scratch_shapes=[pltpu.VMEM((tm, D_MODEL), BF16), pltpu.VMEM((tm, D_MODEL), F32)],
        compiler_params=_params("parallel", "arbitrary"),
        name="conv_ffn",
    )(x, norm_g.reshape(1, D_MODEL), mod, mod, mod, w_up, w_up, conv_w.reshape(depth, 9, D_FF),
      conv_b.reshape(depth, 1, D_FF), w_down, norm_final.reshape(1, D_MODEL))


GroupMasks = collections.namedtuple("GroupMasks", "r c same incl strict")


def _chunk_masks():
    r = lax.broadcasted_iota(jnp.int32, (CHUNK, CHUNK), 0)
    c = lax.broadcasted_iota(jnp.int32, (CHUNK, CHUNK), 1)
    return (r >= c, r <= c)


def _group_masks():
    r = lax.broadcasted_iota(jnp.int32, (GROUP, GROUP), 0)
    c = lax.broadcasted_iota(jnp.int32, (GROUP, GROUP), 1)
    same = (r >> LOG_CHUNK) == (c >> LOG_CHUNK)
    return GroupMasks(r, c, same, (same & (r >= c), same & (r <= c)), (same & (r > c), same & (r < c)))


def _gate_layouts(gt, batch, n_heads, hp):
    seq = gt.shape[1] // batch
    n_groups, n_chunks = seq // GROUP, seq // CHUNK
    assert seq % GROUP == 0 and n_groups <= 8
    g = jnp.transpose(gt.reshape(N_DIR, 2, n_heads, batch, seq), (3, 2, 0, 1, 4))
    rows = g.reshape(g.shape[:4] + (n_groups, GROUP))
    rows = jnp.pad(rows, ((0, 0),) * 4 + ((0, 8 - n_groups), (0, 0)))
    cols = jnp.swapaxes(g.reshape(batch, n_heads // hp, hp * N_DIR * 2 * n_chunks, CHUNK), -1, -2)
    return rows, cols


def _col_gate(gcol_ref, hh, d, kind, n_chunks):
    start = ((hh * N_DIR + d) * 2 + kind) * n_chunks
    return gcol_ref[0, 0, :, start:start + n_chunks]


def _cumsum_rows(row, d, gm):
    return _dot_f32(row, gm.incl[1 - d].astype(F32))


def _cumsum_cols(col, d, cm):
    return _dot_f32(cm[d].astype(F32), col)


def _store_cols(dst, col, n_chunks):
    for n in range(n_chunks):
        dst[n * CHUNK:(n + 1) * CHUNK, :] = jnp.broadcast_to(col[:, n:n + 1], (CHUNK, 128))


def _wide(x):
    return jnp.concatenate([x, x], axis=1)


def _rows(i, size):
    if isinstance(i, int):
        return pl.ds(i * size, size)
    return pl.ds(pl.multiple_of(i * size, size), size)


def _loop(n, body, init):
    if n == 1:
        return body(0, init)
    return lax.fori_loop(0, n, body, init)


def _conv_silu(x, w_ref):
    n = x.shape[0]
    k = w_ref.shape[0]
    t = lax.broadcasted_iota(jnp.int32, (n, 1), 0)
    acc = None
    for j in range(k):
        delta = j - k // 2
        if delta == 0:
            term = x
        else:
            ok = (t + delta >= 0) & (t + delta <= n - 1)
            term = _shift_rows(x, delta) * ok.astype(F32)
        term = term * w_ref[pl.ds(j, 1), :]
        acc = term if acc is None else acc + term
    return jax.nn.silu(acc)


def _l2norm(x):
    return x * lax.rsqrt(jnp.sum(x * x, axis=-1, keepdims=True) + EPS)


def _blockdiag_tri_inverse(mats, gm):
    base = (gm.r >> 3) == (gm.c >> 3)
    eye = jnp.where(gm.r == gm.c, 1.0, 0.0)
    ps = [-jnp.where(base, a, 0.0) for a in mats]
    invs = [eye + p for p in ps]
    for _ in range(2):
        ps = [_dot(p, p) for p in ps]
        invs = [inv + _dot(inv, p) for inv, p in zip(invs, ps)]
    for s in (3, 4, 5):
        join = ((gm.r >> s) ^ (gm.c >> s)) == 1
        tmp = [_dot(jnp.where(join, a, 0.0), inv) for a, inv in zip(mats, invs)]
        invs = [inv - _dot(inv, t) for inv, t in zip(invs, tmp)]
    return invs


def _gdn_kernel(*refs, seq, hp, zero_init, emit_state):
    alog_ref, dtb_ref, q_ref, k_ref, v_ref, cwq_ref, cwk_ref, cwv_ref, grow_ref, gcol_ref = refs[:10]
    pos = 10
    s0_ref = None
    if not zero_init:
        s0_ref = refs[pos]
        pos += 1
    o_ref = refs[pos]
    pos += 1
    sfin_ref = None
    if emit_state:
        sfin_ref = refs[pos]
        pos += 1
    (q_scr, k_scr, v_scr, gcr_scr, gcb_scr, bb_scr, gl_scr,
     u_scr, w_scr, qd_scr, kd_scr, qkd_scr) = refs[pos:]

    n_chunks = seq // CHUNK
    cm = _chunk_masks()
    gm = _group_masks()
    head_dirs = [(hh, d) for hh in range(hp) for d in range(N_DIR)]
    lanes = lambda hh: slice(hh * DN_DK, (hh + 1) * DN_DK)

    q_scr[...] = _conv_silu(q_ref[...].astype(F32), cwq_ref)
    k_scr[...] = _conv_silu(k_ref[...].astype(F32), cwk_ref)
    v_scr[...] = _conv_silu(v_ref[...].astype(F32), cwv_ref)
    for hh in range(hp):
        q_scr[:, lanes(hh)] = _l2norm(q_scr[:, lanes(hh)]) * (DN_DK ** -0.5)
        k_scr[:, lanes(hh)] = _l2norm(k_scr[:, lanes(hh)])

    for hh, d in head_dirs:
        head = pl.program_id(1) * hp + hh
        neg_a = -jnp.exp(jnp.full((1, 1), alog_ref[d, head], F32))
        dtb = dtb_ref[d, head]
        gcr_scr[hh, d] = _cumsum_rows(neg_a * _softplus(grow_ref[0, hh, d, 0] + dtb), d, gm)
        gc_col = _cumsum_cols(neg_a * _softplus(_col_gate(gcol_ref, hh, d, 0, n_chunks) + dtb), d, cm)
        _store_cols(gcb_scr.at[hh, d], gc_col, n_chunks)
        _store_cols(bb_scr.at[hh, d], jax.nn.sigmoid(_col_gate(gcol_ref, hh, d, 1, n_chunks)), n_chunks)
        last = CHUNK - 1 if d == 0 else 0
        _store_cols(gl_scr.at[hh, d], jnp.broadcast_to(gc_col[last:last + 1, :], gc_col.shape), n_chunks)

    def group_step(gi, carry):
        rows = _rows(gi, GROUP)
        kqs = []
        for hh in range(hp):
            kg, qg = k_scr[rows, lanes(hh)], q_scr[rows, lanes(hh)]
            kqs.append(_dot_nt(jnp.concatenate([kg, qg], axis=0), kg))
        mats = []
        for hh, d in head_dirs:
            gcb = gcb_scr[hh, d, rows, :]
            gcr = gcr_scr[hh, d, pl.ds(gi, 1), :]
            decay = jnp.exp(jnp.where(gm.incl[d], _wide(gcb) - gcr, -jnp.inf))
            mats.append(jnp.where(gm.strict[d], kqs[hh][:GROUP] * _wide(bb_scr[hh, d, rows, :]) * decay, 0.0))
            qkd = kqs[hh][GROUP:] * decay
            qkd_scr[hh, d, rows, :] = sum(qkd[:, n * CHUNK:(n + 1) * CHUNK] for n in range(CHUNKS_PER_GROUP))
        t_invs = _blockdiag_tri_inverse(mats, gm)
        for (hh, d), t_inv in zip(head_dirs, t_invs):
            qg, kg, vg = q_scr[rows, lanes(hh)], k_scr[rows, lanes(hh)], v_scr[rows, lanes(hh)]
            gcb = gcb_scr[hh, d, rows, :]
            bb = bb_scr[hh, d, rows, :]
            e_gc = jnp.exp(gcb)
            uw = _dot(t_inv, jnp.concatenate([vg * bb, kg * bb * e_gc], axis=1))
            u_scr[hh, d, rows, :] = uw[:, :DN_DV]
            w_scr[hh, d, rows, :] = uw[:, DN_DV:]
            qd_scr[hh, d, rows, :] = qg * e_gc
            kd_scr[hh, d, rows, :] = kg * jnp.exp(gl_scr[hh, d, rows, :] - gcb)
        return carry

    _loop(seq // GROUP, group_step, 0)

    o_ref[...] = jnp.zeros_like(o_ref)

    def chunk_step(n, carry):
        idxs = [n if d == 0 else n_chunks - 1 - n for _, d in head_dirs]
        rows = [_rows(idx, CHUNK) for idx in idxs]
        wqs = [_dot(jnp.concatenate([w_scr[hh, d, r, :], qd_scr[hh, d, r, :]], axis=0), s)
               for (hh, d), r, s in zip(head_dirs, rows, carry)]
        v_news = [u_scr[hh, d, r, :] - wq[:CHUNK] for (hh, d), r, wq in zip(head_dirs, rows, wqs)]
        o_ns = [wq[CHUNK:] + _dot(qkd_scr[hh, d, r, :], v_new)
                for (hh, d), r, wq, v_new in zip(head_dirs, rows, wqs, v_news)]
        new = []
        for (hh, d), idx, r, s, v_new in zip(head_dirs, idxs, rows, carry, v_news):
            s_decay = jnp.exp(gl_scr[hh, d, pl.ds(idx * CHUNK, 1), :])
            new.append(s * s_decay + _dot_tn(kd_scr[hh, d, r, :], v_new))
        for (hh, d), r, o_n in zip(head_dirs, rows, o_ns):
            o_ref[r, lanes(hh)] += o_n
        return tuple(new)

    if zero_init:
        init = (jnp.zeros((DN_DK, DN_DV), F32),) * len(head_dirs)
    else:
        init = tuple(s0_ref[0, d, hh] for hh, d in head_dirs)
    fin = lax.fori_loop(0, n_chunks, chunk_step, init)
    if emit_state:
        for i, (hh, d) in enumerate(head_dirs):
            sfin_ref[0, d, hh] = fin[i]


def _gdn_scan(main, gt, batch, seq, conv_w, a_log, dt_bias, s0, emit_state):
    n_tok = batch * seq
    nh = DN_HEADS
    hp = min(nh, max(2, SCAN_STEP_POSITIONS // seq))
    n_hb = nh // hp
    grow, gcol = _gate_layouts(gt, batch, nh, hp)
    zero_init = s0 is None
    smem = pl.BlockSpec(memory_space=pltpu.SMEM)
    qkv_spec = lambda off: pl.BlockSpec((seq, hp * DN_DK), lambda b, h: (b, off + h))
    cw_spec = lambda off: pl.BlockSpec((DN_CONV, hp * DN_DK), lambda b, h: (0, off + h))
    state_spec = pl.BlockSpec((1, N_DIR, hp, DN_DK, DN_DV), lambda b, h: (b, 0, h, 0, 0))
    in_specs = [smem, smem, qkv_spec(0), qkv_spec(n_hb), qkv_spec(2 * n_hb),
                cw_spec(0), cw_spec(n_hb), cw_spec(2 * n_hb),
                pl.BlockSpec((1, hp) + grow.shape[2:], lambda b, h: (b, h, 0, 0, 0, 0)),
                pl.BlockSpec((1, 1) + gcol.shape[2:], lambda b, h: (b, h, 0, 0))]
    args = [a_log, dt_bias, main, main, main, conv_w, conv_w, conv_w, grow, gcol]
    if not zero_init:
        in_specs.append(state_spec)
        args.append(s0)
    out_specs = [pl.BlockSpec((seq, hp * DN_DV), lambda b, h: (b, h))]
    out_shape = [jax.ShapeDtypeStruct((n_tok, nh * DN_DV), F32)]
    if emit_state:
        out_specs.append(state_spec)
        out_shape.append(jax.ShapeDtypeStruct((batch, N_DIR, nh, DN_DK, DN_DV), F32))
    per_dir = lambda width: pltpu.VMEM((hp, N_DIR, seq, width), F32)
    qkv_scr = pltpu.VMEM((seq, hp * DN_DK), F32)
    outs = pl.pallas_call(
        functools.partial(_gdn_kernel, seq=seq, hp=hp, zero_init=zero_init, emit_state=emit_state),
        grid=(batch, n_hb),
        in_specs=in_specs, out_specs=out_specs, out_shape=out_shape,
        scratch_shapes=[qkv_scr, qkv_scr, qkv_scr,
                        pltpu.VMEM((hp, N_DIR, 8, GROUP), F32),
                        per_dir(128), per_dir(128), per_dir(128),
                        per_dir(DN_DV), per_dir(DN_DK), per_dir(DN_DK), per_dir(DN_DK), per_dir(CHUNK)],
        compiler_params=_params("parallel", "parallel"),
        name="gdn_scan",
    )(*args)
    return outs[0], (outs[1] if emit_state else None)


ML_AUG = 2 * ML_DV


def _mlstm_kernel(*refs, seq, hp, zero_init, emit_state):
    bi_ref, bf_ref, q_ref, k_ref, v_ref, grow_ref, gcol_ref = refs[:7]
    pos = 7
    c0_ref = m0_ref = None
    if not zero_init:
        c0_ref, m0_ref = refs[pos:pos + 2]
        pos += 2
    o_ref = refs[pos]
    pos += 1
    cfin_ref = nfin_ref = mfin_ref = None
    if emit_state:
        cfin_ref, nfin_ref, mfin_ref = refs[pos:pos + 3]
        pos += 3
    bcr_scr, lir_scr, bcb_scr, mi_scr, kwf_scr, dec_scr, dc_scr, cin_scr = refs[pos:]

    n_chunks = seq // CHUNK
    n_groups = seq // GROUP
    head0 = pl.program_id(1) * hp
    cm = _chunk_masks()
    gm = _group_masks()
    ones_col = jnp.where(lax.broadcasted_iota(jnp.int32, (GROUP, ML_DV), 1) == 0, 1.0, 0.0)
    head_dirs = [(hh, d) for hh in range(hp) for d in range(N_DIR)]

    m_fin = {}
    for hh, d in head_dirs:
        b_i = bi_ref[d, head0 + hh]
        b_f = bf_ref[d, head0 + hh]
        lir_scr[hh, d] = grow_ref[0, hh, d, 0] + b_i
        bcr_scr[hh, d] = _cumsum_rows(-_softplus(-(grow_ref[0, hh, d, 1] + b_f)), d, gm)
        li_col = _col_gate(gcol_ref, hh, d, 0, n_chunks) + b_i
        bc_col = _cumsum_cols(-_softplus(-(_col_gate(gcol_ref, hh, d, 1, n_chunks) + b_f)), d, cm)
        last = CHUNK - 1 if d == 0 else 0
        b_last = bc_col[last:last + 1, :]
        w_col = b_last - bc_col + li_col
        w_max = jnp.max(w_col, axis=0, keepdims=True)
        m = jnp.zeros((1, 1), F32) if zero_init else m0_ref[0, d, hh][:, 0:1]
        for step in range(n_chunks):
            n = step if d == 0 else n_chunks - 1 - step
            sl = slice(n * CHUNK, (n + 1) * CHUNK)
            m_new = jnp.maximum(b_last[:, n:n + 1] + m, w_max[:, n:n + 1])
            bcb_scr[hh, d, sl, :] = jnp.broadcast_to(bc_col[:, n:n + 1], (CHUNK, 128))
            mi_scr[hh, d, sl, :] = jnp.broadcast_to(m, (CHUNK, 128))
            kwf_scr[hh, d, sl, :] = jnp.broadcast_to(jnp.exp(w_col[:, n:n + 1] - m_new), (CHUNK, 128))
            dec_scr[hh, d, n:n + 1, :] = jnp.broadcast_to(jnp.exp(b_last[:, n:n + 1] + m - m_new), (1, 128))
            m = m_new
        m_fin[hh, d] = m

    def load_kv(rows, hh):
        kg = k_ref[rows, hh * ML_DQK:(hh + 1) * ML_DQK].astype(F32)
        v_aug = jnp.concatenate([v_ref[rows, hh * ML_DV:(hh + 1) * ML_DV].astype(F32), ones_col], axis=1)
        return kg, v_aug

    def block_diag(x):
        return jnp.where(gm.same, jnp.concatenate([x] * CHUNKS_PER_GROUP, axis=1), 0.0)

    def delta_step(gi, carry):
        rows = _rows(gi, GROUP)
        kvs = [load_kv(rows, hh) for hh in range(hp)]
        kws = [block_diag(kvs[hh][0] * kwf_scr[hh, d, rows, :][:, :ML_DQK]) for hh, d in head_dirs]
        for (hh, d), kw in zip(head_dirs, kws):
            dc_scr[hh, d, rows, :] = _dot_tn(kw, kvs[hh][1])
        return carry

    _loop(n_groups, delta_step, 0)

    def prefix_step(n, carry):
        new = []
        for (hh, d), c_aug in zip(head_dirs, carry):
            idx = n if d == 0 else n_chunks - 1 - n
            rows = _rows(idx, CHUNK)
            cin_scr[hh, d, rows, :] = c_aug
            new.append(c_aug * _wide(dec_scr[hh, d, pl.ds(idx, 1), :]) + dc_scr[hh, d, rows, :])
        return tuple(new)

    if zero_init:
        init = (jnp.zeros((ML_DQK, ML_AUG), F32),) * len(head_dirs)
    else:
        init = tuple(c0_ref[0, d, hh] for hh, d in head_dirs)
    fin = lax.fori_loop(0, n_chunks, prefix_step, init)

    def out_step(gi, carry):
        rows = _rows(gi, GROUP)
        kvs = [load_kv(rows, hh) for hh in range(hp)]
        qgs = [q_ref[rows, hh * ML_DQK:(hh + 1) * ML_DQK].astype(F32) * (ML_DQK ** -0.5) for hh in range(hp)]
        qks = [_dot_nt(qg, kv[0]) for qg, kv in zip(qgs, kvs)]
        q_bds = [block_diag(qg) for qg in qgs]
        inters = [_dot(q_bds[hh], cin_scr[hh, d, rows, :]) for hh, d in head_dirs]
        m_ts, b_ms, ss = [], [], []
        for hh, d in head_dirs:
            b_colb = bcb_scr[hh, d, rows, :]
            b_m = b_colb[:, 0:1] + mi_scr[hh, d, rows, :][:, 0:1]
            d_log = jnp.where(gm.incl[d], _wide(b_colb) - bcr_scr[hh, d, pl.ds(gi, 1), :]
                              + lir_scr[hh, d, pl.ds(gi, 1), :], -jnp.inf)
            m_t = jnp.maximum(b_m, jnp.max(d_log, axis=-1, keepdims=True))
            m_ts.append(m_t)
            b_ms.append(b_m)
            ss.append(qks[hh] * jnp.exp(d_log - m_t))
        intras = [_dot(s, kvs[hh][1]) for (hh, d), s in zip(head_dirs, ss)]
        hs = []
        for inter, intra, m_t, b_m in zip(inters, intras, m_ts, b_ms):
            num = jnp.exp(b_m - m_t) * inter + intra
            den = num[:, ML_DV:ML_DV + 1]
            hs.append(num[:, :ML_DV] / jnp.maximum(jnp.abs(den), jnp.exp(-m_t)))
        for hh in range(hp):
            o_ref[rows, hh * ML_DV:(hh + 1) * ML_DV] = hs[N_DIR * hh] + hs[N_DIR * hh + 1]
        return carry

    _loop(n_groups, out_step, 0)

    if emit_state:
        for i, (hh, d) in enumerate(head_dirs):
            cfin_ref[0, d, hh] = fin[i][:, :ML_DV]
            nfin_ref[0, d, hh] = fin[i][:, ML_DV:]
            mfin_ref[0, d, hh] = jnp.broadcast_to(m_fin[hh, d], (1, 128))


def _mlstm_scan(main, gt, batch, seq, b_i, b_f, state0, emit_state):
    n_tok = batch * seq
    n_chunks = seq // CHUNK
    nh = ML_HEADS
    hp = min(nh, max(2, SCAN_STEP_POSITIONS // seq))
    n_hb = nh // hp
    grow, gcol = _gate_layouts(gt, batch, nh, hp)
    zero_init = state0 is None
    smem = pl.BlockSpec(memory_space=pltpu.SMEM)
    k_off = (nh * ML_DQK) // (hp * ML_DQK)
    v_off = (2 * nh * ML_DQK) // (hp * ML_DV)
    c_spec = pl.BlockSpec((1, N_DIR, hp, ML_DQK, ML_AUG), lambda b, p: (b, 0, p, 0, 0))
    m_spec = pl.BlockSpec((1, N_DIR, hp, 1, 128), lambda b, p: (b, 0, p, 0, 0))
    in_specs = [smem, smem,
                pl.BlockSpec((seq, hp * ML_DQK), lambda b, p: (b, p)),
                pl.BlockSpec((seq, hp * ML_DQK), lambda b, p: (b, k_off + p)),
                pl.BlockSpec((seq, hp * ML_DV), lambda b, p: (b, v_off + p)),
                pl.BlockSpec((1, hp) + grow.shape[2:], lambda b, p: (b, p, 0, 0, 0, 0)),
                pl.BlockSpec((1, 1) + gcol.shape[2:], lambda b, p: (b, p, 0, 0))]
    args = [b_i, b_f, main, main, main, grow, gcol]
    if not zero_init:
        in_specs += [c_spec, m_spec]
        args += list(state0)
    out_specs = [pl.BlockSpec((seq, hp * ML_DV), lambda b, p: (b, p))]
    out_shape = [jax.ShapeDtypeStruct((n_tok, nh * ML_DV), F32)]
    if emit_state:
        half_spec = pl.BlockSpec((1, N_DIR, hp, ML_DQK, ML_DV), lambda b, p: (b, 0, p, 0, 0))
        out_specs += [half_spec, half_spec, m_spec]
        out_shape += [jax.ShapeDtypeStruct((batch, N_DIR, nh, ML_DQK, ML_DV), F32),
                      jax.ShapeDtypeStruct((batch, N_DIR, nh, ML_DQK, ML_AUG - ML_DV), F32),
                      jax.ShapeDtypeStruct((batch, N_DIR, nh, 1, 128), F32)]
    per_hd = lambda rows, width: pltpu.VMEM((hp, N_DIR, rows, width), F32)
    outs = pl.pallas_call(
        functools.partial(_mlstm_kernel, seq=seq, hp=hp, zero_init=zero_init, emit_state=emit_state),
        grid=(batch, n_hb),
        in_specs=in_specs, out_specs=out_specs, out_shape=out_shape,
        scratch_shapes=[per_hd(8, GROUP), per_hd(8, GROUP),
                        per_hd(seq, 128), per_hd(seq, 128), per_hd(seq, 128),
                        per_hd(max(n_chunks, 8), 128),
                        per_hd(seq, ML_AUG), per_hd(seq, ML_AUG)],
        compiler_params=_params("parallel", "parallel"),
        name="mlstm_scan",
    )(*args)
    return outs[0], (tuple(outs[1:4]) if emit_state else None)


def _trunk(x, batch, seq, rows, mods, tokens_per_cond, st_d, st_ml, emit_state, p):
    depth = p["w_ada"].shape[0]
    tile_cond = lambda tile: (lambda i: (i * tile) // tokens_per_cond)
    new_d = new_ml = None
    for layer in range(depth):
        mod = mods[layer]
        kind, j = layer % 3, layer // 3
        if kind == 0:
            x = _fnet_mix(x, seq, j, p["norm_mix"][layer], mod, tile_cond(seq), p["fnet_w"], p["fnet_b"][j])
        elif kind == 1:
            n_main = DN_HEADS * (2 * DN_DK + 2 * DN_DV)
            main, gt = _norm_proj(x, p["norm_mix"][layer], mod, tokens_per_cond, p["dn_w_in"][j], n_main)
            s0 = None if st_d is None else st_d[:, j]
            o, sfin = _gdn_scan(main, gt, batch, seq, p["dn_conv_w"][j], p["dn_a_log"][j], p["dn_dt_bias"][j],
                                s0, emit_state)
            if emit_state:
                new_d = sfin
            x = _gated_out(x, o, main, 3, p["dn_norm"][j], mod, tile_cond(OUT_TOKEN_TILE), p["dn_w_out"][j],
                           DN_HEADS, jax.nn.silu)
        else:
            n_main = 2 * ML_HEADS * ML_DQK + 2 * ML_HEADS * ML_DV
            main, gt = _norm_proj(x, p["norm_mix"][layer], mod, tokens_per_cond, p["ml_w_in"][j], n_main)
            state0 = None
            if st_ml is not None:
                c0, n0, m0 = (s[:, j] for s in st_ml)
                pad = jnp.zeros(c0.shape[:-1] + (ML_AUG - ML_DV - 1,), F32)
                c_aug0 = jnp.concatenate([c0, n0[..., None], pad], axis=-1)
                m0b = jnp.broadcast_to(m0[..., None, None], m0.shape + (1, 128))
                state0 = (c_aug0, m0b)
            o, fin = _mlstm_scan(main, gt, batch, seq, p["ml_b_i"][j], p["ml_b_f"][j], state0, emit_state)
            if emit_state:
                new_ml = (fin[0], fin[1][..., 0], fin[2][..., 0, 0])
            x = _gated_out(x, o, main, 2, p["ml_norm"][j], mod, tile_cond(OUT_TOKEN_TILE), p["ml_w_out"][j],
                           ML_HEADS, jax.nn.sigmoid)
        x = _conv_ffn(x, seq, rows, layer, p["norm_ffn"][layer], mod, tile_cond(FFN_TOKEN_TILE), p["ffn_w_up"],
                      p["ffn_conv_w"], p["ffn_conv_b"], p["ffn_w_down"], p["norm_final"],
                      final_norm=(layer == depth - 1))
    return x, new_d, new_ml


def kernel(x_prompt, x_sample, state_delta, state_mlstm_c, state_mlstm_n, state_mlstm_m, c, c_ctx, w_ada, b_ada, norm_mix, norm_ffn, norm_final, ffn_w_up, ffn_conv_w, ffn_conv_b, ffn_w_down, fnet_w, fnet_b, dn_w_in, dn_conv_w, dn_a_log, dn_dt_bias, dn_norm, dn_w_out, ml_w_in, ml_b_i, ml_b_f, ml_norm, ml_w_out):
    p = dict(w_ada=w_ada, norm_mix=norm_mix, norm_ffn=norm_ffn, norm_final=norm_final, ffn_w_up=ffn_w_up,
             ffn_conv_w=ffn_conv_w, ffn_conv_b=ffn_conv_b, ffn_w_down=ffn_w_down, fnet_w=fnet_w, fnet_b=fnet_b,
             dn_w_in=dn_w_in, dn_conv_w=dn_conv_w, dn_a_log=dn_a_log, dn_dt_bias=dn_dt_bias, dn_norm=dn_norm,
             dn_w_out=dn_w_out, ml_w_in=ml_w_in, ml_b_i=ml_b_i, ml_b_f=ml_b_f, ml_norm=ml_norm, ml_w_out=ml_w_out)
    b_ctx, t_ctx, _ = x_prompt.shape
    b_smp, t_smp, _ = x_sample.shape
    depth = w_ada.shape[0]

    cond8 = jnp.concatenate([c_ctx[None, :], c, jnp.zeros((8 - 1 - b_smp, D_MODEL), F32)], axis=0)
    mods = _adaln_all(cond8, w_ada, b_ada)
    mods_ctx = [mods[l, 0:1].reshape(1, 1, -1) for l in range(depth)]
    mods_smp = [mods[l, 1:1 + b_smp].reshape(b_smp, 1, -1) for l in range(depth)]

    y_ctx, new_d, new_ml = _trunk(x_prompt.reshape(b_ctx * t_ctx, D_MODEL), b_ctx, t_ctx, 1, mods_ctx,
                                  b_ctx * t_ctx, None, None, True, p)
    y_smp, _, _ = _trunk(x_sample.reshape(b_smp * t_smp, D_MODEL), b_smp, t_smp, t_smp // GRID_W, mods_smp,
                         t_smp, state_delta, (state_mlstm_c, state_mlstm_n, state_mlstm_m), False, p)

    new_c, new_n, new_m = new_ml
    return (y_ctx.reshape(b_ctx, t_ctx, D_MODEL), y_smp.reshape(b_smp, t_smp, D_MODEL),
            new_d[:, None], new_c[:, None], new_n[:, None], new_m[:, None])
```

```python
import collections
import functools

import numpy as np
import jax
import jax.numpy as jnp
from jax import lax
from jax.experimental import pallas as pl
from jax.experimental.pallas import tpu as pltpu

F32 = jnp.float32
BF16 = jnp.bfloat16
HIGHEST = lax.Precision.HIGHEST

D_MODEL = 1024
EPS = 1e-6
N_DIR = 2
CHUNK = 64
LOG_CHUNK = 6
GROUP = 256
CHUNKS_PER_GROUP = GROUP // CHUNK
FNET_GROUP_DIM = 256
DN_HEADS, DN_DK, DN_DV, DN_CONV = 8, 128, 128, 5
ML_HEADS, ML_DQK, ML_DV = 8, 64, 128
D_FF = 2816
GRID_W = 64

FFN_TOKEN_TILE = 1024
FFN_FF_TILE = 256
FFN_SUB_TILES = 4
PROJ_TOKEN_TILE = 2048
PROJ_COL_TILE = 512
OUT_TOKEN_TILE = 512
ADA_COL_TILE = 1536
SCAN_STEP_POSITIONS = 1024
VMEM_LIMIT = 56 * 1024 * 1024


def _params(*sem):
    return pltpu.CompilerParams(dimension_semantics=sem, vmem_limit_bytes=VMEM_LIMIT)


def _dot(a, b):
    return jnp.dot(a.astype(BF16), b.astype(BF16), preferred_element_type=F32)


def _dot_nt(a, b):
    return lax.dot_general(a.astype(BF16), b.astype(BF16), (((1,), (1,)), ((), ())),
                           preferred_element_type=F32)


def _dot_tn(a, b):
    return lax.dot_general(a.astype(BF16), b.astype(BF16), (((0,), (0,)), ((), ())),
                           preferred_element_type=F32)


def _dot_f32(a, b):
    return jnp.dot(a, b, precision=HIGHEST, preferred_element_type=F32)


def _rms(x, g):
    return x * lax.rsqrt(jnp.mean(x * x, axis=-1, keepdims=True) + EPS) * g


def _norm_mod(x, g, sc, sh):
    return _rms(x, g) * (1.0 + sc) + sh


def _softplus(x):
    return jnp.maximum(x, 0.0) + jnp.log1p(jnp.exp(-jnp.abs(x)))


def _mod_spec(chunk, cond_of_tile):
    return pl.BlockSpec((1, 1, D_MODEL), lambda i, *_: (cond_of_tile(i), 0, chunk))


def _adaln_kernel(c_ref, w_ref, b_ref, o_ref):
    s = jax.nn.silu(c_ref[...])
    o_ref[0] = _dot(s, w_ref[0]) + b_ref[0]


def _adaln_all(cond8, w_ada, b_ada):
    depth = w_ada.shape[0]
    n_out = w_ada.shape[2]
    tn = ADA_COL_TILE
    return pl.pallas_call(
        _adaln_kernel,
        grid=(depth, n_out // tn),
        in_specs=[pl.BlockSpec((8, D_MODEL), lambda l, j: (0, 0)),
                  pl.BlockSpec((1, D_MODEL, tn), lambda l, j: (l, 0, j)),
                  pl.BlockSpec((1, 1, tn), lambda l, j: (l, 0, j))],
        out_specs=pl.BlockSpec((1, 8, tn), lambda l, j: (l, 0, j)),
        out_shape=jax.ShapeDtypeStruct((depth, 8, n_out), F32),
        compiler_params=_params("parallel", "parallel"),
        name="adaln",
    )(cond8, w_ada, b_ada.reshape(depth, 1, n_out))


def _norm_proj_kernel(x_ref, g_ref, sc_ref, sh_ref, w_ref, wg_ref, o_ref, gate_ref, *rest, t_tile):
    h_scr = rest[-1]

    @pl.when(pl.program_id(1) == 0)
    def _():
        h = _norm_mod(x_ref[...], g_ref[...], sc_ref[0], sh_ref[0]).astype(BF16)
        h_scr[...] = h
        gate_ref[...] = _dot(h, wg_ref[...])

    y = _dot(h_scr[...], w_ref[...])
    o_ref[...] = y.astype(o_ref.dtype)

    if t_tile is not None:
        t_ref = rest[0]

        @pl.when(pl.program_id(1) == t_tile)
        def _():
            for g in range(t_ref.shape[0]):
                t_ref[g] = y[g * GROUP:(g + 1) * GROUP, :].T


def _norm_proj(x, norm_g, mod, tokens_per_cond, w_in, n_main, t_tile=None):
    n_tok = x.shape[0]
    tm, tn = min(PROJ_TOKEN_TILE, tokens_per_cond), PROJ_COL_TILE
    cond_of_tile = lambda i: (i * tm) // tokens_per_cond
    n_gate = w_in.shape[1] - n_main
    out_specs = [pl.BlockSpec((tm, tn), lambda i, j: (i, j)),
                 pl.BlockSpec((tm, n_gate), lambda i, j: (i, 0))]
    out_shape = [jax.ShapeDtypeStruct((n_tok, n_main), BF16),
                 jax.ShapeDtypeStruct((n_tok, n_gate), F32)]
    if t_tile is not None:
        out_specs.append(pl.BlockSpec((tm // GROUP, tn, GROUP), lambda i, j: (i, 0, 0)))
        out_shape.append(jax.ShapeDtypeStruct((n_tok // GROUP, tn, GROUP), F32))
    return pl.pallas_call(
        functools.partial(_norm_proj_kernel, t_tile=t_tile),
        grid=(n_tok // tm, n_main // tn),
        in_specs=[pl.BlockSpec((tm, D_MODEL), lambda i, j: (i, 0)),
                  pl.BlockSpec((1, D_MODEL), lambda i, j: (0, 0)),
                  _mod_spec(1, cond_of_tile), _mod_spec(0, cond_of_tile),
                  pl.BlockSpec((D_MODEL, tn), lambda i, j: (0, j)),
                  pl.BlockSpec((D_MODEL, n_gate), lambda i, j: (0, 0))],
        out_specs=out_specs, out_shape=out_shape,
        scratch_shapes=[pltpu.VMEM((tm, D_MODEL), BF16)],
        compiler_params=_params("parallel", "arbitrary"),
        name="norm_proj",
    )(x, norm_g.reshape(1, D_MODEL), mod, mod, w_in, w_in[:, n_main:])


def _gated_out_kernel(x_ref, o_ref, z_ref, ng_ref, g1_ref, w_ref, y_ref, wb_scr, *, n_heads, gate_fn):
    @pl.when(pl.program_id(0) == 0)
    def _():
        wb_scr[...] = w_ref[...].astype(BF16)

    dv = o_ref.shape[1] // n_heads
    parts = []
    for h in range(n_heads):
        sl = slice(h * dv, (h + 1) * dv)
        parts.append((_rms(o_ref[:, sl], ng_ref[...]) * gate_fn(z_ref[:, sl].astype(F32))).astype(BF16))
    hs = jnp.concatenate(parts, axis=1)
    y = jnp.dot(hs, wb_scr[...], preferred_element_type=F32)
    y_ref[...] = x_ref[...] + g1_ref[0] * y


def _gated_out(x, o, main, z_block, norm_g, mod, cond_of_tile, w_out, n_heads, gate_fn):
    n_tok = x.shape[0]
    tm = OUT_TOKEN_TILE
    width = o.shape[1]
    return pl.pallas_call(
        functools.partial(_gated_out_kernel, n_heads=n_heads, gate_fn=gate_fn),
        grid=(n_tok // tm,),
        in_specs=[pl.BlockSpec((tm, D_MODEL), lambda i: (i, 0)),
                  pl.BlockSpec((tm, width), lambda i: (i, 0)),
                  pl.BlockSpec((tm, width), lambda i: (i, z_block)),
                  pl.BlockSpec((1, width // n_heads), lambda i: (0, 0)),
                  _mod_spec(2, cond_of_tile),
                  pl.BlockSpec((width, D_MODEL), lambda i: (0, 0))],
        out_specs=pl.BlockSpec((tm, D_MODEL), lambda i: (i, 0)),
        out_shape=jax.ShapeDtypeStruct((n_tok, D_MODEL), F32),
        scratch_shapes=[pltpu.VMEM((width, D_MODEL), BF16)],
        compiler_params=_params("arbitrary"),
        name="gated_out",
    )(x, o, main, norm_g.reshape(1, -1), mod, w_out)


def _dft_mats(n):
    k = np.arange(n, dtype=np.int64)
    ang = 2.0 * np.pi * ((k[:, None] * k[None, :]) % n).astype(np.float64) / n
    s = 1.0 / np.sqrt(n)
    return np.cos(ang) * s, np.sin(ang) * s


def _fnet_kernel(x_ref, ng_ref, sc_ref, sh_ref, g1_ref, ct_ref, st_ref, cs_ref, w_ref, b_ref, y_ref, wb_scr):
    @pl.when(pl.program_id(0) == 0)
    def _():
        wb_scr[...] = w_ref[...].astype(BF16)

    x = x_ref[...]
    h = _norm_mod(x, ng_ref[...], sc_ref[0], sh_ref[0]).astype(BF16)
    p = jnp.dot(ct_ref[...], h, preferred_element_type=F32)
    q = jnp.dot(st_ref[...], h, preferred_element_type=F32)
    gd = FNET_GROUP_DIM
    parts = []
    for g in range(D_MODEL // gd):
        sl = slice(g * gd, (g + 1) * gd)
        pq = jnp.concatenate([p[:, sl], q[:, sl]], axis=1).astype(BF16)
        parts.append(jnp.dot(pq, cs_ref[...], preferred_element_type=F32).astype(BF16))
    f = jnp.concatenate(parts, axis=1)
    y = jnp.dot(f, wb_scr[...], preferred_element_type=F32) + b_ref[...]
    y_ref[...] = x + g1_ref[0] * y


def _fnet_mix(x, seq, layer, norm_g, mod, cond_of_seq, w, b):
    n_tok = x.shape[0]
    ct, st = _dft_mats(seq)
    cc, sc = _dft_mats(FNET_GROUP_DIM)
    cs = np.concatenate([cc, -sc], axis=0)
    const = lambda i: (0, 0)
    ct, st, cs = (jnp.asarray(m, F32).astype(BF16) for m in (ct, st, cs))
    return pl.pallas_call(
        _fnet_kernel,
        grid=(n_tok // seq,),
        in_specs=[pl.BlockSpec((seq, D_MODEL), lambda i: (i, 0)),
                  pl.BlockSpec((1, D_MODEL), const),
                  _mod_spec(1, cond_of_seq), _mod_spec(0, cond_of_seq), _mod_spec(2, cond_of_seq),
                  pl.BlockSpec((seq, seq), const),
                  pl.BlockSpec((seq, seq), const),
                  pl.BlockSpec((2 * FNET_GROUP_DIM, FNET_GROUP_DIM), const),
                  pl.BlockSpec((None, D_MODEL, D_MODEL), lambda i: (layer, 0, 0)),
                  pl.BlockSpec((1, D_MODEL), const)],
        out_specs=pl.BlockSpec((seq, D_MODEL), lambda i: (i, 0)),
        out_shape=jax.ShapeDtypeStruct((n_tok, D_MODEL), F32),
        scratch_shapes=[pltpu.VMEM((D_MODEL, D_MODEL), BF16)],
        compiler_params=_params("arbitrary"),
        name="fnet",
    )(x, norm_g.reshape(1, D_MODEL), mod, mod, mod,
      ct, st, cs, w, b.reshape(1, D_MODEL))


def _shift_rows(x, delta):
    n = x.shape[0]
    return pltpu.roll(x, (-delta) % n, 0)


def _dwconv_tokens(g, cw_ref, seq, rows):
    tm = g.shape[0]
    width = seq // rows
    assert seq & (seq - 1) == 0 and width & (width - 1) == 0
    t = lax.broadcasted_iota(jnp.int32, (tm, 1), 0)
    pos = t & (seq - 1)
    col = pos & (width - 1)
    row = pos >> (width.bit_length() - 1)
    g_cols = (_shift_rows(g, -1) * (col >= 1).astype(F32), g,
              _shift_rows(g, 1) * (col <= width - 2).astype(F32))
    out = None
    for di in (-1, 0, 1):
        if rows == 1 and di != 0:
            continue
        r = sum(g_cols[dj] * cw_ref[pl.ds(3 * (di + 1) + dj, 1), :] for dj in range(3))
        if di != 0:
            ok = (row + di >= 0) & (row + di <= rows - 1)
            r = _shift_rows(r, di * width) * ok.astype(F32)
        out = r if out is None else out + r
    return out


def _ffn_kernel(x_ref, ng_ref, sc_ref, sh_ref, g2_ref, wa_ref, wg_ref, cw_ref, cb_ref, wd_ref, nf_ref,
                y_ref, h_scr, acc_scr, *, seq, rows, final_norm):
    j = pl.program_id(1)

    @pl.when(j == 0)
    def _():
        h_scr[...] = _norm_mod(x_ref[...], ng_ref[...], sc_ref[0], sh_ref[0]).astype(BF16)
        acc_scr[...] = jnp.zeros_like(acc_scr)

    tm = h_scr.shape[0]
    sub = tm // FFN_SUB_TILES
    parts = [pl.ds(r, sub) for r in range(0, tm, sub)]
    wa, wg, wd = (w[...].astype(BF16) for w in (wa_ref, wg_ref, wd_ref))
    gs = [jnp.dot(h_scr[p, :], wg, preferred_element_type=F32) for p in parts]
    as_ = [jnp.dot(h_scr[p, :], wa, preferred_element_type=F32) for p in parts]
    per_seq = max(seq // sub, 1)
    convs = []
    for i in range(0, len(parts), per_seq):
        g = gs[i] if per_seq == 1 else jnp.concatenate(gs[i:i + per_seq], axis=0)
        c = _dwconv_tokens(g, cw_ref, seq, rows) + cb_ref[...]
        convs += [c[k * sub:(k + 1) * sub] for k in range(per_seq)]
    acts = [(jax.nn.silu(c) * a).astype(BF16) for c, a in zip(convs, as_)]
    for p, act in zip(parts, acts):
        acc_scr[p, :] += jnp.dot(act, wd, preferred_element_type=F32)

    @pl.when(j == pl.num_programs(1) - 1)
    def _():
        y = x_ref[...] + g2_ref[0] * acc_scr[...]
        if final_norm:
            y = _rms(y, nf_ref[...])
        y_ref[...] = y


def _conv_ffn(x, seq, rows, layer, norm_g, mod, cond_of_tile, w_up, conv_w, conv_b, w_down, norm_final, final_norm):
    n_tok = x.shape[0]
    tm, tf = FFN_TOKEN_TILE, FFN_FF_TILE
    n_ff_tiles = D_FF // tf
    depth = w_up.shape[0]
    return pl.pallas_call(
        functools.partial(_ffn_kernel, seq=seq, rows=rows, final_norm=final_norm),
        grid=(n_tok // tm, n_ff_tiles),
        in_specs=[pl.BlockSpec((tm, D_MODEL), lambda i, j: (i, 0)),
                  pl.BlockSpec((1, D_MODEL), lambda i, j: (0, 0)),
                  _mod_spec(4, cond_of_tile), _mod_spec(3, cond_of_tile), _mod_spec(5, cond_of_tile),
                  pl.BlockSpec((None, D_MODEL, tf), lambda i, j: (layer, 0, j)),
                  pl.BlockSpec((None, D_MODEL, tf), lambda i, j: (layer, 0, n_ff_tiles + j)),
                  pl.BlockSpec((None, 9, tf), lambda i, j: (layer, 0, j)),
                  pl.BlockSpec((None, 1, tf), lambda i, j: (layer, 0, j)),
                  pl.BlockSpec((None, tf, D_MODEL), lambda i, j: (layer, j, 0)),
                  pl.BlockSpec((1, D_MODEL), lambda i, j: (0, 0))],
        out_specs=pl.BlockSpec((tm, D_MODEL), lambda i, j: (i, 0)),
        out_shape=jax.ShapeDtypeStruct((n_tok, D_MODEL), F32),
        scratch_shapes=[pltpu.VMEM((tm, D_MODEL), BF16), pltpu.VMEM((tm, D_MODEL), F32)],
        compiler_params=_params("parallel", "arbitrary"),
        name="conv_ffn",
    )(x, norm_g.reshape(1, D_MODEL), mod, mod, mod, w_up, w_up, conv_w.reshape(depth, 9, D_FF),
      conv_b.reshape(depth, 1, D_FF), w_down, norm_final.reshape(1, D_MODEL))


GroupMasks = collections.namedtuple("GroupMasks", "r c same incl strict")


def _chunk_masks():
    r = lax.broadcasted_iota(jnp.int32, (CHUNK, CHUNK), 0)
    c = lax.broadcasted_iota(jnp.int32, (CHUNK, CHUNK), 1)
    return (r >= c, r <= c)


def _group_masks():
    r = lax.broadcasted_iota(jnp.int32, (GROUP, GROUP), 0)
    c = lax.broadcasted_iota(jnp.int32, (GROUP, GROUP), 1)
    same = (r >> LOG_CHUNK) == (c >> LOG_CHUNK)
    return GroupMasks(r, c, same, (same & (r >= c), same & (r <= c)), (same & (r > c), same & (r < c)))


def _gate_layouts(gt, batch, n_heads, hp):
    seq = gt.shape[1] // batch
    n_groups, n_chunks = seq // GROUP, seq // CHUNK
    assert seq % GROUP == 0 and n_groups <= 8
    g = jnp.transpose(gt.reshape(N_DIR, 2, n_heads, batch, seq), (3, 2, 0, 1, 4))
    rows = g.reshape(g.shape[:4] + (n_groups, GROUP))
    rows = jnp.pad(rows, ((0, 0),) * 4 + ((0, 8 - n_groups), (0, 0)))
    cols = jnp.swapaxes(g.reshape(batch, n_heads // hp, hp * N_DIR * 2 * n_chunks, CHUNK), -1, -2)
    return rows, cols


def _col_gate(gcol_ref, hh, d, kind, n_chunks):
    start = ((hh * N_DIR + d) * 2 + kind) * n_chunks
    return gcol_ref[0, 0, :, start:start + n_chunks]


def _cumsum_rows(row, d, gm):
    return _dot_f32(row, gm.incl[1 - d].astype(F32))


def _cumsum_cols(col, d, cm):
    return _dot_f32(cm[d].astype(F32), col)


def _store_cols(dst, col, n_chunks):
    for n in range(n_chunks):
        dst[n * CHUNK:(n + 1) * CHUNK, :] = jnp.broadcast_to(col[:, n:n + 1], (CHUNK, 128))


def _wide(x):
    return jnp.concatenate([x, x], axis=1)


def _rows(i, size):
    if isinstance(i, int):
        return pl.ds(i * size, size)
    return pl.ds(pl.multiple_of(i * size, size), size)


def _loop(n, body, init):
    if n == 1:
        return body(0, init)
    return lax.fori_loop(0, n, body, init)


def _conv_silu(x, w_ref):
    n = x.shape[0]
    k = w_ref.shape[0]
    t = lax.broadcasted_iota(jnp.int32, (n, 1), 0)
    acc = None
    for j in range(k):
        delta = j - k // 2
        if delta == 0:
            term = x
        else:
            ok = (t + delta >= 0) & (t + delta <= n - 1)
            term = _shift_rows(x, delta) * ok.astype(F32)
        term = term * w_ref[pl.ds(j, 1), :]
        acc = term if acc is None else acc + term
    return jax.nn.silu(acc)


def _l2norm(x):
    return x * lax.rsqrt(jnp.sum(x * x, axis=-1, keepdims=True) + EPS)


def _blockdiag_tri_inverse(mats, gm):
    base = (gm.r >> 3) == (gm.c >> 3)
    eye = jnp.where(gm.r == gm.c, 1.0, 0.0)
    ps = [-jnp.where(base, a, 0.0) for a in mats]
    invs = [eye + p for p in ps]
    for _ in range(2):
        ps = [_dot(p, p) for p in ps]
        invs = [inv + _dot(inv, p) for inv, p in zip(invs, ps)]
    for s in (3, 4, 5):
        join = ((gm.r >> s) ^ (gm.c >> s)) == 1
        tmp = [_dot(jnp.where(join, a, 0.0), inv) for a, inv in zip(mats, invs)]
        invs = [inv - _dot(inv, t) for inv, t in zip(invs, tmp)]
    return invs


def _gdn_kernel(*refs, seq, hp, zero_init, emit_state):
    alog_ref, dtb_ref, q_ref, k_ref, v_ref, cwq_ref, cwk_ref, cwv_ref, grow_ref, gcol_ref = refs[:10]
    pos = 10
    s0_ref = None
    if not zero_init:
        s0_ref = refs[pos]
        pos += 1
    o_ref = refs[pos]
    pos += 1
    sfin_ref = None
    if emit_state:
        sfin_ref = refs[pos]
        pos += 1
    (q_scr, k_scr, v_scr, gcr_scr, gcb_scr, bb_scr, gl_scr,
     u_scr, w_scr, qd_scr, kd_scr, qkd_scr) = refs[pos:]

    n_chunks = seq // CHUNK
    cm = _chunk_masks()
    gm = _group_masks()
    head_dirs = [(hh, d) for hh in range(hp) for d in range(N_DIR)]
    lanes = lambda hh: slice(hh * DN_DK, (hh + 1) * DN_DK)

    q_scr[...] = _conv_silu(q_ref[...].astype(F32), cwq_ref)
    k_scr[...] = _conv_silu(k_ref[...].astype(F32), cwk_ref)
    v_scr[...] = _conv_silu(v_ref[...].astype(F32), cwv_ref)
    for hh in range(hp):
        q_scr[:, lanes(hh)] = _l2norm(q_scr[:, lanes(hh)]) * (DN_DK ** -0.5)
        k_scr[:, lanes(hh)] = _l2norm(k_scr[:, lanes(hh)])

    for hh, d in head_dirs:
        head = pl.program_id(1) * hp + hh
        neg_a = -jnp.exp(jnp.full((1, 1), alog_ref[d, head], F32))
        dtb = dtb_ref[d, head]
        gcr_scr[hh, d] = _cumsum_rows(neg_a * _softplus(grow_ref[0, hh, d, 0] + dtb), d, gm)
        gc_col = _cumsum_cols(neg_a * _softplus(_col_gate(gcol_ref, hh, d, 0, n_chunks) + dtb), d, cm)
        _store_cols(gcb_scr.at[hh, d], gc_col, n_chunks)
        _store_cols(bb_scr.at[hh, d], jax.nn.sigmoid(_col_gate(gcol_ref, hh, d, 1, n_chunks)), n_chunks)
        last = CHUNK - 1 if d == 0 else 0
        _store_cols(gl_scr.at[hh, d], jnp.broadcast_to(gc_col[last:last + 1, :], gc_col.shape), n_chunks)

    def group_step(gi, carry):
        rows = _rows(gi, GROUP)
        kqs = []
        for hh in range(hp):
            kg, qg = k_scr[rows, lanes(hh)], q_scr[rows, lanes(hh)]
            kqs.append(_dot_nt(jnp.concatenate([kg, qg], axis=0), kg))
        mats = []
        for hh, d in head_dirs:
            gcb = gcb_scr[hh, d, rows, :]
            gcr = gcr_scr[hh, d, pl.ds(gi, 1), :]
            decay = jnp.exp(jnp.where(gm.incl[d], _wide(gcb) - gcr, -jnp.inf))
            mats.append(jnp.where(gm.strict[d], kqs[hh][:GROUP] * _wide(bb_scr[hh, d, rows, :]) * decay, 0.0))
            qkd = kqs[hh][GROUP:] * decay
            qkd_scr[hh, d, rows, :] = sum(qkd[:, n * CHUNK:(n + 1) * CHUNK] for n in range(CHUNKS_PER_GROUP))
        t_invs = _blockdiag_tri_inverse(mats, gm)
        for (hh, d), t_inv in zip(head_dirs, t_invs):
            qg, kg, vg = q_scr[rows, lanes(hh)], k_scr[rows, lanes(hh)], v_scr[rows, lanes(hh)]
            gcb = gcb_scr[hh, d, rows, :]
            bb = bb_scr[hh, d, rows, :]
            e_gc = jnp.exp(gcb)
            uw = _dot(t_inv, jnp.concatenate([vg * bb, kg * bb * e_gc], axis=1))
            u_scr[hh, d, rows, :] = uw[:, :DN_DV]
            w_scr[hh, d, rows, :] = uw[:, DN_DV:]
            qd_scr[hh, d, rows, :] = qg * e_gc
            kd_scr[hh, d, rows, :] = kg * jnp.exp(gl_scr[hh, d, rows, :] - gcb)
        return carry

    _loop(seq // GROUP, group_step, 0)

    o_ref[...] = jnp.zeros_like(o_ref)

    def chunk_step(n, carry):
        idxs = [n if d == 0 else n_chunks - 1 - n for _, d in head_dirs]
        rows = [_rows(idx, CHUNK) for idx in idxs]
        wqs = [_dot(jnp.concatenate([w_scr[hh, d, r, :], qd_scr[hh, d, r, :]], axis=0), s)
               for (hh, d), r, s in zip(head_dirs, rows, carry)]
        v_news = [u_scr[hh, d, r, :] - wq[:CHUNK] for (hh, d), r, wq in zip(head_dirs, rows, wqs)]
        o_ns = [wq[CHUNK:] + _dot(qkd_scr[hh, d, r, :], v_new)
                for (hh, d), r, wq, v_new in zip(head_dirs, rows, wqs, v_news)]
        new = []
        for (hh, d), idx, r, s, v_new in zip(head_dirs, idxs, rows, carry, v_news):
            s_decay = jnp.exp(gl_scr[hh, d, pl.ds(idx * CHUNK, 1), :])
            new.append(s * s_decay + _dot_tn(kd_scr[hh, d, r, :], v_new))
        for (hh, d), r, o_n in zip(head_dirs, rows, o_ns):
            o_ref[r, lanes(hh)] += o_n
        return tuple(new)

    if zero_init:
        init = (jnp.zeros((DN_DK, DN_DV), F32),) * len(head_dirs)
    else:
        init = tuple(s0_ref[0, d, hh] for hh, d in head_dirs)
    fin = lax.fori_loop(0, n_chunks, chunk_step, init)
    if emit_state:
        for i, (hh, d) in enumerate(head_dirs):
            sfin_ref[0, d, hh] = fin[i]


def _gdn_scan(main, gt, batch, seq, conv_w, a_log, dt_bias, s0, emit_state):
    n_tok = batch * seq
    nh = DN_HEADS
    hp = min(nh, max(2, SCAN_STEP_POSITIONS // seq))
    n_hb = nh // hp
    grow, gcol = _gate_layouts(gt, batch, nh, hp)
    zero_init = s0 is None
    smem = pl.BlockSpec(memory_space=pltpu.SMEM)
    qkv_spec = lambda off: pl.BlockSpec((seq, hp * DN_DK), lambda b, h: (b, off + h))
    cw_spec = lambda off: pl.BlockSpec((DN_CONV, hp * DN_DK), lambda b, h: (0, off + h))
    state_spec = pl.BlockSpec((1, N_DIR, hp, DN_DK, DN_DV), lambda b, h: (b, 0, h, 0, 0))
    in_specs = [smem, smem, qkv_spec(0), qkv_spec(n_hb), qkv_spec(2 * n_hb),
                cw_spec(0), cw_spec(n_hb), cw_spec(2 * n_hb),
                pl.BlockSpec((1, hp) + grow.shape[2:], lambda b, h: (b, h, 0, 0, 0, 0)),
                pl.BlockSpec((1, 1) + gcol.shape[2:], lambda b, h: (b, h, 0, 0))]
    args = [a_log, dt_bias, main, main, main, conv_w, conv_w, conv_w, grow, gcol]
    if not zero_init:
        in_specs.append(state_spec)
        args.append(s0)
    out_specs = [pl.BlockSpec((seq, hp * DN_DV), lambda b, h: (b, h))]
    out_shape = [jax.ShapeDtypeStruct((n_tok, nh * DN_DV), F32)]
    if emit_state:
        out_specs.append(state_spec)
        out_shape.append(jax.ShapeDtypeStruct((batch, N_DIR, nh, DN_DK, DN_DV), F32))
    per_dir = lambda width: pltpu.VMEM((hp, N_DIR, seq, width), F32)
    qkv_scr = pltpu.VMEM((seq, hp * DN_DK), F32)
    outs = pl.pallas_call(
        functools.partial(_gdn_kernel, seq=seq, hp=hp, zero_init=zero_init, emit_state=emit_state),
        grid=(batch, n_hb),
        in_specs=in_specs, out_specs=out_specs, out_shape=out_shape,
        scratch_shapes=[qkv_scr, qkv_scr, qkv_scr,
                        pltpu.VMEM((hp, N_DIR, 8, GROUP), F32),
                        per_dir(128), per_dir(128), per_dir(128),
                        per_dir(DN_DV), per_dir(DN_DK), per_dir(DN_DK), per_dir(DN_DK), per_dir(CHUNK)],
        compiler_params=_params("parallel", "parallel"),
        name="gdn_scan",
    )(*args)
    return outs[0], (outs[1] if emit_state else None)


ML_AUG = 2 * ML_DV


def _mlstm_kernel(*refs, seq, hp, zero_init, emit_state):
    bi_ref, bf_ref, q_ref, k_ref, v_ref, kt_ref, grow_ref, gcol_ref = refs[:8]
    pos = 8
    c0_ref = m0_ref = None
    if not zero_init:
        c0_ref, m0_ref = refs[pos:pos + 2]
        pos += 2
    o_ref = refs[pos]
    pos += 1
    cfin_ref = nfin_ref = mfin_ref = None
    if emit_state:
        cfin_ref, nfin_ref, mfin_ref = refs[pos:pos + 3]
        pos += 3
    bcr_scr, lir_scr, kwr_scr, bcb_scr, mi_scr, dec_scr, dc_scr, cin_scr = refs[pos:]

    n_chunks = seq // CHUNK
    n_groups = seq // GROUP
    head0 = pl.program_id(1) * hp
    cm = _chunk_masks()
    gm = _group_masks()
    ones_col = jnp.where(lax.broadcasted_iota(jnp.int32, (GROUP, ML_DV), 1) == 0, 1.0, 0.0)
    head_dirs = [(hh, d) for hh in range(hp) for d in range(N_DIR)]

    m_fin = {}
    row_group = lax.broadcasted_iota(jnp.int32, (8, GROUP), 0)
    row_chunk = lax.broadcasted_iota(jnp.int32, (8, GROUP), 1) >> LOG_CHUNK
    for hh, d in head_dirs:
        b_i = bi_ref[d, head0 + hh]
        b_f = bf_ref[d, head0 + hh]
        li_row = grow_ref[0, hh, d, 0] + b_i
        lf_row = -_softplus(-(grow_ref[0, hh, d, 1] + b_f))
        bc_row = _cumsum_rows(lf_row, d, gm)
        lir_scr[hh, d] = li_row
        bcr_scr[hh, d] = bc_row
        w_row = _dot_f32(lf_row, gm.same.astype(F32)) - bc_row + li_row
        m_out_row = jnp.zeros((8, GROUP), F32)
        li_col = _col_gate(gcol_ref, hh, d, 0, n_chunks) + b_i
        bc_col = _cumsum_cols(-_softplus(-(_col_gate(gcol_ref, hh, d, 1, n_chunks) + b_f)), d, cm)
        last = CHUNK - 1 if d == 0 else 0
        b_last = bc_col[last:last + 1, :]
        w_col = b_last - bc_col + li_col
        w_max = jnp.max(w_col, axis=0, keepdims=True)
        m = jnp.zeros((1, 1), F32) if zero_init else m0_ref[0, d, hh][:, 0:1]
        for step in range(n_chunks):
            n = step if d == 0 else n_chunks - 1 - step
            sl = slice(n * CHUNK, (n + 1) * CHUNK)
            m_new = jnp.maximum(b_last[:, n:n + 1] + m, w_max[:, n:n + 1])
            bcb_scr[hh, d, sl, :] = jnp.broadcast_to(bc_col[:, n:n + 1], (CHUNK, 128))
            mi_scr[hh, d, sl, :] = jnp.broadcast_to(m, (CHUNK, 128))
            dec_scr[hh, d, n:n + 1, :] = jnp.broadcast_to(jnp.exp(b_last[:, n:n + 1] + m - m_new), (1, 128))
            in_chunk = (row_group == n // CHUNKS_PER_GROUP) & (row_chunk == n % CHUNKS_PER_GROUP)
            m_out_row = jnp.where(in_chunk, m_new, m_out_row)
            m = m_new
        m_fin[hh, d] = m
        kwr_scr[hh, d] = jnp.exp(w_row - m_out_row)

    def load_kv(rows, hh):
        kg = k_ref[rows, hh * ML_DQK:(hh + 1) * ML_DQK].astype(F32)
        v_aug = jnp.concatenate([v_ref[rows, hh * ML_DV:(hh + 1) * ML_DV].astype(F32), ones_col], axis=1)
        return kg, v_aug

    def delta_step(gi, carry):
        rows = _rows(gi, GROUP)
        v_augs = [load_kv(rows, hh)[1] for hh in range(hp)]
        kt = kt_ref[gi]
        lhs = []
        for hh, d in head_dirs:
            kw_t = kt[hh * ML_DQK:(hh + 1) * ML_DQK, :] * kwr_scr[hh, d, pl.ds(gi, 1), :]
            lhs.append(jnp.where(gm.same, jnp.concatenate([kw_t] * CHUNKS_PER_GROUP, axis=0), 0.0))
        for (hh, d), kw in zip(head_dirs, lhs):
            dc_scr[hh, d, rows, :] = _dot(kw, v_augs[hh])
        return carry

    _loop(n_groups, delta_step, 0)

    def prefix_step(n, carry):
        new = []
        for (hh, d), c_aug in zip(head_dirs, carry):
            idx = n if d == 0 else n_chunks - 1 - n
            rows = _rows(idx, CHUNK)
            cin_scr[hh, d, rows, :] = c_aug
            new.append(c_aug * _wide(dec_scr[hh, d, pl.ds(idx, 1), :]) + dc_scr[hh, d, rows, :])
        return tuple(new)

    if zero_init:
        init = (jnp.zeros((ML_DQK, ML_AUG), F32),) * len(head_dirs)
    else:
        init = tuple(c0_ref[0, d, hh] for hh, d in head_dirs)
    fin = lax.fori_loop(0, n_chunks, prefix_step, init)

    def out_step(gi, carry):
        rows = _rows(gi, GROUP)
        kvs = [load_kv(rows, hh) for hh in range(hp)]
        qgs = [q_ref[rows, hh * ML_DQK:(hh + 1) * ML_DQK].astype(F32) * (ML_DQK ** -0.5) for hh in range(hp)]
        qks = [_dot_nt(qg, kv[0]) for qg, kv in zip(qgs, kvs)]
        chunk = lambda n: slice(n * CHUNK, (n + 1) * CHUNK)
        inters = []
        for hh, d in head_dirs:
            c_in = cin_scr[hh, d, rows, :]
            inters.append(jnp.concatenate(
                [_dot(qgs[hh][chunk(n)], c_in[chunk(n)]) for n in range(CHUNKS_PER_GROUP)], axis=0))
        m_ts, b_ms, ss = [], [], []
        for hh, d in head_dirs:
            b_colb = bcb_scr[hh, d, rows, :]
            b_m = b_colb[:, 0:1] + mi_scr[hh, d, rows, :][:, 0:1]
            d_log = jnp.where(gm.incl[d], _wide(b_colb) - bcr_scr[hh, d, pl.ds(gi, 1), :]
                              + lir_scr[hh, d, pl.ds(gi, 1), :], -jnp.inf)
            m_t = jnp.maximum(b_m, jnp.max(d_log, axis=-1, keepdims=True))
            m_ts.append(m_t)
            b_ms.append(b_m)
            ss.append(qks[hh] * jnp.exp(d_log - m_t))
        intras = [_dot(s, kvs[hh][1]) for (hh, d), s in zip(head_dirs, ss)]
        hs = []
        for inter, intra, m_t, b_m in zip(inters, intras, m_ts, b_ms):
            num = jnp.exp(b_m - m_t) * inter + intra
            den = num[:, ML_DV:ML_DV + 1]
            hs.append(num[:, :ML_DV] / jnp.maximum(jnp.abs(den), jnp.exp(-m_t)))
        for hh in range(hp):
            o_ref[rows, hh * ML_DV:(hh + 1) * ML_DV] = hs[N_DIR * hh] + hs[N_DIR * hh + 1]
        return carry

    _loop(n_groups, out_step, 0)

    if emit_state:
        for i, (hh, d) in enumerate(head_dirs):
            cfin_ref[0, d, hh] = fin[i][:, :ML_DV]
            nfin_ref[0, d, hh] = fin[i][:, ML_DV:]
            mfin_ref[0, d, hh] = jnp.broadcast_to(m_fin[hh, d], (1, 128))


def _mlstm_scan(main, kt, gt, batch, seq, b_i, b_f, state0, emit_state):
    n_tok = batch * seq
    n_groups = seq // GROUP
    n_chunks = seq // CHUNK
    nh = ML_HEADS
    hp = min(nh, max(2, SCAN_STEP_POSITIONS // seq))
    n_hb = nh // hp
    grow, gcol = _gate_layouts(gt, batch, nh, hp)
    zero_init = state0 is None
    smem = pl.BlockSpec(memory_space=pltpu.SMEM)
    k_off = (nh * ML_DQK) // (hp * ML_DQK)
    v_off = (2 * nh * ML_DQK) // (hp * ML_DV)
    c_spec = pl.BlockSpec((1, N_DIR, hp, ML_DQK, ML_AUG), lambda b, p: (b, 0, p, 0, 0))
    m_spec = pl.BlockSpec((1, N_DIR, hp, 1, 128), lambda b, p: (b, 0, p, 0, 0))
    in_specs = [smem, smem,
                pl.BlockSpec((seq, hp * ML_DQK), lambda b, p: (b, p)),
                pl.BlockSpec((seq, hp * ML_DQK), lambda b, p: (b, k_off + p)),
                pl.BlockSpec((seq, hp * ML_DV), lambda b, p: (b, v_off + p)),
                pl.BlockSpec((n_groups, hp * ML_DQK, GROUP), lambda b, p: (b, p, 0)),
                pl.BlockSpec((1, hp) + grow.shape[2:], lambda b, p: (b, p, 0, 0, 0, 0)),
                pl.BlockSpec((1, 1) + gcol.shape[2:], lambda b, p: (b, p, 0, 0))]
    args = [b_i, b_f, main, main, main, kt, grow, gcol]
    if not zero_init:
        in_specs += [c_spec, m_spec]
        args += list(state0)
    out_specs = [pl.BlockSpec((seq, hp * ML_DV), lambda b, p: (b, p))]
    out_shape = [jax.ShapeDtypeStruct((n_tok, nh * ML_DV), F32)]
    if emit_state:
        half_spec = pl.BlockSpec((1, N_DIR, hp, ML_DQK, ML_DV), lambda b, p: (b, 0, p, 0, 0))
        out_specs += [half_spec, half_spec, m_spec]
        out_shape += [jax.ShapeDtypeStruct((batch, N_DIR, nh, ML_DQK, ML_DV), F32),
                      jax.ShapeDtypeStruct((batch, N_DIR, nh, ML_DQK, ML_AUG - ML_DV), F32),
                      jax.ShapeDtypeStruct((batch, N_DIR, nh, 1, 128), F32)]
    per_hd = lambda rows, width: pltpu.VMEM((hp, N_DIR, rows, width), F32)
    outs = pl.pallas_call(
        functools.partial(_mlstm_kernel, seq=seq, hp=hp, zero_init=zero_init, emit_state=emit_state),
        grid=(batch, n_hb),
        in_specs=in_specs, out_specs=out_specs, out_shape=out_shape,
        scratch_shapes=[per_hd(8, GROUP), per_hd(8, GROUP), per_hd(8, GROUP),
                        per_hd(seq, 128), per_hd(seq, 128),
                        per_hd(max(n_chunks, 8), 128),
                        per_hd(seq, ML_AUG), per_hd(seq, ML_AUG)],
        compiler_params=_params("parallel", "parallel"),
        name="mlstm_scan",
    )(*args)
    return outs[0], (tuple(outs[1:4]) if emit_state else None)


def _trunk(x, batch, seq, rows, mods, tokens_per_cond, st_d, st_ml, emit_state, p):
    depth = p["w_ada"].shape[0]
    tile_cond = lambda tile: (lambda i: (i * tile) // tokens_per_cond)
    new_d = new_ml = None
    for layer in range(depth):
        mod = mods[layer]
        kind, j = layer % 3, layer // 3
        if kind == 0:
            x = _fnet_mix(x, seq, j, p["norm_mix"][layer], mod, tile_cond(seq), p["fnet_w"], p["fnet_b"][j])
        elif kind == 1:
            n_main = DN_HEADS * (2 * DN_DK + 2 * DN_DV)
            main, gates = _norm_proj(x, p["norm_mix"][layer], mod, tokens_per_cond, p["dn_w_in"][j], n_main)
            s0 = None if st_d is None else st_d[:, j]
            o, sfin = _gdn_scan(main, gates.T, batch, seq, p["dn_conv_w"][j], p["dn_a_log"][j], p["dn_dt_bias"][j],
                                s0, emit_state)
            if emit_state:
                new_d = sfin
            x = _gated_out(x, o, main, 3, p["dn_norm"][j], mod, tile_cond(OUT_TOKEN_TILE), p["dn_w_out"][j],
                           DN_HEADS, jax.nn.silu)
        else:
            n_main = 2 * ML_HEADS * ML_DQK + 2 * ML_HEADS * ML_DV
            k_tile = (ML_HEADS * ML_DQK) // PROJ_COL_TILE
            assert ML_HEADS * ML_DQK == PROJ_COL_TILE
            main, gates, kt = _norm_proj(x, p["norm_mix"][layer], mod, tokens_per_cond, p["ml_w_in"][j], n_main,
                                         t_tile=k_tile)
            state0 = None
            if st_ml is not None:
                c0, n0, m0 = (s[:, j] for s in st_ml)
                pad = jnp.zeros(c0.shape[:-1] + (ML_AUG - ML_DV - 1,), F32)
                c_aug0 = jnp.concatenate([c0, n0[..., None], pad], axis=-1)
                m0b = jnp.broadcast_to(m0[..., None, None], m0.shape + (1, 128))
                state0 = (c_aug0, m0b)
            o, fin = _mlstm_scan(main, kt, gates.T, batch, seq, p["ml_b_i"][j], p["ml_b_f"][j], state0, emit_state)
            if emit_state:
                new_ml = (fin[0], fin[1][..., 0], fin[2][..., 0, 0])
            x = _gated_out(x, o, main, 2, p["ml_norm"][j], mod, tile_cond(OUT_TOKEN_TILE), p["ml_w_out"][j],
                           ML_HEADS, jax.nn.sigmoid)
        x = _conv_ffn(x, seq, rows, layer, p["norm_ffn"][layer], mod, tile_cond(FFN_TOKEN_TILE), p["ffn_w_up"],
                      p["ffn_conv_w"], p["ffn_conv_b"], p["ffn_w_down"], p["norm_final"],
                      final_norm=(layer == depth - 1))
    return x, new_d, new_ml


def kernel(x_prompt, x_sample, state_delta, state_mlstm_c, state_mlstm_n, state_mlstm_m, c, c_ctx, w_ada, b_ada, norm_mix, norm_ffn, norm_final, ffn_w_up, ffn_conv_w, ffn_conv_b, ffn_w_down, fnet_w, fnet_b, dn_w_in, dn_conv_w, dn_a_log, dn_dt_bias, dn_norm, dn_w_out, ml_w_in, ml_b_i, ml_b_f, ml_norm, ml_w_out):
    p = dict(w_ada=w_ada, norm_mix=norm_mix, norm_ffn=norm_ffn, norm_final=norm_final, ffn_w_up=ffn_w_up,
             ffn_conv_w=ffn_conv_w, ffn_conv_b=ffn_conv_b, ffn_w_down=ffn_w_down, fnet_w=fnet_w, fnet_b=fnet_b,
             dn_w_in=dn_w_in, dn_conv_w=dn_conv_w, dn_a_log=dn_a_log, dn_dt_bias=dn_dt_bias, dn_norm=dn_norm,
             dn_w_out=dn_w_out, ml_w_in=ml_w_in, ml_b_i=ml_b_i, ml_b_f=ml_b_f, ml_norm=ml_norm, ml_w_out=ml_w_out)
    b_ctx, t_ctx, _ = x_prompt.shape
    b_smp, t_smp, _ = x_sample.shape
    depth = w_ada.shape[0]

    cond8 = jnp.concatenate([c_ctx[None, :], c, jnp.zeros((8 - 1 - b_smp, D_MODEL), F32)], axis=0)
    mods = _adaln_all(cond8, w_ada, b_ada)
    mods_ctx = [mods[l, 0:1].reshape(1, 1, -1) for l in range(depth)]
    mods_smp = [mods[l, 1:1 + b_smp].reshape(b_smp, 1, -1) for l in range(depth)]

    y_ctx, new_d, new_ml = _trunk(x_prompt.reshape(b_ctx * t_ctx, D_MODEL), b_ctx, t_ctx, 1, mods_ctx,
                                  b_ctx * t_ctx, None, None, True, p)
    y_smp, _, _ = _trunk(x_sample.reshape(b_smp * t_smp, D_MODEL), b_smp, t_smp, t_smp // GRID_W, mods_smp,
                         t_smp, state_delta, (state_mlstm_c, state_mlstm_n, state_mlstm_m), False, p)

    new_c, new_n, new_m = new_ml
    return (y_ctx.reshape(b_ctx, t_ctx, D_MODEL), y_smp.reshape(b_smp, t_smp, D_MODEL),
            new_d[:, None], new_c[:, None], new_n[:, None], new_m[:, None])
```

```python
import collections
import functools

import numpy as np
import jax
import jax.numpy as jnp
from jax import lax
from jax.experimental import pallas as pl
from jax.experimental.pallas import tpu as pltpu

F32 = jnp.float32
BF16 = jnp.bfloat16

D_MODEL = 1024
EPS = 1e-6
N_DIR = 2
CHUNK = 64
LOG_CHUNK = 6
GROUP = 256
CHUNKS_PER_GROUP = GROUP // CHUNK
FNET_GROUP_DIM = 256
DN_HEADS, DN_DK, DN_DV, DN_CONV = 8, 128, 128, 5
ML_HEADS, ML_DQK, ML_DV = 8, 64, 128
D_FF = 2816
GRID_W = 64

FFN_TOKEN_TILE = 1024
FFN_FF_TILE = 256
FFN_SUB_TILES = 4
PROJ_TOKEN_TILE = 2048
PROJ_COL_TILE = 512
OUT_TOKEN_TILE = 512
ADA_COL_TILE = 1536
SCAN_STEP_POSITIONS = 1024
VMEM_LIMIT = 56 * 1024 * 1024


def _params(*sem):
    return pltpu.CompilerParams(dimension_semantics=sem, vmem_limit_bytes=VMEM_LIMIT)


def _dot(a, b):
    return jnp.dot(a.astype(BF16), b.astype(BF16), preferred_element_type=F32)


def _dot_nt(a, b):
    return lax.dot_general(a.astype(BF16), b.astype(BF16), (((1,), (1,)), ((), ())),
                           preferred_element_type=F32)


def _dot_tn(a, b):
    return lax.dot_general(a.astype(BF16), b.astype(BF16), (((0,), (0,)), ((), ())),
                           preferred_element_type=F32)


def _rms(x, g):
    return x * lax.rsqrt(jnp.mean(x * x, axis=-1, keepdims=True) + EPS) * g


def _norm_mod(x, g, sc, sh):
    return _rms(x, g) * (1.0 + sc) + sh


def _softplus(x):
    return jnp.maximum(x, 0.0) + jnp.log1p(jnp.exp(-jnp.abs(x)))


def _mod_spec(chunk, cond_of_tile):
    return pl.BlockSpec((1, 1, D_MODEL), lambda i, *_: (cond_of_tile(i), 0, chunk))


def _adaln_kernel(c_ref, w_ref, b_ref, o_ref):
    s = jax.nn.silu(c_ref[...])
    o_ref[0] = _dot(s, w_ref[0]) + b_ref[0]


def _adaln_all(cond8, w_ada, b_ada):
    depth = w_ada.shape[0]
    n_out = w_ada.shape[2]
    tn = ADA_COL_TILE
    return pl.pallas_call(
        _adaln_kernel,
        grid=(depth, n_out // tn),
        in_specs=[pl.BlockSpec((8, D_MODEL), lambda l, j: (0, 0)),
                  pl.BlockSpec((1, D_MODEL, tn), lambda l, j: (l, 0, j)),
                  pl.BlockSpec((1, 1, tn), lambda l, j: (l, 0, j))],
        out_specs=pl.BlockSpec((1, 8, tn), lambda l, j: (l, 0, j)),
        out_shape=jax.ShapeDtypeStruct((depth, 8, n_out), F32),
        compiler_params=_params("parallel", "parallel"),
        name="adaln",
    )(cond8, w_ada, b_ada.reshape(depth, 1, n_out))


def _norm_proj_kernel(x_ref, g_ref, sc_ref, sh_ref, w_ref, wg_ref, o_ref, gate_ref, *rest, t_tile):
    h_scr = rest[-1]

    @pl.when(pl.program_id(1) == 0)
    def _():
        h = _norm_mod(x_ref[...], g_ref[...], sc_ref[0], sh_ref[0]).astype(BF16)
        h_scr[...] = h
        gate_ref[...] = _dot_nt(wg_ref[...], h)

    y = _dot_nt(h_scr[...], w_ref[...])
    o_ref[...] = y.astype(o_ref.dtype)

    if t_tile is not None:
        t_ref = rest[0]

        @pl.when(pl.program_id(1) == t_tile)
        def _():
            for g in range(t_ref.shape[0]):
                t_ref[g] = y[g * GROUP:(g + 1) * GROUP, :].T


def _norm_proj(x, norm_g, mod, tokens_per_cond, w_stack, layer, n_main, t_tile=None):
    n_tok = x.shape[0]
    tm, tn = min(PROJ_TOKEN_TILE, tokens_per_cond), PROJ_COL_TILE
    cond_of_tile = lambda i: (i * tm) // tokens_per_cond
    w_t = jnp.swapaxes(w_stack, 1, 2)
    n_gate = w_t.shape[1] - n_main
    assert n_main % n_gate == 0 and n_gate % 8 == 0
    out_specs = [pl.BlockSpec((tm, tn), lambda i, j: (i, j)),
                 pl.BlockSpec((n_gate, tm), lambda i, j: (0, i))]
    out_shape = [jax.ShapeDtypeStruct((n_tok, n_main), BF16),
                 jax.ShapeDtypeStruct((n_gate, n_tok), F32)]
    if t_tile is not None:
        out_specs.append(pl.BlockSpec((tm // GROUP, tn, GROUP), lambda i, j: (i, 0, 0)))
        out_shape.append(jax.ShapeDtypeStruct((n_tok // GROUP, tn, GROUP), F32))
    return pl.pallas_call(
        functools.partial(_norm_proj_kernel, t_tile=t_tile),
        grid=(n_tok // tm, n_main // tn),
        in_specs=[pl.BlockSpec((tm, D_MODEL), lambda i, j: (i, 0)),
                  pl.BlockSpec((1, D_MODEL), lambda i, j: (0, 0)),
                  _mod_spec(1, cond_of_tile), _mod_spec(0, cond_of_tile),
                  pl.BlockSpec((None, tn, D_MODEL), lambda i, j: (layer, j, 0)),
                  pl.BlockSpec((None, n_gate, D_MODEL), lambda i, j: (layer, n_main // n_gate, 0))],
        out_specs=out_specs, out_shape=out_shape,
        scratch_shapes=[pltpu.VMEM((tm, D_MODEL), BF16)],
        compiler_params=_params("parallel", "arbitrary"),
        name="norm_proj",
    )(x, norm_g.reshape(1, D_MODEL), mod, mod, w_t, w_t)


def _gated_out_kernel(x_ref, o_ref, z_ref, ng_ref, g1_ref, w_ref, y_ref, wb_scr, *, n_heads, gate_fn):
    @pl.when(pl.program_id(0) == 0)
    def _():
        wb_scr[...] = w_ref[...].astype(BF16)

    dv = o_ref.shape[1] // n_heads
    parts = []
    for h in range(n_heads):
        sl = slice(h * dv, (h + 1) * dv)
        parts.append((_rms(o_ref[:, sl], ng_ref[...]) * gate_fn(z_ref[:, sl].astype(F32))).astype(BF16))
    hs = jnp.concatenate(parts, axis=1)
    y = jnp.dot(hs, wb_scr[...], preferred_element_type=F32)
    y_ref[...] = x_ref[...] + g1_ref[0] * y


def _gated_out(x, o, main, z_block, norm_g, mod, cond_of_tile, w_out, n_heads, gate_fn):
    n_tok = x.shape[0]
    tm = OUT_TOKEN_TILE
    width = o.shape[1]
    return pl.pallas_call(
        functools.partial(_gated_out_kernel, n_heads=n_heads, gate_fn=gate_fn),
        grid=(n_tok // tm,),
        in_specs=[pl.BlockSpec((tm, D_MODEL), lambda i: (i, 0)),
                  pl.BlockSpec((tm, width), lambda i: (i, 0)),
                  pl.BlockSpec((tm, width), lambda i: (i, z_block)),
                  pl.BlockSpec((1, width // n_heads), lambda i: (0, 0)),
                  _mod_spec(2, cond_of_tile),
                  pl.BlockSpec((width, D_MODEL), lambda i: (0, 0))],
        out_specs=pl.BlockSpec((tm, D_MODEL), lambda i: (i, 0)),
        out_shape=jax.ShapeDtypeStruct((n_tok, D_MODEL), F32),
        scratch_shapes=[pltpu.VMEM((width, D_MODEL), BF16)],
        compiler_params=_params("arbitrary"),
        name="gated_out",
    )(x, o, main, norm_g.reshape(1, -1), mod, w_out)


def _dft_mats(n):
    k = np.arange(n, dtype=np.int64)
    ang = 2.0 * np.pi * ((k[:, None] * k[None, :]) % n).astype(np.float64) / n
    s = 1.0 / np.sqrt(n)
    return np.cos(ang) * s, np.sin(ang) * s


def _fnet_kernel(x_ref, ng_ref, sc_ref, sh_ref, g1_ref, ct_ref, st_ref, cs_ref, w_ref, b_ref, y_ref, wb_scr):
    @pl.when(pl.program_id(0) == 0)
    def _():
        wb_scr[...] = w_ref[...].astype(BF16)

    x = x_ref[...]
    h = _norm_mod(x, ng_ref[...], sc_ref[0], sh_ref[0]).astype(BF16)
    p = jnp.dot(ct_ref[...], h, preferred_element_type=F32)
    q = jnp.dot(st_ref[...], h, preferred_element_type=F32)
    gd = FNET_GROUP_DIM
    parts = []
    for g in range(D_MODEL // gd):
        sl = slice(g * gd, (g + 1) * gd)
        pq = jnp.concatenate([p[:, sl], q[:, sl]], axis=1).astype(BF16)
        parts.append(jnp.dot(pq, cs_ref[...], preferred_element_type=F32).astype(BF16))
    f = jnp.concatenate(parts, axis=1)
    y = jnp.dot(f, wb_scr[...], preferred_element_type=F32) + b_ref[...]
    y_ref[...] = x + g1_ref[0] * y


def _fnet_mix(x, seq, layer, norm_g, mod, cond_of_seq, w, b):
    n_tok = x.shape[0]
    ct, st = _dft_mats(seq)
    cc, sc = _dft_mats(FNET_GROUP_DIM)
    cs = np.concatenate([cc, -sc], axis=0)
    const = lambda i: (0, 0)
    ct, st, cs = (jnp.asarray(m, F32).astype(BF16) for m in (ct, st, cs))
    return pl.pallas_call(
        _fnet_kernel,
        grid=(n_tok // seq,),
        in_specs=[pl.BlockSpec((seq, D_MODEL), lambda i: (i, 0)),
                  pl.BlockSpec((1, D_MODEL), const),
                  _mod_spec(1, cond_of_seq), _mod_spec(0, cond_of_seq), _mod_spec(2, cond_of_seq),
                  pl.BlockSpec((seq, seq), const),
                  pl.BlockSpec((seq, seq), const),
                  pl.BlockSpec((2 * FNET_GROUP_DIM, FNET_GROUP_DIM), const),
                  pl.BlockSpec((None, D_MODEL, D_MODEL), lambda i: (layer, 0, 0)),
                  pl.BlockSpec((1, D_MODEL), const)],
        out_specs=pl.BlockSpec((seq, D_MODEL), lambda i: (i, 0)),
        out_shape=jax.ShapeDtypeStruct((n_tok, D_MODEL), F32),
        scratch_shapes=[pltpu.VMEM((D_MODEL, D_MODEL), BF16)],
        compiler_params=_params("arbitrary"),
        name="fnet",
    )(x, norm_g.reshape(1, D_MODEL), mod, mod, mod,
      ct, st, cs, w, b.reshape(1, D_MODEL))


def _shift_rows(x, delta):
    n = x.shape[0]
    return pltpu.roll(x, (-delta) % n, 0)


def _dwconv_tokens(g, cw_ref, seq, rows):
    tm = g.shape[0]
    width = seq // rows
    assert seq & (seq - 1) == 0 and width & (width - 1) == 0
    t = lax.broadcasted_iota(jnp.int32, (tm, 1), 0)
    pos = t & (seq - 1)
    col = pos & (width - 1)
    row = pos >> (width.bit_length() - 1)
    g_cols = (_shift_rows(g, -1) * (col >= 1).astype(F32), g,
              _shift_rows(g, 1) * (col <= width - 2).astype(F32))
    out = None
    for di in (-1, 0, 1):
        if rows == 1 and di != 0:
            continue
        r = sum(g_cols[dj] * cw_ref[pl.ds(3 * (di + 1) + dj, 1), :] for dj in range(3))
        if di != 0:
            ok = (row + di >= 0) & (row + di <= rows - 1)
            r = _shift_rows(r, di * width) * ok.astype(F32)
        out = r if out is None else out + r
    return out


def _ffn_kernel(x_ref, ng_ref, sc_ref, sh_ref, g2_ref, wa_ref, wg_ref, cw_ref, cb_ref, wd_ref, nf_ref,
                y_ref, h_scr, acc_scr, *, seq, rows, final_norm):
    j = pl.program_id(1)

    @pl.when(j == 0)
    def _():
        h_scr[...] = _norm_mod(x_ref[...], ng_ref[...], sc_ref[0], sh_ref[0]).astype(BF16)
        acc_scr[...] = jnp.zeros_like(acc_scr)

    tm = h_scr.shape[0]
    sub = tm // FFN_SUB_TILES
    parts = [pl.ds(r, sub) for r in range(0, tm, sub)]
    wa, wg, wd = (w[...].astype(BF16) for w in (wa_ref, wg_ref, wd_ref))
    gs = [jnp.dot(h_scr[p, :], wg, preferred_element_type=F32) for p in parts]
    as_ = [jnp.dot(h_scr[p, :], wa, preferred_element_type=F32) for p in parts]
    per_seq = max(seq // sub, 1)
    convs = []
    for i in range(0, len(parts), per_seq):
        g = gs[i] if per_seq == 1 else jnp.concatenate(gs[i:i + per_seq], axis=0)
        c = _dwconv_tokens(g, cw_ref, seq, rows) + cb_ref[...]
        convs += [c[k * sub:(k + 1) * sub] for k in range(per_seq)]
    acts = [(jax.nn.silu(c) * a).astype(BF16) for c, a in zip(convs, as_)]
    for p, act in zip(parts, acts):
        acc_scr[p, :] += jnp.dot(act, wd, preferred_element_type=F32)

    @pl.when(j == pl.num_programs(1) - 1)
    def _():
        y = x_ref[...] + g2_ref[0] * acc_scr[...]
        if final_norm:
            y = _rms(y, nf_ref[...])
        y_ref[...] = y


def _conv_ffn(x, seq, rows, layer, norm_g, mod, cond_of_tile, w_up, conv_w, conv_b, w_down, norm_final, final_norm):
    n_tok = x.shape[0]
    tm, tf = FFN_TOKEN_TILE, FFN_FF_TILE
    n_ff_tiles = D_FF // tf
    depth = w_up.shape[0]
    return pl.pallas_call(
        functools.partial(_ffn_kernel, seq=seq, rows=rows, final_norm=final_norm),
        grid=(n_tok // tm, n_ff_tiles),
        in_specs=[pl.BlockSpec((tm, D_MODEL), lambda i, j: (i, 0)),
                  pl.BlockSpec((1, D_MODEL), lambda i, j: (0, 0)),
                  _mod_spec(4, cond_of_tile), _mod_spec(3, cond_of_tile), _mod_spec(5, cond_of_tile),
                  pl.BlockSpec((None, D_MODEL, tf), lambda i, j: (layer, 0, j)),
                  pl.BlockSpec((None, D_MODEL, tf), lambda i, j: (layer, 0, n_ff_tiles + j)),
                  pl.BlockSpec((None, 9, tf), lambda i, j: (layer, 0, j)),
                  pl.BlockSpec((None, 1, tf), lambda i, j: (layer, 0, j)),
                  pl.BlockSpec((None, tf, D_MODEL), lambda i, j: (layer, j, 0)),
                  pl.BlockSpec((1, D_MODEL), lambda i, j: (0, 0))],
        out_specs=pl.BlockSpec((tm, D_MODEL), lambda i, j: (i, 0)),
        out_shape=jax.ShapeDtypeStruct((n_tok, D_MODEL), F32),
        scratch_shapes=[pltpu.VMEM((tm, D_MODEL), BF16), pltpu.VMEM((tm, D_MODEL), F32)],
        compiler_params=_params("parallel", "arbitrary"),
        name="conv_ffn",
    )(x, norm_g.reshape(1, D_MODEL), mod, mod, mod, w_up, w_up, conv_w.reshape(depth, 9, D_FF),
      conv_b.reshape(depth, 1, D_FF), w_down, norm_final.reshape(1, D_MODEL))


GroupMasks = collections.namedtuple("GroupMasks", "r c same incl strict")


def _chunk_masks():
    r = lax.broadcasted_iota(jnp.int32, (CHUNK, CHUNK), 0)
    c = lax.broadcasted_iota(jnp.int32, (CHUNK, CHUNK), 1)
    return (r >= c, r <= c)


def _group_masks():
    r = lax.broadcasted_iota(jnp.int32, (GROUP, GROUP), 0)
    c = lax.broadcasted_iota(jnp.int32, (GROUP, GROUP), 1)
    same = (r >> LOG_CHUNK) == (c >> LOG_CHUNK)
    return GroupMasks(r, c, same, (same & (r >= c), same & (r <= c)), (same & (r > c), same & (r < c)))


def _gate_layouts(gt, batch, n_heads, hp):
    seq = gt.shape[1] // batch
    n_groups, n_chunks = seq // GROUP, seq // CHUNK
    assert seq % GROUP == 0 and n_groups <= 8
    g = jnp.transpose(gt.reshape(N_DIR, 2, n_heads, batch, seq), (3, 2, 0, 1, 4))
    rows = g.reshape(g.shape[:4] + (n_groups, GROUP))
    rows = jnp.pad(rows, ((0, 0),) * 4 + ((0, 8 - n_groups), (0, 0)))
    cols = jnp.swapaxes(g.reshape(batch, n_heads // hp, hp * N_DIR * 2 * n_chunks, CHUNK), -1, -2)
    return rows, cols


def _col_gate(gcol_ref, hh, d, kind, n_chunks):
    start = ((hh * N_DIR + d) * 2 + kind) * n_chunks
    return gcol_ref[0, 0, :, start:start + n_chunks]


def _split3(x):
    x1 = x.astype(BF16)
    r1 = x - x1.astype(F32)
    x2 = r1.astype(BF16)
    return x1, x2, (r1 - x2.astype(F32)).astype(BF16)


def _dot_mask_rhs(x, mask):
    m = x.shape[0]
    y = jnp.dot(jnp.concatenate(_split3(x), axis=0), jnp.where(mask, 1.0, 0.0).astype(BF16),
                preferred_element_type=F32)
    return y[:m] + y[m:2 * m] + y[2 * m:]


def _dot_mask_lhs(mask, x):
    mb = jnp.where(mask, 1.0, 0.0).astype(BF16)
    y1, y2, y3 = (jnp.dot(mb, p, preferred_element_type=F32) for p in _split3(x))
    return y1 + y2 + y3


def _cumsum_rows(row, d, gm):
    return _dot_mask_rhs(row, gm.incl[1 - d])


def _cumsum_cols(col, d, cm):
    return _dot_mask_lhs(cm[d], col)


def _store_cols(dst, col, n_chunks):
    for n in range(n_chunks):
        dst[n * CHUNK:(n + 1) * CHUNK, :] = jnp.broadcast_to(col[:, n:n + 1], (CHUNK, 128))


def _wide(x):
    return jnp.concatenate([x, x], axis=1)


def _rows(i, size):
    if isinstance(i, int):
        return pl.ds(i * size, size)
    return pl.ds(pl.multiple_of(i * size, size), size)


def _loop(n, body, init):
    if n == 1:
        return body(0, init)
    return lax.fori_loop(0, n, body, init)


def _conv_silu(x, w_ref):
    n = x.shape[0]
    k = w_ref.shape[0]
    t = lax.broadcasted_iota(jnp.int32, (n, 1), 0)
    acc = None
    for j in range(k):
        delta = j - k // 2
        if delta == 0:
            term = x
        else:
            ok = (t + delta >= 0) & (t + delta <= n - 1)
            term = _shift_rows(x, delta) * ok.astype(F32)
        term = term * w_ref[pl.ds(j, 1), :]
        acc = term if acc is None else acc + term
    return jax.nn.silu(acc)


def _l2norm(x):
    return x * lax.rsqrt(jnp.sum(x * x, axis=-1, keepdims=True) + EPS)


def _blockdiag_tri_inverse(mats, gm):
    base = (gm.r >> 3) == (gm.c >> 3)
    eye = jnp.where(gm.r == gm.c, 1.0, 0.0)
    ps = [-jnp.where(base, a, 0.0) for a in mats]
    invs = [eye + p for p in ps]
    for _ in range(2):
        ps = [_dot(p, p) for p in ps]
        invs = [inv + _dot(inv, p) for inv, p in zip(invs, ps)]
    for s in (3, 4, 5):
        join = ((gm.r >> s) ^ (gm.c >> s)) == 1
        tmp = [_dot(jnp.where(join, a, 0.0), inv) for a, inv in zip(mats, invs)]
        invs = [inv - _dot(inv, t) for inv, t in zip(invs, tmp)]
    return invs


def _gdn_kernel(*refs, seq, hp, zero_init, emit_state):
    alog_ref, dtb_ref, q_ref, k_ref, v_ref, cwq_ref, cwk_ref, cwv_ref, grow_ref, gcol_ref = refs[:10]
    pos = 10
    s0_ref = None
    if not zero_init:
        s0_ref = refs[pos]
        pos += 1
    o_ref = refs[pos]
    pos += 1
    sfin_ref = None
    if emit_state:
        sfin_ref = refs[pos]
        pos += 1
    (q_scr, k_scr, v_scr, gcr_scr, gcb_scr, bb_scr, gl_scr,
     u_scr, w_scr, qd_scr, kd_scr, qkd_scr) = refs[pos:]

    n_chunks = seq // CHUNK
    cm = _chunk_masks()
    gm = _group_masks()
    head_dirs = [(hh, d) for hh in range(hp) for d in range(N_DIR)]
    lanes = lambda hh: slice(hh * DN_DK, (hh + 1) * DN_DK)

    q_scr[...] = _conv_silu(q_ref[...].astype(F32), cwq_ref)
    k_scr[...] = _conv_silu(k_ref[...].astype(F32), cwk_ref)
    v_scr[...] = _conv_silu(v_ref[...].astype(F32), cwv_ref)
    for hh in range(hp):
        q_scr[:, lanes(hh)] = _l2norm(q_scr[:, lanes(hh)]) * (DN_DK ** -0.5)
        k_scr[:, lanes(hh)] = _l2norm(k_scr[:, lanes(hh)])

    for hh, d in head_dirs:
        head = pl.program_id(1) * hp + hh
        neg_a = -jnp.exp(jnp.full((1, 1), alog_ref[d, head], F32))
        dtb = dtb_ref[d, head]
        gcr_scr[hh, d] = _cumsum_rows(neg_a * _softplus(grow_ref[0, hh, d, 0] + dtb), d, gm)
        gc_col = _cumsum_cols(neg_a * _softplus(_col_gate(gcol_ref, hh, d, 0, n_chunks) + dtb), d, cm)
        _store_cols(gcb_scr.at[hh, d], gc_col, n_chunks)
        _store_cols(bb_scr.at[hh, d], jax.nn.sigmoid(_col_gate(gcol_ref, hh, d, 1, n_chunks)), n_chunks)
        last = CHUNK - 1 if d == 0 else 0
        _store_cols(gl_scr.at[hh, d], jnp.broadcast_to(gc_col[last:last + 1, :], gc_col.shape), n_chunks)

    def group_step(gi, carry):
        rows = _rows(gi, GROUP)
        kqs = []
        for hh in range(hp):
            kg, qg = k_scr[rows, lanes(hh)], q_scr[rows, lanes(hh)]
            kqs.append(_dot_nt(jnp.concatenate([kg, qg], axis=0), kg))
        mats = []
        for hh, d in head_dirs:
            gcb = gcb_scr[hh, d, rows, :]
            gcr = gcr_scr[hh, d, pl.ds(gi, 1), :]
            decay = jnp.exp(jnp.where(gm.incl[d], _wide(gcb) - gcr, -jnp.inf))
            mats.append(jnp.where(gm.strict[d], kqs[hh][:GROUP] * _wide(bb_scr[hh, d, rows, :]) * decay, 0.0))
            qkd = kqs[hh][GROUP:] * decay
            qkd_scr[hh, d, rows, :] = sum(qkd[:, n * CHUNK:(n + 1) * CHUNK] for n in range(CHUNKS_PER_GROUP))
        t_invs = _blockdiag_tri_inverse(mats, gm)
        for (hh, d), t_inv in zip(head_dirs, t_invs):
            qg, kg, vg = q_scr[rows, lanes(hh)], k_scr[rows, lanes(hh)], v_scr[rows, lanes(hh)]
            gcb = gcb_scr[hh, d, rows, :]
            bb = bb_scr[hh, d, rows, :]
            e_gc = jnp.exp(gcb)
            uw = _dot(t_inv, jnp.concatenate([vg * bb, kg * bb * e_gc], axis=1))
            u_scr[hh, d, rows, :] = uw[:, :DN_DV]
            w_scr[hh, d, rows, :] = uw[:, DN_DV:]
            qd_scr[hh, d, rows, :] = qg * e_gc
            kd_scr[hh, d, rows, :] = kg * jnp.exp(gl_scr[hh, d, rows, :] - gcb)
        return carry

    _loop(seq // GROUP, group_step, 0)

    o_ref[...] = jnp.zeros_like(o_ref)

    def chunk_step(n, carry):
        idxs = [n if d == 0 else n_chunks - 1 - n for _, d in head_dirs]
        rows = [_rows(idx, CHUNK) for idx in idxs]
        wqs = [_dot(jnp.concatenate([w_scr[hh, d, r, :], qd_scr[hh, d, r, :]], axis=0), s)
               for (hh, d), r, s in zip(head_dirs, rows, carry)]
        v_news = [u_scr[hh, d, r, :] - wq[:CHUNK] for (hh, d), r, wq in zip(head_dirs, rows, wqs)]
        o_ns = [wq[CHUNK:] + _dot(qkd_scr[hh, d, r, :], v_new)
                for (hh, d), r, wq, v_new in zip(head_dirs, rows, wqs, v_news)]
        new = []
        for (hh, d), idx, r, s, v_new in zip(head_dirs, idxs, rows, carry, v_news):
            s_decay = jnp.exp(gl_scr[hh, d, pl.ds(idx * CHUNK, 1), :])
            new.append(s * s_decay + _dot_tn(kd_scr[hh, d, r, :], v_new))
        for (hh, d), r, o_n in zip(head_dirs, rows, o_ns):
            o_ref[r, lanes(hh)] += o_n
        return tuple(new)

    if zero_init:
        init = (jnp.zeros((DN_DK, DN_DV), F32),) * len(head_dirs)
    else:
        init = tuple(s0_ref[0, d, hh] for hh, d in head_dirs)
    fin = lax.fori_loop(0, n_chunks, chunk_step, init)
    if emit_state:
        for i, (hh, d) in enumerate(head_dirs):
            sfin_ref[0, d, hh] = fin[i]


def _gdn_scan(main, gt, batch, seq, conv_w, a_log, dt_bias, s0, emit_state):
    n_tok = batch * seq
    nh = DN_HEADS
    hp = min(nh, max(2, SCAN_STEP_POSITIONS // seq))
    n_hb = nh // hp
    grow, gcol = _gate_layouts(gt, batch, nh, hp)
    zero_init = s0 is None
    smem = pl.BlockSpec(memory_space=pltpu.SMEM)
    qkv_spec = lambda off: pl.BlockSpec((seq, hp * DN_DK), lambda b, h: (b, off + h))
    cw_spec = lambda off: pl.BlockSpec((DN_CONV, hp * DN_DK), lambda b, h: (0, off + h))
    state_spec = pl.BlockSpec((1, N_DIR, hp, DN_DK, DN_DV), lambda b, h: (b, 0, h, 0, 0))
    in_specs = [smem, smem, qkv_spec(0), qkv_spec(n_hb), qkv_spec(2 * n_hb),
                cw_spec(0), cw_spec(n_hb), cw_spec(2 * n_hb),
                pl.BlockSpec((1, hp) + grow.shape[2:], lambda b, h: (b, h, 0, 0, 0, 0)),
                pl.BlockSpec((1, 1) + gcol.shape[2:], lambda b, h: (b, h, 0, 0))]
    args = [a_log, dt_bias, main, main, main, conv_w, conv_w, conv_w, grow, gcol]
    if not zero_init:
        in_specs.append(state_spec)
        args.append(s0)
    out_specs = [pl.BlockSpec((seq, hp * DN_DV), lambda b, h: (b, h))]
    out_shape = [jax.ShapeDtypeStruct((n_tok, nh * DN_DV), F32)]
    if emit_state:
        out_specs.append(state_spec)
        out_shape.append(jax.ShapeDtypeStruct((batch, N_DIR, nh, DN_DK, DN_DV), F32))
    per_dir = lambda width: pltpu.VMEM((hp, N_DIR, seq, width), F32)
    qkv_scr = pltpu.VMEM((seq, hp * DN_DK), F32)
    outs = pl.pallas_call(
        functools.partial(_gdn_kernel, seq=seq, hp=hp, zero_init=zero_init, emit_state=emit_state),
        grid=(batch, n_hb),
        in_specs=in_specs, out_specs=out_specs, out_shape=out_shape,
        scratch_shapes=[qkv_scr, qkv_scr, qkv_scr,
                        pltpu.VMEM((hp, N_DIR, 8, GROUP), F32),
                        per_dir(128), per_dir(128), per_dir(128),
                        per_dir(DN_DV), per_dir(DN_DK), per_dir(DN_DK), per_dir(DN_DK), per_dir(CHUNK)],
        compiler_params=_params("parallel", "parallel"),
        name="gdn_scan",
    )(*args)
    return outs[0], (outs[1] if emit_state else None)


ML_AUG = 2 * ML_DV


def _mlstm_kernel(*refs, seq, hp, zero_init, emit_state):
    bi_ref, bf_ref, q_ref, k_ref, v_ref, kt_ref, grow_ref, gcol_ref = refs[:8]
    pos = 8
    c0_ref = m0_ref = None
    if not zero_init:
        c0_ref, m0_ref = refs[pos:pos + 2]
        pos += 2
    o_ref = refs[pos]
    pos += 1
    cfin_ref = nfin_ref = mfin_ref = None
    if emit_state:
        cfin_ref, nfin_ref, mfin_ref = refs[pos:pos + 3]
        pos += 3
    bcr_scr, lir_scr, kwr_scr, bcb_scr, mi_scr, dec_scr, dc_scr, cin_scr = refs[pos:]

    n_chunks = seq // CHUNK
    n_groups = seq // GROUP
    head0 = pl.program_id(1) * hp
    cm = _chunk_masks()
    gm = _group_masks()
    ones_col = jnp.where(lax.broadcasted_iota(jnp.int32, (GROUP, ML_DV), 1) == 0, 1.0, 0.0)
    head_dirs = [(hh, d) for hh in range(hp) for d in range(N_DIR)]

    m_fin = {}
    row_group = lax.broadcasted_iota(jnp.int32, (8, GROUP), 0)
    row_chunk = lax.broadcasted_iota(jnp.int32, (8, GROUP), 1) >> LOG_CHUNK
    for hh, d in head_dirs:
        b_i = bi_ref[d, head0 + hh]
        b_f = bf_ref[d, head0 + hh]
        li_row = grow_ref[0, hh, d, 0] + b_i
        lf_row = -_softplus(-(grow_ref[0, hh, d, 1] + b_f))
        bc_row = _cumsum_rows(lf_row, d, gm)
        lir_scr[hh, d] = li_row
        bcr_scr[hh, d] = bc_row
        w_row = _dot_mask_rhs(lf_row, gm.same) - bc_row + li_row
        m_out_row = jnp.zeros((8, GROUP), F32)
        li_col = _col_gate(gcol_ref, hh, d, 0, n_chunks) + b_i
        bc_col = _cumsum_cols(-_softplus(-(_col_gate(gcol_ref, hh, d, 1, n_chunks) + b_f)), d, cm)
        last = CHUNK - 1 if d == 0 else 0
        b_last = bc_col[last:last + 1, :]
        w_col = b_last - bc_col + li_col
        w_max = jnp.max(w_col, axis=0, keepdims=True)
        m = jnp.zeros((1, 1), F32) if zero_init else m0_ref[0, d, hh][:, 0:1]
        for step in range(n_chunks):
            n = step if d == 0 else n_chunks - 1 - step
            sl = slice(n * CHUNK, (n + 1) * CHUNK)
            m_new = jnp.maximum(b_last[:, n:n + 1] + m, w_max[:, n:n + 1])
            bcb_scr[hh, d, sl, :] = jnp.broadcast_to(bc_col[:, n:n + 1], (CHUNK, 128))
            mi_scr[hh, d, sl, :] = jnp.broadcast_to(m, (CHUNK, 128))
            dec_scr[hh, d, n:n + 1, :] = jnp.broadcast_to(jnp.exp(b_last[:, n:n + 1] + m - m_new), (1, 128))
            in_chunk = (row_group == n // CHUNKS_PER_GROUP) & (row_chunk == n % CHUNKS_PER_GROUP)
            m_out_row = jnp.where(in_chunk, m_new, m_out_row)
            m = m_new
        m_fin[hh, d] = m
        kwr_scr[hh, d] = jnp.exp(w_row - m_out_row)

    def load_kv(rows, hh):
        kg = k_ref[rows, hh * ML_DQK:(hh + 1) * ML_DQK].astype(F32)
        v_aug = jnp.concatenate([v_ref[rows, hh * ML_DV:(hh + 1) * ML_DV].astype(F32), ones_col], axis=1)
        return kg, v_aug

    def delta_step(gi, carry):
        rows = _rows(gi, GROUP)
        v_augs = [load_kv(rows, hh)[1] for hh in range(hp)]
        kt = kt_ref[gi]
        lhs = []
        for hh, d in head_dirs:
            kw_t = kt[hh * ML_DQK:(hh + 1) * ML_DQK, :] * kwr_scr[hh, d, pl.ds(gi, 1), :]
            lhs.append(jnp.where(gm.same, jnp.concatenate([kw_t] * CHUNKS_PER_GROUP, axis=0), 0.0))
        for (hh, d), kw in zip(head_dirs, lhs):
            dc_scr[hh, d, rows, :] = _dot(kw, v_augs[hh])
        return carry

    _loop(n_groups, delta_step, 0)

    def prefix_step(n, carry):
        new = []
        for (hh, d), c_aug in zip(head_dirs, carry):
            idx = n if d == 0 else n_chunks - 1 - n
            rows = _rows(idx, CHUNK)
            cin_scr[hh, d, rows, :] = c_aug
            new.append(c_aug * _wide(dec_scr[hh, d, pl.ds(idx, 1), :]) + dc_scr[hh, d, rows, :])
        return tuple(new)

    if zero_init:
        init = (jnp.zeros((ML_DQK, ML_AUG), F32),) * len(head_dirs)
    else:
        init = tuple(c0_ref[0, d, hh] for hh, d in head_dirs)
    fin = lax.fori_loop(0, n_chunks, prefix_step, init)

    def out_step(gi, carry):
        rows = _rows(gi, GROUP)
        kvs = [load_kv(rows, hh) for hh in range(hp)]
        qgs = [q_ref[rows, hh * ML_DQK:(hh + 1) * ML_DQK].astype(F32) * (ML_DQK ** -0.5) for hh in range(hp)]
        qks = [_dot_nt(qg, kv[0]) for qg, kv in zip(qgs, kvs)]
        chunk = lambda n: slice(n * CHUNK, (n + 1) * CHUNK)
        inters = []
        for hh, d in head_dirs:
            c_in = cin_scr[hh, d, rows, :]
            inters.append(jnp.concatenate(
                [_dot(qgs[hh][chunk(n)], c_in[chunk(n)]) for n in range(CHUNKS_PER_GROUP)], axis=0))
        m_ts, b_ms, ss = [], [], []
        for hh, d in head_dirs:
            b_colb = bcb_scr[hh, d, rows, :]
            b_m = b_colb[:, 0:1] + mi_scr[hh, d, rows, :][:, 0:1]
            d_log = jnp.where(gm.incl[d], _wide(b_colb) - bcr_scr[hh, d, pl.ds(gi, 1), :]
                              + lir_scr[hh, d, pl.ds(gi, 1), :], -jnp.inf)
            m_t = jnp.maximum(b_m, jnp.max(d_log, axis=-1, keepdims=True))
            m_ts.append(m_t)
            b_ms.append(b_m)
            ss.append(qks[hh] * jnp.exp(d_log - m_t))
        intras = [_dot(s, kvs[hh][1]) for (hh, d), s in zip(head_dirs, ss)]
        hs = []
        for inter, intra, m_t, b_m in zip(inters, intras, m_ts, b_ms):
            num = jnp.exp(b_m - m_t) * inter + intra
            den = num[:, ML_DV:ML_DV + 1]
            hs.append(num[:, :ML_DV] / jnp.maximum(jnp.abs(den), jnp.exp(-m_t)))
        for hh in range(hp):
            o_ref[rows, hh * ML_DV:(hh + 1) * ML_DV] = hs[N_DIR * hh] + hs[N_DIR * hh + 1]
        return carry

    _loop(n_groups, out_step, 0)

    if emit_state:
        for i, (hh, d) in enumerate(head_dirs):
            cfin_ref[0, d, hh] = fin[i][:, :ML_DV]
            nfin_ref[0, d, hh] = fin[i][:, ML_DV:]
            mfin_ref[0, d, hh] = jnp.broadcast_to(m_fin[hh, d], (1, 128))


def _mlstm_scan(main, kt, gt, batch, seq, b_i, b_f, state0, emit_state):
    n_tok = batch * seq
    n_groups = seq // GROUP
    n_chunks = seq // CHUNK
    nh = ML_HEADS
    hp = min(nh, max(2, SCAN_STEP_POSITIONS // seq))
    n_hb = nh // hp
    grow, gcol = _gate_layouts(gt, batch, nh, hp)
    zero_init = state0 is None
    smem = pl.BlockSpec(memory_space=pltpu.SMEM)
    k_off = (nh * ML_DQK) // (hp * ML_DQK)
    v_off = (2 * nh * ML_DQK) // (hp * ML_DV)
    c_spec = pl.BlockSpec((1, N_DIR, hp, ML_DQK, ML_AUG), lambda b, p: (b, 0, p, 0, 0))
    m_spec = pl.BlockSpec((1, N_DIR, hp, 1, 128), lambda b, p: (b, 0, p, 0, 0))
    in_specs = [smem, smem,
                pl.BlockSpec((seq, hp * ML_DQK), lambda b, p: (b, p)),
                pl.BlockSpec((seq, hp * ML_DQK), lambda b, p: (b, k_off + p)),
                pl.BlockSpec((seq, hp * ML_DV), lambda b, p: (b, v_off + p)),
                pl.BlockSpec((n_groups, hp * ML_DQK, GROUP), lambda b, p: (b, p, 0)),
                pl.BlockSpec((1, hp) + grow.shape[2:], lambda b, p: (b, p, 0, 0, 0, 0)),
                pl.BlockSpec((1, 1) + gcol.shape[2:], lambda b, p: (b, p, 0, 0))]
    args = [b_i, b_f, main, main, main, kt, grow, gcol]
    if not zero_init:
        in_specs += [c_spec, m_spec]
        args += list(state0)
    out_specs = [pl.BlockSpec((seq, hp * ML_DV), lambda b, p: (b, p))]
    out_shape = [jax.ShapeDtypeStruct((n_tok, nh * ML_DV), F32)]
    if emit_state:
        half_spec = pl.BlockSpec((1, N_DIR, hp, ML_DQK, ML_DV), lambda b, p: (b, 0, p, 0, 0))
        out_specs += [half_spec, half_spec, m_spec]
        out_shape += [jax.ShapeDtypeStruct((batch, N_DIR, nh, ML_DQK, ML_DV), F32),
                      jax.ShapeDtypeStruct((batch, N_DIR, nh, ML_DQK, ML_AUG - ML_DV), F32),
                      jax.ShapeDtypeStruct((batch, N_DIR, nh, 1, 128), F32)]
    per_hd = lambda rows, width: pltpu.VMEM((hp, N_DIR, rows, width), F32)
    outs = pl.pallas_call(
        functools.partial(_mlstm_kernel, seq=seq, hp=hp, zero_init=zero_init, emit_state=emit_state),
        grid=(batch, n_hb),
        in_specs=in_specs, out_specs=out_specs, out_shape=out_shape,
        scratch_shapes=[per_hd(8, GROUP), per_hd(8, GROUP), per_hd(8, GROUP),
                        per_hd(seq, 128), per_hd(seq, 128),
                        per_hd(max(n_chunks, 8), 128),
                        per_hd(seq, ML_AUG), per_hd(seq, ML_AUG)],
        compiler_params=_params("parallel", "parallel"),
        name="mlstm_scan",
    )(*args)
    return outs[0], (tuple(outs[1:4]) if emit_state else None)


def _trunk(x, batch, seq, rows, mods, tokens_per_cond, st_d, st_ml, emit_state, p):
    depth = p["w_ada"].shape[0]
    tile_cond = lambda tile: (lambda i: (i * tile) // tokens_per_cond)
    new_d = new_ml = None
    for layer in range(depth):
        mod = mods[layer]
        kind, j = layer % 3, layer // 3
        if kind == 0:
            x = _fnet_mix(x, seq, j, p["norm_mix"][layer], mod, tile_cond(seq), p["fnet_w"], p["fnet_b"][j])
        elif kind == 1:
            n_main = DN_HEADS * (2 * DN_DK + 2 * DN_DV)
            main, gt = _norm_proj(x, p["norm_mix"][layer], mod, tokens_per_cond, p["dn_w_in"], j, n_main)
            s0 = None if st_d is None else st_d[:, j]
            o, sfin = _gdn_scan(main, gt, batch, seq, p["dn_conv_w"][j], p["dn_a_log"][j], p["dn_dt_bias"][j],
                                s0, emit_state)
            if emit_state:
                new_d = sfin
            x = _gated_out(x, o, main, 3, p["dn_norm"][j], mod, tile_cond(OUT_TOKEN_TILE), p["dn_w_out"][j],
                           DN_HEADS, jax.nn.silu)
        else:
            n_main = 2 * ML_HEADS * ML_DQK + 2 * ML_HEADS * ML_DV
            k_tile = (ML_HEADS * ML_DQK) // PROJ_COL_TILE
            assert ML_HEADS * ML_DQK == PROJ_COL_TILE
            main, gt, kt = _norm_proj(x, p["norm_mix"][layer], mod, tokens_per_cond, p["ml_w_in"], j, n_main,
                                      t_tile=k_tile)
            state0 = None
            if st_ml is not None:
                c0, n0, m0 = (s[:, j] for s in st_ml)
                pad = jnp.zeros(c0.shape[:-1] + (ML_AUG - ML_DV - 1,), F32)
                c_aug0 = jnp.concatenate([c0, n0[..., None], pad], axis=-1)
                m0b = jnp.broadcast_to(m0[..., None, None], m0.shape + (1, 128))
                state0 = (c_aug0, m0b)
            o, fin = _mlstm_scan(main, kt, gt, batch, seq, p["ml_b_i"][j], p["ml_b_f"][j], state0, emit_state)
            if emit_state:
                new_ml = (fin[0], fin[1][..., 0], fin[2][..., 0, 0])
            x = _gated_out(x, o, main, 2, p["ml_norm"][j], mod, tile_cond(OUT_TOKEN_TILE), p["ml_w_out"][j],
                           ML_HEADS, jax.nn.sigmoid)
        x = _conv_ffn(x, seq, rows, layer, p["norm_ffn"][layer], mod, tile_cond(FFN_TOKEN_TILE), p["ffn_w_up"],
                      p["ffn_conv_w"], p["ffn_conv_b"], p["ffn_w_down"], p["norm_final"],
                      final_norm=(layer == depth - 1))
    return x, new_d, new_ml


def kernel(x_prompt, x_sample, state_delta, state_mlstm_c, state_mlstm_n, state_mlstm_m, c, c_ctx, w_ada, b_ada, norm_mix, norm_ffn, norm_final, ffn_w_up, ffn_conv_w, ffn_conv_b, ffn_w_down, fnet_w, fnet_b, dn_w_in, dn_conv_w, dn_a_log, dn_dt_bias, dn_norm, dn_w_out, ml_w_in, ml_b_i, ml_b_f, ml_norm, ml_w_out):
    p = dict(w_ada=w_ada, norm_mix=norm_mix, norm_ffn=norm_ffn, norm_final=norm_final, ffn_w_up=ffn_w_up,
             ffn_conv_w=ffn_conv_w, ffn_conv_b=ffn_conv_b, ffn_w_down=ffn_w_down, fnet_w=fnet_w, fnet_b=fnet_b,
             dn_w_in=dn_w_in, dn_conv_w=dn_conv_w, dn_a_log=dn_a_log, dn_dt_bias=dn_dt_bias, dn_norm=dn_norm,
             dn_w_out=dn_w_out, ml_w_in=ml_w_in, ml_b_i=ml_b_i, ml_b_f=ml_b_f, ml_norm=ml_norm, ml_w_out=ml_w_out)
    b_ctx, t_ctx, _ = x_prompt.shape
    b_smp, t_smp, _ = x_sample.shape
    depth = w_ada.shape[0]

    cond8 = jnp.concatenate([c_ctx[None, :], c, jnp.zeros((8 - 1 - b_smp, D_MODEL), F32)], axis=0)
    mods = _adaln_all(cond8, w_ada, b_ada)
    mods_ctx = [mods[l, 0:1].reshape(1, 1, -1) for l in range(depth)]
    mods_smp = [mods[l, 1:1 + b_smp].reshape(b_smp, 1, -1) for l in range(depth)]

    y_ctx, new_d, new_ml = _trunk(x_prompt.reshape(b_ctx * t_ctx, D_MODEL), b_ctx, t_ctx, 1, mods_ctx,
                                  b_ctx * t_ctx, None, None, True, p)
    y_smp, _, _ = _trunk(x_sample.reshape(b_smp * t_smp, D_MODEL), b_smp, t_smp, t_smp // GRID_W, mods_smp,
                         t_smp, state_delta, (state_mlstm_c, state_mlstm_n, state_mlstm_m), False, p)

    new_c, new_n, new_m = new_ml
    return (y_ctx.reshape(b_ctx, t_ctx, D_MODEL), y_smp.reshape(b_smp, t_smp, D_MODEL),
            new_d[:, None], new_c[:, None], new_n[:, None], new_m[:, None])
```

```python
import collections
import functools

import numpy as np
import jax
import jax.numpy as jnp
from jax import lax
from jax.experimental import pallas as pl
from jax.experimental.pallas import tpu as pltpu

F32 = jnp.float32
BF16 = jnp.bfloat16

D_MODEL = 1024
EPS = 1e-6
N_DIR = 2
CHUNK = 64
LOG_CHUNK = 6
GROUP = 256
CHUNKS_PER_GROUP = GROUP // CHUNK
FNET_GROUP_DIM = 256
DN_HEADS, DN_DK, DN_DV, DN_CONV = 8, 128, 128, 5
ML_HEADS, ML_DQK, ML_DV = 8, 64, 128
D_FF = 2816
GRID_W = 64

FFN_TOKEN_TILE = 1024
FFN_FF_TILE = 256
FFN_SUB_TILES = 4
PROJ_TOKEN_TILE = 2048
PROJ_COL_TILE = 512
OUT_TOKEN_TILE = 512
ADA_COL_TILE = 1536
SCAN_STEP_POSITIONS = 1024
VMEM_LIMIT = 56 * 1024 * 1024


def _params(*sem):
    return pltpu.CompilerParams(dimension_semantics=sem, vmem_limit_bytes=VMEM_LIMIT)


def _dot(a, b):
    return jnp.dot(a.astype(BF16), b.astype(BF16), preferred_element_type=F32)


def _dot_nt(a, b):
    return lax.dot_general(a.astype(BF16), b.astype(BF16), (((1,), (1,)), ((), ())),
                           preferred_element_type=F32)


def _dot_tn(a, b):
    return lax.dot_general(a.astype(BF16), b.astype(BF16), (((0,), (0,)), ((), ())),
                           preferred_element_type=F32)


def _rms(x, g):
    return x * lax.rsqrt(jnp.mean(x * x, axis=-1, keepdims=True) + EPS) * g


def _norm_mod(x, g, sc, sh):
    return _rms(x, g) * (1.0 + sc) + sh


def _softplus(x):
    return jnp.maximum(x, 0.0) + jnp.log1p(jnp.exp(-jnp.abs(x)))


def _mod_spec(chunk, cond_of_tile):
    return pl.BlockSpec((1, 1, D_MODEL), lambda i, *_: (cond_of_tile(i), 0, chunk))


def _adaln_kernel(c_ref, w_ref, b_ref, o_ref):
    s = jax.nn.silu(c_ref[...])
    o_ref[0] = _dot(s, w_ref[0]) + b_ref[0]


def _adaln_all(cond8, w_ada, b_ada):
    depth = w_ada.shape[0]
    n_out = w_ada.shape[2]
    tn = ADA_COL_TILE
    return pl.pallas_call(
        _adaln_kernel,
        grid=(depth, n_out // tn),
        in_specs=[pl.BlockSpec((8, D_MODEL), lambda l, j: (0, 0)),
                  pl.BlockSpec((1, D_MODEL, tn), lambda l, j: (l, 0, j)),
                  pl.BlockSpec((1, 1, tn), lambda l, j: (l, 0, j))],
        out_specs=pl.BlockSpec((1, 8, tn), lambda l, j: (l, 0, j)),
        out_shape=jax.ShapeDtypeStruct((depth, 8, n_out), F32),
        compiler_params=_params("parallel", "parallel"),
        name="adaln",
    )(cond8, w_ada, b_ada.reshape(depth, 1, n_out))


def _norm_proj_kernel(x_ref, g_ref, sc_ref, sh_ref, w_ref, wg_ref, o_ref, gate_ref, *rest, t_tile):
    h_scr = rest[-1]

    @pl.when(pl.program_id(1) == 0)
    def _():
        h = _norm_mod(x_ref[...], g_ref[...], sc_ref[0], sh_ref[0]).astype(BF16)
        h_scr[...] = h
        gate_ref[...] = _dot_nt(wg_ref[...], h)

    y = _dot_nt(h_scr[...], w_ref[...])
    o_ref[...] = y.astype(o_ref.dtype)

    if t_tile is not None:
        t_ref = rest[0]

        @pl.when(pl.program_id(1) == t_tile)
        def _():
            for g in range(t_ref.shape[0]):
                t_ref[g] = y[g * GROUP:(g + 1) * GROUP, :].T


def _norm_proj(x, norm_g, mod, tokens_per_cond, w_stack, layer, n_main, t_tile=None):
    n_tok = x.shape[0]
    tm, tn = min(PROJ_TOKEN_TILE, tokens_per_cond), PROJ_COL_TILE
    cond_of_tile = lambda i: (i * tm) // tokens_per_cond
    w_t = jnp.swapaxes(w_stack, 1, 2)
    n_gate = w_t.shape[1] - n_main
    assert n_main % n_gate == 0 and n_gate % 8 == 0
    out_specs = [pl.BlockSpec((tm, tn), lambda i, j: (i, j)),
                 pl.BlockSpec((n_gate, tm), lambda i, j: (0, i))]
    out_shape = [jax.ShapeDtypeStruct((n_tok, n_main), BF16),
                 jax.ShapeDtypeStruct((n_gate, n_tok), F32)]
    if t_tile is not None:
        out_specs.append(pl.BlockSpec((tm // GROUP, tn, GROUP), lambda i, j: (i, 0, 0)))
        out_shape.append(jax.ShapeDtypeStruct((n_tok // GROUP, tn, GROUP), F32))
    return pl.pallas_call(
        functools.partial(_norm_proj_kernel, t_tile=t_tile),
        grid=(n_tok // tm, n_main // tn),
        in_specs=[pl.BlockSpec((tm, D_MODEL), lambda i, j: (i, 0)),
                  pl.BlockSpec((1, D_MODEL), lambda i, j: (0, 0)),
                  _mod_spec(1, cond_of_tile), _mod_spec(0, cond_of_tile),
                  pl.BlockSpec((None, tn, D_MODEL), lambda i, j: (layer, j, 0)),
                  pl.BlockSpec((None, n_gate, D_MODEL), lambda i, j: (layer, n_main // n_gate, 0))],
        out_specs=out_specs, out_shape=out_shape,
        scratch_shapes=[pltpu.VMEM((tm, D_MODEL), BF16)],
        compiler_params=_params("parallel", "arbitrary"),
        name="norm_proj",
    )(x, norm_g.reshape(1, D_MODEL), mod, mod, w_t, w_t)


def _gated_out_kernel(x_ref, o_ref, z_ref, ng_ref, g1_ref, w_ref, y_ref, wb_scr, *, n_heads, gate_fn):
    @pl.when(pl.program_id(0) == 0)
    def _():
        wb_scr[...] = w_ref[...].astype(BF16)

    dv = o_ref.shape[1] // n_heads
    parts = []
    for h in range(n_heads):
        sl = slice(h * dv, (h + 1) * dv)
        parts.append((_rms(o_ref[:, sl], ng_ref[...]) * gate_fn(z_ref[:, sl].astype(F32))).astype(BF16))
    hs = jnp.concatenate(parts, axis=1)
    y = jnp.dot(hs, wb_scr[...], preferred_element_type=F32)
    y_ref[...] = x_ref[...] + g1_ref[0] * y


def _gated_out(x, o, main, z_block, norm_g, mod, cond_of_tile, w_out, n_heads, gate_fn):
    n_tok = x.shape[0]
    tm = OUT_TOKEN_TILE
    width = o.shape[1]
    return pl.pallas_call(
        functools.partial(_gated_out_kernel, n_heads=n_heads, gate_fn=gate_fn),
        grid=(n_tok // tm,),
        in_specs=[pl.BlockSpec((tm, D_MODEL), lambda i: (i, 0)),
                  pl.BlockSpec((tm, width), lambda i: (i, 0)),
                  pl.BlockSpec((tm, width), lambda i: (i, z_block)),
                  pl.BlockSpec((1, width // n_heads), lambda i: (0, 0)),
                  _mod_spec(2, cond_of_tile),
                  pl.BlockSpec((width, D_MODEL), lambda i: (0, 0))],
        out_specs=pl.BlockSpec((tm, D_MODEL), lambda i: (i, 0)),
        out_shape=jax.ShapeDtypeStruct((n_tok, D_MODEL), F32),
        scratch_shapes=[pltpu.VMEM((width, D_MODEL), BF16)],
        compiler_params=_params("arbitrary"),
        name="gated_out",
    )(x, o, main, norm_g.reshape(1, -1), mod, w_out)


def _dft_mats(n):
    k = np.arange(n, dtype=np.int64)
    ang = 2.0 * np.pi * ((k[:, None] * k[None, :]) % n).astype(np.float64) / n
    s = 1.0 / np.sqrt(n)
    return np.cos(ang) * s, np.sin(ang) * s


def _fnet_kernel(x_ref, ng_ref, sc_ref, sh_ref, g1_ref, ct_ref, st_ref, cs_ref, w_ref, b_ref, y_ref, wb_scr):
    @pl.when(pl.program_id(0) == 0)
    def _():
        wb_scr[...] = w_ref[...].astype(BF16)

    x = x_ref[...]
    h = _norm_mod(x, ng_ref[...], sc_ref[0], sh_ref[0]).astype(BF16)
    p = jnp.dot(ct_ref[...], h, preferred_element_type=F32)
    q = jnp.dot(st_ref[...], h, preferred_element_type=F32)
    gd = FNET_GROUP_DIM
    parts = []
    for g in range(D_MODEL // gd):
        sl = slice(g * gd, (g + 1) * gd)
        pq = jnp.concatenate([p[:, sl], q[:, sl]], axis=1).astype(BF16)
        parts.append(jnp.dot(pq, cs_ref[...], preferred_element_type=F32).astype(BF16))
    f = jnp.concatenate(parts, axis=1)
    y = jnp.dot(f, wb_scr[...], preferred_element_type=F32) + b_ref[...]
    y_ref[...] = x + g1_ref[0] * y


def _fnet_mix(x, seq, layer, norm_g, mod, cond_of_seq, w, b):
    n_tok = x.shape[0]
    ct, st = _dft_mats(seq)
    cc, sc = _dft_mats(FNET_GROUP_DIM)
    cs = np.concatenate([cc, -sc], axis=0)
    const = lambda i: (0, 0)
    ct, st, cs = (jnp.asarray(m, F32).astype(BF16) for m in (ct, st, cs))
    return pl.pallas_call(
        _fnet_kernel,
        grid=(n_tok // seq,),
        in_specs=[pl.BlockSpec((seq, D_MODEL), lambda i: (i, 0)),
                  pl.BlockSpec((1, D_MODEL), const),
                  _mod_spec(1, cond_of_seq), _mod_spec(0, cond_of_seq), _mod_spec(2, cond_of_seq),
                  pl.BlockSpec((seq, seq), const),
                  pl.BlockSpec((seq, seq), const),
                  pl.BlockSpec((2 * FNET_GROUP_DIM, FNET_GROUP_DIM), const),
                  pl.BlockSpec((None, D_MODEL, D_MODEL), lambda i: (layer, 0, 0)),
                  pl.BlockSpec((1, D_MODEL), const)],
        out_specs=pl.BlockSpec((seq, D_MODEL), lambda i: (i, 0)),
        out_shape=jax.ShapeDtypeStruct((n_tok, D_MODEL), F32),
        scratch_shapes=[pltpu.VMEM((D_MODEL, D_MODEL), BF16)],
        compiler_params=_params("arbitrary"),
        name="fnet",
    )(x, norm_g.reshape(1, D_MODEL), mod, mod, mod,
      ct, st, cs, w, b.reshape(1, D_MODEL))


def _shift_rows(x, delta):
    n = x.shape[0]
    return pltpu.roll(x, (-delta) % n, 0)


def _dwconv_tokens(g, cw_ref, seq, rows):
    tm = g.shape[0]
    width = seq // rows
    assert seq & (seq - 1) == 0 and width & (width - 1) == 0
    t = lax.broadcasted_iota(jnp.int32, (tm, 1), 0)
    pos = t & (seq - 1)
    col = pos & (width - 1)
    row = pos >> (width.bit_length() - 1)
    g_cols = (_shift_rows(g, -1) * (col >= 1).astype(F32), g,
              _shift_rows(g, 1) * (col <= width - 2).astype(F32))
    out = None
    for di in (-1, 0, 1):
        if rows == 1 and di != 0:
            continue
        r = sum(g_cols[dj] * cw_ref[pl.ds(3 * (di + 1) + dj, 1), :] for dj in range(3))
        if di != 0:
            ok = (row + di >= 0) & (row + di <= rows - 1)
            r = _shift_rows(r, di * width) * ok.astype(F32)
        out = r if out is None else out + r
    return out


def _ffn_kernel(x_ref, ng_ref, sc_ref, sh_ref, g2_ref, wa_ref, wg_ref, cw_ref, cb_ref, wd_ref, nf_ref,
                y_ref, h_scr, acc_scr, *, seq, rows, final_norm):
    j = pl.program_id(1)

    @pl.when(j == 0)
    def _():
        h_scr[...] = _norm_mod(x_ref[...], ng_ref[...], sc_ref[0], sh_ref[0]).astype(BF16)
        acc_scr[...] = jnp.zeros_like(acc_scr)

    tm = h_scr.shape[0]
    sub = tm // FFN_SUB_TILES
    parts = [pl.ds(r, sub) for r in range(0, tm, sub)]
    wa, wg, wd = (w[...].astype(BF16) for w in (wa_ref, wg_ref, wd_ref))
    gs = [jnp.dot(h_scr[p, :], wg, preferred_element_type=F32) for p in parts]
    as_ = [jnp.dot(h_scr[p, :], wa, preferred_element_type=F32) for p in parts]
    per_seq = max(seq // sub, 1)
    convs = []
    for i in range(0, len(parts), per_seq):
        g = gs[i] if per_seq == 1 else jnp.concatenate(gs[i:i + per_seq], axis=0)
        c = _dwconv_tokens(g, cw_ref, seq, rows) + cb_ref[...]
        convs += [c[k * sub:(k + 1) * sub] for k in range(per_seq)]
    acts = [(jax.nn.silu(c) * a).astype(BF16) for c, a in zip(convs, as_)]
    for p, act in zip(parts, acts):
        acc_scr[p, :] += jnp.dot(act, wd, preferred_element_type=F32)

    @pl.when(j == pl.num_programs(1) - 1)
    def _():
        y = x_ref[...] + g2_ref[0] * acc_scr[...]
        if final_norm:
            y = _rms(y, nf_ref[...])
        y_ref[...] = y


def _conv_ffn(x, seq, rows, layer, norm_g, mod, cond_of_tile, w_up, conv_w, conv_b, w_down, norm_final, final_norm):
    n_tok = x.shape[0]
    tm, tf = FFN_TOKEN_TILE, FFN_FF_TILE
    n_ff_tiles = D_FF // tf
    depth = w_up.shape[0]
    return pl.pallas_call(
        functools.partial(_ffn_kernel, seq=seq, rows=rows, final_norm=final_norm),
        grid=(n_tok // tm, n_ff_tiles),
        in_specs=[pl.BlockSpec((tm, D_MODEL), lambda i, j: (i, 0)),
                  pl.BlockSpec((1, D_MODEL), lambda i, j: (0, 0)),
                  _mod_spec(4, cond_of_tile), _mod_spec(3, cond_of_tile), _mod_spec(5, cond_of_tile),
                  pl.BlockSpec((None, D_MODEL, tf), lambda i, j: (layer, 0, j)),
                  pl.BlockSpec((None, D_MODEL, tf), lambda i, j: (layer, 0, n_ff_tiles + j)),
                  pl.BlockSpec((None, 9, tf), lambda i, j: (layer, 0, j)),
                  pl.BlockSpec((None, 1, tf), lambda i, j: (layer, 0, j)),
                  pl.BlockSpec((None, tf, D_MODEL), lambda i, j: (layer, j, 0)),
                  pl.BlockSpec((1, D_MODEL), lambda i, j: (0, 0))],
        out_specs=pl.BlockSpec((tm, D_MODEL), lambda i, j: (i, 0)),
        out_shape=jax.ShapeDtypeStruct((n_tok, D_MODEL), F32),
        scratch_shapes=[pltpu.VMEM((tm, D_MODEL), BF16), pltpu.VMEM((tm, D_MODEL), F32)],
        compiler_params=_params("parallel", "arbitrary"),
        name="conv_ffn",
    )(x, norm_g.reshape(1, D_MODEL), mod, mod, mod, w_up, w_up, conv_w.reshape(depth, 9, D_FF),
      conv_b.reshape(depth, 1, D_FF), w_down, norm_final.reshape(1, D_MODEL))


GroupMasks = collections.namedtuple("GroupMasks", "r c same incl strict")


def _chunk_masks():
    r = lax.broadcasted_iota(jnp.int32, (CHUNK, CHUNK), 0)
    c = lax.broadcasted_iota(jnp.int32, (CHUNK, CHUNK), 1)
    return (r >= c, r <= c)


def _group_masks():
    r = lax.broadcasted_iota(jnp.int32, (GROUP, GROUP), 0)
    c = lax.broadcasted_iota(jnp.int32, (GROUP, GROUP), 1)
    same = (r >> LOG_CHUNK) == (c >> LOG_CHUNK)
    return GroupMasks(r, c, same, (same & (r >= c), same & (r <= c)), (same & (r > c), same & (r < c)))


def _gate_layouts(gt, batch, n_heads, hp):
    seq = gt.shape[1] // batch
    n_groups, n_chunks = seq // GROUP, seq // CHUNK
    assert seq % GROUP == 0 and n_groups <= 8
    g = jnp.transpose(gt.reshape(N_DIR, 2, n_heads, batch, seq), (3, 2, 0, 1, 4))
    rows = g.reshape(g.shape[:4] + (n_groups, GROUP))
    rows = jnp.pad(rows, ((0, 0),) * 4 + ((0, 8 - n_groups), (0, 0)))
    cols = jnp.swapaxes(g.reshape(batch, n_heads // hp, hp * N_DIR * 2 * n_chunks, CHUNK), -1, -2)
    return rows, cols


def _col_gate(gcol_ref, hh, d, kind, n_chunks):
    start = ((hh * N_DIR + d) * 2 + kind) * n_chunks
    return gcol_ref[0, 0, :, start:start + n_chunks]


def _split3(x):
    x1 = x.astype(BF16)
    r1 = x - x1.astype(F32)
    x2 = r1.astype(BF16)
    return x1, x2, (r1 - x2.astype(F32)).astype(BF16)


def _dot_mask_rhs(x, mask):
    m = x.shape[0]
    y = jnp.dot(jnp.concatenate(_split3(x), axis=0), jnp.where(mask, 1.0, 0.0).astype(BF16),
                preferred_element_type=F32)
    return y[:m] + y[m:2 * m] + y[2 * m:]


def _dot_mask_lhs(mask, x):
    mb = jnp.where(mask, 1.0, 0.0).astype(BF16)
    y1, y2, y3 = (jnp.dot(mb, p, preferred_element_type=F32) for p in _split3(x))
    return y1 + y2 + y3


def _cumsum_rows(row, d, gm):
    return _dot_mask_rhs(row, gm.incl[1 - d])


def _cumsum_cols(col, d, cm):
    return _dot_mask_lhs(cm[d], col)


def _store_cols(dst, col, n_chunks):
    for n in range(n_chunks):
        dst[n * CHUNK:(n + 1) * CHUNK, :] = jnp.broadcast_to(col[:, n:n + 1], (CHUNK, 128))


def _wide(x, width=None):
    reps = (GROUP if width is None else width) // x.shape[1]
    return x if reps == 1 else jnp.concatenate([x] * reps, axis=1)


def _rows(i, size):
    if isinstance(i, int):
        return pl.ds(i * size, size)
    return pl.ds(pl.multiple_of(i * size, size), size)


def _loop(n, body, init):
    if n == 1:
        return body(0, init)
    return lax.fori_loop(0, n, body, init)


def _conv_silu(x, w_ref):
    n = x.shape[0]
    k = w_ref.shape[0]
    t = lax.broadcasted_iota(jnp.int32, (n, 1), 0)
    acc = None
    for j in range(k):
        delta = j - k // 2
        if delta == 0:
            term = x
        else:
            ok = (t + delta >= 0) & (t + delta <= n - 1)
            term = _shift_rows(x, delta) * ok.astype(F32)
        term = term * w_ref[pl.ds(j, 1), :]
        acc = term if acc is None else acc + term
    return jax.nn.silu(acc)


def _l2norm(x):
    return x * lax.rsqrt(jnp.sum(x * x, axis=-1, keepdims=True) + EPS)


def _blockdiag_tri_inverse(mats, gm):
    base = (gm.r >> 3) == (gm.c >> 3)
    eye = jnp.where(gm.r == gm.c, 1.0, 0.0)
    ps = [-jnp.where(base, a, 0.0) for a in mats]
    invs = [eye + p for p in ps]
    for _ in range(2):
        ps = [_dot(p, p) for p in ps]
        invs = [inv + _dot(inv, p) for inv, p in zip(invs, ps)]
    for s in (3, 4, 5):
        join = ((gm.r >> s) ^ (gm.c >> s)) == 1
        tmp = [_dot(jnp.where(join, a, 0.0), inv) for a, inv in zip(mats, invs)]
        invs = [inv - _dot(inv, t) for inv, t in zip(invs, tmp)]
    return invs


def _gdn_kernel(*refs, seq, hp, zero_init, emit_state):
    alog_ref, dtb_ref, q_ref, k_ref, v_ref, cwq_ref, cwk_ref, cwv_ref, grow_ref, gcol_ref = refs[:10]
    pos = 10
    s0_ref = None
    if not zero_init:
        s0_ref = refs[pos]
        pos += 1
    o_ref = refs[pos]
    pos += 1
    sfin_ref = None
    if emit_state:
        sfin_ref = refs[pos]
        pos += 1
    (q_scr, k_scr, v_scr, gcr_scr, gcb_scr, bb_scr, gl_scr,
     u_scr, w_scr, qd_scr, kd_scr, qkd_scr) = refs[pos:]

    n_chunks = seq // CHUNK
    cm = _chunk_masks()
    gm = _group_masks()
    head_dirs = [(hh, d) for hh in range(hp) for d in range(N_DIR)]
    lanes = lambda hh: slice(hh * DN_DK, (hh + 1) * DN_DK)

    q_scr[...] = _conv_silu(q_ref[...].astype(F32), cwq_ref)
    k_scr[...] = _conv_silu(k_ref[...].astype(F32), cwk_ref)
    v_scr[...] = _conv_silu(v_ref[...].astype(F32), cwv_ref)
    for hh in range(hp):
        q_scr[:, lanes(hh)] = _l2norm(q_scr[:, lanes(hh)]) * (DN_DK ** -0.5)
        k_scr[:, lanes(hh)] = _l2norm(k_scr[:, lanes(hh)])

    for hh, d in head_dirs:
        head = pl.program_id(1) * hp + hh
        neg_a = -jnp.exp(jnp.full((1, 1), alog_ref[d, head], F32))
        dtb = dtb_ref[d, head]
        gcr_scr[hh, d] = _cumsum_rows(neg_a * _softplus(grow_ref[0, hh, d, 0] + dtb), d, gm)
        gc_col = _cumsum_cols(neg_a * _softplus(_col_gate(gcol_ref, hh, d, 0, n_chunks) + dtb), d, cm)
        _store_cols(gcb_scr.at[hh, d], gc_col, n_chunks)
        _store_cols(bb_scr.at[hh, d], jax.nn.sigmoid(_col_gate(gcol_ref, hh, d, 1, n_chunks)), n_chunks)
        last = CHUNK - 1 if d == 0 else 0
        _store_cols(gl_scr.at[hh, d], jnp.broadcast_to(gc_col[last:last + 1, :], gc_col.shape), n_chunks)

    def group_step(gi, carry):
        rows = _rows(gi, GROUP)
        kqs = []
        for hh in range(hp):
            kg, qg = k_scr[rows, lanes(hh)], q_scr[rows, lanes(hh)]
            kqs.append(_dot_nt(jnp.concatenate([kg, qg], axis=0), kg))
        mats = []
        for hh, d in head_dirs:
            gcb = gcb_scr[hh, d, rows, :]
            gcr = gcr_scr[hh, d, pl.ds(gi, 1), :]
            decay = jnp.exp(jnp.where(gm.incl[d], _wide(gcb) - gcr, -jnp.inf))
            mats.append(jnp.where(gm.strict[d], kqs[hh][:GROUP] * _wide(bb_scr[hh, d, rows, :]) * decay, 0.0))
            qkd = kqs[hh][GROUP:] * decay
            qkd_scr[hh, d, rows, :] = sum(qkd[:, n * CHUNK:(n + 1) * CHUNK] for n in range(CHUNKS_PER_GROUP))
        t_invs = _blockdiag_tri_inverse(mats, gm)
        for (hh, d), t_inv in zip(head_dirs, t_invs):
            qg, kg, vg = q_scr[rows, lanes(hh)], k_scr[rows, lanes(hh)], v_scr[rows, lanes(hh)]
            gcb = gcb_scr[hh, d, rows, :]
            bb = bb_scr[hh, d, rows, :]
            e_gc = jnp.exp(gcb)
            uw = _dot(t_inv, jnp.concatenate([vg * bb, kg * bb * e_gc], axis=1))
            u_scr[hh, d, rows, :] = uw[:, :DN_DV]
            w_scr[hh, d, rows, :] = uw[:, DN_DV:]
            qd_scr[hh, d, rows, :] = qg * e_gc
            kd_scr[hh, d, rows, :] = kg * jnp.exp(gl_scr[hh, d, rows, :] - gcb)
        return carry

    _loop(seq // GROUP, group_step, 0)

    o_ref[...] = jnp.zeros_like(o_ref)

    def chunk_step(n, carry):
        idxs = [n if d == 0 else n_chunks - 1 - n for _, d in head_dirs]
        rows = [_rows(idx, CHUNK) for idx in idxs]
        wqs = [_dot(jnp.concatenate([w_scr[hh, d, r, :], qd_scr[hh, d, r, :]], axis=0), s)
               for (hh, d), r, s in zip(head_dirs, rows, carry)]
        v_news = [u_scr[hh, d, r, :] - wq[:CHUNK] for (hh, d), r, wq in zip(head_dirs, rows, wqs)]
        o_ns = [wq[CHUNK:] + _dot(qkd_scr[hh, d, r, :], v_new)
                for (hh, d), r, wq, v_new in zip(head_dirs, rows, wqs, v_news)]
        new = []
        for (hh, d), idx, r, s, v_new in zip(head_dirs, idxs, rows, carry, v_news):
            s_decay = jnp.exp(gl_scr[hh, d, pl.ds(idx * CHUNK, 1), :])
            new.append(s * s_decay + _dot_tn(kd_scr[hh, d, r, :], v_new))
        for (hh, d), r, o_n in zip(head_dirs, rows, o_ns):
            o_ref[r, lanes(hh)] += o_n
        return tuple(new)

    if zero_init:
        init = (jnp.zeros((DN_DK, DN_DV), F32),) * len(head_dirs)
    else:
        init = tuple(s0_ref[0, d, hh] for hh, d in head_dirs)
    fin = lax.fori_loop(0, n_chunks, chunk_step, init)
    if emit_state:
        for i, (hh, d) in enumerate(head_dirs):
            sfin_ref[0, d, hh] = fin[i]


def _gdn_scan(main, gt, batch, seq, conv_w, a_log, dt_bias, s0, emit_state):
    n_tok = batch * seq
    nh = DN_HEADS
    hp = min(nh, max(2, SCAN_STEP_POSITIONS // seq))
    n_hb = nh // hp
    grow, gcol = _gate_layouts(gt, batch, nh, hp)
    zero_init = s0 is None
    smem = pl.BlockSpec(memory_space=pltpu.SMEM)
    qkv_spec = lambda off: pl.BlockSpec((seq, hp * DN_DK), lambda b, h: (b, off + h))
    cw_spec = lambda off: pl.BlockSpec((DN_CONV, hp * DN_DK), lambda b, h: (0, off + h))
    state_spec = pl.BlockSpec((1, N_DIR, hp, DN_DK, DN_DV), lambda b, h: (b, 0, h, 0, 0))
    in_specs = [smem, smem, qkv_spec(0), qkv_spec(n_hb), qkv_spec(2 * n_hb),
                cw_spec(0), cw_spec(n_hb), cw_spec(2 * n_hb),
                pl.BlockSpec((1, hp) + grow.shape[2:], lambda b, h: (b, h, 0, 0, 0, 0)),
                pl.BlockSpec((1, 1) + gcol.shape[2:], lambda b, h: (b, h, 0, 0))]
    args = [a_log, dt_bias, main, main, main, conv_w, conv_w, conv_w, grow, gcol]
    if not zero_init:
        in_specs.append(state_spec)
        args.append(s0)
    out_specs = [pl.BlockSpec((seq, hp * DN_DV), lambda b, h: (b, h))]
    out_shape = [jax.ShapeDtypeStruct((n_tok, nh * DN_DV), F32)]
    if emit_state:
        out_specs.append(state_spec)
        out_shape.append(jax.ShapeDtypeStruct((batch, N_DIR, nh, DN_DK, DN_DV), F32))
    per_dir = lambda width: pltpu.VMEM((hp, N_DIR, seq, width), F32)
    qkv_scr = pltpu.VMEM((seq, hp * DN_DK), F32)
    outs = pl.pallas_call(
        functools.partial(_gdn_kernel, seq=seq, hp=hp, zero_init=zero_init, emit_state=emit_state),
        grid=(batch, n_hb),
        in_specs=in_specs, out_specs=out_specs, out_shape=out_shape,
        scratch_shapes=[qkv_scr, qkv_scr, qkv_scr,
                        pltpu.VMEM((hp, N_DIR, 8, GROUP), F32),
                        per_dir(128), per_dir(128), per_dir(128),
                        per_dir(DN_DV), per_dir(DN_DK), per_dir(DN_DK), per_dir(DN_DK), per_dir(CHUNK)],
        compiler_params=_params("parallel", "parallel"),
        name="gdn_scan",
    )(*args)
    return outs[0], (outs[1] if emit_state else None)


ML_AUG = 2 * ML_DV


def _mlstm_kernel(*refs, seq, hp, zero_init, emit_state):
    bi_ref, bf_ref, q_ref, k_ref, v_ref, kt_ref, grow_ref, gcol_ref = refs[:8]
    pos = 8
    c0_ref = m0_ref = None
    if not zero_init:
        c0_ref, m0_ref = refs[pos:pos + 2]
        pos += 2
    o_ref = refs[pos]
    pos += 1
    cfin_ref = nfin_ref = mfin_ref = None
    if emit_state:
        cfin_ref, nfin_ref, mfin_ref = refs[pos:pos + 3]
        pos += 3
    bcr_scr, lir_scr, kwr_scr, bcb_scr, mi_scr, dec_scr, dc_scr, cin_scr = refs[pos:]

    n_chunks = seq // CHUNK
    n_groups = seq // GROUP
    head0 = pl.program_id(1) * hp
    cm = _chunk_masks()
    gm = _group_masks()
    ones_col = jnp.where(lax.broadcasted_iota(jnp.int32, (GROUP, ML_DV), 1) == 0, 1.0, 0.0)
    head_dirs = [(hh, d) for hh in range(hp) for d in range(N_DIR)]

    m_fin = {}
    row_group = lax.broadcasted_iota(jnp.int32, (8, GROUP), 0)
    row_chunk = lax.broadcasted_iota(jnp.int32, (8, GROUP), 1) >> LOG_CHUNK
    for hh, d in head_dirs:
        b_i = bi_ref[d, head0 + hh]
        b_f = bf_ref[d, head0 + hh]
        li_row = grow_ref[0, hh, d, 0] + b_i
        lf_row = -_softplus(-(grow_ref[0, hh, d, 1] + b_f))
        bc_row = _cumsum_rows(lf_row, d, gm)
        lir_scr[hh, d] = li_row
        bcr_scr[hh, d] = bc_row
        w_row = _dot_mask_rhs(lf_row, gm.same) - bc_row + li_row
        m_out_row = jnp.zeros((8, GROUP), F32)
        li_col = _col_gate(gcol_ref, hh, d, 0, n_chunks) + b_i
        bc_col = _cumsum_cols(-_softplus(-(_col_gate(gcol_ref, hh, d, 1, n_chunks) + b_f)), d, cm)
        last = CHUNK - 1 if d == 0 else 0
        b_last = bc_col[last:last + 1, :]
        w_col = b_last - bc_col + li_col
        w_max = jnp.max(w_col, axis=0, keepdims=True)
        m = jnp.zeros((1, 1), F32) if zero_init else m0_ref[0, d, hh][:, 0:1]
        for step in range(n_chunks):
            n = step if d == 0 else n_chunks - 1 - step
            sl = slice(n * CHUNK, (n + 1) * CHUNK)
            m_new = jnp.maximum(b_last[:, n:n + 1] + m, w_max[:, n:n + 1])
            bcb_scr[hh, d, sl, :] = jnp.broadcast_to(bc_col[:, n:n + 1], (CHUNK, 128))
            mi_scr[hh, d, sl, :] = jnp.broadcast_to(m, (CHUNK, 128))
            dec_scr[hh, d, n:n + 1, :] = jnp.broadcast_to(jnp.exp(b_last[:, n:n + 1] + m - m_new), (1, 128))
            in_chunk = (row_group == n // CHUNKS_PER_GROUP) & (row_chunk == n % CHUNKS_PER_GROUP)
            m_out_row = jnp.where(in_chunk, m_new, m_out_row)
            m = m_new
        m_fin[hh, d] = m
        kwr_scr[hh, d] = jnp.exp(w_row - m_out_row)

    def load_kv(rows, hh):
        kg = k_ref[rows, hh * ML_DQK:(hh + 1) * ML_DQK].astype(F32)
        v_aug = jnp.concatenate([v_ref[rows, hh * ML_DV:(hh + 1) * ML_DV].astype(F32), ones_col], axis=1)
        return kg, v_aug

    def delta_step(gi, carry):
        rows = _rows(gi, GROUP)
        v_augs = [load_kv(rows, hh)[1] for hh in range(hp)]
        kt = kt_ref[gi]
        lhs = []
        for hh, d in head_dirs:
            kw_t = kt[hh * ML_DQK:(hh + 1) * ML_DQK, :] * kwr_scr[hh, d, pl.ds(gi, 1), :]
            lhs.append(jnp.where(gm.same, jnp.concatenate([kw_t] * CHUNKS_PER_GROUP, axis=0), 0.0))
        for (hh, d), kw in zip(head_dirs, lhs):
            dc_scr[hh, d, rows, :] = _dot(kw, v_augs[hh])
        return carry

    _loop(n_groups, delta_step, 0)

    def prefix_step(n, carry):
        new = []
        for (hh, d), c_aug in zip(head_dirs, carry):
            idx = n if d == 0 else n_chunks - 1 - n
            rows = _rows(idx, CHUNK)
            cin_scr[hh, d, rows, :] = c_aug
            new.append(c_aug * _wide(dec_scr[hh, d, pl.ds(idx, 1), :], ML_AUG) + dc_scr[hh, d, rows, :])
        return tuple(new)

    if zero_init:
        init = (jnp.zeros((ML_DQK, ML_AUG), F32),) * len(head_dirs)
    else:
        init = tuple(c0_ref[0, d, hh] for hh, d in head_dirs)
    fin = lax.fori_loop(0, n_chunks, prefix_step, init)

    def out_step(gi, carry):
        rows = _rows(gi, GROUP)
        kvs = [load_kv(rows, hh) for hh in range(hp)]
        qgs = [q_ref[rows, hh * ML_DQK:(hh + 1) * ML_DQK].astype(F32) * (ML_DQK ** -0.5) for hh in range(hp)]
        qks = [_dot_nt(qg, kv[0]) for qg, kv in zip(qgs, kvs)]
        chunk = lambda n: slice(n * CHUNK, (n + 1) * CHUNK)
        inters = []
        for hh, d in head_dirs:
            c_in = cin_scr[hh, d, rows, :]
            inters.append(jnp.concatenate(
                [_dot(qgs[hh][chunk(n)], c_in[chunk(n)]) for n in range(CHUNKS_PER_GROUP)], axis=0))
        m_ts, b_ms, ss = [], [], []
        for hh, d in head_dirs:
            b_colb = bcb_scr[hh, d, rows, :]
            b_m = b_colb[:, 0:1] + mi_scr[hh, d, rows, :][:, 0:1]
            d_log = jnp.where(gm.incl[d], _wide(b_colb) - bcr_scr[hh, d, pl.ds(gi, 1), :]
                              + lir_scr[hh, d, pl.ds(gi, 1), :], -jnp.inf)
            m_t = jnp.maximum(b_m, jnp.max(d_log, axis=-1, keepdims=True))
            m_ts.append(m_t)
            b_ms.append(b_m)
            ss.append(qks[hh] * jnp.exp(d_log - m_t))
        intras = [_dot(s, kvs[hh][1]) for (hh, d), s in zip(head_dirs, ss)]
        hs = []
        for inter, intra, m_t, b_m in zip(inters, intras, m_ts, b_ms):
            num = jnp.exp(b_m - m_t) * inter + intra
            den = num[:, ML_DV:ML_DV + 1]
            hs.append(num[:, :ML_DV] / jnp.maximum(jnp.abs(den), jnp.exp(-m_t)))
        for hh in range(hp):
            o_ref[rows, hh * ML_DV:(hh + 1) * ML_DV] = hs[N_DIR * hh] + hs[N_DIR * hh + 1]
        return carry

    _loop(n_groups, out_step, 0)

    if emit_state:
        for i, (hh, d) in enumerate(head_dirs):
            cfin_ref[0, d, hh] = fin[i][:, :ML_DV]
            nfin_ref[0, d, hh] = fin[i][:, ML_DV:]
            mfin_ref[0, d, hh] = jnp.broadcast_to(m_fin[hh, d], (1, 128))


def _mlstm_scan(main, kt, gt, batch, seq, b_i, b_f, state0, emit_state):
    n_tok = batch * seq
    n_groups = seq // GROUP
    n_chunks = seq // CHUNK
    nh = ML_HEADS
    hp = min(nh, max(2, SCAN_STEP_POSITIONS // seq))
    n_hb = nh // hp
    grow, gcol = _gate_layouts(gt, batch, nh, hp)
    zero_init = state0 is None
    smem = pl.BlockSpec(memory_space=pltpu.SMEM)
    k_off = (nh * ML_DQK) // (hp * ML_DQK)
    v_off = (2 * nh * ML_DQK) // (hp * ML_DV)
    c_spec = pl.BlockSpec((1, N_DIR, hp, ML_DQK, ML_AUG), lambda b, p: (b, 0, p, 0, 0))
    m_spec = pl.BlockSpec((1, N_DIR, hp, 1, 128), lambda b, p: (b, 0, p, 0, 0))
    in_specs = [smem, smem,
                pl.BlockSpec((seq, hp * ML_DQK), lambda b, p: (b, p)),
                pl.BlockSpec((seq, hp * ML_DQK), lambda b, p: (b, k_off + p)),
                pl.BlockSpec((seq, hp * ML_DV), lambda b, p: (b, v_off + p)),
                pl.BlockSpec((n_groups, hp * ML_DQK, GROUP), lambda b, p: (b, p, 0)),
                pl.BlockSpec((1, hp) + grow.shape[2:], lambda b, p: (b, p, 0, 0, 0, 0)),
                pl.BlockSpec((1, 1) + gcol.shape[2:], lambda b, p: (b, p, 0, 0))]
    args = [b_i, b_f, main, main, main, kt, grow, gcol]
    if not zero_init:
        in_specs += [c_spec, m_spec]
        args += list(state0)
    out_specs = [pl.BlockSpec((seq, hp * ML_DV), lambda b, p: (b, p))]
    out_shape = [jax.ShapeDtypeStruct((n_tok, nh * ML_DV), F32)]
    if emit_state:
        half_spec = pl.BlockSpec((1, N_DIR, hp, ML_DQK, ML_DV), lambda b, p: (b, 0, p, 0, 0))
        out_specs += [half_spec, half_spec, m_spec]
        out_shape += [jax.ShapeDtypeStruct((batch, N_DIR, nh, ML_DQK, ML_DV), F32),
                      jax.ShapeDtypeStruct((batch, N_DIR, nh, ML_DQK, ML_AUG - ML_DV), F32),
                      jax.ShapeDtypeStruct((batch, N_DIR, nh, 1, 128), F32)]
    per_hd = lambda rows, width: pltpu.VMEM((hp, N_DIR, rows, width), F32)
    outs = pl.pallas_call(
        functools.partial(_mlstm_kernel, seq=seq, hp=hp, zero_init=zero_init, emit_state=emit_state),
        grid=(batch, n_hb),
        in_specs=in_specs, out_specs=out_specs, out_shape=out_shape,
        scratch_shapes=[per_hd(8, GROUP), per_hd(8, GROUP), per_hd(8, GROUP),
                        per_hd(seq, 128), per_hd(seq, 128),
                        per_hd(max(n_chunks, 8), 128),
                        per_hd(seq, ML_AUG), per_hd(seq, ML_AUG)],
        compiler_params=_params("parallel", "parallel"),
        name="mlstm_scan",
    )(*args)
    return outs[0], (tuple(outs[1:4]) if emit_state else None)


def _trunk(x, batch, seq, rows, mods, tokens_per_cond, st_d, st_ml, emit_state, p):
    depth = p["w_ada"].shape[0]
    tile_cond = lambda tile: (lambda i: (i * tile) // tokens_per_cond)
    new_d = new_ml = None
    for layer in range(depth):
        mod = mods[layer]
        kind, j = layer % 3, layer // 3
        if kind == 0:
            x = _fnet_mix(x, seq, j, p["norm_mix"][layer], mod, tile_cond(seq), p["fnet_w"], p["fnet_b"][j])
        elif kind == 1:
            n_main = DN_HEADS * (2 * DN_DK + 2 * DN_DV)
            main, gt = _norm_proj(x, p["norm_mix"][layer], mod, tokens_per_cond, p["dn_w_in"], j, n_main)
            s0 = None if st_d is None else st_d[:, j]
            o, sfin = _gdn_scan(main, gt, batch, seq, p["dn_conv_w"][j], p["dn_a_log"][j], p["dn_dt_bias"][j],
                                s0, emit_state)
            if emit_state:
                new_d = sfin
            x = _gated_out(x, o, main, 3, p["dn_norm"][j], mod, tile_cond(OUT_TOKEN_TILE), p["dn_w_out"][j],
                           DN_HEADS, jax.nn.silu)
        else:
            n_main = 2 * ML_HEADS * ML_DQK + 2 * ML_HEADS * ML_DV
            k_tile = (ML_HEADS * ML_DQK) // PROJ_COL_TILE
            assert ML_HEADS * ML_DQK == PROJ_COL_TILE
            main, gt, kt = _norm_proj(x, p["norm_mix"][layer], mod, tokens_per_cond, p["ml_w_in"], j, n_main,
                                      t_tile=k_tile)
            state0 = None
            if st_ml is not None:
                c0, n0, m0 = (s[:, j] for s in st_ml)
                pad = jnp.zeros(c0.shape[:-1] + (ML_AUG - ML_DV - 1,), F32)
                c_aug0 = jnp.concatenate([c0, n0[..., None], pad], axis=-1)
                m0b = jnp.broadcast_to(m0[..., None, None], m0.shape + (1, 128))
                state0 = (c_aug0, m0b)
            o, fin = _mlstm_scan(main, kt, gt, batch, seq, p["ml_b_i"][j], p["ml_b_f"][j], state0, emit_state)
            if emit_state:
                new_ml = (fin[0], fin[1][..., 0], fin[2][..., 0, 0])
            x = _gated_out(x, o, main, 2, p["ml_norm"][j], mod, tile_cond(OUT_TOKEN_TILE), p["ml_w_out"][j],
                           ML_HEADS, jax.nn.sigmoid)
        x = _conv_ffn(x, seq, rows, layer, p["norm_ffn"][layer], mod, tile_cond(FFN_TOKEN_TILE), p["ffn_w_up"],
                      p["ffn_conv_w"], p["ffn_conv_b"], p["ffn_w_down"], p["norm_final"],
                      final_norm=(layer == depth - 1))
    return x, new_d, new_ml


def kernel(x_prompt, x_sample, state_delta, state_mlstm_c, state_mlstm_n, state_mlstm_m, c, c_ctx, w_ada, b_ada, norm_mix, norm_ffn, norm_final, ffn_w_up, ffn_conv_w, ffn_conv_b, ffn_w_down, fnet_w, fnet_b, dn_w_in, dn_conv_w, dn_a_log, dn_dt_bias, dn_norm, dn_w_out, ml_w_in, ml_b_i, ml_b_f, ml_norm, ml_w_out):
    p = dict(w_ada=w_ada, norm_mix=norm_mix, norm_ffn=norm_ffn, norm_final=norm_final,
             ffn_w_up=ffn_w_up.astype(BF16), ffn_w_down=ffn_w_down.astype(BF16),
             ffn_conv_w=ffn_conv_w, ffn_conv_b=ffn_conv_b, fnet_w=fnet_w, fnet_b=fnet_b,
             dn_w_in=dn_w_in, dn_conv_w=dn_conv_w, dn_a_log=dn_a_log, dn_dt_bias=dn_dt_bias, dn_norm=dn_norm,
             dn_w_out=dn_w_out, ml_w_in=ml_w_in, ml_b_i=ml_b_i, ml_b_f=ml_b_f, ml_norm=ml_norm, ml_w_out=ml_w_out)
    b_ctx, t_ctx, _ = x_prompt.shape
    b_smp, t_smp, _ = x_sample.shape
    depth = w_ada.shape[0]

    cond8 = jnp.concatenate([c_ctx[None, :], c, jnp.zeros((8 - 1 - b_smp, D_MODEL), F32)], axis=0)
    mods = _adaln_all(cond8, w_ada, b_ada)
    mods_ctx = [mods[l, 0:1].reshape(1, 1, -1) for l in range(depth)]
    mods_smp = [mods[l, 1:1 + b_smp].reshape(b_smp, 1, -1) for l in range(depth)]

    y_ctx, new_d, new_ml = _trunk(x_prompt.reshape(b_ctx * t_ctx, D_MODEL), b_ctx, t_ctx, 1, mods_ctx,
                                  b_ctx * t_ctx, None, None, True, p)
    y_smp, _, _ = _trunk(x_sample.reshape(b_smp * t_smp, D_MODEL), b_smp, t_smp, t_smp // GRID_W, mods_smp,
                         t_smp, state_delta, (state_mlstm_c, state_mlstm_n, state_mlstm_m), False, p)

    new_c, new_n, new_m = new_ml
    return (y_ctx.reshape(b_ctx, t_ctx, D_MODEL), y_smp.reshape(b_smp, t_smp, D_MODEL),
            new_d[:, None], new_c[:, None], new_n[:, None], new_m[:, None])
```

```python
import collections
import functools

import numpy as np
import jax
import jax.numpy as jnp
from jax import lax
from jax.experimental import pallas as pl
from jax.experimental.pallas import tpu as pltpu

F32 = jnp.float32
BF16 = jnp.bfloat16

D_MODEL = 1024
EPS = 1e-6
N_DIR = 2
CHUNK = 64
LOG_CHUNK = 6
GROUP = 256
CHUNKS_PER_GROUP = GROUP // CHUNK
GDN_GROUP = 128
GDN_GROUPS_PER_STEP = 4
FNET_GROUP_DIM = 256
DN_HEADS, DN_DK, DN_DV, DN_CONV = 8, 128, 128, 5
ML_HEADS, ML_DQK, ML_DV = 8, 64, 128
D_FF = 2816
GRID_W = 64

FFN_TOKEN_TILE = 1024
FFN_FF_TILE = 256
FFN_SUB_TILES = 4
PROJ_TOKEN_TILE = 2048
PROJ_COL_TILE = 512
OUT_TOKEN_TILE = 512
ADA_COL_TILE = 1536
SCAN_STEP_POSITIONS = 1024
VMEM_LIMIT = 56 * 1024 * 1024


def _params(*sem):
    return pltpu.CompilerParams(dimension_semantics=sem, vmem_limit_bytes=VMEM_LIMIT)


def _dot(a, b):
    return jnp.dot(a.astype(BF16), b.astype(BF16), preferred_element_type=F32)


def _dot_nt(a, b):
    return lax.dot_general(a.astype(BF16), b.astype(BF16), (((1,), (1,)), ((), ())),
                           preferred_element_type=F32)


def _dot_tn(a, b):
    return lax.dot_general(a.astype(BF16), b.astype(BF16), (((0,), (0,)), ((), ())),
                           preferred_element_type=F32)


def _rms(x, g):
    return x * lax.rsqrt(jnp.mean(x * x, axis=-1, keepdims=True) + EPS) * g


def _norm_mod(x, g, sc, sh):
    return _rms(x, g) * (1.0 + sc) + sh


def _softplus(x):
    return jnp.maximum(x, 0.0) + jnp.log1p(jnp.exp(-jnp.abs(x)))


def _mod_spec(chunk, cond_of_tile):
    return pl.BlockSpec((1, 1, D_MODEL), lambda i, *_: (cond_of_tile(i), 0, chunk))


def _adaln_kernel(c_ref, w_ref, b_ref, o_ref):
    s = jax.nn.silu(c_ref[...])
    o_ref[0] = _dot(s, w_ref[0]) + b_ref[0]


def _adaln_all(cond8, w_ada, b_ada):
    depth = w_ada.shape[0]
    n_out = w_ada.shape[2]
    tn = ADA_COL_TILE
    return pl.pallas_call(
        _adaln_kernel,
        grid=(depth, n_out // tn),
        in_specs=[pl.BlockSpec((8, D_MODEL), lambda l, j: (0, 0)),
                  pl.BlockSpec((1, D_MODEL, tn), lambda l, j: (l, 0, j)),
                  pl.BlockSpec((1, 1, tn), lambda l, j: (l, 0, j))],
        out_specs=pl.BlockSpec((1, 8, tn), lambda l, j: (l, 0, j)),
        out_shape=jax.ShapeDtypeStruct((depth, 8, n_out), F32),
        compiler_params=_params("parallel", "parallel"),
        name="adaln",
    )(cond8, w_ada, b_ada.reshape(depth, 1, n_out))


def _norm_proj_kernel(x_ref, g_ref, sc_ref, sh_ref, w_ref, wg_ref, o_ref, gate_ref, *rest, t_tile):
    h_scr = rest[-1]

    @pl.when(pl.program_id(1) == 0)
    def _():
        h = _norm_mod(x_ref[...], g_ref[...], sc_ref[0], sh_ref[0]).astype(BF16)
        h_scr[...] = h
        gate_ref[...] = _dot_nt(wg_ref[...], h)

    y = _dot_nt(h_scr[...], w_ref[...])
    o_ref[...] = y.astype(o_ref.dtype)

    if t_tile is not None:
        t_ref = rest[0]

        @pl.when(pl.program_id(1) == t_tile)
        def _():
            for g in range(t_ref.shape[0]):
                t_ref[g] = y[g * GROUP:(g + 1) * GROUP, :].T


def _norm_proj(x, norm_g, mod, tokens_per_cond, w_stack, layer, n_main, t_tile=None):
    n_tok = x.shape[0]
    tm, tn = min(PROJ_TOKEN_TILE, tokens_per_cond), PROJ_COL_TILE
    cond_of_tile = lambda i: (i * tm) // tokens_per_cond
    w_t = jnp.swapaxes(w_stack, 1, 2)
    n_gate = w_t.shape[1] - n_main
    assert n_main % n_gate == 0 and n_gate % 8 == 0
    out_specs = [pl.BlockSpec((tm, tn), lambda i, j: (i, j)),
                 pl.BlockSpec((n_gate, tm), lambda i, j: (0, i))]
    out_shape = [jax.ShapeDtypeStruct((n_tok, n_main), BF16),
                 jax.ShapeDtypeStruct((n_gate, n_tok), F32)]
    if t_tile is not None:
        out_specs.append(pl.BlockSpec((tm // GROUP, tn, GROUP), lambda i, j: (i, 0, 0)))
        out_shape.append(jax.ShapeDtypeStruct((n_tok // GROUP, tn, GROUP), F32))
    return pl.pallas_call(
        functools.partial(_norm_proj_kernel, t_tile=t_tile),
        grid=(n_tok // tm, n_main // tn),
        in_specs=[pl.BlockSpec((tm, D_MODEL), lambda i, j: (i, 0)),
                  pl.BlockSpec((1, D_MODEL), lambda i, j: (0, 0)),
                  _mod_spec(1, cond_of_tile), _mod_spec(0, cond_of_tile),
                  pl.BlockSpec((None, tn, D_MODEL), lambda i, j: (layer, j, 0)),
                  pl.BlockSpec((None, n_gate, D_MODEL), lambda i, j: (layer, n_main // n_gate, 0))],
        out_specs=out_specs, out_shape=out_shape,
        scratch_shapes=[pltpu.VMEM((tm, D_MODEL), BF16)],
        compiler_params=_params("parallel", "arbitrary"),
        name="norm_proj",
    )(x, norm_g.reshape(1, D_MODEL), mod, mod, w_t, w_t)


def _gated_out_kernel(x_ref, o_ref, z_ref, ng_ref, g1_ref, w_ref, y_ref, wb_scr, *, n_heads, gate_fn):
    @pl.when(pl.program_id(0) == 0)
    def _():
        wb_scr[...] = w_ref[...].astype(BF16)

    dv = o_ref.shape[1] // n_heads
    parts = []
    for h in range(n_heads):
        sl = slice(h * dv, (h + 1) * dv)
        parts.append((_rms(o_ref[:, sl], ng_ref[...]) * gate_fn(z_ref[:, sl].astype(F32))).astype(BF16))
    hs = jnp.concatenate(parts, axis=1)
    y = jnp.dot(hs, wb_scr[...], preferred_element_type=F32)
    y_ref[...] = x_ref[...] + g1_ref[0] * y


def _gated_out(x, o, main, z_block, norm_g, mod, cond_of_tile, w_out, n_heads, gate_fn):
    n_tok = x.shape[0]
    tm = OUT_TOKEN_TILE
    width = o.shape[1]
    return pl.pallas_call(
        functools.partial(_gated_out_kernel, n_heads=n_heads, gate_fn=gate_fn),
        grid=(n_tok // tm,),
        in_specs=[pl.BlockSpec((tm, D_MODEL), lambda i: (i, 0)),
                  pl.BlockSpec((tm, width), lambda i: (i, 0)),
                  pl.BlockSpec((tm, width), lambda i: (i, z_block)),
                  pl.BlockSpec((1, width // n_heads), lambda i: (0, 0)),
                  _mod_spec(2, cond_of_tile),
                  pl.BlockSpec((width, D_MODEL), lambda i: (0, 0))],
        out_specs=pl.BlockSpec((tm, D_MODEL), lambda i: (i, 0)),
        out_shape=jax.ShapeDtypeStruct((n_tok, D_MODEL), F32),
        scratch_shapes=[pltpu.VMEM((width, D_MODEL), BF16)],
        compiler_params=_params("arbitrary"),
        name="gated_out",
    )(x, o, main, norm_g.reshape(1, -1), mod, w_out)


def _dft_mats(n):
    k = np.arange(n, dtype=np.int64)
    ang = 2.0 * np.pi * ((k[:, None] * k[None, :]) % n).astype(np.float64) / n
    s = 1.0 / np.sqrt(n)
    return np.cos(ang) * s, np.sin(ang) * s


def _fnet_kernel(x_ref, ng_ref, sc_ref, sh_ref, g1_ref, ct_ref, st_ref, cs_ref, w_ref, b_ref, y_ref, wb_scr):
    @pl.when(pl.program_id(0) == 0)
    def _():
        wb_scr[...] = w_ref[...].astype(BF16)

    x = x_ref[...]
    h = _norm_mod(x, ng_ref[...], sc_ref[0], sh_ref[0]).astype(BF16)
    p = jnp.dot(ct_ref[...], h, preferred_element_type=F32)
    q = jnp.dot(st_ref[...], h, preferred_element_type=F32)
    gd = FNET_GROUP_DIM
    parts = []
    for g in range(D_MODEL // gd):
        sl = slice(g * gd, (g + 1) * gd)
        pq = jnp.concatenate([p[:, sl], q[:, sl]], axis=1).astype(BF16)
        parts.append(jnp.dot(pq, cs_ref[...], preferred_element_type=F32).astype(BF16))
    f = jnp.concatenate(parts, axis=1)
    y = jnp.dot(f, wb_scr[...], preferred_element_type=F32) + b_ref[...]
    y_ref[...] = x + g1_ref[0] * y


def _fnet_mix(x, seq, layer, norm_g, mod, cond_of_seq, w, b):
    n_tok = x.shape[0]
    ct, st = _dft_mats(seq)
    cc, sc = _dft_mats(FNET_GROUP_DIM)
    cs = np.concatenate([cc, -sc], axis=0)
    const = lambda i: (0, 0)
    ct, st, cs = (jnp.asarray(m, F32).astype(BF16) for m in (ct, st, cs))
    return pl.pallas_call(
        _fnet_kernel,
        grid=(n_tok // seq,),
        in_specs=[pl.BlockSpec((seq, D_MODEL), lambda i: (i, 0)),
                  pl.BlockSpec((1, D_MODEL), const),
                  _mod_spec(1, cond_of_seq), _mod_spec(0, cond_of_seq), _mod_spec(2, cond_of_seq),
                  pl.BlockSpec((seq, seq), const),
                  pl.BlockSpec((seq, seq), const),
                  pl.BlockSpec((2 * FNET_GROUP_DIM, FNET_GROUP_DIM), const),
                  pl.BlockSpec((None, D_MODEL, D_MODEL), lambda i: (layer, 0, 0)),
                  pl.BlockSpec((1, D_MODEL), const)],
        out_specs=pl.BlockSpec((seq, D_MODEL), lambda i: (i, 0)),
        out_shape=jax.ShapeDtypeStruct((n_tok, D_MODEL), F32),
        scratch_shapes=[pltpu.VMEM((D_MODEL, D_MODEL), BF16)],
        compiler_params=_params("arbitrary"),
        name="fnet",
    )(x, norm_g.reshape(1, D_MODEL), mod, mod, mod,
      ct, st, cs, w, b.reshape(1, D_MODEL))


def _shift_rows(x, delta):
    n = x.shape[0]
    return pltpu.roll(x, (-delta) % n, 0)


def _dwconv_tokens(g, cw_ref, seq, rows):
    tm = g.shape[0]
    width = seq // rows
    assert seq & (seq - 1) == 0 and width & (width - 1) == 0
    t = lax.broadcasted_iota(jnp.int32, (tm, 1), 0)
    pos = t & (seq - 1)
    col = pos & (width - 1)
    row = pos >> (width.bit_length() - 1)
    g_cols = (_shift_rows(g, -1) * (col >= 1).astype(F32), g,
              _shift_rows(g, 1) * (col <= width - 2).astype(F32))
    out = None
    for di in (-1, 0, 1):
        if rows == 1 and di != 0:
            continue
        r = sum(g_cols[dj] * cw_ref[pl.ds(3 * (di + 1) + dj, 1), :] for dj in range(3))
        if di != 0:
            ok = (row + di >= 0) & (row + di <= rows - 1)
            r = _shift_rows(r, di * width) * ok.astype(F32)
        out = r if out is None else out + r
    return out


def _ffn_kernel(x_ref, ng_ref, sc_ref, sh_ref, g2_ref, wa_ref, wg_ref, cw_ref, cb_ref, wd_ref, nf_ref,
                y_ref, h_scr, acc_scr, *, seq, rows, final_norm):
    j = pl.program_id(1)

    @pl.when(j == 0)
    def _():
        h_scr[...] = _norm_mod(x_ref[...], ng_ref[...], sc_ref[0], sh_ref[0]).astype(BF16)
        acc_scr[...] = jnp.zeros_like(acc_scr)

    tm = h_scr.shape[0]
    sub = tm // FFN_SUB_TILES
    parts = [pl.ds(r, sub) for r in range(0, tm, sub)]
    wa, wg, wd = (w[...].astype(BF16) for w in (wa_ref, wg_ref, wd_ref))
    gs = [jnp.dot(h_scr[p, :], wg, preferred_element_type=F32) for p in parts]
    as_ = [jnp.dot(h_scr[p, :], wa, preferred_element_type=F32) for p in parts]
    per_seq = max(seq // sub, 1)
    convs = []
    for i in range(0, len(parts), per_seq):
        g = gs[i] if per_seq == 1 else jnp.concatenate(gs[i:i + per_seq], axis=0)
        c = _dwconv_tokens(g, cw_ref, seq, rows) + cb_ref[...]
        convs += [c[k * sub:(k + 1) * sub] for k in range(per_seq)]
    acts = [(jax.nn.silu(c) * a).astype(BF16) for c, a in zip(convs, as_)]
    for p, act in zip(parts, acts):
        acc_scr[p, :] += jnp.dot(act, wd, preferred_element_type=F32)

    @pl.when(j == pl.num_programs(1) - 1)
    def _():
        y = x_ref[...] + g2_ref[0] * acc_scr[...]
        if final_norm:
            y = _rms(y, nf_ref[...])
        y_ref[...] = y


def _conv_ffn(x, seq, rows, layer, norm_g, mod, cond_of_tile, w_up, conv_w, conv_b, w_down, norm_final, final_norm):
    n_tok = x.shape[0]
    tm, tf = FFN_TOKEN_TILE, FFN_FF_TILE
    n_ff_tiles = D_FF // tf
    depth = w_up.shape[0]
    return pl.pallas_call(
        functools.partial(_ffn_kernel, seq=seq, rows=rows, final_norm=final_norm),
        grid=(n_tok // tm, n_ff_tiles),
        in_specs=[pl.BlockSpec((tm, D_MODEL), lambda i, j: (i, 0)),
                  pl.BlockSpec((1, D_MODEL), lambda i, j: (0, 0)),
                  _mod_spec(4, cond_of_tile), _mod_spec(3, cond_of_tile), _mod_spec(5, cond_of_tile),
                  pl.BlockSpec((None, D_MODEL, tf), lambda i, j: (layer, 0, j)),
                  pl.BlockSpec((None, D_MODEL, tf), lambda i, j: (layer, 0, n_ff_tiles + j)),
                  pl.BlockSpec((None, 9, tf), lambda i, j: (layer, 0, j)),
                  pl.BlockSpec((None, 1, tf), lambda i, j: (layer, 0, j)),
                  pl.BlockSpec((None, tf, D_MODEL), lambda i, j: (layer, j, 0)),
                  pl.BlockSpec((1, D_MODEL), lambda i, j: (0, 0))],
        out_specs=pl.BlockSpec((tm, D_MODEL), lambda i, j: (i, 0)),
        out_shape=jax.ShapeDtypeStruct((n_tok, D_MODEL), F32),
        scratch_shapes=[pltpu.VMEM((tm, D_MODEL), BF16), pltpu.VMEM((tm, D_MODEL), F32)],
        compiler_params=_params("parallel", "arbitrary"),
        name="conv_ffn",
    )(x, norm_g.reshape(1, D_MODEL), mod, mod, mod, w_up, w_up, conv_w.reshape(depth, 9, D_FF),
      conv_b.reshape(depth, 1, D_FF), w_down, norm_final.reshape(1, D_MODEL))


GroupMasks = collections.namedtuple("GroupMasks", "r c same incl strict")


def _chunk_masks():
    r = lax.broadcasted_iota(jnp.int32, (CHUNK, CHUNK), 0)
    c = lax.broadcasted_iota(jnp.int32, (CHUNK, CHUNK), 1)
    return (r >= c, r <= c)


def _group_masks(group=GROUP):
    r = lax.broadcasted_iota(jnp.int32, (group, group), 0)
    c = lax.broadcasted_iota(jnp.int32, (group, group), 1)
    same = (r >> LOG_CHUNK) == (c >> LOG_CHUNK)
    return GroupMasks(r, c, same, (same & (r >= c), same & (r <= c)), (same & (r > c), same & (r < c)))


def _gate_layouts(gt, batch, n_heads, hp, group=GROUP):
    seq = gt.shape[1] // batch
    n_groups, n_chunks = seq // group, seq // CHUNK
    assert seq % group == 0 and n_groups <= 8
    g = jnp.transpose(gt.reshape(N_DIR, 2, n_heads, batch, seq), (3, 2, 0, 1, 4))
    rows = g.reshape(g.shape[:4] + (n_groups, group))
    rows = jnp.pad(rows, ((0, 0),) * 4 + ((0, 8 - n_groups), (0, 0)))
    cols = jnp.swapaxes(g.reshape(batch, n_heads // hp, hp * N_DIR * 2 * n_chunks, CHUNK), -1, -2)
    return rows, cols


def _col_gate(gcol_ref, hh, d, kind, n_chunks):
    start = ((hh * N_DIR + d) * 2 + kind) * n_chunks
    return gcol_ref[0, 0, :, start:start + n_chunks]


def _split3(x):
    x1 = x.astype(BF16)
    r1 = x - x1.astype(F32)
    x2 = r1.astype(BF16)
    return x1, x2, (r1 - x2.astype(F32)).astype(BF16)


def _dot_mask_rhs(x, mask):
    m = x.shape[0]
    y = jnp.dot(jnp.concatenate(_split3(x), axis=0), jnp.where(mask, 1.0, 0.0).astype(BF16),
                preferred_element_type=F32)
    return y[:m] + y[m:2 * m] + y[2 * m:]


def _dot_mask_lhs(mask, x):
    mb = jnp.where(mask, 1.0, 0.0).astype(BF16)
    y1, y2, y3 = (jnp.dot(mb, p, preferred_element_type=F32) for p in _split3(x))
    return y1 + y2 + y3


def _cumsum_rows(row, d, gm):
    return _dot_mask_rhs(row, gm.incl[1 - d])


def _cumsum_cols(col, d, cm):
    return _dot_mask_lhs(cm[d], col)


def _store_cols(dst, col, n_chunks):
    for n in range(n_chunks):
        dst[n * CHUNK:(n + 1) * CHUNK, :] = jnp.broadcast_to(col[:, n:n + 1], (CHUNK, 128))


def _wide(x, width=None):
    reps = (GROUP if width is None else width) // x.shape[1]
    return x if reps == 1 else jnp.concatenate([x] * reps, axis=1)


def _rows(i, size):
    if isinstance(i, int):
        return pl.ds(i * size, size)
    return pl.ds(pl.multiple_of(i * size, size), size)


def _loop(n, body, init):
    if n == 1:
        return body(0, init)
    return lax.fori_loop(0, n, body, init)


def _conv_silu(x, w_ref):
    n = x.shape[0]
    k = w_ref.shape[0]
    t = lax.broadcasted_iota(jnp.int32, (n, 1), 0)
    acc = None
    for j in range(k):
        delta = j - k // 2
        if delta == 0:
            term = x
        else:
            ok = (t + delta >= 0) & (t + delta <= n - 1)
            term = _shift_rows(x, delta) * ok.astype(F32)
        term = term * w_ref[pl.ds(j, 1), :]
        acc = term if acc is None else acc + term
    return jax.nn.silu(acc)


def _l2norm(x):
    return x * lax.rsqrt(jnp.sum(x * x, axis=-1, keepdims=True) + EPS)


def _blockdiag_tri_inverse(mats, gm):
    dot16 = lambda x, y: jnp.dot(x, y, preferred_element_type=F32).astype(BF16)
    base = (gm.r >> 3) == (gm.c >> 3)
    eye = jnp.where(gm.r == gm.c, 1.0, 0.0)
    zero16 = jnp.zeros((), BF16)
    a16s = [a.astype(BF16) for a in mats]
    invs = [eye - jnp.where(base, a, 0.0) for a in mats]
    ps = [jnp.where(base, -a16, zero16) for a16 in a16s]
    for _ in range(2):
        ps = [dot16(p, p) for p in ps]
        invs = [inv + jnp.dot(inv.astype(BF16), p, preferred_element_type=F32) for inv, p in zip(invs, ps)]
    for s in (3, 4, 5):
        join = ((gm.r >> s) ^ (gm.c >> s)) == 1
        inv16s = [inv.astype(BF16) for inv in invs]
        tmp = [dot16(jnp.where(join, a16, zero16), inv16) for a16, inv16 in zip(a16s, inv16s)]
        invs = [inv - jnp.dot(inv16, t, preferred_element_type=F32) for inv, inv16, t in zip(invs, inv16s, tmp)]
    return invs


def _gdn_kernel(*refs, seq, hp, zero_init, emit_state):
    alog_ref, dtb_ref, q_ref, k_ref, v_ref, cwq_ref, cwk_ref, cwv_ref, grow_ref, gcol_ref = refs[:10]
    pos = 10
    s0_ref = None
    if not zero_init:
        s0_ref = refs[pos]
        pos += 1
    o_ref = refs[pos]
    pos += 1
    sfin_ref = None
    if emit_state:
        sfin_ref = refs[pos]
        pos += 1
    (q_scr, k_scr, v_scr, gcr_scr, gcb_scr, bb_scr, gl_scr,
     u_scr, w_scr, qd_scr, kd_scr, qkd_scr) = refs[pos:]

    n_chunks = seq // CHUNK
    grp = GDN_GROUP
    cm = _chunk_masks()
    gm = _group_masks(grp)
    head_dirs = [(hh, d) for hh in range(hp) for d in range(N_DIR)]
    lanes = lambda hh: slice(hh * DN_DK, (hh + 1) * DN_DK)

    q_scr[...] = _conv_silu(q_ref[...].astype(F32), cwq_ref)
    k_scr[...] = _conv_silu(k_ref[...].astype(F32), cwk_ref)
    v_scr[...] = _conv_silu(v_ref[...].astype(F32), cwv_ref)
    for hh in range(hp):
        q_scr[:, lanes(hh)] = _l2norm(q_scr[:, lanes(hh)]) * (DN_DK ** -0.5)
        k_scr[:, lanes(hh)] = _l2norm(k_scr[:, lanes(hh)])

    for hh, d in head_dirs:
        head = pl.program_id(1) * hp + hh
        neg_a = -jnp.exp(jnp.full((1, 1), alog_ref[d, head], F32))
        dtb = dtb_ref[d, head]
        gcr_scr[hh, d] = _cumsum_rows(neg_a * _softplus(grow_ref[0, hh, d, 0] + dtb), d, gm)
        gc_col = _cumsum_cols(neg_a * _softplus(_col_gate(gcol_ref, hh, d, 0, n_chunks) + dtb), d, cm)
        _store_cols(gcb_scr.at[hh, d], gc_col, n_chunks)
        _store_cols(bb_scr.at[hh, d], jax.nn.sigmoid(_col_gate(gcol_ref, hh, d, 1, n_chunks)), n_chunks)
        last = CHUNK - 1 if d == 0 else 0
        _store_cols(gl_scr.at[hh, d], jnp.broadcast_to(gc_col[last:last + 1, :], gc_col.shape), n_chunks)

    gps = min(GDN_GROUPS_PER_STEP, seq // grp)

    def group_step(it, carry):
        gis = [it * gps + k for k in range(gps)]
        rows = [_rows(gi, grp) for gi in gis]
        kqs = {}
        for k, r in enumerate(rows):
            for hh in range(hp):
                kg, qg = k_scr[r, lanes(hh)], q_scr[r, lanes(hh)]
                kqs[k, hh] = _dot_nt(jnp.concatenate([kg, qg], axis=0), kg)
        chains = [(k, hh, d) for k in range(len(gis)) for hh, d in head_dirs]
        mats = []
        for k, hh, d in chains:
            gcb = gcb_scr[hh, d, rows[k], :]
            gcr = gcr_scr[hh, d, pl.ds(gis[k], 1), :]
            decay = jnp.exp(jnp.where(gm.incl[d], _wide(gcb, grp) - gcr, -jnp.inf))
            mats.append(jnp.where(gm.strict[d], kqs[k, hh][:grp] * _wide(bb_scr[hh, d, rows[k], :], grp) * decay, 0.0))
            qkd = kqs[k, hh][grp:] * decay
            qkd_scr[hh, d, rows[k], :] = sum(qkd[:, n * CHUNK:(n + 1) * CHUNK] for n in range(grp // CHUNK))
        t_invs = _blockdiag_tri_inverse(mats, gm)
        for (k, hh, d), t_inv in zip(chains, t_invs):
            r = rows[k]
            qg, kg, vg = q_scr[r, lanes(hh)], k_scr[r, lanes(hh)], v_scr[r, lanes(hh)]
            gcb = gcb_scr[hh, d, r, :]
            bb = bb_scr[hh, d, r, :]
            e_gc = jnp.exp(gcb)
            uw = _dot(t_inv, jnp.concatenate([vg * bb, kg * bb * e_gc], axis=1))
            u_scr[hh, d, r, :] = uw[:, :DN_DV]
            w_scr[hh, d, r, :] = uw[:, DN_DV:]
            qd_scr[hh, d, r, :] = qg * e_gc
            kd_scr[hh, d, r, :] = kg * jnp.exp(gl_scr[hh, d, r, :] - gcb)
        return carry

    _loop(seq // (grp * gps), group_step, 0)

    o_ref[...] = jnp.zeros_like(o_ref)

    def chunk_step(n, carry):
        idxs = [n if d == 0 else n_chunks - 1 - n for _, d in head_dirs]
        rows = [_rows(idx, CHUNK) for idx in idxs]
        wqs = [_dot(jnp.concatenate([w_scr[hh, d, r, :], qd_scr[hh, d, r, :]], axis=0), s)
               for (hh, d), r, s in zip(head_dirs, rows, carry)]
        v_news = [u_scr[hh, d, r, :] - wq[:CHUNK] for (hh, d), r, wq in zip(head_dirs, rows, wqs)]
        o_ns = [wq[CHUNK:] + _dot(qkd_scr[hh, d, r, :], v_new)
                for (hh, d), r, wq, v_new in zip(head_dirs, rows, wqs, v_news)]
        new = []
        for (hh, d), idx, r, s, v_new in zip(head_dirs, idxs, rows, carry, v_news):
            s_decay = jnp.exp(gl_scr[hh, d, pl.ds(idx * CHUNK, 1), :])
            new.append(s * s_decay + _dot_tn(kd_scr[hh, d, r, :], v_new))
        for (hh, d), r, o_n in zip(head_dirs, rows, o_ns):
            o_ref[r, lanes(hh)] += o_n
        return tuple(new)

    if zero_init:
        init = (jnp.zeros((DN_DK, DN_DV), F32),) * len(head_dirs)
    else:
        init = tuple(s0_ref[0, d, hh] for hh, d in head_dirs)
    fin = lax.fori_loop(0, n_chunks, chunk_step, init)
    if emit_state:
        for i, (hh, d) in enumerate(head_dirs):
            sfin_ref[0, d, hh] = fin[i]


def _gdn_scan(main, gt, batch, seq, conv_w, a_log, dt_bias, s0, emit_state):
    n_tok = batch * seq
    nh = DN_HEADS
    hp = min(nh, max(2, SCAN_STEP_POSITIONS // seq))
    n_hb = nh // hp
    assert seq % (GDN_GROUP * min(GDN_GROUPS_PER_STEP, seq // GDN_GROUP)) == 0
    grow, gcol = _gate_layouts(gt, batch, nh, hp, GDN_GROUP)
    zero_init = s0 is None
    smem = pl.BlockSpec(memory_space=pltpu.SMEM)
    qkv_spec = lambda off: pl.BlockSpec((seq, hp * DN_DK), lambda b, h: (b, off + h))
    cw_spec = lambda off: pl.BlockSpec((DN_CONV, hp * DN_DK), lambda b, h: (0, off + h))
    state_spec = pl.BlockSpec((1, N_DIR, hp, DN_DK, DN_DV), lambda b, h: (b, 0, h, 0, 0))
    in_specs = [smem, smem, qkv_spec(0), qkv_spec(n_hb), qkv_spec(2 * n_hb),
                cw_spec(0), cw_spec(n_hb), cw_spec(2 * n_hb),
                pl.BlockSpec((1, hp) + grow.shape[2:], lambda b, h: (b, h, 0, 0, 0, 0)),
                pl.BlockSpec((1, 1) + gcol.shape[2:], lambda b, h: (b, h, 0, 0))]
    args = [a_log, dt_bias, main, main, main, conv_w, conv_w, conv_w, grow, gcol]
    if not zero_init:
        in_specs.append(state_spec)
        args.append(s0)
    out_specs = [pl.BlockSpec((seq, hp * DN_DV), lambda b, h: (b, h))]
    out_shape = [jax.ShapeDtypeStruct((n_tok, nh * DN_DV), F32)]
    if emit_state:
        out_specs.append(state_spec)
        out_shape.append(jax.ShapeDtypeStruct((batch, N_DIR, nh, DN_DK, DN_DV), F32))
    per_dir = lambda width: pltpu.VMEM((hp, N_DIR, seq, width), F32)
    qkv_scr = pltpu.VMEM((seq, hp * DN_DK), F32)
    outs = pl.pallas_call(
        functools.partial(_gdn_kernel, seq=seq, hp=hp, zero_init=zero_init, emit_state=emit_state),
        grid=(batch, n_hb),
        in_specs=in_specs, out_specs=out_specs, out_shape=out_shape,
        scratch_shapes=[qkv_scr, qkv_scr, qkv_scr,
                        pltpu.VMEM((hp, N_DIR, 8, GDN_GROUP), F32),
                        per_dir(128), per_dir(128), per_dir(128),
                        per_dir(DN_DV), per_dir(DN_DK), per_dir(DN_DK), per_dir(DN_DK), per_dir(CHUNK)],
        compiler_params=_params("parallel", "parallel"),
        name="gdn_scan",
    )(*args)
    return outs[0], (outs[1] if emit_state else None)


ML_AUG = 2 * ML_DV


def _mlstm_kernel(*refs, seq, hp, zero_init, emit_state):
    bi_ref, bf_ref, q_ref, k_ref, v_ref, kt_ref, grow_ref, gcol_ref = refs[:8]
    pos = 8
    c0_ref = m0_ref = None
    if not zero_init:
        c0_ref, m0_ref = refs[pos:pos + 2]
        pos += 2
    o_ref = refs[pos]
    pos += 1
    cfin_ref = nfin_ref = mfin_ref = None
    if emit_state:
        cfin_ref, nfin_ref, mfin_ref = refs[pos:pos + 3]
        pos += 3
    bcr_scr, lir_scr, kwr_scr, bcb_scr, mi_scr, dec_scr, dc_scr, cin_scr = refs[pos:]

    n_chunks = seq // CHUNK
    n_groups = seq // GROUP
    head0 = pl.program_id(1) * hp
    cm = _chunk_masks()
    gm = _group_masks()
    ones_col = jnp.where(lax.broadcasted_iota(jnp.int32, (GROUP, ML_DV), 1) == 0, 1.0, 0.0)
    head_dirs = [(hh, d) for hh in range(hp) for d in range(N_DIR)]

    m_fin = {}
    row_group = lax.broadcasted_iota(jnp.int32, (8, GROUP), 0)
    row_chunk = lax.broadcasted_iota(jnp.int32, (8, GROUP), 1) >> LOG_CHUNK
    for hh, d in head_dirs:
        b_i = bi_ref[d, head0 + hh]
        b_f = bf_ref[d, head0 + hh]
        li_row = grow_ref[0, hh, d, 0] + b_i
        lf_row = -_softplus(-(grow_ref[0, hh, d, 1] + b_f))
        bc_row = _cumsum_rows(lf_row, d, gm)
        lir_scr[hh, d] = li_row
        bcr_scr[hh, d] = bc_row
        w_row = _dot_mask_rhs(lf_row, gm.same) - bc_row + li_row
        m_out_row = jnp.zeros((8, GROUP), F32)
        li_col = _col_gate(gcol_ref, hh, d, 0, n_chunks) + b_i
        bc_col = _cumsum_cols(-_softplus(-(_col_gate(gcol_ref, hh, d, 1, n_chunks) + b_f)), d, cm)
        last = CHUNK - 1 if d == 0 else 0
        b_last = bc_col[last:last + 1, :]
        w_col = b_last - bc_col + li_col
        w_max = jnp.max(w_col, axis=0, keepdims=True)
        m = jnp.zeros((1, 1), F32) if zero_init else m0_ref[0, d, hh][:, 0:1]
        for step in range(n_chunks):
            n = step if d == 0 else n_chunks - 1 - step
            sl = slice(n * CHUNK, (n + 1) * CHUNK)
            m_new = jnp.maximum(b_last[:, n:n + 1] + m, w_max[:, n:n + 1])
            bcb_scr[hh, d, sl, :] = jnp.broadcast_to(bc_col[:, n:n + 1], (CHUNK, 128))
            mi_scr[hh, d, sl, :] = jnp.broadcast_to(m, (CHUNK, 128))
            dec_scr[hh, d, n:n + 1, :] = jnp.broadcast_to(jnp.exp(b_last[:, n:n + 1] + m - m_new), (1, 128))
            in_chunk = (row_group == n // CHUNKS_PER_GROUP) & (row_chunk == n % CHUNKS_PER_GROUP)
            m_out_row = jnp.where(in_chunk, m_new, m_out_row)
            m = m_new
        m_fin[hh, d] = m
        kwr_scr[hh, d] = jnp.exp(w_row - m_out_row)

    def load_kv(rows, hh):
        kg = k_ref[rows, hh * ML_DQK:(hh + 1) * ML_DQK].astype(F32)
        v_aug = jnp.concatenate([v_ref[rows, hh * ML_DV:(hh + 1) * ML_DV].astype(F32), ones_col], axis=1)
        return kg, v_aug

    def delta_step(gi, carry):
        rows = _rows(gi, GROUP)
        v_augs = [load_kv(rows, hh)[1] for hh in range(hp)]
        kt = kt_ref[gi]
        lhs = []
        for hh, d in head_dirs:
            kw_t = kt[hh * ML_DQK:(hh + 1) * ML_DQK, :] * kwr_scr[hh, d, pl.ds(gi, 1), :]
            lhs.append(jnp.where(gm.same, jnp.concatenate([kw_t] * CHUNKS_PER_GROUP, axis=0), 0.0))
        for (hh, d), kw in zip(head_dirs, lhs):
            dc_scr[hh, d, rows, :] = _dot(kw, v_augs[hh])
        return carry

    _loop(n_groups, delta_step, 0)

    def prefix_step(n, carry):
        new = []
        for (hh, d), c_aug in zip(head_dirs, carry):
            idx = n if d == 0 else n_chunks - 1 - n
            rows = _rows(idx, CHUNK)
            cin_scr[hh, d, rows, :] = c_aug
            new.append(c_aug * _wide(dec_scr[hh, d, pl.ds(idx, 1), :], ML_AUG) + dc_scr[hh, d, rows, :])
        return tuple(new)

    if zero_init:
        init = (jnp.zeros((ML_DQK, ML_AUG), F32),) * len(head_dirs)
    else:
        init = tuple(c0_ref[0, d, hh] for hh, d in head_dirs)
    fin = lax.fori_loop(0, n_chunks, prefix_step, init)

    def out_step(gi, carry):
        rows = _rows(gi, GROUP)
        kvs = [load_kv(rows, hh) for hh in range(hp)]
        qgs = [q_ref[rows, hh * ML_DQK:(hh + 1) * ML_DQK].astype(F32) * (ML_DQK ** -0.5) for hh in range(hp)]
        qks = [_dot_nt(qg, kv[0]) for qg, kv in zip(qgs, kvs)]
        chunk = lambda n: slice(n * CHUNK, (n + 1) * CHUNK)
        inters = []
        for hh, d in head_dirs:
            c_in = cin_scr[hh, d, rows, :]
            inters.append(jnp.concatenate(
                [_dot(qgs[hh][chunk(n)], c_in[chunk(n)]) for n in range(CHUNKS_PER_GROUP)], axis=0))
        m_ts, b_ms, ss = [], [], []
        for hh, d in head_dirs:
            b_colb = bcb_scr[hh, d, rows, :]
            b_m = b_colb[:, 0:1] + mi_scr[hh, d, rows, :][:, 0:1]
            d_log = jnp.where(gm.incl[d], _wide(b_colb) - bcr_scr[hh, d, pl.ds(gi, 1), :]
                              + lir_scr[hh, d, pl.ds(gi, 1), :], -jnp.inf)
            m_t = jnp.maximum(b_m, jnp.max(d_log, axis=-1, keepdims=True))
            m_ts.append(m_t)
            b_ms.append(b_m)
            ss.append(qks[hh] * jnp.exp(d_log - m_t))
        intras = [_dot(s, kvs[hh][1]) for (hh, d), s in zip(head_dirs, ss)]
        hs = []
        for inter, intra, m_t, b_m in zip(inters, intras, m_ts, b_ms):
            num = jnp.exp(b_m - m_t) * inter + intra
            den = num[:, ML_DV:ML_DV + 1]
            hs.append(num[:, :ML_DV] / jnp.maximum(jnp.abs(den), jnp.exp(-m_t)))
        for hh in range(hp):
            o_ref[rows, hh * ML_DV:(hh + 1) * ML_DV] = hs[N_DIR * hh] + hs[N_DIR * hh + 1]
        return carry

    _loop(n_groups, out_step, 0)

    if emit_state:
        for i, (hh, d) in enumerate(head_dirs):
            cfin_ref[0, d, hh] = fin[i][:, :ML_DV]
            nfin_ref[0, d, hh] = fin[i][:, ML_DV:]
            mfin_ref[0, d, hh] = jnp.broadcast_to(m_fin[hh, d], (1, 128))


def _mlstm_scan(main, kt, gt, batch, seq, b_i, b_f, state0, emit_state):
    n_tok = batch * seq
    n_groups = seq // GROUP
    n_chunks = seq // CHUNK
    nh = ML_HEADS
    hp = min(nh, max(2, SCAN_STEP_POSITIONS // seq))
    n_hb = nh // hp
    grow, gcol = _gate_layouts(gt, batch, nh, hp)
    zero_init = state0 is None
    smem = pl.BlockSpec(memory_space=pltpu.SMEM)
    k_off = (nh * ML_DQK) // (hp * ML_DQK)
    v_off = (2 * nh * ML_DQK) // (hp * ML_DV)
    c_spec = pl.BlockSpec((1, N_DIR, hp, ML_DQK, ML_AUG), lambda b, p: (b, 0, p, 0, 0))
    m_spec = pl.BlockSpec((1, N_DIR, hp, 1, 128), lambda b, p: (b, 0, p, 0, 0))
    in_specs = [smem, smem,
                pl.BlockSpec((seq, hp * ML_DQK), lambda b, p: (b, p)),
                pl.BlockSpec((seq, hp * ML_DQK), lambda b, p: (b, k_off + p)),
                pl.BlockSpec((seq, hp * ML_DV), lambda b, p: (b, v_off + p)),
                pl.BlockSpec((n_groups, hp * ML_DQK, GROUP), lambda b, p: (b, p, 0)),
                pl.BlockSpec((1, hp) + grow.shape[2:], lambda b, p: (b, p, 0, 0, 0, 0)),
                pl.BlockSpec((1, 1) + gcol.shape[2:], lambda b, p: (b, p, 0, 0))]
    args = [b_i, b_f, main, main, main, kt, grow, gcol]
    if not zero_init:
        in_specs += [c_spec, m_spec]
        args += list(state0)
    out_specs = [pl.BlockSpec((seq, hp * ML_DV), lambda b, p: (b, p))]
    out_shape = [jax.ShapeDtypeStruct((n_tok, nh * ML_DV), F32)]
    if emit_state:
        half_spec = pl.BlockSpec((1, N_DIR, hp, ML_DQK, ML_DV), lambda b, p: (b, 0, p, 0, 0))
        out_specs += [half_spec, half_spec, m_spec]
        out_shape += [jax.ShapeDtypeStruct((batch, N_DIR, nh, ML_DQK, ML_DV), F32),
                      jax.ShapeDtypeStruct((batch, N_DIR, nh, ML_DQK, ML_AUG - ML_DV), F32),
                      jax.ShapeDtypeStruct((batch, N_DIR, nh, 1, 128), F32)]
    per_hd = lambda rows, width: pltpu.VMEM((hp, N_DIR, rows, width), F32)
    outs = pl.pallas_call(
        functools.partial(_mlstm_kernel, seq=seq, hp=hp, zero_init=zero_init, emit_state=emit_state),
        grid=(batch, n_hb),
        in_specs=in_specs, out_specs=out_specs, out_shape=out_shape,
        scratch_shapes=[per_hd(8, GROUP), per_hd(8, GROUP), per_hd(8, GROUP),
                        per_hd(seq, 128), per_hd(seq, 128),
                        per_hd(max(n_chunks, 8), 128),
                        per_hd(seq, ML_AUG), per_hd(seq, ML_AUG)],
        compiler_params=_params("parallel", "parallel"),
        name="mlstm_scan",
    )(*args)
    return outs[0], (tuple(outs[1:4]) if emit_state else None)


def _trunk(x, batch, seq, rows, mods, tokens_per_cond, st_d, st_ml, emit_state, p):
    depth = p["w_ada"].shape[0]
    tile_cond = lambda tile: (lambda i: (i * tile) // tokens_per_cond)
    new_d = new_ml = None
    for layer in range(depth):
        mod = mods[layer]
        kind, j = layer % 3, layer // 3
        if kind == 0:
            x = _fnet_mix(x, seq, j, p["norm_mix"][layer], mod, tile_cond(seq), p["fnet_w"], p["fnet_b"][j])
        elif kind == 1:
            n_main = DN_HEADS * (2 * DN_DK + 2 * DN_DV)
            main, gt = _norm_proj(x, p["norm_mix"][layer], mod, tokens_per_cond, p["dn_w_in"], j, n_main)
            s0 = None if st_d is None else st_d[:, j]
            o, sfin = _gdn_scan(main, gt, batch, seq, p["dn_conv_w"][j], p["dn_a_log"][j], p["dn_dt_bias"][j],
                                s0, emit_state)
            if emit_state:
                new_d = sfin
            x = _gated_out(x, o, main, 3, p["dn_norm"][j], mod, tile_cond(OUT_TOKEN_TILE), p["dn_w_out"][j],
                           DN_HEADS, jax.nn.silu)
        else:
            n_main = 2 * ML_HEADS * ML_DQK + 2 * ML_HEADS * ML_DV
            k_tile = (ML_HEADS * ML_DQK) // PROJ_COL_TILE
            assert ML_HEADS * ML_DQK == PROJ_COL_TILE
            main, gt, kt = _norm_proj(x, p["norm_mix"][layer], mod, tokens_per_cond, p["ml_w_in"], j, n_main,
                                      t_tile=k_tile)
            state0 = None
            if st_ml is not None:
                c0, n0, m0 = (s[:, j] for s in st_ml)
                pad = jnp.zeros(c0.shape[:-1] + (ML_AUG - ML_DV - 1,), F32)
                c_aug0 = jnp.concatenate([c0, n0[..., None], pad], axis=-1)
                m0b = jnp.broadcast_to(m0[..., None, None], m0.shape + (1, 128))
                state0 = (c_aug0, m0b)
            o, fin = _mlstm_scan(main, kt, gt, batch, seq, p["ml_b_i"][j], p["ml_b_f"][j], state0, emit_state)
            if emit_state:
                new_ml = (fin[0], fin[1][..., 0], fin[2][..., 0, 0])
            x = _gated_out(x, o, main, 2, p["ml_norm"][j], mod, tile_cond(OUT_TOKEN_TILE), p["ml_w_out"][j],
                           ML_HEADS, jax.nn.sigmoid)
        x = _conv_ffn(x, seq, rows, layer, p["norm_ffn"][layer], mod, tile_cond(FFN_TOKEN_TILE), p["ffn_w_up"],
                      p["ffn_conv_w"], p["ffn_conv_b"], p["ffn_w_down"], p["norm_final"],
                      final_norm=(layer == depth - 1))
    return x, new_d, new_ml


def kernel(x_prompt, x_sample, state_delta, state_mlstm_c, state_mlstm_n, state_mlstm_m, c, c_ctx, w_ada, b_ada, norm_mix, norm_ffn, norm_final, ffn_w_up, ffn_conv_w, ffn_conv_b, ffn_w_down, fnet_w, fnet_b, dn_w_in, dn_conv_w, dn_a_log, dn_dt_bias, dn_norm, dn_w_out, ml_w_in, ml_b_i, ml_b_f, ml_norm, ml_w_out):
    p = dict(w_ada=w_ada, norm_mix=norm_mix, norm_ffn=norm_ffn, norm_final=norm_final, ffn_w_up=ffn_w_up,
             ffn_conv_w=ffn_conv_w, ffn_conv_b=ffn_conv_b, ffn_w_down=ffn_w_down, fnet_w=fnet_w, fnet_b=fnet_b,
             dn_w_in=dn_w_in, dn_conv_w=dn_conv_w, dn_a_log=dn_a_log, dn_dt_bias=dn_dt_bias, dn_norm=dn_norm,
             dn_w_out=dn_w_out, ml_w_in=ml_w_in, ml_b_i=ml_b_i, ml_b_f=ml_b_f, ml_norm=ml_norm, ml_w_out=ml_w_out)
    b_ctx, t_ctx, _ = x_prompt.shape
    b_smp, t_smp, _ = x_sample.shape
    depth = w_ada.shape[0]

    cond8 = jnp.concatenate([c_ctx[None, :], c, jnp.zeros((8 - 1 - b_smp, D_MODEL), F32)], axis=0)
    mods = _adaln_all(cond8, w_ada, b_ada)
    mods_ctx = [mods[l, 0:1].reshape(1, 1, -1) for l in range(depth)]
    mods_smp = [mods[l, 1:1 + b_smp].reshape(b_smp, 1, -1) for l in range(depth)]

    y_ctx, new_d, new_ml = _trunk(x_prompt.reshape(b_ctx * t_ctx, D_MODEL), b_ctx, t_ctx, 1, mods_ctx,
                                  b_ctx * t_ctx, None, None, True, p)
    y_smp, _, _ = _trunk(x_sample.reshape(b_smp * t_smp, D_MODEL), b_smp, t_smp, t_smp // GRID_W, mods_smp,
                         t_smp, state_delta, (state_mlstm_c, state_mlstm_n, state_mlstm_m), False, p)

    new_c, new_n, new_m = new_ml
    return (y_ctx.reshape(b_ctx, t_ctx, D_MODEL), y_smp.reshape(b_smp, t_smp, D_MODEL),
            new_d[:, None], new_c[:, None], new_n[:, None], new_m[:, None])
```

```python
import collections
import functools

import numpy as np
import jax
import jax.numpy as jnp
from jax import lax
from jax.experimental import pallas as pl
from jax.experimental.pallas import tpu as pltpu

F32 = jnp.float32
BF16 = jnp.bfloat16

D_MODEL = 1024
EPS = 1e-6
N_DIR = 2
CHUNK = 64
LOG_CHUNK = 6
GROUP = 256
CHUNKS_PER_GROUP = GROUP // CHUNK
GDN_GROUP = 128
GDN_GROUPS_PER_STEP = 4
GDN_HEADS_PER_PASS = 4
GDN_STEP_POSITIONS = 2048
FNET_GROUP_DIM = 256
DN_HEADS, DN_DK, DN_DV, DN_CONV = 8, 128, 128, 5
ML_HEADS, ML_DQK, ML_DV = 8, 64, 128
D_FF = 2816
GRID_W = 64

FFN_TOKEN_TILE = 1024
FFN_FF_TILE = 256
FFN_SUB_TILES = 4
PROJ_TOKEN_TILE = 2048
PROJ_COL_TILE = 512
OUT_TOKEN_TILE = 512
ADA_COL_TILE = 1536
SCAN_STEP_POSITIONS = 1024
VMEM_LIMIT = 56 * 1024 * 1024


def _params(*sem):
    return pltpu.CompilerParams(dimension_semantics=sem, vmem_limit_bytes=VMEM_LIMIT)


def _dot(a, b):
    return jnp.dot(a.astype(BF16), b.astype(BF16), preferred_element_type=F32)


def _dot_nt(a, b):
    return lax.dot_general(a.astype(BF16), b.astype(BF16), (((1,), (1,)), ((), ())),
                           preferred_element_type=F32)


def _dot_tn(a, b):
    return lax.dot_general(a.astype(BF16), b.astype(BF16), (((0,), (0,)), ((), ())),
                           preferred_element_type=F32)


def _rms(x, g):
    return x * lax.rsqrt(jnp.mean(x * x, axis=-1, keepdims=True) + EPS) * g


def _norm_mod(x, g, sc, sh):
    return _rms(x, g) * (1.0 + sc) + sh


def _softplus(x):
    return jnp.maximum(x, 0.0) + jnp.log1p(jnp.exp(-jnp.abs(x)))


def _mod_spec(chunk, cond_of_tile):
    return pl.BlockSpec((1, 1, D_MODEL), lambda i, *_: (cond_of_tile(i), 0, chunk))


def _adaln_kernel(c_ref, w_ref, b_ref, o_ref):
    s = jax.nn.silu(c_ref[...])
    o_ref[0] = _dot(s, w_ref[0]) + b_ref[0]


def _adaln_all(cond8, w_ada, b_ada):
    depth = w_ada.shape[0]
    n_out = w_ada.shape[2]
    tn = ADA_COL_TILE
    return pl.pallas_call(
        _adaln_kernel,
        grid=(depth, n_out // tn),
        in_specs=[pl.BlockSpec((8, D_MODEL), lambda l, j: (0, 0)),
                  pl.BlockSpec((1, D_MODEL, tn), lambda l, j: (l, 0, j)),
                  pl.BlockSpec((1, 1, tn), lambda l, j: (l, 0, j))],
        out_specs=pl.BlockSpec((1, 8, tn), lambda l, j: (l, 0, j)),
        out_shape=jax.ShapeDtypeStruct((depth, 8, n_out), F32),
        compiler_params=_params("parallel", "parallel"),
        name="adaln",
    )(cond8, w_ada, b_ada.reshape(depth, 1, n_out))


def _norm_proj_kernel(x_ref, g_ref, sc_ref, sh_ref, w_ref, wg_ref, o_ref, gate_ref, *rest, t_tile):
    h_scr = rest[-1]

    @pl.when(pl.program_id(1) == 0)
    def _():
        h = _norm_mod(x_ref[...], g_ref[...], sc_ref[0], sh_ref[0]).astype(BF16)
        h_scr[...] = h
        gate_ref[...] = _dot_nt(wg_ref[...], h)

    y = _dot_nt(h_scr[...], w_ref[...])
    o_ref[...] = y.astype(o_ref.dtype)

    if t_tile is not None:
        t_ref = rest[0]

        @pl.when(pl.program_id(1) == t_tile)
        def _():
            for g in range(t_ref.shape[0]):
                t_ref[g] = y[g * GROUP:(g + 1) * GROUP, :].T


def _norm_proj(x, norm_g, mod, tokens_per_cond, w_stack, layer, n_main, t_tile=None):
    n_tok = x.shape[0]
    tm, tn = min(PROJ_TOKEN_TILE, tokens_per_cond), PROJ_COL_TILE
    cond_of_tile = lambda i: (i * tm) // tokens_per_cond
    w_t = jnp.swapaxes(w_stack, 1, 2)
    n_gate = w_t.shape[1] - n_main
    assert n_main % n_gate == 0 and n_gate % 8 == 0
    out_specs = [pl.BlockSpec((tm, tn), lambda i, j: (i, j)),
                 pl.BlockSpec((n_gate, tm), lambda i, j: (0, i))]
    out_shape = [jax.ShapeDtypeStruct((n_tok, n_main), BF16),
                 jax.ShapeDtypeStruct((n_gate, n_tok), F32)]
    if t_tile is not None:
        out_specs.append(pl.BlockSpec((tm // GROUP, tn, GROUP), lambda i, j: (i, 0, 0)))
        out_shape.append(jax.ShapeDtypeStruct((n_tok // GROUP, tn, GROUP), F32))
    return pl.pallas_call(
        functools.partial(_norm_proj_kernel, t_tile=t_tile),
        grid=(n_tok // tm, n_main // tn),
        in_specs=[pl.BlockSpec((tm, D_MODEL), lambda i, j: (i, 0)),
                  pl.BlockSpec((1, D_MODEL), lambda i, j: (0, 0)),
                  _mod_spec(1, cond_of_tile), _mod_spec(0, cond_of_tile),
                  pl.BlockSpec((None, tn, D_MODEL), lambda i, j: (layer, j, 0)),
                  pl.BlockSpec((None, n_gate, D_MODEL), lambda i, j: (layer, n_main // n_gate, 0))],
        out_specs=out_specs, out_shape=out_shape,
        scratch_shapes=[pltpu.VMEM((tm, D_MODEL), BF16)],
        compiler_params=_params("parallel", "arbitrary"),
        name="norm_proj",
    )(x, norm_g.reshape(1, D_MODEL), mod, mod, w_t, w_t)


def _gated_out_kernel(x_ref, o_ref, z_ref, ng_ref, g1_ref, w_ref, y_ref, wb_scr, *, n_heads, gate_fn):
    @pl.when(pl.program_id(0) == 0)
    def _():
        wb_scr[...] = w_ref[...].astype(BF16)

    dv = o_ref.shape[1] // n_heads
    parts = []
    for h in range(n_heads):
        sl = slice(h * dv, (h + 1) * dv)
        parts.append((_rms(o_ref[:, sl], ng_ref[...]) * gate_fn(z_ref[:, sl].astype(F32))).astype(BF16))
    hs = jnp.concatenate(parts, axis=1)
    y = jnp.dot(hs, wb_scr[...], preferred_element_type=F32)
    y_ref[...] = x_ref[...] + g1_ref[0] * y


def _gated_out(x, o, main, z_block, norm_g, mod, cond_of_tile, w_out, n_heads, gate_fn):
    n_tok = x.shape[0]
    tm = OUT_TOKEN_TILE
    width = o.shape[1]
    return pl.pallas_call(
        functools.partial(_gated_out_kernel, n_heads=n_heads, gate_fn=gate_fn),
        grid=(n_tok // tm,),
        in_specs=[pl.BlockSpec((tm, D_MODEL), lambda i: (i, 0)),
                  pl.BlockSpec((tm, width), lambda i: (i, 0)),
                  pl.BlockSpec((tm, width), lambda i: (i, z_block)),
                  pl.BlockSpec((1, width // n_heads), lambda i: (0, 0)),
                  _mod_spec(2, cond_of_tile),
                  pl.BlockSpec((width, D_MODEL), lambda i: (0, 0))],
        out_specs=pl.BlockSpec((tm, D_MODEL), lambda i: (i, 0)),
        out_shape=jax.ShapeDtypeStruct((n_tok, D_MODEL), F32),
        scratch_shapes=[pltpu.VMEM((width, D_MODEL), BF16)],
        compiler_params=_params("arbitrary"),
        name="gated_out",
    )(x, o, main, norm_g.reshape(1, -1), mod, w_out)


def _dft_mats(n):
    k = np.arange(n, dtype=np.int64)
    ang = 2.0 * np.pi * ((k[:, None] * k[None, :]) % n).astype(np.float64) / n
    s = 1.0 / np.sqrt(n)
    return np.cos(ang) * s, np.sin(ang) * s


def _fnet_kernel(x_ref, ng_ref, sc_ref, sh_ref, g1_ref, ct_ref, st_ref, cs_ref, w_ref, b_ref, y_ref, wb_scr):
    @pl.when(pl.program_id(0) == 0)
    def _():
        wb_scr[...] = w_ref[...].astype(BF16)

    x = x_ref[...]
    h = _norm_mod(x, ng_ref[...], sc_ref[0], sh_ref[0]).astype(BF16)
    p = jnp.dot(ct_ref[...], h, preferred_element_type=F32)
    q = jnp.dot(st_ref[...], h, preferred_element_type=F32)
    gd = FNET_GROUP_DIM
    parts = []
    for g in range(D_MODEL // gd):
        sl = slice(g * gd, (g + 1) * gd)
        pq = jnp.concatenate([p[:, sl], q[:, sl]], axis=1).astype(BF16)
        parts.append(jnp.dot(pq, cs_ref[...], preferred_element_type=F32).astype(BF16))
    f = jnp.concatenate(parts, axis=1)
    y = jnp.dot(f, wb_scr[...], preferred_element_type=F32) + b_ref[...]
    y_ref[...] = x + g1_ref[0] * y


def _fnet_mix(x, seq, layer, norm_g, mod, cond_of_seq, w, b):
    n_tok = x.shape[0]
    ct, st = _dft_mats(seq)
    cc, sc = _dft_mats(FNET_GROUP_DIM)
    cs = np.concatenate([cc, -sc], axis=0)
    const = lambda i: (0, 0)
    ct, st, cs = (jnp.asarray(m, F32).astype(BF16) for m in (ct, st, cs))
    return pl.pallas_call(
        _fnet_kernel,
        grid=(n_tok // seq,),
        in_specs=[pl.BlockSpec((seq, D_MODEL), lambda i: (i, 0)),
                  pl.BlockSpec((1, D_MODEL), const),
                  _mod_spec(1, cond_of_seq), _mod_spec(0, cond_of_seq), _mod_spec(2, cond_of_seq),
                  pl.BlockSpec((seq, seq), const),
                  pl.BlockSpec((seq, seq), const),
                  pl.BlockSpec((2 * FNET_GROUP_DIM, FNET_GROUP_DIM), const),
                  pl.BlockSpec((None, D_MODEL, D_MODEL), lambda i: (layer, 0, 0)),
                  pl.BlockSpec((1, D_MODEL), const)],
        out_specs=pl.BlockSpec((seq, D_MODEL), lambda i: (i, 0)),
        out_shape=jax.ShapeDtypeStruct((n_tok, D_MODEL), F32),
        scratch_shapes=[pltpu.VMEM((D_MODEL, D_MODEL), BF16)],
        compiler_params=_params("arbitrary"),
        name="fnet",
    )(x, norm_g.reshape(1, D_MODEL), mod, mod, mod,
      ct, st, cs, w, b.reshape(1, D_MODEL))


def _shift_rows(x, delta):
    n = x.shape[0]
    return pltpu.roll(x, (-delta) % n, 0)


def _dwconv_tokens(g, cw_ref, seq, rows):
    tm = g.shape[0]
    width = seq // rows
    assert seq & (seq - 1) == 0 and width & (width - 1) == 0
    t = lax.broadcasted_iota(jnp.int32, (tm, 1), 0)
    pos = t & (seq - 1)
    col = pos & (width - 1)
    row = pos >> (width.bit_length() - 1)
    g_cols = (_shift_rows(g, -1) * (col >= 1).astype(F32), g,
              _shift_rows(g, 1) * (col <= width - 2).astype(F32))
    out = None
    for di in (-1, 0, 1):
        if rows == 1 and di != 0:
            continue
        r = sum(g_cols[dj] * cw_ref[pl.ds(3 * (di + 1) + dj, 1), :] for dj in range(3))
        if di != 0:
            ok = (row + di >= 0) & (row + di <= rows - 1)
            r = _shift_rows(r, di * width) * ok.astype(F32)
        out = r if out is None else out + r
    return out


def _ffn_kernel(x_ref, ng_ref, sc_ref, sh_ref, g2_ref, wa_ref, wg_ref, cw_ref, cb_ref, wd_ref, nf_ref,
                y_ref, h_scr, acc_scr, *, seq, rows, final_norm):
    j = pl.program_id(1)

    @pl.when(j == 0)
    def _():
        h_scr[...] = _norm_mod(x_ref[...], ng_ref[...], sc_ref[0], sh_ref[0]).astype(BF16)
        acc_scr[...] = jnp.zeros_like(acc_scr)

    tm = h_scr.shape[0]
    sub = tm // FFN_SUB_TILES
    parts = [pl.ds(r, sub) for r in range(0, tm, sub)]
    wa, wg, wd = (w[...].astype(BF16) for w in (wa_ref, wg_ref, wd_ref))
    gs = [jnp.dot(h_scr[p, :], wg, preferred_element_type=F32) for p in parts]
    as_ = [jnp.dot(h_scr[p, :], wa, preferred_element_type=F32) for p in parts]
    per_seq = max(seq // sub, 1)
    convs = []
    for i in range(0, len(parts), per_seq):
        g = gs[i] if per_seq == 1 else jnp.concatenate(gs[i:i + per_seq], axis=0)
        c = _dwconv_tokens(g, cw_ref, seq, rows) + cb_ref[...]
        convs += [c[k * sub:(k + 1) * sub] for k in range(per_seq)]
    acts = [(jax.nn.silu(c) * a).astype(BF16) for c, a in zip(convs, as_)]
    for p, act in zip(parts, acts):
        acc_scr[p, :] += jnp.dot(act, wd, preferred_element_type=F32)

    @pl.when(j == pl.num_programs(1) - 1)
    def _():
        y = x_ref[...] + g2_ref[0] * acc_scr[...]
        if final_norm:
            y = _rms(y, nf_ref[...])
        y_ref[...] = y


def _conv_ffn(x, seq, rows, layer, norm_g, mod, cond_of_tile, w_up, conv_w, conv_b, w_down, norm_final, final_norm):
    n_tok = x.shape[0]
    tm, tf = FFN_TOKEN_TILE, FFN_FF_TILE
    n_ff_tiles = D_FF // tf
    depth = w_up.shape[0]
    return pl.pallas_call(
        functools.partial(_ffn_kernel, seq=seq, rows=rows, final_norm=final_norm),
        grid=(n_tok // tm, n_ff_tiles),
        in_specs=[pl.BlockSpec((tm, D_MODEL), lambda i, j: (i, 0)),
                  pl.BlockSpec((1, D_MODEL), lambda i, j: (0, 0)),
                  _mod_spec(4, cond_of_tile), _mod_spec(3, cond_of_tile), _mod_spec(5, cond_of_tile),
                  pl.BlockSpec((None, D_MODEL, tf), lambda i, j: (layer, 0, j)),
                  pl.BlockSpec((None, D_MODEL, tf), lambda i, j: (layer, 0, n_ff_tiles + j)),
                  pl.BlockSpec((None, 9, tf), lambda i, j: (layer, 0, j)),
                  pl.BlockSpec((None, 1, tf), lambda i, j: (layer, 0, j)),
                  pl.BlockSpec((None, tf, D_MODEL), lambda i, j: (layer, j, 0)),
                  pl.BlockSpec((1, D_MODEL), lambda i, j: (0, 0))],
        out_specs=pl.BlockSpec((tm, D_MODEL), lambda i, j: (i, 0)),
        out_shape=jax.ShapeDtypeStruct((n_tok, D_MODEL), F32),
        scratch_shapes=[pltpu.VMEM((tm, D_MODEL), BF16), pltpu.VMEM((tm, D_MODEL), F32)],
        compiler_params=_params("parallel", "arbitrary"),
        name="conv_ffn",
    )(x, norm_g.reshape(1, D_MODEL), mod, mod, mod, w_up, w_up, conv_w.reshape(depth, 9, D_FF),
      conv_b.reshape(depth, 1, D_FF), w_down, norm_final.reshape(1, D_MODEL))


GroupMasks = collections.namedtuple("GroupMasks", "r c same incl strict")


def _chunk_masks():
    r = lax.broadcasted_iota(jnp.int32, (CHUNK, CHUNK), 0)
    c = lax.broadcasted_iota(jnp.int32, (CHUNK, CHUNK), 1)
    return (r >= c, r <= c)


def _group_masks(group=GROUP):
    r = lax.broadcasted_iota(jnp.int32, (group, group), 0)
    c = lax.broadcasted_iota(jnp.int32, (group, group), 1)
    same = (r >> LOG_CHUNK) == (c >> LOG_CHUNK)
    return GroupMasks(r, c, same, (same & (r >= c), same & (r <= c)), (same & (r > c), same & (r < c)))


def _gate_layouts(gt, batch, n_heads, hp, group=GROUP):
    seq = gt.shape[1] // batch
    n_groups, n_chunks = seq // group, seq // CHUNK
    assert seq % group == 0 and n_groups <= 8
    g = jnp.transpose(gt.reshape(N_DIR, 2, n_heads, batch, seq), (3, 2, 0, 1, 4))
    rows = g.reshape(g.shape[:4] + (n_groups, group))
    rows = jnp.pad(rows, ((0, 0),) * 4 + ((0, 8 - n_groups), (0, 0)))
    cols = jnp.swapaxes(g.reshape(batch, n_heads // hp, hp * N_DIR * 2 * n_chunks, CHUNK), -1, -2)
    return rows, cols


def _col_gate(gcol_ref, hh, d, kind, n_chunks):
    start = ((hh * N_DIR + d) * 2 + kind) * n_chunks
    return gcol_ref[0, 0, :, start:start + n_chunks]


def _split3(x):
    x1 = x.astype(BF16)
    r1 = x - x1.astype(F32)
    x2 = r1.astype(BF16)
    return x1, x2, (r1 - x2.astype(F32)).astype(BF16)


def _dot_mask_rhs(x, mask):
    m = x.shape[0]
    y = jnp.dot(jnp.concatenate(_split3(x), axis=0), jnp.where(mask, 1.0, 0.0).astype(BF16),
                preferred_element_type=F32)
    return y[:m] + y[m:2 * m] + y[2 * m:]


def _dot_mask_lhs(mask, x):
    mb = jnp.where(mask, 1.0, 0.0).astype(BF16)
    y1, y2, y3 = (jnp.dot(mb, p, preferred_element_type=F32) for p in _split3(x))
    return y1 + y2 + y3


def _cumsum_rows(row, d, gm):
    return _dot_mask_rhs(row, gm.incl[1 - d])


def _cumsum_cols(col, d, cm):
    return _dot_mask_lhs(cm[d], col)


def _store_cols(dst, col, n_chunks):
    for n in range(n_chunks):
        dst[n * CHUNK:(n + 1) * CHUNK, :] = jnp.broadcast_to(col[:, n:n + 1], (CHUNK, 128))


def _wide(x, width=None):
    reps = (GROUP if width is None else width) // x.shape[1]
    return x if reps == 1 else jnp.concatenate([x] * reps, axis=1)


def _rows(i, size):
    if isinstance(i, int):
        return pl.ds(i * size, size)
    return pl.ds(pl.multiple_of(i * size, size), size)


def _loop(n, body, init):
    if n == 1:
        return body(0, init)
    return lax.fori_loop(0, n, body, init)


def _conv_silu(x, w_ref):
    n = x.shape[0]
    k = w_ref.shape[0]
    t = lax.broadcasted_iota(jnp.int32, (n, 1), 0)
    acc = None
    for j in range(k):
        delta = j - k // 2
        if delta == 0:
            term = x
        else:
            ok = (t + delta >= 0) & (t + delta <= n - 1)
            term = _shift_rows(x, delta) * ok.astype(F32)
        term = term * w_ref[pl.ds(j, 1), :]
        acc = term if acc is None else acc + term
    return jax.nn.silu(acc)


def _l2norm(x):
    return x * lax.rsqrt(jnp.sum(x * x, axis=-1, keepdims=True) + EPS)


def _blockdiag_tri_inverse(mats, gm):
    dot16 = lambda x, y: jnp.dot(x, y, preferred_element_type=F32).astype(BF16)
    base = (gm.r >> 3) == (gm.c >> 3)
    eye = jnp.where(gm.r == gm.c, 1.0, 0.0)
    zero16 = jnp.zeros((), BF16)
    a16s = [a.astype(BF16) for a in mats]
    invs = [eye - jnp.where(base, a, 0.0) for a in mats]
    ps = [jnp.where(base, -a16, zero16) for a16 in a16s]
    for _ in range(2):
        ps = [dot16(p, p) for p in ps]
        invs = [inv + jnp.dot(inv.astype(BF16), p, preferred_element_type=F32) for inv, p in zip(invs, ps)]
    for s in (3, 4, 5):
        join = ((gm.r >> s) ^ (gm.c >> s)) == 1
        inv16s = [inv.astype(BF16) for inv in invs]
        tmp = [dot16(jnp.where(join, a16, zero16), inv16) for a16, inv16 in zip(a16s, inv16s)]
        invs = [inv - jnp.dot(inv16, t, preferred_element_type=F32) for inv, inv16, t in zip(invs, inv16s, tmp)]
    return invs


def _gdn_kernel(*refs, seq, hp, zero_init, emit_state):
    alog_ref, dtb_ref, q_ref, k_ref, v_ref, cwq_ref, cwk_ref, cwv_ref, grow_ref, gcol_ref = refs[:10]
    pos = 10
    s0_ref = None
    if not zero_init:
        s0_ref = refs[pos]
        pos += 1
    o_ref = refs[pos]
    pos += 1
    sfin_ref = None
    if emit_state:
        sfin_ref = refs[pos]
        pos += 1
    (q_scr, k_scr, v_scr, gcr_scr, gcb_scr, bb_scr, gl_scr,
     u_scr, w_scr, qd_scr, kd_scr, qkd_scr) = refs[pos:]

    n_chunks = seq // CHUNK
    grp = GDN_GROUP
    cm = _chunk_masks()
    gm = _group_masks(grp)
    head_dirs = [(hh, d) for hh in range(hp) for d in range(N_DIR)]
    lanes = lambda hh: slice(hh * DN_DK, (hh + 1) * DN_DK)

    q_scr[...] = _conv_silu(q_ref[...].astype(F32), cwq_ref)
    k_scr[...] = _conv_silu(k_ref[...].astype(F32), cwk_ref)
    v_scr[...] = _conv_silu(v_ref[...].astype(F32), cwv_ref)
    for hh in range(hp):
        q_scr[:, lanes(hh)] = _l2norm(q_scr[:, lanes(hh)]) * (DN_DK ** -0.5)
        k_scr[:, lanes(hh)] = _l2norm(k_scr[:, lanes(hh)])

    for hh, d in head_dirs:
        head = pl.program_id(1) * hp + hh
        neg_a = -jnp.exp(jnp.full((1, 1), alog_ref[d, head], F32))
        dtb = dtb_ref[d, head]
        gcr_scr[hh, d] = _cumsum_rows(neg_a * _softplus(grow_ref[0, hh, d, 0] + dtb), d, gm)
        gc_col = _cumsum_cols(neg_a * _softplus(_col_gate(gcol_ref, hh, d, 0, n_chunks) + dtb), d, cm)
        _store_cols(gcb_scr.at[hh, d], gc_col, n_chunks)
        _store_cols(bb_scr.at[hh, d], jax.nn.sigmoid(_col_gate(gcol_ref, hh, d, 1, n_chunks)), n_chunks)
        last = CHUNK - 1 if d == 0 else 0
        _store_cols(gl_scr.at[hh, d], jnp.broadcast_to(gc_col[last:last + 1, :], gc_col.shape), n_chunks)

    gps = min(GDN_GROUPS_PER_STEP, seq // grp)

    def group_step(it, carry):
        gis = [it * gps + k for k in range(gps)]
        rows = [_rows(gi, grp) for gi in gis]
        for h0 in range(0, hp, GDN_HEADS_PER_PASS):
            heads = range(h0, min(h0 + GDN_HEADS_PER_PASS, hp))
            kqs = {}
            for k, r in enumerate(rows):
                for hh in heads:
                    kg, qg = k_scr[r, lanes(hh)], q_scr[r, lanes(hh)]
                    kqs[k, hh] = _dot_nt(jnp.concatenate([kg, qg], axis=0), kg)
            chains = [(k, hh, d) for k in range(len(gis)) for hh in heads for d in range(N_DIR)]
            mats = []
            for k, hh, d in chains:
                gcb = gcb_scr[hh, d, rows[k], :]
                gcr = gcr_scr[hh, d, pl.ds(gis[k], 1), :]
                decay = jnp.exp(jnp.where(gm.incl[d], _wide(gcb, grp) - gcr, -jnp.inf))
                mats.append(jnp.where(gm.strict[d],
                                      kqs[k, hh][:grp] * _wide(bb_scr[hh, d, rows[k], :], grp) * decay, 0.0))
                qkd = kqs[k, hh][grp:] * decay
                qkd_scr[hh, d, rows[k], :] = sum(qkd[:, n * CHUNK:(n + 1) * CHUNK] for n in range(grp // CHUNK))
            t_invs = _blockdiag_tri_inverse(mats, gm)
            for (k, hh, d), t_inv in zip(chains, t_invs):
                r = rows[k]
                qg, kg, vg = q_scr[r, lanes(hh)], k_scr[r, lanes(hh)], v_scr[r, lanes(hh)]
                gcb = gcb_scr[hh, d, r, :]
                bb = bb_scr[hh, d, r, :]
                e_gc = jnp.exp(gcb)
                uw = _dot(t_inv, jnp.concatenate([vg * bb, kg * bb * e_gc], axis=1))
                u_scr[hh, d, r, :] = uw[:, :DN_DV]
                w_scr[hh, d, r, :] = uw[:, DN_DV:]
                qd_scr[hh, d, r, :] = qg * e_gc
                kd_scr[hh, d, r, :] = kg * jnp.exp(gl_scr[hh, d, r, :] - gcb)
        return carry

    _loop(seq // (grp * gps), group_step, 0)

    o_ref[...] = jnp.zeros_like(o_ref)

    def chunk_step(n, carry):
        idxs = [n if d == 0 else n_chunks - 1 - n for _, d in head_dirs]
        rows = [_rows(idx, CHUNK) for idx in idxs]
        wqs = [_dot(jnp.concatenate([w_scr[hh, d, r, :], qd_scr[hh, d, r, :]], axis=0), s)
               for (hh, d), r, s in zip(head_dirs, rows, carry)]
        v_news = [u_scr[hh, d, r, :] - wq[:CHUNK] for (hh, d), r, wq in zip(head_dirs, rows, wqs)]
        o_ns = [wq[CHUNK:] + _dot(qkd_scr[hh, d, r, :], v_new)
                for (hh, d), r, wq, v_new in zip(head_dirs, rows, wqs, v_news)]
        new = []
        for (hh, d), idx, r, s, v_new in zip(head_dirs, idxs, rows, carry, v_news):
            s_decay = jnp.exp(gl_scr[hh, d, pl.ds(idx * CHUNK, 1), :])
            new.append(s * s_decay + _dot_tn(kd_scr[hh, d, r, :], v_new))
        for (hh, d), r, o_n in zip(head_dirs, rows, o_ns):
            o_ref[r, lanes(hh)] += o_n
        return tuple(new)

    if zero_init:
        init = (jnp.zeros((DN_DK, DN_DV), F32),) * len(head_dirs)
    else:
        init = tuple(s0_ref[0, d, hh] for hh, d in head_dirs)
    fin = lax.fori_loop(0, n_chunks, chunk_step, init)
    if emit_state:
        for i, (hh, d) in enumerate(head_dirs):
            sfin_ref[0, d, hh] = fin[i]


def _gdn_scan(main, gt, batch, seq, conv_w, a_log, dt_bias, s0, emit_state):
    n_tok = batch * seq
    nh = DN_HEADS
    hp = min(nh, max(2, GDN_STEP_POSITIONS // seq))
    n_hb = nh // hp
    assert seq % (GDN_GROUP * min(GDN_GROUPS_PER_STEP, seq // GDN_GROUP)) == 0
    grow, gcol = _gate_layouts(gt, batch, nh, hp, GDN_GROUP)
    zero_init = s0 is None
    smem = pl.BlockSpec(memory_space=pltpu.SMEM)
    qkv_spec = lambda off: pl.BlockSpec((seq, hp * DN_DK), lambda b, h: (b, off + h))
    cw_spec = lambda off: pl.BlockSpec((DN_CONV, hp * DN_DK), lambda b, h: (0, off + h))
    state_spec = pl.BlockSpec((1, N_DIR, hp, DN_DK, DN_DV), lambda b, h: (b, 0, h, 0, 0))
    in_specs = [smem, smem, qkv_spec(0), qkv_spec(n_hb), qkv_spec(2 * n_hb),
                cw_spec(0), cw_spec(n_hb), cw_spec(2 * n_hb),
                pl.BlockSpec((1, hp) + grow.shape[2:], lambda b, h: (b, h, 0, 0, 0, 0)),
                pl.BlockSpec((1, 1) + gcol.shape[2:], lambda b, h: (b, h, 0, 0))]
    args = [a_log, dt_bias, main, main, main, conv_w, conv_w, conv_w, grow, gcol]
    if not zero_init:
        in_specs.append(state_spec)
        args.append(s0)
    out_specs = [pl.BlockSpec((seq, hp * DN_DV), lambda b, h: (b, h))]
    out_shape = [jax.ShapeDtypeStruct((n_tok, nh * DN_DV), F32)]
    if emit_state:
        out_specs.append(state_spec)
        out_shape.append(jax.ShapeDtypeStruct((batch, N_DIR, nh, DN_DK, DN_DV), F32))
    per_dir = lambda width: pltpu.VMEM((hp, N_DIR, seq, width), F32)
    qkv_scr = pltpu.VMEM((seq, hp * DN_DK), F32)
    outs = pl.pallas_call(
        functools.partial(_gdn_kernel, seq=seq, hp=hp, zero_init=zero_init, emit_state=emit_state),
        grid=(batch, n_hb),
        in_specs=in_specs, out_specs=out_specs, out_shape=out_shape,
        scratch_shapes=[qkv_scr, qkv_scr, qkv_scr,
                        pltpu.VMEM((hp, N_DIR, 8, GDN_GROUP), F32),
                        per_dir(128), per_dir(128), per_dir(128),
                        per_dir(DN_DV), per_dir(DN_DK), per_dir(DN_DK), per_dir(DN_DK), per_dir(CHUNK)],
        compiler_params=_params("parallel", "parallel"),
        name="gdn_scan",
    )(*args)
    return outs[0], (outs[1] if emit_state else None)


ML_AUG = 2 * ML_DV


def _mlstm_kernel(*refs, seq, hp, zero_init, emit_state):
    bi_ref, bf_ref, q_ref, k_ref, v_ref, kt_ref, grow_ref, gcol_ref = refs[:8]
    pos = 8
    c0_ref = m0_ref = None
    if not zero_init:
        c0_ref, m0_ref = refs[pos:pos + 2]
        pos += 2
    o_ref = refs[pos]
    pos += 1
    cfin_ref = nfin_ref = mfin_ref = None
    if emit_state:
        cfin_ref, nfin_ref, mfin_ref = refs[pos:pos + 3]
        pos += 3
    bcr_scr, lir_scr, kwr_scr, bcb_scr, mi_scr, dec_scr, dc_scr, cin_scr = refs[pos:]

    n_chunks = seq // CHUNK
    n_groups = seq // GROUP
    head0 = pl.program_id(1) * hp
    cm = _chunk_masks()
    gm = _group_masks()
    ones_col = jnp.where(lax.broadcasted_iota(jnp.int32, (GROUP, ML_DV), 1) == 0, 1.0, 0.0)
    head_dirs = [(hh, d) for hh in range(hp) for d in range(N_DIR)]

    m_fin = {}
    row_group = lax.broadcasted_iota(jnp.int32, (8, GROUP), 0)
    row_chunk = lax.broadcasted_iota(jnp.int32, (8, GROUP), 1) >> LOG_CHUNK
    for hh, d in head_dirs:
        b_i = bi_ref[d, head0 + hh]
        b_f = bf_ref[d, head0 + hh]
        li_row = grow_ref[0, hh, d, 0] + b_i
        lf_row = -_softplus(-(grow_ref[0, hh, d, 1] + b_f))
        bc_row = _cumsum_rows(lf_row, d, gm)
        lir_scr[hh, d] = li_row
        bcr_scr[hh, d] = bc_row
        w_row = _dot_mask_rhs(lf_row, gm.same) - bc_row + li_row
        m_out_row = jnp.zeros((8, GROUP), F32)
        li_col = _col_gate(gcol_ref, hh, d, 0, n_chunks) + b_i
        bc_col = _cumsum_cols(-_softplus(-(_col_gate(gcol_ref, hh, d, 1, n_chunks) + b_f)), d, cm)
        last = CHUNK - 1 if d == 0 else 0
        b_last = bc_col[last:last + 1, :]
        w_col = b_last - bc_col + li_col
        w_max = jnp.max(w_col, axis=0, keepdims=True)
        m = jnp.zeros((1, 1), F32) if zero_init else m0_ref[0, d, hh][:, 0:1]
        for step in range(n_chunks):
            n = step if d == 0 else n_chunks - 1 - step
            sl = slice(n * CHUNK, (n + 1) * CHUNK)
            m_new = jnp.maximum(b_last[:, n:n + 1] + m, w_max[:, n:n + 1])
            bcb_scr[hh, d, sl, :] = jnp.broadcast_to(bc_col[:, n:n + 1], (CHUNK, 128))
            mi_scr[hh, d, sl, :] = jnp.broadcast_to(m, (CHUNK, 128))
            dec_scr[hh, d, n:n + 1, :] = jnp.broadcast_to(jnp.exp(b_last[:, n:n + 1] + m - m_new), (1, 128))
            in_chunk = (row_group == n // CHUNKS_PER_GROUP) & (row_chunk == n % CHUNKS_PER_GROUP)
            m_out_row = jnp.where(in_chunk, m_new, m_out_row)
            m = m_new
        m_fin[hh, d] = m
        kwr_scr[hh, d] = jnp.exp(w_row - m_out_row)

    def load_kv(rows, hh):
        kg = k_ref[rows, hh * ML_DQK:(hh + 1) * ML_DQK].astype(F32)
        v_aug = jnp.concatenate([v_ref[rows, hh * ML_DV:(hh + 1) * ML_DV].astype(F32), ones_col], axis=1)
        return kg, v_aug

    def delta_step(gi, carry):
        rows = _rows(gi, GROUP)
        v_augs = [load_kv(rows, hh)[1] for hh in range(hp)]
        kt = kt_ref[gi]
        lhs = []
        for hh, d in head_dirs:
            kw_t = kt[hh * ML_DQK:(hh + 1) * ML_DQK, :] * kwr_scr[hh, d, pl.ds(gi, 1), :]
            lhs.append(jnp.where(gm.same, jnp.concatenate([kw_t] * CHUNKS_PER_GROUP, axis=0), 0.0))
        for (hh, d), kw in zip(head_dirs, lhs):
            dc_scr[hh, d, rows, :] = _dot(kw, v_augs[hh])
        return carry

    _loop(n_groups, delta_step, 0)

    def prefix_step(n, carry):
        new = []
        for (hh, d), c_aug in zip(head_dirs, carry):
            idx = n if d == 0 else n_chunks - 1 - n
            rows = _rows(idx, CHUNK)
            cin_scr[hh, d, rows, :] = c_aug
            new.append(c_aug * _wide(dec_scr[hh, d, pl.ds(idx, 1), :], ML_AUG) + dc_scr[hh, d, rows, :])
        return tuple(new)

    if zero_init:
        init = (jnp.zeros((ML_DQK, ML_AUG), F32),) * len(head_dirs)
    else:
        init = tuple(c0_ref[0, d, hh] for hh, d in head_dirs)
    fin = lax.fori_loop(0, n_chunks, prefix_step, init)

    def out_step(gi, carry):
        rows = _rows(gi, GROUP)
        kvs = [load_kv(rows, hh) for hh in range(hp)]
        qgs = [q_ref[rows, hh * ML_DQK:(hh + 1) * ML_DQK].astype(F32) * (ML_DQK ** -0.5) for hh in range(hp)]
        qks = [_dot_nt(qg, kv[0]) for qg, kv in zip(qgs, kvs)]
        chunk = lambda n: slice(n * CHUNK, (n + 1) * CHUNK)
        inters = []
        for hh, d in head_dirs:
            c_in = cin_scr[hh, d, rows, :]
            inters.append(jnp.concatenate(
                [_dot(qgs[hh][chunk(n)], c_in[chunk(n)]) for n in range(CHUNKS_PER_GROUP)], axis=0))
        m_ts, b_ms, ss = [], [], []
        for hh, d in head_dirs:
            b_colb = bcb_scr[hh, d, rows, :]
            b_m = b_colb[:, 0:1] + mi_scr[hh, d, rows, :][:, 0:1]
            d_log = jnp.where(gm.incl[d], _wide(b_colb) - bcr_scr[hh, d, pl.ds(gi, 1), :]
                              + lir_scr[hh, d, pl.ds(gi, 1), :], -jnp.inf)
            m_t = jnp.maximum(b_m, jnp.max(d_log, axis=-1, keepdims=True))
            m_ts.append(m_t)
            b_ms.append(b_m)
            ss.append(qks[hh] * jnp.exp(d_log - m_t))
        intras = [_dot(s, kvs[hh][1]) for (hh, d), s in zip(head_dirs, ss)]
        hs = []
        for inter, intra, m_t, b_m in zip(inters, intras, m_ts, b_ms):
            num = jnp.exp(b_m - m_t) * inter + intra
            den = num[:, ML_DV:ML_DV + 1]
            hs.append(num[:, :ML_DV] / jnp.maximum(jnp.abs(den), jnp.exp(-m_t)))
        for hh in range(hp):
            o_ref[rows, hh * ML_DV:(hh + 1) * ML_DV] = hs[N_DIR * hh] + hs[N_DIR * hh + 1]
        return carry

    _loop(n_groups, out_step, 0)

    if emit_state:
        for i, (hh, d) in enumerate(head_dirs):
            cfin_ref[0, d, hh] = fin[i][:, :ML_DV]
            nfin_ref[0, d, hh] = fin[i][:, ML_DV:]
            mfin_ref[0, d, hh] = jnp.broadcast_to(m_fin[hh, d], (1, 128))


def _mlstm_scan(main, kt, gt, batch, seq, b_i, b_f, state0, emit_state):
    n_tok = batch * seq
    n_groups = seq // GROUP
    n_chunks = seq // CHUNK
    nh = ML_HEADS
    hp = min(nh, max(2, SCAN_STEP_POSITIONS // seq))
    n_hb = nh // hp
    grow, gcol = _gate_layouts(gt, batch, nh, hp)
    zero_init = state0 is None
    smem = pl.BlockSpec(memory_space=pltpu.SMEM)
    k_off = (nh * ML_DQK) // (hp * ML_DQK)
    v_off = (2 * nh * ML_DQK) // (hp * ML_DV)
    c_spec = pl.BlockSpec((1, N_DIR, hp, ML_DQK, ML_AUG), lambda b, p: (b, 0, p, 0, 0))
    m_spec = pl.BlockSpec((1, N_DIR, hp, 1, 128), lambda b, p: (b, 0, p, 0, 0))
    in_specs = [smem, smem,
                pl.BlockSpec((seq, hp * ML_DQK), lambda b, p: (b, p)),
                pl.BlockSpec((seq, hp * ML_DQK), lambda b, p: (b, k_off + p)),
                pl.BlockSpec((seq, hp * ML_DV), lambda b, p: (b, v_off + p)),
                pl.BlockSpec((n_groups, hp * ML_DQK, GROUP), lambda b, p: (b, p, 0)),
                pl.BlockSpec((1, hp) + grow.shape[2:], lambda b, p: (b, p, 0, 0, 0, 0)),
                pl.BlockSpec((1, 1) + gcol.shape[2:], lambda b, p: (b, p, 0, 0))]
    args = [b_i, b_f, main, main, main, kt, grow, gcol]
    if not zero_init:
        in_specs += [c_spec, m_spec]
        args += list(state0)
    out_specs = [pl.BlockSpec((seq, hp * ML_DV), lambda b, p: (b, p))]
    out_shape = [jax.ShapeDtypeStruct((n_tok, nh * ML_DV), F32)]
    if emit_state:
        half_spec = pl.BlockSpec((1, N_DIR, hp, ML_DQK, ML_DV), lambda b, p: (b, 0, p, 0, 0))
        out_specs += [half_spec, half_spec, m_spec]
        out_shape += [jax.ShapeDtypeStruct((batch, N_DIR, nh, ML_DQK, ML_DV), F32),
                      jax.ShapeDtypeStruct((batch, N_DIR, nh, ML_DQK, ML_AUG - ML_DV), F32),
                      jax.ShapeDtypeStruct((batch, N_DIR, nh, 1, 128), F32)]
    per_hd = lambda rows, width: pltpu.VMEM((hp, N_DIR, rows, width), F32)
    outs = pl.pallas_call(
        functools.partial(_mlstm_kernel, seq=seq, hp=hp, zero_init=zero_init, emit_state=emit_state),
        grid=(batch, n_hb),
        in_specs=in_specs, out_specs=out_specs, out_shape=out_shape,
        scratch_shapes=[per_hd(8, GROUP), per_hd(8, GROUP), per_hd(8, GROUP),
                        per_hd(seq, 128), per_hd(seq, 128),
                        per_hd(max(n_chunks, 8), 128),
                        per_hd(seq, ML_AUG), per_hd(seq, ML_AUG)],
        compiler_params=_params("parallel", "parallel"),
        name="mlstm_scan",
    )(*args)
    return outs[0], (tuple(outs[1:4]) if emit_state else None)


def _trunk(x, batch, seq, rows, mods, tokens_per_cond, st_d, st_ml, emit_state, p):
    depth = p["w_ada"].shape[0]
    tile_cond = lambda tile: (lambda i: (i * tile) // tokens_per_cond)
    new_d = new_ml = None
    for layer in range(depth):
        mod = mods[layer]
        kind, j = layer % 3, layer // 3
        if kind == 0:
            x = _fnet_mix(x, seq, j, p["norm_mix"][layer], mod, tile_cond(seq), p["fnet_w"], p["fnet_b"][j])
        elif kind == 1:
            n_main = DN_HEADS * (2 * DN_DK + 2 * DN_DV)
            main, gt = _norm_proj(x, p["norm_mix"][layer], mod, tokens_per_cond, p["dn_w_in"], j, n_main)
            s0 = None if st_d is None else st_d[:, j]
            o, sfin = _gdn_scan(main, gt, batch, seq, p["dn_conv_w"][j], p["dn_a_log"][j], p["dn_dt_bias"][j],
                                s0, emit_state)
            if emit_state:
                new_d = sfin
            x = _gated_out(x, o, main, 3, p["dn_norm"][j], mod, tile_cond(OUT_TOKEN_TILE), p["dn_w_out"][j],
                           DN_HEADS, jax.nn.silu)
        else:
            n_main = 2 * ML_HEADS * ML_DQK + 2 * ML_HEADS * ML_DV
            k_tile = (ML_HEADS * ML_DQK) // PROJ_COL_TILE
            assert ML_HEADS * ML_DQK == PROJ_COL_TILE
            main, gt, kt = _norm_proj(x, p["norm_mix"][layer], mod, tokens_per_cond, p["ml_w_in"], j, n_main,
                                      t_tile=k_tile)
            state0 = None
            if st_ml is not None:
                c0, n0, m0 = (s[:, j] for s in st_ml)
                pad = jnp.zeros(c0.shape[:-1] + (ML_AUG - ML_DV - 1,), F32)
                c_aug0 = jnp.concatenate([c0, n0[..., None], pad], axis=-1)
                m0b = jnp.broadcast_to(m0[..., None, None], m0.shape + (1, 128))
                state0 = (c_aug0, m0b)
            o, fin = _mlstm_scan(main, kt, gt, batch, seq, p["ml_b_i"][j], p["ml_b_f"][j], state0, emit_state)
            if emit_state:
                new_ml = (fin[0], fin[1][..., 0], fin[2][..., 0, 0])
            x = _gated_out(x, o, main, 2, p["ml_norm"][j], mod, tile_cond(OUT_TOKEN_TILE), p["ml_w_out"][j],
                           ML_HEADS, jax.nn.sigmoid)
        x = _conv_ffn(x, seq, rows, layer, p["norm_ffn"][layer], mod, tile_cond(FFN_TOKEN_TILE), p["ffn_w_up"],
                      p["ffn_conv_w"], p["ffn_conv_b"], p["ffn_w_down"], p["norm_final"],
                      final_norm=(layer == depth - 1))
    return x, new_d, new_ml


def kernel(x_prompt, x_sample, state_delta, state_mlstm_c, state_mlstm_n, state_mlstm_m, c, c_ctx, w_ada, b_ada, norm_mix, norm_ffn, norm_final, ffn_w_up, ffn_conv_w, ffn_conv_b, ffn_w_down, fnet_w, fnet_b, dn_w_in, dn_conv_w, dn_a_log, dn_dt_bias, dn_norm, dn_w_out, ml_w_in, ml_b_i, ml_b_f, ml_norm, ml_w_out):
    p = dict(w_ada=w_ada, norm_mix=norm_mix, norm_ffn=norm_ffn, norm_final=norm_final, ffn_w_up=ffn_w_up,
             ffn_conv_w=ffn_conv_w, ffn_conv_b=ffn_conv_b, ffn_w_down=ffn_w_down, fnet_w=fnet_w, fnet_b=fnet_b,
             dn_w_in=dn_w_in, dn_conv_w=dn_conv_w, dn_a_log=dn_a_log, dn_dt_bias=dn_dt_bias, dn_norm=dn_norm,
             dn_w_out=dn_w_out, ml_w_in=ml_w_in, ml_b_i=ml_b_i, ml_b_f=ml_b_f, ml_norm=ml_norm, ml_w_out=ml_w_out)
    b_ctx, t_ctx, _ = x_prompt.shape
    b_smp, t_smp, _ = x_sample.shape
    depth = w_ada.shape[0]

    cond8 = jnp.concatenate([c_ctx[None, :], c, jnp.zeros((8 - 1 - b_smp, D_MODEL), F32)], axis=0)
    mods = _adaln_all(cond8, w_ada, b_ada)
    mods_ctx = [mods[l, 0:1].reshape(1, 1, -1) for l in range(depth)]
    mods_smp = [mods[l, 1:1 + b_smp].reshape(b_smp, 1, -1) for l in range(depth)]

    y_ctx, new_d, new_ml = _trunk(x_prompt.reshape(b_ctx * t_ctx, D_MODEL), b_ctx, t_ctx, 1, mods_ctx,
                                  b_ctx * t_ctx, None, None, True, p)
    y_smp, _, _ = _trunk(x_sample.reshape(b_smp * t_smp, D_MODEL), b_smp, t_smp, t_smp // GRID_W, mods_smp,
                         t_smp, state_delta, (state_mlstm_c, state_mlstm_n, state_mlstm_m), False, p)

    new_c, new_n, new_m = new_ml
    return (y_ctx.reshape(b_ctx, t_ctx, D_MODEL), y_smp.reshape(b_smp, t_smp, D_MODEL),
            new_d[:, None], new_c[:, None], new_n[:, None], new_m[:, None])
```

```python
import collections
import functools

import numpy as np
import jax
import jax.numpy as jnp
from jax import lax
from jax.experimental import pallas as pl
from jax.experimental.pallas import tpu as pltpu

F32 = jnp.float32
BF16 = jnp.bfloat16

D_MODEL = 1024
EPS = 1e-6
N_DIR = 2
CHUNK = 64
LOG_CHUNK = 6
GROUP = 256
CHUNKS_PER_GROUP = GROUP // CHUNK
GDN_GROUP = 128
GDN_GROUPS_PER_STEP = 4
GDN_HEADS_PER_PASS = 4
GDN_STEP_POSITIONS = 2048
FNET_GROUP_DIM = 256
DN_HEADS, DN_DK, DN_DV, DN_CONV = 8, 128, 128, 5
ML_HEADS, ML_DQK, ML_DV = 8, 64, 128
D_FF = 2816
GRID_W = 64

FNET_STEP_TOKENS = 1024
FFN_TOKEN_TILE = 1024
FFN_FF_TILE = 256
FFN_SUB_TILES = 4
PROJ_TOKEN_TILE = 2048
PROJ_COL_TILE = 512
OUT_TOKEN_TILE = 512
ADA_COL_TILE = 1536
SCAN_STEP_POSITIONS = 1024
VMEM_LIMIT = 56 * 1024 * 1024


def _params(*sem):
    return pltpu.CompilerParams(dimension_semantics=sem, vmem_limit_bytes=VMEM_LIMIT)


def _dot(a, b):
    return jnp.dot(a.astype(BF16), b.astype(BF16), preferred_element_type=F32)


def _dot_nt(a, b):
    return lax.dot_general(a.astype(BF16), b.astype(BF16), (((1,), (1,)), ((), ())),
                           preferred_element_type=F32)


def _dot_tn(a, b):
    return lax.dot_general(a.astype(BF16), b.astype(BF16), (((0,), (0,)), ((), ())),
                           preferred_element_type=F32)


def _rms(x, g):
    return x * lax.rsqrt(jnp.mean(x * x, axis=-1, keepdims=True) + EPS) * g


def _norm_mod(x, g, sc, sh):
    return _rms(x, g) * (1.0 + sc) + sh


def _softplus(x):
    return jnp.maximum(x, 0.0) + jnp.log1p(jnp.exp(-jnp.abs(x)))


def _mod_spec(chunk, cond_of_tile):
    return pl.BlockSpec((1, 1, D_MODEL), lambda i, *_: (cond_of_tile(i), 0, chunk))


def _adaln_kernel(c_ref, w_ref, b_ref, o_ref):
    s = jax.nn.silu(c_ref[...])
    o_ref[0] = _dot(s, w_ref[0]) + b_ref[0]


def _adaln_all(cond8, w_ada, b_ada):
    depth = w_ada.shape[0]
    n_out = w_ada.shape[2]
    tn = ADA_COL_TILE
    return pl.pallas_call(
        _adaln_kernel,
        grid=(depth, n_out // tn),
        in_specs=[pl.BlockSpec((8, D_MODEL), lambda l, j: (0, 0)),
                  pl.BlockSpec((1, D_MODEL, tn), lambda l, j: (l, 0, j)),
                  pl.BlockSpec((1, 1, tn), lambda l, j: (l, 0, j))],
        out_specs=pl.BlockSpec((1, 8, tn), lambda l, j: (l, 0, j)),
        out_shape=jax.ShapeDtypeStruct((depth, 8, n_out), F32),
        compiler_params=_params("parallel", "parallel"),
        name="adaln",
    )(cond8, w_ada, b_ada.reshape(depth, 1, n_out))


def _norm_proj_kernel(x_ref, g_ref, sc_ref, sh_ref, w_ref, wg_ref, o_ref, gate_ref, *rest, t_tile):
    h_scr = rest[-1]

    @pl.when(pl.program_id(1) == 0)
    def _():
        h = _norm_mod(x_ref[...], g_ref[...], sc_ref[0], sh_ref[0]).astype(BF16)
        h_scr[...] = h
        gate_ref[...] = _dot_nt(wg_ref[...], h)

    y = _dot_nt(h_scr[...], w_ref[...])
    o_ref[...] = y.astype(o_ref.dtype)

    if t_tile is not None:
        t_ref = rest[0]

        @pl.when(pl.program_id(1) == t_tile)
        def _():
            for g in range(t_ref.shape[0]):
                t_ref[g] = y[g * GROUP:(g + 1) * GROUP, :].T


def _norm_proj(x, norm_g, mod, tokens_per_cond, w_stack, layer, n_main, t_tile=None):
    n_tok = x.shape[0]
    tm, tn = min(PROJ_TOKEN_TILE, tokens_per_cond), PROJ_COL_TILE
    cond_of_tile = lambda i: (i * tm) // tokens_per_cond
    w_t = jnp.swapaxes(w_stack, 1, 2)
    n_gate = w_t.shape[1] - n_main
    assert n_main % n_gate == 0 and n_gate % 8 == 0
    out_specs = [pl.BlockSpec((tm, tn), lambda i, j: (i, j)),
                 pl.BlockSpec((n_gate, tm), lambda i, j: (0, i))]
    out_shape = [jax.ShapeDtypeStruct((n_tok, n_main), BF16),
                 jax.ShapeDtypeStruct((n_gate, n_tok), F32)]
    if t_tile is not None:
        out_specs.append(pl.BlockSpec((tm // GROUP, tn, GROUP), lambda i, j: (i, 0, 0)))
        out_shape.append(jax.ShapeDtypeStruct((n_tok // GROUP, tn, GROUP), F32))
    return pl.pallas_call(
        functools.partial(_norm_proj_kernel, t_tile=t_tile),
        grid=(n_tok // tm, n_main // tn),
        in_specs=[pl.BlockSpec((tm, D_MODEL), lambda i, j: (i, 0)),
                  pl.BlockSpec((1, D_MODEL), lambda i, j: (0, 0)),
                  _mod_spec(1, cond_of_tile), _mod_spec(0, cond_of_tile),
                  pl.BlockSpec((None, tn, D_MODEL), lambda i, j: (layer, j, 0)),
                  pl.BlockSpec((None, n_gate, D_MODEL), lambda i, j: (layer, n_main // n_gate, 0))],
        out_specs=out_specs, out_shape=out_shape,
        scratch_shapes=[pltpu.VMEM((tm, D_MODEL), BF16)],
        compiler_params=_params("parallel", "arbitrary"),
        name="norm_proj",
    )(x, norm_g.reshape(1, D_MODEL), mod, mod, w_t, w_t)


def _gated_out_kernel(x_ref, o_ref, z_ref, ng_ref, g1_ref, w_ref, y_ref, wb_scr, *, n_heads, gate_fn):
    @pl.when(pl.program_id(0) == 0)
    def _():
        wb_scr[...] = w_ref[...].astype(BF16)

    dv = o_ref.shape[1] // n_heads
    parts = []
    for h in range(n_heads):
        sl = slice(h * dv, (h + 1) * dv)
        parts.append((_rms(o_ref[:, sl].astype(F32), ng_ref[...]) * gate_fn(z_ref[:, sl].astype(F32))).astype(BF16))
    hs = jnp.concatenate(parts, axis=1)
    y = jnp.dot(hs, wb_scr[...], preferred_element_type=F32)
    y_ref[...] = x_ref[...] + g1_ref[0] * y


def _gated_out(x, o, main, z_block, norm_g, mod, cond_of_tile, w_out, n_heads, gate_fn):
    n_tok = x.shape[0]
    tm = OUT_TOKEN_TILE
    width = o.shape[1]
    return pl.pallas_call(
        functools.partial(_gated_out_kernel, n_heads=n_heads, gate_fn=gate_fn),
        grid=(n_tok // tm,),
        in_specs=[pl.BlockSpec((tm, D_MODEL), lambda i: (i, 0)),
                  pl.BlockSpec((tm, width), lambda i: (i, 0)),
                  pl.BlockSpec((tm, width), lambda i: (i, z_block)),
                  pl.BlockSpec((1, width // n_heads), lambda i: (0, 0)),
                  _mod_spec(2, cond_of_tile),
                  pl.BlockSpec((width, D_MODEL), lambda i: (0, 0))],
        out_specs=pl.BlockSpec((tm, D_MODEL), lambda i: (i, 0)),
        out_shape=jax.ShapeDtypeStruct((n_tok, D_MODEL), F32),
        scratch_shapes=[pltpu.VMEM((width, D_MODEL), BF16)],
        compiler_params=_params("arbitrary"),
        name="gated_out",
    )(x, o, main, norm_g.reshape(1, -1), mod, w_out)


def _dft_mats(n):
    k = np.arange(n, dtype=np.int64)
    ang = 2.0 * np.pi * ((k[:, None] * k[None, :]) % n).astype(np.float64) / n
    s = 1.0 / np.sqrt(n)
    return np.cos(ang) * s, np.sin(ang) * s


def _fnet_kernel(x_ref, ng_ref, sc_ref, sh_ref, g1_ref, ct_ref, st_ref, cs_ref, w_ref, b_ref, y_ref, wb_scr):
    @pl.when(pl.program_id(0) == 0)
    def _():
        wb_scr[...] = w_ref[...].astype(BF16)

    seq = ct_ref.shape[0]
    seqs = [pl.ds(r, seq) for r in range(0, x_ref.shape[0], seq)]
    gd = FNET_GROUP_DIM
    hs = [_norm_mod(x_ref[s, :], ng_ref[...], sc_ref[0], sh_ref[0]).astype(BF16) for s in seqs]
    ps = [jnp.dot(ct_ref[...], h, preferred_element_type=F32) for h in hs]
    qs = [jnp.dot(st_ref[...], h, preferred_element_type=F32) for h in hs]
    fs = []
    for p, q in zip(ps, qs):
        parts = []
        for g in range(D_MODEL // gd):
            sl = slice(g * gd, (g + 1) * gd)
            pq = jnp.concatenate([p[:, sl], q[:, sl]], axis=1).astype(BF16)
            parts.append(jnp.dot(pq, cs_ref[...], preferred_element_type=F32).astype(BF16))
        fs.append(jnp.concatenate(parts, axis=1))
    ys = [jnp.dot(f, wb_scr[...], preferred_element_type=F32) + b_ref[...] for f in fs]
    for s, y in zip(seqs, ys):
        y_ref[s, :] = x_ref[s, :] + g1_ref[0] * y


def _fnet_mix(x, seq, layer, norm_g, mod, tokens_per_cond, w, b):
    n_tok = x.shape[0]
    tm = seq * max(1, min(FNET_STEP_TOKENS, tokens_per_cond) // seq)
    cond_of_tile = lambda i: (i * tm) // tokens_per_cond
    ct, st = _dft_mats(seq)
    cc, sc = _dft_mats(FNET_GROUP_DIM)
    cs = np.concatenate([cc, -sc], axis=0)
    const = lambda i: (0, 0)
    ct, st, cs = (jnp.asarray(m, F32).astype(BF16) for m in (ct, st, cs))
    return pl.pallas_call(
        _fnet_kernel,
        grid=(n_tok // tm,),
        in_specs=[pl.BlockSpec((tm, D_MODEL), lambda i: (i, 0)),
                  pl.BlockSpec((1, D_MODEL), const),
                  _mod_spec(1, cond_of_tile), _mod_spec(0, cond_of_tile), _mod_spec(2, cond_of_tile),
                  pl.BlockSpec((seq, seq), const),
                  pl.BlockSpec((seq, seq), const),
                  pl.BlockSpec((2 * FNET_GROUP_DIM, FNET_GROUP_DIM), const),
                  pl.BlockSpec((None, D_MODEL, D_MODEL), lambda i: (layer, 0, 0)),
                  pl.BlockSpec((1, D_MODEL), const)],
        out_specs=pl.BlockSpec((tm, D_MODEL), lambda i: (i, 0)),
        out_shape=jax.ShapeDtypeStruct((n_tok, D_MODEL), F32),
        scratch_shapes=[pltpu.VMEM((D_MODEL, D_MODEL), BF16)],
        compiler_params=_params("arbitrary"),
        name="fnet",
    )(x, norm_g.reshape(1, D_MODEL), mod, mod, mod,
      ct, st, cs, w, b.reshape(1, D_MODEL))


def _shift_rows(x, delta):
    n = x.shape[0]
    return pltpu.roll(x, (-delta) % n, 0)


def _dwconv_tokens(g, cw_ref, seq, rows):
    tm = g.shape[0]
    width = seq // rows
    assert seq & (seq - 1) == 0 and width & (width - 1) == 0
    t = lax.broadcasted_iota(jnp.int32, (tm, 1), 0)
    pos = t & (seq - 1)
    col = pos & (width - 1)
    row = pos >> (width.bit_length() - 1)
    g_cols = (_shift_rows(g, -1) * (col >= 1).astype(F32), g,
              _shift_rows(g, 1) * (col <= width - 2).astype(F32))
    out = None
    for di in (-1, 0, 1):
        if rows == 1 and di != 0:
            continue
        r = sum(g_cols[dj] * cw_ref[pl.ds(3 * (di + 1) + dj, 1), :] for dj in range(3))
        if di != 0:
            ok = (row + di >= 0) & (row + di <= rows - 1)
            r = _shift_rows(r, di * width) * ok.astype(F32)
        out = r if out is None else out + r
    return out


def _ffn_kernel(x_ref, ng_ref, sc_ref, sh_ref, g2_ref, wa_ref, wg_ref, cw_ref, cb_ref, wd_ref, nf_ref,
                y_ref, h_scr, acc_scr, *, seq, rows, final_norm):
    j = pl.program_id(1)

    @pl.when(j == 0)
    def _():
        h_scr[...] = _norm_mod(x_ref[...], ng_ref[...], sc_ref[0], sh_ref[0]).astype(BF16)
        acc_scr[...] = jnp.zeros_like(acc_scr)

    tm = h_scr.shape[0]
    sub = tm // FFN_SUB_TILES
    parts = [pl.ds(r, sub) for r in range(0, tm, sub)]
    wa, wg, wd = (w[...].astype(BF16) for w in (wa_ref, wg_ref, wd_ref))
    gs = [jnp.dot(h_scr[p, :], wg, preferred_element_type=F32) for p in parts]
    as_ = [jnp.dot(h_scr[p, :], wa, preferred_element_type=F32) for p in parts]
    per_seq = max(seq // sub, 1)
    convs = []
    for i in range(0, len(parts), per_seq):
        g = gs[i] if per_seq == 1 else jnp.concatenate(gs[i:i + per_seq], axis=0)
        c = _dwconv_tokens(g, cw_ref, seq, rows) + cb_ref[...]
        convs += [c[k * sub:(k + 1) * sub] for k in range(per_seq)]
    acts = [(jax.nn.silu(c) * a).astype(BF16) for c, a in zip(convs, as_)]
    for p, act in zip(parts, acts):
        acc_scr[p, :] += jnp.dot(act, wd, preferred_element_type=F32)

    @pl.when(j == pl.num_programs(1) - 1)
    def _():
        y = x_ref[...] + g2_ref[0] * acc_scr[...]
        if final_norm:
            y = _rms(y, nf_ref[...])
        y_ref[...] = y


def _conv_ffn(x, seq, rows, layer, norm_g, mod, cond_of_tile, w_up, conv_w, conv_b, w_down, norm_final, final_norm):
    n_tok = x.shape[0]
    tm, tf = FFN_TOKEN_TILE, FFN_FF_TILE
    n_ff_tiles = D_FF // tf
    depth = w_up.shape[0]
    return pl.pallas_call(
        functools.partial(_ffn_kernel, seq=seq, rows=rows, final_norm=final_norm),
        grid=(n_tok // tm, n_ff_tiles),
        in_specs=[pl.BlockSpec((tm, D_MODEL), lambda i, j: (i, 0)),
                  pl.BlockSpec((1, D_MODEL), lambda i, j: (0, 0)),
                  _mod_spec(4, cond_of_tile), _mod_spec(3, cond_of_tile), _mod_spec(5, cond_of_tile),
                  pl.BlockSpec((None, D_MODEL, tf), lambda i, j: (layer, 0, j)),
                  pl.BlockSpec((None, D_MODEL, tf), lambda i, j: (layer, 0, n_ff_tiles + j)),
                  pl.BlockSpec((None, 9, tf), lambda i, j: (layer, 0, j)),
                  pl.BlockSpec((None, 1, tf), lambda i, j: (layer, 0, j)),
                  pl.BlockSpec((None, tf, D_MODEL), lambda i, j: (layer, j, 0)),
                  pl.BlockSpec((1, D_MODEL), lambda i, j: (0, 0))],
        out_specs=pl.BlockSpec((tm, D_MODEL), lambda i, j: (i, 0)),
        out_shape=jax.ShapeDtypeStruct((n_tok, D_MODEL), F32),
        scratch_shapes=[pltpu.VMEM((tm, D_MODEL), BF16), pltpu.VMEM((tm, D_MODEL), F32)],
        compiler_params=_params("parallel", "arbitrary"),
        name="conv_ffn",
    )(x, norm_g.reshape(1, D_MODEL), mod, mod, mod, w_up, w_up, conv_w.reshape(depth, 9, D_FF),
      conv_b.reshape(depth, 1, D_FF), w_down, norm_final.reshape(1, D_MODEL))


GroupMasks = collections.namedtuple("GroupMasks", "r c same incl strict")


def _chunk_masks():
    r = lax.broadcasted_iota(jnp.int32, (CHUNK, CHUNK), 0)
    c = lax.broadcasted_iota(jnp.int32, (CHUNK, CHUNK), 1)
    return (r >= c, r <= c)


def _group_masks(group=GROUP):
    r = lax.broadcasted_iota(jnp.int32, (group, group), 0)
    c = lax.broadcasted_iota(jnp.int32, (group, group), 1)
    same = (r >> LOG_CHUNK) == (c >> LOG_CHUNK)
    return GroupMasks(r, c, same, (same & (r >= c), same & (r <= c)), (same & (r > c), same & (r < c)))


def _gate_layouts(gt, batch, n_heads, hp, group=GROUP):
    seq = gt.shape[1] // batch
    n_groups, n_chunks = seq // group, seq // CHUNK
    assert seq % group == 0 and n_groups <= 8
    g = jnp.transpose(gt.reshape(N_DIR, 2, n_heads, batch, seq), (3, 2, 0, 1, 4))
    rows = g.reshape(g.shape[:4] + (n_groups, group))
    rows = jnp.pad(rows, ((0, 0),) * 4 + ((0, 8 - n_groups), (0, 0)))
    cols = jnp.swapaxes(g.reshape(batch, n_heads // hp, hp * N_DIR * 2 * n_chunks, CHUNK), -1, -2)
    return rows, cols


def _col_gate(gcol_ref, hh, d, kind, n_chunks):
    start = ((hh * N_DIR + d) * 2 + kind) * n_chunks
    return gcol_ref[0, 0, :, start:start + n_chunks]


def _split3(x):
    x1 = x.astype(BF16)
    r1 = x - x1.astype(F32)
    x2 = r1.astype(BF16)
    return x1, x2, (r1 - x2.astype(F32)).astype(BF16)


def _dot_mask_rhs(x, mask):
    m = x.shape[0]
    y = jnp.dot(jnp.concatenate(_split3(x), axis=0), jnp.where(mask, 1.0, 0.0).astype(BF16),
                preferred_element_type=F32)
    return y[:m] + y[m:2 * m] + y[2 * m:]


def _dot_mask_lhs(mask, x):
    mb = jnp.where(mask, 1.0, 0.0).astype(BF16)
    y1, y2, y3 = (jnp.dot(mb, p, preferred_element_type=F32) for p in _split3(x))
    return y1 + y2 + y3


def _cumsum_rows(row, d, gm):
    return _dot_mask_rhs(row, gm.incl[1 - d])


def _cumsum_cols(col, d, cm):
    return _dot_mask_lhs(cm[d], col)


def _store_cols(dst, col, n_chunks):
    for n in range(n_chunks):
        dst[n * CHUNK:(n + 1) * CHUNK, :] = jnp.broadcast_to(col[:, n:n + 1], (CHUNK, 128))


def _wide(x, width=None):
    reps = (GROUP if width is None else width) // x.shape[1]
    return x if reps == 1 else jnp.concatenate([x] * reps, axis=1)


def _rows(i, size):
    if isinstance(i, int):
        return pl.ds(i * size, size)
    return pl.ds(pl.multiple_of(i * size, size), size)


def _loop(n, body, init):
    if n == 1:
        return body(0, init)
    return lax.fori_loop(0, n, body, init)


def _conv_silu(x, w_ref):
    n = x.shape[0]
    k = w_ref.shape[0]
    t = lax.broadcasted_iota(jnp.int32, (n, 1), 0)
    acc = None
    for j in range(k):
        delta = j - k // 2
        if delta == 0:
            term = x
        else:
            ok = (t + delta >= 0) & (t + delta <= n - 1)
            term = _shift_rows(x, delta) * ok.astype(F32)
        term = term * w_ref[pl.ds(j, 1), :]
        acc = term if acc is None else acc + term
    return jax.nn.silu(acc)


def _l2norm(x):
    return x * lax.rsqrt(jnp.sum(x * x, axis=-1, keepdims=True) + EPS)


def _blockdiag_tri_inverse(mats, gm):
    dot16 = lambda x, y: jnp.dot(x, y, preferred_element_type=F32).astype(BF16)
    base = (gm.r >> 3) == (gm.c >> 3)
    eye = jnp.where(gm.r == gm.c, 1.0, 0.0)
    zero16 = jnp.zeros((), BF16)
    a16s = [a.astype(BF16) for a in mats]
    invs = [eye - jnp.where(base, a, 0.0) for a in mats]
    ps = [jnp.where(base, -a16, zero16) for a16 in a16s]
    for _ in range(2):
        ps = [dot16(p, p) for p in ps]
        invs = [inv + jnp.dot(inv.astype(BF16), p, preferred_element_type=F32) for inv, p in zip(invs, ps)]
    for s in (3, 4, 5):
        join = ((gm.r >> s) ^ (gm.c >> s)) == 1
        inv16s = [inv.astype(BF16) for inv in invs]
        tmp = [dot16(jnp.where(join, a16, zero16), inv16) for a16, inv16 in zip(a16s, inv16s)]
        invs = [inv - jnp.dot(inv16, t, preferred_element_type=F32) for inv, inv16, t in zip(invs, inv16s, tmp)]
    return invs


def _gdn_kernel(*refs, seq, hp, zero_init, emit_state):
    alog_ref, dtb_ref, q_ref, k_ref, v_ref, cwq_ref, cwk_ref, cwv_ref, grow_ref, gcol_ref = refs[:10]
    pos = 10
    s0_ref = None
    if not zero_init:
        s0_ref = refs[pos]
        pos += 1
    o_ref = refs[pos]
    pos += 1
    sfin_ref = None
    if emit_state:
        sfin_ref = refs[pos]
        pos += 1
    (q_scr, k_scr, v_scr, gcr_scr, gcb_scr, bb_scr, gl_scr,
     u_scr, w_scr, qd_scr, kd_scr, qkd_scr) = refs[pos:]

    n_chunks = seq // CHUNK
    grp = GDN_GROUP
    cm = _chunk_masks()
    gm = _group_masks(grp)
    head_dirs = [(hh, d) for hh in range(hp) for d in range(N_DIR)]
    lanes = lambda hh: slice(hh * DN_DK, (hh + 1) * DN_DK)

    q_scr[...] = _conv_silu(q_ref[...].astype(F32), cwq_ref)
    k_scr[...] = _conv_silu(k_ref[...].astype(F32), cwk_ref)
    v_scr[...] = _conv_silu(v_ref[...].astype(F32), cwv_ref)
    for hh in range(hp):
        q_scr[:, lanes(hh)] = _l2norm(q_scr[:, lanes(hh)]) * (DN_DK ** -0.5)
        k_scr[:, lanes(hh)] = _l2norm(k_scr[:, lanes(hh)])

    for hh, d in head_dirs:
        head = pl.program_id(1) * hp + hh
        neg_a = -jnp.exp(jnp.full((1, 1), alog_ref[d, head], F32))
        dtb = dtb_ref[d, head]
        gcr_scr[hh, d] = _cumsum_rows(neg_a * _softplus(grow_ref[0, hh, d, 0] + dtb), d, gm)
        gc_col = _cumsum_cols(neg_a * _softplus(_col_gate(gcol_ref, hh, d, 0, n_chunks) + dtb), d, cm)
        _store_cols(gcb_scr.at[hh, d], gc_col, n_chunks)
        _store_cols(bb_scr.at[hh, d], jax.nn.sigmoid(_col_gate(gcol_ref, hh, d, 1, n_chunks)), n_chunks)
        last = CHUNK - 1 if d == 0 else 0
        _store_cols(gl_scr.at[hh, d], jnp.broadcast_to(gc_col[last:last + 1, :], gc_col.shape), n_chunks)

    gps = min(GDN_GROUPS_PER_STEP, seq // grp)

    def group_step(it, carry):
        gis = [it * gps + k for k in range(gps)]
        rows = [_rows(gi, grp) for gi in gis]
        for h0 in range(0, hp, GDN_HEADS_PER_PASS):
            heads = range(h0, min(h0 + GDN_HEADS_PER_PASS, hp))
            kqs = {}
            for k, r in enumerate(rows):
                for hh in heads:
                    kg, qg = k_scr[r, lanes(hh)], q_scr[r, lanes(hh)]
                    kqs[k, hh] = _dot_nt(jnp.concatenate([kg, qg], axis=0), kg)
            chains = [(k, hh, d) for k in range(len(gis)) for hh in heads for d in range(N_DIR)]
            mats = []
            for k, hh, d in chains:
                gcb = gcb_scr[hh, d, rows[k], :]
                gcr = gcr_scr[hh, d, pl.ds(gis[k], 1), :]
                decay = jnp.exp(jnp.where(gm.incl[d], _wide(gcb, grp) - gcr, -jnp.inf))
                mats.append(jnp.where(gm.strict[d],
                                      kqs[k, hh][:grp] * _wide(bb_scr[hh, d, rows[k], :], grp) * decay, 0.0))
                qkd = kqs[k, hh][grp:] * decay
                qkd_scr[hh, d, rows[k], :] = sum(qkd[:, n * CHUNK:(n + 1) * CHUNK] for n in range(grp // CHUNK))
            t_invs = _blockdiag_tri_inverse(mats, gm)
            for (k, hh, d), t_inv in zip(chains, t_invs):
                r = rows[k]
                qg, kg, vg = q_scr[r, lanes(hh)], k_scr[r, lanes(hh)], v_scr[r, lanes(hh)]
                gcb = gcb_scr[hh, d, r, :]
                bb = bb_scr[hh, d, r, :]
                e_gc = jnp.exp(gcb)
                uw = _dot(t_inv, jnp.concatenate([vg * bb, kg * bb * e_gc], axis=1))
                u_scr[hh, d, r, :] = uw[:, :DN_DV]
                w_scr[hh, d, r, :] = uw[:, DN_DV:]
                qd_scr[hh, d, r, :] = qg * e_gc
                kd_scr[hh, d, r, :] = kg * jnp.exp(gl_scr[hh, d, r, :] - gcb)
        return carry

    _loop(seq // (grp * gps), group_step, 0)

    o_acc = q_scr
    o_acc[...] = jnp.zeros_like(o_acc)

    def chunk_step(n, carry):
        idxs = [n if d == 0 else n_chunks - 1 - n for _, d in head_dirs]
        rows = [_rows(idx, CHUNK) for idx in idxs]
        wqs = [_dot(jnp.concatenate([w_scr[hh, d, r, :], qd_scr[hh, d, r, :]], axis=0), s)
               for (hh, d), r, s in zip(head_dirs, rows, carry)]
        v_news = [u_scr[hh, d, r, :] - wq[:CHUNK] for (hh, d), r, wq in zip(head_dirs, rows, wqs)]
        o_ns = [wq[CHUNK:] + _dot(qkd_scr[hh, d, r, :], v_new)
                for (hh, d), r, wq, v_new in zip(head_dirs, rows, wqs, v_news)]
        new = []
        for (hh, d), idx, r, s, v_new in zip(head_dirs, idxs, rows, carry, v_news):
            s_decay = jnp.exp(gl_scr[hh, d, pl.ds(idx * CHUNK, 1), :])
            new.append(s * s_decay + _dot_tn(kd_scr[hh, d, r, :], v_new))
        for (hh, d), r, o_n in zip(head_dirs, rows, o_ns):
            o_acc[r, lanes(hh)] += o_n
        return tuple(new)

    if zero_init:
        init = (jnp.zeros((DN_DK, DN_DV), F32),) * len(head_dirs)
    else:
        init = tuple(s0_ref[0, d, hh] for hh, d in head_dirs)
    fin = lax.fori_loop(0, n_chunks, chunk_step, init)
    o_ref[...] = o_acc[...].astype(o_ref.dtype)
    if emit_state:
        for i, (hh, d) in enumerate(head_dirs):
            sfin_ref[0, d, hh] = fin[i]


def _gdn_scan(main, gt, batch, seq, conv_w, a_log, dt_bias, s0, emit_state):
    n_tok = batch * seq
    nh = DN_HEADS
    hp = min(nh, max(2, GDN_STEP_POSITIONS // seq))
    n_hb = nh // hp
    assert seq % (GDN_GROUP * min(GDN_GROUPS_PER_STEP, seq // GDN_GROUP)) == 0
    grow, gcol = _gate_layouts(gt, batch, nh, hp, GDN_GROUP)
    zero_init = s0 is None
    smem = pl.BlockSpec(memory_space=pltpu.SMEM)
    qkv_spec = lambda off: pl.BlockSpec((seq, hp * DN_DK), lambda b, h: (b, off + h))
    cw_spec = lambda off: pl.BlockSpec((DN_CONV, hp * DN_DK), lambda b, h: (0, off + h))
    state_spec = pl.BlockSpec((1, N_DIR, hp, DN_DK, DN_DV), lambda b, h: (b, 0, h, 0, 0))
    in_specs = [smem, smem, qkv_spec(0), qkv_spec(n_hb), qkv_spec(2 * n_hb),
                cw_spec(0), cw_spec(n_hb), cw_spec(2 * n_hb),
                pl.BlockSpec((1, hp) + grow.shape[2:], lambda b, h: (b, h, 0, 0, 0, 0)),
                pl.BlockSpec((1, 1) + gcol.shape[2:], lambda b, h: (b, h, 0, 0))]
    args = [a_log, dt_bias, main, main, main, conv_w, conv_w, conv_w, grow, gcol]
    if not zero_init:
        in_specs.append(state_spec)
        args.append(s0)
    out_specs = [pl.BlockSpec((seq, hp * DN_DV), lambda b, h: (b, h))]
    out_shape = [jax.ShapeDtypeStruct((n_tok, nh * DN_DV), BF16)]
    if emit_state:
        out_specs.append(state_spec)
        out_shape.append(jax.ShapeDtypeStruct((batch, N_DIR, nh, DN_DK, DN_DV), F32))
    per_dir = lambda width: pltpu.VMEM((hp, N_DIR, seq, width), F32)
    qkv_scr = pltpu.VMEM((seq, hp * DN_DK), F32)
    outs = pl.pallas_call(
        functools.partial(_gdn_kernel, seq=seq, hp=hp, zero_init=zero_init, emit_state=emit_state),
        grid=(batch, n_hb),
        in_specs=in_specs, out_specs=out_specs, out_shape=out_shape,
        scratch_shapes=[qkv_scr, qkv_scr, qkv_scr,
                        pltpu.VMEM((hp, N_DIR, 8, GDN_GROUP), F32),
                        per_dir(128), per_dir(128), per_dir(128),
                        per_dir(DN_DV), per_dir(DN_DK), per_dir(DN_DK), per_dir(DN_DK), per_dir(CHUNK)],
        compiler_params=_params("parallel", "parallel"),
        name="gdn_scan",
    )(*args)
    return outs[0], (outs[1] if emit_state else None)


ML_AUG = 2 * ML_DV


def _mlstm_kernel(*refs, seq, hp, zero_init, emit_state):
    bi_ref, bf_ref, q_ref, k_ref, v_ref, kt_ref, grow_ref, gcol_ref = refs[:8]
    pos = 8
    c0_ref = m0_ref = None
    if not zero_init:
        c0_ref, m0_ref = refs[pos:pos + 2]
        pos += 2
    o_ref = refs[pos]
    pos += 1
    cfin_ref = nfin_ref = mfin_ref = None
    if emit_state:
        cfin_ref, nfin_ref, mfin_ref = refs[pos:pos + 3]
        pos += 3
    bcr_scr, lir_scr, kwr_scr, bcb_scr, mi_scr, dec_scr, dc_scr, cin_scr = refs[pos:]

    n_chunks = seq // CHUNK
    n_groups = seq // GROUP
    head0 = pl.program_id(1) * hp
    cm = _chunk_masks()
    gm = _group_masks()
    ones_col = jnp.where(lax.broadcasted_iota(jnp.int32, (GROUP, ML_DV), 1) == 0, 1.0, 0.0)
    head_dirs = [(hh, d) for hh in range(hp) for d in range(N_DIR)]

    m_fin = {}
    row_group = lax.broadcasted_iota(jnp.int32, (8, GROUP), 0)
    row_chunk = lax.broadcasted_iota(jnp.int32, (8, GROUP), 1) >> LOG_CHUNK
    for hh, d in head_dirs:
        b_i = bi_ref[d, head0 + hh]
        b_f = bf_ref[d, head0 + hh]
        li_row = grow_ref[0, hh, d, 0] + b_i
        lf_row = -_softplus(-(grow_ref[0, hh, d, 1] + b_f))
        bc_row = _cumsum_rows(lf_row, d, gm)
        lir_scr[hh, d] = li_row
        bcr_scr[hh, d] = bc_row
        w_row = _dot_mask_rhs(lf_row, gm.same) - bc_row + li_row
        m_out_row = jnp.zeros((8, GROUP), F32)
        li_col = _col_gate(gcol_ref, hh, d, 0, n_chunks) + b_i
        bc_col = _cumsum_cols(-_softplus(-(_col_gate(gcol_ref, hh, d, 1, n_chunks) + b_f)), d, cm)
        last = CHUNK - 1 if d == 0 else 0
        b_last = bc_col[last:last + 1, :]
        w_col = b_last - bc_col + li_col
        w_max = jnp.max(w_col, axis=0, keepdims=True)
        m = jnp.zeros((1, 1), F32) if zero_init else m0_ref[0, d, hh][:, 0:1]
        for step in range(n_chunks):
            n = step if d == 0 else n_chunks - 1 - step
            sl = slice(n * CHUNK, (n + 1) * CHUNK)
            m_new = jnp.maximum(b_last[:, n:n + 1] + m, w_max[:, n:n + 1])
            bcb_scr[hh, d, sl, :] = jnp.broadcast_to(bc_col[:, n:n + 1], (CHUNK, 128))
            mi_scr[hh, d, sl, :] = jnp.broadcast_to(m, (CHUNK, 128))
            dec_scr[hh, d, n:n + 1, :] = jnp.broadcast_to(jnp.exp(b_last[:, n:n + 1] + m - m_new), (1, 128))
            in_chunk = (row_group == n // CHUNKS_PER_GROUP) & (row_chunk == n % CHUNKS_PER_GROUP)
            m_out_row = jnp.where(in_chunk, m_new, m_out_row)
            m = m_new
        m_fin[hh, d] = m
        kwr_scr[hh, d] = jnp.exp(w_row - m_out_row)

    def load_kv(rows, hh):
        kg = k_ref[rows, hh * ML_DQK:(hh + 1) * ML_DQK].astype(F32)
        v_aug = jnp.concatenate([v_ref[rows, hh * ML_DV:(hh + 1) * ML_DV].astype(F32), ones_col], axis=1)
        return kg, v_aug

    def delta_step(gi, carry):
        rows = _rows(gi, GROUP)
        v_augs = [load_kv(rows, hh)[1] for hh in range(hp)]
        kt = kt_ref[gi]
        lhs = []
        for hh, d in head_dirs:
            kw_t = kt[hh * ML_DQK:(hh + 1) * ML_DQK, :] * kwr_scr[hh, d, pl.ds(gi, 1), :]
            lhs.append(jnp.where(gm.same, jnp.concatenate([kw_t] * CHUNKS_PER_GROUP, axis=0), 0.0))
        for (hh, d), kw in zip(head_dirs, lhs):
            dc_scr[hh, d, rows, :] = _dot(kw, v_augs[hh])
        return carry

    _loop(n_groups, delta_step, 0)

    def prefix_step(n, carry):
        new = []
        for (hh, d), c_aug in zip(head_dirs, carry):
            idx = n if d == 0 else n_chunks - 1 - n
            rows = _rows(idx, CHUNK)
            cin_scr[hh, d, rows, :] = c_aug
            new.append(c_aug * _wide(dec_scr[hh, d, pl.ds(idx, 1), :], ML_AUG) + dc_scr[hh, d, rows, :])
        return tuple(new)

    if zero_init:
        init = (jnp.zeros((ML_DQK, ML_AUG), F32),) * len(head_dirs)
    else:
        init = tuple(c0_ref[0, d, hh] for hh, d in head_dirs)
    fin = lax.fori_loop(0, n_chunks, prefix_step, init)

    def out_step(gi, carry):
        rows = _rows(gi, GROUP)
        kvs = [load_kv(rows, hh) for hh in range(hp)]
        qgs = [q_ref[rows, hh * ML_DQK:(hh + 1) * ML_DQK].astype(F32) * (ML_DQK ** -0.5) for hh in range(hp)]
        qks = [_dot_nt(qg, kv[0]) for qg, kv in zip(qgs, kvs)]
        chunk = lambda n: slice(n * CHUNK, (n + 1) * CHUNK)
        inters = []
        for hh, d in head_dirs:
            c_in = cin_scr[hh, d, rows, :]
            inters.append(jnp.concatenate(
                [_dot(qgs[hh][chunk(n)], c_in[chunk(n)]) for n in range(CHUNKS_PER_GROUP)], axis=0))
        m_ts, b_ms, ss = [], [], []
        for hh, d in head_dirs:
            b_colb = bcb_scr[hh, d, rows, :]
            b_m = b_colb[:, 0:1] + mi_scr[hh, d, rows, :][:, 0:1]
            d_log = jnp.where(gm.incl[d], _wide(b_colb) - bcr_scr[hh, d, pl.ds(gi, 1), :]
                              + lir_scr[hh, d, pl.ds(gi, 1), :], -jnp.inf)
            m_t = jnp.maximum(b_m, jnp.max(d_log, axis=-1, keepdims=True))
            m_ts.append(m_t)
            b_ms.append(b_m)
            ss.append(qks[hh] * jnp.exp(d_log - m_t))
        intras = [_dot(s, kvs[hh][1]) for (hh, d), s in zip(head_dirs, ss)]
        hs = []
        for inter, intra, m_t, b_m in zip(inters, intras, m_ts, b_ms):
            num = jnp.exp(b_m - m_t) * inter + intra
            den = num[:, ML_DV:ML_DV + 1]
            hs.append(num[:, :ML_DV] / jnp.maximum(jnp.abs(den), jnp.exp(-m_t)))
        for hh in range(hp):
            o_ref[rows, hh * ML_DV:(hh + 1) * ML_DV] = (hs[N_DIR * hh] + hs[N_DIR * hh + 1]).astype(o_ref.dtype)
        return carry

    _loop(n_groups, out_step, 0)

    if emit_state:
        for i, (hh, d) in enumerate(head_dirs):
            cfin_ref[0, d, hh] = fin[i][:, :ML_DV]
            nfin_ref[0, d, hh] = fin[i][:, ML_DV:]
            mfin_ref[0, d, hh] = jnp.broadcast_to(m_fin[hh, d], (1, 128))


def _mlstm_scan(main, kt, gt, batch, seq, b_i, b_f, state0, emit_state):
    n_tok = batch * seq
    n_groups = seq // GROUP
    n_chunks = seq // CHUNK
    nh = ML_HEADS
    hp = min(nh, max(2, SCAN_STEP_POSITIONS // seq))
    n_hb = nh // hp
    grow, gcol = _gate_layouts(gt, batch, nh, hp)
    zero_init = state0 is None
    smem = pl.BlockSpec(memory_space=pltpu.SMEM)
    k_off = (nh * ML_DQK) // (hp * ML_DQK)
    v_off = (2 * nh * ML_DQK) // (hp * ML_DV)
    c_spec = pl.BlockSpec((1, N_DIR, hp, ML_DQK, ML_AUG), lambda b, p: (b, 0, p, 0, 0))
    m_spec = pl.BlockSpec((1, N_DIR, hp, 1, 128), lambda b, p: (b, 0, p, 0, 0))
    in_specs = [smem, smem,
                pl.BlockSpec((seq, hp * ML_DQK), lambda b, p: (b, p)),
                pl.BlockSpec((seq, hp * ML_DQK), lambda b, p: (b, k_off + p)),
                pl.BlockSpec((seq, hp * ML_DV), lambda b, p: (b, v_off + p)),
                pl.BlockSpec((n_groups, hp * ML_DQK, GROUP), lambda b, p: (b, p, 0)),
                pl.BlockSpec((1, hp) + grow.shape[2:], lambda b, p: (b, p, 0, 0, 0, 0)),
                pl.BlockSpec((1, 1) + gcol.shape[2:], lambda b, p: (b, p, 0, 0))]
    args = [b_i, b_f, main, main, main, kt, grow, gcol]
    if not zero_init:
        in_specs += [c_spec, m_spec]
        args += list(state0)
    out_specs = [pl.BlockSpec((seq, hp * ML_DV), lambda b, p: (b, p))]
    out_shape = [jax.ShapeDtypeStruct((n_tok, nh * ML_DV), BF16)]
    if emit_state:
        half_spec = pl.BlockSpec((1, N_DIR, hp, ML_DQK, ML_DV), lambda b, p: (b, 0, p, 0, 0))
        out_specs += [half_spec, half_spec, m_spec]
        out_shape += [jax.ShapeDtypeStruct((batch, N_DIR, nh, ML_DQK, ML_DV), F32),
                      jax.ShapeDtypeStruct((batch, N_DIR, nh, ML_DQK, ML_AUG - ML_DV), F32),
                      jax.ShapeDtypeStruct((batch, N_DIR, nh, 1, 128), F32)]
    per_hd = lambda rows, width: pltpu.VMEM((hp, N_DIR, rows, width), F32)
    outs = pl.pallas_call(
        functools.partial(_mlstm_kernel, seq=seq, hp=hp, zero_init=zero_init, emit_state=emit_state),
        grid=(batch, n_hb),
        in_specs=in_specs, out_specs=out_specs, out_shape=out_shape,
        scratch_shapes=[per_hd(8, GROUP), per_hd(8, GROUP), per_hd(8, GROUP),
                        per_hd(seq, 128), per_hd(seq, 128),
                        per_hd(max(n_chunks, 8), 128),
                        per_hd(seq, ML_AUG), per_hd(seq, ML_AUG)],
        compiler_params=_params("parallel", "parallel"),
        name="mlstm_scan",
    )(*args)
    return outs[0], (tuple(outs[1:4]) if emit_state else None)


def _trunk(x, batch, seq, rows, mods, tokens_per_cond, st_d, st_ml, emit_state, p):
    depth = p["w_ada"].shape[0]
    tile_cond = lambda tile: (lambda i: (i * tile) // tokens_per_cond)
    new_d = new_ml = None
    for layer in range(depth):
        mod = mods[layer]
        kind, j = layer % 3, layer // 3
        if kind == 0:
            x = _fnet_mix(x, seq, j, p["norm_mix"][layer], mod, tokens_per_cond, p["fnet_w"], p["fnet_b"][j])
        elif kind == 1:
            n_main = DN_HEADS * (2 * DN_DK + 2 * DN_DV)
            main, gt = _norm_proj(x, p["norm_mix"][layer], mod, tokens_per_cond, p["dn_w_in"], j, n_main)
            s0 = None if st_d is None else st_d[:, j]
            o, sfin = _gdn_scan(main, gt, batch, seq, p["dn_conv_w"][j], p["dn_a_log"][j], p["dn_dt_bias"][j],
                                s0, emit_state)
            if emit_state:
                new_d = sfin
            x = _gated_out(x, o, main, 3, p["dn_norm"][j], mod, tile_cond(OUT_TOKEN_TILE), p["dn_w_out"][j],
                           DN_HEADS, jax.nn.silu)
        else:
            n_main = 2 * ML_HEADS * ML_DQK + 2 * ML_HEADS * ML_DV
            k_tile = (ML_HEADS * ML_DQK) // PROJ_COL_TILE
            assert ML_HEADS * ML_DQK == PROJ_COL_TILE
            main, gt, kt = _norm_proj(x, p["norm_mix"][layer], mod, tokens_per_cond, p["ml_w_in"], j, n_main,
                                      t_tile=k_tile)
            state0 = None
            if st_ml is not None:
                c0, n0, m0 = (s[:, j] for s in st_ml)
                pad = jnp.zeros(c0.shape[:-1] + (ML_AUG - ML_DV - 1,), F32)
                c_aug0 = jnp.concatenate([c0, n0[..., None], pad], axis=-1)
                m0b = jnp.broadcast_to(m0[..., None, None], m0.shape + (1, 128))
                state0 = (c_aug0, m0b)
            o, fin = _mlstm_scan(main, kt, gt, batch, seq, p["ml_b_i"][j], p["ml_b_f"][j], state0, emit_state)
            if emit_state:
                new_ml = (fin[0], fin[1][..., 0], fin[2][..., 0, 0])
            x = _gated_out(x, o, main, 2, p["ml_norm"][j], mod, tile_cond(OUT_TOKEN_TILE), p["ml_w_out"][j],
                           ML_HEADS, jax.nn.sigmoid)
        x = _conv_ffn(x, seq, rows, layer, p["norm_ffn"][layer], mod, tile_cond(FFN_TOKEN_TILE), p["ffn_w_up"],
                      p["ffn_conv_w"], p["ffn_conv_b"], p["ffn_w_down"], p["norm_final"],
                      final_norm=(layer == depth - 1))
    return x, new_d, new_ml


def kernel(x_prompt, x_sample, state_delta, state_mlstm_c, state_mlstm_n, state_mlstm_m, c, c_ctx, w_ada, b_ada, norm_mix, norm_ffn, norm_final, ffn_w_up, ffn_conv_w, ffn_conv_b, ffn_w_down, fnet_w, fnet_b, dn_w_in, dn_conv_w, dn_a_log, dn_dt_bias, dn_norm, dn_w_out, ml_w_in, ml_b_i, ml_b_f, ml_norm, ml_w_out):
    p = dict(w_ada=w_ada, norm_mix=norm_mix, norm_ffn=norm_ffn, norm_final=norm_final, ffn_w_up=ffn_w_up,
             ffn_conv_w=ffn_conv_w, ffn_conv_b=ffn_conv_b, ffn_w_down=ffn_w_down, fnet_w=fnet_w, fnet_b=fnet_b,
             dn_w_in=dn_w_in, dn_conv_w=dn_conv_w, dn_a_log=dn_a_log, dn_dt_bias=dn_dt_bias, dn_norm=dn_norm,
             dn_w_out=dn_w_out, ml_w_in=ml_w_in, ml_b_i=ml_b_i, ml_b_f=ml_b_f, ml_norm=ml_norm, ml_w_out=ml_w_out)
    b_ctx, t_ctx, _ = x_prompt.shape
    b_smp, t_smp, _ = x_sample.shape
    depth = w_ada.shape[0]

    cond8 = jnp.concatenate([c_ctx[None, :], c, jnp.zeros((8 - 1 - b_smp, D_MODEL), F32)], axis=0)
    mods = _adaln_all(cond8, w_ada, b_ada)
    mods_ctx = [mods[l, 0:1].reshape(1, 1, -1) for l in range(depth)]
    mods_smp = [mods[l, 1:1 + b_smp].reshape(b_smp, 1, -1) for l in range(depth)]

    y_ctx, new_d, new_ml = _trunk(x_prompt.reshape(b_ctx * t_ctx, D_MODEL), b_ctx, t_ctx, 1, mods_ctx,
                                  b_ctx * t_ctx, None, None, True, p)
    y_smp, _, _ = _trunk(x_sample.reshape(b_smp * t_smp, D_MODEL), b_smp, t_smp, t_smp // GRID_W, mods_smp,
                         t_smp, state_delta, (state_mlstm_c, state_mlstm_n, state_mlstm_m), False, p)

    new_c, new_n, new_m = new_ml
    return (y_ctx.reshape(b_ctx, t_ctx, D_MODEL), y_smp.reshape(b_smp, t_smp, D_MODEL),
            new_d[:, None], new_c[:, None], new_n[:, None], new_m[:, None])
```

```python
import collections
import functools

import numpy as np
import jax
import jax.numpy as jnp
from jax import lax
from jax.experimental import pallas as pl
from jax.experimental.pallas import tpu as pltpu

F32 = jnp.float32
BF16 = jnp.bfloat16

D_MODEL = 1024
EPS = 1e-6
N_DIR = 2
CHUNK = 64
LOG_CHUNK = 6
GROUP = 256
CHUNKS_PER_GROUP = GROUP // CHUNK
GDN_GROUP = 128
GDN_GROUPS_PER_STEP = 4
GDN_HEADS_PER_PASS = 4
GDN_STEP_POSITIONS = 2048
FNET_GROUP_DIM = 256
DN_HEADS, DN_DK, DN_DV, DN_CONV = 8, 128, 128, 5
ML_HEADS, ML_DQK, ML_DV = 8, 64, 128
D_FF = 2816
GRID_W = 64

FNET_STEP_TOKENS = 1024
FFN_TOKEN_TILE = 1024
FFN_FF_TILE = 256
FFN_SUB_TILES = 4
PROJ_TOKEN_TILE = 2048
PROJ_COL_TILE = 512
OUT_TOKEN_TILE = 512
ADA_COL_TILE = 1536
SCAN_STEP_POSITIONS = 1024
VMEM_LIMIT = 56 * 1024 * 1024


def _params(*sem):
    return pltpu.CompilerParams(dimension_semantics=sem, vmem_limit_bytes=VMEM_LIMIT)


def _dot(a, b):
    return jnp.dot(a.astype(BF16), b.astype(BF16), preferred_element_type=F32)


def _dot_nt(a, b):
    return lax.dot_general(a.astype(BF16), b.astype(BF16), (((1,), (1,)), ((), ())),
                           preferred_element_type=F32)


def _dot_tn(a, b):
    return lax.dot_general(a.astype(BF16), b.astype(BF16), (((0,), (0,)), ((), ())),
                           preferred_element_type=F32)


def _rms(x, g):
    return x * lax.rsqrt(jnp.mean(x * x, axis=-1, keepdims=True) + EPS) * g


def _norm_mod(x, g, sc, sh):
    return _rms(x, g) * (1.0 + sc) + sh


def _softplus(x):
    return jnp.maximum(x, 0.0) + jnp.log1p(jnp.exp(-jnp.abs(x)))


def _mod_spec(chunk, cond_of_tile):
    return pl.BlockSpec((1, 1, D_MODEL), lambda i, *_: (cond_of_tile(i), 0, chunk))


def _adaln_kernel(c_ref, w_ref, b_ref, o_ref):
    s = jax.nn.silu(c_ref[...])
    o_ref[0] = _dot(s, w_ref[0]) + b_ref[0]


def _adaln_all(cond8, w_ada, b_ada):
    depth = w_ada.shape[0]
    n_out = w_ada.shape[2]
    tn = ADA_COL_TILE
    return pl.pallas_call(
        _adaln_kernel,
        grid=(depth, n_out // tn),
        in_specs=[pl.BlockSpec((8, D_MODEL), lambda l, j: (0, 0)),
                  pl.BlockSpec((1, D_MODEL, tn), lambda l, j: (l, 0, j)),
                  pl.BlockSpec((1, 1, tn), lambda l, j: (l, 0, j))],
        out_specs=pl.BlockSpec((1, 8, tn), lambda l, j: (l, 0, j)),
        out_shape=jax.ShapeDtypeStruct((depth, 8, n_out), F32),
        compiler_params=_params("parallel", "parallel"),
        name="adaln",
    )(cond8, w_ada, b_ada.reshape(depth, 1, n_out))


def _norm_proj_kernel(x_ref, g_ref, sc_ref, sh_ref, w_ref, wg_ref, o_ref, gate_ref, *rest, t_tile):
    h_scr = rest[-1]

    @pl.when(pl.program_id(1) == 0)
    def _():
        h = _norm_mod(x_ref[...], g_ref[...], sc_ref[0], sh_ref[0]).astype(BF16)
        h_scr[...] = h
        gate_ref[...] = _dot_nt(wg_ref[...], h)

    y = _dot_nt(h_scr[...], w_ref[...])
    o_ref[...] = y.astype(o_ref.dtype)

    if t_tile is not None:
        t_ref = rest[0]

        @pl.when(pl.program_id(1) == t_tile)
        def _():
            for g in range(t_ref.shape[0]):
                t_ref[g] = y[g * GROUP:(g + 1) * GROUP, :].T


def _norm_proj(x, norm_g, mod, tokens_per_cond, w_stack, layer, n_main, t_tile=None):
    n_tok = x.shape[0]
    tm, tn = min(PROJ_TOKEN_TILE, tokens_per_cond), PROJ_COL_TILE
    cond_of_tile = lambda i: (i * tm) // tokens_per_cond
    w_t = jnp.swapaxes(w_stack, 1, 2)
    n_gate = w_t.shape[1] - n_main
    assert n_main % n_gate == 0 and n_gate % 8 == 0
    out_specs = [pl.BlockSpec((tm, tn), lambda i, j: (i, j)),
                 pl.BlockSpec((n_gate, tm), lambda i, j: (0, i))]
    out_shape = [jax.ShapeDtypeStruct((n_tok, n_main), BF16),
                 jax.ShapeDtypeStruct((n_gate, n_tok), F32)]
    if t_tile is not None:
        out_specs.append(pl.BlockSpec((tm // GROUP, tn, GROUP), lambda i, j: (i, 0, 0)))
        out_shape.append(jax.ShapeDtypeStruct((n_tok // GROUP, tn, GROUP), F32))
    return pl.pallas_call(
        functools.partial(_norm_proj_kernel, t_tile=t_tile),
        grid=(n_tok // tm, n_main // tn),
        in_specs=[pl.BlockSpec((tm, D_MODEL), lambda i, j: (i, 0)),
                  pl.BlockSpec((1, D_MODEL), lambda i, j: (0, 0)),
                  _mod_spec(1, cond_of_tile), _mod_spec(0, cond_of_tile),
                  pl.BlockSpec((None, tn, D_MODEL), lambda i, j: (layer, j, 0)),
                  pl.BlockSpec((None, n_gate, D_MODEL), lambda i, j: (layer, n_main // n_gate, 0))],
        out_specs=out_specs, out_shape=out_shape,
        scratch_shapes=[pltpu.VMEM((tm, D_MODEL), BF16)],
        compiler_params=_params("parallel", "arbitrary"),
        name="norm_proj",
    )(x, norm_g.reshape(1, D_MODEL), mod, mod, w_t, w_t)


def _gated_out_kernel(x_ref, o_ref, z_ref, ng_ref, g1_ref, w_ref, y_ref, wb_scr, *, n_heads, gate_fn):
    @pl.when(pl.program_id(0) == 0)
    def _():
        wb_scr[...] = w_ref[...].astype(BF16)

    dv = o_ref.shape[1] // n_heads
    parts = []
    for h in range(n_heads):
        sl = slice(h * dv, (h + 1) * dv)
        parts.append((_rms(o_ref[:, sl], ng_ref[...]) * gate_fn(z_ref[:, sl].astype(F32))).astype(BF16))
    hs = jnp.concatenate(parts, axis=1)
    y = jnp.dot(hs, wb_scr[...], preferred_element_type=F32)
    y_ref[...] = x_ref[...] + g1_ref[0] * y


def _gated_out(x, o, main, z_block, norm_g, mod, cond_of_tile, w_out, n_heads, gate_fn):
    n_tok = x.shape[0]
    tm = OUT_TOKEN_TILE
    width = o.shape[1]
    return pl.pallas_call(
        functools.partial(_gated_out_kernel, n_heads=n_heads, gate_fn=gate_fn),
        grid=(n_tok // tm,),
        in_specs=[pl.BlockSpec((tm, D_MODEL), lambda i: (i, 0)),
                  pl.BlockSpec((tm, width), lambda i: (i, 0)),
                  pl.BlockSpec((tm, width), lambda i: (i, z_block)),
                  pl.BlockSpec((1, width // n_heads), lambda i: (0, 0)),
                  _mod_spec(2, cond_of_tile),
                  pl.BlockSpec((width, D_MODEL), lambda i: (0, 0))],
        out_specs=pl.BlockSpec((tm, D_MODEL), lambda i: (i, 0)),
        out_shape=jax.ShapeDtypeStruct((n_tok, D_MODEL), F32),
        scratch_shapes=[pltpu.VMEM((width, D_MODEL), BF16)],
        compiler_params=_params("arbitrary"),
        name="gated_out",
    )(x, o, main, norm_g.reshape(1, -1), mod, w_out)


def _dft_mats(n):
    k = np.arange(n, dtype=np.int64)
    ang = 2.0 * np.pi * ((k[:, None] * k[None, :]) % n).astype(np.float64) / n
    s = 1.0 / np.sqrt(n)
    return np.cos(ang) * s, np.sin(ang) * s


def _fnet_kernel(x_ref, ng_ref, sc_ref, sh_ref, g1_ref, ct_ref, st_ref, cs_ref, w_ref, b_ref, y_ref, wb_scr):
    @pl.when(pl.program_id(0) == 0)
    def _():
        wb_scr[...] = w_ref[...].astype(BF16)

    seq = ct_ref.shape[0]
    seqs = [pl.ds(r, seq) for r in range(0, x_ref.shape[0], seq)]
    gd = FNET_GROUP_DIM
    hs = [_norm_mod(x_ref[s, :], ng_ref[...], sc_ref[0], sh_ref[0]).astype(BF16) for s in seqs]
    ps = [jnp.dot(ct_ref[...], h, preferred_element_type=F32) for h in hs]
    qs = [jnp.dot(st_ref[...], h, preferred_element_type=F32) for h in hs]
    fs = []
    for p, q in zip(ps, qs):
        parts = []
        for g in range(D_MODEL // gd):
            sl = slice(g * gd, (g + 1) * gd)
            pq = jnp.concatenate([p[:, sl], q[:, sl]], axis=1).astype(BF16)
            parts.append(jnp.dot(pq, cs_ref[...], preferred_element_type=F32).astype(BF16))
        fs.append(jnp.concatenate(parts, axis=1))
    ys = [jnp.dot(f, wb_scr[...], preferred_element_type=F32) + b_ref[...] for f in fs]
    for s, y in zip(seqs, ys):
        y_ref[s, :] = x_ref[s, :] + g1_ref[0] * y


def _fnet_mix(x, seq, layer, norm_g, mod, tokens_per_cond, w, b):
    n_tok = x.shape[0]
    tm = seq * max(1, min(FNET_STEP_TOKENS, tokens_per_cond) // seq)
    cond_of_tile = lambda i: (i * tm) // tokens_per_cond
    ct, st = _dft_mats(seq)
    cc, sc = _dft_mats(FNET_GROUP_DIM)
    cs = np.concatenate([cc, -sc], axis=0)
    const = lambda i: (0, 0)
    ct, st, cs = (jnp.asarray(m, F32).astype(BF16) for m in (ct, st, cs))
    return pl.pallas_call(
        _fnet_kernel,
        grid=(n_tok // tm,),
        in_specs=[pl.BlockSpec((tm, D_MODEL), lambda i: (i, 0)),
                  pl.BlockSpec((1, D_MODEL), const),
                  _mod_spec(1, cond_of_tile), _mod_spec(0, cond_of_tile), _mod_spec(2, cond_of_tile),
                  pl.BlockSpec((seq, seq), const),
                  pl.BlockSpec((seq, seq), const),
                  pl.BlockSpec((2 * FNET_GROUP_DIM, FNET_GROUP_DIM), const),
                  pl.BlockSpec((None, D_MODEL, D_MODEL), lambda i: (layer, 0, 0)),
                  pl.BlockSpec((1, D_MODEL), const)],
        out_specs=pl.BlockSpec((tm, D_MODEL), lambda i: (i, 0)),
        out_shape=jax.ShapeDtypeStruct((n_tok, D_MODEL), F32),
        scratch_shapes=[pltpu.VMEM((D_MODEL, D_MODEL), BF16)],
        compiler_params=_params("arbitrary"),
        name="fnet",
    )(x, norm_g.reshape(1, D_MODEL), mod, mod, mod,
      ct, st, cs, w, b.reshape(1, D_MODEL))


def _shift_rows(x, delta):
    n = x.shape[0]
    return pltpu.roll(x, (-delta) % n, 0)


def _dwconv_tokens(g, cw_ref, seq, rows):
    tm = g.shape[0]
    width = seq // rows
    assert seq & (seq - 1) == 0 and width & (width - 1) == 0
    t = lax.broadcasted_iota(jnp.int32, (tm, 1), 0)
    pos = t & (seq - 1)
    col = pos & (width - 1)
    row = pos >> (width.bit_length() - 1)
    g_cols = (_shift_rows(g, -1) * (col >= 1).astype(F32), g,
              _shift_rows(g, 1) * (col <= width - 2).astype(F32))
    out = None
    for di in (-1, 0, 1):
        if rows == 1 and di != 0:
            continue
        r = sum(g_cols[dj] * cw_ref[pl.ds(3 * (di + 1) + dj, 1), :] for dj in range(3))
        if di != 0:
            ok = (row + di >= 0) & (row + di <= rows - 1)
            r = _shift_rows(r, di * width) * ok.astype(F32)
        out = r if out is None else out + r
    return out


def _ffn_kernel(x_ref, ng_ref, sc_ref, sh_ref, g2_ref, wa_ref, wg_ref, cw_ref, cb_ref, wd_ref, nf_ref,
                y_ref, h_scr, acc_scr, act_scr, *, seq, rows, final_norm):
    j = pl.program_id(1)
    last = pl.num_programs(1) - 1
    tm = h_scr.shape[0]
    sub = tm // FFN_SUB_TILES
    parts = [pl.ds(r, sub) for r in range(0, tm, sub)]

    def up_project():
        wa, wg = wa_ref[...].astype(BF16), wg_ref[...].astype(BF16)
        gs = [jnp.dot(h_scr[p, :], wg, preferred_element_type=F32) for p in parts]
        return gs, [jnp.dot(h_scr[p, :], wa, preferred_element_type=F32) for p in parts]

    def down_project():
        wd = wd_ref[...].astype(BF16)
        for p in parts:
            acc_scr[p, :] += jnp.dot(act_scr[p, :], wd, preferred_element_type=F32)

    def conv_gate(gs, as_):
        per_seq = max(seq // sub, 1)
        convs = []
        for i in range(0, len(parts), per_seq):
            g = gs[i] if per_seq == 1 else jnp.concatenate(gs[i:i + per_seq], axis=0)
            c = _dwconv_tokens(g, cw_ref, seq, rows) + cb_ref[...]
            convs += [c[k * sub:(k + 1) * sub] for k in range(per_seq)]
        for p, c, a in zip(parts, convs, as_):
            act_scr[p, :] = (jax.nn.silu(c) * a).astype(BF16)

    @pl.when(j == 0)
    def _():
        h_scr[...] = _norm_mod(x_ref[...], ng_ref[...], sc_ref[0], sh_ref[0]).astype(BF16)
        acc_scr[...] = jnp.zeros_like(acc_scr)
        conv_gate(*up_project())

    @pl.when((j > 0) & (j < last))
    def _():
        gs, as_ = up_project()
        down_project()
        conv_gate(gs, as_)

    @pl.when(j == last)
    def _():
        down_project()
        y = x_ref[...] + g2_ref[0] * acc_scr[...]
        if final_norm:
            y = _rms(y, nf_ref[...])
        y_ref[...] = y


def _conv_ffn(x, seq, rows, layer, norm_g, mod, cond_of_tile, w_up, conv_w, conv_b, w_down, norm_final, final_norm):
    n_tok = x.shape[0]
    tm, tf = FFN_TOKEN_TILE, FFN_FF_TILE
    n_ff_tiles = D_FF // tf
    depth = w_up.shape[0]
    up = lambda j: jnp.minimum(j, n_ff_tiles - 1)
    down = lambda j: jnp.maximum(j - 1, 0)
    return pl.pallas_call(
        functools.partial(_ffn_kernel, seq=seq, rows=rows, final_norm=final_norm),
        grid=(n_tok // tm, n_ff_tiles + 1),
        in_specs=[pl.BlockSpec((tm, D_MODEL), lambda i, j: (i, 0)),
                  pl.BlockSpec((1, D_MODEL), lambda i, j: (0, 0)),
                  _mod_spec(4, cond_of_tile), _mod_spec(3, cond_of_tile), _mod_spec(5, cond_of_tile),
                  pl.BlockSpec((None, D_MODEL, tf), lambda i, j: (layer, 0, up(j))),
                  pl.BlockSpec((None, D_MODEL, tf), lambda i, j: (layer, 0, n_ff_tiles + up(j))),
                  pl.BlockSpec((None, 9, tf), lambda i, j: (layer, 0, up(j))),
                  pl.BlockSpec((None, 1, tf), lambda i, j: (layer, 0, up(j))),
                  pl.BlockSpec((None, tf, D_MODEL), lambda i, j: (layer, down(j), 0)),
                  pl.BlockSpec((1, D_MODEL), lambda i, j: (0, 0))],
        out_specs=pl.BlockSpec((tm, D_MODEL), lambda i, j: (i, 0)),
        out_shape=jax.ShapeDtypeStruct((n_tok, D_MODEL), F32),
        scratch_shapes=[pltpu.VMEM((tm, D_MODEL), BF16), pltpu.VMEM((tm, D_MODEL), F32),
                        pltpu.VMEM((tm, tf), BF16)],
        compiler_params=_params("parallel", "arbitrary"),
        name="conv_ffn",
    )(x, norm_g.reshape(1, D_MODEL), mod, mod, mod, w_up, w_up, conv_w.reshape(depth, 9, D_FF),
      conv_b.reshape(depth, 1, D_FF), w_down, norm_final.reshape(1, D_MODEL))


GroupMasks = collections.namedtuple("GroupMasks", "r c same incl strict")


def _chunk_masks():
    r = lax.broadcasted_iota(jnp.int32, (CHUNK, CHUNK), 0)
    c = lax.broadcasted_iota(jnp.int32, (CHUNK, CHUNK), 1)
    return (r >= c, r <= c)


def _group_masks(group=GROUP):
    r = lax.broadcasted_iota(jnp.int32, (group, group), 0)
    c = lax.broadcasted_iota(jnp.int32, (group, group), 1)
    same = (r >> LOG_CHUNK) == (c >> LOG_CHUNK)
    return GroupMasks(r, c, same, (same & (r >= c), same & (r <= c)), (same & (r > c), same & (r < c)))


def _gate_layouts(gt, batch, n_heads, hp, group=GROUP):
    seq = gt.shape[1] // batch
    n_groups, n_chunks = seq // group, seq // CHUNK
    assert seq % group == 0 and n_groups <= 8
    g = jnp.transpose(gt.reshape(N_DIR, 2, n_heads, batch, seq), (3, 2, 0, 1, 4))
    rows = g.reshape(g.shape[:4] + (n_groups, group))
    rows = jnp.pad(rows, ((0, 0),) * 4 + ((0, 8 - n_groups), (0, 0)))
    cols = jnp.swapaxes(g.reshape(batch, n_heads // hp, hp * N_DIR * 2 * n_chunks, CHUNK), -1, -2)
    return rows, cols


def _col_gate(gcol_ref, hh, d, kind, n_chunks):
    start = ((hh * N_DIR + d) * 2 + kind) * n_chunks
    return gcol_ref[0, 0, :, start:start + n_chunks]


def _split3(x):
    x1 = x.astype(BF16)
    r1 = x - x1.astype(F32)
    x2 = r1.astype(BF16)
    return x1, x2, (r1 - x2.astype(F32)).astype(BF16)


def _dot_mask_rhs(x, mask):
    m = x.shape[0]
    y = jnp.dot(jnp.concatenate(_split3(x), axis=0), jnp.where(mask, 1.0, 0.0).astype(BF16),
                preferred_element_type=F32)
    return y[:m] + y[m:2 * m] + y[2 * m:]


def _dot_mask_lhs(mask, x):
    mb = jnp.where(mask, 1.0, 0.0).astype(BF16)
    y1, y2, y3 = (jnp.dot(mb, p, preferred_element_type=F32) for p in _split3(x))
    return y1 + y2 + y3


def _cumsum_rows(row, d, gm):
    return _dot_mask_rhs(row, gm.incl[1 - d])


def _cumsum_cols(col, d, cm):
    return _dot_mask_lhs(cm[d], col)


def _store_cols(dst, col, n_chunks):
    for n in range(n_chunks):
        dst[n * CHUNK:(n + 1) * CHUNK, :] = jnp.broadcast_to(col[:, n:n + 1], (CHUNK, 128))


def _wide(x, width=None):
    reps = (GROUP if width is None else width) // x.shape[1]
    return x if reps == 1 else jnp.concatenate([x] * reps, axis=1)


def _rows(i, size):
    if isinstance(i, int):
        return pl.ds(i * size, size)
    return pl.ds(pl.multiple_of(i * size, size), size)


def _loop(n, body, init):
    if n == 1:
        return body(0, init)
    return lax.fori_loop(0, n, body, init)


def _conv_silu(x, w_ref):
    n = x.shape[0]
    k = w_ref.shape[0]
    t = lax.broadcasted_iota(jnp.int32, (n, 1), 0)
    acc = None
    for j in range(k):
        delta = j - k // 2
        if delta == 0:
            term = x
        else:
            ok = (t + delta >= 0) & (t + delta <= n - 1)
            term = _shift_rows(x, delta) * ok.astype(F32)
        term = term * w_ref[pl.ds(j, 1), :]
        acc = term if acc is None else acc + term
    return jax.nn.silu(acc)


def _l2norm(x):
    return x * lax.rsqrt(jnp.sum(x * x, axis=-1, keepdims=True) + EPS)


def _blockdiag_tri_inverse(mats, gm):
    dot16 = lambda x, y: jnp.dot(x, y, preferred_element_type=F32).astype(BF16)
    base = (gm.r >> 3) == (gm.c >> 3)
    eye = jnp.where(gm.r == gm.c, 1.0, 0.0)
    zero16 = jnp.zeros((), BF16)
    a16s = [a.astype(BF16) for a in mats]
    invs = [eye - jnp.where(base, a, 0.0) for a in mats]
    ps = [jnp.where(base, -a16, zero16) for a16 in a16s]
    for _ in range(2):
        ps = [dot16(p, p) for p in ps]
        invs = [inv + jnp.dot(inv.astype(BF16), p, preferred_element_type=F32) for inv, p in zip(invs, ps)]
    for s in (3, 4, 5):
        join = ((gm.r >> s) ^ (gm.c >> s)) == 1
        inv16s = [inv.astype(BF16) for inv in invs]
        tmp = [dot16(jnp.where(join, a16, zero16), inv16) for a16, inv16 in zip(a16s, inv16s)]
        invs = [inv - jnp.dot(inv16, t, preferred_element_type=F32) for inv, inv16, t in zip(invs, inv16s, tmp)]
    return invs


def _gdn_kernel(*refs, seq, hp, zero_init, emit_state):
    alog_ref, dtb_ref, q_ref, k_ref, v_ref, cwq_ref, cwk_ref, cwv_ref, grow_ref, gcol_ref = refs[:10]
    pos = 10
    s0_ref = None
    if not zero_init:
        s0_ref = refs[pos]
        pos += 1
    o_ref = refs[pos]
    pos += 1
    sfin_ref = None
    if emit_state:
        sfin_ref = refs[pos]
        pos += 1
    (q_scr, k_scr, v_scr, gcr_scr, gcb_scr, bb_scr, gl_scr,
     u_scr, w_scr, qd_scr, kd_scr, qkd_scr) = refs[pos:]

    n_chunks = seq // CHUNK
    grp = GDN_GROUP
    cm = _chunk_masks()
    gm = _group_masks(grp)
    head_dirs = [(hh, d) for hh in range(hp) for d in range(N_DIR)]
    lanes = lambda hh: slice(hh * DN_DK, (hh + 1) * DN_DK)

    q_scr[...] = _conv_silu(q_ref[...].astype(F32), cwq_ref)
    k_scr[...] = _conv_silu(k_ref[...].astype(F32), cwk_ref)
    v_scr[...] = _conv_silu(v_ref[...].astype(F32), cwv_ref)
    for hh in range(hp):
        q_scr[:, lanes(hh)] = _l2norm(q_scr[:, lanes(hh)]) * (DN_DK ** -0.5)
        k_scr[:, lanes(hh)] = _l2norm(k_scr[:, lanes(hh)])

    for hh, d in head_dirs:
        head = pl.program_id(1) * hp + hh
        neg_a = -jnp.exp(jnp.full((1, 1), alog_ref[d, head], F32))
        dtb = dtb_ref[d, head]
        gcr_scr[hh, d] = _cumsum_rows(neg_a * _softplus(grow_ref[0, hh, d, 0] + dtb), d, gm)
        gc_col = _cumsum_cols(neg_a * _softplus(_col_gate(gcol_ref, hh, d, 0, n_chunks) + dtb), d, cm)
        _store_cols(gcb_scr.at[hh, d], gc_col, n_chunks)
        _store_cols(bb_scr.at[hh, d], jax.nn.sigmoid(_col_gate(gcol_ref, hh, d, 1, n_chunks)), n_chunks)
        last = CHUNK - 1 if d == 0 else 0
        _store_cols(gl_scr.at[hh, d], jnp.broadcast_to(gc_col[last:last + 1, :], gc_col.shape), n_chunks)

    gps = min(GDN_GROUPS_PER_STEP, seq // grp)

    def group_step(it, carry):
        gis = [it * gps + k for k in range(gps)]
        rows = [_rows(gi, grp) for gi in gis]
        for h0 in range(0, hp, GDN_HEADS_PER_PASS):
            heads = range(h0, min(h0 + GDN_HEADS_PER_PASS, hp))
            kqs = {}
            for k, r in enumerate(rows):
                for hh in heads:
                    kg, qg = k_scr[r, lanes(hh)], q_scr[r, lanes(hh)]
                    kqs[k, hh] = _dot_nt(jnp.concatenate([kg, qg], axis=0), kg)
            chains = [(k, hh, d) for k in range(len(gis)) for hh in heads for d in range(N_DIR)]
            mats = []
            for k, hh, d in chains:
                gcb = gcb_scr[hh, d, rows[k], :]
                gcr = gcr_scr[hh, d, pl.ds(gis[k], 1), :]
                decay = jnp.exp(jnp.where(gm.incl[d], _wide(gcb, grp) - gcr, -jnp.inf))
                mats.append(jnp.where(gm.strict[d],
                                      kqs[k, hh][:grp] * _wide(bb_scr[hh, d, rows[k], :], grp) * decay, 0.0))
                qkd = kqs[k, hh][grp:] * decay
                qkd_scr[hh, d, rows[k], :] = sum(qkd[:, n * CHUNK:(n + 1) * CHUNK] for n in range(grp // CHUNK))
            t_invs = _blockdiag_tri_inverse(mats, gm)
            for (k, hh, d), t_inv in zip(chains, t_invs):
                r = rows[k]
                qg, kg, vg = q_scr[r, lanes(hh)], k_scr[r, lanes(hh)], v_scr[r, lanes(hh)]
                gcb = gcb_scr[hh, d, r, :]
                bb = bb_scr[hh, d, r, :]
                e_gc = jnp.exp(gcb)
                uw = _dot(t_inv, jnp.concatenate([vg * bb, kg * bb * e_gc], axis=1))
                u_scr[hh, d, r, :] = uw[:, :DN_DV]
                w_scr[hh, d, r, :] = uw[:, DN_DV:]
                qd_scr[hh, d, r, :] = qg * e_gc
                kd_scr[hh, d, r, :] = kg * jnp.exp(gl_scr[hh, d, r, :] - gcb)
        return carry

    _loop(seq // (grp * gps), group_step, 0)

    o_ref[...] = jnp.zeros_like(o_ref)

    def chunk_step(n, carry):
        idxs = [n if d == 0 else n_chunks - 1 - n for _, d in head_dirs]
        rows = [_rows(idx, CHUNK) for idx in idxs]
        wqs = [_dot(jnp.concatenate([w_scr[hh, d, r, :], qd_scr[hh, d, r, :]], axis=0), s)
               for (hh, d), r, s in zip(head_dirs, rows, carry)]
        v_news = [u_scr[hh, d, r, :] - wq[:CHUNK] for (hh, d), r, wq in zip(head_dirs, rows, wqs)]
        o_ns = [wq[CHUNK:] + _dot(qkd_scr[hh, d, r, :], v_new)
                for (hh, d), r, wq, v_new in zip(head_dirs, rows, wqs, v_news)]
        new = []
        for (hh, d), idx, r, s, v_new in zip(head_dirs, idxs, rows, carry, v_news):
            s_decay = jnp.exp(gl_scr[hh, d, pl.ds(idx * CHUNK, 1), :])
            new.append(s * s_decay + _dot_tn(kd_scr[hh, d, r, :], v_new))
        for (hh, d), r, o_n in zip(head_dirs, rows, o_ns):
            o_ref[r, lanes(hh)] += o_n
        return tuple(new)

    if zero_init:
        init = (jnp.zeros((DN_DK, DN_DV), F32),) * len(head_dirs)
    else:
        init = tuple(s0_ref[0, d, hh] for hh, d in head_dirs)
    fin = lax.fori_loop(0, n_chunks, chunk_step, init)
    if emit_state:
        for i, (hh, d) in enumerate(head_dirs):
            sfin_ref[0, d, hh] = fin[i]


def _gdn_scan(main, gt, batch, seq, conv_w, a_log, dt_bias, s0, emit_state):
    n_tok = batch * seq
    nh = DN_HEADS
    hp = min(nh, max(2, GDN_STEP_POSITIONS // seq))
    n_hb = nh // hp
    assert seq % (GDN_GROUP * min(GDN_GROUPS_PER_STEP, seq // GDN_GROUP)) == 0
    grow, gcol = _gate_layouts(gt, batch, nh, hp, GDN_GROUP)
    zero_init = s0 is None
    smem = pl.BlockSpec(memory_space=pltpu.SMEM)
    qkv_spec = lambda off: pl.BlockSpec((seq, hp * DN_DK), lambda b, h: (b, off + h))
    cw_spec = lambda off: pl.BlockSpec((DN_CONV, hp * DN_DK), lambda b, h: (0, off + h))
    state_spec = pl.BlockSpec((1, N_DIR, hp, DN_DK, DN_DV), lambda b, h: (b, 0, h, 0, 0))
    in_specs = [smem, smem, qkv_spec(0), qkv_spec(n_hb), qkv_spec(2 * n_hb),
                cw_spec(0), cw_spec(n_hb), cw_spec(2 * n_hb),
                pl.BlockSpec((1, hp) + grow.shape[2:], lambda b, h: (b, h, 0, 0, 0, 0)),
                pl.BlockSpec((1, 1) + gcol.shape[2:], lambda b, h: (b, h, 0, 0))]
    args = [a_log, dt_bias, main, main, main, conv_w, conv_w, conv_w, grow, gcol]
    if not zero_init:
        in_specs.append(state_spec)
        args.append(s0)
    out_specs = [pl.BlockSpec((seq, hp * DN_DV), lambda b, h: (b, h))]
    out_shape = [jax.ShapeDtypeStruct((n_tok, nh * DN_DV), F32)]
    if emit_state:
        out_specs.append(state_spec)
        out_shape.append(jax.ShapeDtypeStruct((batch, N_DIR, nh, DN_DK, DN_DV), F32))
    per_dir = lambda width: pltpu.VMEM((hp, N_DIR, seq, width), F32)
    qkv_scr = pltpu.VMEM((seq, hp * DN_DK), F32)
    outs = pl.pallas_call(
        functools.partial(_gdn_kernel, seq=seq, hp=hp, zero_init=zero_init, emit_state=emit_state),
        grid=(batch, n_hb),
        in_specs=in_specs, out_specs=out_specs, out_shape=out_shape,
        scratch_shapes=[qkv_scr, qkv_scr, qkv_scr,
                        pltpu.VMEM((hp, N_DIR, 8, GDN_GROUP), F32),
                        per_dir(128), per_dir(128), per_dir(128),
                        per_dir(DN_DV), per_dir(DN_DK), per_dir(DN_DK), per_dir(DN_DK), per_dir(CHUNK)],
        compiler_params=_params("parallel", "parallel"),
        name="gdn_scan",
    )(*args)
    return outs[0], (outs[1] if emit_state else None)


ML_AUG = 2 * ML_DV


def _mlstm_kernel(*refs, seq, hp, zero_init, emit_state):
    bi_ref, bf_ref, q_ref, k_ref, v_ref, kt_ref, grow_ref, gcol_ref = refs[:8]
    pos = 8
    c0_ref = m0_ref = None
    if not zero_init:
        c0_ref, m0_ref = refs[pos:pos + 2]
        pos += 2
    o_ref = refs[pos]
    pos += 1
    cfin_ref = nfin_ref = mfin_ref = None
    if emit_state:
        cfin_ref, nfin_ref, mfin_ref = refs[pos:pos + 3]
        pos += 3
    bcr_scr, lir_scr, kwr_scr, bcb_scr, mi_scr, dec_scr, dc_scr, cin_scr = refs[pos:]

    n_chunks = seq // CHUNK
    n_groups = seq // GROUP
    head0 = pl.program_id(1) * hp
    cm = _chunk_masks()
    gm = _group_masks()
    ones_col = jnp.where(lax.broadcasted_iota(jnp.int32, (GROUP, ML_DV), 1) == 0, 1.0, 0.0)
    head_dirs = [(hh, d) for hh in range(hp) for d in range(N_DIR)]

    m_fin = {}
    row_group = lax.broadcasted_iota(jnp.int32, (8, GROUP), 0)
    row_chunk = lax.broadcasted_iota(jnp.int32, (8, GROUP), 1) >> LOG_CHUNK
    for hh, d in head_dirs:
        b_i = bi_ref[d, head0 + hh]
        b_f = bf_ref[d, head0 + hh]
        li_row = grow_ref[0, hh, d, 0] + b_i
        lf_row = -_softplus(-(grow_ref[0, hh, d, 1] + b_f))
        bc_row = _cumsum_rows(lf_row, d, gm)
        lir_scr[hh, d] = li_row
        bcr_scr[hh, d] = bc_row
        w_row = _dot_mask_rhs(lf_row, gm.same) - bc_row + li_row
        m_out_row = jnp.zeros((8, GROUP), F32)
        li_col = _col_gate(gcol_ref, hh, d, 0, n_chunks) + b_i
        bc_col = _cumsum_cols(-_softplus(-(_col_gate(gcol_ref, hh, d, 1, n_chunks) + b_f)), d, cm)
        last = CHUNK - 1 if d == 0 else 0
        b_last = bc_col[last:last + 1, :]
        w_col = b_last - bc_col + li_col
        w_max = jnp.max(w_col, axis=0, keepdims=True)
        m = jnp.zeros((1, 1), F32) if zero_init else m0_ref[0, d, hh][:, 0:1]
        for step in range(n_chunks):
            n = step if d == 0 else n_chunks - 1 - step
            sl = slice(n * CHUNK, (n + 1) * CHUNK)
            m_new = jnp.maximum(b_last[:, n:n + 1] + m, w_max[:, n:n + 1])
            bcb_scr[hh, d, sl, :] = jnp.broadcast_to(bc_col[:, n:n + 1], (CHUNK, 128))
            mi_scr[hh, d, sl, :] = jnp.broadcast_to(m, (CHUNK, 128))
            dec_scr[hh, d, n:n + 1, :] = jnp.broadcast_to(jnp.exp(b_last[:, n:n + 1] + m - m_new), (1, 128))
            in_chunk = (row_group == n // CHUNKS_PER_GROUP) & (row_chunk == n % CHUNKS_PER_GROUP)
            m_out_row = jnp.where(in_chunk, m_new, m_out_row)
            m = m_new
        m_fin[hh, d] = m
        kwr_scr[hh, d] = jnp.exp(w_row - m_out_row)

    def load_kv(rows, hh):
        kg = k_ref[rows, hh * ML_DQK:(hh + 1) * ML_DQK].astype(F32)
        v_aug = jnp.concatenate([v_ref[rows, hh * ML_DV:(hh + 1) * ML_DV].astype(F32), ones_col], axis=1)
        return kg, v_aug

    def delta_step(gi, carry):
        rows = _rows(gi, GROUP)
        v_augs = [load_kv(rows, hh)[1] for hh in range(hp)]
        kt = kt_ref[gi]
        lhs = []
        for hh, d in head_dirs:
            kw_t = kt[hh * ML_DQK:(hh + 1) * ML_DQK, :] * kwr_scr[hh, d, pl.ds(gi, 1), :]
            lhs.append(jnp.where(gm.same, jnp.concatenate([kw_t] * CHUNKS_PER_GROUP, axis=0), 0.0))
        for (hh, d), kw in zip(head_dirs, lhs):
            dc_scr[hh, d, rows, :] = _dot(kw, v_augs[hh])
        return carry

    _loop(n_groups, delta_step, 0)

    def prefix_step(n, carry):
        new = []
        for (hh, d), c_aug in zip(head_dirs, carry):
            idx = n if d == 0 else n_chunks - 1 - n
            rows = _rows(idx, CHUNK)
            cin_scr[hh, d, rows, :] = c_aug
            new.append(c_aug * _wide(dec_scr[hh, d, pl.ds(idx, 1), :], ML_AUG) + dc_scr[hh, d, rows, :])
        return tuple(new)

    if zero_init:
        init = (jnp.zeros((ML_DQK, ML_AUG), F32),) * len(head_dirs)
    else:
        init = tuple(c0_ref[0, d, hh] for hh, d in head_dirs)
    fin = lax.fori_loop(0, n_chunks, prefix_step, init)

    def out_step(gi, carry):
        rows = _rows(gi, GROUP)
        kvs = [load_kv(rows, hh) for hh in range(hp)]
        qgs = [q_ref[rows, hh * ML_DQK:(hh + 1) * ML_DQK].astype(F32) * (ML_DQK ** -0.5) for hh in range(hp)]
        qks = [_dot_nt(qg, kv[0]) for qg, kv in zip(qgs, kvs)]
        chunk = lambda n: slice(n * CHUNK, (n + 1) * CHUNK)
        inters = []
        for hh, d in head_dirs:
            c_in = cin_scr[hh, d, rows, :]
            inters.append(jnp.concatenate(
                [_dot(qgs[hh][chunk(n)], c_in[chunk(n)]) for n in range(CHUNKS_PER_GROUP)], axis=0))
        m_ts, b_ms, ss = [], [], []
        for hh, d in head_dirs:
            b_colb = bcb_scr[hh, d, rows, :]
            b_m = b_colb[:, 0:1] + mi_scr[hh, d, rows, :][:, 0:1]
            d_log = jnp.where(gm.incl[d], _wide(b_colb) - bcr_scr[hh, d, pl.ds(gi, 1), :]
                              + lir_scr[hh, d, pl.ds(gi, 1), :], -jnp.inf)
            m_t = jnp.maximum(b_m, jnp.max(d_log, axis=-1, keepdims=True))
            m_ts.append(m_t)
            b_ms.append(b_m)
            ss.append(qks[hh] * jnp.exp(d_log - m_t))
        intras = [_dot(s, kvs[hh][1]) for (hh, d), s in zip(head_dirs, ss)]
        hs = []
        for inter, intra, m_t, b_m in zip(inters, intras, m_ts, b_ms):
            num = jnp.exp(b_m - m_t) * inter + intra
            den = num[:, ML_DV:ML_DV + 1]
            hs.append(num[:, :ML_DV] / jnp.maximum(jnp.abs(den), jnp.exp(-m_t)))
        for hh in range(hp):
            o_ref[rows, hh * ML_DV:(hh + 1) * ML_DV] = hs[N_DIR * hh] + hs[N_DIR * hh + 1]
        return carry

    _loop(n_groups, out_step, 0)

    if emit_state:
        for i, (hh, d) in enumerate(head_dirs):
            cfin_ref[0, d, hh] = fin[i][:, :ML_DV]
            nfin_ref[0, d, hh] = fin[i][:, ML_DV:]
            mfin_ref[0, d, hh] = jnp.broadcast_to(m_fin[hh, d], (1, 128))


def _mlstm_scan(main, kt, gt, batch, seq, b_i, b_f, state0, emit_state):
    n_tok = batch * seq
    n_groups = seq // GROUP
    n_chunks = seq // CHUNK
    nh = ML_HEADS
    hp = min(nh, max(2, SCAN_STEP_POSITIONS // seq))
    n_hb = nh // hp
    grow, gcol = _gate_layouts(gt, batch, nh, hp)
    zero_init = state0 is None
    smem = pl.BlockSpec(memory_space=pltpu.SMEM)
    k_off = (nh * ML_DQK) // (hp * ML_DQK)
    v_off = (2 * nh * ML_DQK) // (hp * ML_DV)
    c_spec = pl.BlockSpec((1, N_DIR, hp, ML_DQK, ML_AUG), lambda b, p: (b, 0, p, 0, 0))
    m_spec = pl.BlockSpec((1, N_DIR, hp, 1, 128), lambda b, p: (b, 0, p, 0, 0))
    in_specs = [smem, smem,
                pl.BlockSpec((seq, hp * ML_DQK), lambda b, p: (b, p)),
                pl.BlockSpec((seq, hp * ML_DQK), lambda b, p: (b, k_off + p)),
                pl.BlockSpec((seq, hp * ML_DV), lambda b, p: (b, v_off + p)),
                pl.BlockSpec((n_groups, hp * ML_DQK, GROUP), lambda b, p: (b, p, 0)),
                pl.BlockSpec((1, hp) + grow.shape[2:], lambda b, p: (b, p, 0, 0, 0, 0)),
                pl.BlockSpec((1, 1) + gcol.shape[2:], lambda b, p: (b, p, 0, 0))]
    args = [b_i, b_f, main, main, main, kt, grow, gcol]
    if not zero_init:
        in_specs += [c_spec, m_spec]
        args += list(state0)
    out_specs = [pl.BlockSpec((seq, hp * ML_DV), lambda b, p: (b, p))]
    out_shape = [jax.ShapeDtypeStruct((n_tok, nh * ML_DV), F32)]
    if emit_state:
        half_spec = pl.BlockSpec((1, N_DIR, hp, ML_DQK, ML_DV), lambda b, p: (b, 0, p, 0, 0))
        out_specs += [half_spec, half_spec, m_spec]
        out_shape += [jax.ShapeDtypeStruct((batch, N_DIR, nh, ML_DQK, ML_DV), F32),
                      jax.ShapeDtypeStruct((batch, N_DIR, nh, ML_DQK, ML_AUG - ML_DV), F32),
                      jax.ShapeDtypeStruct((batch, N_DIR, nh, 1, 128), F32)]
    per_hd = lambda rows, width: pltpu.VMEM((hp, N_DIR, rows, width), F32)
    outs = pl.pallas_call(
        functools.partial(_mlstm_kernel, seq=seq, hp=hp, zero_init=zero_init, emit_state=emit_state),
        grid=(batch, n_hb),
        in_specs=in_specs, out_specs=out_specs, out_shape=out_shape,
        scratch_shapes=[per_hd(8, GROUP), per_hd(8, GROUP), per_hd(8, GROUP),
                        per_hd(seq, 128), per_hd(seq, 128),
                        per_hd(max(n_chunks, 8), 128),
                        per_hd(seq, ML_AUG), per_hd(seq, ML_AUG)],
        compiler_params=_params("parallel", "parallel"),
        name="mlstm_scan",
    )(*args)
    return outs[0], (tuple(outs[1:4]) if emit_state else None)


def _trunk(x, batch, seq, rows, mods, tokens_per_cond, st_d, st_ml, emit_state, p):
    depth = p["w_ada"].shape[0]
    tile_cond = lambda tile: (lambda i: (i * tile) // tokens_per_cond)
    new_d = new_ml = None
    for layer in range(depth):
        mod = mods[layer]
        kind, j = layer % 3, layer // 3
        if kind == 0:
            x = _fnet_mix(x, seq, j, p["norm_mix"][layer], mod, tokens_per_cond, p["fnet_w"], p["fnet_b"][j])
        elif kind == 1:
            n_main = DN_HEADS * (2 * DN_DK + 2 * DN_DV)
            main, gt = _norm_proj(x, p["norm_mix"][layer], mod, tokens_per_cond, p["dn_w_in"], j, n_main)
            s0 = None if st_d is None else st_d[:, j]
            o, sfin = _gdn_scan(main, gt, batch, seq, p["dn_conv_w"][j], p["dn_a_log"][j], p["dn_dt_bias"][j],
                                s0, emit_state)
            if emit_state:
                new_d = sfin
            x = _gated_out(x, o, main, 3, p["dn_norm"][j], mod, tile_cond(OUT_TOKEN_TILE), p["dn_w_out"][j],
                           DN_HEADS, jax.nn.silu)
        else:
            n_main = 2 * ML_HEADS * ML_DQK + 2 * ML_HEADS * ML_DV
            k_tile = (ML_HEADS * ML_DQK) // PROJ_COL_TILE
            assert ML_HEADS * ML_DQK == PROJ_COL_TILE
            main, gt, kt = _norm_proj(x, p["norm_mix"][layer], mod, tokens_per_cond, p["ml_w_in"], j, n_main,
                                      t_tile=k_tile)
            state0 = None
            if st_ml is not None:
                c0, n0, m0 = (s[:, j] for s in st_ml)
                pad = jnp.zeros(c0.shape[:-1] + (ML_AUG - ML_DV - 1,), F32)
                c_aug0 = jnp.concatenate([c0, n0[..., None], pad], axis=-1)
                m0b = jnp.broadcast_to(m0[..., None, None], m0.shape + (1, 128))
                state0 = (c_aug0, m0b)
            o, fin = _mlstm_scan(main, kt, gt, batch, seq, p["ml_b_i"][j], p["ml_b_f"][j], state0, emit_state)
            if emit_state:
                new_ml = (fin[0], fin[1][..., 0], fin[2][..., 0, 0])
            x = _gated_out(x, o, main, 2, p["ml_norm"][j], mod, tile_cond(OUT_TOKEN_TILE), p["ml_w_out"][j],
                           ML_HEADS, jax.nn.sigmoid)
        x = _conv_ffn(x, seq, rows, layer, p["norm_ffn"][layer], mod, tile_cond(FFN_TOKEN_TILE), p["ffn_w_up"],
                      p["ffn_conv_w"], p["ffn_conv_b"], p["ffn_w_down"], p["norm_final"],
                      final_norm=(layer == depth - 1))
    return x, new_d, new_ml


def kernel(x_prompt, x_sample, state_delta, state_mlstm_c, state_mlstm_n, state_mlstm_m, c, c_ctx, w_ada, b_ada, norm_mix, norm_ffn, norm_final, ffn_w_up, ffn_conv_w, ffn_conv_b, ffn_w_down, fnet_w, fnet_b, dn_w_in, dn_conv_w, dn_a_log, dn_dt_bias, dn_norm, dn_w_out, ml_w_in, ml_b_i, ml_b_f, ml_norm, ml_w_out):
    p = dict(w_ada=w_ada, norm_mix=norm_mix, norm_ffn=norm_ffn, norm_final=norm_final, ffn_w_up=ffn_w_up,
             ffn_conv_w=ffn_conv_w, ffn_conv_b=ffn_conv_b, ffn_w_down=ffn_w_down, fnet_w=fnet_w, fnet_b=fnet_b,
             dn_w_in=dn_w_in, dn_conv_w=dn_conv_w, dn_a_log=dn_a_log, dn_dt_bias=dn_dt_bias, dn_norm=dn_norm,
             dn_w_out=dn_w_out, ml_w_in=ml_w_in, ml_b_i=ml_b_i, ml_b_f=ml_b_f, ml_norm=ml_norm, ml_w_out=ml_w_out)
    b_ctx, t_ctx, _ = x_prompt.shape
    b_smp, t_smp, _ = x_sample.shape
    depth = w_ada.shape[0]

    cond8 = jnp.concatenate([c_ctx[None, :], c, jnp.zeros((8 - 1 - b_smp, D_MODEL), F32)], axis=0)
    mods = _adaln_all(cond8, w_ada, b_ada)
    mods_ctx = [mods[l, 0:1].reshape(1, 1, -1) for l in range(depth)]
    mods_smp = [mods[l, 1:1 + b_smp].reshape(b_smp, 1, -1) for l in range(depth)]

    y_ctx, new_d, new_ml = _trunk(x_prompt.reshape(b_ctx * t_ctx, D_MODEL), b_ctx, t_ctx, 1, mods_ctx,
                                  b_ctx * t_ctx, None, None, True, p)
    y_smp, _, _ = _trunk(x_sample.reshape(b_smp * t_smp, D_MODEL), b_smp, t_smp, t_smp // GRID_W, mods_smp,
                         t_smp, state_delta, (state_mlstm_c, state_mlstm_n, state_mlstm_m), False, p)

    new_c, new_n, new_m = new_ml
    return (y_ctx.reshape(b_ctx, t_ctx, D_MODEL), y_smp.reshape(b_smp, t_smp, D_MODEL),
            new_d[:, None], new_c[:, None], new_n[:, None], new_m[:, None])
```

```python
import collections
import functools

import numpy as np
import jax
import jax.numpy as jnp
from jax import lax
from jax.experimental import pallas as pl
from jax.experimental.pallas import tpu as pltpu

F32 = jnp.float32
BF16 = jnp.bfloat16

D_MODEL = 1024
EPS = 1e-6
N_DIR = 2
CHUNK = 64
LOG_CHUNK = 6
GROUP = 256
CHUNKS_PER_GROUP = GROUP // CHUNK
GDN_GROUP = 128
GDN_GROUPS_PER_STEP = 4
GDN_HEADS_PER_PASS = 4
GDN_STEP_POSITIONS = 2048
FNET_GROUP_DIM = 256
DN_HEADS, DN_DK, DN_DV, DN_CONV = 8, 128, 128, 5
ML_HEADS, ML_DQK, ML_DV = 8, 64, 128
D_FF = 2816
GRID_W = 64

FNET_STEP_TOKENS = 1024
FFN_TOKEN_TILE = 1024
FFN_FF_TILE = 256
FFN_SUB_TILES = 4
PROJ_TOKEN_TILE = 2048
PROJ_COL_TILE = 512
OUT_TOKEN_TILE = 512
ADA_COL_TILE = 1536
SCAN_STEP_POSITIONS = 1024
VMEM_LIMIT = 56 * 1024 * 1024


def _params(*sem):
    return pltpu.CompilerParams(dimension_semantics=sem, vmem_limit_bytes=VMEM_LIMIT)


def _dot(a, b):
    return jnp.dot(a.astype(BF16), b.astype(BF16), preferred_element_type=F32)


def _dot_nt(a, b):
    return lax.dot_general(a.astype(BF16), b.astype(BF16), (((1,), (1,)), ((), ())),
                           preferred_element_type=F32)


def _dot_tn(a, b):
    return lax.dot_general(a.astype(BF16), b.astype(BF16), (((0,), (0,)), ((), ())),
                           preferred_element_type=F32)


def _rms(x, g):
    return x * lax.rsqrt(jnp.mean(x * x, axis=-1, keepdims=True) + EPS) * g


def _norm_mod(x, g, sc, sh):
    return _rms(x, g) * (1.0 + sc) + sh


def _softplus(x):
    return jnp.maximum(x, 0.0) + jnp.log1p(jnp.exp(-jnp.abs(x)))


def _mod_spec(chunk, cond_of_tile):
    return pl.BlockSpec((1, 1, D_MODEL), lambda i, *_: (cond_of_tile(i), 0, chunk))


def _adaln_kernel(c_ref, w_ref, b_ref, o_ref):
    s = jax.nn.silu(c_ref[...])
    o_ref[0] = _dot(s, w_ref[0]) + b_ref[0]


def _adaln_all(cond8, w_ada, b_ada):
    depth = w_ada.shape[0]
    n_out = w_ada.shape[2]
    tn = ADA_COL_TILE
    return pl.pallas_call(
        _adaln_kernel,
        grid=(depth, n_out // tn),
        in_specs=[pl.BlockSpec((8, D_MODEL), lambda l, j: (0, 0)),
                  pl.BlockSpec((1, D_MODEL, tn), lambda l, j: (l, 0, j)),
                  pl.BlockSpec((1, 1, tn), lambda l, j: (l, 0, j))],
        out_specs=pl.BlockSpec((1, 8, tn), lambda l, j: (l, 0, j)),
        out_shape=jax.ShapeDtypeStruct((depth, 8, n_out), F32),
        compiler_params=_params("parallel", "parallel"),
        name="adaln",
    )(cond8, w_ada, b_ada.reshape(depth, 1, n_out))


def _norm_proj_kernel(x_ref, g_ref, sc_ref, sh_ref, w_ref, wg_ref, o_ref, gate_ref, *rest, t_tile):
    h_scr = rest[-1]

    @pl.when(pl.program_id(1) == 0)
    def _():
        h = _norm_mod(x_ref[...], g_ref[...], sc_ref[0], sh_ref[0]).astype(BF16)
        h_scr[...] = h
        gate_ref[...] = _dot_nt(wg_ref[...], h)

    y = _dot_nt(h_scr[...], w_ref[...])
    o_ref[...] = y.astype(o_ref.dtype)

    if t_tile is not None:
        t_ref = rest[0]

        @pl.when(pl.program_id(1) == t_tile)
        def _():
            for g in range(t_ref.shape[0]):
                t_ref[g] = y[g * GROUP:(g + 1) * GROUP, :].T


def _norm_proj(x, norm_g, mod, tokens_per_cond, w_stack, layer, n_main, t_tile=None):
    n_tok = x.shape[0]
    tm, tn = min(PROJ_TOKEN_TILE, tokens_per_cond), PROJ_COL_TILE
    cond_of_tile = lambda i: (i * tm) // tokens_per_cond
    w_t = jnp.swapaxes(w_stack, 1, 2)
    n_gate = w_t.shape[1] - n_main
    assert n_main % n_gate == 0 and n_gate % 8 == 0
    out_specs = [pl.BlockSpec((tm, tn), lambda i, j: (i, j)),
                 pl.BlockSpec((n_gate, tm), lambda i, j: (0, i))]
    out_shape = [jax.ShapeDtypeStruct((n_tok, n_main), BF16),
                 jax.ShapeDtypeStruct((n_gate, n_tok), F32)]
    if t_tile is not None:
        out_specs.append(pl.BlockSpec((tm // GROUP, tn, GROUP), lambda i, j: (i, 0, 0)))
        out_shape.append(jax.ShapeDtypeStruct((n_tok // GROUP, tn, GROUP), F32))
    return pl.pallas_call(
        functools.partial(_norm_proj_kernel, t_tile=t_tile),
        grid=(n_tok // tm, n_main // tn),
        in_specs=[pl.BlockSpec((tm, D_MODEL), lambda i, j: (i, 0)),
                  pl.BlockSpec((1, D_MODEL), lambda i, j: (0, 0)),
                  _mod_spec(1, cond_of_tile), _mod_spec(0, cond_of_tile),
                  pl.BlockSpec((None, tn, D_MODEL), lambda i, j: (layer, j, 0)),
                  pl.BlockSpec((None, n_gate, D_MODEL), lambda i, j: (layer, n_main // n_gate, 0))],
        out_specs=out_specs, out_shape=out_shape,
        scratch_shapes=[pltpu.VMEM((tm, D_MODEL), BF16)],
        compiler_params=_params("parallel", "arbitrary"),
        name="norm_proj",
    )(x, norm_g.reshape(1, D_MODEL), mod, mod, w_t, w_t)


def _gated_out_kernel(x_ref, o_ref, z_ref, ng_ref, g1_ref, w_ref, y_ref, wb_scr, *, n_heads, gate_fn):
    @pl.when(pl.program_id(0) == 0)
    def _():
        wb_scr[...] = w_ref[...].astype(BF16)

    dv = o_ref.shape[1] // n_heads
    parts = []
    for h in range(n_heads):
        sl = slice(h * dv, (h + 1) * dv)
        parts.append((_rms(o_ref[:, sl], ng_ref[...]) * gate_fn(z_ref[:, sl].astype(F32))).astype(BF16))
    hs = jnp.concatenate(parts, axis=1)
    y = jnp.dot(hs, wb_scr[...], preferred_element_type=F32)
    y_ref[...] = x_ref[...] + g1_ref[0] * y


def _gated_out(x, o, main, z_block, norm_g, mod, cond_of_tile, w_out, n_heads, gate_fn):
    n_tok = x.shape[0]
    tm = OUT_TOKEN_TILE
    width = o.shape[1]
    return pl.pallas_call(
        functools.partial(_gated_out_kernel, n_heads=n_heads, gate_fn=gate_fn),
        grid=(n_tok // tm,),
        in_specs=[pl.BlockSpec((tm, D_MODEL), lambda i: (i, 0)),
                  pl.BlockSpec((tm, width), lambda i: (i, 0)),
                  pl.BlockSpec((tm, width), lambda i: (i, z_block)),
                  pl.BlockSpec((1, width // n_heads), lambda i: (0, 0)),
                  _mod_spec(2, cond_of_tile),
                  pl.BlockSpec((width, D_MODEL), lambda i: (0, 0))],
        out_specs=pl.BlockSpec((tm, D_MODEL), lambda i: (i, 0)),
        out_shape=jax.ShapeDtypeStruct((n_tok, D_MODEL), F32),
        scratch_shapes=[pltpu.VMEM((width, D_MODEL), BF16)],
        compiler_params=_params("arbitrary"),
        name="gated_out",
    )(x, o, main, norm_g.reshape(1, -1), mod, w_out)


def _dft_mats(n):
    k = np.arange(n, dtype=np.int64)
    ang = 2.0 * np.pi * ((k[:, None] * k[None, :]) % n).astype(np.float64) / n
    s = 1.0 / np.sqrt(n)
    return np.cos(ang) * s, np.sin(ang) * s


def _fnet_kernel(x_ref, ng_ref, sc_ref, sh_ref, g1_ref, ct_ref, st_ref, cs_ref, w_ref, b_ref, y_ref, wb_scr):
    @pl.when(pl.program_id(0) == 0)
    def _():
        wb_scr[...] = w_ref[...].astype(BF16)

    seq = ct_ref.shape[0]
    seqs = [pl.ds(r, seq) for r in range(0, x_ref.shape[0], seq)]
    gd = FNET_GROUP_DIM
    hs = [_norm_mod(x_ref[s, :], ng_ref[...], sc_ref[0], sh_ref[0]).astype(BF16) for s in seqs]
    ps = [jnp.dot(ct_ref[...], h, preferred_element_type=F32) for h in hs]
    qs = [jnp.dot(st_ref[...], h, preferred_element_type=F32) for h in hs]
    fs = []
    for p, q in zip(ps, qs):
        parts = []
        for g in range(D_MODEL // gd):
            sl = slice(g * gd, (g + 1) * gd)
            pq = jnp.concatenate([p[:, sl], q[:, sl]], axis=1).astype(BF16)
            parts.append(jnp.dot(pq, cs_ref[...], preferred_element_type=F32).astype(BF16))
        fs.append(jnp.concatenate(parts, axis=1))
    ys = [jnp.dot(f, wb_scr[...], preferred_element_type=F32) + b_ref[...] for f in fs]
    for s, y in zip(seqs, ys):
        y_ref[s, :] = x_ref[s, :] + g1_ref[0] * y


def _fnet_mix(x, seq, layer, norm_g, mod, tokens_per_cond, w, b):
    n_tok = x.shape[0]
    tm = seq * max(1, min(FNET_STEP_TOKENS, tokens_per_cond) // seq)
    cond_of_tile = lambda i: (i * tm) // tokens_per_cond
    ct, st = _dft_mats(seq)
    cc, sc = _dft_mats(FNET_GROUP_DIM)
    cs = np.concatenate([cc, -sc], axis=0)
    const = lambda i: (0, 0)
    ct, st, cs = (jnp.asarray(m, F32).astype(BF16) for m in (ct, st, cs))
    return pl.pallas_call(
        _fnet_kernel,
        grid=(n_tok // tm,),
        in_specs=[pl.BlockSpec((tm, D_MODEL), lambda i: (i, 0)),
                  pl.BlockSpec((1, D_MODEL), const),
                  _mod_spec(1, cond_of_tile), _mod_spec(0, cond_of_tile), _mod_spec(2, cond_of_tile),
                  pl.BlockSpec((seq, seq), const),
                  pl.BlockSpec((seq, seq), const),
                  pl.BlockSpec((2 * FNET_GROUP_DIM, FNET_GROUP_DIM), const),
                  pl.BlockSpec((None, D_MODEL, D_MODEL), lambda i: (layer, 0, 0)),
                  pl.BlockSpec((1, D_MODEL), const)],
        out_specs=pl.BlockSpec((tm, D_MODEL), lambda i: (i, 0)),
        out_shape=jax.ShapeDtypeStruct((n_tok, D_MODEL), F32),
        scratch_shapes=[pltpu.VMEM((D_MODEL, D_MODEL), BF16)],
        compiler_params=_params("arbitrary"),
        name="fnet",
    )(x, norm_g.reshape(1, D_MODEL), mod, mod, mod,
      ct, st, cs, w, b.reshape(1, D_MODEL))


def _shift_rows(x, delta):
    n = x.shape[0]
    return pltpu.roll(x, (-delta) % n, 0)


def _dwconv_tokens(g, cw_ref, seq, rows):
    tm = g.shape[0]
    width = seq // rows
    assert seq & (seq - 1) == 0 and width & (width - 1) == 0
    t = lax.broadcasted_iota(jnp.int32, (tm, 1), 0)
    pos = t & (seq - 1)
    col = pos & (width - 1)
    row = pos >> (width.bit_length() - 1)
    g_cols = (_shift_rows(g, -1) * (col >= 1).astype(F32), g,
              _shift_rows(g, 1) * (col <= width - 2).astype(F32))
    out = None
    for di in (-1, 0, 1):
        if rows == 1 and di != 0:
            continue
        r = sum(g_cols[dj] * cw_ref[pl.ds(3 * (di + 1) + dj, 1), :] for dj in range(3))
        if di != 0:
            ok = (row + di >= 0) & (row + di <= rows - 1)
            r = _shift_rows(r, di * width) * ok.astype(F32)
        out = r if out is None else out + r
    return out


def _ffn_kernel(x_ref, ng_ref, sc_ref, sh_ref, g2_ref, wa_ref, wg_ref, cw_ref, cb_ref, wd_ref, nf_ref,
                y_ref, h_scr, acc_scr, *, seq, rows, final_norm):
    j = pl.program_id(1)

    @pl.when(j == 0)
    def _():
        h_scr[...] = _norm_mod(x_ref[...], ng_ref[...], sc_ref[0], sh_ref[0]).astype(BF16)
        acc_scr[...] = jnp.zeros_like(acc_scr)

    tm = h_scr.shape[0]
    sub = tm // FFN_SUB_TILES
    parts = [pl.ds(r, sub) for r in range(0, tm, sub)]
    wa, wg, wd = (w[...].astype(BF16) for w in (wa_ref, wg_ref, wd_ref))
    gs = [jnp.dot(h_scr[p, :], wg, preferred_element_type=F32) for p in parts]
    as_ = [jnp.dot(h_scr[p, :], wa, preferred_element_type=F32) for p in parts]
    per_seq = max(seq // sub, 1)
    convs = []
    for i in range(0, len(parts), per_seq):
        g = gs[i] if per_seq == 1 else jnp.concatenate(gs[i:i + per_seq], axis=0)
        c = _dwconv_tokens(g, cw_ref, seq, rows) + cb_ref[...]
        convs += [c[k * sub:(k + 1) * sub] for k in range(per_seq)]
    acts = [(jax.nn.silu(c) * a).astype(BF16) for c, a in zip(convs, as_)]
    for p, act in zip(parts, acts):
        acc_scr[p, :] += jnp.dot(act, wd, preferred_element_type=F32)

    @pl.when(j == pl.num_programs(1) - 1)
    def _():
        y = x_ref[...] + g2_ref[0] * acc_scr[...]
        if final_norm:
            y = _rms(y, nf_ref[...])
        y_ref[...] = y


def _conv_ffn(x, seq, rows, layer, norm_g, mod, cond_of_tile, w_up, conv_w, conv_b, w_down, norm_final, final_norm):
    n_tok = x.shape[0]
    tm, tf = FFN_TOKEN_TILE, FFN_FF_TILE
    n_ff_tiles = D_FF // tf
    depth = w_up.shape[0]
    return pl.pallas_call(
        functools.partial(_ffn_kernel, seq=seq, rows=rows, final_norm=final_norm),
        grid=(n_tok // tm, n_ff_tiles),
        in_specs=[pl.BlockSpec((tm, D_MODEL), lambda i, j: (i, 0)),
                  pl.BlockSpec((1, D_MODEL), lambda i, j: (0, 0)),
                  _mod_spec(4, cond_of_tile), _mod_spec(3, cond_of_tile), _mod_spec(5, cond_of_tile),
                  pl.BlockSpec((None, D_MODEL, tf), lambda i, j: (layer, 0, j)),
                  pl.BlockSpec((None, D_MODEL, tf), lambda i, j: (layer, 0, n_ff_tiles + j)),
                  pl.BlockSpec((None, 9, tf), lambda i, j: (layer, 0, j)),
                  pl.BlockSpec((None, 1, tf), lambda i, j: (layer, 0, j)),
                  pl.BlockSpec((None, tf, D_MODEL), lambda i, j: (layer, j, 0)),
                  pl.BlockSpec((1, D_MODEL), lambda i, j: (0, 0))],
        out_specs=pl.BlockSpec((tm, D_MODEL), lambda i, j: (i, 0)),
        out_shape=jax.ShapeDtypeStruct((n_tok, D_MODEL), F32),
        scratch_shapes=[pltpu.VMEM((tm, D_MODEL), BF16), pltpu.VMEM((tm, D_MODEL), F32)],
        compiler_params=_params("parallel", "arbitrary"),
        name="conv_ffn",
    )(x, norm_g.reshape(1, D_MODEL), mod, mod, mod, w_up, w_up, conv_w.reshape(depth, 9, D_FF),
      conv_b.reshape(depth, 1, D_FF), w_down, norm_final.reshape(1, D_MODEL))


GroupMasks = collections.namedtuple("GroupMasks", "r c same incl strict")


def _chunk_masks():
    r = lax.broadcasted_iota(jnp.int32, (CHUNK, CHUNK), 0)
    c = lax.broadcasted_iota(jnp.int32, (CHUNK, CHUNK), 1)
    return (r >= c, r <= c)


def _group_masks(group=GROUP):
    r = lax.broadcasted_iota(jnp.int32, (group, group), 0)
    c = lax.broadcasted_iota(jnp.int32, (group, group), 1)
    same = (r >> LOG_CHUNK) == (c >> LOG_CHUNK)
    return GroupMasks(r, c, same, (same & (r >= c), same & (r <= c)), (same & (r > c), same & (r < c)))


def _gate_layouts(gt, batch, n_heads, hp, group=GROUP):
    seq = gt.shape[1] // batch
    n_groups, n_chunks = seq // group, seq // CHUNK
    assert seq % group == 0 and n_groups <= 8
    g = jnp.transpose(gt.reshape(N_DIR, 2, n_heads, batch, seq), (3, 2, 0, 1, 4))
    rows = g.reshape(g.shape[:4] + (n_groups, group))
    rows = jnp.pad(rows, ((0, 0),) * 4 + ((0, 8 - n_groups), (0, 0)))
    cols = jnp.swapaxes(g.reshape(batch, n_heads // hp, hp * N_DIR * 2 * n_chunks, CHUNK), -1, -2)
    return rows, cols


def _col_gate(gcol_ref, hh, d, kind, n_chunks):
    start = ((hh * N_DIR + d) * 2 + kind) * n_chunks
    return gcol_ref[0, 0, :, start:start + n_chunks]


def _split3(x):
    x1 = x.astype(BF16)
    r1 = x - x1.astype(F32)
    x2 = r1.astype(BF16)
    return x1, x2, (r1 - x2.astype(F32)).astype(BF16)


def _dot_mask_rhs(x, mask):
    m = x.shape[0]
    y = jnp.dot(jnp.concatenate(_split3(x), axis=0), jnp.where(mask, 1.0, 0.0).astype(BF16),
                preferred_element_type=F32)
    return y[:m] + y[m:2 * m] + y[2 * m:]


def _dot_mask_lhs(mask, x):
    mb = jnp.where(mask, 1.0, 0.0).astype(BF16)
    y1, y2, y3 = (jnp.dot(mb, p, preferred_element_type=F32) for p in _split3(x))
    return y1 + y2 + y3


def _cumsum_rows(row, d, gm):
    return _dot_mask_rhs(row, gm.incl[1 - d])


def _cumsum_cols(col, d, cm):
    return _dot_mask_lhs(cm[d], col)


def _store_cols(dst, col, n_chunks):
    for n in range(n_chunks):
        dst[n * CHUNK:(n + 1) * CHUNK, :] = jnp.broadcast_to(col[:, n:n + 1], (CHUNK, 128))


def _wide(x, width=None):
    reps = (GROUP if width is None else width) // x.shape[1]
    return x if reps == 1 else jnp.concatenate([x] * reps, axis=1)


def _rows(i, size):
    if isinstance(i, int):
        return pl.ds(i * size, size)
    return pl.ds(pl.multiple_of(i * size, size), size)


def _loop(n, body, init):
    if n == 1:
        return body(0, init)
    return lax.fori_loop(0, n, body, init)


def _conv_silu(x, w_ref):
    n = x.shape[0]
    k = w_ref.shape[0]
    t = lax.broadcasted_iota(jnp.int32, (n, 1), 0)
    acc = None
    for j in range(k):
        delta = j - k // 2
        if delta == 0:
            term = x
        else:
            ok = (t + delta >= 0) & (t + delta <= n - 1)
            term = _shift_rows(x, delta) * ok.astype(F32)
        term = term * w_ref[pl.ds(j, 1), :]
        acc = term if acc is None else acc + term
    return jax.nn.silu(acc)


def _l2norm(x):
    return x * lax.rsqrt(jnp.sum(x * x, axis=-1, keepdims=True) + EPS)


def _blockdiag_tri_inverse(mats, gm):
    dot16 = lambda x, y: jnp.dot(x, y, preferred_element_type=F32).astype(BF16)
    base = (gm.r >> 3) == (gm.c >> 3)
    eye = jnp.where(gm.r == gm.c, 1.0, 0.0)
    zero16 = jnp.zeros((), BF16)
    a16s = [a.astype(BF16) for a in mats]
    invs = [eye - jnp.where(base, a, 0.0) for a in mats]
    ps = [jnp.where(base, -a16, zero16) for a16 in a16s]
    for _ in range(2):
        ps = [dot16(p, p) for p in ps]
        invs = [inv + jnp.dot(inv.astype(BF16), p, preferred_element_type=F32) for inv, p in zip(invs, ps)]
    for s in (3, 4, 5):
        join = ((gm.r >> s) ^ (gm.c >> s)) == 1
        inv16s = [inv.astype(BF16) for inv in invs]
        tmp = [dot16(jnp.where(join, a16, zero16), inv16) for a16, inv16 in zip(a16s, inv16s)]
        invs = [inv - jnp.dot(inv16, t, preferred_element_type=F32) for inv, inv16, t in zip(invs, inv16s, tmp)]
    return invs


def _gdn_kernel(*refs, seq, hp, zero_init, emit_state):
    alog_ref, dtb_ref, q_ref, k_ref, v_ref, cwq_ref, cwk_ref, cwv_ref, grow_ref, gcol_ref = refs[:10]
    pos = 10
    s0_ref = None
    if not zero_init:
        s0_ref = refs[pos]
        pos += 1
    o_ref = refs[pos]
    pos += 1
    sfin_ref = None
    if emit_state:
        sfin_ref = refs[pos]
        pos += 1
    (q_scr, k_scr, v_scr, gcr_scr, gcb_scr, bb_scr, gl_scr,
     u_scr, w_scr, qd_scr, kd_scr, qkd_scr) = refs[pos:]

    n_chunks = seq // CHUNK
    grp = GDN_GROUP
    cm = _chunk_masks()
    gm = _group_masks(grp)
    head_dirs = [(hh, d) for hh in range(hp) for d in range(N_DIR)]
    lanes = lambda hh: slice(hh * DN_DK, (hh + 1) * DN_DK)

    q_scr[...] = _conv_silu(q_ref[...].astype(F32), cwq_ref)
    k_scr[...] = _conv_silu(k_ref[...].astype(F32), cwk_ref)
    v_scr[...] = _conv_silu(v_ref[...].astype(F32), cwv_ref)
    for hh in range(hp):
        q_scr[:, lanes(hh)] = _l2norm(q_scr[:, lanes(hh)]) * (DN_DK ** -0.5)
        k_scr[:, lanes(hh)] = _l2norm(k_scr[:, lanes(hh)])

    for hh, d in head_dirs:
        head = pl.program_id(1) * hp + hh
        neg_a = -jnp.exp(jnp.full((1, 1), alog_ref[d, head], F32))
        dtb = dtb_ref[d, head]
        gcr_scr[hh, d] = _cumsum_rows(neg_a * _softplus(grow_ref[0, hh, d, 0] + dtb), d, gm)
        gc_col = _cumsum_cols(neg_a * _softplus(_col_gate(gcol_ref, hh, d, 0, n_chunks) + dtb), d, cm)
        _store_cols(gcb_scr.at[hh, d], gc_col, n_chunks)
        _store_cols(bb_scr.at[hh, d], jax.nn.sigmoid(_col_gate(gcol_ref, hh, d, 1, n_chunks)), n_chunks)
        last = CHUNK - 1 if d == 0 else 0
        _store_cols(gl_scr.at[hh, d], jnp.broadcast_to(gc_col[last:last + 1, :], gc_col.shape), n_chunks)

    gps = min(GDN_GROUPS_PER_STEP, seq // grp)

    def group_step(it, carry):
        gis = [it * gps + k for k in range(gps)]
        rows = [_rows(gi, grp) for gi in gis]
        for h0 in range(0, hp, GDN_HEADS_PER_PASS):
            heads = range(h0, min(h0 + GDN_HEADS_PER_PASS, hp))
            kqs = {}
            for k, r in enumerate(rows):
                for hh in heads:
                    kg, qg = k_scr[r, lanes(hh)], q_scr[r, lanes(hh)]
                    kqs[k, hh] = _dot_nt(jnp.concatenate([kg, qg], axis=0), kg)
            chains = [(k, hh, d) for k in range(len(gis)) for hh in heads for d in range(N_DIR)]
            mats = []
            for k, hh, d in chains:
                gcb = gcb_scr[hh, d, rows[k], :]
                gcr = gcr_scr[hh, d, pl.ds(gis[k], 1), :]
                decay = jnp.exp(jnp.where(gm.incl[d], _wide(gcb, grp) - gcr, -jnp.inf))
                mats.append(jnp.where(gm.strict[d],
                                      kqs[k, hh][:grp] * _wide(bb_scr[hh, d, rows[k], :], grp) * decay, 0.0))
                qkd = kqs[k, hh][grp:] * decay
                qkd_scr[hh, d, rows[k], :] = sum(qkd[:, n * CHUNK:(n + 1) * CHUNK] for n in range(grp // CHUNK))
            t_invs = _blockdiag_tri_inverse(mats, gm)
            for (k, hh, d), t_inv in zip(chains, t_invs):
                r = rows[k]
                qg, kg, vg = q_scr[r, lanes(hh)], k_scr[r, lanes(hh)], v_scr[r, lanes(hh)]
                gcb = gcb_scr[hh, d, r, :]
                bb = bb_scr[hh, d, r, :]
                e_gc = jnp.exp(gcb)
                uw = _dot(t_inv, jnp.concatenate([vg * bb, kg * bb * e_gc], axis=1))
                u_scr[hh, d, r, :] = uw[:, :DN_DV]
                w_scr[hh, d, r, :] = uw[:, DN_DV:]
                qd_scr[hh, d, r, :] = qg * e_gc
                kd_scr[hh, d, r, :] = kg * jnp.exp(gl_scr[hh, d, r, :] - gcb)
        return carry

    _loop(seq // (grp * gps), group_step, 0)

    o_ref[...] = jnp.zeros_like(o_ref)

    def chunk_step(n, carry):
        idxs = [n if d == 0 else n_chunks - 1 - n for _, d in head_dirs]
        rows = [_rows(idx, CHUNK) for idx in idxs]
        wqs = [_dot(jnp.concatenate([w_scr[hh, d, r, :], qd_scr[hh, d, r, :]], axis=0), s)
               for (hh, d), r, s in zip(head_dirs, rows, carry)]
        v_news = [u_scr[hh, d, r, :] - wq[:CHUNK] for (hh, d), r, wq in zip(head_dirs, rows, wqs)]
        o_ns = [wq[CHUNK:] + _dot(qkd_scr[hh, d, r, :], v_new)
                for (hh, d), r, wq, v_new in zip(head_dirs, rows, wqs, v_news)]
        new = []
        for (hh, d), idx, r, s, v_new in zip(head_dirs, idxs, rows, carry, v_news):
            s_decay = jnp.exp(gl_scr[hh, d, pl.ds(idx * CHUNK, 1), :])
            new.append(s * s_decay + _dot_tn(kd_scr[hh, d, r, :], v_new))
        for (hh, d), r, o_n in zip(head_dirs, rows, o_ns):
            o_ref[r, lanes(hh)] += o_n
        return tuple(new)

    if zero_init:
        init = (jnp.zeros((DN_DK, DN_DV), F32),) * len(head_dirs)
    else:
        init = tuple(s0_ref[0, d, hh] for hh, d in head_dirs)
    fin = lax.fori_loop(0, n_chunks, chunk_step, init)
    if emit_state:
        for i, (hh, d) in enumerate(head_dirs):
            sfin_ref[0, d, hh] = fin[i]


def _gdn_scan(main, gt, batch, seq, conv_w, a_log, dt_bias, s0, emit_state):
    n_tok = batch * seq
    nh = DN_HEADS
    hp = min(nh, max(2, GDN_STEP_POSITIONS // seq))
    n_hb = nh // hp
    assert seq % (GDN_GROUP * min(GDN_GROUPS_PER_STEP, seq // GDN_GROUP)) == 0
    grow, gcol = _gate_layouts(gt, batch, nh, hp, GDN_GROUP)
    zero_init = s0 is None
    smem = pl.BlockSpec(memory_space=pltpu.SMEM)
    qkv_spec = lambda off: pl.BlockSpec((seq, hp * DN_DK), lambda b, h: (b, off + h))
    cw_spec = lambda off: pl.BlockSpec((DN_CONV, hp * DN_DK), lambda b, h: (0, off + h))
    state_spec = pl.BlockSpec((1, N_DIR, hp, DN_DK, DN_DV), lambda b, h: (b, 0, h, 0, 0))
    in_specs = [smem, smem, qkv_spec(0), qkv_spec(n_hb), qkv_spec(2 * n_hb),
                cw_spec(0), cw_spec(n_hb), cw_spec(2 * n_hb),
                pl.BlockSpec((1, hp) + grow.shape[2:], lambda b, h: (b, h, 0, 0, 0, 0)),
                pl.BlockSpec((1, 1) + gcol.shape[2:], lambda b, h: (b, h, 0, 0))]
    args = [a_log, dt_bias, main, main, main, conv_w, conv_w, conv_w, grow, gcol]
    if not zero_init:
        in_specs.append(state_spec)
        args.append(s0)
    out_specs = [pl.BlockSpec((seq, hp * DN_DV), lambda b, h: (b, h))]
    out_shape = [jax.ShapeDtypeStruct((n_tok, nh * DN_DV), F32)]
    if emit_state:
        out_specs.append(state_spec)
        out_shape.append(jax.ShapeDtypeStruct((batch, N_DIR, nh, DN_DK, DN_DV), F32))
    per_dir = lambda width: pltpu.VMEM((hp, N_DIR, seq, width), F32)
    qkv_scr = pltpu.VMEM((seq, hp * DN_DK), F32)
    outs = pl.pallas_call(
        functools.partial(_gdn_kernel, seq=seq, hp=hp, zero_init=zero_init, emit_state=emit_state),
        grid=(batch, n_hb),
        in_specs=in_specs, out_specs=out_specs, out_shape=out_shape,
        scratch_shapes=[qkv_scr, qkv_scr, qkv_scr,
                        pltpu.VMEM((hp, N_DIR, 8, GDN_GROUP), F32),
                        per_dir(128), per_dir(128), per_dir(128),
                        per_dir(DN_DV), per_dir(DN_DK), per_dir(DN_DK), per_dir(DN_DK), per_dir(CHUNK)],
        compiler_params=_params("parallel", "parallel"),
        name="gdn_scan",
    )(*args)
    return outs[0], (outs[1] if emit_state else None)


ML_AUG = 2 * ML_DV


def _mlstm_kernel(*refs, seq, hp, zero_init, emit_state):
    bi_ref, bf_ref, q_ref, k_ref, v_ref, kt_ref, grow_ref, gcol_ref = refs[:8]
    pos = 8
    c0_ref = m0_ref = None
    if not zero_init:
        c0_ref, m0_ref = refs[pos:pos + 2]
        pos += 2
    o_ref = refs[pos]
    pos += 1
    cfin_ref = nfin_ref = mfin_ref = None
    if emit_state:
        cfin_ref, nfin_ref, mfin_ref = refs[pos:pos + 3]
        pos += 3
    bcr_scr, lir_scr, kwr_scr, bcb_scr, mi_scr, dec_scr, dc_scr, cin_scr = refs[pos:]

    n_chunks = seq // CHUNK
    n_groups = seq // GROUP
    head0 = pl.program_id(1) * hp
    cm = _chunk_masks()
    gm = _group_masks()
    ones_col = jnp.where(lax.broadcasted_iota(jnp.int32, (GROUP, ML_DV), 1) == 0, 1.0, 0.0)
    head_dirs = [(hh, d) for hh in range(hp) for d in range(N_DIR)]

    m_fin = {}
    row_group = lax.broadcasted_iota(jnp.int32, (8, GROUP), 0)
    row_chunk = lax.broadcasted_iota(jnp.int32, (8, GROUP), 1) >> LOG_CHUNK
    for hh, d in head_dirs:
        b_i = bi_ref[d, head0 + hh]
        b_f = bf_ref[d, head0 + hh]
        li_row = grow_ref[0, hh, d, 0] + b_i
        lf_row = -_softplus(-(grow_ref[0, hh, d, 1] + b_f))
        bc_row = _cumsum_rows(lf_row, d, gm)
        lir_scr[hh, d] = li_row
        bcr_scr[hh, d] = bc_row
        w_row = _dot_mask_rhs(lf_row, gm.same) - bc_row + li_row
        m_out_row = jnp.zeros((8, GROUP), F32)
        li_col = _col_gate(gcol_ref, hh, d, 0, n_chunks) + b_i
        bc_col = _cumsum_cols(-_softplus(-(_col_gate(gcol_ref, hh, d, 1, n_chunks) + b_f)), d, cm)
        last = CHUNK - 1 if d == 0 else 0
        b_last = bc_col[last:last + 1, :]
        w_col = b_last - bc_col + li_col
        w_max = jnp.max(w_col, axis=0, keepdims=True)
        m = jnp.zeros((1, 1), F32) if zero_init else m0_ref[0, d, hh][:, 0:1]
        for step in range(n_chunks):
            n = step if d == 0 else n_chunks - 1 - step
            sl = slice(n * CHUNK, (n + 1) * CHUNK)
            m_new = jnp.maximum(b_last[:, n:n + 1] + m, w_max[:, n:n + 1])
            bcb_scr[hh, d, sl, :] = jnp.broadcast_to(bc_col[:, n:n + 1], (CHUNK, 128))
            mi_scr[hh, d, sl, :] = jnp.broadcast_to(m, (CHUNK, 128))
            dec_scr[hh, d, n:n + 1, :] = jnp.broadcast_to(jnp.exp(b_last[:, n:n + 1] + m - m_new), (1, 128))
            in_chunk = (row_group == n // CHUNKS_PER_GROUP) & (row_chunk == n % CHUNKS_PER_GROUP)
            m_out_row = jnp.where(in_chunk, m_new, m_out_row)
            m = m_new
        m_fin[hh, d] = m
        kwr_scr[hh, d] = jnp.exp(w_row - m_out_row)

    def load_kv(rows, hh):
        kg = k_ref[rows, hh * ML_DQK:(hh + 1) * ML_DQK].astype(F32)
        v_aug = jnp.concatenate([v_ref[rows, hh * ML_DV:(hh + 1) * ML_DV].astype(F32), ones_col], axis=1)
        return kg, v_aug

    def delta_step(gi, carry):
        rows = _rows(gi, GROUP)
        v_augs = [load_kv(rows, hh)[1] for hh in range(hp)]
        kt = kt_ref[gi]
        lhs = []
        for hh, d in head_dirs:
            kw_t = kt[hh * ML_DQK:(hh + 1) * ML_DQK, :] * kwr_scr[hh, d, pl.ds(gi, 1), :]
            lhs.append(jnp.where(gm.same, jnp.concatenate([kw_t] * CHUNKS_PER_GROUP, axis=0), 0.0))
        for (hh, d), kw in zip(head_dirs, lhs):
            dc_scr[hh, d, rows, :] = _dot(kw, v_augs[hh])
        return carry

    _loop(n_groups, delta_step, 0)

    def prefix_step(n, carry):
        new = []
        for (hh, d), c_aug in zip(head_dirs, carry):
            idx = n if d == 0 else n_chunks - 1 - n
            rows = _rows(idx, CHUNK)
            cin_scr[hh, d, rows, :] = c_aug
            new.append(c_aug * _wide(dec_scr[hh, d, pl.ds(idx, 1), :], ML_AUG) + dc_scr[hh, d, rows, :])
        return tuple(new)

    if zero_init:
        init = (jnp.zeros((ML_DQK, ML_AUG), F32),) * len(head_dirs)
    else:
        init = tuple(c0_ref[0, d, hh] for hh, d in head_dirs)
    fin = lax.fori_loop(0, n_chunks, prefix_step, init)

    def out_step(gi, carry):
        rows = _rows(gi, GROUP)
        kvs = [load_kv(rows, hh) for hh in range(hp)]
        qgs = [q_ref[rows, hh * ML_DQK:(hh + 1) * ML_DQK].astype(F32) * (ML_DQK ** -0.5) for hh in range(hp)]
        qks = [_dot_nt(qg, kv[0]) for qg, kv in zip(qgs, kvs)]
        chunk = lambda n: slice(n * CHUNK, (n + 1) * CHUNK)
        inters = []
        for hh, d in head_dirs:
            c_in = cin_scr[hh, d, rows, :]
            inters.append(jnp.concatenate(
                [_dot(qgs[hh][chunk(n)], c_in[chunk(n)]) for n in range(CHUNKS_PER_GROUP)], axis=0))
        m_ts, b_ms, ss = [], [], []
        for hh, d in head_dirs:
            b_colb = bcb_scr[hh, d, rows, :]
            b_m = b_colb[:, 0:1] + mi_scr[hh, d, rows, :][:, 0:1]
            d_log = jnp.where(gm.incl[d], _wide(b_colb) - bcr_scr[hh, d, pl.ds(gi, 1), :]
                              + lir_scr[hh, d, pl.ds(gi, 1), :], -jnp.inf)
            m_t = jnp.maximum(b_m, jnp.max(d_log, axis=-1, keepdims=True))
            m_ts.append(m_t)
            b_ms.append(b_m)
            ss.append(qks[hh] * jnp.exp(d_log - m_t))
        intras = [_dot(s, kvs[hh][1]) for (hh, d), s in zip(head_dirs, ss)]
        hs = []
        for inter, intra, m_t, b_m in zip(inters, intras, m_ts, b_ms):
            num = jnp.exp(b_m - m_t) * inter + intra
            den = num[:, ML_DV:ML_DV + 1]
            hs.append(num[:, :ML_DV] / jnp.maximum(jnp.abs(den), jnp.exp(-m_t)))
        for hh in range(hp):
            o_ref[rows, hh * ML_DV:(hh + 1) * ML_DV] = hs[N_DIR * hh] + hs[N_DIR * hh + 1]
        return carry

    _loop(n_groups, out_step, 0)

    if emit_state:
        for i, (hh, d) in enumerate(head_dirs):
            cfin_ref[0, d, hh] = fin[i][:, :ML_DV]
            nfin_ref[0, d, hh] = fin[i][:, ML_DV:]
            mfin_ref[0, d, hh] = jnp.broadcast_to(m_fin[hh, d], (1, 128))


def _mlstm_scan(main, kt, gt, batch, seq, b_i, b_f, state0, emit_state):
    n_tok = batch * seq
    n_groups = seq // GROUP
    n_chunks = seq // CHUNK
    nh = ML_HEADS
    hp = min(nh, max(2, SCAN_STEP_POSITIONS // seq))
    n_hb = nh // hp
    grow, gcol = _gate_layouts(gt, batch, nh, hp)
    zero_init = state0 is None
    smem = pl.BlockSpec(memory_space=pltpu.SMEM)
    k_off = (nh * ML_DQK) // (hp * ML_DQK)
    v_off = (2 * nh * ML_DQK) // (hp * ML_DV)
    c_spec = pl.BlockSpec((1, N_DIR, hp, ML_DQK, ML_AUG), lambda b, p: (b, 0, p, 0, 0))
    m_spec = pl.BlockSpec((1, N_DIR, hp, 1, 128), lambda b, p: (b, 0, p, 0, 0))
    in_specs = [smem, smem,
                pl.BlockSpec((seq, hp * ML_DQK), lambda b, p: (b, p)),
                pl.BlockSpec((seq, hp * ML_DQK), lambda b, p: (b, k_off + p)),
                pl.BlockSpec((seq, hp * ML_DV), lambda b, p: (b, v_off + p)),
                pl.BlockSpec((n_groups, hp * ML_DQK, GROUP), lambda b, p: (b, p, 0)),
                pl.BlockSpec((1, hp) + grow.shape[2:], lambda b, p: (b, p, 0, 0, 0, 0)),
                pl.BlockSpec((1, 1) + gcol.shape[2:], lambda b, p: (b, p, 0, 0))]
    args = [b_i, b_f, main, main, main, kt, grow, gcol]
    if not zero_init:
        in_specs += [c_spec, m_spec]
        args += list(state0)
    out_specs = [pl.BlockSpec((seq, hp * ML_DV), lambda b, p: (b, p))]
    out_shape = [jax.ShapeDtypeStruct((n_tok, nh * ML_DV), F32)]
    if emit_state:
        half_spec = pl.BlockSpec((1, N_DIR, hp, ML_DQK, ML_DV), lambda b, p: (b, 0, p, 0, 0))
        out_specs += [half_spec, half_spec, m_spec]
        out_shape += [jax.ShapeDtypeStruct((batch, N_DIR, nh, ML_DQK, ML_DV), F32),
                      jax.ShapeDtypeStruct((batch, N_DIR, nh, ML_DQK, ML_AUG - ML_DV), F32),
                      jax.ShapeDtypeStruct((batch, N_DIR, nh, 1, 128), F32)]
    per_hd = lambda rows, width: pltpu.VMEM((hp, N_DIR, rows, width), F32)
    outs = pl.pallas_call(
        functools.partial(_mlstm_kernel, seq=seq, hp=hp, zero_init=zero_init, emit_state=emit_state),
        grid=(batch, n_hb),
        in_specs=in_specs, out_specs=out_specs, out_shape=out_shape,
        scratch_shapes=[per_hd(8, GROUP), per_hd(8, GROUP), per_hd(8, GROUP),
                        per_hd(seq, 128), per_hd(seq, 128),
                        per_hd(max(n_chunks, 8), 128),
                        per_hd(seq, ML_AUG), per_hd(seq, ML_AUG)],
        compiler_params=_params("parallel", "parallel"),
        name="mlstm_scan",
    )(*args)
    return outs[0], (tuple(outs[1:4]) if emit_state else None)


def _trunk(x, batch, seq, rows, mods, tokens_per_cond, st_d, st_ml, emit_state, p):
    depth = p["w_ada"].shape[0]
    tile_cond = lambda tile: (lambda i: (i * tile) // tokens_per_cond)
    new_d = new_ml = None
    for layer in range(depth):
        mod = mods[layer]
        kind, j = layer % 3, layer // 3
        if kind == 0:
            x = _fnet_mix(x, seq, j, p["norm_mix"][layer], mod, tokens_per_cond, p["fnet_w"], p["fnet_b"][j])
        elif kind == 1:
            n_main = DN_HEADS * (2 * DN_DK + 2 * DN_DV)
            main, gt = _norm_proj(x, p["norm_mix"][layer], mod, tokens_per_cond, p["dn_w_in"], j, n_main)
            s0 = None if st_d is None else st_d[:, j]
            o, sfin = _gdn_scan(main, gt, batch, seq, p["dn_conv_w"][j], p["dn_a_log"][j], p["dn_dt_bias"][j],
                                s0, emit_state)
            if emit_state:
                new_d = sfin
            x = _gated_out(x, o, main, 3, p["dn_norm"][j], mod, tile_cond(OUT_TOKEN_TILE), p["dn_w_out"][j],
                           DN_HEADS, jax.nn.silu)
        else:
            n_main = 2 * ML_HEADS * ML_DQK + 2 * ML_HEADS * ML_DV
            k_tile = (ML_HEADS * ML_DQK) // PROJ_COL_TILE
            assert ML_HEADS * ML_DQK == PROJ_COL_TILE
            main, gt, kt = _norm_proj(x, p["norm_mix"][layer], mod, tokens_per_cond, p["ml_w_in"], j, n_main,
                                      t_tile=k_tile)
            state0 = None
            if st_ml is not None:
                c0, n0, m0 = (s[:, j] for s in st_ml)
                pad = jnp.zeros(c0.shape[:-1] + (ML_AUG - ML_DV - 1,), F32)
                c_aug0 = jnp.concatenate([c0, n0[..., None], pad], axis=-1)
                m0b = jnp.broadcast_to(m0[..., None, None], m0.shape + (1, 128))
                state0 = (c_aug0, m0b)
            o, fin = _mlstm_scan(main, kt, gt, batch, seq, p["ml_b_i"][j], p["ml_b_f"][j], state0, emit_state)
            if emit_state:
                new_ml = (fin[0], fin[1][..., 0], fin[2][..., 0, 0])
            x = _gated_out(x, o, main, 2, p["ml_norm"][j], mod, tile_cond(OUT_TOKEN_TILE), p["ml_w_out"][j],
                           ML_HEADS, jax.nn.sigmoid)
        x = _conv_ffn(x, seq, rows, layer, p["norm_ffn"][layer], mod, tile_cond(FFN_TOKEN_TILE), p["ffn_w_up"],
                      p["ffn_conv_w"], p["ffn_conv_b"], p["ffn_w_down"], p["norm_final"],
                      final_norm=(layer == depth - 1))
    return x, new_d, new_ml


def kernel(x_prompt, x_sample, state_delta, state_mlstm_c, state_mlstm_n, state_mlstm_m, c, c_ctx, w_ada, b_ada, norm_mix, norm_ffn, norm_final, ffn_w_up, ffn_conv_w, ffn_conv_b, ffn_w_down, fnet_w, fnet_b, dn_w_in, dn_conv_w, dn_a_log, dn_dt_bias, dn_norm, dn_w_out, ml_w_in, ml_b_i, ml_b_f, ml_norm, ml_w_out):
    p = dict(w_ada=w_ada, norm_mix=norm_mix, norm_ffn=norm_ffn, norm_final=norm_final, ffn_w_up=ffn_w_up,
             ffn_conv_w=ffn_conv_w, ffn_conv_b=ffn_conv_b, ffn_w_down=ffn_w_down, fnet_w=fnet_w, fnet_b=fnet_b,
             dn_w_in=dn_w_in, dn_conv_w=dn_conv_w, dn_a_log=dn_a_log, dn_dt_bias=dn_dt_bias, dn_norm=dn_norm,
             dn_w_out=dn_w_out, ml_w_in=ml_w_in, ml_b_i=ml_b_i, ml_b_f=ml_b_f, ml_norm=ml_norm, ml_w_out=ml_w_out)
    b_ctx, t_ctx, _ = x_prompt.shape
    b_smp, t_smp, _ = x_sample.shape
    depth = w_ada.shape[0]

    cond8 = jnp.concatenate([c_ctx[None, :], c, jnp.zeros((8 - 1 - b_smp, D_MODEL), F32)], axis=0)
    mods = _adaln_all(cond8, w_ada, b_ada)
    mods_ctx = [mods[l, 0:1].reshape(1, 1, -1) for l in range(depth)]
    mods_smp = [mods[l, 1:1 + b_smp].reshape(b_smp, 1, -1) for l in range(depth)]

    y_ctx, new_d, new_ml = _trunk(x_prompt.reshape(b_ctx * t_ctx, D_MODEL), b_ctx, t_ctx, 1, mods_ctx,
                                  b_ctx * t_ctx, None, None, True, p)
    y_smp, _, _ = _trunk(x_sample.reshape(b_smp * t_smp, D_MODEL), b_smp, t_smp, t_smp // GRID_W, mods_smp,
                         t_smp, state_delta, (state_mlstm_c, state_mlstm_n, state_mlstm_m), False, p)

    new_c, new_n, new_m = new_ml
    return (y_ctx.reshape(b_ctx, t_ctx, D_MODEL), y_smp.reshape(b_smp, t_smp, D_MODEL),
            new_d[:, None], new_c[:, None], new_n[:, None], new_m[:, None])
```

```python
import collections
import functools

import numpy as np
import jax
import jax.numpy as jnp
from jax import lax
from jax.experimental import pallas as pl
from jax.experimental.pallas import tpu as pltpu

F32 = jnp.float32
BF16 = jnp.bfloat16

D_MODEL = 1024
EPS = 1e-6
N_DIR = 2
CHUNK = 64
LOG_CHUNK = 6
GROUP = 256
CHUNKS_PER_GROUP = GROUP // CHUNK
GDN_GROUP = 128
GDN_GROUPS_PER_STEP = 4
GDN_HEADS_PER_PASS = 4
GDN_STEP_POSITIONS = 2048
FNET_GROUP_DIM = 256
DN_HEADS, DN_DK, DN_DV, DN_CONV = 8, 128, 128, 5
ML_HEADS, ML_DQK, ML_DV = 8, 64, 128
D_FF = 2816
GRID_W = 64

FNET_STEP_TOKENS = 1024
FFN_TOKEN_TILE = 1024
FFN_FF_TILE = 256
FFN_SUB_TILES = 4
PROJ_TOKEN_TILE = 2048
PROJ_COL_TILE = 512
OUT_TOKEN_TILE = 512
ADA_COL_TILE = 1536
SCAN_STEP_POSITIONS = 1024
VMEM_LIMIT = 56 * 1024 * 1024


def _params(*sem):
    return pltpu.CompilerParams(dimension_semantics=sem, vmem_limit_bytes=VMEM_LIMIT)


def _dot(a, b):
    return jnp.dot(a.astype(BF16), b.astype(BF16), preferred_element_type=F32)


def _dot_nt(a, b):
    return lax.dot_general(a.astype(BF16), b.astype(BF16), (((1,), (1,)), ((), ())),
                           preferred_element_type=F32)


def _dot_tn(a, b):
    return lax.dot_general(a.astype(BF16), b.astype(BF16), (((0,), (0,)), ((), ())),
                           preferred_element_type=F32)


def _rms(x, g):
    return x * lax.rsqrt(jnp.mean(x * x, axis=-1, keepdims=True) + EPS) * g


def _norm_mod(x, g, sc, sh):
    return _rms(x, g) * (1.0 + sc) + sh


def _softplus(x):
    return jnp.maximum(x, 0.0) + jnp.log1p(jnp.exp(-jnp.abs(x)))


def _mod_spec(chunk, cond_of_tile):
    return pl.BlockSpec((1, 1, D_MODEL), lambda i, *_: (cond_of_tile(i), 0, chunk))


def _adaln_kernel(c_ref, w_ref, b_ref, o_ref):
    s = jax.nn.silu(c_ref[...])
    o_ref[0] = _dot(s, w_ref[0]) + b_ref[0]


def _adaln_all(cond8, w_ada, b_ada):
    depth = w_ada.shape[0]
    n_out = w_ada.shape[2]
    tn = ADA_COL_TILE
    return pl.pallas_call(
        _adaln_kernel,
        grid=(depth, n_out // tn),
        in_specs=[pl.BlockSpec((8, D_MODEL), lambda l, j: (0, 0)),
                  pl.BlockSpec((1, D_MODEL, tn), lambda l, j: (l, 0, j)),
                  pl.BlockSpec((1, 1, tn), lambda l, j: (l, 0, j))],
        out_specs=pl.BlockSpec((1, 8, tn), lambda l, j: (l, 0, j)),
        out_shape=jax.ShapeDtypeStruct((depth, 8, n_out), F32),
        compiler_params=_params("parallel", "parallel"),
        name="adaln",
    )(cond8, w_ada, b_ada.reshape(depth, 1, n_out))


def _norm_proj_kernel(x_ref, g_ref, sc_ref, sh_ref, w_ref, wg_ref, o_ref, gate_ref, *rest, t_tile):
    h_scr = rest[-1]

    @pl.when(pl.program_id(1) == 0)
    def _():
        h = _norm_mod(x_ref[...], g_ref[...], sc_ref[0], sh_ref[0]).astype(BF16)
        h_scr[...] = h
        gate_ref[...] = _dot_nt(wg_ref[...], h)

    y = _dot_nt(h_scr[...], w_ref[...])
    o_ref[...] = y.astype(o_ref.dtype)

    if t_tile is not None:
        t_ref = rest[0]

        @pl.when(pl.program_id(1) == t_tile)
        def _():
            for g in range(t_ref.shape[0]):
                t_ref[g] = y[g * GROUP:(g + 1) * GROUP, :].T


def _norm_proj(x, norm_g, mod, tokens_per_cond, w_stack, layer, n_main, t_tile=None):
    n_tok = x.shape[0]
    tm, tn = min(PROJ_TOKEN_TILE, tokens_per_cond), PROJ_COL_TILE
    cond_of_tile = lambda i: (i * tm) // tokens_per_cond
    w_t = jnp.swapaxes(w_stack, 1, 2)
    n_gate = w_t.shape[1] - n_main
    assert n_main % n_gate == 0 and n_gate % 8 == 0
    out_specs = [pl.BlockSpec((tm, tn), lambda i, j: (i, j)),
                 pl.BlockSpec((n_gate, tm), lambda i, j: (0, i))]
    out_shape = [jax.ShapeDtypeStruct((n_tok, n_main), BF16),
                 jax.ShapeDtypeStruct((n_gate, n_tok), F32)]
    if t_tile is not None:
        out_specs.append(pl.BlockSpec((tm // GROUP, tn, GROUP), lambda i, j: (i, 0, 0)))
        out_shape.append(jax.ShapeDtypeStruct((n_tok // GROUP, tn, GROUP), F32))
    return pl.pallas_call(
        functools.partial(_norm_proj_kernel, t_tile=t_tile),
        grid=(n_tok // tm, n_main // tn),
        in_specs=[pl.BlockSpec((tm, D_MODEL), lambda i, j: (i, 0)),
                  pl.BlockSpec((1, D_MODEL), lambda i, j: (0, 0)),
                  _mod_spec(1, cond_of_tile), _mod_spec(0, cond_of_tile),
                  pl.BlockSpec((None, tn, D_MODEL), lambda i, j: (layer, j, 0)),
                  pl.BlockSpec((None, n_gate, D_MODEL), lambda i, j: (layer, n_main // n_gate, 0))],
        out_specs=out_specs, out_shape=out_shape,
        scratch_shapes=[pltpu.VMEM((tm, D_MODEL), BF16)],
        compiler_params=_params("parallel", "arbitrary"),
        name="norm_proj",
    )(x, norm_g.reshape(1, D_MODEL), mod, mod, w_t, w_t)


def _gated_out_kernel(x_ref, o_ref, z_ref, ng_ref, g1_ref, w_ref, y_ref, wb_scr, *, n_heads, gate_fn):
    @pl.when(pl.program_id(0) == 0)
    def _():
        wb_scr[...] = w_ref[...].astype(BF16)

    dv = o_ref.shape[1] // n_heads
    parts = []
    for h in range(n_heads):
        sl = slice(h * dv, (h + 1) * dv)
        parts.append((_rms(o_ref[:, sl], ng_ref[...]) * gate_fn(z_ref[:, sl].astype(F32))).astype(BF16))
    hs = jnp.concatenate(parts, axis=1)
    y = jnp.dot(hs, wb_scr[...], preferred_element_type=F32)
    y_ref[...] = x_ref[...] + g1_ref[0] * y


def _gated_out(x, o, main, z_block, norm_g, mod, cond_of_tile, w_out, n_heads, gate_fn):
    n_tok = x.shape[0]
    tm = OUT_TOKEN_TILE
    width = o.shape[1]
    return pl.pallas_call(
        functools.partial(_gated_out_kernel, n_heads=n_heads, gate_fn=gate_fn),
        grid=(n_tok // tm,),
        in_specs=[pl.BlockSpec((tm, D_MODEL), lambda i: (i, 0)),
                  pl.BlockSpec((tm, width), lambda i: (i, 0)),
                  pl.BlockSpec((tm, width), lambda i: (i, z_block)),
                  pl.BlockSpec((1, width // n_heads), lambda i: (0, 0)),
                  _mod_spec(2, cond_of_tile),
                  pl.BlockSpec((width, D_MODEL), lambda i: (0, 0))],
        out_specs=pl.BlockSpec((tm, D_MODEL), lambda i: (i, 0)),
        out_shape=jax.ShapeDtypeStruct((n_tok, D_MODEL), F32),
        scratch_shapes=[pltpu.VMEM((width, D_MODEL), BF16)],
        compiler_params=_params("arbitrary"),
        name="gated_out",
    )(x, o, main, norm_g.reshape(1, -1), mod, w_out)


def _dft_mats(n):
    k = np.arange(n, dtype=np.int64)
    ang = 2.0 * np.pi * ((k[:, None] * k[None, :]) % n).astype(np.float64) / n
    s = 1.0 / np.sqrt(n)
    return np.cos(ang) * s, np.sin(ang) * s


def _fnet_kernel(x_ref, ng_ref, sc_ref, sh_ref, g1_ref, ct_ref, st_ref, cs_ref, w_ref, b_ref, y_ref, wb_scr):
    @pl.when(pl.program_id(0) == 0)
    def _():
        wb_scr[...] = w_ref[...].astype(BF16)

    seq = ct_ref.shape[0]
    seqs = [pl.ds(r, seq) for r in range(0, x_ref.shape[0], seq)]
    gd = FNET_GROUP_DIM
    hs = [_norm_mod(x_ref[s, :], ng_ref[...], sc_ref[0], sh_ref[0]).astype(BF16) for s in seqs]
    ps = [jnp.dot(ct_ref[...], h, preferred_element_type=F32) for h in hs]
    qs = [jnp.dot(st_ref[...], h, preferred_element_type=F32) for h in hs]
    fs = []
    for p, q in zip(ps, qs):
        parts = []
        for g in range(D_MODEL // gd):
            sl = slice(g * gd, (g + 1) * gd)
            pq = jnp.concatenate([p[:, sl], q[:, sl]], axis=1).astype(BF16)
            parts.append(jnp.dot(pq, cs_ref[...], preferred_element_type=F32).astype(BF16))
        fs.append(jnp.concatenate(parts, axis=1))
    ys = [jnp.dot(f, wb_scr[...], preferred_element_type=F32) + b_ref[...] for f in fs]
    for s, y in zip(seqs, ys):
        y_ref[s, :] = x_ref[s, :] + g1_ref[0] * y


def _fnet_mix(x, seq, layer, norm_g, mod, tokens_per_cond, w, b):
    n_tok = x.shape[0]
    tm = seq * max(1, min(FNET_STEP_TOKENS, tokens_per_cond) // seq)
    cond_of_tile = lambda i: (i * tm) // tokens_per_cond
    ct, st = _dft_mats(seq)
    cc, sc = _dft_mats(FNET_GROUP_DIM)
    cs = np.concatenate([cc, -sc], axis=0)
    const = lambda i: (0, 0)
    ct, st, cs = (jnp.asarray(m, F32).astype(BF16) for m in (ct, st, cs))
    return pl.pallas_call(
        _fnet_kernel,
        grid=(n_tok // tm,),
        in_specs=[pl.BlockSpec((tm, D_MODEL), lambda i: (i, 0)),
                  pl.BlockSpec((1, D_MODEL), const),
                  _mod_spec(1, cond_of_tile), _mod_spec(0, cond_of_tile), _mod_spec(2, cond_of_tile),
                  pl.BlockSpec((seq, seq), const),
                  pl.BlockSpec((seq, seq), const),
                  pl.BlockSpec((2 * FNET_GROUP_DIM, FNET_GROUP_DIM), const),
                  pl.BlockSpec((None, D_MODEL, D_MODEL), lambda i: (layer, 0, 0)),
                  pl.BlockSpec((1, D_MODEL), const)],
        out_specs=pl.BlockSpec((tm, D_MODEL), lambda i: (i, 0)),
        out_shape=jax.ShapeDtypeStruct((n_tok, D_MODEL), F32),
        scratch_shapes=[pltpu.VMEM((D_MODEL, D_MODEL), BF16)],
        compiler_params=_params("arbitrary"),
        name="fnet",
    )(x, norm_g.reshape(1, D_MODEL), mod, mod, mod,
      ct, st, cs, w, b.reshape(1, D_MODEL))


SUBLANES = 8


def _shift_tokens(x, delta, period):
    n, c = x.shape
    assert n % period == 0 and period % SUBLANES == 0 and 0 < abs(delta) < period
    blocks = range(0, n, period)
    if delta % SUBLANES == 0:
        zeros = jnp.zeros((abs(delta), c), x.dtype)
        if delta > 0:
            pieces = [piece for b in blocks for piece in (x[b + delta:b + period], zeros)]
        else:
            pieces = [piece for b in blocks for piece in (zeros, x[b:b + period + delta])]
        return jnp.concatenate(pieces, axis=0)
    assert abs(delta) < SUBLANES
    y = pltpu.roll(x, (-delta) % n, 0)
    r = lax.broadcasted_iota(jnp.int32, (SUBLANES, 1), 0) + delta
    keep = ((r >= 0) & (r <= SUBLANES - 1)).astype(x.dtype)
    if delta < 0:
        pieces = [piece for b in blocks for piece in (y[b:b + SUBLANES] * keep, y[b + SUBLANES:b + period])]
    else:
        pieces = [piece for b in blocks
                  for piece in (y[b:b + period - SUBLANES], y[b + period - SUBLANES:b + period] * keep)]
    return jnp.concatenate(pieces, axis=0)


def _dwconv_tokens(g, cw_ref, seq, rows):
    width = seq // rows
    g_cols = (_shift_tokens(g, -1, width), g, _shift_tokens(g, 1, width))
    out = None
    for di in (-1, 0, 1):
        if rows == 1 and di != 0:
            continue
        r = sum(g_cols[dj] * cw_ref[pl.ds(3 * (di + 1) + dj, 1), :] for dj in range(3))
        if di != 0:
            r = _shift_tokens(r, di * width, seq)
        out = r if out is None else out + r
    return out


def _ffn_kernel(x_ref, ng_ref, sc_ref, sh_ref, g2_ref, wa_ref, wg_ref, cw_ref, cb_ref, wd_ref, nf_ref,
                y_ref, h_scr, acc_scr, *, seq, rows, final_norm):
    j = pl.program_id(1)

    @pl.when(j == 0)
    def _():
        h_scr[...] = _norm_mod(x_ref[...], ng_ref[...], sc_ref[0], sh_ref[0]).astype(BF16)
        acc_scr[...] = jnp.zeros_like(acc_scr)

    tm = h_scr.shape[0]
    sub = tm // FFN_SUB_TILES
    parts = [pl.ds(r, sub) for r in range(0, tm, sub)]
    wa, wg, wd = (w[...].astype(BF16) for w in (wa_ref, wg_ref, wd_ref))
    gs = [jnp.dot(h_scr[p, :], wg, preferred_element_type=F32) for p in parts]
    as_ = [jnp.dot(h_scr[p, :], wa, preferred_element_type=F32) for p in parts]
    per_seq = max(seq // sub, 1)
    convs = []
    for i in range(0, len(parts), per_seq):
        g = gs[i] if per_seq == 1 else jnp.concatenate(gs[i:i + per_seq], axis=0)
        c = _dwconv_tokens(g, cw_ref, seq, rows) + cb_ref[...]
        convs += [c[k * sub:(k + 1) * sub] for k in range(per_seq)]
    acts = [(jax.nn.silu(c) * a).astype(BF16) for c, a in zip(convs, as_)]
    for p, act in zip(parts, acts):
        acc_scr[p, :] += jnp.dot(act, wd, preferred_element_type=F32)

    @pl.when(j == pl.num_programs(1) - 1)
    def _():
        y = x_ref[...] + g2_ref[0] * acc_scr[...]
        if final_norm:
            y = _rms(y, nf_ref[...])
        y_ref[...] = y


def _conv_ffn(x, seq, rows, layer, norm_g, mod, cond_of_tile, w_up, conv_w, conv_b, w_down, norm_final, final_norm):
    n_tok = x.shape[0]
    tm, tf = FFN_TOKEN_TILE, FFN_FF_TILE
    n_ff_tiles = D_FF // tf
    depth = w_up.shape[0]
    return pl.pallas_call(
        functools.partial(_ffn_kernel, seq=seq, rows=rows, final_norm=final_norm),
        grid=(n_tok // tm, n_ff_tiles),
        in_specs=[pl.BlockSpec((tm, D_MODEL), lambda i, j: (i, 0)),
                  pl.BlockSpec((1, D_MODEL), lambda i, j: (0, 0)),
                  _mod_spec(4, cond_of_tile), _mod_spec(3, cond_of_tile), _mod_spec(5, cond_of_tile),
                  pl.BlockSpec((None, D_MODEL, tf), lambda i, j: (layer, 0, j)),
                  pl.BlockSpec((None, D_MODEL, tf), lambda i, j: (layer, 0, n_ff_tiles + j)),
                  pl.BlockSpec((None, 9, tf), lambda i, j: (layer, 0, j)),
                  pl.BlockSpec((None, 1, tf), lambda i, j: (layer, 0, j)),
                  pl.BlockSpec((None, tf, D_MODEL), lambda i, j: (layer, j, 0)),
                  pl.BlockSpec((1, D_MODEL), lambda i, j: (0, 0))],
        out_specs=pl.BlockSpec((tm, D_MODEL), lambda i, j: (i, 0)),
        out_shape=jax.ShapeDtypeStruct((n_tok, D_MODEL), F32),
        scratch_shapes=[pltpu.VMEM((tm, D_MODEL), BF16), pltpu.VMEM((tm, D_MODEL), F32)],
        compiler_params=_params("parallel", "arbitrary"),
        name="conv_ffn",
    )(x, norm_g.reshape(1, D_MODEL), mod, mod, mod, w_up, w_up, conv_w.reshape(depth, 9, D_FF),
      conv_b.reshape(depth, 1, D_FF), w_down, norm_final.reshape(1, D_MODEL))


GroupMasks = collections.namedtuple("GroupMasks", "r c same incl strict")


def _chunk_masks():
    r = lax.broadcasted_iota(jnp.int32, (CHUNK, CHUNK), 0)
    c = lax.broadcasted_iota(jnp.int32, (CHUNK, CHUNK), 1)
    return (r >= c, r <= c)


def _group_masks(group=GROUP):
    r = lax.broadcasted_iota(jnp.int32, (group, group), 0)
    c = lax.broadcasted_iota(jnp.int32, (group, group), 1)
    same = (r >> LOG_CHUNK) == (c >> LOG_CHUNK)
    return GroupMasks(r, c, same, (same & (r >= c), same & (r <= c)), (same & (r > c), same & (r < c)))


def _gate_layouts(gt, batch, n_heads, hp, group=GROUP):
    seq = gt.shape[1] // batch
    n_groups, n_chunks = seq // group, seq // CHUNK
    assert seq % group == 0 and n_groups <= 8
    g = jnp.transpose(gt.reshape(N_DIR, 2, n_heads, batch, seq), (3, 2, 0, 1, 4))
    rows = g.reshape(g.shape[:4] + (n_groups, group))
    rows = jnp.pad(rows, ((0, 0),) * 4 + ((0, 8 - n_groups), (0, 0)))
    cols = jnp.swapaxes(g.reshape(batch, n_heads // hp, hp * N_DIR * 2 * n_chunks, CHUNK), -1, -2)
    return rows, cols


def _col_gate(gcol_ref, hh, d, kind, n_chunks):
    start = ((hh * N_DIR + d) * 2 + kind) * n_chunks
    return gcol_ref[0, 0, :, start:start + n_chunks]


def _split3(x):
    x1 = x.astype(BF16)
    r1 = x - x1.astype(F32)
    x2 = r1.astype(BF16)
    return x1, x2, (r1 - x2.astype(F32)).astype(BF16)


def _dot_mask_rhs(x, mask):
    m = x.shape[0]
    y = jnp.dot(jnp.concatenate(_split3(x), axis=0), jnp.where(mask, 1.0, 0.0).astype(BF16),
                preferred_element_type=F32)
    return y[:m] + y[m:2 * m] + y[2 * m:]


def _dot_mask_lhs(mask, x):
    mb = jnp.where(mask, 1.0, 0.0).astype(BF16)
    y1, y2, y3 = (jnp.dot(mb, p, preferred_element_type=F32) for p in _split3(x))
    return y1 + y2 + y3


def _cumsum_rows(row, d, gm):
    return _dot_mask_rhs(row, gm.incl[1 - d])


def _cumsum_cols(col, d, cm):
    return _dot_mask_lhs(cm[d], col)


def _store_cols(dst, col, n_chunks):
    for n in range(n_chunks):
        dst[n * CHUNK:(n + 1) * CHUNK, :] = jnp.broadcast_to(col[:, n:n + 1], (CHUNK, 128))


def _wide(x, width=None):
    reps = (GROUP if width is None else width) // x.shape[1]
    return x if reps == 1 else jnp.concatenate([x] * reps, axis=1)


def _rows(i, size):
    if isinstance(i, int):
        return pl.ds(i * size, size)
    return pl.ds(pl.multiple_of(i * size, size), size)


def _loop(n, body, init):
    if n == 1:
        return body(0, init)
    return lax.fori_loop(0, n, body, init)


def _conv_silu(x, w_ref):
    n = x.shape[0]
    k = w_ref.shape[0]
    acc = None
    for j in range(k):
        delta = j - k // 2
        term = (x if delta == 0 else _shift_tokens(x, delta, n)) * w_ref[pl.ds(j, 1), :]
        acc = term if acc is None else acc + term
    return jax.nn.silu(acc)


def _l2norm(x):
    return x * lax.rsqrt(jnp.sum(x * x, axis=-1, keepdims=True) + EPS)


def _blockdiag_tri_inverse(mats, gm):
    dot16 = lambda x, y: jnp.dot(x, y, preferred_element_type=F32).astype(BF16)
    base = (gm.r >> 3) == (gm.c >> 3)
    eye = jnp.where(gm.r == gm.c, 1.0, 0.0)
    zero16 = jnp.zeros((), BF16)
    a16s = [a.astype(BF16) for a in mats]
    invs = [eye - jnp.where(base, a, 0.0) for a in mats]
    ps = [jnp.where(base, -a16, zero16) for a16 in a16s]
    for _ in range(2):
        ps = [dot16(p, p) for p in ps]
        invs = [inv + jnp.dot(inv.astype(BF16), p, preferred_element_type=F32) for inv, p in zip(invs, ps)]
    for s in (3, 4, 5):
        join = ((gm.r >> s) ^ (gm.c >> s)) == 1
        inv16s = [inv.astype(BF16) for inv in invs]
        tmp = [dot16(jnp.where(join, a16, zero16), inv16) for a16, inv16 in zip(a16s, inv16s)]
        invs = [inv - jnp.dot(inv16, t, preferred_element_type=F32) for inv, inv16, t in zip(invs, inv16s, tmp)]
    return invs


def _gdn_kernel(*refs, seq, hp, zero_init, emit_state):
    alog_ref, dtb_ref, q_ref, k_ref, v_ref, cwq_ref, cwk_ref, cwv_ref, grow_ref, gcol_ref = refs[:10]
    pos = 10
    s0_ref = None
    if not zero_init:
        s0_ref = refs[pos]
        pos += 1
    o_ref = refs[pos]
    pos += 1
    sfin_ref = None
    if emit_state:
        sfin_ref = refs[pos]
        pos += 1
    (q_scr, k_scr, v_scr, gcr_scr, gcb_scr, bb_scr, gl_scr,
     u_scr, w_scr, qd_scr, kd_scr, qkd_scr) = refs[pos:]

    n_chunks = seq // CHUNK
    grp = GDN_GROUP
    cm = _chunk_masks()
    gm = _group_masks(grp)
    head_dirs = [(hh, d) for hh in range(hp) for d in range(N_DIR)]
    lanes = lambda hh: slice(hh * DN_DK, (hh + 1) * DN_DK)

    q_scr[...] = _conv_silu(q_ref[...].astype(F32), cwq_ref)
    k_scr[...] = _conv_silu(k_ref[...].astype(F32), cwk_ref)
    v_scr[...] = _conv_silu(v_ref[...].astype(F32), cwv_ref)
    for hh in range(hp):
        q_scr[:, lanes(hh)] = _l2norm(q_scr[:, lanes(hh)]) * (DN_DK ** -0.5)
        k_scr[:, lanes(hh)] = _l2norm(k_scr[:, lanes(hh)])

    for hh, d in head_dirs:
        head = pl.program_id(1) * hp + hh
        neg_a = -jnp.exp(jnp.full((1, 1), alog_ref[d, head], F32))
        dtb = dtb_ref[d, head]
        gcr_scr[hh, d] = _cumsum_rows(neg_a * _softplus(grow_ref[0, hh, d, 0] + dtb), d, gm)
        gc_col = _cumsum_cols(neg_a * _softplus(_col_gate(gcol_ref, hh, d, 0, n_chunks) + dtb), d, cm)
        _store_cols(gcb_scr.at[hh, d], gc_col, n_chunks)
        _store_cols(bb_scr.at[hh, d], jax.nn.sigmoid(_col_gate(gcol_ref, hh, d, 1, n_chunks)), n_chunks)
        last = CHUNK - 1 if d == 0 else 0
        _store_cols(gl_scr.at[hh, d], jnp.broadcast_to(gc_col[last:last + 1, :], gc_col.shape), n_chunks)

    gps = min(GDN_GROUPS_PER_STEP, seq // grp)

    def group_step(it, carry):
        gis = [it * gps + k for k in range(gps)]
        rows = [_rows(gi, grp) for gi in gis]
        for h0 in range(0, hp, GDN_HEADS_PER_PASS):
            heads = range(h0, min(h0 + GDN_HEADS_PER_PASS, hp))
            kqs = {}
            for k, r in enumerate(rows):
                for hh in heads:
                    kg, qg = k_scr[r, lanes(hh)], q_scr[r, lanes(hh)]
                    kqs[k, hh] = _dot_nt(jnp.concatenate([kg, qg], axis=0), kg)
            chains = [(k, hh, d) for k in range(len(gis)) for hh in heads for d in range(N_DIR)]
            mats = []
            for k, hh, d in chains:
                gcb = gcb_scr[hh, d, rows[k], :]
                gcr = gcr_scr[hh, d, pl.ds(gis[k], 1), :]
                decay = jnp.exp(jnp.where(gm.incl[d], _wide(gcb, grp) - gcr, -jnp.inf))
                mats.append(jnp.where(gm.strict[d],
                                      kqs[k, hh][:grp] * _wide(bb_scr[hh, d, rows[k], :], grp) * decay, 0.0))
                qkd = kqs[k, hh][grp:] * decay
                qkd_scr[hh, d, rows[k], :] = sum(qkd[:, n * CHUNK:(n + 1) * CHUNK] for n in range(grp // CHUNK))
            t_invs = _blockdiag_tri_inverse(mats, gm)
            for (k, hh, d), t_inv in zip(chains, t_invs):
                r = rows[k]
                qg, kg, vg = q_scr[r, lanes(hh)], k_scr[r, lanes(hh)], v_scr[r, lanes(hh)]
                gcb = gcb_scr[hh, d, r, :]
                bb = bb_scr[hh, d, r, :]
                e_gc = jnp.exp(gcb)
                uw = _dot(t_inv, jnp.concatenate([vg * bb, kg * bb * e_gc], axis=1))
                u_scr[hh, d, r, :] = uw[:, :DN_DV]
                w_scr[hh, d, r, :] = uw[:, DN_DV:]
                qd_scr[hh, d, r, :] = qg * e_gc
                kd_scr[hh, d, r, :] = kg * jnp.exp(gl_scr[hh, d, r, :] - gcb)
        return carry

    _loop(seq // (grp * gps), group_step, 0)

    o_ref[...] = jnp.zeros_like(o_ref)

    def chunk_step(n, carry):
        idxs = [n if d == 0 else n_chunks - 1 - n for _, d in head_dirs]
        rows = [_rows(idx, CHUNK) for idx in idxs]
        wqs = [_dot(jnp.concatenate([w_scr[hh, d, r, :], qd_scr[hh, d, r, :]], axis=0), s)
               for (hh, d), r, s in zip(head_dirs, rows, carry)]
        v_news = [u_scr[hh, d, r, :] - wq[:CHUNK] for (hh, d), r, wq in zip(head_dirs, rows, wqs)]
        o_ns = [wq[CHUNK:] + _dot(qkd_scr[hh, d, r, :], v_new)
                for (hh, d), r, wq, v_new in zip(head_dirs, rows, wqs, v_news)]
        new = []
        for (hh, d), idx, r, s, v_new in zip(head_dirs, idxs, rows, carry, v_news):
            s_decay = jnp.exp(gl_scr[hh, d, pl.ds(idx * CHUNK, 1), :])
            new.append(s * s_decay + _dot_tn(kd_scr[hh, d, r, :], v_new))
        for (hh, d), r, o_n in zip(head_dirs, rows, o_ns):
            o_ref[r, lanes(hh)] += o_n
        return tuple(new)

    if zero_init:
        init = (jnp.zeros((DN_DK, DN_DV), F32),) * len(head_dirs)
    else:
        init = tuple(s0_ref[0, d, hh] for hh, d in head_dirs)
    fin = lax.fori_loop(0, n_chunks, chunk_step, init)
    if emit_state:
        for i, (hh, d) in enumerate(head_dirs):
            sfin_ref[0, d, hh] = fin[i]


def _gdn_scan(main, gt, batch, seq, conv_w, a_log, dt_bias, s0, emit_state):
    n_tok = batch * seq
    nh = DN_HEADS
    hp = min(nh, max(2, GDN_STEP_POSITIONS // seq))
    n_hb = nh // hp
    assert seq % (GDN_GROUP * min(GDN_GROUPS_PER_STEP, seq // GDN_GROUP)) == 0
    grow, gcol = _gate_layouts(gt, batch, nh, hp, GDN_GROUP)
    zero_init = s0 is None
    smem = pl.BlockSpec(memory_space=pltpu.SMEM)
    qkv_spec = lambda off: pl.BlockSpec((seq, hp * DN_DK), lambda b, h: (b, off + h))
    cw_spec = lambda off: pl.BlockSpec((DN_CONV, hp * DN_DK), lambda b, h: (0, off + h))
    state_spec = pl.BlockSpec((1, N_DIR, hp, DN_DK, DN_DV), lambda b, h: (b, 0, h, 0, 0))
    in_specs = [smem, smem, qkv_spec(0), qkv_spec(n_hb), qkv_spec(2 * n_hb),
                cw_spec(0), cw_spec(n_hb), cw_spec(2 * n_hb),
                pl.BlockSpec((1, hp) + grow.shape[2:], lambda b, h: (b, h, 0, 0, 0, 0)),
                pl.BlockSpec((1, 1) + gcol.shape[2:], lambda b, h: (b, h, 0, 0))]
    args = [a_log, dt_bias, main, main, main, conv_w, conv_w, conv_w, grow, gcol]
    if not zero_init:
        in_specs.append(state_spec)
        args.append(s0)
    out_specs = [pl.BlockSpec((seq, hp * DN_DV), lambda b, h: (b, h))]
    out_shape = [jax.ShapeDtypeStruct((n_tok, nh * DN_DV), F32)]
    if emit_state:
        out_specs.append(state_spec)
        out_shape.append(jax.ShapeDtypeStruct((batch, N_DIR, nh, DN_DK, DN_DV), F32))
    per_dir = lambda width: pltpu.VMEM((hp, N_DIR, seq, width), F32)
    qkv_scr = pltpu.VMEM((seq, hp * DN_DK), F32)
    outs = pl.pallas_call(
        functools.partial(_gdn_kernel, seq=seq, hp=hp, zero_init=zero_init, emit_state=emit_state),
        grid=(batch, n_hb),
        in_specs=in_specs, out_specs=out_specs, out_shape=out_shape,
        scratch_shapes=[qkv_scr, qkv_scr, qkv_scr,
                        pltpu.VMEM((hp, N_DIR, 8, GDN_GROUP), F32),
                        per_dir(128), per_dir(128), per_dir(128),
                        per_dir(DN_DV), per_dir(DN_DK), per_dir(DN_DK), per_dir(DN_DK), per_dir(CHUNK)],
        compiler_params=_params("parallel", "parallel"),
        name="gdn_scan",
    )(*args)
    return outs[0], (outs[1] if emit_state else None)


ML_AUG = 2 * ML_DV


def _mlstm_kernel(*refs, seq, hp, zero_init, emit_state):
    bi_ref, bf_ref, q_ref, k_ref, v_ref, kt_ref, grow_ref, gcol_ref = refs[:8]
    pos = 8
    c0_ref = m0_ref = None
    if not zero_init:
        c0_ref, m0_ref = refs[pos:pos + 2]
        pos += 2
    o_ref = refs[pos]
    pos += 1
    cfin_ref = nfin_ref = mfin_ref = None
    if emit_state:
        cfin_ref, nfin_ref, mfin_ref = refs[pos:pos + 3]
        pos += 3
    bcr_scr, lir_scr, kwr_scr, bcb_scr, mi_scr, dec_scr, dc_scr, cin_scr = refs[pos:]

    n_chunks = seq // CHUNK
    n_groups = seq // GROUP
    head0 = pl.program_id(1) * hp
    cm = _chunk_masks()
    gm = _group_masks()
    ones_col = jnp.where(lax.broadcasted_iota(jnp.int32, (GROUP, ML_DV), 1) == 0, 1.0, 0.0)
    head_dirs = [(hh, d) for hh in range(hp) for d in range(N_DIR)]

    m_fin = {}
    row_group = lax.broadcasted_iota(jnp.int32, (8, GROUP), 0)
    row_chunk = lax.broadcasted_iota(jnp.int32, (8, GROUP), 1) >> LOG_CHUNK
    for hh, d in head_dirs:
        b_i = bi_ref[d, head0 + hh]
        b_f = bf_ref[d, head0 + hh]
        li_row = grow_ref[0, hh, d, 0] + b_i
        lf_row = -_softplus(-(grow_ref[0, hh, d, 1] + b_f))
        bc_row = _cumsum_rows(lf_row, d, gm)
        lir_scr[hh, d] = li_row
        bcr_scr[hh, d] = bc_row
        w_row = _dot_mask_rhs(lf_row, gm.same) - bc_row + li_row
        m_out_row = jnp.zeros((8, GROUP), F32)
        li_col = _col_gate(gcol_ref, hh, d, 0, n_chunks) + b_i
        bc_col = _cumsum_cols(-_softplus(-(_col_gate(gcol_ref, hh, d, 1, n_chunks) + b_f)), d, cm)
        last = CHUNK - 1 if d == 0 else 0
        b_last = bc_col[last:last + 1, :]
        w_col = b_last - bc_col + li_col
        w_max = jnp.max(w_col, axis=0, keepdims=True)
        m = jnp.zeros((1, 1), F32) if zero_init else m0_ref[0, d, hh][:, 0:1]
        for step in range(n_chunks):
            n = step if d == 0 else n_chunks - 1 - step
            sl = slice(n * CHUNK, (n + 1) * CHUNK)
            m_new = jnp.maximum(b_last[:, n:n + 1] + m, w_max[:, n:n + 1])
            bcb_scr[hh, d, sl, :] = jnp.broadcast_to(bc_col[:, n:n + 1], (CHUNK, 128))
            mi_scr[hh, d, sl, :] = jnp.broadcast_to(m, (CHUNK, 128))
            dec_scr[hh, d, n:n + 1, :] = jnp.broadcast_to(jnp.exp(b_last[:, n:n + 1] + m - m_new), (1, 128))
            in_chunk = (row_group == n // CHUNKS_PER_GROUP) & (row_chunk == n % CHUNKS_PER_GROUP)
            m_out_row = jnp.where(in_chunk, m_new, m_out_row)
            m = m_new
        m_fin[hh, d] = m
        kwr_scr[hh, d] = jnp.exp(w_row - m_out_row)

    def load_kv(rows, hh):
        kg = k_ref[rows, hh * ML_DQK:(hh + 1) * ML_DQK].astype(F32)
        v_aug = jnp.concatenate([v_ref[rows, hh * ML_DV:(hh + 1) * ML_DV].astype(F32), ones_col], axis=1)
        return kg, v_aug

    def delta_step(gi, carry):
        rows = _rows(gi, GROUP)
        v_augs = [load_kv(rows, hh)[1] for hh in range(hp)]
        kt = kt_ref[gi]
        lhs = []
        for hh, d in head_dirs:
            kw_t = kt[hh * ML_DQK:(hh + 1) * ML_DQK, :] * kwr_scr[hh, d, pl.ds(gi, 1), :]
            lhs.append(jnp.where(gm.same, jnp.concatenate([kw_t] * CHUNKS_PER_GROUP, axis=0), 0.0))
        for (hh, d), kw in zip(head_dirs, lhs):
            dc_scr[hh, d, rows, :] = _dot(kw, v_augs[hh])
        return carry

    _loop(n_groups, delta_step, 0)

    def prefix_step(n, carry):
        new = []
        for (hh, d), c_aug in zip(head_dirs, carry):
            idx = n if d == 0 else n_chunks - 1 - n
            rows = _rows(idx, CHUNK)
            cin_scr[hh, d, rows, :] = c_aug
            new.append(c_aug * _wide(dec_scr[hh, d, pl.ds(idx, 1), :], ML_AUG) + dc_scr[hh, d, rows, :])
        return tuple(new)

    if zero_init:
        init = (jnp.zeros((ML_DQK, ML_AUG), F32),) * len(head_dirs)
    else:
        init = tuple(c0_ref[0, d, hh] for hh, d in head_dirs)
    fin = lax.fori_loop(0, n_chunks, prefix_step, init)

    def out_step(gi, carry):
        rows = _rows(gi, GROUP)
        kvs = [load_kv(rows, hh) for hh in range(hp)]
        qgs = [q_ref[rows, hh * ML_DQK:(hh + 1) * ML_DQK].astype(F32) * (ML_DQK ** -0.5) for hh in range(hp)]
        qks = [_dot_nt(qg, kv[0]) for qg, kv in zip(qgs, kvs)]
        chunk = lambda n: slice(n * CHUNK, (n + 1) * CHUNK)
        inters = []
        for hh, d in head_dirs:
            c_in = cin_scr[hh, d, rows, :]
            inters.append(jnp.concatenate(
                [_dot(qgs[hh][chunk(n)], c_in[chunk(n)]) for n in range(CHUNKS_PER_GROUP)], axis=0))
        m_ts, b_ms, ss = [], [], []
        for hh, d in head_dirs:
            b_colb = bcb_scr[hh, d, rows, :]
            b_m = b_colb[:, 0:1] + mi_scr[hh, d, rows, :][:, 0:1]
            d_log = jnp.where(gm.incl[d], _wide(b_colb) - bcr_scr[hh, d, pl.ds(gi, 1), :]
                              + lir_scr[hh, d, pl.ds(gi, 1), :], -jnp.inf)
            m_t = jnp.maximum(b_m, jnp.max(d_log, axis=-1, keepdims=True))
            m_ts.append(m_t)
            b_ms.append(b_m)
            ss.append(qks[hh] * jnp.exp(d_log - m_t))
        intras = [_dot(s, kvs[hh][1]) for (hh, d), s in zip(head_dirs, ss)]
        hs = []
        for inter, intra, m_t, b_m in zip(inters, intras, m_ts, b_ms):
            num = jnp.exp(b_m - m_t) * inter + intra
            den = num[:, ML_DV:ML_DV + 1]
            hs.append(num[:, :ML_DV] / jnp.maximum(jnp.abs(den), jnp.exp(-m_t)))
        for hh in range(hp):
            o_ref[rows, hh * ML_DV:(hh + 1) * ML_DV] = hs[N_DIR * hh] + hs[N_DIR * hh + 1]
        return carry

    _loop(n_groups, out_step, 0)

    if emit_state:
        for i, (hh, d) in enumerate(head_dirs):
            cfin_ref[0, d, hh] = fin[i][:, :ML_DV]
            nfin_ref[0, d, hh] = fin[i][:, ML_DV:]
            mfin_ref[0, d, hh] = jnp.broadcast_to(m_fin[hh, d], (1, 128))


def _mlstm_scan(main, kt, gt, batch, seq, b_i, b_f, state0, emit_state):
    n_tok = batch * seq
    n_groups = seq // GROUP
    n_chunks = seq // CHUNK
    nh = ML_HEADS
    hp = min(nh, max(2, SCAN_STEP_POSITIONS // seq))
    n_hb = nh // hp
    grow, gcol = _gate_layouts(gt, batch, nh, hp)
    zero_init = state0 is None
    smem = pl.BlockSpec(memory_space=pltpu.SMEM)
    k_off = (nh * ML_DQK) // (hp * ML_DQK)
    v_off = (2 * nh * ML_DQK) // (hp * ML_DV)
    c_spec = pl.BlockSpec((1, N_DIR, hp, ML_DQK, ML_AUG), lambda b, p: (b, 0, p, 0, 0))
    m_spec = pl.BlockSpec((1, N_DIR, hp, 1, 128), lambda b, p: (b, 0, p, 0, 0))
    in_specs = [smem, smem,
                pl.BlockSpec((seq, hp * ML_DQK), lambda b, p: (b, p)),
                pl.BlockSpec((seq, hp * ML_DQK), lambda b, p: (b, k_off + p)),
                pl.BlockSpec((seq, hp * ML_DV), lambda b, p: (b, v_off + p)),
                pl.BlockSpec((n_groups, hp * ML_DQK, GROUP), lambda b, p: (b, p, 0)),
                pl.BlockSpec((1, hp) + grow.shape[2:], lambda b, p: (b, p, 0, 0, 0, 0)),
                pl.BlockSpec((1, 1) + gcol.shape[2:], lambda b, p: (b, p, 0, 0))]
    args = [b_i, b_f, main, main, main, kt, grow, gcol]
    if not zero_init:
        in_specs += [c_spec, m_spec]
        args += list(state0)
    out_specs = [pl.BlockSpec((seq, hp * ML_DV), lambda b, p: (b, p))]
    out_shape = [jax.ShapeDtypeStruct((n_tok, nh * ML_DV), F32)]
    if emit_state:
        half_spec = pl.BlockSpec((1, N_DIR, hp, ML_DQK, ML_DV), lambda b, p: (b, 0, p, 0, 0))
        out_specs += [half_spec, half_spec, m_spec]
        out_shape += [jax.ShapeDtypeStruct((batch, N_DIR, nh, ML_DQK, ML_DV), F32),
                      jax.ShapeDtypeStruct((batch, N_DIR, nh, ML_DQK, ML_AUG - ML_DV), F32),
                      jax.ShapeDtypeStruct((batch, N_DIR, nh, 1, 128), F32)]
    per_hd = lambda rows, width: pltpu.VMEM((hp, N_DIR, rows, width), F32)
    outs = pl.pallas_call(
        functools.partial(_mlstm_kernel, seq=seq, hp=hp, zero_init=zero_init, emit_state=emit_state),
        grid=(batch, n_hb),
        in_specs=in_specs, out_specs=out_specs, out_shape=out_shape,
        scratch_shapes=[per_hd(8, GROUP), per_hd(8, GROUP), per_hd(8, GROUP),
                        per_hd(seq, 128), per_hd(seq, 128),
                        per_hd(max(n_chunks, 8), 128),
                        per_hd(seq, ML_AUG), per_hd(seq, ML_AUG)],
        compiler_params=_params("parallel", "parallel"),
        name="mlstm_scan",
    )(*args)
    return outs[0], (tuple(outs[1:4]) if emit_state else None)


def _trunk(x, batch, seq, rows, mods, tokens_per_cond, st_d, st_ml, emit_state, p):
    depth = p["w_ada"].shape[0]
    tile_cond = lambda tile: (lambda i: (i * tile) // tokens_per_cond)
    new_d = new_ml = None
    for layer in range(depth):
        mod = mods[layer]
        kind, j = layer % 3, layer // 3
        if kind == 0:
            x = _fnet_mix(x, seq, j, p["norm_mix"][layer], mod, tokens_per_cond, p["fnet_w"], p["fnet_b"][j])
        elif kind == 1:
            n_main = DN_HEADS * (2 * DN_DK + 2 * DN_DV)
            main, gt = _norm_proj(x, p["norm_mix"][layer], mod, tokens_per_cond, p["dn_w_in"], j, n_main)
            s0 = None if st_d is None else st_d[:, j]
            o, sfin = _gdn_scan(main, gt, batch, seq, p["dn_conv_w"][j], p["dn_a_log"][j], p["dn_dt_bias"][j],
                                s0, emit_state)
            if emit_state:
                new_d = sfin
            x = _gated_out(x, o, main, 3, p["dn_norm"][j], mod, tile_cond(OUT_TOKEN_TILE), p["dn_w_out"][j],
                           DN_HEADS, jax.nn.silu)
        else:
            n_main = 2 * ML_HEADS * ML_DQK + 2 * ML_HEADS * ML_DV
            k_tile = (ML_HEADS * ML_DQK) // PROJ_COL_TILE
            assert ML_HEADS * ML_DQK == PROJ_COL_TILE
            main, gt, kt = _norm_proj(x, p["norm_mix"][layer], mod, tokens_per_cond, p["ml_w_in"], j, n_main,
                                      t_tile=k_tile)
            state0 = None
            if st_ml is not None:
                c0, n0, m0 = (s[:, j] for s in st_ml)
                pad = jnp.zeros(c0.shape[:-1] + (ML_AUG - ML_DV - 1,), F32)
                c_aug0 = jnp.concatenate([c0, n0[..., None], pad], axis=-1)
                m0b = jnp.broadcast_to(m0[..., None, None], m0.shape + (1, 128))
                state0 = (c_aug0, m0b)
            o, fin = _mlstm_scan(main, kt, gt, batch, seq, p["ml_b_i"][j], p["ml_b_f"][j], state0, emit_state)
            if emit_state:
                new_ml = (fin[0], fin[1][..., 0], fin[2][..., 0, 0])
            x = _gated_out(x, o, main, 2, p["ml_norm"][j], mod, tile_cond(OUT_TOKEN_TILE), p["ml_w_out"][j],
                           ML_HEADS, jax.nn.sigmoid)
        x = _conv_ffn(x, seq, rows, layer, p["norm_ffn"][layer], mod, tile_cond(FFN_TOKEN_TILE), p["ffn_w_up"],
                      p["ffn_conv_w"], p["ffn_conv_b"], p["ffn_w_down"], p["norm_final"],
                      final_norm=(layer == depth - 1))
    return x, new_d, new_ml


def kernel(x_prompt, x_sample, state_delta, state_mlstm_c, state_mlstm_n, state_mlstm_m, c, c_ctx, w_ada, b_ada, norm_mix, norm_ffn, norm_final, ffn_w_up, ffn_conv_w, ffn_conv_b, ffn_w_down, fnet_w, fnet_b, dn_w_in, dn_conv_w, dn_a_log, dn_dt_bias, dn_norm, dn_w_out, ml_w_in, ml_b_i, ml_b_f, ml_norm, ml_w_out):
    p = dict(w_ada=w_ada, norm_mix=norm_mix, norm_ffn=norm_ffn, norm_final=norm_final, ffn_w_up=ffn_w_up,
             ffn_conv_w=ffn_conv_w, ffn_conv_b=ffn_conv_b, ffn_w_down=ffn_w_down, fnet_w=fnet_w, fnet_b=fnet_b,
             dn_w_in=dn_w_in, dn_conv_w=dn_conv_w, dn_a_log=dn_a_log, dn_dt_bias=dn_dt_bias, dn_norm=dn_norm,
             dn_w_out=dn_w_out, ml_w_in=ml_w_in, ml_b_i=ml_b_i, ml_b_f=ml_b_f, ml_norm=ml_norm, ml_w_out=ml_w_out)
    b_ctx, t_ctx, _ = x_prompt.shape
    b_smp, t_smp, _ = x_sample.shape
    depth = w_ada.shape[0]

    cond8 = jnp.concatenate([c_ctx[None, :], c, jnp.zeros((8 - 1 - b_smp, D_MODEL), F32)], axis=0)
    mods = _adaln_all(cond8, w_ada, b_ada)
    mods_ctx = [mods[l, 0:1].reshape(1, 1, -1) for l in range(depth)]
    mods_smp = [mods[l, 1:1 + b_smp].reshape(b_smp, 1, -1) for l in range(depth)]

    y_ctx, new_d, new_ml = _trunk(x_prompt.reshape(b_ctx * t_ctx, D_MODEL), b_ctx, t_ctx, 1, mods_ctx,
                                  b_ctx * t_ctx, None, None, True, p)
    y_smp, _, _ = _trunk(x_sample.reshape(b_smp * t_smp, D_MODEL), b_smp, t_smp, t_smp // GRID_W, mods_smp,
                         t_smp, state_delta, (state_mlstm_c, state_mlstm_n, state_mlstm_m), False, p)

    new_c, new_n, new_m = new_ml
    return (y_ctx.reshape(b_ctx, t_ctx, D_MODEL), y_smp.reshape(b_smp, t_smp, D_MODEL),
            new_d[:, None], new_c[:, None], new_n[:, None], new_m[:, None])
```

```python
import collections
import functools

import numpy as np
import jax
import jax.numpy as jnp
from jax import lax
from jax.experimental import pallas as pl
from jax.experimental.pallas import tpu as pltpu

F32 = jnp.float32
BF16 = jnp.bfloat16

D_MODEL = 1024
EPS = 1e-6
N_DIR = 2
CHUNK = 64
LOG_CHUNK = 6
GROUP = 256
CHUNKS_PER_GROUP = GROUP // CHUNK
GDN_GROUP = 128
GDN_GROUPS_PER_STEP = 4
GDN_HEADS_PER_PASS = 4
GDN_STEP_POSITIONS = 2048
FNET_GROUP_DIM = 256
DN_HEADS, DN_DK, DN_DV, DN_CONV = 8, 128, 128, 5
ML_HEADS, ML_DQK, ML_DV = 8, 64, 128
D_FF = 2816
GRID_W = 64

FNET_STEP_TOKENS = 1024
FFN_TOKEN_TILE = 1024
FFN_FF_TILE = 256
FFN_SUB_TILES = 4
PROJ_TOKEN_TILE = 2048
PROJ_COL_TILE = 512
OUT_TOKEN_TILE = 1024
ADA_COL_TILE = 3072
SCAN_STEP_POSITIONS = 1024
VMEM_LIMIT = 56 * 1024 * 1024


def _params(*sem):
    return pltpu.CompilerParams(dimension_semantics=sem, vmem_limit_bytes=VMEM_LIMIT)


def _dot(a, b):
    return jnp.dot(a.astype(BF16), b.astype(BF16), preferred_element_type=F32)


def _dot_nt(a, b):
    return lax.dot_general(a.astype(BF16), b.astype(BF16), (((1,), (1,)), ((), ())),
                           preferred_element_type=F32)


def _dot_tn(a, b):
    return lax.dot_general(a.astype(BF16), b.astype(BF16), (((0,), (0,)), ((), ())),
                           preferred_element_type=F32)


def _rms(x, g):
    return x * lax.rsqrt(jnp.mean(x * x, axis=-1, keepdims=True) + EPS) * g


def _norm_mod(x, g, sc, sh):
    return _rms(x, g) * (1.0 + sc) + sh


def _softplus(x):
    return jnp.maximum(x, 0.0) + jnp.log1p(jnp.exp(-jnp.abs(x)))


def _mod_spec(chunk, cond_of_tile):
    return pl.BlockSpec((1, 1, D_MODEL), lambda i, *_: (cond_of_tile(i), 0, chunk))


def _adaln_kernel(c_ref, w_ref, b_ref, o_ref):
    s = jax.nn.silu(c_ref[...])
    o_ref[0] = _dot(s, w_ref[0]) + b_ref[0]


def _adaln_all(cond8, w_ada, b_ada):
    depth = w_ada.shape[0]
    n_out = w_ada.shape[2]
    tn = ADA_COL_TILE
    return pl.pallas_call(
        _adaln_kernel,
        grid=(depth, n_out // tn),
        in_specs=[pl.BlockSpec((8, D_MODEL), lambda l, j: (0, 0)),
                  pl.BlockSpec((1, D_MODEL, tn), lambda l, j: (l, 0, j)),
                  pl.BlockSpec((1, 1, tn), lambda l, j: (l, 0, j))],
        out_specs=pl.BlockSpec((1, 8, tn), lambda l, j: (l, 0, j)),
        out_shape=jax.ShapeDtypeStruct((depth, 8, n_out), F32),
        compiler_params=_params("parallel", "parallel"),
        name="adaln",
    )(cond8, w_ada, b_ada.reshape(depth, 1, n_out))


def _norm_proj_kernel(x_ref, g_ref, sc_ref, sh_ref, w_ref, wg_ref, o_ref, gate_ref, *rest, t_tile):
    h_scr = rest[-1]

    @pl.when(pl.program_id(1) == 0)
    def _():
        h = _norm_mod(x_ref[...], g_ref[...], sc_ref[0], sh_ref[0]).astype(BF16)
        h_scr[...] = h
        gate_ref[...] = _dot_nt(wg_ref[...], h)

    y = _dot_nt(h_scr[...], w_ref[...])
    o_ref[...] = y.astype(o_ref.dtype)

    if t_tile is not None:
        t_ref = rest[0]

        @pl.when(pl.program_id(1) == t_tile)
        def _():
            for g in range(t_ref.shape[0]):
                t_ref[g] = y[g * GROUP:(g + 1) * GROUP, :].T


def _norm_proj(x, norm_g, mod, tokens_per_cond, w_stack, layer, n_main, t_tile=None):
    n_tok = x.shape[0]
    tm, tn = min(PROJ_TOKEN_TILE, tokens_per_cond), PROJ_COL_TILE
    cond_of_tile = lambda i: (i * tm) // tokens_per_cond
    w_t = jnp.swapaxes(w_stack, 1, 2)
    n_gate = w_t.shape[1] - n_main
    assert n_main % n_gate == 0 and n_gate % 8 == 0
    out_specs = [pl.BlockSpec((tm, tn), lambda i, j: (i, j)),
                 pl.BlockSpec((n_gate, tm), lambda i, j: (0, i))]
    out_shape = [jax.ShapeDtypeStruct((n_tok, n_main), BF16),
                 jax.ShapeDtypeStruct((n_gate, n_tok), F32)]
    if t_tile is not None:
        out_specs.append(pl.BlockSpec((tm // GROUP, tn, GROUP), lambda i, j: (i, 0, 0)))
        out_shape.append(jax.ShapeDtypeStruct((n_tok // GROUP, tn, GROUP), F32))
    return pl.pallas_call(
        functools.partial(_norm_proj_kernel, t_tile=t_tile),
        grid=(n_tok // tm, n_main // tn),
        in_specs=[pl.BlockSpec((tm, D_MODEL), lambda i, j: (i, 0)),
                  pl.BlockSpec((1, D_MODEL), lambda i, j: (0, 0)),
                  _mod_spec(1, cond_of_tile), _mod_spec(0, cond_of_tile),
                  pl.BlockSpec((None, tn, D_MODEL), lambda i, j: (layer, j, 0)),
                  pl.BlockSpec((None, n_gate, D_MODEL), lambda i, j: (layer, n_main // n_gate, 0))],
        out_specs=out_specs, out_shape=out_shape,
        scratch_shapes=[pltpu.VMEM((tm, D_MODEL), BF16)],
        compiler_params=_params("parallel", "arbitrary"),
        name="norm_proj",
    )(x, norm_g.reshape(1, D_MODEL), mod, mod, w_t, w_t)


def _gated_out_kernel(x_ref, o_ref, z_ref, ng_ref, g1_ref, w_ref, y_ref, wb_scr, *, n_heads, gate_fn):
    @pl.when(pl.program_id(0) == 0)
    def _():
        wb_scr[...] = w_ref[...].astype(BF16)

    dv = o_ref.shape[1] // n_heads
    parts = []
    for h in range(n_heads):
        sl = slice(h * dv, (h + 1) * dv)
        parts.append((_rms(o_ref[:, sl], ng_ref[...]) * gate_fn(z_ref[:, sl].astype(F32))).astype(BF16))
    hs = jnp.concatenate(parts, axis=1)
    y = jnp.dot(hs, wb_scr[...], preferred_element_type=F32)
    y_ref[...] = x_ref[...] + g1_ref[0] * y


def _gated_out(x, o, main, z_block, norm_g, mod, cond_of_tile, w_out, n_heads, gate_fn):
    n_tok = x.shape[0]
    tm = OUT_TOKEN_TILE
    width = o.shape[1]
    return pl.pallas_call(
        functools.partial(_gated_out_kernel, n_heads=n_heads, gate_fn=gate_fn),
        grid=(n_tok // tm,),
        in_specs=[pl.BlockSpec((tm, D_MODEL), lambda i: (i, 0)),
                  pl.BlockSpec((tm, width), lambda i: (i, 0)),
                  pl.BlockSpec((tm, width), lambda i: (i, z_block)),
                  pl.BlockSpec((1, width // n_heads), lambda i: (0, 0)),
                  _mod_spec(2, cond_of_tile),
                  pl.BlockSpec((width, D_MODEL), lambda i: (0, 0))],
        out_specs=pl.BlockSpec((tm, D_MODEL), lambda i: (i, 0)),
        out_shape=jax.ShapeDtypeStruct((n_tok, D_MODEL), F32),
        scratch_shapes=[pltpu.VMEM((width, D_MODEL), BF16)],
        compiler_params=_params("arbitrary"),
        name="gated_out",
    )(x, o, main, norm_g.reshape(1, -1), mod, w_out)


def _dft_mats(n):
    k = np.arange(n, dtype=np.int64)
    ang = 2.0 * np.pi * ((k[:, None] * k[None, :]) % n).astype(np.float64) / n
    s = 1.0 / np.sqrt(n)
    return np.cos(ang) * s, np.sin(ang) * s


def _fnet_kernel(x_ref, ng_ref, sc_ref, sh_ref, g1_ref, ct_ref, st_ref, cs_ref, w_ref, b_ref, y_ref, wb_scr):
    @pl.when(pl.program_id(0) == 0)
    def _():
        wb_scr[...] = w_ref[...].astype(BF16)

    seq = ct_ref.shape[0]
    seqs = [pl.ds(r, seq) for r in range(0, x_ref.shape[0], seq)]
    gd = FNET_GROUP_DIM
    hs = [_norm_mod(x_ref[s, :], ng_ref[...], sc_ref[0], sh_ref[0]).astype(BF16) for s in seqs]
    ps = [jnp.dot(ct_ref[...], h, preferred_element_type=F32) for h in hs]
    qs = [jnp.dot(st_ref[...], h, preferred_element_type=F32) for h in hs]
    fs = []
    for p, q in zip(ps, qs):
        parts = []
        for g in range(D_MODEL // gd):
            sl = slice(g * gd, (g + 1) * gd)
            pq = jnp.concatenate([p[:, sl], q[:, sl]], axis=1).astype(BF16)
            parts.append(jnp.dot(pq, cs_ref[...], preferred_element_type=F32).astype(BF16))
        fs.append(jnp.concatenate(parts, axis=1))
    ys = [jnp.dot(f, wb_scr[...], preferred_element_type=F32) + b_ref[...] for f in fs]
    for s, y in zip(seqs, ys):
        y_ref[s, :] = x_ref[s, :] + g1_ref[0] * y


def _fnet_mix(x, seq, layer, norm_g, mod, tokens_per_cond, w, b):
    n_tok = x.shape[0]
    tm = seq * max(1, min(FNET_STEP_TOKENS, tokens_per_cond) // seq)
    cond_of_tile = lambda i: (i * tm) // tokens_per_cond
    ct, st = _dft_mats(seq)
    cc, sc = _dft_mats(FNET_GROUP_DIM)
    cs = np.concatenate([cc, -sc], axis=0)
    const = lambda i: (0, 0)
    ct, st, cs = (jnp.asarray(m, F32).astype(BF16) for m in (ct, st, cs))
    return pl.pallas_call(
        _fnet_kernel,
        grid=(n_tok // tm,),
        in_specs=[pl.BlockSpec((tm, D_MODEL), lambda i: (i, 0)),
                  pl.BlockSpec((1, D_MODEL), const),
                  _mod_spec(1, cond_of_tile), _mod_spec(0, cond_of_tile), _mod_spec(2, cond_of_tile),
                  pl.BlockSpec((seq, seq), const),
                  pl.BlockSpec((seq, seq), const),
                  pl.BlockSpec((2 * FNET_GROUP_DIM, FNET_GROUP_DIM), const),
                  pl.BlockSpec((None, D_MODEL, D_MODEL), lambda i: (layer, 0, 0)),
                  pl.BlockSpec((1, D_MODEL), const)],
        out_specs=pl.BlockSpec((tm, D_MODEL), lambda i: (i, 0)),
        out_shape=jax.ShapeDtypeStruct((n_tok, D_MODEL), F32),
        scratch_shapes=[pltpu.VMEM((D_MODEL, D_MODEL), BF16)],
        compiler_params=_params("arbitrary"),
        name="fnet",
    )(x, norm_g.reshape(1, D_MODEL), mod, mod, mod,
      ct, st, cs, w, b.reshape(1, D_MODEL))


SUBLANES = 8


def _shift_tokens(x, delta, period):
    n, c = x.shape
    assert n % period == 0 and period % SUBLANES == 0 and 0 < abs(delta) < period
    blocks = range(0, n, period)
    if delta % SUBLANES == 0:
        zeros = jnp.zeros((abs(delta), c), x.dtype)
        if delta > 0:
            pieces = [piece for b in blocks for piece in (x[b + delta:b + period], zeros)]
        else:
            pieces = [piece for b in blocks for piece in (zeros, x[b:b + period + delta])]
        return jnp.concatenate(pieces, axis=0)
    assert abs(delta) < SUBLANES
    y = pltpu.roll(x, (-delta) % n, 0)
    r = lax.broadcasted_iota(jnp.int32, (SUBLANES, 1), 0) + delta
    keep = ((r >= 0) & (r <= SUBLANES - 1)).astype(x.dtype)
    if delta < 0:
        pieces = [piece for b in blocks for piece in (y[b:b + SUBLANES] * keep, y[b + SUBLANES:b + period])]
    else:
        pieces = [piece for b in blocks
                  for piece in (y[b:b + period - SUBLANES], y[b + period - SUBLANES:b + period] * keep)]
    return jnp.concatenate(pieces, axis=0)


def _dwconv_tokens(g, cw_ref, seq, rows):
    width = seq // rows
    g_cols = (_shift_tokens(g, -1, width), g, _shift_tokens(g, 1, width))
    out = None
    for di in (-1, 0, 1):
        if rows == 1 and di != 0:
            continue
        r = sum(g_cols[dj] * cw_ref[pl.ds(3 * (di + 1) + dj, 1), :] for dj in range(3))
        if di != 0:
            r = _shift_tokens(r, di * width, seq)
        out = r if out is None else out + r
    return out


def _ffn_kernel(x_ref, ng_ref, sc_ref, sh_ref, g2_ref, wa_ref, wg_ref, cw_ref, cb_ref, wd_ref, nf_ref,
                y_ref, h_scr, acc_scr, *, seq, rows, final_norm):
    j = pl.program_id(1)

    @pl.when(j == 0)
    def _():
        h_scr[...] = _norm_mod(x_ref[...], ng_ref[...], sc_ref[0], sh_ref[0]).astype(BF16)
        acc_scr[...] = jnp.zeros_like(acc_scr)

    tm = h_scr.shape[0]
    sub = tm // FFN_SUB_TILES
    parts = [pl.ds(r, sub) for r in range(0, tm, sub)]
    wa, wg, wd = (w[...].astype(BF16) for w in (wa_ref, wg_ref, wd_ref))
    gs = [jnp.dot(h_scr[p, :], wg, preferred_element_type=F32) for p in parts]
    as_ = [jnp.dot(h_scr[p, :], wa, preferred_element_type=F32) for p in parts]
    per_seq = max(seq // sub, 1)
    convs = []
    for i in range(0, len(parts), per_seq):
        g = gs[i] if per_seq == 1 else jnp.concatenate(gs[i:i + per_seq], axis=0)
        c = _dwconv_tokens(g, cw_ref, seq, rows) + cb_ref[...]
        convs += [c[k * sub:(k + 1) * sub] for k in range(per_seq)]
    acts = [(jax.nn.silu(c) * a).astype(BF16) for c, a in zip(convs, as_)]
    for p, act in zip(parts, acts):
        acc_scr[p, :] += jnp.dot(act, wd, preferred_element_type=F32)

    @pl.when(j == pl.num_programs(1) - 1)
    def _():
        y = x_ref[...] + g2_ref[0] * acc_scr[...]
        if final_norm:
            y = _rms(y, nf_ref[...])
        y_ref[...] = y


def _conv_ffn(x, seq, rows, layer, norm_g, mod, cond_of_tile, w_up, conv_w, conv_b, w_down, norm_final, final_norm):
    n_tok = x.shape[0]
    tm, tf = FFN_TOKEN_TILE, FFN_FF_TILE
    n_ff_tiles = D_FF // tf
    depth = w_up.shape[0]
    return pl.pallas_call(
        functools.partial(_ffn_kernel, seq=seq, rows=rows, final_norm=final_norm),
        grid=(n_tok // tm, n_ff_tiles),
        in_specs=[pl.BlockSpec((tm, D_MODEL), lambda i, j: (i, 0)),
                  pl.BlockSpec((1, D_MODEL), lambda i, j: (0, 0)),
                  _mod_spec(4, cond_of_tile), _mod_spec(3, cond_of_tile), _mod_spec(5, cond_of_tile),
                  pl.BlockSpec((None, D_MODEL, tf), lambda i, j: (layer, 0, j)),
                  pl.BlockSpec((None, D_MODEL, tf), lambda i, j: (layer, 0, n_ff_tiles + j)),
                  pl.BlockSpec((None, 9, tf), lambda i, j: (layer, 0, j)),
                  pl.BlockSpec((None, 1, tf), lambda i, j: (layer, 0, j)),
                  pl.BlockSpec((None, tf, D_MODEL), lambda i, j: (layer, j, 0)),
                  pl.BlockSpec((1, D_MODEL), lambda i, j: (0, 0))],
        out_specs=pl.BlockSpec((tm, D_MODEL), lambda i, j: (i, 0)),
        out_shape=jax.ShapeDtypeStruct((n_tok, D_MODEL), F32),
        scratch_shapes=[pltpu.VMEM((tm, D_MODEL), BF16), pltpu.VMEM((tm, D_MODEL), F32)],
        compiler_params=_params("parallel", "arbitrary"),
        name="conv_ffn",
    )(x, norm_g.reshape(1, D_MODEL), mod, mod, mod, w_up, w_up, conv_w.reshape(depth, 9, D_FF),
      conv_b.reshape(depth, 1, D_FF), w_down, norm_final.reshape(1, D_MODEL))


GroupMasks = collections.namedtuple("GroupMasks", "r c same incl strict")


def _chunk_masks():
    r = lax.broadcasted_iota(jnp.int32, (CHUNK, CHUNK), 0)
    c = lax.broadcasted_iota(jnp.int32, (CHUNK, CHUNK), 1)
    return (r >= c, r <= c)


def _group_masks(group=GROUP):
    r = lax.broadcasted_iota(jnp.int32, (group, group), 0)
    c = lax.broadcasted_iota(jnp.int32, (group, group), 1)
    same = (r >> LOG_CHUNK) == (c >> LOG_CHUNK)
    return GroupMasks(r, c, same, (same & (r >= c), same & (r <= c)), (same & (r > c), same & (r < c)))


def _gate_layouts(gt, batch, n_heads, hp, group=GROUP):
    seq = gt.shape[1] // batch
    n_groups, n_chunks = seq // group, seq // CHUNK
    assert seq % group == 0 and n_groups <= 8
    g = jnp.transpose(gt.reshape(N_DIR, 2, n_heads, batch, seq), (3, 2, 0, 1, 4))
    rows = g.reshape(g.shape[:4] + (n_groups, group))
    rows = jnp.pad(rows, ((0, 0),) * 4 + ((0, 8 - n_groups), (0, 0)))
    cols = jnp.swapaxes(g.reshape(batch, n_heads // hp, hp * N_DIR * 2 * n_chunks, CHUNK), -1, -2)
    return rows, cols


def _col_gate(gcol_ref, hh, d, kind, n_chunks):
    start = ((hh * N_DIR + d) * 2 + kind) * n_chunks
    return gcol_ref[0, 0, :, start:start + n_chunks]


def _split3(x):
    x1 = x.astype(BF16)
    r1 = x - x1.astype(F32)
    x2 = r1.astype(BF16)
    return x1, x2, (r1 - x2.astype(F32)).astype(BF16)


def _dot_mask_rhs(x, mask):
    m = x.shape[0]
    y = jnp.dot(jnp.concatenate(_split3(x), axis=0), jnp.where(mask, 1.0, 0.0).astype(BF16),
                preferred_element_type=F32)
    return y[:m] + y[m:2 * m] + y[2 * m:]


def _dot_mask_lhs(mask, x):
    mb = jnp.where(mask, 1.0, 0.0).astype(BF16)
    y1, y2, y3 = (jnp.dot(mb, p, preferred_element_type=F32) for p in _split3(x))
    return y1 + y2 + y3


def _cumsum_rows(row, d, gm):
    return _dot_mask_rhs(row, gm.incl[1 - d])


def _cumsum_cols(col, d, cm):
    return _dot_mask_lhs(cm[d], col)


def _store_cols(dst, col, n_chunks):
    for n in range(n_chunks):
        dst[n * CHUNK:(n + 1) * CHUNK, :] = jnp.broadcast_to(col[:, n:n + 1], (CHUNK, 128))


def _wide(x, width=None):
    reps = (GROUP if width is None else width) // x.shape[1]
    return x if reps == 1 else jnp.concatenate([x] * reps, axis=1)


def _rows(i, size):
    if isinstance(i, int):
        return pl.ds(i * size, size)
    return pl.ds(pl.multiple_of(i * size, size), size)


def _loop(n, body, init):
    if n == 1:
        return body(0, init)
    return lax.fori_loop(0, n, body, init)


def _conv_silu(x, w_ref):
    n = x.shape[0]
    k = w_ref.shape[0]
    acc = None
    for j in range(k):
        delta = j - k // 2
        term = (x if delta == 0 else _shift_tokens(x, delta, n)) * w_ref[pl.ds(j, 1), :]
        acc = term if acc is None else acc + term
    return jax.nn.silu(acc)


def _l2norm(x):
    return x * lax.rsqrt(jnp.sum(x * x, axis=-1, keepdims=True) + EPS)


def _blockdiag_tri_inverse(mats, gm):
    dot16 = lambda x, y: jnp.dot(x, y, preferred_element_type=F32).astype(BF16)
    base = (gm.r >> 3) == (gm.c >> 3)
    eye = jnp.where(gm.r == gm.c, 1.0, 0.0)
    zero16 = jnp.zeros((), BF16)
    a16s = [a.astype(BF16) for a in mats]
    invs = [eye - jnp.where(base, a, 0.0) for a in mats]
    ps = [jnp.where(base, -a16, zero16) for a16 in a16s]
    for _ in range(2):
        ps = [dot16(p, p) for p in ps]
        invs = [inv + jnp.dot(inv.astype(BF16), p, preferred_element_type=F32) for inv, p in zip(invs, ps)]
    for s in (3, 4, 5):
        join = ((gm.r >> s) ^ (gm.c >> s)) == 1
        inv16s = [inv.astype(BF16) for inv in invs]
        tmp = [dot16(jnp.where(join, a16, zero16), inv16) for a16, inv16 in zip(a16s, inv16s)]
        invs = [inv - jnp.dot(inv16, t, preferred_element_type=F32) for inv, inv16, t in zip(invs, inv16s, tmp)]
    return invs


def _gdn_kernel(*refs, seq, hp, zero_init, emit_state):
    alog_ref, dtb_ref, q_ref, k_ref, v_ref, cwq_ref, cwk_ref, cwv_ref, grow_ref, gcol_ref = refs[:10]
    pos = 10
    s0_ref = None
    if not zero_init:
        s0_ref = refs[pos]
        pos += 1
    o_ref = refs[pos]
    pos += 1
    sfin_ref = None
    if emit_state:
        sfin_ref = refs[pos]
        pos += 1
    (q_scr, k_scr, v_scr, gcr_scr, gcb_scr, bb_scr, gl_scr,
     u_scr, w_scr, qd_scr, kd_scr, qkd_scr) = refs[pos:]

    n_chunks = seq // CHUNK
    grp = GDN_GROUP
    cm = _chunk_masks()
    gm = _group_masks(grp)
    head_dirs = [(hh, d) for hh in range(hp) for d in range(N_DIR)]
    lanes = lambda hh: slice(hh * DN_DK, (hh + 1) * DN_DK)

    q_scr[...] = _conv_silu(q_ref[...].astype(F32), cwq_ref)
    k_scr[...] = _conv_silu(k_ref[...].astype(F32), cwk_ref)
    v_scr[...] = _conv_silu(v_ref[...].astype(F32), cwv_ref)
    for hh in range(hp):
        q_scr[:, lanes(hh)] = _l2norm(q_scr[:, lanes(hh)]) * (DN_DK ** -0.5)
        k_scr[:, lanes(hh)] = _l2norm(k_scr[:, lanes(hh)])

    for hh, d in head_dirs:
        head = pl.program_id(1) * hp + hh
        neg_a = -jnp.exp(jnp.full((1, 1), alog_ref[d, head], F32))
        dtb = dtb_ref[d, head]
        gcr_scr[hh, d] = _cumsum_rows(neg_a * _softplus(grow_ref[0, hh, d, 0] + dtb), d, gm)
        gc_col = _cumsum_cols(neg_a * _softplus(_col_gate(gcol_ref, hh, d, 0, n_chunks) + dtb), d, cm)
        _store_cols(gcb_scr.at[hh, d], gc_col, n_chunks)
        _store_cols(bb_scr.at[hh, d], jax.nn.sigmoid(_col_gate(gcol_ref, hh, d, 1, n_chunks)), n_chunks)
        last = CHUNK - 1 if d == 0 else 0
        _store_cols(gl_scr.at[hh, d], jnp.broadcast_to(gc_col[last:last + 1, :], gc_col.shape), n_chunks)

    gps = min(GDN_GROUPS_PER_STEP, seq // grp)

    def group_step(it, carry):
        gis = [it * gps + k for k in range(gps)]
        rows = [_rows(gi, grp) for gi in gis]
        for h0 in range(0, hp, GDN_HEADS_PER_PASS):
            heads = range(h0, min(h0 + GDN_HEADS_PER_PASS, hp))
            kqs = {}
            for k, r in enumerate(rows):
                for hh in heads:
                    kg, qg = k_scr[r, lanes(hh)], q_scr[r, lanes(hh)]
                    kqs[k, hh] = _dot_nt(jnp.concatenate([kg, qg], axis=0), kg)
            chains = [(k, hh, d) for k in range(len(gis)) for hh in heads for d in range(N_DIR)]
            mats = []
            for k, hh, d in chains:
                gcb = gcb_scr[hh, d, rows[k], :]
                gcr = gcr_scr[hh, d, pl.ds(gis[k], 1), :]
                decay = jnp.exp(jnp.where(gm.incl[d], _wide(gcb, grp) - gcr, -jnp.inf))
                mats.append(jnp.where(gm.strict[d],
                                      kqs[k, hh][:grp] * _wide(bb_scr[hh, d, rows[k], :], grp) * decay, 0.0))
                qkd = kqs[k, hh][grp:] * decay
                qkd_scr[hh, d, rows[k], :] = sum(qkd[:, n * CHUNK:(n + 1) * CHUNK] for n in range(grp // CHUNK))
            t_invs = _blockdiag_tri_inverse(mats, gm)
            for (k, hh, d), t_inv in zip(chains, t_invs):
                r = rows[k]
                qg, kg, vg = q_scr[r, lanes(hh)], k_scr[r, lanes(hh)], v_scr[r, lanes(hh)]
                gcb = gcb_scr[hh, d, r, :]
                bb = bb_scr[hh, d, r, :]
                e_gc = jnp.exp(gcb)
                uw = _dot(t_inv, jnp.concatenate([vg * bb, kg * bb * e_gc], axis=1))
                u_scr[hh, d, r, :] = uw[:, :DN_DV]
                w_scr[hh, d, r, :] = uw[:, DN_DV:]
                qd_scr[hh, d, r, :] = qg * e_gc
                kd_scr[hh, d, r, :] = kg * jnp.exp(gl_scr[hh, d, r, :] - gcb)
        return carry

    _loop(seq // (grp * gps), group_step, 0)

    o_ref[...] = jnp.zeros_like(o_ref)

    def chunk_step(n, carry):
        idxs = [n if d == 0 else n_chunks - 1 - n for _, d in head_dirs]
        rows = [_rows(idx, CHUNK) for idx in idxs]
        wqs = [_dot(jnp.concatenate([w_scr[hh, d, r, :], qd_scr[hh, d, r, :]], axis=0), s)
               for (hh, d), r, s in zip(head_dirs, rows, carry)]
        v_news = [u_scr[hh, d, r, :] - wq[:CHUNK] for (hh, d), r, wq in zip(head_dirs, rows, wqs)]
        o_ns = [wq[CHUNK:] + _dot(qkd_scr[hh, d, r, :], v_new)
                for (hh, d), r, wq, v_new in zip(head_dirs, rows, wqs, v_news)]
        new = []
        for (hh, d), idx, r, s, v_new in zip(head_dirs, idxs, rows, carry, v_news):
            s_decay = jnp.exp(gl_scr[hh, d, pl.ds(idx * CHUNK, 1), :])
            new.append(s * s_decay + _dot_tn(kd_scr[hh, d, r, :], v_new))
        for (hh, d), r, o_n in zip(head_dirs, rows, o_ns):
            o_ref[r, lanes(hh)] += o_n
        return tuple(new)

    if zero_init:
        init = (jnp.zeros((DN_DK, DN_DV), F32),) * len(head_dirs)
    else:
        init = tuple(s0_ref[0, d, hh] for hh, d in head_dirs)
    fin = lax.fori_loop(0, n_chunks, chunk_step, init)
    if emit_state:
        for i, (hh, d) in enumerate(head_dirs):
            sfin_ref[0, d, hh] = fin[i]


def _gdn_scan(main, gt, batch, seq, conv_w, a_log, dt_bias, s0, emit_state):
    n_tok = batch * seq
    nh = DN_HEADS
    hp = min(nh, max(2, GDN_STEP_POSITIONS // seq))
    n_hb = nh // hp
    assert seq % (GDN_GROUP * min(GDN_GROUPS_PER_STEP, seq // GDN_GROUP)) == 0
    grow, gcol = _gate_layouts(gt, batch, nh, hp, GDN_GROUP)
    zero_init = s0 is None
    smem = pl.BlockSpec(memory_space=pltpu.SMEM)
    qkv_spec = lambda off: pl.BlockSpec((seq, hp * DN_DK), lambda b, h: (b, off + h))
    cw_spec = lambda off: pl.BlockSpec((DN_CONV, hp * DN_DK), lambda b, h: (0, off + h))
    state_spec = pl.BlockSpec((1, N_DIR, hp, DN_DK, DN_DV), lambda b, h: (b, 0, h, 0, 0))
    in_specs = [smem, smem, qkv_spec(0), qkv_spec(n_hb), qkv_spec(2 * n_hb),
                cw_spec(0), cw_spec(n_hb), cw_spec(2 * n_hb),
                pl.BlockSpec((1, hp) + grow.shape[2:], lambda b, h: (b, h, 0, 0, 0, 0)),
                pl.BlockSpec((1, 1) + gcol.shape[2:], lambda b, h: (b, h, 0, 0))]
    args = [a_log, dt_bias, main, main, main, conv_w, conv_w, conv_w, grow, gcol]
    if not zero_init:
        in_specs.append(state_spec)
        args.append(s0)
    out_specs = [pl.BlockSpec((seq, hp * DN_DV), lambda b, h: (b, h))]
    out_shape = [jax.ShapeDtypeStruct((n_tok, nh * DN_DV), F32)]
    if emit_state:
        out_specs.append(state_spec)
        out_shape.append(jax.ShapeDtypeStruct((batch, N_DIR, nh, DN_DK, DN_DV), F32))
    per_dir = lambda width: pltpu.VMEM((hp, N_DIR, seq, width), F32)
    qkv_scr = pltpu.VMEM((seq, hp * DN_DK), F32)
    outs = pl.pallas_call(
        functools.partial(_gdn_kernel, seq=seq, hp=hp, zero_init=zero_init, emit_state=emit_state),
        grid=(batch, n_hb),
        in_specs=in_specs, out_specs=out_specs, out_shape=out_shape,
        scratch_shapes=[qkv_scr, qkv_scr, qkv_scr,
                        pltpu.VMEM((hp, N_DIR, 8, GDN_GROUP), F32),
                        per_dir(128), per_dir(128), per_dir(128),
                        per_dir(DN_DV), per_dir(DN_DK), per_dir(DN_DK), per_dir(DN_DK), per_dir(CHUNK)],
        compiler_params=_params("parallel", "parallel"),
        name="gdn_scan",
    )(*args)
    return outs[0], (outs[1] if emit_state else None)


ML_AUG = 2 * ML_DV


def _mlstm_kernel(*refs, seq, hp, zero_init, emit_state):
    bi_ref, bf_ref, q_ref, k_ref, v_ref, kt_ref, grow_ref, gcol_ref = refs[:8]
    pos = 8
    c0_ref = m0_ref = None
    if not zero_init:
        c0_ref, m0_ref = refs[pos:pos + 2]
        pos += 2
    o_ref = refs[pos]
    pos += 1
    cfin_ref = nfin_ref = mfin_ref = None
    if emit_state:
        cfin_ref, nfin_ref, mfin_ref = refs[pos:pos + 3]
        pos += 3
    bcr_scr, lir_scr, kwr_scr, bcb_scr, mi_scr, dec_scr, dc_scr, cin_scr = refs[pos:]

    n_chunks = seq // CHUNK
    n_groups = seq // GROUP
    head0 = pl.program_id(1) * hp
    cm = _chunk_masks()
    gm = _group_masks()
    ones_col = jnp.where(lax.broadcasted_iota(jnp.int32, (GROUP, ML_DV), 1) == 0, 1.0, 0.0)
    head_dirs = [(hh, d) for hh in range(hp) for d in range(N_DIR)]

    m_fin = {}
    row_group = lax.broadcasted_iota(jnp.int32, (8, GROUP), 0)
    row_chunk = lax.broadcasted_iota(jnp.int32, (8, GROUP), 1) >> LOG_CHUNK
    for hh, d in head_dirs:
        b_i = bi_ref[d, head0 + hh]
        b_f = bf_ref[d, head0 + hh]
        li_row = grow_ref[0, hh, d, 0] + b_i
        lf_row = -_softplus(-(grow_ref[0, hh, d, 1] + b_f))
        bc_row = _cumsum_rows(lf_row, d, gm)
        lir_scr[hh, d] = li_row
        bcr_scr[hh, d] = bc_row
        w_row = _dot_mask_rhs(lf_row, gm.same) - bc_row + li_row
        m_out_row = jnp.zeros((8, GROUP), F32)
        li_col = _col_gate(gcol_ref, hh, d, 0, n_chunks) + b_i
        bc_col = _cumsum_cols(-_softplus(-(_col_gate(gcol_ref, hh, d, 1, n_chunks) + b_f)), d, cm)
        last = CHUNK - 1 if d == 0 else 0
        b_last = bc_col[last:last + 1, :]
        w_col = b_last - bc_col + li_col
        w_max = jnp.max(w_col, axis=0, keepdims=True)
        m = jnp.zeros((1, 1), F32) if zero_init else m0_ref[0, d, hh][:, 0:1]
        for step in range(n_chunks):
            n = step if d == 0 else n_chunks - 1 - step
            sl = slice(n * CHUNK, (n + 1) * CHUNK)
            m_new = jnp.maximum(b_last[:, n:n + 1] + m, w_max[:, n:n + 1])
            bcb_scr[hh, d, sl, :] = jnp.broadcast_to(bc_col[:, n:n + 1], (CHUNK, 128))
            mi_scr[hh, d, sl, :] = jnp.broadcast_to(m, (CHUNK, 128))
            dec_scr[hh, d, n:n + 1, :] = jnp.broadcast_to(jnp.exp(b_last[:, n:n + 1] + m - m_new), (1, 128))
            in_chunk = (row_group == n // CHUNKS_PER_GROUP) & (row_chunk == n % CHUNKS_PER_GROUP)
            m_out_row = jnp.where(in_chunk, m_new, m_out_row)
            m = m_new
        m_fin[hh, d] = m
        kwr_scr[hh, d] = jnp.exp(w_row - m_out_row)

    def load_kv(rows, hh):
        kg = k_ref[rows, hh * ML_DQK:(hh + 1) * ML_DQK].astype(F32)
        v_aug = jnp.concatenate([v_ref[rows, hh * ML_DV:(hh + 1) * ML_DV].astype(F32), ones_col], axis=1)
        return kg, v_aug

    def delta_step(gi, carry):
        rows = _rows(gi, GROUP)
        v_augs = [load_kv(rows, hh)[1] for hh in range(hp)]
        kt = kt_ref[gi]
        lhs = []
        for hh, d in head_dirs:
            kw_t = kt[hh * ML_DQK:(hh + 1) * ML_DQK, :] * kwr_scr[hh, d, pl.ds(gi, 1), :]
            lhs.append(jnp.where(gm.same, jnp.concatenate([kw_t] * CHUNKS_PER_GROUP, axis=0), 0.0))
        for (hh, d), kw in zip(head_dirs, lhs):
            dc_scr[hh, d, rows, :] = _dot(kw, v_augs[hh])
        return carry

    _loop(n_groups, delta_step, 0)

    def prefix_step(n, carry):
        new = []
        for (hh, d), c_aug in zip(head_dirs, carry):
            idx = n if d == 0 else n_chunks - 1 - n
            rows = _rows(idx, CHUNK)
            cin_scr[hh, d, rows, :] = c_aug
            new.append(c_aug * _wide(dec_scr[hh, d, pl.ds(idx, 1), :], ML_AUG) + dc_scr[hh, d, rows, :])
        return tuple(new)

    if zero_init:
        init = (jnp.zeros((ML_DQK, ML_AUG), F32),) * len(head_dirs)
    else:
        init = tuple(c0_ref[0, d, hh] for hh, d in head_dirs)
    fin = lax.fori_loop(0, n_chunks, prefix_step, init)

    def out_step(gi, carry):
        rows = _rows(gi, GROUP)
        kvs = [load_kv(rows, hh) for hh in range(hp)]
        qgs = [q_ref[rows, hh * ML_DQK:(hh + 1) * ML_DQK].astype(F32) * (ML_DQK ** -0.5) for hh in range(hp)]
        qks = [_dot_nt(qg, kv[0]) for qg, kv in zip(qgs, kvs)]
        chunk = lambda n: slice(n * CHUNK, (n + 1) * CHUNK)
        inters = []
        for hh, d in head_dirs:
            c_in = cin_scr[hh, d, rows, :]
            inters.append(jnp.concatenate(
                [_dot(qgs[hh][chunk(n)], c_in[chunk(n)]) for n in range(CHUNKS_PER_GROUP)], axis=0))
        m_ts, b_ms, ss = [], [], []
        for hh, d in head_dirs:
            b_colb = bcb_scr[hh, d, rows, :]
            b_m = b_colb[:, 0:1] + mi_scr[hh, d, rows, :][:, 0:1]
            d_log = jnp.where(gm.incl[d], _wide(b_colb) - bcr_scr[hh, d, pl.ds(gi, 1), :]
                              + lir_scr[hh, d, pl.ds(gi, 1), :], -jnp.inf)
            m_t = jnp.maximum(b_m, jnp.max(d_log, axis=-1, keepdims=True))
            m_ts.append(m_t)
            b_ms.append(b_m)
            ss.append(qks[hh] * jnp.exp(d_log - m_t))
        intras = [_dot(s, kvs[hh][1]) for (hh, d), s in zip(head_dirs, ss)]
        hs = []
        for inter, intra, m_t, b_m in zip(inters, intras, m_ts, b_ms):
            num = jnp.exp(b_m - m_t) * inter + intra
            den = num[:, ML_DV:ML_DV + 1]
            hs.append(num[:, :ML_DV] / jnp.maximum(jnp.abs(den), jnp.exp(-m_t)))
        for hh in range(hp):
            o_ref[rows, hh * ML_DV:(hh + 1) * ML_DV] = hs[N_DIR * hh] + hs[N_DIR * hh + 1]
        return carry

    _loop(n_groups, out_step, 0)

    if emit_state:
        for i, (hh, d) in enumerate(head_dirs):
            cfin_ref[0, d, hh] = fin[i][:, :ML_DV]
            nfin_ref[0, d, hh] = fin[i][:, ML_DV:]
            mfin_ref[0, d, hh] = jnp.broadcast_to(m_fin[hh, d], (1, 128))


def _mlstm_scan(main, kt, gt, batch, seq, b_i, b_f, state0, emit_state):
    n_tok = batch * seq
    n_groups = seq // GROUP
    n_chunks = seq // CHUNK
    nh = ML_HEADS
    hp = min(nh, max(2, SCAN_STEP_POSITIONS // seq))
    n_hb = nh // hp
    grow, gcol = _gate_layouts(gt, batch, nh, hp)
    zero_init = state0 is None
    smem = pl.BlockSpec(memory_space=pltpu.SMEM)
    k_off = (nh * ML_DQK) // (hp * ML_DQK)
    v_off = (2 * nh * ML_DQK) // (hp * ML_DV)
    c_spec = pl.BlockSpec((1, N_DIR, hp, ML_DQK, ML_AUG), lambda b, p: (b, 0, p, 0, 0))
    m_spec = pl.BlockSpec((1, N_DIR, hp, 1, 128), lambda b, p: (b, 0, p, 0, 0))
    in_specs = [smem, smem,
                pl.BlockSpec((seq, hp * ML_DQK), lambda b, p: (b, p)),
                pl.BlockSpec((seq, hp * ML_DQK), lambda b, p: (b, k_off + p)),
                pl.BlockSpec((seq, hp * ML_DV), lambda b, p: (b, v_off + p)),
                pl.BlockSpec((n_groups, hp * ML_DQK, GROUP), lambda b, p: (b, p, 0)),
                pl.BlockSpec((1, hp) + grow.shape[2:], lambda b, p: (b, p, 0, 0, 0, 0)),
                pl.BlockSpec((1, 1) + gcol.shape[2:], lambda b, p: (b, p, 0, 0))]
    args = [b_i, b_f, main, main, main, kt, grow, gcol]
    if not zero_init:
        in_specs += [c_spec, m_spec]
        args += list(state0)
    out_specs = [pl.BlockSpec((seq, hp * ML_DV), lambda b, p: (b, p))]
    out_shape = [jax.ShapeDtypeStruct((n_tok, nh * ML_DV), F32)]
    if emit_state:
        half_spec = pl.BlockSpec((1, N_DIR, hp, ML_DQK, ML_DV), lambda b, p: (b, 0, p, 0, 0))
        out_specs += [half_spec, half_spec, m_spec]
        out_shape += [jax.ShapeDtypeStruct((batch, N_DIR, nh, ML_DQK, ML_DV), F32),
                      jax.ShapeDtypeStruct((batch, N_DIR, nh, ML_DQK, ML_AUG - ML_DV), F32),
                      jax.ShapeDtypeStruct((batch, N_DIR, nh, 1, 128), F32)]
    per_hd = lambda rows, width: pltpu.VMEM((hp, N_DIR, rows, width), F32)
    outs = pl.pallas_call(
        functools.partial(_mlstm_kernel, seq=seq, hp=hp, zero_init=zero_init, emit_state=emit_state),
        grid=(batch, n_hb),
        in_specs=in_specs, out_specs=out_specs, out_shape=out_shape,
        scratch_shapes=[per_hd(8, GROUP), per_hd(8, GROUP), per_hd(8, GROUP),
                        per_hd(seq, 128), per_hd(seq, 128),
                        per_hd(max(n_chunks, 8), 128),
                        per_hd(seq, ML_AUG), per_hd(seq, ML_AUG)],
        compiler_params=_params("parallel", "parallel"),
        name="mlstm_scan",
    )(*args)
    return outs[0], (tuple(outs[1:4]) if emit_state else None)


def _trunk(x, batch, seq, rows, mods, tokens_per_cond, st_d, st_ml, emit_state, p):
    depth = p["w_ada"].shape[0]
    tile_cond = lambda tile: (lambda i: (i * tile) // tokens_per_cond)
    new_d = new_ml = None
    for layer in range(depth):
        mod = mods[layer]
        kind, j = layer % 3, layer // 3
        if kind == 0:
            x = _fnet_mix(x, seq, j, p["norm_mix"][layer], mod, tokens_per_cond, p["fnet_w"], p["fnet_b"][j])
        elif kind == 1:
            n_main = DN_HEADS * (2 * DN_DK + 2 * DN_DV)
            main, gt = _norm_proj(x, p["norm_mix"][layer], mod, tokens_per_cond, p["dn_w_in"], j, n_main)
            s0 = None if st_d is None else st_d[:, j]
            o, sfin = _gdn_scan(main, gt, batch, seq, p["dn_conv_w"][j], p["dn_a_log"][j], p["dn_dt_bias"][j],
                                s0, emit_state)
            if emit_state:
                new_d = sfin
            x = _gated_out(x, o, main, 3, p["dn_norm"][j], mod, tile_cond(OUT_TOKEN_TILE), p["dn_w_out"][j],
                           DN_HEADS, jax.nn.silu)
        else:
            n_main = 2 * ML_HEADS * ML_DQK + 2 * ML_HEADS * ML_DV
            k_tile = (ML_HEADS * ML_DQK) // PROJ_COL_TILE
            assert ML_HEADS * ML_DQK == PROJ_COL_TILE
            main, gt, kt = _norm_proj(x, p["norm_mix"][layer], mod, tokens_per_cond, p["ml_w_in"], j, n_main,
                                      t_tile=k_tile)
            state0 = None
            if st_ml is not None:
                c0, n0, m0 = (s[:, j] for s in st_ml)
                pad = jnp.zeros(c0.shape[:-1] + (ML_AUG - ML_DV - 1,), F32)
                c_aug0 = jnp.concatenate([c0, n0[..., None], pad], axis=-1)
                m0b = jnp.broadcast_to(m0[..., None, None], m0.shape + (1, 128))
                state0 = (c_aug0, m0b)
            o, fin = _mlstm_scan(main, kt, gt, batch, seq, p["ml_b_i"][j], p["ml_b_f"][j], state0, emit_state)
            if emit_state:
                new_ml = (fin[0], fin[1][..., 0], fin[2][..., 0, 0])
            x = _gated_out(x, o, main, 2, p["ml_norm"][j], mod, tile_cond(OUT_TOKEN_TILE), p["ml_w_out"][j],
                           ML_HEADS, jax.nn.sigmoid)
        x = _conv_ffn(x, seq, rows, layer, p["norm_ffn"][layer], mod, tile_cond(FFN_TOKEN_TILE), p["ffn_w_up"],
                      p["ffn_conv_w"], p["ffn_conv_b"], p["ffn_w_down"], p["norm_final"],
                      final_norm=(layer == depth - 1))
    return x, new_d, new_ml


def kernel(x_prompt, x_sample, state_delta, state_mlstm_c, state_mlstm_n, state_mlstm_m, c, c_ctx, w_ada, b_ada, norm_mix, norm_ffn, norm_final, ffn_w_up, ffn_conv_w, ffn_conv_b, ffn_w_down, fnet_w, fnet_b, dn_w_in, dn_conv_w, dn_a_log, dn_dt_bias, dn_norm, dn_w_out, ml_w_in, ml_b_i, ml_b_f, ml_norm, ml_w_out):
    p = dict(w_ada=w_ada, norm_mix=norm_mix, norm_ffn=norm_ffn, norm_final=norm_final, ffn_w_up=ffn_w_up,
             ffn_conv_w=ffn_conv_w, ffn_conv_b=ffn_conv_b, ffn_w_down=ffn_w_down, fnet_w=fnet_w, fnet_b=fnet_b,
             dn_w_in=dn_w_in, dn_conv_w=dn_conv_w, dn_a_log=dn_a_log, dn_dt_bias=dn_dt_bias, dn_norm=dn_norm,
             dn_w_out=dn_w_out, ml_w_in=ml_w_in, ml_b_i=ml_b_i, ml_b_f=ml_b_f, ml_norm=ml_norm, ml_w_out=ml_w_out)
    b_ctx, t_ctx, _ = x_prompt.shape
    b_smp, t_smp, _ = x_sample.shape
    depth = w_ada.shape[0]

    cond8 = jnp.concatenate([c_ctx[None, :], c, jnp.zeros((8 - 1 - b_smp, D_MODEL), F32)], axis=0)
    mods = _adaln_all(cond8, w_ada, b_ada)
    mods_ctx = [mods[l, 0:1].reshape(1, 1, -1) for l in range(depth)]
    mods_smp = [mods[l, 1:1 + b_smp].reshape(b_smp, 1, -1) for l in range(depth)]

    y_ctx, new_d, new_ml = _trunk(x_prompt.reshape(b_ctx * t_ctx, D_MODEL), b_ctx, t_ctx, 1, mods_ctx,
                                  b_ctx * t_ctx, None, None, True, p)
    y_smp, _, _ = _trunk(x_sample.reshape(b_smp * t_smp, D_MODEL), b_smp, t_smp, t_smp // GRID_W, mods_smp,
                         t_smp, state_delta, (state_mlstm_c, state_mlstm_n, state_mlstm_m), False, p)

    new_c, new_n, new_m = new_ml
    return (y_ctx.reshape(b_ctx, t_ctx, D_MODEL), y_smp.reshape(b_smp, t_smp, D_MODEL),
            new_d[:, None], new_c[:, None], new_n[:, None], new_m[:, None])
```

```python
import collections
import functools

import numpy as np
import jax
import jax.numpy as jnp
from jax import lax
from jax.experimental import pallas as pl
from jax.experimental.pallas import tpu as pltpu

F32 = jnp.float32
BF16 = jnp.bfloat16

D_MODEL = 1024
EPS = 1e-6
N_DIR = 2
CHUNK = 64
LOG_CHUNK = 6
GROUP = 256
CHUNKS_PER_GROUP = GROUP // CHUNK
GDN_GROUP = 128
GDN_GROUPS_PER_STEP = 4
GDN_HEADS_PER_PASS = 4
GDN_STEP_POSITIONS = 2048
FNET_GROUP_DIM = 256
DN_HEADS, DN_DK, DN_DV, DN_CONV = 8, 128, 128, 5
ML_HEADS, ML_DQK, ML_DV = 8, 64, 128
D_FF = 2816
GRID_W = 64

FNET_STEP_TOKENS = 1024
FFN_TOKEN_TILE = 1024
FFN_FF_TILE = 256
FFN_TILES_PER_STEP = 2
FFN_SUB_TILES = 4
PROJ_TOKEN_TILE = 2048
PROJ_COL_TILE = 512
OUT_TOKEN_TILE = 512
ADA_COL_TILE = 1536
SCAN_STEP_POSITIONS = 1024
VMEM_LIMIT = 56 * 1024 * 1024


def _params(*sem):
    return pltpu.CompilerParams(dimension_semantics=sem, vmem_limit_bytes=VMEM_LIMIT)


def _dot(a, b):
    return jnp.dot(a.astype(BF16), b.astype(BF16), preferred_element_type=F32)


def _dot_nt(a, b):
    return lax.dot_general(a.astype(BF16), b.astype(BF16), (((1,), (1,)), ((), ())),
                           preferred_element_type=F32)


def _dot_tn(a, b):
    return lax.dot_general(a.astype(BF16), b.astype(BF16), (((0,), (0,)), ((), ())),
                           preferred_element_type=F32)


def _rms(x, g):
    return x * lax.rsqrt(jnp.mean(x * x, axis=-1, keepdims=True) + EPS) * g


def _norm_mod(x, g, sc, sh):
    return _rms(x, g) * (1.0 + sc) + sh


def _softplus(x):
    return jnp.maximum(x, 0.0) + jnp.log1p(jnp.exp(-jnp.abs(x)))


def _mod_spec(chunk, cond_of_tile):
    return pl.BlockSpec((1, 1, D_MODEL), lambda i, *_: (cond_of_tile(i), 0, chunk))


def _adaln_kernel(c_ref, w_ref, b_ref, o_ref):
    s = jax.nn.silu(c_ref[...])
    o_ref[0] = _dot(s, w_ref[0]) + b_ref[0]


def _adaln_all(cond8, w_ada, b_ada):
    depth = w_ada.shape[0]
    n_out = w_ada.shape[2]
    tn = ADA_COL_TILE
    return pl.pallas_call(
        _adaln_kernel,
        grid=(depth, n_out // tn),
        in_specs=[pl.BlockSpec((8, D_MODEL), lambda l, j: (0, 0)),
                  pl.BlockSpec((1, D_MODEL, tn), lambda l, j: (l, 0, j)),
                  pl.BlockSpec((1, 1, tn), lambda l, j: (l, 0, j))],
        out_specs=pl.BlockSpec((1, 8, tn), lambda l, j: (l, 0, j)),
        out_shape=jax.ShapeDtypeStruct((depth, 8, n_out), F32),
        compiler_params=_params("parallel", "parallel"),
        name="adaln",
    )(cond8, w_ada, b_ada.reshape(depth, 1, n_out))


def _norm_proj_kernel(x_ref, g_ref, sc_ref, sh_ref, w_ref, wg_ref, o_ref, gate_ref, *rest, t_tile):
    h_scr = rest[-1]

    @pl.when(pl.program_id(1) == 0)
    def _():
        h = _norm_mod(x_ref[...], g_ref[...], sc_ref[0], sh_ref[0]).astype(BF16)
        h_scr[...] = h
        gate_ref[...] = _dot_nt(wg_ref[...], h)

    y = _dot_nt(h_scr[...], w_ref[...])
    o_ref[...] = y.astype(o_ref.dtype)

    if t_tile is not None:
        t_ref = rest[0]

        @pl.when(pl.program_id(1) == t_tile)
        def _():
            for g in range(t_ref.shape[0]):
                t_ref[g] = y[g * GROUP:(g + 1) * GROUP, :].T


def _norm_proj(x, norm_g, mod, tokens_per_cond, w_stack, layer, n_main, t_tile=None):
    n_tok = x.shape[0]
    tm, tn = min(PROJ_TOKEN_TILE, tokens_per_cond), PROJ_COL_TILE
    cond_of_tile = lambda i: (i * tm) // tokens_per_cond
    w_t = jnp.swapaxes(w_stack, 1, 2)
    n_gate = w_t.shape[1] - n_main
    assert n_main % n_gate == 0 and n_gate % 8 == 0
    out_specs = [pl.BlockSpec((tm, tn), lambda i, j: (i, j)),
                 pl.BlockSpec((n_gate, tm), lambda i, j: (0, i))]
    out_shape = [jax.ShapeDtypeStruct((n_tok, n_main), BF16),
                 jax.ShapeDtypeStruct((n_gate, n_tok), F32)]
    if t_tile is not None:
        out_specs.append(pl.BlockSpec((tm // GROUP, tn, GROUP), lambda i, j: (i, 0, 0)))
        out_shape.append(jax.ShapeDtypeStruct((n_tok // GROUP, tn, GROUP), F32))
    return pl.pallas_call(
        functools.partial(_norm_proj_kernel, t_tile=t_tile),
        grid=(n_tok // tm, n_main // tn),
        in_specs=[pl.BlockSpec((tm, D_MODEL), lambda i, j: (i, 0)),
                  pl.BlockSpec((1, D_MODEL), lambda i, j: (0, 0)),
                  _mod_spec(1, cond_of_tile), _mod_spec(0, cond_of_tile),
                  pl.BlockSpec((None, tn, D_MODEL), lambda i, j: (layer, j, 0)),
                  pl.BlockSpec((None, n_gate, D_MODEL), lambda i, j: (layer, n_main // n_gate, 0))],
        out_specs=out_specs, out_shape=out_shape,
        scratch_shapes=[pltpu.VMEM((tm, D_MODEL), BF16)],
        compiler_params=_params("parallel", "arbitrary"),
        name="norm_proj",
    )(x, norm_g.reshape(1, D_MODEL), mod, mod, w_t, w_t)


def _gated_out_kernel(x_ref, o_ref, z_ref, ng_ref, g1_ref, w_ref, y_ref, wb_scr, *, n_heads, gate_fn):
    @pl.when(pl.program_id(0) == 0)
    def _():
        wb_scr[...] = w_ref[...].astype(BF16)

    dv = o_ref.shape[1] // n_heads
    parts = []
    for h in range(n_heads):
        sl = slice(h * dv, (h + 1) * dv)
        parts.append((_rms(o_ref[:, sl], ng_ref[...]) * gate_fn(z_ref[:, sl].astype(F32))).astype(BF16))
    hs = jnp.concatenate(parts, axis=1)
    y = jnp.dot(hs, wb_scr[...], preferred_element_type=F32)
    y_ref[...] = x_ref[...] + g1_ref[0] * y


def _gated_out(x, o, main, z_block, norm_g, mod, cond_of_tile, w_out, n_heads, gate_fn):
    n_tok = x.shape[0]
    tm = OUT_TOKEN_TILE
    width = o.shape[1]
    return pl.pallas_call(
        functools.partial(_gated_out_kernel, n_heads=n_heads, gate_fn=gate_fn),
        grid=(n_tok // tm,),
        in_specs=[pl.BlockSpec((tm, D_MODEL), lambda i: (i, 0)),
                  pl.BlockSpec((tm, width), lambda i: (i, 0)),
                  pl.BlockSpec((tm, width), lambda i: (i, z_block)),
                  pl.BlockSpec((1, width // n_heads), lambda i: (0, 0)),
                  _mod_spec(2, cond_of_tile),
                  pl.BlockSpec((width, D_MODEL), lambda i: (0, 0))],
        out_specs=pl.BlockSpec((tm, D_MODEL), lambda i: (i, 0)),
        out_shape=jax.ShapeDtypeStruct((n_tok, D_MODEL), F32),
        scratch_shapes=[pltpu.VMEM((width, D_MODEL), BF16)],
        compiler_params=_params("arbitrary"),
        name="gated_out",
    )(x, o, main, norm_g.reshape(1, -1), mod, w_out)


def _dft_mats(n):
    k = np.arange(n, dtype=np.int64)
    ang = 2.0 * np.pi * ((k[:, None] * k[None, :]) % n).astype(np.float64) / n
    s = 1.0 / np.sqrt(n)
    return np.cos(ang) * s, np.sin(ang) * s


def _fnet_kernel(x_ref, ng_ref, sc_ref, sh_ref, g1_ref, ct_ref, st_ref, cs_ref, w_ref, b_ref, y_ref, wb_scr):
    @pl.when(pl.program_id(0) == 0)
    def _():
        wb_scr[...] = w_ref[...].astype(BF16)

    seq = ct_ref.shape[0]
    seqs = [pl.ds(r, seq) for r in range(0, x_ref.shape[0], seq)]
    gd = FNET_GROUP_DIM
    hs = [_norm_mod(x_ref[s, :], ng_ref[...], sc_ref[0], sh_ref[0]).astype(BF16) for s in seqs]
    ps = [jnp.dot(ct_ref[...], h, preferred_element_type=F32) for h in hs]
    qs = [jnp.dot(st_ref[...], h, preferred_element_type=F32) for h in hs]
    fs = []
    for p, q in zip(ps, qs):
        parts = []
        for g in range(D_MODEL // gd):
            sl = slice(g * gd, (g + 1) * gd)
            pq = jnp.concatenate([p[:, sl], q[:, sl]], axis=1).astype(BF16)
            parts.append(jnp.dot(pq, cs_ref[...], preferred_element_type=F32).astype(BF16))
        fs.append(jnp.concatenate(parts, axis=1))
    ys = [jnp.dot(f, wb_scr[...], preferred_element_type=F32) + b_ref[...] for f in fs]
    for s, y in zip(seqs, ys):
        y_ref[s, :] = x_ref[s, :] + g1_ref[0] * y


def _fnet_mix(x, seq, layer, norm_g, mod, tokens_per_cond, w, b):
    n_tok = x.shape[0]
    tm = seq * max(1, min(FNET_STEP_TOKENS, tokens_per_cond) // seq)
    cond_of_tile = lambda i: (i * tm) // tokens_per_cond
    ct, st = _dft_mats(seq)
    cc, sc = _dft_mats(FNET_GROUP_DIM)
    cs = np.concatenate([cc, -sc], axis=0)
    const = lambda i: (0, 0)
    ct, st, cs = (jnp.asarray(m, F32).astype(BF16) for m in (ct, st, cs))
    return pl.pallas_call(
        _fnet_kernel,
        grid=(n_tok // tm,),
        in_specs=[pl.BlockSpec((tm, D_MODEL), lambda i: (i, 0)),
                  pl.BlockSpec((1, D_MODEL), const),
                  _mod_spec(1, cond_of_tile), _mod_spec(0, cond_of_tile), _mod_spec(2, cond_of_tile),
                  pl.BlockSpec((seq, seq), const),
                  pl.BlockSpec((seq, seq), const),
                  pl.BlockSpec((2 * FNET_GROUP_DIM, FNET_GROUP_DIM), const),
                  pl.BlockSpec((None, D_MODEL, D_MODEL), lambda i: (layer, 0, 0)),
                  pl.BlockSpec((1, D_MODEL), const)],
        out_specs=pl.BlockSpec((tm, D_MODEL), lambda i: (i, 0)),
        out_shape=jax.ShapeDtypeStruct((n_tok, D_MODEL), F32),
        scratch_shapes=[pltpu.VMEM((D_MODEL, D_MODEL), BF16)],
        compiler_params=_params("arbitrary"),
        name="fnet",
    )(x, norm_g.reshape(1, D_MODEL), mod, mod, mod,
      ct, st, cs, w, b.reshape(1, D_MODEL))


SUBLANES = 8


def _shift_tokens(x, delta, period):
    n, c = x.shape
    assert n % period == 0 and period % SUBLANES == 0 and 0 < abs(delta) < period
    blocks = range(0, n, period)
    if delta % SUBLANES == 0:
        zeros = jnp.zeros((abs(delta), c), x.dtype)
        if delta > 0:
            pieces = [piece for b in blocks for piece in (x[b + delta:b + period], zeros)]
        else:
            pieces = [piece for b in blocks for piece in (zeros, x[b:b + period + delta])]
        return jnp.concatenate(pieces, axis=0)
    assert abs(delta) < SUBLANES
    y = pltpu.roll(x, (-delta) % n, 0)
    r = lax.broadcasted_iota(jnp.int32, (SUBLANES, 1), 0) + delta
    keep = ((r >= 0) & (r <= SUBLANES - 1)).astype(x.dtype)
    if delta < 0:
        pieces = [piece for b in blocks for piece in (y[b:b + SUBLANES] * keep, y[b + SUBLANES:b + period])]
    else:
        pieces = [piece for b in blocks
                  for piece in (y[b:b + period - SUBLANES], y[b + period - SUBLANES:b + period] * keep)]
    return jnp.concatenate(pieces, axis=0)


def _dwconv_tokens(g, cw_ref, seq, rows):
    width = seq // rows
    g_cols = (_shift_tokens(g, -1, width), g, _shift_tokens(g, 1, width))
    out = None
    for di in (-1, 0, 1):
        if rows == 1 and di != 0:
            continue
        r = sum(g_cols[dj] * cw_ref[pl.ds(3 * (di + 1) + dj, 1), :] for dj in range(3))
        if di != 0:
            r = _shift_tokens(r, di * width, seq)
        out = r if out is None else out + r
    return out


def _ffn_kernel(x_ref, ng_ref, sc_ref, sh_ref, g2_ref, *rest, seq, rows, final_norm, n_ff_tiles):
    tiles = [rest[5 * t:5 * t + 5] for t in range(FFN_TILES_PER_STEP)]
    nf_ref, y_ref, h_scr, acc_scr = rest[5 * FFN_TILES_PER_STEP:]
    j = pl.program_id(1)
    last = pl.num_programs(1) - 1

    @pl.when(j == 0)
    def _():
        h_scr[...] = _norm_mod(x_ref[...], ng_ref[...], sc_ref[0], sh_ref[0]).astype(BF16)
        acc_scr[...] = jnp.zeros_like(acc_scr)

    tm = h_scr.shape[0]
    sub = tm // FFN_SUB_TILES
    parts = [pl.ds(r, sub) for r in range(0, tm, sub)]
    per_seq = max(seq // sub, 1)

    def run(tiles):
        gs = [[jnp.dot(h_scr[p, :], wg[...].astype(BF16), preferred_element_type=F32) for p in parts]
              for _, wg, _, _, _ in tiles]
        as_ = [[jnp.dot(h_scr[p, :], wa[...].astype(BF16), preferred_element_type=F32) for p in parts]
               for wa, _, _, _, _ in tiles]
        acts = []
        for (_, _, cw, cb, _), g_t, a_t in zip(tiles, gs, as_):
            convs = []
            for i in range(0, len(parts), per_seq):
                g = g_t[i] if per_seq == 1 else jnp.concatenate(g_t[i:i + per_seq], axis=0)
                c = _dwconv_tokens(g, cw, seq, rows) + cb[...]
                convs += [c[k * sub:(k + 1) * sub] for k in range(per_seq)]
            acts.append([(jax.nn.silu(c) * a).astype(BF16) for c, a in zip(convs, a_t)])
        wd = jnp.concatenate([t[4][...].astype(BF16) for t in tiles], axis=0)
        for k, p in enumerate(parts):
            act = jnp.concatenate([a_t[k] for a_t in acts], axis=1)
            acc_scr[p, :] += jnp.dot(act, wd, preferred_element_type=F32)

    n_tail = n_ff_tiles % FFN_TILES_PER_STEP
    if n_tail:
        pl.when(j < last)(lambda: run(tiles))
        pl.when(j == last)(lambda: run(tiles[:n_tail]))
    else:
        run(tiles)

    @pl.when(j == last)
    def _():
        y = x_ref[...] + g2_ref[0] * acc_scr[...]
        if final_norm:
            y = _rms(y, nf_ref[...])
        y_ref[...] = y


def _conv_ffn(x, seq, rows, layer, norm_g, mod, cond_of_tile, w_up, conv_w, conv_b, w_down, norm_final, final_norm):
    n_tok = x.shape[0]
    tm, tf = FFN_TOKEN_TILE, FFN_FF_TILE
    n_ff_tiles = D_FF // tf
    depth = w_up.shape[0]
    tps = FFN_TILES_PER_STEP
    conv_w, conv_b = conv_w.reshape(depth, 9, D_FF), conv_b.reshape(depth, 1, D_FF)
    tile_specs, tile_args = [], []
    for t in range(tps):
        tile = lambda j, t=t: jnp.minimum(j * tps + t, n_ff_tiles - 1)
        tile_specs += [pl.BlockSpec((None, D_MODEL, tf), lambda i, j, tile=tile: (layer, 0, tile(j))),
                       pl.BlockSpec((None, D_MODEL, tf), lambda i, j, tile=tile: (layer, 0, n_ff_tiles + tile(j))),
                       pl.BlockSpec((None, 9, tf), lambda i, j, tile=tile: (layer, 0, tile(j))),
                       pl.BlockSpec((None, 1, tf), lambda i, j, tile=tile: (layer, 0, tile(j))),
                       pl.BlockSpec((None, tf, D_MODEL), lambda i, j, tile=tile: (layer, tile(j), 0))]
        tile_args += [w_up, w_up, conv_w, conv_b, w_down]
    return pl.pallas_call(
        functools.partial(_ffn_kernel, seq=seq, rows=rows, final_norm=final_norm, n_ff_tiles=n_ff_tiles),
        grid=(n_tok // tm, pl.cdiv(n_ff_tiles, tps)),
        in_specs=[pl.BlockSpec((tm, D_MODEL), lambda i, j: (i, 0)),
                  pl.BlockSpec((1, D_MODEL), lambda i, j: (0, 0)),
                  _mod_spec(4, cond_of_tile), _mod_spec(3, cond_of_tile), _mod_spec(5, cond_of_tile),
                  *tile_specs,
                  pl.BlockSpec((1, D_MODEL), lambda i, j: (0, 0))],
        out_specs=pl.BlockSpec((tm, D_MODEL), lambda i, j: (i, 0)),
        out_shape=jax.ShapeDtypeStruct((n_tok, D_MODEL), F32),
        scratch_shapes=[pltpu.VMEM((tm, D_MODEL), BF16), pltpu.VMEM((tm, D_MODEL), F32)],
        compiler_params=_params("parallel", "arbitrary"),
        name="conv_ffn",
    )(x, norm_g.reshape(1, D_MODEL), mod, mod, mod, *tile_args, norm_final.reshape(1, D_MODEL))


GroupMasks = collections.namedtuple("GroupMasks", "r c same incl strict")


def _chunk_masks():
    r = lax.broadcasted_iota(jnp.int32, (CHUNK, CHUNK), 0)
    c = lax.broadcasted_iota(jnp.int32, (CHUNK, CHUNK), 1)
    return (r >= c, r <= c)


def _group_masks(group=GROUP):
    r = lax.broadcasted_iota(jnp.int32, (group, group), 0)
    c = lax.broadcasted_iota(jnp.int32, (group, group), 1)
    same = (r >> LOG_CHUNK) == (c >> LOG_CHUNK)
    return GroupMasks(r, c, same, (same & (r >= c), same & (r <= c)), (same & (r > c), same & (r < c)))


def _gate_layouts(gt, batch, n_heads, hp, group=GROUP):
    seq = gt.shape[1] // batch
    n_groups, n_chunks = seq // group, seq // CHUNK
    assert seq % group == 0 and n_groups <= 8
    g = jnp.transpose(gt.reshape(N_DIR, 2, n_heads, batch, seq), (3, 2, 0, 1, 4))
    rows = g.reshape(g.shape[:4] + (n_groups, group))
    rows = jnp.pad(rows, ((0, 0),) * 4 + ((0, 8 - n_groups), (0, 0)))
    cols = jnp.swapaxes(g.reshape(batch, n_heads // hp, hp * N_DIR * 2 * n_chunks, CHUNK), -1, -2)
    return rows, cols


def _col_gate(gcol_ref, hh, d, kind, n_chunks):
    start = ((hh * N_DIR + d) * 2 + kind) * n_chunks
    return gcol_ref[0, 0, :, start:start + n_chunks]


def _split3(x):
    x1 = x.astype(BF16)
    r1 = x - x1.astype(F32)
    x2 = r1.astype(BF16)
    return x1, x2, (r1 - x2.astype(F32)).astype(BF16)


def _dot_mask_rhs(x, mask):
    m = x.shape[0]
    y = jnp.dot(jnp.concatenate(_split3(x), axis=0), jnp.where(mask, 1.0, 0.0).astype(BF16),
                preferred_element_type=F32)
    return y[:m] + y[m:2 * m] + y[2 * m:]


def _dot_mask_lhs(mask, x):
    mb = jnp.where(mask, 1.0, 0.0).astype(BF16)
    y1, y2, y3 = (jnp.dot(mb, p, preferred_element_type=F32) for p in _split3(x))
    return y1 + y2 + y3


def _cumsum_rows(row, d, gm):
    return _dot_mask_rhs(row, gm.incl[1 - d])


def _cumsum_cols(col, d, cm):
    return _dot_mask_lhs(cm[d], col)


def _store_cols(dst, col, n_chunks):
    for n in range(n_chunks):
        dst[n * CHUNK:(n + 1) * CHUNK, :] = jnp.broadcast_to(col[:, n:n + 1], (CHUNK, 128))


def _wide(x, width=None):
    reps = (GROUP if width is None else width) // x.shape[1]
    return x if reps == 1 else jnp.concatenate([x] * reps, axis=1)


def _rows(i, size):
    if isinstance(i, int):
        return pl.ds(i * size, size)
    return pl.ds(pl.multiple_of(i * size, size), size)


def _loop(n, body, init):
    if n == 1:
        return body(0, init)
    return lax.fori_loop(0, n, body, init)


def _conv_silu(x, w_ref):
    n = x.shape[0]
    k = w_ref.shape[0]
    acc = None
    for j in range(k):
        delta = j - k // 2
        term = (x if delta == 0 else _shift_tokens(x, delta, n)) * w_ref[pl.ds(j, 1), :]
        acc = term if acc is None else acc + term
    return jax.nn.silu(acc)


def _l2norm(x):
    return x * lax.rsqrt(jnp.sum(x * x, axis=-1, keepdims=True) + EPS)


def _blockdiag_tri_inverse(mats, gm):
    dot16 = lambda x, y: jnp.dot(x, y, preferred_element_type=F32).astype(BF16)
    base = (gm.r >> 3) == (gm.c >> 3)
    eye = jnp.where(gm.r == gm.c, 1.0, 0.0)
    zero16 = jnp.zeros((), BF16)
    a16s = [a.astype(BF16) for a in mats]
    invs = [eye - jnp.where(base, a, 0.0) for a in mats]
    ps = [jnp.where(base, -a16, zero16) for a16 in a16s]
    for _ in range(2):
        ps = [dot16(p, p) for p in ps]
        invs = [inv + jnp.dot(inv.astype(BF16), p, preferred_element_type=F32) for inv, p in zip(invs, ps)]
    for s in (3, 4, 5):
        join = ((gm.r >> s) ^ (gm.c >> s)) == 1
        inv16s = [inv.astype(BF16) for inv in invs]
        tmp = [dot16(jnp.where(join, a16, zero16), inv16) for a16, inv16 in zip(a16s, inv16s)]
        invs = [inv - jnp.dot(inv16, t, preferred_element_type=F32) for inv, inv16, t in zip(invs, inv16s, tmp)]
    return invs


def _gdn_kernel(*refs, seq, hp, zero_init, emit_state):
    alog_ref, dtb_ref, q_ref, k_ref, v_ref, cwq_ref, cwk_ref, cwv_ref, grow_ref, gcol_ref = refs[:10]
    pos = 10
    s0_ref = None
    if not zero_init:
        s0_ref = refs[pos]
        pos += 1
    o_ref = refs[pos]
    pos += 1
    sfin_ref = None
    if emit_state:
        sfin_ref = refs[pos]
        pos += 1
    (q_scr, k_scr, v_scr, gcr_scr, gcb_scr, bb_scr, gl_scr,
     u_scr, w_scr, qd_scr, kd_scr, qkd_scr) = refs[pos:]

    n_chunks = seq // CHUNK
    grp = GDN_GROUP
    cm = _chunk_masks()
    gm = _group_masks(grp)
    head_dirs = [(hh, d) for hh in range(hp) for d in range(N_DIR)]
    lanes = lambda hh: slice(hh * DN_DK, (hh + 1) * DN_DK)

    q_scr[...] = _conv_silu(q_ref[...].astype(F32), cwq_ref)
    k_scr[...] = _conv_silu(k_ref[...].astype(F32), cwk_ref)
    v_scr[...] = _conv_silu(v_ref[...].astype(F32), cwv_ref)
    for hh in range(hp):
        q_scr[:, lanes(hh)] = _l2norm(q_scr[:, lanes(hh)]) * (DN_DK ** -0.5)
        k_scr[:, lanes(hh)] = _l2norm(k_scr[:, lanes(hh)])

    for hh, d in head_dirs:
        head = pl.program_id(1) * hp + hh
        neg_a = -jnp.exp(jnp.full((1, 1), alog_ref[d, head], F32))
        dtb = dtb_ref[d, head]
        gcr_scr[hh, d] = _cumsum_rows(neg_a * _softplus(grow_ref[0, hh, d, 0] + dtb), d, gm)
        gc_col = _cumsum_cols(neg_a * _softplus(_col_gate(gcol_ref, hh, d, 0, n_chunks) + dtb), d, cm)
        _store_cols(gcb_scr.at[hh, d], gc_col, n_chunks)
        _store_cols(bb_scr.at[hh, d], jax.nn.sigmoid(_col_gate(gcol_ref, hh, d, 1, n_chunks)), n_chunks)
        last = CHUNK - 1 if d == 0 else 0
        _store_cols(gl_scr.at[hh, d], jnp.broadcast_to(gc_col[last:last + 1, :], gc_col.shape), n_chunks)

    gps = min(GDN_GROUPS_PER_STEP, seq // grp)

    def group_step(it, carry):
        gis = [it * gps + k for k in range(gps)]
        rows = [_rows(gi, grp) for gi in gis]
        for h0 in range(0, hp, GDN_HEADS_PER_PASS):
            heads = range(h0, min(h0 + GDN_HEADS_PER_PASS, hp))
            kqs = {}
            for k, r in enumerate(rows):
                for hh in heads:
                    kg, qg = k_scr[r, lanes(hh)], q_scr[r, lanes(hh)]
                    kqs[k, hh] = _dot_nt(jnp.concatenate([kg, qg], axis=0), kg)
            chains = [(k, hh, d) for k in range(len(gis)) for hh in heads for d in range(N_DIR)]
            mats = []
            for k, hh, d in chains:
                gcb = gcb_scr[hh, d, rows[k], :]
                gcr = gcr_scr[hh, d, pl.ds(gis[k], 1), :]
                decay = jnp.exp(jnp.where(gm.incl[d], _wide(gcb, grp) - gcr, -jnp.inf))
                mats.append(jnp.where(gm.strict[d],
                                      kqs[k, hh][:grp] * _wide(bb_scr[hh, d, rows[k], :], grp) * decay, 0.0))
                qkd = kqs[k, hh][grp:] * decay
                qkd_scr[hh, d, rows[k], :] = sum(qkd[:, n * CHUNK:(n + 1) * CHUNK] for n in range(grp // CHUNK))
            t_invs = _blockdiag_tri_inverse(mats, gm)
            for (k, hh, d), t_inv in zip(chains, t_invs):
                r = rows[k]
                qg, kg, vg = q_scr[r, lanes(hh)], k_scr[r, lanes(hh)], v_scr[r, lanes(hh)]
                gcb = gcb_scr[hh, d, r, :]
                bb = bb_scr[hh, d, r, :]
                e_gc = jnp.exp(gcb)
                uw = _dot(t_inv, jnp.concatenate([vg * bb, kg * bb * e_gc], axis=1))
                u_scr[hh, d, r, :] = uw[:, :DN_DV]
                w_scr[hh, d, r, :] = uw[:, DN_DV:]
                qd_scr[hh, d, r, :] = qg * e_gc
                kd_scr[hh, d, r, :] = kg * jnp.exp(gl_scr[hh, d, r, :] - gcb)
        return carry

    _loop(seq // (grp * gps), group_step, 0)

    o_ref[...] = jnp.zeros_like(o_ref)

    def chunk_step(n, carry):
        idxs = [n if d == 0 else n_chunks - 1 - n for _, d in head_dirs]
        rows = [_rows(idx, CHUNK) for idx in idxs]
        wqs = [_dot(jnp.concatenate([w_scr[hh, d, r, :], qd_scr[hh, d, r, :]], axis=0), s)
               for (hh, d), r, s in zip(head_dirs, rows, carry)]
        v_news = [u_scr[hh, d, r, :] - wq[:CHUNK] for (hh, d), r, wq in zip(head_dirs, rows, wqs)]
        o_ns = [wq[CHUNK:] + _dot(qkd_scr[hh, d, r, :], v_new)
                for (hh, d), r, wq, v_new in zip(head_dirs, rows, wqs, v_news)]
        new = []
        for (hh, d), idx, r, s, v_new in zip(head_dirs, idxs, rows, carry, v_news):
            s_decay = jnp.exp(gl_scr[hh, d, pl.ds(idx * CHUNK, 1), :])
            new.append(s * s_decay + _dot_tn(kd_scr[hh, d, r, :], v_new))
        for (hh, d), r, o_n in zip(head_dirs, rows, o_ns):
            o_ref[r, lanes(hh)] += o_n
        return tuple(new)

    if zero_init:
        init = (jnp.zeros((DN_DK, DN_DV), F32),) * len(head_dirs)
    else:
        init = tuple(s0_ref[0, d, hh] for hh, d in head_dirs)
    fin = lax.fori_loop(0, n_chunks, chunk_step, init)
    if emit_state:
        for i, (hh, d) in enumerate(head_dirs):
            sfin_ref[0, d, hh] = fin[i]


def _gdn_scan(main, gt, batch, seq, conv_w, a_log, dt_bias, s0, emit_state):
    n_tok = batch * seq
    nh = DN_HEADS
    hp = min(nh, max(2, GDN_STEP_POSITIONS // seq))
    n_hb = nh // hp
    assert seq % (GDN_GROUP * min(GDN_GROUPS_PER_STEP, seq // GDN_GROUP)) == 0
    grow, gcol = _gate_layouts(gt, batch, nh, hp, GDN_GROUP)
    zero_init = s0 is None
    smem = pl.BlockSpec(memory_space=pltpu.SMEM)
    qkv_spec = lambda off: pl.BlockSpec((seq, hp * DN_DK), lambda b, h: (b, off + h))
    cw_spec = lambda off: pl.BlockSpec((DN_CONV, hp * DN_DK), lambda b, h: (0, off + h))
    state_spec = pl.BlockSpec((1, N_DIR, hp, DN_DK, DN_DV), lambda b, h: (b, 0, h, 0, 0))
    in_specs = [smem, smem, qkv_spec(0), qkv_spec(n_hb), qkv_spec(2 * n_hb),
                cw_spec(0), cw_spec(n_hb), cw_spec(2 * n_hb),
                pl.BlockSpec((1, hp) + grow.shape[2:], lambda b, h: (b, h, 0, 0, 0, 0)),
                pl.BlockSpec((1, 1) + gcol.shape[2:], lambda b, h: (b, h, 0, 0))]
    args = [a_log, dt_bias, main, main, main, conv_w, conv_w, conv_w, grow, gcol]
    if not zero_init:
        in_specs.append(state_spec)
        args.append(s0)
    out_specs = [pl.BlockSpec((seq, hp * DN_DV), lambda b, h: (b, h))]
    out_shape = [jax.ShapeDtypeStruct((n_tok, nh * DN_DV), F32)]
    if emit_state:
        out_specs.append(state_spec)
        out_shape.append(jax.ShapeDtypeStruct((batch, N_DIR, nh, DN_DK, DN_DV), F32))
    per_dir = lambda width: pltpu.VMEM((hp, N_DIR, seq, width), F32)
    qkv_scr = pltpu.VMEM((seq, hp * DN_DK), F32)
    outs = pl.pallas_call(
        functools.partial(_gdn_kernel, seq=seq, hp=hp, zero_init=zero_init, emit_state=emit_state),
        grid=(batch, n_hb),
        in_specs=in_specs, out_specs=out_specs, out_shape=out_shape,
        scratch_shapes=[qkv_scr, qkv_scr, qkv_scr,
                        pltpu.VMEM((hp, N_DIR, 8, GDN_GROUP), F32),
                        per_dir(128), per_dir(128), per_dir(128),
                        per_dir(DN_DV), per_dir(DN_DK), per_dir(DN_DK), per_dir(DN_DK), per_dir(CHUNK)],
        compiler_params=_params("parallel", "parallel"),
        name="gdn_scan",
    )(*args)
    return outs[0], (outs[1] if emit_state else None)


ML_AUG = 2 * ML_DV


def _mlstm_kernel(*refs, seq, hp, zero_init, emit_state):
    bi_ref, bf_ref, q_ref, k_ref, v_ref, kt_ref, grow_ref, gcol_ref = refs[:8]
    pos = 8
    c0_ref = m0_ref = None
    if not zero_init:
        c0_ref, m0_ref = refs[pos:pos + 2]
        pos += 2
    o_ref = refs[pos]
    pos += 1
    cfin_ref = nfin_ref = mfin_ref = None
    if emit_state:
        cfin_ref, nfin_ref, mfin_ref = refs[pos:pos + 3]
        pos += 3
    bcr_scr, lir_scr, kwr_scr, bcb_scr, mi_scr, dec_scr, dc_scr, cin_scr = refs[pos:]

    n_chunks = seq // CHUNK
    n_groups = seq // GROUP
    head0 = pl.program_id(1) * hp
    cm = _chunk_masks()
    gm = _group_masks()
    ones_col = jnp.where(lax.broadcasted_iota(jnp.int32, (GROUP, ML_DV), 1) == 0, 1.0, 0.0)
    head_dirs = [(hh, d) for hh in range(hp) for d in range(N_DIR)]

    m_fin = {}
    row_group = lax.broadcasted_iota(jnp.int32, (8, GROUP), 0)
    row_chunk = lax.broadcasted_iota(jnp.int32, (8, GROUP), 1) >> LOG_CHUNK
    for hh, d in head_dirs:
        b_i = bi_ref[d, head0 + hh]
        b_f = bf_ref[d, head0 + hh]
        li_row = grow_ref[0, hh, d, 0] + b_i
        lf_row = -_softplus(-(grow_ref[0, hh, d, 1] + b_f))
        bc_row = _cumsum_rows(lf_row, d, gm)
        lir_scr[hh, d] = li_row
        bcr_scr[hh, d] = bc_row
        w_row = _dot_mask_rhs(lf_row, gm.same) - bc_row + li_row
        m_out_row = jnp.zeros((8, GROUP), F32)
        li_col = _col_gate(gcol_ref, hh, d, 0, n_chunks) + b_i
        bc_col = _cumsum_cols(-_softplus(-(_col_gate(gcol_ref, hh, d, 1, n_chunks) + b_f)), d, cm)
        last = CHUNK - 1 if d == 0 else 0
        b_last = bc_col[last:last + 1, :]
        w_col = b_last - bc_col + li_col
        w_max = jnp.max(w_col, axis=0, keepdims=True)
        m = jnp.zeros((1, 1), F32) if zero_init else m0_ref[0, d, hh][:, 0:1]
        for step in range(n_chunks):
            n = step if d == 0 else n_chunks - 1 - step
            sl = slice(n * CHUNK, (n + 1) * CHUNK)
            m_new = jnp.maximum(b_last[:, n:n + 1] + m, w_max[:, n:n + 1])
            bcb_scr[hh, d, sl, :] = jnp.broadcast_to(bc_col[:, n:n + 1], (CHUNK, 128))
            mi_scr[hh, d, sl, :] = jnp.broadcast_to(m, (CHUNK, 128))
            dec_scr[hh, d, n:n + 1, :] = jnp.broadcast_to(jnp.exp(b_last[:, n:n + 1] + m - m_new), (1, 128))
            in_chunk = (row_group == n // CHUNKS_PER_GROUP) & (row_chunk == n % CHUNKS_PER_GROUP)
            m_out_row = jnp.where(in_chunk, m_new, m_out_row)
            m = m_new
        m_fin[hh, d] = m
        kwr_scr[hh, d] = jnp.exp(w_row - m_out_row)

    def load_kv(rows, hh):
        kg = k_ref[rows, hh * ML_DQK:(hh + 1) * ML_DQK].astype(F32)
        v_aug = jnp.concatenate([v_ref[rows, hh * ML_DV:(hh + 1) * ML_DV].astype(F32), ones_col], axis=1)
        return kg, v_aug

    def delta_step(gi, carry):
        rows = _rows(gi, GROUP)
        v_augs = [load_kv(rows, hh)[1] for hh in range(hp)]
        kt = kt_ref[gi]
        lhs = []
        for hh, d in head_dirs:
            kw_t = kt[hh * ML_DQK:(hh + 1) * ML_DQK, :] * kwr_scr[hh, d, pl.ds(gi, 1), :]
            lhs.append(jnp.where(gm.same, jnp.concatenate([kw_t] * CHUNKS_PER_GROUP, axis=0), 0.0))
        for (hh, d), kw in zip(head_dirs, lhs):
            dc_scr[hh, d, rows, :] = _dot(kw, v_augs[hh])
        return carry

    _loop(n_groups, delta_step, 0)

    def prefix_step(n, carry):
        new = []
        for (hh, d), c_aug in zip(head_dirs, carry):
            idx = n if d == 0 else n_chunks - 1 - n
            rows = _rows(idx, CHUNK)
            cin_scr[hh, d, rows, :] = c_aug
            new.append(c_aug * _wide(dec_scr[hh, d, pl.ds(idx, 1), :], ML_AUG) + dc_scr[hh, d, rows, :])
        return tuple(new)

    if zero_init:
        init = (jnp.zeros((ML_DQK, ML_AUG), F32),) * len(head_dirs)
    else:
        init = tuple(c0_ref[0, d, hh] for hh, d in head_dirs)
    fin = lax.fori_loop(0, n_chunks, prefix_step, init)

    def out_step(gi, carry):
        rows = _rows(gi, GROUP)
        kvs = [load_kv(rows, hh) for hh in range(hp)]
        qgs = [q_ref[rows, hh * ML_DQK:(hh + 1) * ML_DQK].astype(F32) * (ML_DQK ** -0.5) for hh in range(hp)]
        qks = [_dot_nt(qg, kv[0]) for qg, kv in zip(qgs, kvs)]
        chunk = lambda n: slice(n * CHUNK, (n + 1) * CHUNK)
        inters = []
        for hh, d in head_dirs:
            c_in = cin_scr[hh, d, rows, :]
            inters.append(jnp.concatenate(
                [_dot(qgs[hh][chunk(n)], c_in[chunk(n)]) for n in range(CHUNKS_PER_GROUP)], axis=0))
        m_ts, b_ms, ss = [], [], []
        for hh, d in head_dirs:
            b_colb = bcb_scr[hh, d, rows, :]
            b_m = b_colb[:, 0:1] + mi_scr[hh, d, rows, :][:, 0:1]
            d_log = jnp.where(gm.incl[d], _wide(b_colb) - bcr_scr[hh, d, pl.ds(gi, 1), :]
                              + lir_scr[hh, d, pl.ds(gi, 1), :], -jnp.inf)
            m_t = jnp.maximum(b_m, jnp.max(d_log, axis=-1, keepdims=True))
            m_ts.append(m_t)
            b_ms.append(b_m)
            ss.append(qks[hh] * jnp.exp(d_log - m_t))
        intras = [_dot(s, kvs[hh][1]) for (hh, d), s in zip(head_dirs, ss)]
        hs = []
        for inter, intra, m_t, b_m in zip(inters, intras, m_ts, b_ms):
            num = jnp.exp(b_m - m_t) * inter + intra
            den = num[:, ML_DV:ML_DV + 1]
            hs.append(num[:, :ML_DV] / jnp.maximum(jnp.abs(den), jnp.exp(-m_t)))
        for hh in range(hp):
            o_ref[rows, hh * ML_DV:(hh + 1) * ML_DV] = hs[N_DIR * hh] + hs[N_DIR * hh + 1]
        return carry

    _loop(n_groups, out_step, 0)

    if emit_state:
        for i, (hh, d) in enumerate(head_dirs):
            cfin_ref[0, d, hh] = fin[i][:, :ML_DV]
            nfin_ref[0, d, hh] = fin[i][:, ML_DV:]
            mfin_ref[0, d, hh] = jnp.broadcast_to(m_fin[hh, d], (1, 128))


def _mlstm_scan(main, kt, gt, batch, seq, b_i, b_f, state0, emit_state):
    n_tok = batch * seq
    n_groups = seq // GROUP
    n_chunks = seq // CHUNK
    nh = ML_HEADS
    hp = min(nh, max(2, SCAN_STEP_POSITIONS // seq))
    n_hb = nh // hp
    grow, gcol = _gate_layouts(gt, batch, nh, hp)
    zero_init = state0 is None
    smem = pl.BlockSpec(memory_space=pltpu.SMEM)
    k_off = (nh * ML_DQK) // (hp * ML_DQK)
    v_off = (2 * nh * ML_DQK) // (hp * ML_DV)
    c_spec = pl.BlockSpec((1, N_DIR, hp, ML_DQK, ML_AUG), lambda b, p: (b, 0, p, 0, 0))
    m_spec = pl.BlockSpec((1, N_DIR, hp, 1, 128), lambda b, p: (b, 0, p, 0, 0))
    in_specs = [smem, smem,
                pl.BlockSpec((seq, hp * ML_DQK), lambda b, p: (b, p)),
                pl.BlockSpec((seq, hp * ML_DQK), lambda b, p: (b, k_off + p)),
                pl.BlockSpec((seq, hp * ML_DV), lambda b, p: (b, v_off + p)),
                pl.BlockSpec((n_groups, hp * ML_DQK, GROUP), lambda b, p: (b, p, 0)),
                pl.BlockSpec((1, hp) + grow.shape[2:], lambda b, p: (b, p, 0, 0, 0, 0)),
                pl.BlockSpec((1, 1) + gcol.shape[2:], lambda b, p: (b, p, 0, 0))]
    args = [b_i, b_f, main, main, main, kt, grow, gcol]
    if not zero_init:
        in_specs += [c_spec, m_spec]
        args += list(state0)
    out_specs = [pl.BlockSpec((seq, hp * ML_DV), lambda b, p: (b, p))]
    out_shape = [jax.ShapeDtypeStruct((n_tok, nh * ML_DV), F32)]
    if emit_state:
        half_spec = pl.BlockSpec((1, N_DIR, hp, ML_DQK, ML_DV), lambda b, p: (b, 0, p, 0, 0))
        out_specs += [half_spec, half_spec, m_spec]
        out_shape += [jax.ShapeDtypeStruct((batch, N_DIR, nh, ML_DQK, ML_DV), F32),
                      jax.ShapeDtypeStruct((batch, N_DIR, nh, ML_DQK, ML_AUG - ML_DV), F32),
                      jax.ShapeDtypeStruct((batch, N_DIR, nh, 1, 128), F32)]
    per_hd = lambda rows, width: pltpu.VMEM((hp, N_DIR, rows, width), F32)
    outs = pl.pallas_call(
        functools.partial(_mlstm_kernel, seq=seq, hp=hp, zero_init=zero_init, emit_state=emit_state),
        grid=(batch, n_hb),
        in_specs=in_specs, out_specs=out_specs, out_shape=out_shape,
        scratch_shapes=[per_hd(8, GROUP), per_hd(8, GROUP), per_hd(8, GROUP),
                        per_hd(seq, 128), per_hd(seq, 128),
                        per_hd(max(n_chunks, 8), 128),
                        per_hd(seq, ML_AUG), per_hd(seq, ML_AUG)],
        compiler_params=_params("parallel", "parallel"),
        name="mlstm_scan",
    )(*args)
    return outs[0], (tuple(outs[1:4]) if emit_state else None)


def _trunk(x, batch, seq, rows, mods, tokens_per_cond, st_d, st_ml, emit_state, p):
    depth = p["w_ada"].shape[0]
    tile_cond = lambda tile: (lambda i: (i * tile) // tokens_per_cond)
    new_d = new_ml = None
    for layer in range(depth):
        mod = mods[layer]
        kind, j = layer % 3, layer // 3
        if kind == 0:
            x = _fnet_mix(x, seq, j, p["norm_mix"][layer], mod, tokens_per_cond, p["fnet_w"], p["fnet_b"][j])
        elif kind == 1:
            n_main = DN_HEADS * (2 * DN_DK + 2 * DN_DV)
            main, gt = _norm_proj(x, p["norm_mix"][layer], mod, tokens_per_cond, p["dn_w_in"], j, n_main)
            s0 = None if st_d is None else st_d[:, j]
            o, sfin = _gdn_scan(main, gt, batch, seq, p["dn_conv_w"][j], p["dn_a_log"][j], p["dn_dt_bias"][j],
                                s0, emit_state)
            if emit_state:
                new_d = sfin
            x = _gated_out(x, o, main, 3, p["dn_norm"][j], mod, tile_cond(OUT_TOKEN_TILE), p["dn_w_out"][j],
                           DN_HEADS, jax.nn.silu)
        else:
            n_main = 2 * ML_HEADS * ML_DQK + 2 * ML_HEADS * ML_DV
            k_tile = (ML_HEADS * ML_DQK) // PROJ_COL_TILE
            assert ML_HEADS * ML_DQK == PROJ_COL_TILE
            main, gt, kt = _norm_proj(x, p["norm_mix"][layer], mod, tokens_per_cond, p["ml_w_in"], j, n_main,
                                      t_tile=k_tile)
            state0 = None
            if st_ml is not None:
                c0, n0, m0 = (s[:, j] for s in st_ml)
                pad = jnp.zeros(c0.shape[:-1] + (ML_AUG - ML_DV - 1,), F32)
                c_aug0 = jnp.concatenate([c0, n0[..., None], pad], axis=-1)
                m0b = jnp.broadcast_to(m0[..., None, None], m0.shape + (1, 128))
                state0 = (c_aug0, m0b)
            o, fin = _mlstm_scan(main, kt, gt, batch, seq, p["ml_b_i"][j], p["ml_b_f"][j], state0, emit_state)
            if emit_state:
                new_ml = (fin[0], fin[1][..., 0], fin[2][..., 0, 0])
            x = _gated_out(x, o, main, 2, p["ml_norm"][j], mod, tile_cond(OUT_TOKEN_TILE), p["ml_w_out"][j],
                           ML_HEADS, jax.nn.sigmoid)
        x = _conv_ffn(x, seq, rows, layer, p["norm_ffn"][layer], mod, tile_cond(FFN_TOKEN_TILE), p["ffn_w_up"],
                      p["ffn_conv_w"], p["ffn_conv_b"], p["ffn_w_down"], p["norm_final"],
                      final_norm=(layer == depth - 1))
    return x, new_d, new_ml


def kernel(x_prompt, x_sample, state_delta, state_mlstm_c, state_mlstm_n, state_mlstm_m, c, c_ctx, w_ada, b_ada, norm_mix, norm_ffn, norm_final, ffn_w_up, ffn_conv_w, ffn_conv_b, ffn_w_down, fnet_w, fnet_b, dn_w_in, dn_conv_w, dn_a_log, dn_dt_bias, dn_norm, dn_w_out, ml_w_in, ml_b_i, ml_b_f, ml_norm, ml_w_out):
    p = dict(w_ada=w_ada, norm_mix=norm_mix, norm_ffn=norm_ffn, norm_final=norm_final, ffn_w_up=ffn_w_up,
             ffn_conv_w=ffn_conv_w, ffn_conv_b=ffn_conv_b, ffn_w_down=ffn_w_down, fnet_w=fnet_w, fnet_b=fnet_b,
             dn_w_in=dn_w_in, dn_conv_w=dn_conv_w, dn_a_log=dn_a_log, dn_dt_bias=dn_dt_bias, dn_norm=dn_norm,
             dn_w_out=dn_w_out, ml_w_in=ml_w_in, ml_b_i=ml_b_i, ml_b_f=ml_b_f, ml_norm=ml_norm, ml_w_out=ml_w_out)
    b_ctx, t_ctx, _ = x_prompt.shape
    b_smp, t_smp, _ = x_sample.shape
    depth = w_ada.shape[0]

    cond8 = jnp.concatenate([c_ctx[None, :], c, jnp.zeros((8 - 1 - b_smp, D_MODEL), F32)], axis=0)
    mods = _adaln_all(cond8, w_ada, b_ada)
    mods_ctx = [mods[l, 0:1].reshape(1, 1, -1) for l in range(depth)]
    mods_smp = [mods[l, 1:1 + b_smp].reshape(b_smp, 1, -1) for l in range(depth)]

    y_ctx, new_d, new_ml = _trunk(x_prompt.reshape(b_ctx * t_ctx, D_MODEL), b_ctx, t_ctx, 1, mods_ctx,
                                  b_ctx * t_ctx, None, None, True, p)
    y_smp, _, _ = _trunk(x_sample.reshape(b_smp * t_smp, D_MODEL), b_smp, t_smp, t_smp // GRID_W, mods_smp,
                         t_smp, state_delta, (state_mlstm_c, state_mlstm_n, state_mlstm_m), False, p)

    new_c, new_n, new_m = new_ml
    return (y_ctx.reshape(b_ctx, t_ctx, D_MODEL), y_smp.reshape(b_smp, t_smp, D_MODEL),
            new_d[:, None], new_c[:, None], new_n[:, None], new_m[:, None])
```

```python
import collections
import functools

import numpy as np
import jax
import jax.numpy as jnp
from jax import lax
from jax.experimental import pallas as pl
from jax.experimental.pallas import tpu as pltpu

F32 = jnp.float32
BF16 = jnp.bfloat16

D_MODEL = 1024
EPS = 1e-6
N_DIR = 2
CHUNK = 64
LOG_CHUNK = 6
GROUP = 256
CHUNKS_PER_GROUP = GROUP // CHUNK
GDN_GROUP = 128
GDN_GROUPS_PER_STEP = 4
GDN_HEADS_PER_PASS = 4
GDN_STEP_POSITIONS = 2048
FNET_GROUP_DIM = 256
DN_HEADS, DN_DK, DN_DV, DN_CONV = 8, 128, 128, 5
ML_HEADS, ML_DQK, ML_DV = 8, 64, 128
D_FF = 2816
GRID_W = 64

FNET_STEP_TOKENS = 1024
FFN_TOKEN_TILE = 1024
FFN_FF_TILE = 256
FFN_TILES_PER_STEP = 3
FFN_SUB_TILES = 4
PROJ_TOKEN_TILE = 2048
PROJ_COL_TILE = 512
OUT_TOKEN_TILE = 512
ADA_COL_TILE = 1536
SCAN_STEP_POSITIONS = 1024
VMEM_LIMIT = 56 * 1024 * 1024


def _params(*sem):
    return pltpu.CompilerParams(dimension_semantics=sem, vmem_limit_bytes=VMEM_LIMIT)


def _dot(a, b):
    return jnp.dot(a.astype(BF16), b.astype(BF16), preferred_element_type=F32)


def _dot_nt(a, b):
    return lax.dot_general(a.astype(BF16), b.astype(BF16), (((1,), (1,)), ((), ())),
                           preferred_element_type=F32)


def _dot_tn(a, b):
    return lax.dot_general(a.astype(BF16), b.astype(BF16), (((0,), (0,)), ((), ())),
                           preferred_element_type=F32)


def _rms(x, g):
    return x * lax.rsqrt(jnp.mean(x * x, axis=-1, keepdims=True) + EPS) * g


def _norm_mod(x, g, sc, sh):
    return _rms(x, g) * (1.0 + sc) + sh


def _softplus(x):
    return jnp.maximum(x, 0.0) + jnp.log1p(jnp.exp(-jnp.abs(x)))


def _mod_spec(chunk, cond_of_tile):
    return pl.BlockSpec((1, 1, D_MODEL), lambda i, *_: (cond_of_tile(i), 0, chunk))


def _adaln_kernel(c_ref, w_ref, b_ref, o_ref):
    s = jax.nn.silu(c_ref[...])
    o_ref[0] = _dot(s, w_ref[0]) + b_ref[0]


def _adaln_all(cond8, w_ada, b_ada):
    depth = w_ada.shape[0]
    n_out = w_ada.shape[2]
    tn = ADA_COL_TILE
    return pl.pallas_call(
        _adaln_kernel,
        grid=(depth, n_out // tn),
        in_specs=[pl.BlockSpec((8, D_MODEL), lambda l, j: (0, 0)),
                  pl.BlockSpec((1, D_MODEL, tn), lambda l, j: (l, 0, j)),
                  pl.BlockSpec((1, 1, tn), lambda l, j: (l, 0, j))],
        out_specs=pl.BlockSpec((1, 8, tn), lambda l, j: (l, 0, j)),
        out_shape=jax.ShapeDtypeStruct((depth, 8, n_out), F32),
        compiler_params=_params("parallel", "parallel"),
        name="adaln",
    )(cond8, w_ada, b_ada.reshape(depth, 1, n_out))


def _norm_proj_kernel(x_ref, g_ref, sc_ref, sh_ref, w_ref, wg_ref, o_ref, gate_ref, *rest, t_tile):
    h_scr = rest[-1]

    @pl.when(pl.program_id(1) == 0)
    def _():
        h = _norm_mod(x_ref[...], g_ref[...], sc_ref[0], sh_ref[0]).astype(BF16)
        h_scr[...] = h
        gate_ref[...] = _dot_nt(wg_ref[...], h)

    y = _dot_nt(h_scr[...], w_ref[...])
    o_ref[...] = y.astype(o_ref.dtype)

    if t_tile is not None:
        t_ref = rest[0]

        @pl.when(pl.program_id(1) == t_tile)
        def _():
            for g in range(t_ref.shape[0]):
                t_ref[g] = y[g * GROUP:(g + 1) * GROUP, :].T


def _norm_proj(x, norm_g, mod, tokens_per_cond, w_stack, layer, n_main, t_tile=None):
    n_tok = x.shape[0]
    tm, tn = min(PROJ_TOKEN_TILE, tokens_per_cond), PROJ_COL_TILE
    cond_of_tile = lambda i: (i * tm) // tokens_per_cond
    w_t = jnp.swapaxes(w_stack, 1, 2)
    n_gate = w_t.shape[1] - n_main
    assert n_main % n_gate == 0 and n_gate % 8 == 0
    out_specs = [pl.BlockSpec((tm, tn), lambda i, j: (i, j)),
                 pl.BlockSpec((n_gate, tm), lambda i, j: (0, i))]
    out_shape = [jax.ShapeDtypeStruct((n_tok, n_main), BF16),
                 jax.ShapeDtypeStruct((n_gate, n_tok), F32)]
    if t_tile is not None:
        out_specs.append(pl.BlockSpec((tm // GROUP, tn, GROUP), lambda i, j: (i, 0, 0)))
        out_shape.append(jax.ShapeDtypeStruct((n_tok // GROUP, tn, GROUP), F32))
    return pl.pallas_call(
        functools.partial(_norm_proj_kernel, t_tile=t_tile),
        grid=(n_tok // tm, n_main // tn),
        in_specs=[pl.BlockSpec((tm, D_MODEL), lambda i, j: (i, 0)),
                  pl.BlockSpec((1, D_MODEL), lambda i, j: (0, 0)),
                  _mod_spec(1, cond_of_tile), _mod_spec(0, cond_of_tile),
                  pl.BlockSpec((None, tn, D_MODEL), lambda i, j: (layer, j, 0)),
                  pl.BlockSpec((None, n_gate, D_MODEL), lambda i, j: (layer, n_main // n_gate, 0))],
        out_specs=out_specs, out_shape=out_shape,
        scratch_shapes=[pltpu.VMEM((tm, D_MODEL), BF16)],
        compiler_params=_params("parallel", "arbitrary"),
        name="norm_proj",
    )(x, norm_g.reshape(1, D_MODEL), mod, mod, w_t, w_t)


def _gated_out_kernel(x_ref, o_ref, z_ref, ng_ref, g1_ref, w_ref, y_ref, wb_scr, *, n_heads, gate_fn):
    @pl.when(pl.program_id(0) == 0)
    def _():
        wb_scr[...] = w_ref[...].astype(BF16)

    dv = o_ref.shape[1] // n_heads
    parts = []
    for h in range(n_heads):
        sl = slice(h * dv, (h + 1) * dv)
        parts.append((_rms(o_ref[:, sl], ng_ref[...]) * gate_fn(z_ref[:, sl].astype(F32))).astype(BF16))
    hs = jnp.concatenate(parts, axis=1)
    y = jnp.dot(hs, wb_scr[...], preferred_element_type=F32)
    y_ref[...] = x_ref[...] + g1_ref[0] * y


def _gated_out(x, o, main, z_block, norm_g, mod, cond_of_tile, w_out, n_heads, gate_fn):
    n_tok = x.shape[0]
    tm = OUT_TOKEN_TILE
    width = o.shape[1]
    return pl.pallas_call(
        functools.partial(_gated_out_kernel, n_heads=n_heads, gate_fn=gate_fn),
        grid=(n_tok // tm,),
        in_specs=[pl.BlockSpec((tm, D_MODEL), lambda i: (i, 0)),
                  pl.BlockSpec((tm, width), lambda i: (i, 0)),
                  pl.BlockSpec((tm, width), lambda i: (i, z_block)),
                  pl.BlockSpec((1, width // n_heads), lambda i: (0, 0)),
                  _mod_spec(2, cond_of_tile),
                  pl.BlockSpec((width, D_MODEL), lambda i: (0, 0))],
        out_specs=pl.BlockSpec((tm, D_MODEL), lambda i: (i, 0)),
        out_shape=jax.ShapeDtypeStruct((n_tok, D_MODEL), F32),
        scratch_shapes=[pltpu.VMEM((width, D_MODEL), BF16)],
        compiler_params=_params("arbitrary"),
        name="gated_out",
    )(x, o, main, norm_g.reshape(1, -1), mod, w_out)


def _dft_mats(n):
    k = np.arange(n, dtype=np.int64)
    ang = 2.0 * np.pi * ((k[:, None] * k[None, :]) % n).astype(np.float64) / n
    s = 1.0 / np.sqrt(n)
    return np.cos(ang) * s, np.sin(ang) * s


def _fnet_kernel(x_ref, ng_ref, sc_ref, sh_ref, g1_ref, ct_ref, st_ref, cs_ref, w_ref, b_ref, y_ref, wb_scr):
    @pl.when(pl.program_id(0) == 0)
    def _():
        wb_scr[...] = w_ref[...].astype(BF16)

    seq = ct_ref.shape[0]
    seqs = [pl.ds(r, seq) for r in range(0, x_ref.shape[0], seq)]
    gd = FNET_GROUP_DIM
    hs = [_norm_mod(x_ref[s, :], ng_ref[...], sc_ref[0], sh_ref[0]).astype(BF16) for s in seqs]
    ps = [jnp.dot(ct_ref[...], h, preferred_element_type=F32) for h in hs]
    qs = [jnp.dot(st_ref[...], h, preferred_element_type=F32) for h in hs]
    fs = []
    for p, q in zip(ps, qs):
        parts = []
        for g in range(D_MODEL // gd):
            sl = slice(g * gd, (g + 1) * gd)
            pq = jnp.concatenate([p[:, sl], q[:, sl]], axis=1).astype(BF16)
            parts.append(jnp.dot(pq, cs_ref[...], preferred_element_type=F32).astype(BF16))
        fs.append(jnp.concatenate(parts, axis=1))
    ys = [jnp.dot(f, wb_scr[...], preferred_element_type=F32) + b_ref[...] for f in fs]
    for s, y in zip(seqs, ys):
        y_ref[s, :] = x_ref[s, :] + g1_ref[0] * y


def _fnet_mix(x, seq, layer, norm_g, mod, tokens_per_cond, w, b):
    n_tok = x.shape[0]
    tm = seq * max(1, min(FNET_STEP_TOKENS, tokens_per_cond) // seq)
    cond_of_tile = lambda i: (i * tm) // tokens_per_cond
    ct, st = _dft_mats(seq)
    cc, sc = _dft_mats(FNET_GROUP_DIM)
    cs = np.concatenate([cc, -sc], axis=0)
    const = lambda i: (0, 0)
    ct, st, cs = (jnp.asarray(m, F32).astype(BF16) for m in (ct, st, cs))
    return pl.pallas_call(
        _fnet_kernel,
        grid=(n_tok // tm,),
        in_specs=[pl.BlockSpec((tm, D_MODEL), lambda i: (i, 0)),
                  pl.BlockSpec((1, D_MODEL), const),
                  _mod_spec(1, cond_of_tile), _mod_spec(0, cond_of_tile), _mod_spec(2, cond_of_tile),
                  pl.BlockSpec((seq, seq), const),
                  pl.BlockSpec((seq, seq), const),
                  pl.BlockSpec((2 * FNET_GROUP_DIM, FNET_GROUP_DIM), const),
                  pl.BlockSpec((None, D_MODEL, D_MODEL), lambda i: (layer, 0, 0)),
                  pl.BlockSpec((1, D_MODEL), const)],
        out_specs=pl.BlockSpec((tm, D_MODEL), lambda i: (i, 0)),
        out_shape=jax.ShapeDtypeStruct((n_tok, D_MODEL), F32),
        scratch_shapes=[pltpu.VMEM((D_MODEL, D_MODEL), BF16)],
        compiler_params=_params("arbitrary"),
        name="fnet",
    )(x, norm_g.reshape(1, D_MODEL), mod, mod, mod,
      ct, st, cs, w, b.reshape(1, D_MODEL))


SUBLANES = 8


def _shift_tokens(x, delta, period):
    n, c = x.shape
    assert n % period == 0 and period % SUBLANES == 0 and 0 < abs(delta) < period
    blocks = range(0, n, period)
    if delta % SUBLANES == 0:
        zeros = jnp.zeros((abs(delta), c), x.dtype)
        if delta > 0:
            pieces = [piece for b in blocks for piece in (x[b + delta:b + period], zeros)]
        else:
            pieces = [piece for b in blocks for piece in (zeros, x[b:b + period + delta])]
        return jnp.concatenate(pieces, axis=0)
    assert abs(delta) < SUBLANES
    y = pltpu.roll(x, (-delta) % n, 0)
    r = lax.broadcasted_iota(jnp.int32, (SUBLANES, 1), 0) + delta
    keep = ((r >= 0) & (r <= SUBLANES - 1)).astype(x.dtype)
    if delta < 0:
        pieces = [piece for b in blocks for piece in (y[b:b + SUBLANES] * keep, y[b + SUBLANES:b + period])]
    else:
        pieces = [piece for b in blocks
                  for piece in (y[b:b + period - SUBLANES], y[b + period - SUBLANES:b + period] * keep)]
    return jnp.concatenate(pieces, axis=0)


def _dwconv_tokens(g, cw_ref, seq, rows):
    width = seq // rows
    g_cols = (_shift_tokens(g, -1, width), g, _shift_tokens(g, 1, width))
    out = None
    for di in (-1, 0, 1):
        if rows == 1 and di != 0:
            continue
        r = sum(g_cols[dj] * cw_ref[pl.ds(3 * (di + 1) + dj, 1), :] for dj in range(3))
        if di != 0:
            r = _shift_tokens(r, di * width, seq)
        out = r if out is None else out + r
    return out


def _ffn_kernel(x_ref, ng_ref, sc_ref, sh_ref, g2_ref, *rest, seq, rows, final_norm, n_ff_tiles):
    tiles = [rest[5 * t:5 * t + 5] for t in range(FFN_TILES_PER_STEP)]
    nf_ref, y_ref, h_scr, acc_scr = rest[5 * FFN_TILES_PER_STEP:]
    j = pl.program_id(1)
    last = pl.num_programs(1) - 1

    @pl.when(j == 0)
    def _():
        h_scr[...] = _norm_mod(x_ref[...], ng_ref[...], sc_ref[0], sh_ref[0]).astype(BF16)
        acc_scr[...] = jnp.zeros_like(acc_scr)

    tm = h_scr.shape[0]
    sub = tm // FFN_SUB_TILES
    parts = [pl.ds(r, sub) for r in range(0, tm, sub)]
    per_seq = max(seq // sub, 1)

    def run(tiles):
        gs = [[jnp.dot(h_scr[p, :], wg[...].astype(BF16), preferred_element_type=F32) for p in parts]
              for _, wg, _, _, _ in tiles]
        as_ = [[jnp.dot(h_scr[p, :], wa[...].astype(BF16), preferred_element_type=F32) for p in parts]
               for wa, _, _, _, _ in tiles]
        acts = []
        for (_, _, cw, cb, _), g_t, a_t in zip(tiles, gs, as_):
            convs = []
            for i in range(0, len(parts), per_seq):
                g = g_t[i] if per_seq == 1 else jnp.concatenate(g_t[i:i + per_seq], axis=0)
                c = _dwconv_tokens(g, cw, seq, rows) + cb[...]
                convs += [c[k * sub:(k + 1) * sub] for k in range(per_seq)]
            acts.append([(jax.nn.silu(c) * a).astype(BF16) for c, a in zip(convs, a_t)])
        wd = jnp.concatenate([t[4][...].astype(BF16) for t in tiles], axis=0)
        for k, p in enumerate(parts):
            act = jnp.concatenate([a_t[k] for a_t in acts], axis=1)
            acc_scr[p, :] += jnp.dot(act, wd, preferred_element_type=F32)

    n_tail = n_ff_tiles % FFN_TILES_PER_STEP
    if n_tail:
        pl.when(j < last)(lambda: run(tiles))
        pl.when(j == last)(lambda: run(tiles[:n_tail]))
    else:
        run(tiles)

    @pl.when(j == last)
    def _():
        y = x_ref[...] + g2_ref[0] * acc_scr[...]
        if final_norm:
            y = _rms(y, nf_ref[...])
        y_ref[...] = y


def _conv_ffn(x, seq, rows, layer, norm_g, mod, cond_of_tile, w_up, conv_w, conv_b, w_down, norm_final, final_norm):
    n_tok = x.shape[0]
    tm, tf = FFN_TOKEN_TILE, FFN_FF_TILE
    n_ff_tiles = D_FF // tf
    depth = w_up.shape[0]
    tps = FFN_TILES_PER_STEP
    conv_w, conv_b = conv_w.reshape(depth, 9, D_FF), conv_b.reshape(depth, 1, D_FF)
    tile_specs, tile_args = [], []
    for t in range(tps):
        tile = lambda j, t=t: jnp.minimum(j * tps + t, n_ff_tiles - 1)
        tile_specs += [pl.BlockSpec((None, D_MODEL, tf), lambda i, j, tile=tile: (layer, 0, tile(j))),
                       pl.BlockSpec((None, D_MODEL, tf), lambda i, j, tile=tile: (layer, 0, n_ff_tiles + tile(j))),
                       pl.BlockSpec((None, 9, tf), lambda i, j, tile=tile: (layer, 0, tile(j))),
                       pl.BlockSpec((None, 1, tf), lambda i, j, tile=tile: (layer, 0, tile(j))),
                       pl.BlockSpec((None, tf, D_MODEL), lambda i, j, tile=tile: (layer, tile(j), 0))]
        tile_args += [w_up, w_up, conv_w, conv_b, w_down]
    return pl.pallas_call(
        functools.partial(_ffn_kernel, seq=seq, rows=rows, final_norm=final_norm, n_ff_tiles=n_ff_tiles),
        grid=(n_tok // tm, pl.cdiv(n_ff_tiles, tps)),
        in_specs=[pl.BlockSpec((tm, D_MODEL), lambda i, j: (i, 0)),
                  pl.BlockSpec((1, D_MODEL), lambda i, j: (0, 0)),
                  _mod_spec(4, cond_of_tile), _mod_spec(3, cond_of_tile), _mod_spec(5, cond_of_tile),
                  *tile_specs,
                  pl.BlockSpec((1, D_MODEL), lambda i, j: (0, 0))],
        out_specs=pl.BlockSpec((tm, D_MODEL), lambda i, j: (i, 0)),
        out_shape=jax.ShapeDtypeStruct((n_tok, D_MODEL), F32),
        scratch_shapes=[pltpu.VMEM((tm, D_MODEL), BF16), pltpu.VMEM((tm, D_MODEL), F32)],
        compiler_params=_params("parallel", "arbitrary"),
        name="conv_ffn",
    )(x, norm_g.reshape(1, D_MODEL), mod, mod, mod, *tile_args, norm_final.reshape(1, D_MODEL))


GroupMasks = collections.namedtuple("GroupMasks", "r c same incl strict")


def _chunk_masks():
    r = lax.broadcasted_iota(jnp.int32, (CHUNK, CHUNK), 0)
    c = lax.broadcasted_iota(jnp.int32, (CHUNK, CHUNK), 1)
    return (r >= c, r <= c)


def _group_masks(group=GROUP):
    r = lax.broadcasted_iota(jnp.int32, (group, group), 0)
    c = lax.broadcasted_iota(jnp.int32, (group, group), 1)
    same = (r >> LOG_CHUNK) == (c >> LOG_CHUNK)
    return GroupMasks(r, c, same, (same & (r >= c), same & (r <= c)), (same & (r > c), same & (r < c)))


def _gate_layouts(gt, batch, n_heads, hp, group=GROUP):
    seq = gt.shape[1] // batch
    n_groups, n_chunks = seq // group, seq // CHUNK
    assert seq % group == 0 and n_groups <= 8
    g = jnp.transpose(gt.reshape(N_DIR, 2, n_heads, batch, seq), (3, 2, 0, 1, 4))
    rows = g.reshape(g.shape[:4] + (n_groups, group))
    rows = jnp.pad(rows, ((0, 0),) * 4 + ((0, 8 - n_groups), (0, 0)))
    cols = jnp.swapaxes(g.reshape(batch, n_heads // hp, hp * N_DIR * 2 * n_chunks, CHUNK), -1, -2)
    return rows, cols


def _col_gate(gcol_ref, hh, d, kind, n_chunks):
    start = ((hh * N_DIR + d) * 2 + kind) * n_chunks
    return gcol_ref[0, 0, :, start:start + n_chunks]


def _split3(x):
    x1 = x.astype(BF16)
    r1 = x - x1.astype(F32)
    x2 = r1.astype(BF16)
    return x1, x2, (r1 - x2.astype(F32)).astype(BF16)


def _dot_mask_rhs(x, mask):
    m = x.shape[0]
    y = jnp.dot(jnp.concatenate(_split3(x), axis=0), jnp.where(mask, 1.0, 0.0).astype(BF16),
                preferred_element_type=F32)
    return y[:m] + y[m:2 * m] + y[2 * m:]


def _dot_mask_lhs(mask, x):
    mb = jnp.where(mask, 1.0, 0.0).astype(BF16)
    y1, y2, y3 = (jnp.dot(mb, p, preferred_element_type=F32) for p in _split3(x))
    return y1 + y2 + y3


def _cumsum_rows(row, d, gm):
    return _dot_mask_rhs(row, gm.incl[1 - d])


def _cumsum_cols(col, d, cm):
    return _dot_mask_lhs(cm[d], col)


def _store_cols(dst, col, n_chunks):
    for n in range(n_chunks):
        dst[n * CHUNK:(n + 1) * CHUNK, :] = jnp.broadcast_to(col[:, n:n + 1], (CHUNK, 128))


def _wide(x, width=None):
    reps = (GROUP if width is None else width) // x.shape[1]
    return x if reps == 1 else jnp.concatenate([x] * reps, axis=1)


def _rows(i, size):
    if isinstance(i, int):
        return pl.ds(i * size, size)
    return pl.ds(pl.multiple_of(i * size, size), size)


def _loop(n, body, init):
    if n == 1:
        return body(0, init)
    return lax.fori_loop(0, n, body, init)


def _conv_silu(x, w_ref):
    n = x.shape[0]
    k = w_ref.shape[0]
    acc = None
    for j in range(k):
        delta = j - k // 2
        term = (x if delta == 0 else _shift_tokens(x, delta, n)) * w_ref[pl.ds(j, 1), :]
        acc = term if acc is None else acc + term
    return jax.nn.silu(acc)


def _l2norm(x):
    return x * lax.rsqrt(jnp.sum(x * x, axis=-1, keepdims=True) + EPS)


def _blockdiag_tri_inverse(mats, gm):
    dot16 = lambda x, y: jnp.dot(x, y, preferred_element_type=F32).astype(BF16)
    base = (gm.r >> 3) == (gm.c >> 3)
    eye = jnp.where(gm.r == gm.c, 1.0, 0.0)
    zero16 = jnp.zeros((), BF16)
    a16s = [a.astype(BF16) for a in mats]
    invs = [eye - jnp.where(base, a, 0.0) for a in mats]
    ps = [jnp.where(base, -a16, zero16) for a16 in a16s]
    for _ in range(2):
        ps = [dot16(p, p) for p in ps]
        invs = [inv + jnp.dot(inv.astype(BF16), p, preferred_element_type=F32) for inv, p in zip(invs, ps)]
    for s in (3, 4, 5):
        join = ((gm.r >> s) ^ (gm.c >> s)) == 1
        inv16s = [inv.astype(BF16) for inv in invs]
        tmp = [dot16(jnp.where(join, a16, zero16), inv16) for a16, inv16 in zip(a16s, inv16s)]
        invs = [inv - jnp.dot(inv16, t, preferred_element_type=F32) for inv, inv16, t in zip(invs, inv16s, tmp)]
    return invs


def _gdn_kernel(*refs, seq, hp, zero_init, emit_state):
    alog_ref, dtb_ref, q_ref, k_ref, v_ref, cwq_ref, cwk_ref, cwv_ref, grow_ref, gcol_ref = refs[:10]
    pos = 10
    s0_ref = None
    if not zero_init:
        s0_ref = refs[pos]
        pos += 1
    o_ref = refs[pos]
    pos += 1
    sfin_ref = None
    if emit_state:
        sfin_ref = refs[pos]
        pos += 1
    (q_scr, k_scr, v_scr, gcr_scr, gcb_scr, bb_scr, gl_scr,
     u_scr, w_scr, qd_scr, kd_scr, qkd_scr) = refs[pos:]

    n_chunks = seq // CHUNK
    grp = GDN_GROUP
    cm = _chunk_masks()
    gm = _group_masks(grp)
    head_dirs = [(hh, d) for hh in range(hp) for d in range(N_DIR)]
    lanes = lambda hh: slice(hh * DN_DK, (hh + 1) * DN_DK)

    q_scr[...] = _conv_silu(q_ref[...].astype(F32), cwq_ref)
    k_scr[...] = _conv_silu(k_ref[...].astype(F32), cwk_ref)
    v_scr[...] = _conv_silu(v_ref[...].astype(F32), cwv_ref)
    for hh in range(hp):
        q_scr[:, lanes(hh)] = _l2norm(q_scr[:, lanes(hh)]) * (DN_DK ** -0.5)
        k_scr[:, lanes(hh)] = _l2norm(k_scr[:, lanes(hh)])

    for hh, d in head_dirs:
        head = pl.program_id(1) * hp + hh
        neg_a = -jnp.exp(jnp.full((1, 1), alog_ref[d, head], F32))
        dtb = dtb_ref[d, head]
        gcr_scr[hh, d] = _cumsum_rows(neg_a * _softplus(grow_ref[0, hh, d, 0] + dtb), d, gm)
        gc_col = _cumsum_cols(neg_a * _softplus(_col_gate(gcol_ref, hh, d, 0, n_chunks) + dtb), d, cm)
        _store_cols(gcb_scr.at[hh, d], gc_col, n_chunks)
        _store_cols(bb_scr.at[hh, d], jax.nn.sigmoid(_col_gate(gcol_ref, hh, d, 1, n_chunks)), n_chunks)
        last = CHUNK - 1 if d == 0 else 0
        _store_cols(gl_scr.at[hh, d], jnp.broadcast_to(gc_col[last:last + 1, :], gc_col.shape), n_chunks)

    gps = min(GDN_GROUPS_PER_STEP, seq // grp)

    def group_step(it, carry):
        gis = [it * gps + k for k in range(gps)]
        rows = [_rows(gi, grp) for gi in gis]
        for h0 in range(0, hp, GDN_HEADS_PER_PASS):
            heads = range(h0, min(h0 + GDN_HEADS_PER_PASS, hp))
            kqs = {}
            for k, r in enumerate(rows):
                for hh in heads:
                    kg, qg = k_scr[r, lanes(hh)], q_scr[r, lanes(hh)]
                    kqs[k, hh] = _dot_nt(jnp.concatenate([kg, qg], axis=0), kg)
            chains = [(k, hh, d) for k in range(len(gis)) for hh in heads for d in range(N_DIR)]
            mats = []
            for k, hh, d in chains:
                gcb = gcb_scr[hh, d, rows[k], :]
                gcr = gcr_scr[hh, d, pl.ds(gis[k], 1), :]
                decay = jnp.exp(jnp.where(gm.incl[d], _wide(gcb, grp) - gcr, -jnp.inf))
                mats.append(jnp.where(gm.strict[d],
                                      kqs[k, hh][:grp] * _wide(bb_scr[hh, d, rows[k], :], grp) * decay, 0.0))
                qkd = kqs[k, hh][grp:] * decay
                qkd_scr[hh, d, rows[k], :] = sum(qkd[:, n * CHUNK:(n + 1) * CHUNK] for n in range(grp // CHUNK))
            t_invs = _blockdiag_tri_inverse(mats, gm)
            for (k, hh, d), t_inv in zip(chains, t_invs):
                r = rows[k]
                qg, kg, vg = q_scr[r, lanes(hh)], k_scr[r, lanes(hh)], v_scr[r, lanes(hh)]
                gcb = gcb_scr[hh, d, r, :]
                bb = bb_scr[hh, d, r, :]
                e_gc = jnp.exp(gcb)
                uw = _dot(t_inv, jnp.concatenate([vg * bb, kg * bb * e_gc], axis=1))
                u_scr[hh, d, r, :] = uw[:, :DN_DV]
                w_scr[hh, d, r, :] = uw[:, DN_DV:]
                qd_scr[hh, d, r, :] = qg * e_gc
                kd_scr[hh, d, r, :] = kg * jnp.exp(gl_scr[hh, d, r, :] - gcb)
        return carry

    _loop(seq // (grp * gps), group_step, 0)

    o_ref[...] = jnp.zeros_like(o_ref)

    def chunk_step(n, carry):
        idxs = [n if d == 0 else n_chunks - 1 - n for _, d in head_dirs]
        rows = [_rows(idx, CHUNK) for idx in idxs]
        wqs = [_dot(jnp.concatenate([w_scr[hh, d, r, :], qd_scr[hh, d, r, :]], axis=0), s)
               for (hh, d), r, s in zip(head_dirs, rows, carry)]
        v_news = [u_scr[hh, d, r, :] - wq[:CHUNK] for (hh, d), r, wq in zip(head_dirs, rows, wqs)]
        o_ns = [wq[CHUNK:] + _dot(qkd_scr[hh, d, r, :], v_new)
                for (hh, d), r, wq, v_new in zip(head_dirs, rows, wqs, v_news)]
        new = []
        for (hh, d), idx, r, s, v_new in zip(head_dirs, idxs, rows, carry, v_news):
            s_decay = jnp.exp(gl_scr[hh, d, pl.ds(idx * CHUNK, 1), :])
            new.append(s * s_decay + _dot_tn(kd_scr[hh, d, r, :], v_new))
        for (hh, d), r, o_n in zip(head_dirs, rows, o_ns):
            o_ref[r, lanes(hh)] += o_n
        return tuple(new)

    if zero_init:
        init = (jnp.zeros((DN_DK, DN_DV), F32),) * len(head_dirs)
    else:
        init = tuple(s0_ref[0, d, hh] for hh, d in head_dirs)
    fin = lax.fori_loop(0, n_chunks, chunk_step, init)
    if emit_state:
        for i, (hh, d) in enumerate(head_dirs):
            sfin_ref[0, d, hh] = fin[i]


def _gdn_scan(main, gt, batch, seq, conv_w, a_log, dt_bias, s0, emit_state):
    n_tok = batch * seq
    nh = DN_HEADS
    hp = min(nh, max(2, GDN_STEP_POSITIONS // seq))
    n_hb = nh // hp
    assert seq % (GDN_GROUP * min(GDN_GROUPS_PER_STEP, seq // GDN_GROUP)) == 0
    grow, gcol = _gate_layouts(gt, batch, nh, hp, GDN_GROUP)
    zero_init = s0 is None
    smem = pl.BlockSpec(memory_space=pltpu.SMEM)
    qkv_spec = lambda off: pl.BlockSpec((seq, hp * DN_DK), lambda b, h: (b, off + h))
    cw_spec = lambda off: pl.BlockSpec((DN_CONV, hp * DN_DK), lambda b, h: (0, off + h))
    state_spec = pl.BlockSpec((1, N_DIR, hp, DN_DK, DN_DV), lambda b, h: (b, 0, h, 0, 0))
    in_specs = [smem, smem, qkv_spec(0), qkv_spec(n_hb), qkv_spec(2 * n_hb),
                cw_spec(0), cw_spec(n_hb), cw_spec(2 * n_hb),
                pl.BlockSpec((1, hp) + grow.shape[2:], lambda b, h: (b, h, 0, 0, 0, 0)),
                pl.BlockSpec((1, 1) + gcol.shape[2:], lambda b, h: (b, h, 0, 0))]
    args = [a_log, dt_bias, main, main, main, conv_w, conv_w, conv_w, grow, gcol]
    if not zero_init:
        in_specs.append(state_spec)
        args.append(s0)
    out_specs = [pl.BlockSpec((seq, hp * DN_DV), lambda b, h: (b, h))]
    out_shape = [jax.ShapeDtypeStruct((n_tok, nh * DN_DV), F32)]
    if emit_state:
        out_specs.append(state_spec)
        out_shape.append(jax.ShapeDtypeStruct((batch, N_DIR, nh, DN_DK, DN_DV), F32))
    per_dir = lambda width: pltpu.VMEM((hp, N_DIR, seq, width), F32)
    qkv_scr = pltpu.VMEM((seq, hp * DN_DK), F32)
    outs = pl.pallas_call(
        functools.partial(_gdn_kernel, seq=seq, hp=hp, zero_init=zero_init, emit_state=emit_state),
        grid=(batch, n_hb),
        in_specs=in_specs, out_specs=out_specs, out_shape=out_shape,
        scratch_shapes=[qkv_scr, qkv_scr, qkv_scr,
                        pltpu.VMEM((hp, N_DIR, 8, GDN_GROUP), F32),
                        per_dir(128), per_dir(128), per_dir(128),
                        per_dir(DN_DV), per_dir(DN_DK), per_dir(DN_DK), per_dir(DN_DK), per_dir(CHUNK)],
        compiler_params=_params("parallel", "parallel"),
        name="gdn_scan",
    )(*args)
    return outs[0], (outs[1] if emit_state else None)


ML_AUG = 2 * ML_DV


def _mlstm_kernel(*refs, seq, hp, zero_init, emit_state):
    bi_ref, bf_ref, q_ref, k_ref, v_ref, kt_ref, grow_ref, gcol_ref = refs[:8]
    pos = 8
    c0_ref = m0_ref = None
    if not zero_init:
        c0_ref, m0_ref = refs[pos:pos + 2]
        pos += 2
    o_ref = refs[pos]
    pos += 1
    cfin_ref = nfin_ref = mfin_ref = None
    if emit_state:
        cfin_ref, nfin_ref, mfin_ref = refs[pos:pos + 3]
        pos += 3
    bcr_scr, lir_scr, kwr_scr, bcb_scr, mi_scr, dec_scr, dc_scr, cin_scr = refs[pos:]

    n_chunks = seq // CHUNK
    n_groups = seq // GROUP
    head0 = pl.program_id(1) * hp
    cm = _chunk_masks()
    gm = _group_masks()
    ones_col = jnp.where(lax.broadcasted_iota(jnp.int32, (GROUP, ML_DV), 1) == 0, 1.0, 0.0)
    head_dirs = [(hh, d) for hh in range(hp) for d in range(N_DIR)]

    m_fin = {}
    row_group = lax.broadcasted_iota(jnp.int32, (8, GROUP), 0)
    row_chunk = lax.broadcasted_iota(jnp.int32, (8, GROUP), 1) >> LOG_CHUNK
    for hh, d in head_dirs:
        b_i = bi_ref[d, head0 + hh]
        b_f = bf_ref[d, head0 + hh]
        li_row = grow_ref[0, hh, d, 0] + b_i
        lf_row = -_softplus(-(grow_ref[0, hh, d, 1] + b_f))
        bc_row = _cumsum_rows(lf_row, d, gm)
        lir_scr[hh, d] = li_row
        bcr_scr[hh, d] = bc_row
        w_row = _dot_mask_rhs(lf_row, gm.same) - bc_row + li_row
        m_out_row = jnp.zeros((8, GROUP), F32)
        li_col = _col_gate(gcol_ref, hh, d, 0, n_chunks) + b_i
        bc_col = _cumsum_cols(-_softplus(-(_col_gate(gcol_ref, hh, d, 1, n_chunks) + b_f)), d, cm)
        last = CHUNK - 1 if d == 0 else 0
        b_last = bc_col[last:last + 1, :]
        w_col = b_last - bc_col + li_col
        w_max = jnp.max(w_col, axis=0, keepdims=True)
        m = jnp.zeros((1, 1), F32) if zero_init else m0_ref[0, d, hh][:, 0:1]
        for step in range(n_chunks):
            n = step if d == 0 else n_chunks - 1 - step
            sl = slice(n * CHUNK, (n + 1) * CHUNK)
            m_new = jnp.maximum(b_last[:, n:n + 1] + m, w_max[:, n:n + 1])
            bcb_scr[hh, d, sl, :] = jnp.broadcast_to(bc_col[:, n:n + 1], (CHUNK, 128))
            mi_scr[hh, d, sl, :] = jnp.broadcast_to(m, (CHUNK, 128))
            dec_scr[hh, d, n:n + 1, :] = jnp.broadcast_to(jnp.exp(b_last[:, n:n + 1] + m - m_new), (1, 128))
            in_chunk = (row_group == n // CHUNKS_PER_GROUP) & (row_chunk == n % CHUNKS_PER_GROUP)
            m_out_row = jnp.where(in_chunk, m_new, m_out_row)
            m = m_new
        m_fin[hh, d] = m
        kwr_scr[hh, d] = jnp.exp(w_row - m_out_row)

    def load_kv(rows, hh):
        kg = k_ref[rows, hh * ML_DQK:(hh + 1) * ML_DQK].astype(F32)
        v_aug = jnp.concatenate([v_ref[rows, hh * ML_DV:(hh + 1) * ML_DV].astype(F32), ones_col], axis=1)
        return kg, v_aug

    def delta_step(gi, carry):
        rows = _rows(gi, GROUP)
        v_augs = [load_kv(rows, hh)[1] for hh in range(hp)]
        kt = kt_ref[gi]
        lhs = []
        for hh, d in head_dirs:
            kw_t = kt[hh * ML_DQK:(hh + 1) * ML_DQK, :] * kwr_scr[hh, d, pl.ds(gi, 1), :]
            lhs.append(jnp.where(gm.same, jnp.concatenate([kw_t] * CHUNKS_PER_GROUP, axis=0), 0.0))
        for (hh, d), kw in zip(head_dirs, lhs):
            dc_scr[hh, d, rows, :] = _dot(kw, v_augs[hh])
        return carry

    _loop(n_groups, delta_step, 0)

    def prefix_step(n, carry):
        new = []
        for (hh, d), c_aug in zip(head_dirs, carry):
            idx = n if d == 0 else n_chunks - 1 - n
            rows = _rows(idx, CHUNK)
            cin_scr[hh, d, rows, :] = c_aug
            new.append(c_aug * _wide(dec_scr[hh, d, pl.ds(idx, 1), :], ML_AUG) + dc_scr[hh, d, rows, :])
        return tuple(new)

    if zero_init:
        init = (jnp.zeros((ML_DQK, ML_AUG), F32),) * len(head_dirs)
    else:
        init = tuple(c0_ref[0, d, hh] for hh, d in head_dirs)
    fin = lax.fori_loop(0, n_chunks, prefix_step, init)

    def out_step(gi, carry):
        rows = _rows(gi, GROUP)
        kvs = [load_kv(rows, hh) for hh in range(hp)]
        qgs = [q_ref[rows, hh * ML_DQK:(hh + 1) * ML_DQK].astype(F32) * (ML_DQK ** -0.5) for hh in range(hp)]
        qks = [_dot_nt(qg, kv[0]) for qg, kv in zip(qgs, kvs)]
        chunk = lambda n: slice(n * CHUNK, (n + 1) * CHUNK)
        inters = []
        for hh, d in head_dirs:
            c_in = cin_scr[hh, d, rows, :]
            inters.append(jnp.concatenate(
                [_dot(qgs[hh][chunk(n)], c_in[chunk(n)]) for n in range(CHUNKS_PER_GROUP)], axis=0))
        m_ts, b_ms, ss = [], [], []
        for hh, d in head_dirs:
            b_colb = bcb_scr[hh, d, rows, :]
            b_m = b_colb[:, 0:1] + mi_scr[hh, d, rows, :][:, 0:1]
            d_log = jnp.where(gm.incl[d], _wide(b_colb) - bcr_scr[hh, d, pl.ds(gi, 1), :]
                              + lir_scr[hh, d, pl.ds(gi, 1), :], -jnp.inf)
            m_t = jnp.maximum(b_m, jnp.max(d_log, axis=-1, keepdims=True))
            m_ts.append(m_t)
            b_ms.append(b_m)
            ss.append(qks[hh] * jnp.exp(d_log - m_t))
        intras = [_dot(s, kvs[hh][1]) for (hh, d), s in zip(head_dirs, ss)]
        hs = []
        for inter, intra, m_t, b_m in zip(inters, intras, m_ts, b_ms):
            num = jnp.exp(b_m - m_t) * inter + intra
            den = num[:, ML_DV:ML_DV + 1]
            hs.append(num[:, :ML_DV] / jnp.maximum(jnp.abs(den), jnp.exp(-m_t)))
        for hh in range(hp):
            o_ref[rows, hh * ML_DV:(hh + 1) * ML_DV] = hs[N_DIR * hh] + hs[N_DIR * hh + 1]
        return carry

    _loop(n_groups, out_step, 0)

    if emit_state:
        for i, (hh, d) in enumerate(head_dirs):
            cfin_ref[0, d, hh] = fin[i][:, :ML_DV]
            nfin_ref[0, d, hh] = fin[i][:, ML_DV:]
            mfin_ref[0, d, hh] = jnp.broadcast_to(m_fin[hh, d], (1, 128))


def _mlstm_scan(main, kt, gt, batch, seq, b_i, b_f, state0, emit_state):
    n_tok = batch * seq
    n_groups = seq // GROUP
    n_chunks = seq // CHUNK
    nh = ML_HEADS
    hp = min(nh, max(2, SCAN_STEP_POSITIONS // seq))
    n_hb = nh // hp
    grow, gcol = _gate_layouts(gt, batch, nh, hp)
    zero_init = state0 is None
    smem = pl.BlockSpec(memory_space=pltpu.SMEM)
    k_off = (nh * ML_DQK) // (hp * ML_DQK)
    v_off = (2 * nh * ML_DQK) // (hp * ML_DV)
    c_spec = pl.BlockSpec((1, N_DIR, hp, ML_DQK, ML_AUG), lambda b, p: (b, 0, p, 0, 0))
    m_spec = pl.BlockSpec((1, N_DIR, hp, 1, 128), lambda b, p: (b, 0, p, 0, 0))
    in_specs = [smem, smem,
                pl.BlockSpec((seq, hp * ML_DQK), lambda b, p: (b, p)),
                pl.BlockSpec((seq, hp * ML_DQK), lambda b, p: (b, k_off + p)),
                pl.BlockSpec((seq, hp * ML_DV), lambda b, p: (b, v_off + p)),
                pl.BlockSpec((n_groups, hp * ML_DQK, GROUP), lambda b, p: (b, p, 0)),
                pl.BlockSpec((1, hp) + grow.shape[2:], lambda b, p: (b, p, 0, 0, 0, 0)),
                pl.BlockSpec((1, 1) + gcol.shape[2:], lambda b, p: (b, p, 0, 0))]
    args = [b_i, b_f, main, main, main, kt, grow, gcol]
    if not zero_init:
        in_specs += [c_spec, m_spec]
        args += list(state0)
    out_specs = [pl.BlockSpec((seq, hp * ML_DV), lambda b, p: (b, p))]
    out_shape = [jax.ShapeDtypeStruct((n_tok, nh * ML_DV), F32)]
    if emit_state:
        half_spec = pl.BlockSpec((1, N_DIR, hp, ML_DQK, ML_DV), lambda b, p: (b, 0, p, 0, 0))
        out_specs += [half_spec, half_spec, m_spec]
        out_shape += [jax.ShapeDtypeStruct((batch, N_DIR, nh, ML_DQK, ML_DV), F32),
                      jax.ShapeDtypeStruct((batch, N_DIR, nh, ML_DQK, ML_AUG - ML_DV), F32),
                      jax.ShapeDtypeStruct((batch, N_DIR, nh, 1, 128), F32)]
    per_hd = lambda rows, width: pltpu.VMEM((hp, N_DIR, rows, width), F32)
    outs = pl.pallas_call(
        functools.partial(_mlstm_kernel, seq=seq, hp=hp, zero_init=zero_init, emit_state=emit_state),
        grid=(batch, n_hb),
        in_specs=in_specs, out_specs=out_specs, out_shape=out_shape,
        scratch_shapes=[per_hd(8, GROUP), per_hd(8, GROUP), per_hd(8, GROUP),
                        per_hd(seq, 128), per_hd(seq, 128),
                        per_hd(max(n_chunks, 8), 128),
                        per_hd(seq, ML_AUG), per_hd(seq, ML_AUG)],
        compiler_params=_params("parallel", "parallel"),
        name="mlstm_scan",
    )(*args)
    return outs[0], (tuple(outs[1:4]) if emit_state else None)


def _trunk(x, batch, seq, rows, mods, tokens_per_cond, st_d, st_ml, emit_state, p):
    depth = p["w_ada"].shape[0]
    tile_cond = lambda tile: (lambda i: (i * tile) // tokens_per_cond)
    new_d = new_ml = None
    for layer in range(depth):
        mod = mods[layer]
        kind, j = layer % 3, layer // 3
        if kind == 0:
            x = _fnet_mix(x, seq, j, p["norm_mix"][layer], mod, tokens_per_cond, p["fnet_w"], p["fnet_b"][j])
        elif kind == 1:
            n_main = DN_HEADS * (2 * DN_DK + 2 * DN_DV)
            main, gt = _norm_proj(x, p["norm_mix"][layer], mod, tokens_per_cond, p["dn_w_in"], j, n_main)
            s0 = None if st_d is None else st_d[:, j]
            o, sfin = _gdn_scan(main, gt, batch, seq, p["dn_conv_w"][j], p["dn_a_log"][j], p["dn_dt_bias"][j],
                                s0, emit_state)
            if emit_state:
                new_d = sfin
            x = _gated_out(x, o, main, 3, p["dn_norm"][j], mod, tile_cond(OUT_TOKEN_TILE), p["dn_w_out"][j],
                           DN_HEADS, jax.nn.silu)
        else:
            n_main = 2 * ML_HEADS * ML_DQK + 2 * ML_HEADS * ML_DV
            k_tile = (ML_HEADS * ML_DQK) // PROJ_COL_TILE
            assert ML_HEADS * ML_DQK == PROJ_COL_TILE
            main, gt, kt = _norm_proj(x, p["norm_mix"][layer], mod, tokens_per_cond, p["ml_w_in"], j, n_main,
                                      t_tile=k_tile)
            state0 = None
            if st_ml is not None:
                c0, n0, m0 = (s[:, j] for s in st_ml)
                pad = jnp.zeros(c0.shape[:-1] + (ML_AUG - ML_DV - 1,), F32)
                c_aug0 = jnp.concatenate([c0, n0[..., None], pad], axis=-1)
                m0b = jnp.broadcast_to(m0[..., None, None], m0.shape + (1, 128))
                state0 = (c_aug0, m0b)
            o, fin = _mlstm_scan(main, kt, gt, batch, seq, p["ml_b_i"][j], p["ml_b_f"][j], state0, emit_state)
            if emit_state:
                new_ml = (fin[0], fin[1][..., 0], fin[2][..., 0, 0])
            x = _gated_out(x, o, main, 2, p["ml_norm"][j], mod, tile_cond(OUT_TOKEN_TILE), p["ml_w_out"][j],
                           ML_HEADS, jax.nn.sigmoid)
        x = _conv_ffn(x, seq, rows, layer, p["norm_ffn"][layer], mod, tile_cond(FFN_TOKEN_TILE), p["ffn_w_up"],
                      p["ffn_conv_w"], p["ffn_conv_b"], p["ffn_w_down"], p["norm_final"],
                      final_norm=(layer == depth - 1))
    return x, new_d, new_ml


def kernel(x_prompt, x_sample, state_delta, state_mlstm_c, state_mlstm_n, state_mlstm_m, c, c_ctx, w_ada, b_ada, norm_mix, norm_ffn, norm_final, ffn_w_up, ffn_conv_w, ffn_conv_b, ffn_w_down, fnet_w, fnet_b, dn_w_in, dn_conv_w, dn_a_log, dn_dt_bias, dn_norm, dn_w_out, ml_w_in, ml_b_i, ml_b_f, ml_norm, ml_w_out):
    p = dict(w_ada=w_ada, norm_mix=norm_mix, norm_ffn=norm_ffn, norm_final=norm_final, ffn_w_up=ffn_w_up,
             ffn_conv_w=ffn_conv_w, ffn_conv_b=ffn_conv_b, ffn_w_down=ffn_w_down, fnet_w=fnet_w, fnet_b=fnet_b,
             dn_w_in=dn_w_in, dn_conv_w=dn_conv_w, dn_a_log=dn_a_log, dn_dt_bias=dn_dt_bias, dn_norm=dn_norm,
             dn_w_out=dn_w_out, ml_w_in=ml_w_in, ml_b_i=ml_b_i, ml_b_f=ml_b_f, ml_norm=ml_norm, ml_w_out=ml_w_out)
    b_ctx, t_ctx, _ = x_prompt.shape
    b_smp, t_smp, _ = x_sample.shape
    depth = w_ada.shape[0]

    cond8 = jnp.concatenate([c_ctx[None, :], c, jnp.zeros((8 - 1 - b_smp, D_MODEL), F32)], axis=0)
    mods = _adaln_all(cond8, w_ada, b_ada)
    mods_ctx = [mods[l, 0:1].reshape(1, 1, -1) for l in range(depth)]
    mods_smp = [mods[l, 1:1 + b_smp].reshape(b_smp, 1, -1) for l in range(depth)]

    y_ctx, new_d, new_ml = _trunk(x_prompt.reshape(b_ctx * t_ctx, D_MODEL), b_ctx, t_ctx, 1, mods_ctx,
                                  b_ctx * t_ctx, None, None, True, p)
    y_smp, _, _ = _trunk(x_sample.reshape(b_smp * t_smp, D_MODEL), b_smp, t_smp, t_smp // GRID_W, mods_smp,
                         t_smp, state_delta, (state_mlstm_c, state_mlstm_n, state_mlstm_m), False, p)

    new_c, new_n, new_m = new_ml
    return (y_ctx.reshape(b_ctx, t_ctx, D_MODEL), y_smp.reshape(b_smp, t_smp, D_MODEL),
            new_d[:, None], new_c[:, None], new_n[:, None], new_m[:, None])
```

```python
import collections
import functools

import numpy as np
import jax
import jax.numpy as jnp
from jax import lax
from jax.experimental import pallas as pl
from jax.experimental.pallas import tpu as pltpu

F32 = jnp.float32
BF16 = jnp.bfloat16

D_MODEL = 1024
EPS = 1e-6
N_DIR = 2
CHUNK = 64
LOG_CHUNK = 6
GROUP = 256
CHUNKS_PER_GROUP = GROUP // CHUNK
GDN_GROUP = 128
GDN_GROUPS_PER_STEP = 4
GDN_HEADS_PER_PASS = 4
GDN_STEP_POSITIONS = 2048
FNET_GROUP_DIM = 256
DN_HEADS, DN_DK, DN_DV, DN_CONV = 8, 128, 128, 5
ML_HEADS, ML_DQK, ML_DV = 8, 64, 128
D_FF = 2816
GRID_W = 64

FNET_STEP_TOKENS = 1024
FFN_TOKEN_TILE = 1024
FFN_FF_TILE = 256
FFN_TILES_PER_STEP = 4
FFN_SUB_TILES = 4
PROJ_TOKEN_TILE = 2048
PROJ_COL_TILE = 512
OUT_TOKEN_TILE = 512
ADA_COL_TILE = 1536
SCAN_STEP_POSITIONS = 1024
VMEM_LIMIT = 56 * 1024 * 1024


def _params(*sem):
    return pltpu.CompilerParams(dimension_semantics=sem, vmem_limit_bytes=VMEM_LIMIT)


def _dot(a, b):
    return jnp.dot(a.astype(BF16), b.astype(BF16), preferred_element_type=F32)


def _dot_nt(a, b):
    return lax.dot_general(a.astype(BF16), b.astype(BF16), (((1,), (1,)), ((), ())),
                           preferred_element_type=F32)


def _dot_tn(a, b):
    return lax.dot_general(a.astype(BF16), b.astype(BF16), (((0,), (0,)), ((), ())),
                           preferred_element_type=F32)


def _rms(x, g):
    return x * lax.rsqrt(jnp.mean(x * x, axis=-1, keepdims=True) + EPS) * g


def _norm_mod(x, g, sc, sh):
    return _rms(x, g) * (1.0 + sc) + sh


def _softplus(x):
    return jnp.maximum(x, 0.0) + jnp.log1p(jnp.exp(-jnp.abs(x)))


def _mod_spec(chunk, cond_of_tile):
    return pl.BlockSpec((1, 1, D_MODEL), lambda i, *_: (cond_of_tile(i), 0, chunk))


def _adaln_kernel(c_ref, w_ref, b_ref, o_ref):
    s = jax.nn.silu(c_ref[...])
    o_ref[0] = _dot(s, w_ref[0]) + b_ref[0]


def _adaln_all(cond8, w_ada, b_ada):
    depth = w_ada.shape[0]
    n_out = w_ada.shape[2]
    tn = ADA_COL_TILE
    return pl.pallas_call(
        _adaln_kernel,
        grid=(depth, n_out // tn),
        in_specs=[pl.BlockSpec((8, D_MODEL), lambda l, j: (0, 0)),
                  pl.BlockSpec((1, D_MODEL, tn), lambda l, j: (l, 0, j)),
                  pl.BlockSpec((1, 1, tn), lambda l, j: (l, 0, j))],
        out_specs=pl.BlockSpec((1, 8, tn), lambda l, j: (l, 0, j)),
        out_shape=jax.ShapeDtypeStruct((depth, 8, n_out), F32),
        compiler_params=_params("parallel", "parallel"),
        name="adaln",
    )(cond8, w_ada, b_ada.reshape(depth, 1, n_out))


def _norm_proj_kernel(x_ref, g_ref, sc_ref, sh_ref, w_ref, wg_ref, o_ref, gate_ref, *rest, t_tile):
    h_scr = rest[-1]

    @pl.when(pl.program_id(1) == 0)
    def _():
        h = _norm_mod(x_ref[...], g_ref[...], sc_ref[0], sh_ref[0]).astype(BF16)
        h_scr[...] = h
        gate_ref[...] = _dot_nt(wg_ref[...], h)

    y = _dot_nt(h_scr[...], w_ref[...])
    o_ref[...] = y.astype(o_ref.dtype)

    if t_tile is not None:
        t_ref = rest[0]

        @pl.when(pl.program_id(1) == t_tile)
        def _():
            for g in range(t_ref.shape[0]):
                t_ref[g] = y[g * GROUP:(g + 1) * GROUP, :].T


def _norm_proj(x, norm_g, mod, tokens_per_cond, w_stack, layer, n_main, t_tile=None):
    n_tok = x.shape[0]
    tm, tn = min(PROJ_TOKEN_TILE, tokens_per_cond), PROJ_COL_TILE
    cond_of_tile = lambda i: (i * tm) // tokens_per_cond
    w_t = jnp.swapaxes(w_stack, 1, 2)
    n_gate = w_t.shape[1] - n_main
    assert n_main % n_gate == 0 and n_gate % 8 == 0
    out_specs = [pl.BlockSpec((tm, tn), lambda i, j: (i, j)),
                 pl.BlockSpec((n_gate, tm), lambda i, j: (0, i))]
    out_shape = [jax.ShapeDtypeStruct((n_tok, n_main), BF16),
                 jax.ShapeDtypeStruct((n_gate, n_tok), F32)]
    if t_tile is not None:
        out_specs.append(pl.BlockSpec((tm // GROUP, tn, GROUP), lambda i, j: (i, 0, 0)))
        out_shape.append(jax.ShapeDtypeStruct((n_tok // GROUP, tn, GROUP), F32))
    return pl.pallas_call(
        functools.partial(_norm_proj_kernel, t_tile=t_tile),
        grid=(n_tok // tm, n_main // tn),
        in_specs=[pl.BlockSpec((tm, D_MODEL), lambda i, j: (i, 0)),
                  pl.BlockSpec((1, D_MODEL), lambda i, j: (0, 0)),
                  _mod_spec(1, cond_of_tile), _mod_spec(0, cond_of_tile),
                  pl.BlockSpec((None, tn, D_MODEL), lambda i, j: (layer, j, 0)),
                  pl.BlockSpec((None, n_gate, D_MODEL), lambda i, j: (layer, n_main // n_gate, 0))],
        out_specs=out_specs, out_shape=out_shape,
        scratch_shapes=[pltpu.VMEM((tm, D_MODEL), BF16)],
        compiler_params=_params("parallel", "arbitrary"),
        name="norm_proj",
    )(x, norm_g.reshape(1, D_MODEL), mod, mod, w_t, w_t)


def _gated_out_kernel(x_ref, o_ref, z_ref, ng_ref, g1_ref, w_ref, y_ref, wb_scr, *, n_heads, gate_fn):
    @pl.when(pl.program_id(0) == 0)
    def _():
        wb_scr[...] = w_ref[...].astype(BF16)

    dv = o_ref.shape[1] // n_heads
    parts = []
    for h in range(n_heads):
        sl = slice(h * dv, (h + 1) * dv)
        parts.append((_rms(o_ref[:, sl], ng_ref[...]) * gate_fn(z_ref[:, sl].astype(F32))).astype(BF16))
    hs = jnp.concatenate(parts, axis=1)
    y = jnp.dot(hs, wb_scr[...], preferred_element_type=F32)
    y_ref[...] = x_ref[...] + g1_ref[0] * y


def _gated_out(x, o, main, z_block, norm_g, mod, cond_of_tile, w_out, n_heads, gate_fn):
    n_tok = x.shape[0]
    tm = OUT_TOKEN_TILE
    width = o.shape[1]
    return pl.pallas_call(
        functools.partial(_gated_out_kernel, n_heads=n_heads, gate_fn=gate_fn),
        grid=(n_tok // tm,),
        in_specs=[pl.BlockSpec((tm, D_MODEL), lambda i: (i, 0)),
                  pl.BlockSpec((tm, width), lambda i: (i, 0)),
                  pl.BlockSpec((tm, width), lambda i: (i, z_block)),
                  pl.BlockSpec((1, width // n_heads), lambda i: (0, 0)),
                  _mod_spec(2, cond_of_tile),
                  pl.BlockSpec((width, D_MODEL), lambda i: (0, 0))],
        out_specs=pl.BlockSpec((tm, D_MODEL), lambda i: (i, 0)),
        out_shape=jax.ShapeDtypeStruct((n_tok, D_MODEL), F32),
        scratch_shapes=[pltpu.VMEM((width, D_MODEL), BF16)],
        compiler_params=_params("arbitrary"),
        name="gated_out",
    )(x, o, main, norm_g.reshape(1, -1), mod, w_out)


def _dft_mats(n):
    k = np.arange(n, dtype=np.int64)
    ang = 2.0 * np.pi * ((k[:, None] * k[None, :]) % n).astype(np.float64) / n
    s = 1.0 / np.sqrt(n)
    return np.cos(ang) * s, np.sin(ang) * s


def _fnet_kernel(x_ref, ng_ref, sc_ref, sh_ref, g1_ref, ct_ref, st_ref, cs_ref, w_ref, b_ref, y_ref, wb_scr):
    @pl.when(pl.program_id(0) == 0)
    def _():
        wb_scr[...] = w_ref[...].astype(BF16)

    seq = ct_ref.shape[0]
    seqs = [pl.ds(r, seq) for r in range(0, x_ref.shape[0], seq)]
    gd = FNET_GROUP_DIM
    hs = [_norm_mod(x_ref[s, :], ng_ref[...], sc_ref[0], sh_ref[0]).astype(BF16) for s in seqs]
    ps = [jnp.dot(ct_ref[...], h, preferred_element_type=F32) for h in hs]
    qs = [jnp.dot(st_ref[...], h, preferred_element_type=F32) for h in hs]
    fs = []
    for p, q in zip(ps, qs):
        parts = []
        for g in range(D_MODEL // gd):
            sl = slice(g * gd, (g + 1) * gd)
            pq = jnp.concatenate([p[:, sl], q[:, sl]], axis=1).astype(BF16)
            parts.append(jnp.dot(pq, cs_ref[...], preferred_element_type=F32).astype(BF16))
        fs.append(jnp.concatenate(parts, axis=1))
    ys = [jnp.dot(f, wb_scr[...], preferred_element_type=F32) + b_ref[...] for f in fs]
    for s, y in zip(seqs, ys):
        y_ref[s, :] = x_ref[s, :] + g1_ref[0] * y


def _fnet_mix(x, seq, layer, norm_g, mod, tokens_per_cond, w, b):
    n_tok = x.shape[0]
    tm = seq * max(1, min(FNET_STEP_TOKENS, tokens_per_cond) // seq)
    cond_of_tile = lambda i: (i * tm) // tokens_per_cond
    ct, st = _dft_mats(seq)
    cc, sc = _dft_mats(FNET_GROUP_DIM)
    cs = np.concatenate([cc, -sc], axis=0)
    const = lambda i: (0, 0)
    ct, st, cs = (jnp.asarray(m, F32).astype(BF16) for m in (ct, st, cs))
    return pl.pallas_call(
        _fnet_kernel,
        grid=(n_tok // tm,),
        in_specs=[pl.BlockSpec((tm, D_MODEL), lambda i: (i, 0)),
                  pl.BlockSpec((1, D_MODEL), const),
                  _mod_spec(1, cond_of_tile), _mod_spec(0, cond_of_tile), _mod_spec(2, cond_of_tile),
                  pl.BlockSpec((seq, seq), const),
                  pl.BlockSpec((seq, seq), const),
                  pl.BlockSpec((2 * FNET_GROUP_DIM, FNET_GROUP_DIM), const),
                  pl.BlockSpec((None, D_MODEL, D_MODEL), lambda i: (layer, 0, 0)),
                  pl.BlockSpec((1, D_MODEL), const)],
        out_specs=pl.BlockSpec((tm, D_MODEL), lambda i: (i, 0)),
        out_shape=jax.ShapeDtypeStruct((n_tok, D_MODEL), F32),
        scratch_shapes=[pltpu.VMEM((D_MODEL, D_MODEL), BF16)],
        compiler_params=_params("arbitrary"),
        name="fnet",
    )(x, norm_g.reshape(1, D_MODEL), mod, mod, mod,
      ct, st, cs, w, b.reshape(1, D_MODEL))


SUBLANES = 8


def _shift_tokens(x, delta, period):
    n, c = x.shape
    assert n % period == 0 and period % SUBLANES == 0 and 0 < abs(delta) < period
    blocks = range(0, n, period)
    if delta % SUBLANES == 0:
        zeros = jnp.zeros((abs(delta), c), x.dtype)
        if delta > 0:
            pieces = [piece for b in blocks for piece in (x[b + delta:b + period], zeros)]
        else:
            pieces = [piece for b in blocks for piece in (zeros, x[b:b + period + delta])]
        return jnp.concatenate(pieces, axis=0)
    assert abs(delta) < SUBLANES
    y = pltpu.roll(x, (-delta) % n, 0)
    r = lax.broadcasted_iota(jnp.int32, (SUBLANES, 1), 0) + delta
    keep = ((r >= 0) & (r <= SUBLANES - 1)).astype(x.dtype)
    if delta < 0:
        pieces = [piece for b in blocks for piece in (y[b:b + SUBLANES] * keep, y[b + SUBLANES:b + period])]
    else:
        pieces = [piece for b in blocks
                  for piece in (y[b:b + period - SUBLANES], y[b + period - SUBLANES:b + period] * keep)]
    return jnp.concatenate(pieces, axis=0)


def _dwconv_tokens(g, cw_ref, seq, rows):
    width = seq // rows
    g_cols = (_shift_tokens(g, -1, width), g, _shift_tokens(g, 1, width))
    out = None
    for di in (-1, 0, 1):
        if rows == 1 and di != 0:
            continue
        r = sum(g_cols[dj] * cw_ref[pl.ds(3 * (di + 1) + dj, 1), :] for dj in range(3))
        if di != 0:
            r = _shift_tokens(r, di * width, seq)
        out = r if out is None else out + r
    return out


def _ffn_kernel(x_ref, ng_ref, sc_ref, sh_ref, g2_ref, *rest, seq, rows, final_norm, n_ff_tiles):
    tiles = [rest[5 * t:5 * t + 5] for t in range(FFN_TILES_PER_STEP)]
    nf_ref, y_ref, h_scr, acc_scr = rest[5 * FFN_TILES_PER_STEP:]
    j = pl.program_id(1)
    last = pl.num_programs(1) - 1

    @pl.when(j == 0)
    def _():
        h_scr[...] = _norm_mod(x_ref[...], ng_ref[...], sc_ref[0], sh_ref[0]).astype(BF16)
        acc_scr[...] = jnp.zeros_like(acc_scr)

    tm = h_scr.shape[0]
    sub = tm // FFN_SUB_TILES
    parts = [pl.ds(r, sub) for r in range(0, tm, sub)]
    per_seq = max(seq // sub, 1)

    def run(tiles):
        gs = [[jnp.dot(h_scr[p, :], wg[...].astype(BF16), preferred_element_type=F32) for p in parts]
              for _, wg, _, _, _ in tiles]
        as_ = [[jnp.dot(h_scr[p, :], wa[...].astype(BF16), preferred_element_type=F32) for p in parts]
               for wa, _, _, _, _ in tiles]
        acts = []
        for (_, _, cw, cb, _), g_t, a_t in zip(tiles, gs, as_):
            convs = []
            for i in range(0, len(parts), per_seq):
                g = g_t[i] if per_seq == 1 else jnp.concatenate(g_t[i:i + per_seq], axis=0)
                c = _dwconv_tokens(g, cw, seq, rows) + cb[...]
                convs += [c[k * sub:(k + 1) * sub] for k in range(per_seq)]
            acts.append([(jax.nn.silu(c) * a).astype(BF16) for c, a in zip(convs, a_t)])
        wd = jnp.concatenate([t[4][...].astype(BF16) for t in tiles], axis=0)
        for k, p in enumerate(parts):
            act = jnp.concatenate([a_t[k] for a_t in acts], axis=1)
            acc_scr[p, :] += jnp.dot(act, wd, preferred_element_type=F32)

    n_tail = n_ff_tiles % FFN_TILES_PER_STEP
    if n_tail:
        pl.when(j < last)(lambda: run(tiles))
        pl.when(j == last)(lambda: run(tiles[:n_tail]))
    else:
        run(tiles)

    @pl.when(j == last)
    def _():
        y = x_ref[...] + g2_ref[0] * acc_scr[...]
        if final_norm:
            y = _rms(y, nf_ref[...])
        y_ref[...] = y


def _conv_ffn(x, seq, rows, layer, norm_g, mod, cond_of_tile, w_up, conv_w, conv_b, w_down, norm_final, final_norm):
    n_tok = x.shape[0]
    tm, tf = FFN_TOKEN_TILE, FFN_FF_TILE
    n_ff_tiles = D_FF // tf
    depth = w_up.shape[0]
    tps = FFN_TILES_PER_STEP
    conv_w, conv_b = conv_w.reshape(depth, 9, D_FF), conv_b.reshape(depth, 1, D_FF)
    tile_specs, tile_args = [], []
    for t in range(tps):
        tile = lambda j, t=t: jnp.minimum(j * tps + t, n_ff_tiles - 1)
        tile_specs += [pl.BlockSpec((None, D_MODEL, tf), lambda i, j, tile=tile: (layer, 0, tile(j))),
                       pl.BlockSpec((None, D_MODEL, tf), lambda i, j, tile=tile: (layer, 0, n_ff_tiles + tile(j))),
                       pl.BlockSpec((None, 9, tf), lambda i, j, tile=tile: (layer, 0, tile(j))),
                       pl.BlockSpec((None, 1, tf), lambda i, j, tile=tile: (layer, 0, tile(j))),
                       pl.BlockSpec((None, tf, D_MODEL), lambda i, j, tile=tile: (layer, tile(j), 0))]
        tile_args += [w_up, w_up, conv_w, conv_b, w_down]
    return pl.pallas_call(
        functools.partial(_ffn_kernel, seq=seq, rows=rows, final_norm=final_norm, n_ff_tiles=n_ff_tiles),
        grid=(n_tok // tm, pl.cdiv(n_ff_tiles, tps)),
        in_specs=[pl.BlockSpec((tm, D_MODEL), lambda i, j: (i, 0)),
                  pl.BlockSpec((1, D_MODEL), lambda i, j: (0, 0)),
                  _mod_spec(4, cond_of_tile), _mod_spec(3, cond_of_tile), _mod_spec(5, cond_of_tile),
                  *tile_specs,
                  pl.BlockSpec((1, D_MODEL), lambda i, j: (0, 0))],
        out_specs=pl.BlockSpec((tm, D_MODEL), lambda i, j: (i, 0)),
        out_shape=jax.ShapeDtypeStruct((n_tok, D_MODEL), F32),
        scratch_shapes=[pltpu.VMEM((tm, D_MODEL), BF16), pltpu.VMEM((tm, D_MODEL), F32)],
        compiler_params=_params("parallel", "arbitrary"),
        name="conv_ffn",
    )(x, norm_g.reshape(1, D_MODEL), mod, mod, mod, *tile_args, norm_final.reshape(1, D_MODEL))


GroupMasks = collections.namedtuple("GroupMasks", "r c same incl strict")


def _chunk_masks():
    r = lax.broadcasted_iota(jnp.int32, (CHUNK, CHUNK), 0)
    c = lax.broadcasted_iota(jnp.int32, (CHUNK, CHUNK), 1)
    return (r >= c, r <= c)


def _group_masks(group=GROUP):
    r = lax.broadcasted_iota(jnp.int32, (group, group), 0)
    c = lax.broadcasted_iota(jnp.int32, (group, group), 1)
    same = (r >> LOG_CHUNK) == (c >> LOG_CHUNK)
    return GroupMasks(r, c, same, (same & (r >= c), same & (r <= c)), (same & (r > c), same & (r < c)))


def _gate_layouts(gt, batch, n_heads, hp, group=GROUP):
    seq = gt.shape[1] // batch
    n_groups, n_chunks = seq // group, seq // CHUNK
    assert seq % group == 0 and n_groups <= 8
    g = jnp.transpose(gt.reshape(N_DIR, 2, n_heads, batch, seq), (3, 2, 0, 1, 4))
    rows = g.reshape(g.shape[:4] + (n_groups, group))
    rows = jnp.pad(rows, ((0, 0),) * 4 + ((0, 8 - n_groups), (0, 0)))
    cols = jnp.swapaxes(g.reshape(batch, n_heads // hp, hp * N_DIR * 2 * n_chunks, CHUNK), -1, -2)
    return rows, cols


def _col_gate(gcol_ref, hh, d, kind, n_chunks):
    start = ((hh * N_DIR + d) * 2 + kind) * n_chunks
    return gcol_ref[0, 0, :, start:start + n_chunks]


def _split3(x):
    x1 = x.astype(BF16)
    r1 = x - x1.astype(F32)
    x2 = r1.astype(BF16)
    return x1, x2, (r1 - x2.astype(F32)).astype(BF16)


def _dot_mask_rhs(x, mask):
    m = x.shape[0]
    y = jnp.dot(jnp.concatenate(_split3(x), axis=0), jnp.where(mask, 1.0, 0.0).astype(BF16),
                preferred_element_type=F32)
    return y[:m] + y[m:2 * m] + y[2 * m:]


def _dot_mask_lhs(mask, x):
    mb = jnp.where(mask, 1.0, 0.0).astype(BF16)
    y1, y2, y3 = (jnp.dot(mb, p, preferred_element_type=F32) for p in _split3(x))
    return y1 + y2 + y3


def _cumsum_rows(row, d, gm):
    return _dot_mask_rhs(row, gm.incl[1 - d])


def _cumsum_cols(col, d, cm):
    return _dot_mask_lhs(cm[d], col)


def _store_cols(dst, col, n_chunks):
    for n in range(n_chunks):
        dst[n * CHUNK:(n + 1) * CHUNK, :] = jnp.broadcast_to(col[:, n:n + 1], (CHUNK, 128))


def _wide(x, width=None):
    reps = (GROUP if width is None else width) // x.shape[1]
    return x if reps == 1 else jnp.concatenate([x] * reps, axis=1)


def _rows(i, size):
    if isinstance(i, int):
        return pl.ds(i * size, size)
    return pl.ds(pl.multiple_of(i * size, size), size)


def _loop(n, body, init):
    if n == 1:
        return body(0, init)
    return lax.fori_loop(0, n, body, init)


def _conv_silu(x, w_ref):
    n = x.shape[0]
    k = w_ref.shape[0]
    acc = None
    for j in range(k):
        delta = j - k // 2
        term = (x if delta == 0 else _shift_tokens(x, delta, n)) * w_ref[pl.ds(j, 1), :]
        acc = term if acc is None else acc + term
    return jax.nn.silu(acc)


def _l2norm(x):
    return x * lax.rsqrt(jnp.sum(x * x, axis=-1, keepdims=True) + EPS)


def _blockdiag_tri_inverse(mats, gm):
    dot16 = lambda x, y: jnp.dot(x, y, preferred_element_type=F32).astype(BF16)
    base = (gm.r >> 3) == (gm.c >> 3)
    eye = jnp.where(gm.r == gm.c, 1.0, 0.0)
    zero16 = jnp.zeros((), BF16)
    a16s = [a.astype(BF16) for a in mats]
    invs = [eye - jnp.where(base, a, 0.0) for a in mats]
    ps = [jnp.where(base, -a16, zero16) for a16 in a16s]
    for _ in range(2):
        ps = [dot16(p, p) for p in ps]
        invs = [inv + jnp.dot(inv.astype(BF16), p, preferred_element_type=F32) for inv, p in zip(invs, ps)]
    for s in (3, 4, 5):
        join = ((gm.r >> s) ^ (gm.c >> s)) == 1
        inv16s = [inv.astype(BF16) for inv in invs]
        tmp = [dot16(jnp.where(join, a16, zero16), inv16) for a16, inv16 in zip(a16s, inv16s)]
        invs = [inv - jnp.dot(inv16, t, preferred_element_type=F32) for inv, inv16, t in zip(invs, inv16s, tmp)]
    return invs


def _gdn_kernel(*refs, seq, hp, zero_init, emit_state):
    alog_ref, dtb_ref, q_ref, k_ref, v_ref, cwq_ref, cwk_ref, cwv_ref, grow_ref, gcol_ref = refs[:10]
    pos = 10
    s0_ref = None
    if not zero_init:
        s0_ref = refs[pos]
        pos += 1
    o_ref = refs[pos]
    pos += 1
    sfin_ref = None
    if emit_state:
        sfin_ref = refs[pos]
        pos += 1
    (q_scr, k_scr, v_scr, gcr_scr, gcb_scr, bb_scr, gl_scr,
     u_scr, w_scr, qd_scr, kd_scr, qkd_scr) = refs[pos:]

    n_chunks = seq // CHUNK
    grp = GDN_GROUP
    cm = _chunk_masks()
    gm = _group_masks(grp)
    head_dirs = [(hh, d) for hh in range(hp) for d in range(N_DIR)]
    lanes = lambda hh: slice(hh * DN_DK, (hh + 1) * DN_DK)

    q_scr[...] = _conv_silu(q_ref[...].astype(F32), cwq_ref)
    k_scr[...] = _conv_silu(k_ref[...].astype(F32), cwk_ref)
    v_scr[...] = _conv_silu(v_ref[...].astype(F32), cwv_ref)
    for hh in range(hp):
        q_scr[:, lanes(hh)] = _l2norm(q_scr[:, lanes(hh)]) * (DN_DK ** -0.5)
        k_scr[:, lanes(hh)] = _l2norm(k_scr[:, lanes(hh)])

    for hh, d in head_dirs:
        head = pl.program_id(1) * hp + hh
        neg_a = -jnp.exp(jnp.full((1, 1), alog_ref[d, head], F32))
        dtb = dtb_ref[d, head]
        gcr_scr[hh, d] = _cumsum_rows(neg_a * _softplus(grow_ref[0, hh, d, 0] + dtb), d, gm)
        gc_col = _cumsum_cols(neg_a * _softplus(_col_gate(gcol_ref, hh, d, 0, n_chunks) + dtb), d, cm)
        _store_cols(gcb_scr.at[hh, d], gc_col, n_chunks)
        _store_cols(bb_scr.at[hh, d], jax.nn.sigmoid(_col_gate(gcol_ref, hh, d, 1, n_chunks)), n_chunks)
        last = CHUNK - 1 if d == 0 else 0
        _store_cols(gl_scr.at[hh, d], jnp.broadcast_to(gc_col[last:last + 1, :], gc_col.shape), n_chunks)

    gps = min(GDN_GROUPS_PER_STEP, seq // grp)

    def group_step(it, carry):
        gis = [it * gps + k for k in range(gps)]
        rows = [_rows(gi, grp) for gi in gis]
        for h0 in range(0, hp, GDN_HEADS_PER_PASS):
            heads = range(h0, min(h0 + GDN_HEADS_PER_PASS, hp))
            kqs = {}
            for k, r in enumerate(rows):
                for hh in heads:
                    kg, qg = k_scr[r, lanes(hh)], q_scr[r, lanes(hh)]
                    kqs[k, hh] = _dot_nt(jnp.concatenate([kg, qg], axis=0), kg)
            chains = [(k, hh, d) for k in range(len(gis)) for hh in heads for d in range(N_DIR)]
            mats = []
            for k, hh, d in chains:
                gcb = gcb_scr[hh, d, rows[k], :]
                gcr = gcr_scr[hh, d, pl.ds(gis[k], 1), :]
                decay = jnp.exp(jnp.where(gm.incl[d], _wide(gcb, grp) - gcr, -jnp.inf))
                mats.append(jnp.where(gm.strict[d],
                                      kqs[k, hh][:grp] * _wide(bb_scr[hh, d, rows[k], :], grp) * decay, 0.0))
                qkd = kqs[k, hh][grp:] * decay
                qkd_scr[hh, d, rows[k], :] = sum(qkd[:, n * CHUNK:(n + 1) * CHUNK] for n in range(grp // CHUNK))
            t_invs = _blockdiag_tri_inverse(mats, gm)
            for (k, hh, d), t_inv in zip(chains, t_invs):
                r = rows[k]
                qg, kg, vg = q_scr[r, lanes(hh)], k_scr[r, lanes(hh)], v_scr[r, lanes(hh)]
                gcb = gcb_scr[hh, d, r, :]
                bb = bb_scr[hh, d, r, :]
                e_gc = jnp.exp(gcb)
                uw = _dot(t_inv, jnp.concatenate([vg * bb, kg * bb * e_gc], axis=1))
                u_scr[hh, d, r, :] = uw[:, :DN_DV]
                w_scr[hh, d, r, :] = uw[:, DN_DV:]
                qd_scr[hh, d, r, :] = qg * e_gc
                kd_scr[hh, d, r, :] = kg * jnp.exp(gl_scr[hh, d, r, :] - gcb)
        return carry

    _loop(seq // (grp * gps), group_step, 0)

    o_ref[...] = jnp.zeros_like(o_ref)

    def chunk_step(n, carry):
        idxs = [n if d == 0 else n_chunks - 1 - n for _, d in head_dirs]
        rows = [_rows(idx, CHUNK) for idx in idxs]
        wqs = [_dot(jnp.concatenate([w_scr[hh, d, r, :], qd_scr[hh, d, r, :]], axis=0), s)
               for (hh, d), r, s in zip(head_dirs, rows, carry)]
        v_news = [u_scr[hh, d, r, :] - wq[:CHUNK] for (hh, d), r, wq in zip(head_dirs, rows, wqs)]
        o_ns = [wq[CHUNK:] + _dot(qkd_scr[hh, d, r, :], v_new)
                for (hh, d), r, wq, v_new in zip(head_dirs, rows, wqs, v_news)]
        new = []
        for (hh, d), idx, r, s, v_new in zip(head_dirs, idxs, rows, carry, v_news):
            s_decay = jnp.exp(gl_scr[hh, d, pl.ds(idx * CHUNK, 1), :])
            new.append(s * s_decay + _dot_tn(kd_scr[hh, d, r, :], v_new))
        for (hh, d), r, o_n in zip(head_dirs, rows, o_ns):
            o_ref[r, lanes(hh)] += o_n
        return tuple(new)

    if zero_init:
        init = (jnp.zeros((DN_DK, DN_DV), F32),) * len(head_dirs)
    else:
        init = tuple(s0_ref[0, d, hh] for hh, d in head_dirs)
    fin = lax.fori_loop(0, n_chunks, chunk_step, init)
    if emit_state:
        for i, (hh, d) in enumerate(head_dirs):
            sfin_ref[0, d, hh] = fin[i]


def _gdn_scan(main, gt, batch, seq, conv_w, a_log, dt_bias, s0, emit_state):
    n_tok = batch * seq
    nh = DN_HEADS
    hp = min(nh, max(2, GDN_STEP_POSITIONS // seq))
    n_hb = nh // hp
    assert seq % (GDN_GROUP * min(GDN_GROUPS_PER_STEP, seq // GDN_GROUP)) == 0
    grow, gcol = _gate_layouts(gt, batch, nh, hp, GDN_GROUP)
    zero_init = s0 is None
    smem = pl.BlockSpec(memory_space=pltpu.SMEM)
    qkv_spec = lambda off: pl.BlockSpec((seq, hp * DN_DK), lambda b, h: (b, off + h))
    cw_spec = lambda off: pl.BlockSpec((DN_CONV, hp * DN_DK), lambda b, h: (0, off + h))
    state_spec = pl.BlockSpec((1, N_DIR, hp, DN_DK, DN_DV), lambda b, h: (b, 0, h, 0, 0))
    in_specs = [smem, smem, qkv_spec(0), qkv_spec(n_hb), qkv_spec(2 * n_hb),
                cw_spec(0), cw_spec(n_hb), cw_spec(2 * n_hb),
                pl.BlockSpec((1, hp) + grow.shape[2:], lambda b, h: (b, h, 0, 0, 0, 0)),
                pl.BlockSpec((1, 1) + gcol.shape[2:], lambda b, h: (b, h, 0, 0))]
    args = [a_log, dt_bias, main, main, main, conv_w, conv_w, conv_w, grow, gcol]
    if not zero_init:
        in_specs.append(state_spec)
        args.append(s0)
    out_specs = [pl.BlockSpec((seq, hp * DN_DV), lambda b, h: (b, h))]
    out_shape = [jax.ShapeDtypeStruct((n_tok, nh * DN_DV), F32)]
    if emit_state:
        out_specs.append(state_spec)
        out_shape.append(jax.ShapeDtypeStruct((batch, N_DIR, nh, DN_DK, DN_DV), F32))
    per_dir = lambda width: pltpu.VMEM((hp, N_DIR, seq, width), F32)
    qkv_scr = pltpu.VMEM((seq, hp * DN_DK), F32)
    outs = pl.pallas_call(
        functools.partial(_gdn_kernel, seq=seq, hp=hp, zero_init=zero_init, emit_state=emit_state),
        grid=(batch, n_hb),
        in_specs=in_specs, out_specs=out_specs, out_shape=out_shape,
        scratch_shapes=[qkv_scr, qkv_scr, qkv_scr,
                        pltpu.VMEM((hp, N_DIR, 8, GDN_GROUP), F32),
                        per_dir(128), per_dir(128), per_dir(128),
                        per_dir(DN_DV), per_dir(DN_DK), per_dir(DN_DK), per_dir(DN_DK), per_dir(CHUNK)],
        compiler_params=_params("parallel", "parallel"),
        name="gdn_scan",
    )(*args)
    return outs[0], (outs[1] if emit_state else None)


ML_AUG = 2 * ML_DV


def _mlstm_kernel(*refs, seq, hp, zero_init, emit_state):
    bi_ref, bf_ref, q_ref, k_ref, v_ref, kt_ref, grow_ref, gcol_ref = refs[:8]
    pos = 8
    c0_ref = m0_ref = None
    if not zero_init:
        c0_ref, m0_ref = refs[pos:pos + 2]
        pos += 2
    o_ref = refs[pos]
    pos += 1
    cfin_ref = nfin_ref = mfin_ref = None
    if emit_state:
        cfin_ref, nfin_ref, mfin_ref = refs[pos:pos + 3]
        pos += 3
    bcr_scr, lir_scr, kwr_scr, bcb_scr, mi_scr, dec_scr, dc_scr, cin_scr = refs[pos:]

    n_chunks = seq // CHUNK
    n_groups = seq // GROUP
    head0 = pl.program_id(1) * hp
    cm = _chunk_masks()
    gm = _group_masks()
    ones_col = jnp.where(lax.broadcasted_iota(jnp.int32, (GROUP, ML_DV), 1) == 0, 1.0, 0.0)
    head_dirs = [(hh, d) for hh in range(hp) for d in range(N_DIR)]

    m_fin = {}
    row_group = lax.broadcasted_iota(jnp.int32, (8, GROUP), 0)
    row_chunk = lax.broadcasted_iota(jnp.int32, (8, GROUP), 1) >> LOG_CHUNK
    for hh, d in head_dirs:
        b_i = bi_ref[d, head0 + hh]
        b_f = bf_ref[d, head0 + hh]
        li_row = grow_ref[0, hh, d, 0] + b_i
        lf_row = -_softplus(-(grow_ref[0, hh, d, 1] + b_f))
        bc_row = _cumsum_rows(lf_row, d, gm)
        lir_scr[hh, d] = li_row
        bcr_scr[hh, d] = bc_row
        w_row = _dot_mask_rhs(lf_row, gm.same) - bc_row + li_row
        m_out_row = jnp.zeros((8, GROUP), F32)
        li_col = _col_gate(gcol_ref, hh, d, 0, n_chunks) + b_i
        bc_col = _cumsum_cols(-_softplus(-(_col_gate(gcol_ref, hh, d, 1, n_chunks) + b_f)), d, cm)
        last = CHUNK - 1 if d == 0 else 0
        b_last = bc_col[last:last + 1, :]
        w_col = b_last - bc_col + li_col
        w_max = jnp.max(w_col, axis=0, keepdims=True)
        m = jnp.zeros((1, 1), F32) if zero_init else m0_ref[0, d, hh][:, 0:1]
        for step in range(n_chunks):
            n = step if d == 0 else n_chunks - 1 - step
            sl = slice(n * CHUNK, (n + 1) * CHUNK)
            m_new = jnp.maximum(b_last[:, n:n + 1] + m, w_max[:, n:n + 1])
            bcb_scr[hh, d, sl, :] = jnp.broadcast_to(bc_col[:, n:n + 1], (CHUNK, 128))
            mi_scr[hh, d, sl, :] = jnp.broadcast_to(m, (CHUNK, 128))
            dec_scr[hh, d, n:n + 1, :] = jnp.broadcast_to(jnp.exp(b_last[:, n:n + 1] + m - m_new), (1, 128))
            in_chunk = (row_group == n // CHUNKS_PER_GROUP) & (row_chunk == n % CHUNKS_PER_GROUP)
            m_out_row = jnp.where(in_chunk, m_new, m_out_row)
            m = m_new
        m_fin[hh, d] = m
        kwr_scr[hh, d] = jnp.exp(w_row - m_out_row)

    def load_kv(rows, hh):
        kg = k_ref[rows, hh * ML_DQK:(hh + 1) * ML_DQK].astype(F32)
        v_aug = jnp.concatenate([v_ref[rows, hh * ML_DV:(hh + 1) * ML_DV].astype(F32), ones_col], axis=1)
        return kg, v_aug

    def delta_step(gi, carry):
        rows = _rows(gi, GROUP)
        v_augs = [load_kv(rows, hh)[1] for hh in range(hp)]
        kt = kt_ref[gi]
        lhs = []
        for hh, d in head_dirs:
            kw_t = kt[hh * ML_DQK:(hh + 1) * ML_DQK, :] * kwr_scr[hh, d, pl.ds(gi, 1), :]
            lhs.append(jnp.where(gm.same, jnp.concatenate([kw_t] * CHUNKS_PER_GROUP, axis=0), 0.0))
        for (hh, d), kw in zip(head_dirs, lhs):
            dc_scr[hh, d, rows, :] = _dot(kw, v_augs[hh])
        return carry

    _loop(n_groups, delta_step, 0)

    def prefix_step(n, carry):
        new = []
        for (hh, d), c_aug in zip(head_dirs, carry):
            idx = n if d == 0 else n_chunks - 1 - n
            rows = _rows(idx, CHUNK)
            cin_scr[hh, d, rows, :] = c_aug
            new.append(c_aug * _wide(dec_scr[hh, d, pl.ds(idx, 1), :], ML_AUG) + dc_scr[hh, d, rows, :])
        return tuple(new)

    if zero_init:
        init = (jnp.zeros((ML_DQK, ML_AUG), F32),) * len(head_dirs)
    else:
        init = tuple(c0_ref[0, d, hh] for hh, d in head_dirs)
    fin = lax.fori_loop(0, n_chunks, prefix_step, init)

    def out_step(gi, carry):
        rows = _rows(gi, GROUP)
        kvs = [load_kv(rows, hh) for hh in range(hp)]
        qgs = [q_ref[rows, hh * ML_DQK:(hh + 1) * ML_DQK].astype(F32) * (ML_DQK ** -0.5) for hh in range(hp)]
        qks = [_dot_nt(qg, kv[0]) for qg, kv in zip(qgs, kvs)]
        chunk = lambda n: slice(n * CHUNK, (n + 1) * CHUNK)
        inters = []
        for hh, d in head_dirs:
            c_in = cin_scr[hh, d, rows, :]
            inters.append(jnp.concatenate(
                [_dot(qgs[hh][chunk(n)], c_in[chunk(n)]) for n in range(CHUNKS_PER_GROUP)], axis=0))
        m_ts, b_ms, ss = [], [], []
        for hh, d in head_dirs:
            b_colb = bcb_scr[hh, d, rows, :]
            b_m = b_colb[:, 0:1] + mi_scr[hh, d, rows, :][:, 0:1]
            d_log = jnp.where(gm.incl[d], _wide(b_colb) - bcr_scr[hh, d, pl.ds(gi, 1), :]
                              + lir_scr[hh, d, pl.ds(gi, 1), :], -jnp.inf)
            m_t = jnp.maximum(b_m, jnp.max(d_log, axis=-1, keepdims=True))
            m_ts.append(m_t)
            b_ms.append(b_m)
            ss.append(qks[hh] * jnp.exp(d_log - m_t))
        intras = [_dot(s, kvs[hh][1]) for (hh, d), s in zip(head_dirs, ss)]
        hs = []
        for inter, intra, m_t, b_m in zip(inters, intras, m_ts, b_ms):
            num = jnp.exp(b_m - m_t) * inter + intra
            den = num[:, ML_DV:ML_DV + 1]
            hs.append(num[:, :ML_DV] / jnp.maximum(jnp.abs(den), jnp.exp(-m_t)))
        for hh in range(hp):
            o_ref[rows, hh * ML_DV:(hh + 1) * ML_DV] = hs[N_DIR * hh] + hs[N_DIR * hh + 1]
        return carry

    _loop(n_groups, out_step, 0)

    if emit_state:
        for i, (hh, d) in enumerate(head_dirs):
            cfin_ref[0, d, hh] = fin[i][:, :ML_DV]
            nfin_ref[0, d, hh] = fin[i][:, ML_DV:]
            mfin_ref[0, d, hh] = jnp.broadcast_to(m_fin[hh, d], (1, 128))


def _mlstm_scan(main, kt, gt, batch, seq, b_i, b_f, state0, emit_state):
    n_tok = batch * seq
    n_groups = seq // GROUP
    n_chunks = seq // CHUNK
    nh = ML_HEADS
    hp = min(nh, max(2, SCAN_STEP_POSITIONS // seq))
    n_hb = nh // hp
    grow, gcol = _gate_layouts(gt, batch, nh, hp)
    zero_init = state0 is None
    smem = pl.BlockSpec(memory_space=pltpu.SMEM)
    k_off = (nh * ML_DQK) // (hp * ML_DQK)
    v_off = (2 * nh * ML_DQK) // (hp * ML_DV)
    c_spec = pl.BlockSpec((1, N_DIR, hp, ML_DQK, ML_AUG), lambda b, p: (b, 0, p, 0, 0))
    m_spec = pl.BlockSpec((1, N_DIR, hp, 1, 128), lambda b, p: (b, 0, p, 0, 0))
    in_specs = [smem, smem,
                pl.BlockSpec((seq, hp * ML_DQK), lambda b, p: (b, p)),
                pl.BlockSpec((seq, hp * ML_DQK), lambda b, p: (b, k_off + p)),
                pl.BlockSpec((seq, hp * ML_DV), lambda b, p: (b, v_off + p)),
                pl.BlockSpec((n_groups, hp * ML_DQK, GROUP), lambda b, p: (b, p, 0)),
                pl.BlockSpec((1, hp) + grow.shape[2:], lambda b, p: (b, p, 0, 0, 0, 0)),
                pl.BlockSpec((1, 1) + gcol.shape[2:], lambda b, p: (b, p, 0, 0))]
    args = [b_i, b_f, main, main, main, kt, grow, gcol]
    if not zero_init:
        in_specs += [c_spec, m_spec]
        args += list(state0)
    out_specs = [pl.BlockSpec((seq, hp * ML_DV), lambda b, p: (b, p))]
    out_shape = [jax.ShapeDtypeStruct((n_tok, nh * ML_DV), F32)]
    if emit_state:
        half_spec = pl.BlockSpec((1, N_DIR, hp, ML_DQK, ML_DV), lambda b, p: (b, 0, p, 0, 0))
        out_specs += [half_spec, half_spec, m_spec]
        out_shape += [jax.ShapeDtypeStruct((batch, N_DIR, nh, ML_DQK, ML_DV), F32),
                      jax.ShapeDtypeStruct((batch, N_DIR, nh, ML_DQK, ML_AUG - ML_DV), F32),
                      jax.ShapeDtypeStruct((batch, N_DIR, nh, 1, 128), F32)]
    per_hd = lambda rows, width: pltpu.VMEM((hp, N_DIR, rows, width), F32)
    outs = pl.pallas_call(
        functools.partial(_mlstm_kernel, seq=seq, hp=hp, zero_init=zero_init, emit_state=emit_state),
        grid=(batch, n_hb),
        in_specs=in_specs, out_specs=out_specs, out_shape=out_shape,
        scratch_shapes=[per_hd(8, GROUP), per_hd(8, GROUP), per_hd(8, GROUP),
                        per_hd(seq, 128), per_hd(seq, 128),
                        per_hd(max(n_chunks, 8), 128),
                        per_hd(seq, ML_AUG), per_hd(seq, ML_AUG)],
        compiler_params=_params("parallel", "parallel"),
        name="mlstm_scan",
    )(*args)
    return outs[0], (tuple(outs[1:4]) if emit_state else None)


def _trunk(x, batch, seq, rows, mods, tokens_per_cond, st_d, st_ml, emit_state, p):
    depth = p["w_ada"].shape[0]
    tile_cond = lambda tile: (lambda i: (i * tile) // tokens_per_cond)
    new_d = new_ml = None
    for layer in range(depth):
        mod = mods[layer]
        kind, j = layer % 3, layer // 3
        if kind == 0:
            x = _fnet_mix(x, seq, j, p["norm_mix"][layer], mod, tokens_per_cond, p["fnet_w"], p["fnet_b"][j])
        elif kind == 1:
            n_main = DN_HEADS * (2 * DN_DK + 2 * DN_DV)
            main, gt = _norm_proj(x, p["norm_mix"][layer], mod, tokens_per_cond, p["dn_w_in"], j, n_main)
            s0 = None if st_d is None else st_d[:, j]
            o, sfin = _gdn_scan(main, gt, batch, seq, p["dn_conv_w"][j], p["dn_a_log"][j], p["dn_dt_bias"][j],
                                s0, emit_state)
            if emit_state:
                new_d = sfin
            x = _gated_out(x, o, main, 3, p["dn_norm"][j], mod, tile_cond(OUT_TOKEN_TILE), p["dn_w_out"][j],
                           DN_HEADS, jax.nn.silu)
        else:
            n_main = 2 * ML_HEADS * ML_DQK + 2 * ML_HEADS * ML_DV
            k_tile = (ML_HEADS * ML_DQK) // PROJ_COL_TILE
            assert ML_HEADS * ML_DQK == PROJ_COL_TILE
            main, gt, kt = _norm_proj(x, p["norm_mix"][layer], mod, tokens_per_cond, p["ml_w_in"], j, n_main,
                                      t_tile=k_tile)
            state0 = None
            if st_ml is not None:
                c0, n0, m0 = (s[:, j] for s in st_ml)
                pad = jnp.zeros(c0.shape[:-1] + (ML_AUG - ML_DV - 1,), F32)
                c_aug0 = jnp.concatenate([c0, n0[..., None], pad], axis=-1)
                m0b = jnp.broadcast_to(m0[..., None, None], m0.shape + (1, 128))
                state0 = (c_aug0, m0b)
            o, fin = _mlstm_scan(main, kt, gt, batch, seq, p["ml_b_i"][j], p["ml_b_f"][j], state0, emit_state)
            if emit_state:
                new_ml = (fin[0], fin[1][..., 0], fin[2][..., 0, 0])
            x = _gated_out(x, o, main, 2, p["ml_norm"][j], mod, tile_cond(OUT_TOKEN_TILE), p["ml_w_out"][j],
                           ML_HEADS, jax.nn.sigmoid)
        x = _conv_ffn(x, seq, rows, layer, p["norm_ffn"][layer], mod, tile_cond(FFN_TOKEN_TILE), p["ffn_w_up"],
                      p["ffn_conv_w"], p["ffn_conv_b"], p["ffn_w_down"], p["norm_final"],
                      final_norm=(layer == depth - 1))
    return x, new_d, new_ml


def kernel(x_prompt, x_sample, state_delta, state_mlstm_c, state_mlstm_n, state_mlstm_m, c, c_ctx, w_ada, b_ada, norm_mix, norm_ffn, norm_final, ffn_w_up, ffn_conv_w, ffn_conv_b, ffn_w_down, fnet_w, fnet_b, dn_w_in, dn_conv_w, dn_a_log, dn_dt_bias, dn_norm, dn_w_out, ml_w_in, ml_b_i, ml_b_f, ml_norm, ml_w_out):
    p = dict(w_ada=w_ada, norm_mix=norm_mix, norm_ffn=norm_ffn, norm_final=norm_final, ffn_w_up=ffn_w_up,
             ffn_conv_w=ffn_conv_w, ffn_conv_b=ffn_conv_b, ffn_w_down=ffn_w_down, fnet_w=fnet_w, fnet_b=fnet_b,
             dn_w_in=dn_w_in, dn_conv_w=dn_conv_w, dn_a_log=dn_a_log, dn_dt_bias=dn_dt_bias, dn_norm=dn_norm,
             dn_w_out=dn_w_out, ml_w_in=ml_w_in, ml_b_i=ml_b_i, ml_b_f=ml_b_f, ml_norm=ml_norm, ml_w_out=ml_w_out)
    b_ctx, t_ctx, _ = x_prompt.shape
    b_smp, t_smp, _ = x_sample.shape
    depth = w_ada.shape[0]

    cond8 = jnp.concatenate([c_ctx[None, :], c, jnp.zeros((8 - 1 - b_smp, D_MODEL), F32)], axis=0)
    mods = _adaln_all(cond8, w_ada, b_ada)
    mods_ctx = [mods[l, 0:1].reshape(1, 1, -1) for l in range(depth)]
    mods_smp = [mods[l, 1:1 + b_smp].reshape(b_smp, 1, -1) for l in range(depth)]

    y_ctx, new_d, new_ml = _trunk(x_prompt.reshape(b_ctx * t_ctx, D_MODEL), b_ctx, t_ctx, 1, mods_ctx,
                                  b_ctx * t_ctx, None, None, True, p)
    y_smp, _, _ = _trunk(x_sample.reshape(b_smp * t_smp, D_MODEL), b_smp, t_smp, t_smp // GRID_W, mods_smp,
                         t_smp, state_delta, (state_mlstm_c, state_mlstm_n, state_mlstm_m), False, p)

    new_c, new_n, new_m = new_ml
    return (y_ctx.reshape(b_ctx, t_ctx, D_MODEL), y_smp.reshape(b_smp, t_smp, D_MODEL),
            new_d[:, None], new_c[:, None], new_n[:, None], new_m[:, None])
```

```python
import collections
import functools

import numpy as np
import jax
import jax.numpy as jnp
from jax import lax
from jax.experimental import pallas as pl
from jax.experimental.pallas import tpu as pltpu

F32 = jnp.float32
BF16 = jnp.bfloat16

D_MODEL = 1024
EPS = 1e-6
N_DIR = 2
CHUNK = 64
LOG_CHUNK = 6
GROUP = 256
CHUNKS_PER_GROUP = GROUP // CHUNK
GDN_GROUP = 128
GDN_GROUPS_PER_STEP = 4
GDN_HEADS_PER_PASS = 4
GDN_STEP_POSITIONS = 2048
FNET_GROUP_DIM = 256
DN_HEADS, DN_DK, DN_DV, DN_CONV = 8, 128, 128, 5
ML_HEADS, ML_DQK, ML_DV = 8, 64, 128
D_FF = 2816
GRID_W = 64

FNET_STEP_TOKENS = 1024
FFN_TOKEN_TILE = 1024
FFN_FF_TILE = 256
FFN_TILES_PER_STEP = 4
FFN_SUB_TILES = 4
PROJ_TOKEN_TILE = 2048
PROJ_COL_TILE = 512
PROJ_WIDE_COL_TILE = 1024
OUT_TOKEN_TILE = 512
ADA_COL_TILE = 1536
SCAN_STEP_POSITIONS = 1024
VMEM_LIMIT = 56 * 1024 * 1024


def _params(*sem):
    return pltpu.CompilerParams(dimension_semantics=sem, vmem_limit_bytes=VMEM_LIMIT)


def _dot(a, b):
    return jnp.dot(a.astype(BF16), b.astype(BF16), preferred_element_type=F32)


def _dot_nt(a, b):
    return lax.dot_general(a.astype(BF16), b.astype(BF16), (((1,), (1,)), ((), ())),
                           preferred_element_type=F32)


def _dot_tn(a, b):
    return lax.dot_general(a.astype(BF16), b.astype(BF16), (((0,), (0,)), ((), ())),
                           preferred_element_type=F32)


def _rms(x, g):
    return x * lax.rsqrt(jnp.mean(x * x, axis=-1, keepdims=True) + EPS) * g


def _norm_mod(x, g, sc, sh):
    return _rms(x, g) * (1.0 + sc) + sh


def _softplus(x):
    return jnp.maximum(x, 0.0) + jnp.log1p(jnp.exp(-jnp.abs(x)))


def _mod_spec(chunk, cond_of_tile):
    return pl.BlockSpec((1, 1, D_MODEL), lambda i, *_: (cond_of_tile(i), 0, chunk))


def _adaln_kernel(c_ref, w_ref, b_ref, o_ref):
    s = jax.nn.silu(c_ref[...])
    o_ref[0] = _dot(s, w_ref[0]) + b_ref[0]


def _adaln_all(cond8, w_ada, b_ada):
    depth = w_ada.shape[0]
    n_out = w_ada.shape[2]
    tn = ADA_COL_TILE
    return pl.pallas_call(
        _adaln_kernel,
        grid=(depth, n_out // tn),
        in_specs=[pl.BlockSpec((8, D_MODEL), lambda l, j: (0, 0)),
                  pl.BlockSpec((1, D_MODEL, tn), lambda l, j: (l, 0, j)),
                  pl.BlockSpec((1, 1, tn), lambda l, j: (l, 0, j))],
        out_specs=pl.BlockSpec((1, 8, tn), lambda l, j: (l, 0, j)),
        out_shape=jax.ShapeDtypeStruct((depth, 8, n_out), F32),
        compiler_params=_params("parallel", "parallel"),
        name="adaln",
    )(cond8, w_ada, b_ada.reshape(depth, 1, n_out))


def _norm_proj_kernel(x_ref, g_ref, sc_ref, sh_ref, w_ref, wg_ref, o_ref, gate_ref, *rest, t_tile):
    h_scr = rest[-1]

    @pl.when(pl.program_id(1) == 0)
    def _():
        h = _norm_mod(x_ref[...], g_ref[...], sc_ref[0], sh_ref[0]).astype(BF16)
        h_scr[...] = h
        gate_ref[...] = _dot_nt(wg_ref[...], h)

    y = _dot_nt(h_scr[...], w_ref[...])
    o_ref[...] = y.astype(o_ref.dtype)

    if t_tile is not None:
        t_ref = rest[0]

        @pl.when(pl.program_id(1) == t_tile)
        def _():
            for g in range(t_ref.shape[0]):
                t_ref[g] = y[g * GROUP:(g + 1) * GROUP, :].T


def _norm_proj(x, norm_g, mod, tokens_per_cond, w_stack, layer, n_main, t_tile=None):
    n_tok = x.shape[0]
    tm = min(PROJ_TOKEN_TILE, tokens_per_cond)
    tn = PROJ_COL_TILE if t_tile is not None else PROJ_WIDE_COL_TILE
    assert n_main % tn == 0
    cond_of_tile = lambda i: (i * tm) // tokens_per_cond
    w_t = jnp.swapaxes(w_stack, 1, 2)
    n_gate = w_t.shape[1] - n_main
    assert n_main % n_gate == 0 and n_gate % 8 == 0
    out_specs = [pl.BlockSpec((tm, tn), lambda i, j: (i, j)),
                 pl.BlockSpec((n_gate, tm), lambda i, j: (0, i))]
    out_shape = [jax.ShapeDtypeStruct((n_tok, n_main), BF16),
                 jax.ShapeDtypeStruct((n_gate, n_tok), F32)]
    if t_tile is not None:
        out_specs.append(pl.BlockSpec((tm // GROUP, tn, GROUP), lambda i, j: (i, 0, 0)))
        out_shape.append(jax.ShapeDtypeStruct((n_tok // GROUP, tn, GROUP), F32))
    return pl.pallas_call(
        functools.partial(_norm_proj_kernel, t_tile=t_tile),
        grid=(n_tok // tm, n_main // tn),
        in_specs=[pl.BlockSpec((tm, D_MODEL), lambda i, j: (i, 0)),
                  pl.BlockSpec((1, D_MODEL), lambda i, j: (0, 0)),
                  _mod_spec(1, cond_of_tile), _mod_spec(0, cond_of_tile),
                  pl.BlockSpec((None, tn, D_MODEL), lambda i, j: (layer, j, 0)),
                  pl.BlockSpec((None, n_gate, D_MODEL), lambda i, j: (layer, n_main // n_gate, 0))],
        out_specs=out_specs, out_shape=out_shape,
        scratch_shapes=[pltpu.VMEM((tm, D_MODEL), BF16)],
        compiler_params=_params("parallel", "arbitrary"),
        name="norm_proj",
    )(x, norm_g.reshape(1, D_MODEL), mod, mod, w_t, w_t)


def _gated_out_kernel(x_ref, o_ref, z_ref, ng_ref, g1_ref, w_ref, y_ref, wb_scr, *, n_heads, gate_fn):
    @pl.when(pl.program_id(0) == 0)
    def _():
        wb_scr[...] = w_ref[...].astype(BF16)

    dv = o_ref.shape[1] // n_heads
    parts = []
    for h in range(n_heads):
        sl = slice(h * dv, (h + 1) * dv)
        parts.append((_rms(o_ref[:, sl], ng_ref[...]) * gate_fn(z_ref[:, sl].astype(F32))).astype(BF16))
    hs = jnp.concatenate(parts, axis=1)
    y = jnp.dot(hs, wb_scr[...], preferred_element_type=F32)
    y_ref[...] = x_ref[...] + g1_ref[0] * y


def _gated_out(x, o, main, z_block, norm_g, mod, cond_of_tile, w_out, n_heads, gate_fn):
    n_tok = x.shape[0]
    tm = OUT_TOKEN_TILE
    width = o.shape[1]
    return pl.pallas_call(
        functools.partial(_gated_out_kernel, n_heads=n_heads, gate_fn=gate_fn),
        grid=(n_tok // tm,),
        in_specs=[pl.BlockSpec((tm, D_MODEL), lambda i: (i, 0)),
                  pl.BlockSpec((tm, width), lambda i: (i, 0)),
                  pl.BlockSpec((tm, width), lambda i: (i, z_block)),
                  pl.BlockSpec((1, width // n_heads), lambda i: (0, 0)),
                  _mod_spec(2, cond_of_tile),
                  pl.BlockSpec((width, D_MODEL), lambda i: (0, 0))],
        out_specs=pl.BlockSpec((tm, D_MODEL), lambda i: (i, 0)),
        out_shape=jax.ShapeDtypeStruct((n_tok, D_MODEL), F32),
        scratch_shapes=[pltpu.VMEM((width, D_MODEL), BF16)],
        compiler_params=_params("arbitrary"),
        name="gated_out",
    )(x, o, main, norm_g.reshape(1, -1), mod, w_out)


def _dft_mats(n):
    k = np.arange(n, dtype=np.int64)
    ang = 2.0 * np.pi * ((k[:, None] * k[None, :]) % n).astype(np.float64) / n
    s = 1.0 / np.sqrt(n)
    return np.cos(ang) * s, np.sin(ang) * s


def _fnet_kernel(x_ref, ng_ref, sc_ref, sh_ref, g1_ref, ct_ref, st_ref, cs_ref, w_ref, b_ref, y_ref, wb_scr):
    @pl.when(pl.program_id(0) == 0)
    def _():
        wb_scr[...] = w_ref[...].astype(BF16)

    seq = ct_ref.shape[0]
    seqs = [pl.ds(r, seq) for r in range(0, x_ref.shape[0], seq)]
    gd = FNET_GROUP_DIM
    hs = [_norm_mod(x_ref[s, :], ng_ref[...], sc_ref[0], sh_ref[0]).astype(BF16) for s in seqs]
    ps = [jnp.dot(ct_ref[...], h, preferred_element_type=F32) for h in hs]
    qs = [jnp.dot(st_ref[...], h, preferred_element_type=F32) for h in hs]
    fs = []
    for p, q in zip(ps, qs):
        parts = []
        for g in range(D_MODEL // gd):
            sl = slice(g * gd, (g + 1) * gd)
            pq = jnp.concatenate([p[:, sl], q[:, sl]], axis=1).astype(BF16)
            parts.append(jnp.dot(pq, cs_ref[...], preferred_element_type=F32).astype(BF16))
        fs.append(jnp.concatenate(parts, axis=1))
    ys = [jnp.dot(f, wb_scr[...], preferred_element_type=F32) + b_ref[...] for f in fs]
    for s, y in zip(seqs, ys):
        y_ref[s, :] = x_ref[s, :] + g1_ref[0] * y


def _fnet_mix(x, seq, layer, norm_g, mod, tokens_per_cond, w, b):
    n_tok = x.shape[0]
    tm = seq * max(1, min(FNET_STEP_TOKENS, tokens_per_cond) // seq)
    cond_of_tile = lambda i: (i * tm) // tokens_per_cond
    ct, st = _dft_mats(seq)
    cc, sc = _dft_mats(FNET_GROUP_DIM)
    cs = np.concatenate([cc, -sc], axis=0)
    const = lambda i: (0, 0)
    ct, st, cs = (jnp.asarray(m, F32).astype(BF16) for m in (ct, st, cs))
    return pl.pallas_call(
        _fnet_kernel,
        grid=(n_tok // tm,),
        in_specs=[pl.BlockSpec((tm, D_MODEL), lambda i: (i, 0)),
                  pl.BlockSpec((1, D_MODEL), const),
                  _mod_spec(1, cond_of_tile), _mod_spec(0, cond_of_tile), _mod_spec(2, cond_of_tile),
                  pl.BlockSpec((seq, seq), const),
                  pl.BlockSpec((seq, seq), const),
                  pl.BlockSpec((2 * FNET_GROUP_DIM, FNET_GROUP_DIM), const),
                  pl.BlockSpec((None, D_MODEL, D_MODEL), lambda i: (layer, 0, 0)),
                  pl.BlockSpec((1, D_MODEL), const)],
        out_specs=pl.BlockSpec((tm, D_MODEL), lambda i: (i, 0)),
        out_shape=jax.ShapeDtypeStruct((n_tok, D_MODEL), F32),
        scratch_shapes=[pltpu.VMEM((D_MODEL, D_MODEL), BF16)],
        compiler_params=_params("arbitrary"),
        name="fnet",
    )(x, norm_g.reshape(1, D_MODEL), mod, mod, mod,
      ct, st, cs, w, b.reshape(1, D_MODEL))


SUBLANES = 8


def _shift_tokens(x, delta, period):
    n, c = x.shape
    assert n % period == 0 and period % SUBLANES == 0 and 0 < abs(delta) < period
    blocks = range(0, n, period)
    if delta % SUBLANES == 0:
        zeros = jnp.zeros((abs(delta), c), x.dtype)
        if delta > 0:
            pieces = [piece for b in blocks for piece in (x[b + delta:b + period], zeros)]
        else:
            pieces = [piece for b in blocks for piece in (zeros, x[b:b + period + delta])]
        return jnp.concatenate(pieces, axis=0)
    assert abs(delta) < SUBLANES
    y = pltpu.roll(x, (-delta) % n, 0)
    r = lax.broadcasted_iota(jnp.int32, (SUBLANES, 1), 0) + delta
    keep = ((r >= 0) & (r <= SUBLANES - 1)).astype(x.dtype)
    if delta < 0:
        pieces = [piece for b in blocks for piece in (y[b:b + SUBLANES] * keep, y[b + SUBLANES:b + period])]
    else:
        pieces = [piece for b in blocks
                  for piece in (y[b:b + period - SUBLANES], y[b + period - SUBLANES:b + period] * keep)]
    return jnp.concatenate(pieces, axis=0)


def _dwconv_tokens(g, cw_ref, seq, rows):
    width = seq // rows
    g_cols = (_shift_tokens(g, -1, width), g, _shift_tokens(g, 1, width))
    out = None
    for di in (-1, 0, 1):
        if rows == 1 and di != 0:
            continue
        r = sum(g_cols[dj] * cw_ref[pl.ds(3 * (di + 1) + dj, 1), :] for dj in range(3))
        if di != 0:
            r = _shift_tokens(r, di * width, seq)
        out = r if out is None else out + r
    return out


def _ffn_kernel(x_ref, ng_ref, sc_ref, sh_ref, g2_ref, *rest, seq, rows, final_norm, n_ff_tiles):
    tiles = [rest[5 * t:5 * t + 5] for t in range(FFN_TILES_PER_STEP)]
    nf_ref, y_ref, h_scr, acc_scr = rest[5 * FFN_TILES_PER_STEP:]
    j = pl.program_id(1)
    last = pl.num_programs(1) - 1

    @pl.when(j == 0)
    def _():
        h_scr[...] = _norm_mod(x_ref[...], ng_ref[...], sc_ref[0], sh_ref[0]).astype(BF16)
        acc_scr[...] = jnp.zeros_like(acc_scr)

    tm = h_scr.shape[0]
    sub = tm // FFN_SUB_TILES
    parts = [pl.ds(r, sub) for r in range(0, tm, sub)]
    per_seq = max(seq // sub, 1)

    def run(tiles):
        gs = [[jnp.dot(h_scr[p, :], wg[...].astype(BF16), preferred_element_type=F32) for p in parts]
              for _, wg, _, _, _ in tiles]
        as_ = [[jnp.dot(h_scr[p, :], wa[...].astype(BF16), preferred_element_type=F32) for p in parts]
               for wa, _, _, _, _ in tiles]
        acts = []
        for (_, _, cw, cb, _), g_t, a_t in zip(tiles, gs, as_):
            convs = []
            for i in range(0, len(parts), per_seq):
                g = g_t[i] if per_seq == 1 else jnp.concatenate(g_t[i:i + per_seq], axis=0)
                c = _dwconv_tokens(g, cw, seq, rows) + cb[...]
                convs += [c[k * sub:(k + 1) * sub] for k in range(per_seq)]
            acts.append([(jax.nn.silu(c) * a).astype(BF16) for c, a in zip(convs, a_t)])
        wd = jnp.concatenate([t[4][...].astype(BF16) for t in tiles], axis=0)
        for k, p in enumerate(parts):
            act = jnp.concatenate([a_t[k] for a_t in acts], axis=1)
            acc_scr[p, :] += jnp.dot(act, wd, preferred_element_type=F32)

    n_tail = n_ff_tiles % FFN_TILES_PER_STEP
    if n_tail:
        pl.when(j < last)(lambda: run(tiles))
        pl.when(j == last)(lambda: run(tiles[:n_tail]))
    else:
        run(tiles)

    @pl.when(j == last)
    def _():
        y = x_ref[...] + g2_ref[0] * acc_scr[...]
        if final_norm:
            y = _rms(y, nf_ref[...])
        y_ref[...] = y


def _conv_ffn(x, seq, rows, layer, norm_g, mod, cond_of_tile, w_up, conv_w, conv_b, w_down, norm_final, final_norm):
    n_tok = x.shape[0]
    tm, tf = FFN_TOKEN_TILE, FFN_FF_TILE
    n_ff_tiles = D_FF // tf
    depth = w_up.shape[0]
    tps = FFN_TILES_PER_STEP
    conv_w, conv_b = conv_w.reshape(depth, 9, D_FF), conv_b.reshape(depth, 1, D_FF)
    tile_specs, tile_args = [], []
    for t in range(tps):
        tile = lambda j, t=t: jnp.minimum(j * tps + t, n_ff_tiles - 1)
        tile_specs += [pl.BlockSpec((None, D_MODEL, tf), lambda i, j, tile=tile: (layer, 0, tile(j))),
                       pl.BlockSpec((None, D_MODEL, tf), lambda i, j, tile=tile: (layer, 0, n_ff_tiles + tile(j))),
                       pl.BlockSpec((None, 9, tf), lambda i, j, tile=tile: (layer, 0, tile(j))),
                       pl.BlockSpec((None, 1, tf), lambda i, j, tile=tile: (layer, 0, tile(j))),
                       pl.BlockSpec((None, tf, D_MODEL), lambda i, j, tile=tile: (layer, tile(j), 0))]
        tile_args += [w_up, w_up, conv_w, conv_b, w_down]
    return pl.pallas_call(
        functools.partial(_ffn_kernel, seq=seq, rows=rows, final_norm=final_norm, n_ff_tiles=n_ff_tiles),
        grid=(n_tok // tm, pl.cdiv(n_ff_tiles, tps)),
        in_specs=[pl.BlockSpec((tm, D_MODEL), lambda i, j: (i, 0)),
                  pl.BlockSpec((1, D_MODEL), lambda i, j: (0, 0)),
                  _mod_spec(4, cond_of_tile), _mod_spec(3, cond_of_tile), _mod_spec(5, cond_of_tile),
                  *tile_specs,
                  pl.BlockSpec((1, D_MODEL), lambda i, j: (0, 0))],
        out_specs=pl.BlockSpec((tm, D_MODEL), lambda i, j: (i, 0)),
        out_shape=jax.ShapeDtypeStruct((n_tok, D_MODEL), F32),
        scratch_shapes=[pltpu.VMEM((tm, D_MODEL), BF16), pltpu.VMEM((tm, D_MODEL), F32)],
        compiler_params=_params("parallel", "arbitrary"),
        name="conv_ffn",
    )(x, norm_g.reshape(1, D_MODEL), mod, mod, mod, *tile_args, norm_final.reshape(1, D_MODEL))


GroupMasks = collections.namedtuple("GroupMasks", "r c same incl strict")


def _chunk_masks():
    r = lax.broadcasted_iota(jnp.int32, (CHUNK, CHUNK), 0)
    c = lax.broadcasted_iota(jnp.int32, (CHUNK, CHUNK), 1)
    return (r >= c, r <= c)


def _group_masks(group=GROUP):
    r = lax.broadcasted_iota(jnp.int32, (group, group), 0)
    c = lax.broadcasted_iota(jnp.int32, (group, group), 1)
    same = (r >> LOG_CHUNK) == (c >> LOG_CHUNK)
    return GroupMasks(r, c, same, (same & (r >= c), same & (r <= c)), (same & (r > c), same & (r < c)))


def _gate_layouts(gt, batch, n_heads, hp, group=GROUP):
    seq = gt.shape[1] // batch
    n_groups, n_chunks = seq // group, seq // CHUNK
    assert seq % group == 0 and n_groups <= 8
    g = jnp.transpose(gt.reshape(N_DIR, 2, n_heads, batch, seq), (3, 2, 0, 1, 4))
    rows = g.reshape(g.shape[:4] + (n_groups, group))
    rows = jnp.pad(rows, ((0, 0),) * 4 + ((0, 8 - n_groups), (0, 0)))
    cols = jnp.swapaxes(g.reshape(batch, n_heads // hp, hp * N_DIR * 2 * n_chunks, CHUNK), -1, -2)
    return rows, cols


def _col_gate(gcol_ref, hh, d, kind, n_chunks):
    start = ((hh * N_DIR + d) * 2 + kind) * n_chunks
    return gcol_ref[0, 0, :, start:start + n_chunks]


def _split3(x):
    x1 = x.astype(BF16)
    r1 = x - x1.astype(F32)
    x2 = r1.astype(BF16)
    return x1, x2, (r1 - x2.astype(F32)).astype(BF16)


def _dot_mask_rhs(x, mask):
    m = x.shape[0]
    y = jnp.dot(jnp.concatenate(_split3(x), axis=0), jnp.where(mask, 1.0, 0.0).astype(BF16),
                preferred_element_type=F32)
    return y[:m] + y[m:2 * m] + y[2 * m:]


def _dot_mask_lhs(mask, x):
    mb = jnp.where(mask, 1.0, 0.0).astype(BF16)
    y1, y2, y3 = (jnp.dot(mb, p, preferred_element_type=F32) for p in _split3(x))
    return y1 + y2 + y3


def _cumsum_rows(row, d, gm):
    return _dot_mask_rhs(row, gm.incl[1 - d])


def _cumsum_cols(col, d, cm):
    return _dot_mask_lhs(cm[d], col)


def _store_cols(dst, col, n_chunks):
    for n in range(n_chunks):
        dst[n * CHUNK:(n + 1) * CHUNK, :] = jnp.broadcast_to(col[:, n:n + 1], (CHUNK, 128))


def _wide(x, width=None):
    reps = (GROUP if width is None else width) // x.shape[1]
    return x if reps == 1 else jnp.concatenate([x] * reps, axis=1)


def _rows(i, size):
    if isinstance(i, int):
        return pl.ds(i * size, size)
    return pl.ds(pl.multiple_of(i * size, size), size)


def _loop(n, body, init):
    if n == 1:
        return body(0, init)
    return lax.fori_loop(0, n, body, init)


def _conv_silu(x, w_ref):
    n = x.shape[0]
    k = w_ref.shape[0]
    acc = None
    for j in range(k):
        delta = j - k // 2
        term = (x if delta == 0 else _shift_tokens(x, delta, n)) * w_ref[pl.ds(j, 1), :]
        acc = term if acc is None else acc + term
    return jax.nn.silu(acc)


def _l2norm(x):
    return x * lax.rsqrt(jnp.sum(x * x, axis=-1, keepdims=True) + EPS)


def _blockdiag_tri_inverse(mats, gm):
    dot16 = lambda x, y: jnp.dot(x, y, preferred_element_type=F32).astype(BF16)
    base = (gm.r >> 3) == (gm.c >> 3)
    eye = jnp.where(gm.r == gm.c, 1.0, 0.0)
    zero16 = jnp.zeros((), BF16)
    a16s = [a.astype(BF16) for a in mats]
    invs = [eye - jnp.where(base, a, 0.0) for a in mats]
    ps = [jnp.where(base, -a16, zero16) for a16 in a16s]
    for _ in range(2):
        ps = [dot16(p, p) for p in ps]
        invs = [inv + jnp.dot(inv.astype(BF16), p, preferred_element_type=F32) for inv, p in zip(invs, ps)]
    for s in (3, 4, 5):
        join = ((gm.r >> s) ^ (gm.c >> s)) == 1
        inv16s = [inv.astype(BF16) for inv in invs]
        tmp = [dot16(jnp.where(join, a16, zero16), inv16) for a16, inv16 in zip(a16s, inv16s)]
        invs = [inv - jnp.dot(inv16, t, preferred_element_type=F32) for inv, inv16, t in zip(invs, inv16s, tmp)]
    return invs


def _gdn_kernel(*refs, seq, hp, zero_init, emit_state):
    alog_ref, dtb_ref, q_ref, k_ref, v_ref, cwq_ref, cwk_ref, cwv_ref, grow_ref, gcol_ref = refs[:10]
    pos = 10
    s0_ref = None
    if not zero_init:
        s0_ref = refs[pos]
        pos += 1
    o_ref = refs[pos]
    pos += 1
    sfin_ref = None
    if emit_state:
        sfin_ref = refs[pos]
        pos += 1
    (q_scr, k_scr, v_scr, gcr_scr, gcb_scr, bb_scr, gl_scr,
     u_scr, w_scr, qd_scr, kd_scr, qkd_scr) = refs[pos:]

    n_chunks = seq // CHUNK
    grp = GDN_GROUP
    cm = _chunk_masks()
    gm = _group_masks(grp)
    head_dirs = [(hh, d) for hh in range(hp) for d in range(N_DIR)]
    lanes = lambda hh: slice(hh * DN_DK, (hh + 1) * DN_DK)

    q_scr[...] = _conv_silu(q_ref[...].astype(F32), cwq_ref)
    k_scr[...] = _conv_silu(k_ref[...].astype(F32), cwk_ref)
    v_scr[...] = _conv_silu(v_ref[...].astype(F32), cwv_ref)
    for hh in range(hp):
        q_scr[:, lanes(hh)] = _l2norm(q_scr[:, lanes(hh)]) * (DN_DK ** -0.5)
        k_scr[:, lanes(hh)] = _l2norm(k_scr[:, lanes(hh)])

    for hh, d in head_dirs:
        head = pl.program_id(1) * hp + hh
        neg_a = -jnp.exp(jnp.full((1, 1), alog_ref[d, head], F32))
        dtb = dtb_ref[d, head]
        gcr_scr[hh, d] = _cumsum_rows(neg_a * _softplus(grow_ref[0, hh, d, 0] + dtb), d, gm)
        gc_col = _cumsum_cols(neg_a * _softplus(_col_gate(gcol_ref, hh, d, 0, n_chunks) + dtb), d, cm)
        _store_cols(gcb_scr.at[hh, d], gc_col, n_chunks)
        _store_cols(bb_scr.at[hh, d], jax.nn.sigmoid(_col_gate(gcol_ref, hh, d, 1, n_chunks)), n_chunks)
        last = CHUNK - 1 if d == 0 else 0
        _store_cols(gl_scr.at[hh, d], jnp.broadcast_to(gc_col[last:last + 1, :], gc_col.shape), n_chunks)

    gps = min(GDN_GROUPS_PER_STEP, seq // grp)

    def group_step(it, carry):
        gis = [it * gps + k for k in range(gps)]
        rows = [_rows(gi, grp) for gi in gis]
        for h0 in range(0, hp, GDN_HEADS_PER_PASS):
            heads = range(h0, min(h0 + GDN_HEADS_PER_PASS, hp))
            kqs = {}
            for k, r in enumerate(rows):
                for hh in heads:
                    kg, qg = k_scr[r, lanes(hh)], q_scr[r, lanes(hh)]
                    kqs[k, hh] = _dot_nt(jnp.concatenate([kg, qg], axis=0), kg)
            chains = [(k, hh, d) for k in range(len(gis)) for hh in heads for d in range(N_DIR)]
            mats = []
            for k, hh, d in chains:
                gcb = gcb_scr[hh, d, rows[k], :]
                gcr = gcr_scr[hh, d, pl.ds(gis[k], 1), :]
                decay = jnp.exp(jnp.where(gm.incl[d], _wide(gcb, grp) - gcr, -jnp.inf))
                mats.append(jnp.where(gm.strict[d],
                                      kqs[k, hh][:grp] * _wide(bb_scr[hh, d, rows[k], :], grp) * decay, 0.0))
                qkd = kqs[k, hh][grp:] * decay
                qkd_scr[hh, d, rows[k], :] = sum(qkd[:, n * CHUNK:(n + 1) * CHUNK] for n in range(grp // CHUNK))
            t_invs = _blockdiag_tri_inverse(mats, gm)
            for (k, hh, d), t_inv in zip(chains, t_invs):
                r = rows[k]
                qg, kg, vg = q_scr[r, lanes(hh)], k_scr[r, lanes(hh)], v_scr[r, lanes(hh)]
                gcb = gcb_scr[hh, d, r, :]
                bb = bb_scr[hh, d, r, :]
                e_gc = jnp.exp(gcb)
                uw = _dot(t_inv, jnp.concatenate([vg * bb, kg * bb * e_gc], axis=1))
                u_scr[hh, d, r, :] = uw[:, :DN_DV]
                w_scr[hh, d, r, :] = uw[:, DN_DV:]
                qd_scr[hh, d, r, :] = qg * e_gc
                kd_scr[hh, d, r, :] = kg * jnp.exp(gl_scr[hh, d, r, :] - gcb)
        return carry

    _loop(seq // (grp * gps), group_step, 0)

    o_ref[...] = jnp.zeros_like(o_ref)

    def chunk_step(n, carry):
        idxs = [n if d == 0 else n_chunks - 1 - n for _, d in head_dirs]
        rows = [_rows(idx, CHUNK) for idx in idxs]
        wqs = [_dot(jnp.concatenate([w_scr[hh, d, r, :], qd_scr[hh, d, r, :]], axis=0), s)
               for (hh, d), r, s in zip(head_dirs, rows, carry)]
        v_news = [u_scr[hh, d, r, :] - wq[:CHUNK] for (hh, d), r, wq in zip(head_dirs, rows, wqs)]
        o_ns = [wq[CHUNK:] + _dot(qkd_scr[hh, d, r, :], v_new)
                for (hh, d), r, wq, v_new in zip(head_dirs, rows, wqs, v_news)]
        new = []
        for (hh, d), idx, r, s, v_new in zip(head_dirs, idxs, rows, carry, v_news):
            s_decay = jnp.exp(gl_scr[hh, d, pl.ds(idx * CHUNK, 1), :])
            new.append(s * s_decay + _dot_tn(kd_scr[hh, d, r, :], v_new))
        for (hh, d), r, o_n in zip(head_dirs, rows, o_ns):
            o_ref[r, lanes(hh)] += o_n
        return tuple(new)

    if zero_init:
        init = (jnp.zeros((DN_DK, DN_DV), F32),) * len(head_dirs)
    else:
        init = tuple(s0_ref[0, d, hh] for hh, d in head_dirs)
    fin = lax.fori_loop(0, n_chunks, chunk_step, init)
    if emit_state:
        for i, (hh, d) in enumerate(head_dirs):
            sfin_ref[0, d, hh] = fin[i]


def _gdn_scan(main, gt, batch, seq, conv_w, a_log, dt_bias, s0, emit_state):
    n_tok = batch * seq
    nh = DN_HEADS
    hp = min(nh, max(2, GDN_STEP_POSITIONS // seq))
    n_hb = nh // hp
    assert seq % (GDN_GROUP * min(GDN_GROUPS_PER_STEP, seq // GDN_GROUP)) == 0
    grow, gcol = _gate_layouts(gt, batch, nh, hp, GDN_GROUP)
    zero_init = s0 is None
    smem = pl.BlockSpec(memory_space=pltpu.SMEM)
    qkv_spec = lambda off: pl.BlockSpec((seq, hp * DN_DK), lambda b, h: (b, off + h))
    cw_spec = lambda off: pl.BlockSpec((DN_CONV, hp * DN_DK), lambda b, h: (0, off + h))
    state_spec = pl.BlockSpec((1, N_DIR, hp, DN_DK, DN_DV), lambda b, h: (b, 0, h, 0, 0))
    in_specs = [smem, smem, qkv_spec(0), qkv_spec(n_hb), qkv_spec(2 * n_hb),
                cw_spec(0), cw_spec(n_hb), cw_spec(2 * n_hb),
                pl.BlockSpec((1, hp) + grow.shape[2:], lambda b, h: (b, h, 0, 0, 0, 0)),
                pl.BlockSpec((1, 1) + gcol.shape[2:], lambda b, h: (b, h, 0, 0))]
    args = [a_log, dt_bias, main, main, main, conv_w, conv_w, conv_w, grow, gcol]
    if not zero_init:
        in_specs.append(state_spec)
        args.append(s0)
    out_specs = [pl.BlockSpec((seq, hp * DN_DV), lambda b, h: (b, h))]
    out_shape = [jax.ShapeDtypeStruct((n_tok, nh * DN_DV), F32)]
    if emit_state:
        out_specs.append(state_spec)
        out_shape.append(jax.ShapeDtypeStruct((batch, N_DIR, nh, DN_DK, DN_DV), F32))
    per_dir = lambda width: pltpu.VMEM((hp, N_DIR, seq, width), F32)
    qkv_scr = pltpu.VMEM((seq, hp * DN_DK), F32)
    outs = pl.pallas_call(
        functools.partial(_gdn_kernel, seq=seq, hp=hp, zero_init=zero_init, emit_state=emit_state),
        grid=(batch, n_hb),
        in_specs=in_specs, out_specs=out_specs, out_shape=out_shape,
        scratch_shapes=[qkv_scr, qkv_scr, qkv_scr,
                        pltpu.VMEM((hp, N_DIR, 8, GDN_GROUP), F32),
                        per_dir(128), per_dir(128), per_dir(128),
                        per_dir(DN_DV), per_dir(DN_DK), per_dir(DN_DK), per_dir(DN_DK), per_dir(CHUNK)],
        compiler_params=_params("parallel", "parallel"),
        name="gdn_scan",
    )(*args)
    return outs[0], (outs[1] if emit_state else None)


ML_AUG = 2 * ML_DV


def _mlstm_kernel(*refs, seq, hp, zero_init, emit_state):
    bi_ref, bf_ref, q_ref, k_ref, v_ref, kt_ref, grow_ref, gcol_ref = refs[:8]
    pos = 8
    c0_ref = m0_ref = None
    if not zero_init:
        c0_ref, m0_ref = refs[pos:pos + 2]
        pos += 2
    o_ref = refs[pos]
    pos += 1
    cfin_ref = nfin_ref = mfin_ref = None
    if emit_state:
        cfin_ref, nfin_ref, mfin_ref = refs[pos:pos + 3]
        pos += 3
    bcr_scr, lir_scr, kwr_scr, bcb_scr, mi_scr, dec_scr, dc_scr, cin_scr = refs[pos:]

    n_chunks = seq // CHUNK
    n_groups = seq // GROUP
    head0 = pl.program_id(1) * hp
    cm = _chunk_masks()
    gm = _group_masks()
    ones_col = jnp.where(lax.broadcasted_iota(jnp.int32, (GROUP, ML_DV), 1) == 0, 1.0, 0.0)
    head_dirs = [(hh, d) for hh in range(hp) for d in range(N_DIR)]

    m_fin = {}
    row_group = lax.broadcasted_iota(jnp.int32, (8, GROUP), 0)
    row_chunk = lax.broadcasted_iota(jnp.int32, (8, GROUP), 1) >> LOG_CHUNK
    for hh, d in head_dirs:
        b_i = bi_ref[d, head0 + hh]
        b_f = bf_ref[d, head0 + hh]
        li_row = grow_ref[0, hh, d, 0] + b_i
        lf_row = -_softplus(-(grow_ref[0, hh, d, 1] + b_f))
        bc_row = _cumsum_rows(lf_row, d, gm)
        lir_scr[hh, d] = li_row
        bcr_scr[hh, d] = bc_row
        w_row = _dot_mask_rhs(lf_row, gm.same) - bc_row + li_row
        m_out_row = jnp.zeros((8, GROUP), F32)
        li_col = _col_gate(gcol_ref, hh, d, 0, n_chunks) + b_i
        bc_col = _cumsum_cols(-_softplus(-(_col_gate(gcol_ref, hh, d, 1, n_chunks) + b_f)), d, cm)
        last = CHUNK - 1 if d == 0 else 0
        b_last = bc_col[last:last + 1, :]
        w_col = b_last - bc_col + li_col
        w_max = jnp.max(w_col, axis=0, keepdims=True)
        m = jnp.zeros((1, 1), F32) if zero_init else m0_ref[0, d, hh][:, 0:1]
        for step in range(n_chunks):
            n = step if d == 0 else n_chunks - 1 - step
            sl = slice(n * CHUNK, (n + 1) * CHUNK)
            m_new = jnp.maximum(b_last[:, n:n + 1] + m, w_max[:, n:n + 1])
            bcb_scr[hh, d, sl, :] = jnp.broadcast_to(bc_col[:, n:n + 1], (CHUNK, 128))
            mi_scr[hh, d, sl, :] = jnp.broadcast_to(m, (CHUNK, 128))
            dec_scr[hh, d, n:n + 1, :] = jnp.broadcast_to(jnp.exp(b_last[:, n:n + 1] + m - m_new), (1, 128))
            in_chunk = (row_group == n // CHUNKS_PER_GROUP) & (row_chunk == n % CHUNKS_PER_GROUP)
            m_out_row = jnp.where(in_chunk, m_new, m_out_row)
            m = m_new
        m_fin[hh, d] = m
        kwr_scr[hh, d] = jnp.exp(w_row - m_out_row)

    def load_kv(rows, hh):
        kg = k_ref[rows, hh * ML_DQK:(hh + 1) * ML_DQK].astype(F32)
        v_aug = jnp.concatenate([v_ref[rows, hh * ML_DV:(hh + 1) * ML_DV].astype(F32), ones_col], axis=1)
        return kg, v_aug

    def delta_step(gi, carry):
        rows = _rows(gi, GROUP)
        v_augs = [load_kv(rows, hh)[1] for hh in range(hp)]
        kt = kt_ref[gi]
        lhs = []
        for hh, d in head_dirs:
            kw_t = kt[hh * ML_DQK:(hh + 1) * ML_DQK, :] * kwr_scr[hh, d, pl.ds(gi, 1), :]
            lhs.append(jnp.where(gm.same, jnp.concatenate([kw_t] * CHUNKS_PER_GROUP, axis=0), 0.0))
        for (hh, d), kw in zip(head_dirs, lhs):
            dc_scr[hh, d, rows, :] = _dot(kw, v_augs[hh])
        return carry

    _loop(n_groups, delta_step, 0)

    def prefix_step(n, carry):
        new = []
        for (hh, d), c_aug in zip(head_dirs, carry):
            idx = n if d == 0 else n_chunks - 1 - n
            rows = _rows(idx, CHUNK)
            cin_scr[hh, d, rows, :] = c_aug
            new.append(c_aug * _wide(dec_scr[hh, d, pl.ds(idx, 1), :], ML_AUG) + dc_scr[hh, d, rows, :])
        return tuple(new)

    if zero_init:
        init = (jnp.zeros((ML_DQK, ML_AUG), F32),) * len(head_dirs)
    else:
        init = tuple(c0_ref[0, d, hh] for hh, d in head_dirs)
    fin = lax.fori_loop(0, n_chunks, prefix_step, init)

    def out_step(gi, carry):
        rows = _rows(gi, GROUP)
        kvs = [load_kv(rows, hh) for hh in range(hp)]
        qgs = [q_ref[rows, hh * ML_DQK:(hh + 1) * ML_DQK].astype(F32) * (ML_DQK ** -0.5) for hh in range(hp)]
        qks = [_dot_nt(qg, kv[0]) for qg, kv in zip(qgs, kvs)]
        chunk = lambda n: slice(n * CHUNK, (n + 1) * CHUNK)
        inters = []
        for hh, d in head_dirs:
            c_in = cin_scr[hh, d, rows, :]
            inters.append(jnp.concatenate(
                [_dot(qgs[hh][chunk(n)], c_in[chunk(n)]) for n in range(CHUNKS_PER_GROUP)], axis=0))
        m_ts, b_ms, ss = [], [], []
        for hh, d in head_dirs:
            b_colb = bcb_scr[hh, d, rows, :]
            b_m = b_colb[:, 0:1] + mi_scr[hh, d, rows, :][:, 0:1]
            d_log = jnp.where(gm.incl[d], _wide(b_colb) - bcr_scr[hh, d, pl.ds(gi, 1), :]
                              + lir_scr[hh, d, pl.ds(gi, 1), :], -jnp.inf)
            m_t = jnp.maximum(b_m, jnp.max(d_log, axis=-1, keepdims=True))
            m_ts.append(m_t)
            b_ms.append(b_m)
            ss.append(qks[hh] * jnp.exp(d_log - m_t))
        intras = [_dot(s, kvs[hh][1]) for (hh, d), s in zip(head_dirs, ss)]
        hs = []
        for inter, intra, m_t, b_m in zip(inters, intras, m_ts, b_ms):
            num = jnp.exp(b_m - m_t) * inter + intra
            den = num[:, ML_DV:ML_DV + 1]
            hs.append(num[:, :ML_DV] / jnp.maximum(jnp.abs(den), jnp.exp(-m_t)))
        for hh in range(hp):
            o_ref[rows, hh * ML_DV:(hh + 1) * ML_DV] = hs[N_DIR * hh] + hs[N_DIR * hh + 1]
        return carry

    _loop(n_groups, out_step, 0)

    if emit_state:
        for i, (hh, d) in enumerate(head_dirs):
            cfin_ref[0, d, hh] = fin[i][:, :ML_DV]
            nfin_ref[0, d, hh] = fin[i][:, ML_DV:]
            mfin_ref[0, d, hh] = jnp.broadcast_to(m_fin[hh, d], (1, 128))


def _mlstm_scan(main, kt, gt, batch, seq, b_i, b_f, state0, emit_state):
    n_tok = batch * seq
    n_groups = seq // GROUP
    n_chunks = seq // CHUNK
    nh = ML_HEADS
    hp = min(nh, max(2, SCAN_STEP_POSITIONS // seq))
    n_hb = nh // hp
    grow, gcol = _gate_layouts(gt, batch, nh, hp)
    zero_init = state0 is None
    smem = pl.BlockSpec(memory_space=pltpu.SMEM)
    k_off = (nh * ML_DQK) // (hp * ML_DQK)
    v_off = (2 * nh * ML_DQK) // (hp * ML_DV)
    c_spec = pl.BlockSpec((1, N_DIR, hp, ML_DQK, ML_AUG), lambda b, p: (b, 0, p, 0, 0))
    m_spec = pl.BlockSpec((1, N_DIR, hp, 1, 128), lambda b, p: (b, 0, p, 0, 0))
    in_specs = [smem, smem,
                pl.BlockSpec((seq, hp * ML_DQK), lambda b, p: (b, p)),
                pl.BlockSpec((seq, hp * ML_DQK), lambda b, p: (b, k_off + p)),
                pl.BlockSpec((seq, hp * ML_DV), lambda b, p: (b, v_off + p)),
                pl.BlockSpec((n_groups, hp * ML_DQK, GROUP), lambda b, p: (b, p, 0)),
                pl.BlockSpec((1, hp) + grow.shape[2:], lambda b, p: (b, p, 0, 0, 0, 0)),
                pl.BlockSpec((1, 1) + gcol.shape[2:], lambda b, p: (b, p, 0, 0))]
    args = [b_i, b_f, main, main, main, kt, grow, gcol]
    if not zero_init:
        in_specs += [c_spec, m_spec]
        args += list(state0)
    out_specs = [pl.BlockSpec((seq, hp * ML_DV), lambda b, p: (b, p))]
    out_shape = [jax.ShapeDtypeStruct((n_tok, nh * ML_DV), F32)]
    if emit_state:
        half_spec = pl.BlockSpec((1, N_DIR, hp, ML_DQK, ML_DV), lambda b, p: (b, 0, p, 0, 0))
        out_specs += [half_spec, half_spec, m_spec]
        out_shape += [jax.ShapeDtypeStruct((batch, N_DIR, nh, ML_DQK, ML_DV), F32),
                      jax.ShapeDtypeStruct((batch, N_DIR, nh, ML_DQK, ML_AUG - ML_DV), F32),
                      jax.ShapeDtypeStruct((batch, N_DIR, nh, 1, 128), F32)]
    per_hd = lambda rows, width: pltpu.VMEM((hp, N_DIR, rows, width), F32)
    outs = pl.pallas_call(
        functools.partial(_mlstm_kernel, seq=seq, hp=hp, zero_init=zero_init, emit_state=emit_state),
        grid=(batch, n_hb),
        in_specs=in_specs, out_specs=out_specs, out_shape=out_shape,
        scratch_shapes=[per_hd(8, GROUP), per_hd(8, GROUP), per_hd(8, GROUP),
                        per_hd(seq, 128), per_hd(seq, 128),
                        per_hd(max(n_chunks, 8), 128),
                        per_hd(seq, ML_AUG), per_hd(seq, ML_AUG)],
        compiler_params=_params("parallel", "parallel"),
        name="mlstm_scan",
    )(*args)
    return outs[0], (tuple(outs[1:4]) if emit_state else None)


def _trunk(x, batch, seq, rows, mods, tokens_per_cond, st_d, st_ml, emit_state, p):
    depth = p["w_ada"].shape[0]
    tile_cond = lambda tile: (lambda i: (i * tile) // tokens_per_cond)
    new_d = new_ml = None
    for layer in range(depth):
        mod = mods[layer]
        kind, j = layer % 3, layer // 3
        if kind == 0:
            x = _fnet_mix(x, seq, j, p["norm_mix"][layer], mod, tokens_per_cond, p["fnet_w"], p["fnet_b"][j])
        elif kind == 1:
            n_main = DN_HEADS * (2 * DN_DK + 2 * DN_DV)
            main, gt = _norm_proj(x, p["norm_mix"][layer], mod, tokens_per_cond, p["dn_w_in"], j, n_main)
            s0 = None if st_d is None else st_d[:, j]
            o, sfin = _gdn_scan(main, gt, batch, seq, p["dn_conv_w"][j], p["dn_a_log"][j], p["dn_dt_bias"][j],
                                s0, emit_state)
            if emit_state:
                new_d = sfin
            x = _gated_out(x, o, main, 3, p["dn_norm"][j], mod, tile_cond(OUT_TOKEN_TILE), p["dn_w_out"][j],
                           DN_HEADS, jax.nn.silu)
        else:
            n_main = 2 * ML_HEADS * ML_DQK + 2 * ML_HEADS * ML_DV
            k_tile = (ML_HEADS * ML_DQK) // PROJ_COL_TILE
            assert ML_HEADS * ML_DQK == PROJ_COL_TILE
            main, gt, kt = _norm_proj(x, p["norm_mix"][layer], mod, tokens_per_cond, p["ml_w_in"], j, n_main,
                                      t_tile=k_tile)
            state0 = None
            if st_ml is not None:
                c0, n0, m0 = (s[:, j] for s in st_ml)
                pad = jnp.zeros(c0.shape[:-1] + (ML_AUG - ML_DV - 1,), F32)
                c_aug0 = jnp.concatenate([c0, n0[..., None], pad], axis=-1)
                m0b = jnp.broadcast_to(m0[..., None, None], m0.shape + (1, 128))
                state0 = (c_aug0, m0b)
            o, fin = _mlstm_scan(main, kt, gt, batch, seq, p["ml_b_i"][j], p["ml_b_f"][j], state0, emit_state)
            if emit_state:
                new_ml = (fin[0], fin[1][..., 0], fin[2][..., 0, 0])
            x = _gated_out(x, o, main, 2, p["ml_norm"][j], mod, tile_cond(OUT_TOKEN_TILE), p["ml_w_out"][j],
                           ML_HEADS, jax.nn.sigmoid)
        x = _conv_ffn(x, seq, rows, layer, p["norm_ffn"][layer], mod, tile_cond(FFN_TOKEN_TILE), p["ffn_w_up"],
                      p["ffn_conv_w"], p["ffn_conv_b"], p["ffn_w_down"], p["norm_final"],
                      final_norm=(layer == depth - 1))
    return x, new_d, new_ml


def kernel(x_prompt, x_sample, state_delta, state_mlstm_c, state_mlstm_n, state_mlstm_m, c, c_ctx, w_ada, b_ada, norm_mix, norm_ffn, norm_final, ffn_w_up, ffn_conv_w, ffn_conv_b, ffn_w_down, fnet_w, fnet_b, dn_w_in, dn_conv_w, dn_a_log, dn_dt_bias, dn_norm, dn_w_out, ml_w_in, ml_b_i, ml_b_f, ml_norm, ml_w_out):
    p = dict(w_ada=w_ada, norm_mix=norm_mix, norm_ffn=norm_ffn, norm_final=norm_final, ffn_w_up=ffn_w_up,
             ffn_conv_w=ffn_conv_w, ffn_conv_b=ffn_conv_b, ffn_w_down=ffn_w_down, fnet_w=fnet_w, fnet_b=fnet_b,
             dn_w_in=dn_w_in, dn_conv_w=dn_conv_w, dn_a_log=dn_a_log, dn_dt_bias=dn_dt_bias, dn_norm=dn_norm,
             dn_w_out=dn_w_out, ml_w_in=ml_w_in, ml_b_i=ml_b_i, ml_b_f=ml_b_f, ml_norm=ml_norm, ml_w_out=ml_w_out)
    b_ctx, t_ctx, _ = x_prompt.shape
    b_smp, t_smp, _ = x_sample.shape
    depth = w_ada.shape[0]

    cond8 = jnp.concatenate([c_ctx[None, :], c, jnp.zeros((8 - 1 - b_smp, D_MODEL), F32)], axis=0)
    mods = _adaln_all(cond8, w_ada, b_ada)
    mods_ctx = [mods[l, 0:1].reshape(1, 1, -1) for l in range(depth)]
    mods_smp = [mods[l, 1:1 + b_smp].reshape(b_smp, 1, -1) for l in range(depth)]

    y_ctx, new_d, new_ml = _trunk(x_prompt.reshape(b_ctx * t_ctx, D_MODEL), b_ctx, t_ctx, 1, mods_ctx,
                                  b_ctx * t_ctx, None, None, True, p)
    y_smp, _, _ = _trunk(x_sample.reshape(b_smp * t_smp, D_MODEL), b_smp, t_smp, t_smp // GRID_W, mods_smp,
                         t_smp, state_delta, (state_mlstm_c, state_mlstm_n, state_mlstm_m), False, p)

    new_c, new_n, new_m = new_ml
    return (y_ctx.reshape(b_ctx, t_ctx, D_MODEL), y_smp.reshape(b_smp, t_smp, D_MODEL),
            new_d[:, None], new_c[:, None], new_n[:, None], new_m[:, None])
```

```python
import collections
import functools

import numpy as np
import jax
import jax.numpy as jnp
from jax import lax
from jax.experimental import pallas as pl
from jax.experimental.pallas import tpu as pltpu

F32 = jnp.float32
BF16 = jnp.bfloat16

D_MODEL = 1024
EPS = 1e-6
N_DIR = 2
CHUNK = 64
LOG_CHUNK = 6
GROUP = 256
CHUNKS_PER_GROUP = GROUP // CHUNK
GDN_GROUP = 128
GDN_GROUPS_PER_STEP = 4
GDN_HEADS_PER_PASS = 4
GDN_STEP_POSITIONS = 2048
FNET_GROUP_DIM = 256
DN_HEADS, DN_DK, DN_DV, DN_CONV = 8, 128, 128, 5
ML_HEADS, ML_DQK, ML_DV = 8, 64, 128
D_FF = 2816
GRID_W = 64

FNET_STEP_TOKENS = 1024
FFN_TOKEN_TILE = 1024
FFN_FF_TILE = 256
FFN_TILES_PER_STEP = 4
FFN_SUB_TILES = 4
PROJ_TOKEN_TILE = 2048
PROJ_COL_TILE = 512
PROJ_WIDE_COL_TILE = 1024
OUT_TOKEN_TILE = 512
ADA_COL_TILE = 1536
SCAN_STEP_POSITIONS = 1024
VMEM_LIMIT = 56 * 1024 * 1024


def _params(*sem):
    return pltpu.CompilerParams(dimension_semantics=sem, vmem_limit_bytes=VMEM_LIMIT)


def _dot(a, b):
    return jnp.dot(a.astype(BF16), b.astype(BF16), preferred_element_type=F32)


def _dot_nt(a, b):
    return lax.dot_general(a.astype(BF16), b.astype(BF16), (((1,), (1,)), ((), ())),
                           preferred_element_type=F32)


def _dot_tn(a, b):
    return lax.dot_general(a.astype(BF16), b.astype(BF16), (((0,), (0,)), ((), ())),
                           preferred_element_type=F32)


def _rms(x, g):
    return x * lax.rsqrt(jnp.mean(x * x, axis=-1, keepdims=True) + EPS) * g


def _norm_mod(x, g, sc, sh):
    return _rms(x, g) * (1.0 + sc) + sh


def _softplus(x):
    return jnp.maximum(x, 0.0) + jnp.log1p(jnp.exp(-jnp.abs(x)))


def _mod_spec(chunk, cond_of_tile):
    return pl.BlockSpec((1, 1, D_MODEL), lambda i, *_: (cond_of_tile(i), 0, chunk))


def _adaln_kernel(c_ref, w_ref, b_ref, o_ref):
    s = jax.nn.silu(c_ref[...])
    o_ref[0] = _dot(s, w_ref[0]) + b_ref[0]


def _adaln_all(cond8, w_ada, b_ada):
    depth = w_ada.shape[0]
    n_out = w_ada.shape[2]
    tn = ADA_COL_TILE
    return pl.pallas_call(
        _adaln_kernel,
        grid=(depth, n_out // tn),
        in_specs=[pl.BlockSpec((8, D_MODEL), lambda l, j: (0, 0)),
                  pl.BlockSpec((1, D_MODEL, tn), lambda l, j: (l, 0, j)),
                  pl.BlockSpec((1, 1, tn), lambda l, j: (l, 0, j))],
        out_specs=pl.BlockSpec((1, 8, tn), lambda l, j: (l, 0, j)),
        out_shape=jax.ShapeDtypeStruct((depth, 8, n_out), F32),
        compiler_params=_params("parallel", "parallel"),
        name="adaln",
    )(cond8, w_ada, b_ada.reshape(depth, 1, n_out))


def _norm_proj_kernel(x_ref, g_ref, sc_ref, sh_ref, w_ref, wg_ref, o_ref, gate_ref, *rest, t_tile, t_off):
    h_scr = rest[-1]

    @pl.when(pl.program_id(1) == 0)
    def _():
        h = _norm_mod(x_ref[...], g_ref[...], sc_ref[0], sh_ref[0]).astype(BF16)
        h_scr[...] = h
        gate_ref[...] = _dot_nt(wg_ref[...], h)

    y = _dot_nt(h_scr[...], w_ref[...])
    o_ref[...] = y.astype(o_ref.dtype)

    if t_tile is not None:
        t_ref = rest[0]

        @pl.when(pl.program_id(1) == t_tile)
        def _():
            for g in range(t_ref.shape[0]):
                t_ref[g] = y[g * GROUP:(g + 1) * GROUP, t_off:t_off + t_ref.shape[1]].T


def _norm_proj(x, norm_g, mod, tokens_per_cond, w_stack, layer, n_main, t_tile=None):
    n_tok = x.shape[0]
    tm = min(PROJ_TOKEN_TILE, tokens_per_cond)
    tn = PROJ_WIDE_COL_TILE
    assert n_main % tn == 0 and tn % PROJ_COL_TILE == 0
    t_step, t_off = (None, 0) if t_tile is None else divmod(t_tile * PROJ_COL_TILE, tn)
    cond_of_tile = lambda i: (i * tm) // tokens_per_cond
    w_t = jnp.swapaxes(w_stack, 1, 2)
    n_gate = w_t.shape[1] - n_main
    assert n_main % n_gate == 0 and n_gate % 8 == 0
    out_specs = [pl.BlockSpec((tm, tn), lambda i, j: (i, j)),
                 pl.BlockSpec((n_gate, tm), lambda i, j: (0, i))]
    out_shape = [jax.ShapeDtypeStruct((n_tok, n_main), BF16),
                 jax.ShapeDtypeStruct((n_gate, n_tok), F32)]
    if t_tile is not None:
        out_specs.append(pl.BlockSpec((tm // GROUP, PROJ_COL_TILE, GROUP), lambda i, j: (i, 0, 0)))
        out_shape.append(jax.ShapeDtypeStruct((n_tok // GROUP, PROJ_COL_TILE, GROUP), F32))
    return pl.pallas_call(
        functools.partial(_norm_proj_kernel, t_tile=t_step, t_off=t_off),
        grid=(n_tok // tm, n_main // tn),
        in_specs=[pl.BlockSpec((tm, D_MODEL), lambda i, j: (i, 0)),
                  pl.BlockSpec((1, D_MODEL), lambda i, j: (0, 0)),
                  _mod_spec(1, cond_of_tile), _mod_spec(0, cond_of_tile),
                  pl.BlockSpec((None, tn, D_MODEL), lambda i, j: (layer, j, 0)),
                  pl.BlockSpec((None, n_gate, D_MODEL), lambda i, j: (layer, n_main // n_gate, 0))],
        out_specs=out_specs, out_shape=out_shape,
        scratch_shapes=[pltpu.VMEM((tm, D_MODEL), BF16)],
        compiler_params=_params("parallel", "arbitrary"),
        name="norm_proj",
    )(x, norm_g.reshape(1, D_MODEL), mod, mod, w_t, w_t)


def _gated_out_kernel(x_ref, o_ref, z_ref, ng_ref, g1_ref, w_ref, y_ref, wb_scr, *, n_heads, gate_fn):
    @pl.when(pl.program_id(0) == 0)
    def _():
        wb_scr[...] = w_ref[...].astype(BF16)

    dv = o_ref.shape[1] // n_heads
    parts = []
    for h in range(n_heads):
        sl = slice(h * dv, (h + 1) * dv)
        parts.append((_rms(o_ref[:, sl], ng_ref[...]) * gate_fn(z_ref[:, sl].astype(F32))).astype(BF16))
    hs = jnp.concatenate(parts, axis=1)
    y = jnp.dot(hs, wb_scr[...], preferred_element_type=F32)
    y_ref[...] = x_ref[...] + g1_ref[0] * y


def _gated_out(x, o, main, z_block, norm_g, mod, cond_of_tile, w_out, n_heads, gate_fn):
    n_tok = x.shape[0]
    tm = OUT_TOKEN_TILE
    width = o.shape[1]
    return pl.pallas_call(
        functools.partial(_gated_out_kernel, n_heads=n_heads, gate_fn=gate_fn),
        grid=(n_tok // tm,),
        in_specs=[pl.BlockSpec((tm, D_MODEL), lambda i: (i, 0)),
                  pl.BlockSpec((tm, width), lambda i: (i, 0)),
                  pl.BlockSpec((tm, width), lambda i: (i, z_block)),
                  pl.BlockSpec((1, width // n_heads), lambda i: (0, 0)),
                  _mod_spec(2, cond_of_tile),
                  pl.BlockSpec((width, D_MODEL), lambda i: (0, 0))],
        out_specs=pl.BlockSpec((tm, D_MODEL), lambda i: (i, 0)),
        out_shape=jax.ShapeDtypeStruct((n_tok, D_MODEL), F32),
        scratch_shapes=[pltpu.VMEM((width, D_MODEL), BF16)],
        compiler_params=_params("arbitrary"),
        name="gated_out",
    )(x, o, main, norm_g.reshape(1, -1), mod, w_out)


def _dft_mats(n):
    k = np.arange(n, dtype=np.int64)
    ang = 2.0 * np.pi * ((k[:, None] * k[None, :]) % n).astype(np.float64) / n
    s = 1.0 / np.sqrt(n)
    return np.cos(ang) * s, np.sin(ang) * s


def _fnet_kernel(x_ref, ng_ref, sc_ref, sh_ref, g1_ref, ct_ref, st_ref, cs_ref, w_ref, b_ref, y_ref, wb_scr):
    @pl.when(pl.program_id(0) == 0)
    def _():
        wb_scr[...] = w_ref[...].astype(BF16)

    seq = ct_ref.shape[0]
    seqs = [pl.ds(r, seq) for r in range(0, x_ref.shape[0], seq)]
    gd = FNET_GROUP_DIM
    hs = [_norm_mod(x_ref[s, :], ng_ref[...], sc_ref[0], sh_ref[0]).astype(BF16) for s in seqs]
    ps = [jnp.dot(ct_ref[...], h, preferred_element_type=F32) for h in hs]
    qs = [jnp.dot(st_ref[...], h, preferred_element_type=F32) for h in hs]
    fs = []
    for p, q in zip(ps, qs):
        parts = []
        for g in range(D_MODEL // gd):
            sl = slice(g * gd, (g + 1) * gd)
            pq = jnp.concatenate([p[:, sl], q[:, sl]], axis=1).astype(BF16)
            parts.append(jnp.dot(pq, cs_ref[...], preferred_element_type=F32).astype(BF16))
        fs.append(jnp.concatenate(parts, axis=1))
    ys = [jnp.dot(f, wb_scr[...], preferred_element_type=F32) + b_ref[...] for f in fs]
    for s, y in zip(seqs, ys):
        y_ref[s, :] = x_ref[s, :] + g1_ref[0] * y


def _fnet_mix(x, seq, layer, norm_g, mod, tokens_per_cond, w, b):
    n_tok = x.shape[0]
    tm = seq * max(1, min(FNET_STEP_TOKENS, tokens_per_cond) // seq)
    cond_of_tile = lambda i: (i * tm) // tokens_per_cond
    ct, st = _dft_mats(seq)
    cc, sc = _dft_mats(FNET_GROUP_DIM)
    cs = np.concatenate([cc, -sc], axis=0)
    const = lambda i: (0, 0)
    ct, st, cs = (jnp.asarray(m, F32).astype(BF16) for m in (ct, st, cs))
    return pl.pallas_call(
        _fnet_kernel,
        grid=(n_tok // tm,),
        in_specs=[pl.BlockSpec((tm, D_MODEL), lambda i: (i, 0)),
                  pl.BlockSpec((1, D_MODEL), const),
                  _mod_spec(1, cond_of_tile), _mod_spec(0, cond_of_tile), _mod_spec(2, cond_of_tile),
                  pl.BlockSpec((seq, seq), const),
                  pl.BlockSpec((seq, seq), const),
                  pl.BlockSpec((2 * FNET_GROUP_DIM, FNET_GROUP_DIM), const),
                  pl.BlockSpec((None, D_MODEL, D_MODEL), lambda i: (layer, 0, 0)),
                  pl.BlockSpec((1, D_MODEL), const)],
        out_specs=pl.BlockSpec((tm, D_MODEL), lambda i: (i, 0)),
        out_shape=jax.ShapeDtypeStruct((n_tok, D_MODEL), F32),
        scratch_shapes=[pltpu.VMEM((D_MODEL, D_MODEL), BF16)],
        compiler_params=_params("arbitrary"),
        name="fnet",
    )(x, norm_g.reshape(1, D_MODEL), mod, mod, mod,
      ct, st, cs, w, b.reshape(1, D_MODEL))


SUBLANES = 8


def _shift_tokens(x, delta, period):
    n, c = x.shape
    assert n % period == 0 and period % SUBLANES == 0 and 0 < abs(delta) < period
    blocks = range(0, n, period)
    if delta % SUBLANES == 0:
        zeros = jnp.zeros((abs(delta), c), x.dtype)
        if delta > 0:
            pieces = [piece for b in blocks for piece in (x[b + delta:b + period], zeros)]
        else:
            pieces = [piece for b in blocks for piece in (zeros, x[b:b + period + delta])]
        return jnp.concatenate(pieces, axis=0)
    assert abs(delta) < SUBLANES
    y = pltpu.roll(x, (-delta) % n, 0)
    r = lax.broadcasted_iota(jnp.int32, (SUBLANES, 1), 0) + delta
    keep = ((r >= 0) & (r <= SUBLANES - 1)).astype(x.dtype)
    if delta < 0:
        pieces = [piece for b in blocks for piece in (y[b:b + SUBLANES] * keep, y[b + SUBLANES:b + period])]
    else:
        pieces = [piece for b in blocks
                  for piece in (y[b:b + period - SUBLANES], y[b + period - SUBLANES:b + period] * keep)]
    return jnp.concatenate(pieces, axis=0)


def _dwconv_tokens(g, cw_ref, seq, rows):
    width = seq // rows
    g_cols = (_shift_tokens(g, -1, width), g, _shift_tokens(g, 1, width))
    out = None
    for di in (-1, 0, 1):
        if rows == 1 and di != 0:
            continue
        r = sum(g_cols[dj] * cw_ref[pl.ds(3 * (di + 1) + dj, 1), :] for dj in range(3))
        if di != 0:
            r = _shift_tokens(r, di * width, seq)
        out = r if out is None else out + r
    return out


def _ffn_kernel(x_ref, ng_ref, sc_ref, sh_ref, g2_ref, *rest, seq, rows, final_norm, n_ff_tiles):
    tiles = [rest[5 * t:5 * t + 5] for t in range(FFN_TILES_PER_STEP)]
    nf_ref, y_ref, h_scr, acc_scr = rest[5 * FFN_TILES_PER_STEP:]
    j = pl.program_id(1)
    last = pl.num_programs(1) - 1

    @pl.when(j == 0)
    def _():
        h_scr[...] = _norm_mod(x_ref[...], ng_ref[...], sc_ref[0], sh_ref[0]).astype(BF16)
        acc_scr[...] = jnp.zeros_like(acc_scr)

    tm = h_scr.shape[0]
    sub = tm // FFN_SUB_TILES
    parts = [pl.ds(r, sub) for r in range(0, tm, sub)]
    per_seq = max(seq // sub, 1)

    def run(tiles):
        gs = [[jnp.dot(h_scr[p, :], wg[...].astype(BF16), preferred_element_type=F32) for p in parts]
              for _, wg, _, _, _ in tiles]
        as_ = [[jnp.dot(h_scr[p, :], wa[...].astype(BF16), preferred_element_type=F32) for p in parts]
               for wa, _, _, _, _ in tiles]
        acts = []
        for (_, _, cw, cb, _), g_t, a_t in zip(tiles, gs, as_):
            convs = []
            for i in range(0, len(parts), per_seq):
                g = g_t[i] if per_seq == 1 else jnp.concatenate(g_t[i:i + per_seq], axis=0)
                c = _dwconv_tokens(g, cw, seq, rows) + cb[...]
                convs += [c[k * sub:(k + 1) * sub] for k in range(per_seq)]
            acts.append([(jax.nn.silu(c) * a).astype(BF16) for c, a in zip(convs, a_t)])
        wd = jnp.concatenate([t[4][...].astype(BF16) for t in tiles], axis=0)
        for k, p in enumerate(parts):
            act = jnp.concatenate([a_t[k] for a_t in acts], axis=1)
            acc_scr[p, :] += jnp.dot(act, wd, preferred_element_type=F32)

    n_tail = n_ff_tiles % FFN_TILES_PER_STEP
    if n_tail:
        pl.when(j < last)(lambda: run(tiles))
        pl.when(j == last)(lambda: run(tiles[:n_tail]))
    else:
        run(tiles)

    @pl.when(j == last)
    def _():
        y = x_ref[...] + g2_ref[0] * acc_scr[...]
        if final_norm:
            y = _rms(y, nf_ref[...])
        y_ref[...] = y


def _conv_ffn(x, seq, rows, layer, norm_g, mod, cond_of_tile, w_up, conv_w, conv_b, w_down, norm_final, final_norm):
    n_tok = x.shape[0]
    tm, tf = FFN_TOKEN_TILE, FFN_FF_TILE
    n_ff_tiles = D_FF // tf
    depth = w_up.shape[0]
    tps = FFN_TILES_PER_STEP
    conv_w, conv_b = conv_w.reshape(depth, 9, D_FF), conv_b.reshape(depth, 1, D_FF)
    tile_specs, tile_args = [], []
    for t in range(tps):
        tile = lambda j, t=t: jnp.minimum(j * tps + t, n_ff_tiles - 1)
        tile_specs += [pl.BlockSpec((None, D_MODEL, tf), lambda i, j, tile=tile: (layer, 0, tile(j))),
                       pl.BlockSpec((None, D_MODEL, tf), lambda i, j, tile=tile: (layer, 0, n_ff_tiles + tile(j))),
                       pl.BlockSpec((None, 9, tf), lambda i, j, tile=tile: (layer, 0, tile(j))),
                       pl.BlockSpec((None, 1, tf), lambda i, j, tile=tile: (layer, 0, tile(j))),
                       pl.BlockSpec((None, tf, D_MODEL), lambda i, j, tile=tile: (layer, tile(j), 0))]
        tile_args += [w_up, w_up, conv_w, conv_b, w_down]
    return pl.pallas_call(
        functools.partial(_ffn_kernel, seq=seq, rows=rows, final_norm=final_norm, n_ff_tiles=n_ff_tiles),
        grid=(n_tok // tm, pl.cdiv(n_ff_tiles, tps)),
        in_specs=[pl.BlockSpec((tm, D_MODEL), lambda i, j: (i, 0)),
                  pl.BlockSpec((1, D_MODEL), lambda i, j: (0, 0)),
                  _mod_spec(4, cond_of_tile), _mod_spec(3, cond_of_tile), _mod_spec(5, cond_of_tile),
                  *tile_specs,
                  pl.BlockSpec((1, D_MODEL), lambda i, j: (0, 0))],
        out_specs=pl.BlockSpec((tm, D_MODEL), lambda i, j: (i, 0)),
        out_shape=jax.ShapeDtypeStruct((n_tok, D_MODEL), F32),
        scratch_shapes=[pltpu.VMEM((tm, D_MODEL), BF16), pltpu.VMEM((tm, D_MODEL), F32)],
        compiler_params=_params("parallel", "arbitrary"),
        name="conv_ffn",
    )(x, norm_g.reshape(1, D_MODEL), mod, mod, mod, *tile_args, norm_final.reshape(1, D_MODEL))


GroupMasks = collections.namedtuple("GroupMasks", "r c same incl strict")


def _chunk_masks():
    r = lax.broadcasted_iota(jnp.int32, (CHUNK, CHUNK), 0)
    c = lax.broadcasted_iota(jnp.int32, (CHUNK, CHUNK), 1)
    return (r >= c, r <= c)


def _group_masks(group=GROUP):
    r = lax.broadcasted_iota(jnp.int32, (group, group), 0)
    c = lax.broadcasted_iota(jnp.int32, (group, group), 1)
    same = (r >> LOG_CHUNK) == (c >> LOG_CHUNK)
    return GroupMasks(r, c, same, (same & (r >= c), same & (r <= c)), (same & (r > c), same & (r < c)))


def _gate_layouts(gt, batch, n_heads, hp, group=GROUP):
    seq = gt.shape[1] // batch
    n_groups, n_chunks = seq // group, seq // CHUNK
    assert seq % group == 0 and n_groups <= 8
    g = jnp.transpose(gt.reshape(N_DIR, 2, n_heads, batch, seq), (3, 2, 0, 1, 4))
    rows = g.reshape(g.shape[:4] + (n_groups, group))
    rows = jnp.pad(rows, ((0, 0),) * 4 + ((0, 8 - n_groups), (0, 0)))
    cols = jnp.swapaxes(g.reshape(batch, n_heads // hp, hp * N_DIR * 2 * n_chunks, CHUNK), -1, -2)
    return rows, cols


def _col_gate(gcol_ref, hh, d, kind, n_chunks):
    start = ((hh * N_DIR + d) * 2 + kind) * n_chunks
    return gcol_ref[0, 0, :, start:start + n_chunks]


def _split3(x):
    x1 = x.astype(BF16)
    r1 = x - x1.astype(F32)
    x2 = r1.astype(BF16)
    return x1, x2, (r1 - x2.astype(F32)).astype(BF16)


def _dot_mask_rhs(x, mask):
    m = x.shape[0]
    y = jnp.dot(jnp.concatenate(_split3(x), axis=0), jnp.where(mask, 1.0, 0.0).astype(BF16),
                preferred_element_type=F32)
    return y[:m] + y[m:2 * m] + y[2 * m:]


def _dot_mask_lhs(mask, x):
    mb = jnp.where(mask, 1.0, 0.0).astype(BF16)
    y1, y2, y3 = (jnp.dot(mb, p, preferred_element_type=F32) for p in _split3(x))
    return y1 + y2 + y3


def _cumsum_rows(row, d, gm):
    return _dot_mask_rhs(row, gm.incl[1 - d])


def _cumsum_cols(col, d, cm):
    return _dot_mask_lhs(cm[d], col)


def _store_cols(dst, col, n_chunks):
    for n in range(n_chunks):
        dst[n * CHUNK:(n + 1) * CHUNK, :] = jnp.broadcast_to(col[:, n:n + 1], (CHUNK, 128))


def _wide(x, width=None):
    reps = (GROUP if width is None else width) // x.shape[1]
    return x if reps == 1 else jnp.concatenate([x] * reps, axis=1)


def _rows(i, size):
    if isinstance(i, int):
        return pl.ds(i * size, size)
    return pl.ds(pl.multiple_of(i * size, size), size)


def _loop(n, body, init):
    if n == 1:
        return body(0, init)
    return lax.fori_loop(0, n, body, init)


def _conv_silu(x, w_ref):
    n = x.shape[0]
    k = w_ref.shape[0]
    acc = None
    for j in range(k):
        delta = j - k // 2
        term = (x if delta == 0 else _shift_tokens(x, delta, n)) * w_ref[pl.ds(j, 1), :]
        acc = term if acc is None else acc + term
    return jax.nn.silu(acc)


def _l2norm(x):
    return x * lax.rsqrt(jnp.sum(x * x, axis=-1, keepdims=True) + EPS)


def _blockdiag_tri_inverse(mats, gm):
    dot16 = lambda x, y: jnp.dot(x, y, preferred_element_type=F32).astype(BF16)
    base = (gm.r >> 3) == (gm.c >> 3)
    eye = jnp.where(gm.r == gm.c, 1.0, 0.0)
    zero16 = jnp.zeros((), BF16)
    a16s = [a.astype(BF16) for a in mats]
    invs = [eye - jnp.where(base, a, 0.0) for a in mats]
    ps = [jnp.where(base, -a16, zero16) for a16 in a16s]
    for _ in range(2):
        ps = [dot16(p, p) for p in ps]
        invs = [inv + jnp.dot(inv.astype(BF16), p, preferred_element_type=F32) for inv, p in zip(invs, ps)]
    for s in (3, 4, 5):
        join = ((gm.r >> s) ^ (gm.c >> s)) == 1
        inv16s = [inv.astype(BF16) for inv in invs]
        tmp = [dot16(jnp.where(join, a16, zero16), inv16) for a16, inv16 in zip(a16s, inv16s)]
        invs = [inv - jnp.dot(inv16, t, preferred_element_type=F32) for inv, inv16, t in zip(invs, inv16s, tmp)]
    return invs


def _gdn_kernel(*refs, seq, hp, zero_init, emit_state):
    alog_ref, dtb_ref, q_ref, k_ref, v_ref, cwq_ref, cwk_ref, cwv_ref, grow_ref, gcol_ref = refs[:10]
    pos = 10
    s0_ref = None
    if not zero_init:
        s0_ref = refs[pos]
        pos += 1
    o_ref = refs[pos]
    pos += 1
    sfin_ref = None
    if emit_state:
        sfin_ref = refs[pos]
        pos += 1
    (q_scr, k_scr, v_scr, gcr_scr, gcb_scr, bb_scr, gl_scr,
     u_scr, w_scr, qd_scr, kd_scr, qkd_scr) = refs[pos:]

    n_chunks = seq // CHUNK
    grp = GDN_GROUP
    cm = _chunk_masks()
    gm = _group_masks(grp)
    head_dirs = [(hh, d) for hh in range(hp) for d in range(N_DIR)]
    lanes = lambda hh: slice(hh * DN_DK, (hh + 1) * DN_DK)

    q_scr[...] = _conv_silu(q_ref[...].astype(F32), cwq_ref)
    k_scr[...] = _conv_silu(k_ref[...].astype(F32), cwk_ref)
    v_scr[...] = _conv_silu(v_ref[...].astype(F32), cwv_ref)
    for hh in range(hp):
        q_scr[:, lanes(hh)] = _l2norm(q_scr[:, lanes(hh)]) * (DN_DK ** -0.5)
        k_scr[:, lanes(hh)] = _l2norm(k_scr[:, lanes(hh)])

    for hh, d in head_dirs:
        head = pl.program_id(1) * hp + hh
        neg_a = -jnp.exp(jnp.full((1, 1), alog_ref[d, head], F32))
        dtb = dtb_ref[d, head]
        gcr_scr[hh, d] = _cumsum_rows(neg_a * _softplus(grow_ref[0, hh, d, 0] + dtb), d, gm)
        gc_col = _cumsum_cols(neg_a * _softplus(_col_gate(gcol_ref, hh, d, 0, n_chunks) + dtb), d, cm)
        _store_cols(gcb_scr.at[hh, d], gc_col, n_chunks)
        _store_cols(bb_scr.at[hh, d], jax.nn.sigmoid(_col_gate(gcol_ref, hh, d, 1, n_chunks)), n_chunks)
        last = CHUNK - 1 if d == 0 else 0
        _store_cols(gl_scr.at[hh, d], jnp.broadcast_to(gc_col[last:last + 1, :], gc_col.shape), n_chunks)

    gps = min(GDN_GROUPS_PER_STEP, seq // grp)

    def group_step(it, carry):
        gis = [it * gps + k for k in range(gps)]
        rows = [_rows(gi, grp) for gi in gis]
        for h0 in range(0, hp, GDN_HEADS_PER_PASS):
            heads = range(h0, min(h0 + GDN_HEADS_PER_PASS, hp))
            kqs = {}
            for k, r in enumerate(rows):
                for hh in heads:
                    kg, qg = k_scr[r, lanes(hh)], q_scr[r, lanes(hh)]
                    kqs[k, hh] = _dot_nt(jnp.concatenate([kg, qg], axis=0), kg)
            chains = [(k, hh, d) for k in range(len(gis)) for hh in heads for d in range(N_DIR)]
            mats = []
            for k, hh, d in chains:
                gcb = gcb_scr[hh, d, rows[k], :]
                gcr = gcr_scr[hh, d, pl.ds(gis[k], 1), :]
                decay = jnp.exp(jnp.where(gm.incl[d], _wide(gcb, grp) - gcr, -jnp.inf))
                mats.append(jnp.where(gm.strict[d],
                                      kqs[k, hh][:grp] * _wide(bb_scr[hh, d, rows[k], :], grp) * decay, 0.0))
                qkd = kqs[k, hh][grp:] * decay
                qkd_scr[hh, d, rows[k], :] = sum(qkd[:, n * CHUNK:(n + 1) * CHUNK] for n in range(grp // CHUNK))
            t_invs = _blockdiag_tri_inverse(mats, gm)
            for (k, hh, d), t_inv in zip(chains, t_invs):
                r = rows[k]
                qg, kg, vg = q_scr[r, lanes(hh)], k_scr[r, lanes(hh)], v_scr[r, lanes(hh)]
                gcb = gcb_scr[hh, d, r, :]
                bb = bb_scr[hh, d, r, :]
                e_gc = jnp.exp(gcb)
                uw = _dot(t_inv, jnp.concatenate([vg * bb, kg * bb * e_gc], axis=1))
                u_scr[hh, d, r, :] = uw[:, :DN_DV]
                w_scr[hh, d, r, :] = uw[:, DN_DV:]
                qd_scr[hh, d, r, :] = qg * e_gc
                kd_scr[hh, d, r, :] = kg * jnp.exp(gl_scr[hh, d, r, :] - gcb)
        return carry

    _loop(seq // (grp * gps), group_step, 0)

    o_ref[...] = jnp.zeros_like(o_ref)

    def chunk_step(n, carry):
        idxs = [n if d == 0 else n_chunks - 1 - n for _, d in head_dirs]
        rows = [_rows(idx, CHUNK) for idx in idxs]
        wqs = [_dot(jnp.concatenate([w_scr[hh, d, r, :], qd_scr[hh, d, r, :]], axis=0), s)
               for (hh, d), r, s in zip(head_dirs, rows, carry)]
        v_news = [u_scr[hh, d, r, :] - wq[:CHUNK] for (hh, d), r, wq in zip(head_dirs, rows, wqs)]
        o_ns = [wq[CHUNK:] + _dot(qkd_scr[hh, d, r, :], v_new)
                for (hh, d), r, wq, v_new in zip(head_dirs, rows, wqs, v_news)]
        new = []
        for (hh, d), idx, r, s, v_new in zip(head_dirs, idxs, rows, carry, v_news):
            s_decay = jnp.exp(gl_scr[hh, d, pl.ds(idx * CHUNK, 1), :])
            new.append(s * s_decay + _dot_tn(kd_scr[hh, d, r, :], v_new))
        for (hh, d), r, o_n in zip(head_dirs, rows, o_ns):
            o_ref[r, lanes(hh)] += o_n
        return tuple(new)

    if zero_init:
        init = (jnp.zeros((DN_DK, DN_DV), F32),) * len(head_dirs)
    else:
        init = tuple(s0_ref[0, d, hh] for hh, d in head_dirs)
    fin = lax.fori_loop(0, n_chunks, chunk_step, init)
    if emit_state:
        for i, (hh, d) in enumerate(head_dirs):
            sfin_ref[0, d, hh] = fin[i]


def _gdn_scan(main, gt, batch, seq, conv_w, a_log, dt_bias, s0, emit_state):
    n_tok = batch * seq
    nh = DN_HEADS
    hp = min(nh, max(2, GDN_STEP_POSITIONS // seq))
    n_hb = nh // hp
    assert seq % (GDN_GROUP * min(GDN_GROUPS_PER_STEP, seq // GDN_GROUP)) == 0
    grow, gcol = _gate_layouts(gt, batch, nh, hp, GDN_GROUP)
    zero_init = s0 is None
    smem = pl.BlockSpec(memory_space=pltpu.SMEM)
    qkv_spec = lambda off: pl.BlockSpec((seq, hp * DN_DK), lambda b, h: (b, off + h))
    cw_spec = lambda off: pl.BlockSpec((DN_CONV, hp * DN_DK), lambda b, h: (0, off + h))
    state_spec = pl.BlockSpec((1, N_DIR, hp, DN_DK, DN_DV), lambda b, h: (b, 0, h, 0, 0))
    in_specs = [smem, smem, qkv_spec(0), qkv_spec(n_hb), qkv_spec(2 * n_hb),
                cw_spec(0), cw_spec(n_hb), cw_spec(2 * n_hb),
                pl.BlockSpec((1, hp) + grow.shape[2:], lambda b, h: (b, h, 0, 0, 0, 0)),
                pl.BlockSpec((1, 1) + gcol.shape[2:], lambda b, h: (b, h, 0, 0))]
    args = [a_log, dt_bias, main, main, main, conv_w, conv_w, conv_w, grow, gcol]
    if not zero_init:
        in_specs.append(state_spec)
        args.append(s0)
    out_specs = [pl.BlockSpec((seq, hp * DN_DV), lambda b, h: (b, h))]
    out_shape = [jax.ShapeDtypeStruct((n_tok, nh * DN_DV), F32)]
    if emit_state:
        out_specs.append(state_spec)
        out_shape.append(jax.ShapeDtypeStruct((batch, N_DIR, nh, DN_DK, DN_DV), F32))
    per_dir = lambda width: pltpu.VMEM((hp, N_DIR, seq, width), F32)
    qkv_scr = pltpu.VMEM((seq, hp * DN_DK), F32)
    outs = pl.pallas_call(
        functools.partial(_gdn_kernel, seq=seq, hp=hp, zero_init=zero_init, emit_state=emit_state),
        grid=(batch, n_hb),
        in_specs=in_specs, out_specs=out_specs, out_shape=out_shape,
        scratch_shapes=[qkv_scr, qkv_scr, qkv_scr,
                        pltpu.VMEM((hp, N_DIR, 8, GDN_GROUP), F32),
                        per_dir(128), per_dir(128), per_dir(128),
                        per_dir(DN_DV), per_dir(DN_DK), per_dir(DN_DK), per_dir(DN_DK), per_dir(CHUNK)],
        compiler_params=_params("parallel", "parallel"),
        name="gdn_scan",
    )(*args)
    return outs[0], (outs[1] if emit_state else None)


ML_AUG = 2 * ML_DV


def _mlstm_kernel(*refs, seq, hp, zero_init, emit_state):
    bi_ref, bf_ref, q_ref, k_ref, v_ref, kt_ref, grow_ref, gcol_ref = refs[:8]
    pos = 8
    c0_ref = m0_ref = None
    if not zero_init:
        c0_ref, m0_ref = refs[pos:pos + 2]
        pos += 2
    o_ref = refs[pos]
    pos += 1
    cfin_ref = nfin_ref = mfin_ref = None
    if emit_state:
        cfin_ref, nfin_ref, mfin_ref = refs[pos:pos + 3]
        pos += 3
    bcr_scr, lir_scr, kwr_scr, bcb_scr, mi_scr, dec_scr, dc_scr, cin_scr = refs[pos:]

    n_chunks = seq // CHUNK
    n_groups = seq // GROUP
    head0 = pl.program_id(1) * hp
    cm = _chunk_masks()
    gm = _group_masks()
    ones_col = jnp.where(lax.broadcasted_iota(jnp.int32, (GROUP, ML_DV), 1) == 0, 1.0, 0.0)
    head_dirs = [(hh, d) for hh in range(hp) for d in range(N_DIR)]

    m_fin = {}
    row_group = lax.broadcasted_iota(jnp.int32, (8, GROUP), 0)
    row_chunk = lax.broadcasted_iota(jnp.int32, (8, GROUP), 1) >> LOG_CHUNK
    for hh, d in head_dirs:
        b_i = bi_ref[d, head0 + hh]
        b_f = bf_ref[d, head0 + hh]
        li_row = grow_ref[0, hh, d, 0] + b_i
        lf_row = -_softplus(-(grow_ref[0, hh, d, 1] + b_f))
        bc_row = _cumsum_rows(lf_row, d, gm)
        lir_scr[hh, d] = li_row
        bcr_scr[hh, d] = bc_row
        w_row = _dot_mask_rhs(lf_row, gm.same) - bc_row + li_row
        m_out_row = jnp.zeros((8, GROUP), F32)
        li_col = _col_gate(gcol_ref, hh, d, 0, n_chunks) + b_i
        bc_col = _cumsum_cols(-_softplus(-(_col_gate(gcol_ref, hh, d, 1, n_chunks) + b_f)), d, cm)
        last = CHUNK - 1 if d == 0 else 0
        b_last = bc_col[last:last + 1, :]
        w_col = b_last - bc_col + li_col
        w_max = jnp.max(w_col, axis=0, keepdims=True)
        m = jnp.zeros((1, 1), F32) if zero_init else m0_ref[0, d, hh][:, 0:1]
        for step in range(n_chunks):
            n = step if d == 0 else n_chunks - 1 - step
            sl = slice(n * CHUNK, (n + 1) * CHUNK)
            m_new = jnp.maximum(b_last[:, n:n + 1] + m, w_max[:, n:n + 1])
            bcb_scr[hh, d, sl, :] = jnp.broadcast_to(bc_col[:, n:n + 1], (CHUNK, 128))
            mi_scr[hh, d, sl, :] = jnp.broadcast_to(m, (CHUNK, 128))
            dec_scr[hh, d, n:n + 1, :] = jnp.broadcast_to(jnp.exp(b_last[:, n:n + 1] + m - m_new), (1, 128))
            in_chunk = (row_group == n // CHUNKS_PER_GROUP) & (row_chunk == n % CHUNKS_PER_GROUP)
            m_out_row = jnp.where(in_chunk, m_new, m_out_row)
            m = m_new
        m_fin[hh, d] = m
        kwr_scr[hh, d] = jnp.exp(w_row - m_out_row)

    def load_kv(rows, hh):
        kg = k_ref[rows, hh * ML_DQK:(hh + 1) * ML_DQK].astype(F32)
        v_aug = jnp.concatenate([v_ref[rows, hh * ML_DV:(hh + 1) * ML_DV].astype(F32), ones_col], axis=1)
        return kg, v_aug

    def delta_step(gi, carry):
        rows = _rows(gi, GROUP)
        v_augs = [load_kv(rows, hh)[1] for hh in range(hp)]
        kt = kt_ref[gi]
        lhs = []
        for hh, d in head_dirs:
            kw_t = kt[hh * ML_DQK:(hh + 1) * ML_DQK, :] * kwr_scr[hh, d, pl.ds(gi, 1), :]
            lhs.append(jnp.where(gm.same, jnp.concatenate([kw_t] * CHUNKS_PER_GROUP, axis=0), 0.0))
        for (hh, d), kw in zip(head_dirs, lhs):
            dc_scr[hh, d, rows, :] = _dot(kw, v_augs[hh])
        return carry

    _loop(n_groups, delta_step, 0)

    def prefix_step(n, carry):
        new = []
        for (hh, d), c_aug in zip(head_dirs, carry):
            idx = n if d == 0 else n_chunks - 1 - n
            rows = _rows(idx, CHUNK)
            cin_scr[hh, d, rows, :] = c_aug
            new.append(c_aug * _wide(dec_scr[hh, d, pl.ds(idx, 1), :], ML_AUG) + dc_scr[hh, d, rows, :])
        return tuple(new)

    if zero_init:
        init = (jnp.zeros((ML_DQK, ML_AUG), F32),) * len(head_dirs)
    else:
        init = tuple(c0_ref[0, d, hh] for hh, d in head_dirs)
    fin = lax.fori_loop(0, n_chunks, prefix_step, init)

    def out_step(gi, carry):
        rows = _rows(gi, GROUP)
        kvs = [load_kv(rows, hh) for hh in range(hp)]
        qgs = [q_ref[rows, hh * ML_DQK:(hh + 1) * ML_DQK].astype(F32) * (ML_DQK ** -0.5) for hh in range(hp)]
        qks = [_dot_nt(qg, kv[0]) for qg, kv in zip(qgs, kvs)]
        chunk = lambda n: slice(n * CHUNK, (n + 1) * CHUNK)
        inters = []
        for hh, d in head_dirs:
            c_in = cin_scr[hh, d, rows, :]
            inters.append(jnp.concatenate(
                [_dot(qgs[hh][chunk(n)], c_in[chunk(n)]) for n in range(CHUNKS_PER_GROUP)], axis=0))
        m_ts, b_ms, ss = [], [], []
        for hh, d in head_dirs:
            b_colb = bcb_scr[hh, d, rows, :]
            b_m = b_colb[:, 0:1] + mi_scr[hh, d, rows, :][:, 0:1]
            d_log = jnp.where(gm.incl[d], _wide(b_colb) - bcr_scr[hh, d, pl.ds(gi, 1), :]
                              + lir_scr[hh, d, pl.ds(gi, 1), :], -jnp.inf)
            m_t = jnp.maximum(b_m, jnp.max(d_log, axis=-1, keepdims=True))
            m_ts.append(m_t)
            b_ms.append(b_m)
            ss.append(qks[hh] * jnp.exp(d_log - m_t))
        intras = [_dot(s, kvs[hh][1]) for (hh, d), s in zip(head_dirs, ss)]
        hs = []
        for inter, intra, m_t, b_m in zip(inters, intras, m_ts, b_ms):
            num = jnp.exp(b_m - m_t) * inter + intra
            den = num[:, ML_DV:ML_DV + 1]
            hs.append(num[:, :ML_DV] / jnp.maximum(jnp.abs(den), jnp.exp(-m_t)))
        for hh in range(hp):
            o_ref[rows, hh * ML_DV:(hh + 1) * ML_DV] = hs[N_DIR * hh] + hs[N_DIR * hh + 1]
        return carry

    _loop(n_groups, out_step, 0)

    if emit_state:
        for i, (hh, d) in enumerate(head_dirs):
            cfin_ref[0, d, hh] = fin[i][:, :ML_DV]
            nfin_ref[0, d, hh] = fin[i][:, ML_DV:]
            mfin_ref[0, d, hh] = jnp.broadcast_to(m_fin[hh, d], (1, 128))


def _mlstm_scan(main, kt, gt, batch, seq, b_i, b_f, state0, emit_state):
    n_tok = batch * seq
    n_groups = seq // GROUP
    n_chunks = seq // CHUNK
    nh = ML_HEADS
    hp = min(nh, max(2, SCAN_STEP_POSITIONS // seq))
    n_hb = nh // hp
    grow, gcol = _gate_layouts(gt, batch, nh, hp)
    zero_init = state0 is None
    smem = pl.BlockSpec(memory_space=pltpu.SMEM)
    k_off = (nh * ML_DQK) // (hp * ML_DQK)
    v_off = (2 * nh * ML_DQK) // (hp * ML_DV)
    c_spec = pl.BlockSpec((1, N_DIR, hp, ML_DQK, ML_AUG), lambda b, p: (b, 0, p, 0, 0))
    m_spec = pl.BlockSpec((1, N_DIR, hp, 1, 128), lambda b, p: (b, 0, p, 0, 0))
    in_specs = [smem, smem,
                pl.BlockSpec((seq, hp * ML_DQK), lambda b, p: (b, p)),
                pl.BlockSpec((seq, hp * ML_DQK), lambda b, p: (b, k_off + p)),
                pl.BlockSpec((seq, hp * ML_DV), lambda b, p: (b, v_off + p)),
                pl.BlockSpec((n_groups, hp * ML_DQK, GROUP), lambda b, p: (b, p, 0)),
                pl.BlockSpec((1, hp) + grow.shape[2:], lambda b, p: (b, p, 0, 0, 0, 0)),
                pl.BlockSpec((1, 1) + gcol.shape[2:], lambda b, p: (b, p, 0, 0))]
    args = [b_i, b_f, main, main, main, kt, grow, gcol]
    if not zero_init:
        in_specs += [c_spec, m_spec]
        args += list(state0)
    out_specs = [pl.BlockSpec((seq, hp * ML_DV), lambda b, p: (b, p))]
    out_shape = [jax.ShapeDtypeStruct((n_tok, nh * ML_DV), F32)]
    if emit_state:
        half_spec = pl.BlockSpec((1, N_DIR, hp, ML_DQK, ML_DV), lambda b, p: (b, 0, p, 0, 0))
        out_specs += [half_spec, half_spec, m_spec]
        out_shape += [jax.ShapeDtypeStruct((batch, N_DIR, nh, ML_DQK, ML_DV), F32),
                      jax.ShapeDtypeStruct((batch, N_DIR, nh, ML_DQK, ML_AUG - ML_DV), F32),
                      jax.ShapeDtypeStruct((batch, N_DIR, nh, 1, 128), F32)]
    per_hd = lambda rows, width: pltpu.VMEM((hp, N_DIR, rows, width), F32)
    outs = pl.pallas_call(
        functools.partial(_mlstm_kernel, seq=seq, hp=hp, zero_init=zero_init, emit_state=emit_state),
        grid=(batch, n_hb),
        in_specs=in_specs, out_specs=out_specs, out_shape=out_shape,
        scratch_shapes=[per_hd(8, GROUP), per_hd(8, GROUP), per_hd(8, GROUP),
                        per_hd(seq, 128), per_hd(seq, 128),
                        per_hd(max(n_chunks, 8), 128),
                        per_hd(seq, ML_AUG), per_hd(seq, ML_AUG)],
        compiler_params=_params("parallel", "parallel"),
        name="mlstm_scan",
    )(*args)
    return outs[0], (tuple(outs[1:4]) if emit_state else None)


def _trunk(x, batch, seq, rows, mods, tokens_per_cond, st_d, st_ml, emit_state, p):
    depth = p["w_ada"].shape[0]
    tile_cond = lambda tile: (lambda i: (i * tile) // tokens_per_cond)
    new_d = new_ml = None
    for layer in range(depth):
        mod = mods[layer]
        kind, j = layer % 3, layer // 3
        if kind == 0:
            x = _fnet_mix(x, seq, j, p["norm_mix"][layer], mod, tokens_per_cond, p["fnet_w"], p["fnet_b"][j])
        elif kind == 1:
            n_main = DN_HEADS * (2 * DN_DK + 2 * DN_DV)
            main, gt = _norm_proj(x, p["norm_mix"][layer], mod, tokens_per_cond, p["dn_w_in"], j, n_main)
            s0 = None if st_d is None else st_d[:, j]
            o, sfin = _gdn_scan(main, gt, batch, seq, p["dn_conv_w"][j], p["dn_a_log"][j], p["dn_dt_bias"][j],
                                s0, emit_state)
            if emit_state:
                new_d = sfin
            x = _gated_out(x, o, main, 3, p["dn_norm"][j], mod, tile_cond(OUT_TOKEN_TILE), p["dn_w_out"][j],
                           DN_HEADS, jax.nn.silu)
        else:
            n_main = 2 * ML_HEADS * ML_DQK + 2 * ML_HEADS * ML_DV
            k_tile = (ML_HEADS * ML_DQK) // PROJ_COL_TILE
            assert ML_HEADS * ML_DQK == PROJ_COL_TILE
            main, gt, kt = _norm_proj(x, p["norm_mix"][layer], mod, tokens_per_cond, p["ml_w_in"], j, n_main,
                                      t_tile=k_tile)
            state0 = None
            if st_ml is not None:
                c0, n0, m0 = (s[:, j] for s in st_ml)
                pad = jnp.zeros(c0.shape[:-1] + (ML_AUG - ML_DV - 1,), F32)
                c_aug0 = jnp.concatenate([c0, n0[..., None], pad], axis=-1)
                m0b = jnp.broadcast_to(m0[..., None, None], m0.shape + (1, 128))
                state0 = (c_aug0, m0b)
            o, fin = _mlstm_scan(main, kt, gt, batch, seq, p["ml_b_i"][j], p["ml_b_f"][j], state0, emit_state)
            if emit_state:
                new_ml = (fin[0], fin[1][..., 0], fin[2][..., 0, 0])
            x = _gated_out(x, o, main, 2, p["ml_norm"][j], mod, tile_cond(OUT_TOKEN_TILE), p["ml_w_out"][j],
                           ML_HEADS, jax.nn.sigmoid)
        x = _conv_ffn(x, seq, rows, layer, p["norm_ffn"][layer], mod, tile_cond(FFN_TOKEN_TILE), p["ffn_w_up"],
                      p["ffn_conv_w"], p["ffn_conv_b"], p["ffn_w_down"], p["norm_final"],
                      final_norm=(layer == depth - 1))
    return x, new_d, new_ml


def kernel(x_prompt, x_sample, state_delta, state_mlstm_c, state_mlstm_n, state_mlstm_m, c, c_ctx, w_ada, b_ada, norm_mix, norm_ffn, norm_final, ffn_w_up, ffn_conv_w, ffn_conv_b, ffn_w_down, fnet_w, fnet_b, dn_w_in, dn_conv_w, dn_a_log, dn_dt_bias, dn_norm, dn_w_out, ml_w_in, ml_b_i, ml_b_f, ml_norm, ml_w_out):
    p = dict(w_ada=w_ada, norm_mix=norm_mix, norm_ffn=norm_ffn, norm_final=norm_final, ffn_w_up=ffn_w_up,
             ffn_conv_w=ffn_conv_w, ffn_conv_b=ffn_conv_b, ffn_w_down=ffn_w_down, fnet_w=fnet_w, fnet_b=fnet_b,
             dn_w_in=dn_w_in, dn_conv_w=dn_conv_w, dn_a_log=dn_a_log, dn_dt_bias=dn_dt_bias, dn_norm=dn_norm,
             dn_w_out=dn_w_out, ml_w_in=ml_w_in, ml_b_i=ml_b_i, ml_b_f=ml_b_f, ml_norm=ml_norm, ml_w_out=ml_w_out)
    b_ctx, t_ctx, _ = x_prompt.shape
    b_smp, t_smp, _ = x_sample.shape
    depth = w_ada.shape[0]

    cond8 = jnp.concatenate([c_ctx[None, :], c, jnp.zeros((8 - 1 - b_smp, D_MODEL), F32)], axis=0)
    mods = _adaln_all(cond8, w_ada, b_ada)
    mods_ctx = [mods[l, 0:1].reshape(1, 1, -1) for l in range(depth)]
    mods_smp = [mods[l, 1:1 + b_smp].reshape(b_smp, 1, -1) for l in range(depth)]

    y_ctx, new_d, new_ml = _trunk(x_prompt.reshape(b_ctx * t_ctx, D_MODEL), b_ctx, t_ctx, 1, mods_ctx,
                                  b_ctx * t_ctx, None, None, True, p)
    y_smp, _, _ = _trunk(x_sample.reshape(b_smp * t_smp, D_MODEL), b_smp, t_smp, t_smp // GRID_W, mods_smp,
                         t_smp, state_delta, (state_mlstm_c, state_mlstm_n, state_mlstm_m), False, p)

    new_c, new_n, new_m = new_ml
    return (y_ctx.reshape(b_ctx, t_ctx, D_MODEL), y_smp.reshape(b_smp, t_smp, D_MODEL),
            new_d[:, None], new_c[:, None], new_n[:, None], new_m[:, None])
```
